```python
import jax, jax.numpy as jnp
from jax import lax
import numpy as np

D_MODEL = 1024
BATCH = 8
SEQ = 8192
DEPTH = 1

GRID_W = 64
CTX_LEN = 256
A_WIDTH = 1024
A_HEAD_DIM = 128
A_HEADS = A_WIDTH // A_HEAD_DIM
A_CHUNK = 64
B_WIDTH = 1024
B_BLOCKS = 8
B_BLOCK_DIM = B_WIDTH // B_BLOCKS
B_CONV = 4
RG_C = 8.0
N_BRANCH = 2
IN_COLS = 5 * A_WIDTH + 2 * B_WIDTH + N_BRANCH * D_MODEL
DEEPNORM_ALPHA = (2 * DEPTH) ** 0.25
DEEPNORM_BETA = (8 * DEPTH) ** -0.25
LN_EPS = 1e-5
RMS_EPS = 1e-6

kernel_name = "hgrn2_rglru_gated_hybrid_dit"


def _in_split_points():
    sizes = [A_WIDTH] * 5 + [B_WIDTH] * 2 + [D_MODEL] * N_BRANCH
    return [int(v) for v in np.cumsum(sizes)[:-1]]


def layer_norm(t, g, b):
    tf = t.astype(jnp.float32)
    mu = jnp.mean(tf, axis=-1, keepdims=True)
    var = jnp.mean(jnp.square(tf - mu), axis=-1, keepdims=True)
    return ((tf - mu) * lax.rsqrt(var + LN_EPS) * g.astype(jnp.float32) + b.astype(jnp.float32)).astype(t.dtype)


def rms_norm(t, g):
    tf = t.astype(jnp.float32)
    return tf * lax.rsqrt(jnp.mean(jnp.square(tf), axis=-1, keepdims=True) + RMS_EPS) * g.astype(jnp.float32)


def to_heads(t):
    b, l, _ = t.shape
    return t.reshape(b, l, A_HEADS, A_HEAD_DIM).transpose(0, 2, 1, 3)


def from_heads(t):
    b, h, l, d = t.shape
    return t.transpose(0, 2, 1, 3).reshape(b, l, h * d)


def grid_to_colmajor(t, rows):
    b, _, ch = t.shape
    return t.reshape(b, rows, GRID_W, ch).transpose(0, 2, 1, 3)


def colmajor_to_grid(t):
    b, w, r, ch = t.shape
    return t.transpose(0, 2, 1, 3).reshape(b, r * w, ch)


def gla_chunkwise(q, k, v, logf, s0):
    b, h, t, dk = q.shape
    dv = v.shape[-1]
    n = t // A_CHUNK
    q = q.reshape(b, h, n, A_CHUNK, dk)
    k = k.reshape(b, h, n, A_CHUNK, dk)
    v = v.reshape(b, h, n, A_CHUNK, dv)
    g = jnp.cumsum(logf.reshape(b, h, n, A_CHUNK, dk), axis=-2)
    g_last = g[..., -1:, :]
    q_dec = q * jnp.exp(g)
    k_inv = k * jnp.exp(-g)
    k_end = k * jnp.exp(g_last - g)
    mask = jnp.tril(jnp.ones((A_CHUNK, A_CHUNK), dtype=bool))
    scores = jnp.where(mask, jnp.einsum('bhnck,bhnsk->bhncs', q_dec, k_inv), 0.0)
    o_intra = jnp.einsum('bhncs,bhnsv->bhncv', scores, v)
    u = jnp.einsum('bhnsk,bhnsv->bhnkv', k_end, v)
    decay = jnp.exp(g_last[..., 0, :])

    def step(s, inp):
        d_n, u_n = inp
        return d_n[..., None] * s + u_n, s

    s_fin, s_start = lax.scan(step, s0, (jnp.moveaxis(decay, 2, 0), jnp.moveaxis(u, 2, 0)))
    s_start = jnp.moveaxis(s_start, 0, 2)
    o_inter = jnp.einsum('bhnck,bhnkv->bhncv', q_dec, s_start)
    return (o_intra + o_inter).reshape(b, h, t, dv), s_fin


def gla_prefixed(ctx_in, lat_in, reverse):
    if reverse:
        ctx_in = tuple(jnp.flip(a, axis=2) for a in ctx_in)
        lat_in = tuple(jnp.flip(a, axis=2) for a in lat_in)
    b, h, _, dk = ctx_in[0].shape
    dv = ctx_in[2].shape[-1]
    s0 = jnp.zeros((b, h, dk, dv), jnp.float32)
    o_c, s_c = gla_chunkwise(*ctx_in, s0)
    o_x, _ = gla_chunkwise(*lat_in, s_c)
    if reverse:
        o_c, o_x = jnp.flip(o_c, axis=2), jnp.flip(o_x, axis=2)
    return o_c, o_x


def hgrn2_features(z, lb):
    f32 = jnp.float32
    q = to_heads(jax.nn.silu(z[0].astype(f32)) * (A_HEAD_DIM ** -0.5))
    v = to_heads(z[3].astype(f32))
    f_fwd = lb[0] + (1.0 - lb[0]) * jax.nn.sigmoid(z[1].astype(f32))
    f_bwd = lb[1] + (1.0 - lb[1]) * jax.nn.sigmoid(z[2].astype(f32))
    fwd = (q, to_heads(1.0 - f_fwd), v, to_heads(jnp.log(f_fwd)))
    bwd = (q, to_heads(1.0 - f_bwd), v, to_heads(jnp.log(f_bwd)))
    return fwd, bwd


def centred_dwconv(t, w, bias):
    lo = (B_CONV - 1) // 2
    hi = B_CONV - 1 - lo
    n = t.shape[-2]
    tp = jnp.pad(t, [(0, 0)] * (t.ndim - 2) + [(lo, hi), (0, 0)])
    out = bias + tp[..., 0:n, :] * w[0]
    for kk in range(1, B_CONV):
        out = out + tp[..., kk:kk + n, :] * w[kk]
    return out


def rglru_gates(xc, w_r, b_r, w_i, b_i, lam):
    b, l, ch = xc.shape
    xb = xc.reshape(b, l, B_BLOCKS, B_BLOCK_DIM)
    r = jax.nn.sigmoid(jnp.einsum('blgi,gij->blgj', xb, w_r).reshape(b, l, ch) + b_r)
    i = jax.nn.sigmoid(jnp.einsum('blgi,gij->blgj', xb, w_i).reshape(b, l, ch) + b_i)
    log_a = -RG_C * r * jax.nn.softplus(-lam)
    a = jnp.exp(log_a)
    mult = jnp.sqrt(-jnp.expm1(2.0 * log_a))
    return a, mult * (i * xc)


def linear_scan(a, bterm, h0):
    bterm = bterm.at[:, 0].add(a[:, 0] * h0)

    def comb(left, right):
        al, bl = left
        ar, br = right
        return al * ar, ar * bl + br

    _, h = lax.associative_scan(comb, (a, bterm), axis=1)
    return h


def rglru_prefixed(xc_c, xc_x, params, reverse):
    if reverse:
        xc_c, xc_x = jnp.flip(xc_c, axis=1), jnp.flip(xc_x, axis=1)
    a_c, b_c = rglru_gates(xc_c, *params)
    h_c = linear_scan(a_c, b_c, jnp.zeros((xc_c.shape[0], B_WIDTH), jnp.float32))
    a_x, b_x = rglru_gates(xc_x, *params)
    h_x = linear_scan(a_x, b_x, h_c[:, -1])
    if reverse:
        h_c, h_x = jnp.flip(h_c, axis=1), jnp.flip(h_x, axis=1)
    return h_c, h_x


def hybrid_layer(x, ctx, c, c_ctx, w_mod, b_mod, w_in, b_in, lb, norm_a_g, conv_w, conv_b,
                 w_r, b_r, w_i, b_i, lam, p_a, p_b, w_out, ln_g, ln_b, last):
    f32 = jnp.float32
    bsz, t, _ = x.shape
    rows = t // GRID_W
    mod_x = jax.nn.silu(c) @ w_mod + b_mod
    mod_c = jax.nn.silu(c_ctx) @ w_mod + b_mod
    sh_x, sc_x, gt_x = jnp.split(mod_x[:, None, :], 3, axis=-1)
    sh_c, sc_c, gt_c = jnp.split(mod_c, 3, axis=-1)
    u_x = x * (1.0 + sc_x) + sh_x
    u_c = ctx * (1.0 + sc_c) + sh_c
    splits = _in_split_points()
    z_x = jnp.split(u_x @ w_in + b_in, splits, axis=-1)
    z_c = jnp.split(u_c @ w_in + b_in, splits, axis=-1)

    fx, bx = hgrn2_features(z_x, lb)
    fc, bc = hgrn2_features(z_c, lb)
    oc_f, ox_f = gla_prefixed(fc, fx, False)
    oc_b, ox_b = gla_prefixed(bc, bx, True)

    xc_x = centred_dwconv(grid_to_colmajor(z_x[5], rows), conv_w, conv_b).astype(f32).reshape(bsz, t, B_WIDTH)
    xc_c = centred_dwconv(z_c[5], conv_w, conv_b).astype(f32)
    hc_f, hx_f = rglru_prefixed(xc_c, xc_x, (w_r[0], b_r[0], w_i[0], b_i[0], lam[0]), False)
    hc_b, hx_b = rglru_prefixed(xc_c, xc_x, (w_r[1], b_r[1], w_i[1], b_i[1], lam[1]), True)
    hx = colmajor_to_grid((hx_f + hx_b).reshape(bsz, GRID_W, rows, B_WIDTH))

    def merge(z, o_a, h_b):
        o_a = from_heads(rms_norm(o_a, norm_a_g)) * jax.nn.silu(z[4].astype(f32))
        o_b = h_b * jax.nn.silu(z[6].astype(f32))
        y = jax.nn.sigmoid(z[7]) * (o_a @ p_a) + jax.nn.sigmoid(z[8]) * (o_b @ p_b)
        return y @ w_out

    x_new = layer_norm(DEEPNORM_ALPHA * x + gt_x * merge(z_x, ox_f + ox_b, hx), ln_g, ln_b)
    if last:
        return x_new, ctx
    ctx_new = layer_norm(DEEPNORM_ALPHA * ctx + gt_c * merge(z_c, oc_f + oc_b, hc_f + hc_b), ln_g, ln_b)
    return x_new, ctx_new


def _fwd_setup_inputs(seed: int = 0) -> dict:
    key = jax.random.key(seed)
    ks = jax.random.split(key, 24)
    f32 = jnp.float32
    n = lambda k, s, sc: jax.random.normal(k, s, f32) * sc
    u_a = jax.random.uniform(ks[17], (DEPTH, 2, B_WIDTH), f32, 0.9, 0.999)
    s_a = u_a ** (1.0 / RG_C)
    return {
        "x": n(ks[0], (BATCH, SEQ, D_MODEL), 1.0),
        "c": n(ks[1], (BATCH, D_MODEL), 1.0),
        "ctx": n(ks[2], (BATCH, CTX_LEN, D_MODEL), 1.0),
        "c_ctx": n(ks[3], (D_MODEL,), 1.0),
        "w_mod": n(ks[4], (DEPTH, D_MODEL, 3 * D_MODEL), 0.5 * D_MODEL ** -0.5),
        "b_mod": n(ks[5], (DEPTH, 3 * D_MODEL), 0.01),
        "w_in": n(ks[6], (DEPTH, D_MODEL, IN_COLS), D_MODEL ** -0.5),
        "b_in": n(ks[7], (DEPTH, IN_COLS), 0.01),
        "lb_logits": n(ks[8], (DEPTH + 1, 2, A_WIDTH), 0.1),
        "norm_a_g": 1.0 + n(ks[9], (DEPTH, A_HEAD_DIM), 0.01),
        "conv_w": n(ks[10], (DEPTH, B_CONV, B_WIDTH), B_CONV ** -0.5),
        "conv_b": n(ks[11], (DEPTH, B_WIDTH), 0.01),
        "w_r": n(ks[12], (DEPTH, 2, B_BLOCKS, B_BLOCK_DIM, B_BLOCK_DIM), B_BLOCK_DIM ** -0.5),
        "b_r": n(ks[13], (DEPTH, 2, B_WIDTH), 0.01),
        "w_i": n(ks[14], (DEPTH, 2, B_BLOCKS, B_BLOCK_DIM, B_BLOCK_DIM), B_BLOCK_DIM ** -0.5),
        "b_i": n(ks[15], (DEPTH, 2, B_WIDTH), 0.01),
        "lam": jnp.log(s_a) - jnp.log1p(-s_a),
        "p_a": n(ks[18], (DEPTH, A_WIDTH, D_MODEL), DEEPNORM_BETA * A_WIDTH ** -0.5),
        "p_b": n(ks[19], (DEPTH, B_WIDTH, D_MODEL), DEEPNORM_BETA * B_WIDTH ** -0.5),
        "w_out": n(ks[20], (DEPTH, D_MODEL, D_MODEL), DEEPNORM_BETA * D_MODEL ** -0.5),
        "ln_g": 1.0 + n(ks[21], (DEPTH, D_MODEL), 0.01),
        "ln_b": n(ks[22], (DEPTH, D_MODEL), 0.01),
    }


def _fwd_reference(x, c, ctx, c_ctx, w_mod, b_mod, w_in, b_in, lb_logits, norm_a_g, conv_w, conv_b,
              w_r, b_r, w_i, b_i, lam, p_a, p_b, w_out, ln_g, ln_b):
    lb_all = jnp.cumsum(jax.nn.softmax(lb_logits.astype(jnp.float32), axis=0), axis=0)
    for layer in range(DEPTH):
        x, ctx = hybrid_layer(
            x, ctx, c, c_ctx, w_mod[layer], b_mod[layer], w_in[layer], b_in[layer], lb_all[layer],
            norm_a_g[layer], conv_w[layer], conv_b[layer], w_r[layer], b_r[layer], w_i[layer], b_i[layer],
            lam[layer], p_a[layer], p_b[layer], w_out[layer], ln_g[layer], ln_b[layer],
            last=(layer == DEPTH - 1))
    return x


import jax as _jax
import jax.numpy as _jnp

TWIN_FORMAT = 'train_step'
FWD_PARAMS = ['x', 'c', 'ctx', 'c_ctx', 'w_mod', 'b_mod', 'w_in', 'b_in', 'lb_logits', 'norm_a_g', 'conv_w', 'conv_b', 'w_r', 'b_r', 'w_i', 'b_i', 'lam', 'p_a', 'p_b', 'w_out', 'ln_g', 'ln_b']
TWIN_WEIGHTS = ['c_ctx', 'w_mod', 'b_mod', 'w_in', 'b_in', 'lb_logits', 'norm_a_g', 'conv_w', 'conv_b', 'w_r', 'b_r', 'w_i', 'b_i', 'lam', 'p_a', 'p_b', 'w_out', 'ln_g', 'ln_b']
TWIN_DIFF_INPUT = 'x'
TWIN_INPUTS = ['x', 'c', 'ctx', 'c_ctx', 'w_mod', 'b_mod', 'w_in', 'b_in', 'lb_logits', 'norm_a_g', 'conv_w', 'conv_b', 'w_r', 'b_r', 'w_i', 'b_i', 'lam', 'p_a', 'p_b', 'w_out', 'ln_g', 'ln_b', 'loss_target', 'm_c_ctx', 'm_w_mod', 'm_b_mod', 'm_w_in', 'm_b_in', 'm_lb_logits', 'm_norm_a_g', 'm_conv_w', 'm_conv_b', 'm_w_r', 'm_b_r', 'm_w_i', 'm_b_i', 'm_lam', 'm_p_a', 'm_p_b', 'm_w_out', 'm_ln_g', 'm_ln_b', 'v_c_ctx', 'v_w_mod', 'v_b_mod', 'v_w_in', 'v_b_in', 'v_lb_logits', 'v_norm_a_g', 'v_conv_w', 'v_conv_b', 'v_w_r', 'v_b_r', 'v_w_i', 'v_b_i', 'v_lam', 'v_p_a', 'v_p_b', 'v_w_out', 'v_ln_g', 'v_ln_b']
TWIN_OUTPUTS = ['loss', 'grad_x', 'grad_c_ctx', 'grad_w_mod', 'grad_b_mod', 'grad_w_in', 'grad_b_in', 'grad_lb_logits', 'grad_norm_a_g', 'grad_conv_w', 'grad_conv_b', 'grad_w_r', 'grad_b_r', 'grad_w_i', 'grad_b_i', 'grad_lam', 'grad_p_a', 'grad_p_b', 'grad_w_out', 'grad_ln_g', 'grad_ln_b', 'delta_c_ctx', 'delta_w_mod', 'delta_b_mod', 'delta_w_in', 'delta_b_in', 'delta_lb_logits', 'delta_norm_a_g', 'delta_conv_w', 'delta_conv_b', 'delta_w_r', 'delta_b_r', 'delta_w_i', 'delta_b_i', 'delta_lam', 'delta_p_a', 'delta_p_b', 'delta_w_out', 'delta_ln_g', 'delta_ln_b', 'new_m_c_ctx', 'new_m_w_mod', 'new_m_b_mod', 'new_m_w_in', 'new_m_b_in', 'new_m_lb_logits', 'new_m_norm_a_g', 'new_m_conv_w', 'new_m_conv_b', 'new_m_w_r', 'new_m_b_r', 'new_m_w_i', 'new_m_b_i', 'new_m_lam', 'new_m_p_a', 'new_m_p_b', 'new_m_w_out', 'new_m_ln_g', 'new_m_ln_b', 'new_v_c_ctx', 'new_v_w_mod', 'new_v_b_mod', 'new_v_w_in', 'new_v_b_in', 'new_v_lb_logits', 'new_v_norm_a_g', 'new_v_conv_w', 'new_v_conv_b', 'new_v_w_r', 'new_v_b_r', 'new_v_w_i', 'new_v_b_i', 'new_v_lam', 'new_v_p_a', 'new_v_p_b', 'new_v_w_out', 'new_v_ln_g', 'new_v_ln_b']
TWIN_LEAF_KINDS = {'loss': 'loss', 'grad_x': 'grad_x', 'grad_c_ctx': 'grad_w', 'grad_w_mod': 'grad_w', 'grad_b_mod': 'grad_w', 'grad_w_in': 'grad_w', 'grad_b_in': 'grad_w', 'grad_lb_logits': 'grad_w', 'grad_norm_a_g': 'grad_w', 'grad_conv_w': 'grad_w', 'grad_conv_b': 'grad_w', 'grad_w_r': 'grad_w', 'grad_b_r': 'grad_w', 'grad_w_i': 'grad_w', 'grad_b_i': 'grad_w', 'grad_lam': 'grad_w', 'grad_p_a': 'grad_w', 'grad_p_b': 'grad_w', 'grad_w_out': 'grad_w', 'grad_ln_g': 'grad_w', 'grad_ln_b': 'grad_w', 'delta_c_ctx': 'delta_w', 'delta_w_mod': 'delta_w', 'delta_b_mod': 'delta_w', 'delta_w_in': 'delta_w', 'delta_b_in': 'delta_w', 'delta_lb_logits': 'delta_w', 'delta_norm_a_g': 'delta_w', 'delta_conv_w': 'delta_w', 'delta_conv_b': 'delta_w', 'delta_w_r': 'delta_w', 'delta_b_r': 'delta_w', 'delta_w_i': 'delta_w', 'delta_b_i': 'delta_w', 'delta_lam': 'delta_w', 'delta_p_a': 'delta_w', 'delta_p_b': 'delta_w', 'delta_w_out': 'delta_w', 'delta_ln_g': 'delta_w', 'delta_ln_b': 'delta_w', 'new_m_c_ctx': 'new_m', 'new_m_w_mod': 'new_m', 'new_m_b_mod': 'new_m', 'new_m_w_in': 'new_m', 'new_m_b_in': 'new_m', 'new_m_lb_logits': 'new_m', 'new_m_norm_a_g': 'new_m', 'new_m_conv_w': 'new_m', 'new_m_conv_b': 'new_m', 'new_m_w_r': 'new_m', 'new_m_b_r': 'new_m', 'new_m_w_i': 'new_m', 'new_m_b_i': 'new_m', 'new_m_lam': 'new_m', 'new_m_p_a': 'new_m', 'new_m_p_b': 'new_m', 'new_m_w_out': 'new_m', 'new_m_ln_g': 'new_m', 'new_m_ln_b': 'new_m', 'new_v_c_ctx': 'new_v', 'new_v_w_mod': 'new_v', 'new_v_b_mod': 'new_v', 'new_v_w_in': 'new_v', 'new_v_b_in': 'new_v', 'new_v_lb_logits': 'new_v', 'new_v_norm_a_g': 'new_v', 'new_v_conv_w': 'new_v', 'new_v_conv_b': 'new_v', 'new_v_w_r': 'new_v', 'new_v_b_r': 'new_v', 'new_v_w_i': 'new_v', 'new_v_b_i': 'new_v', 'new_v_lam': 'new_v', 'new_v_p_a': 'new_v', 'new_v_p_b': 'new_v', 'new_v_w_out': 'new_v', 'new_v_ln_g': 'new_v', 'new_v_ln_b': 'new_v'}


def _forward(args):
    return _fwd_reference(*[args[k] for k in FWD_PARAMS])


def _output_shape():
    def fwd():
        inp = _fwd_setup_inputs(0)
        return _fwd_reference(*[inp[k] for k in FWD_PARAMS])
    out = _jax.eval_shape(fwd)
    return out.shape, out.dtype

N_MICROBATCH = 1
ADAM_LR = 0.001
ADAM_B1 = 0.9
ADAM_B2 = 0.999
ADAM_EPS = 1e-08
ADAM_WD = 0.01
ADAM_STEP = 10
PER_EXAMPLE_BATCH_AXIS = {'x': 0, 'c': 0, 'ctx': 0, 'loss_target': 0}
SHARED_INPUTS = []
_WEIGHT_DTYPES = {'c_ctx': _jnp.float32, 'w_mod': _jnp.float32, 'b_mod': _jnp.float32, 'w_in': _jnp.float32, 'b_in': _jnp.float32, 'lb_logits': _jnp.float32, 'norm_a_g': _jnp.float32, 'conv_w': _jnp.float32, 'conv_b': _jnp.float32, 'w_r': _jnp.float32, 'b_r': _jnp.float32, 'w_i': _jnp.float32, 'b_i': _jnp.float32, 'lam': _jnp.float32, 'p_a': _jnp.float32, 'p_b': _jnp.float32, 'w_out': _jnp.float32, 'ln_g': _jnp.float32, 'ln_b': _jnp.float32}
MOMENT_SCALE = {'c_ctx': 9.143767e-03, 'w_mod': 6.719041e-02, 'b_mod': 1.027271e-01, 'w_in': 1.935359e-02, 'b_in': 3.655724e-02, 'lb_logits': 3.844355e-04, 'norm_a_g': 2.578620e-02, 'conv_w': 4.536886e-02, 'conv_b': 1.128633e-01, 'w_r': 2.421882e-03, 'b_r': 2.823048e-03, 'w_i': 4.574746e-03, 'b_i': 7.299663e-03, 'lam': 7.604419e-03, 'p_a': 1.370504e-02, 'p_b': 6.934462e-02, 'w_out': 6.866445e-02, 'ln_g': 6.400265e+01, 'ln_b': 8.251824e-01}


def _to_microbatches(a, axis):
    t = _jnp.moveaxis(a, axis, 0)
    t = t.reshape((N_MICROBATCH, t.shape[0] // N_MICROBATCH) + t.shape[1:])
    return _jnp.moveaxis(t, 1, axis + 1)


def setup_inputs(seed: int = 0) -> dict:
    inp = _fwd_setup_inputs(seed)
    key = _jax.random.fold_in(_jax.random.key(seed), 7919)
    shape, _ = _output_shape()
    out = dict(inp)
    out["loss_target"] = _jax.random.normal(_jax.random.fold_in(key, 0), shape, _jnp.float32)
    for i, name in enumerate(TWIN_WEIGHTS):
        w = inp[name].astype(_jnp.float32)
        if MOMENT_SCALE is None:
            s = _jnp.sqrt(_jnp.mean(_jnp.square(w)) + 1e-30)
        else:
            s = MOMENT_SCALE[name]
        km, kv = _jax.random.split(_jax.random.fold_in(key, i + 1))
        out[name] = w
        out["m_" + name] = s * _jax.random.normal(km, w.shape, _jnp.float32)
        out["v_" + name] = (s * s) * _jax.random.uniform(kv, w.shape, _jnp.float32, 0.5, 1.5)
    if N_MICROBATCH > 1:
        for name, axis in PER_EXAMPLE_BATCH_AXIS.items():
            out[name] = _to_microbatches(out[name], axis)
    return {'x': out['x'], 'c': out['c'], 'ctx': out['ctx'], 'c_ctx': out['c_ctx'], 'w_mod': out['w_mod'], 'b_mod': out['b_mod'], 'w_in': out['w_in'], 'b_in': out['b_in'], 'lb_logits': out['lb_logits'], 'norm_a_g': out['norm_a_g'], 'conv_w': out['conv_w'], 'conv_b': out['conv_b'], 'w_r': out['w_r'], 'b_r': out['b_r'], 'w_i': out['w_i'], 'b_i': out['b_i'], 'lam': out['lam'], 'p_a': out['p_a'], 'p_b': out['p_b'], 'w_out': out['w_out'], 'ln_g': out['ln_g'], 'ln_b': out['ln_b'], 'loss_target': out['loss_target'], 'm_c_ctx': out['m_c_ctx'], 'm_w_mod': out['m_w_mod'], 'm_b_mod': out['m_b_mod'], 'm_w_in': out['m_w_in'], 'm_b_in': out['m_b_in'], 'm_lb_logits': out['m_lb_logits'], 'm_norm_a_g': out['m_norm_a_g'], 'm_conv_w': out['m_conv_w'], 'm_conv_b': out['m_conv_b'], 'm_w_r': out['m_w_r'], 'm_b_r': out['m_b_r'], 'm_w_i': out['m_w_i'], 'm_b_i': out['m_b_i'], 'm_lam': out['m_lam'], 'm_p_a': out['m_p_a'], 'm_p_b': out['m_p_b'], 'm_w_out': out['m_w_out'], 'm_ln_g': out['m_ln_g'], 'm_ln_b': out['m_ln_b'], 'v_c_ctx': out['v_c_ctx'], 'v_w_mod': out['v_w_mod'], 'v_b_mod': out['v_b_mod'], 'v_w_in': out['v_w_in'], 'v_b_in': out['v_b_in'], 'v_lb_logits': out['v_lb_logits'], 'v_norm_a_g': out['v_norm_a_g'], 'v_conv_w': out['v_conv_w'], 'v_conv_b': out['v_conv_b'], 'v_w_r': out['v_w_r'], 'v_b_r': out['v_b_r'], 'v_w_i': out['v_w_i'], 'v_b_i': out['v_b_i'], 'v_lam': out['v_lam'], 'v_p_a': out['v_p_a'], 'v_p_b': out['v_p_b'], 'v_w_out': out['v_w_out'], 'v_ln_g': out['v_ln_g'], 'v_ln_b': out['v_ln_b']}


def _loss(weights, diff, rest, loss_target):
    with _jax.named_scope("forward"):
        args = {**rest, TWIN_DIFF_INPUT: diff, **{k: w.astype(_WEIGHT_DTYPES[k]) for k, w in weights.items()}}
        y = _forward(args)
    with _jax.named_scope("loss_head"):
        err = _jnp.square(y.astype(_jnp.float32) - loss_target)
        return 0.5 * _jnp.sum(_jnp.mean(err, axis=-1)) if err.ndim else 0.5 * err


def _adamw(w, g, m, v):
    m = ADAM_B1 * m + (1.0 - ADAM_B1) * g
    v = ADAM_B2 * v + (1.0 - ADAM_B2) * _jnp.square(g)
    m_hat = m / (1.0 - ADAM_B1 ** ADAM_STEP)
    v_hat = v / (1.0 - ADAM_B2 ** ADAM_STEP)
    delta = -ADAM_LR * (m_hat / (_jnp.sqrt(v_hat) + ADAM_EPS) + ADAM_WD * w)
    return delta, m, v


def reference(x, c, ctx, c_ctx, w_mod, b_mod, w_in, b_in, lb_logits, norm_a_g, conv_w, conv_b, w_r, b_r, w_i, b_i, lam, p_a, p_b, w_out, ln_g, ln_b, loss_target, m_c_ctx, m_w_mod, m_b_mod, m_w_in, m_b_in, m_lb_logits, m_norm_a_g, m_conv_w, m_conv_b, m_w_r, m_b_r, m_w_i, m_b_i, m_lam, m_p_a, m_p_b, m_w_out, m_ln_g, m_ln_b, v_c_ctx, v_w_mod, v_b_mod, v_w_in, v_b_in, v_lb_logits, v_norm_a_g, v_conv_w, v_conv_b, v_w_r, v_b_r, v_w_i, v_b_i, v_lam, v_p_a, v_p_b, v_w_out, v_ln_g, v_ln_b):
    given = dict(x=x, c=c, ctx=ctx, c_ctx=c_ctx, w_mod=w_mod, b_mod=b_mod, w_in=w_in, b_in=b_in, lb_logits=lb_logits, norm_a_g=norm_a_g, conv_w=conv_w, conv_b=conv_b, w_r=w_r, b_r=b_r, w_i=w_i, b_i=b_i, lam=lam, p_a=p_a, p_b=p_b, w_out=w_out, ln_g=ln_g, ln_b=ln_b, loss_target=loss_target, m_c_ctx=m_c_ctx, m_w_mod=m_w_mod, m_b_mod=m_b_mod, m_w_in=m_w_in, m_b_in=m_b_in, m_lb_logits=m_lb_logits, m_norm_a_g=m_norm_a_g, m_conv_w=m_conv_w, m_conv_b=m_conv_b, m_w_r=m_w_r, m_b_r=m_b_r, m_w_i=m_w_i, m_b_i=m_b_i, m_lam=m_lam, m_p_a=m_p_a, m_p_b=m_p_b, m_w_out=m_w_out, m_ln_g=m_ln_g, m_ln_b=m_ln_b, v_c_ctx=v_c_ctx, v_w_mod=v_w_mod, v_b_mod=v_b_mod, v_w_in=v_w_in, v_b_in=v_b_in, v_lb_logits=v_lb_logits, v_norm_a_g=v_norm_a_g, v_conv_w=v_conv_w, v_conv_b=v_conv_b, v_w_r=v_w_r, v_b_r=v_b_r, v_w_i=v_w_i, v_b_i=v_b_i, v_lam=v_lam, v_p_a=v_p_a, v_p_b=v_p_b, v_w_out=v_w_out, v_ln_g=v_ln_g, v_ln_b=v_ln_b)
    weights = {n: given[n] for n in TWIN_WEIGHTS}
    shared = {n: given[n] for n in SHARED_INPUTS}
    per_example = {n: given[n] for n in ['x', 'c', 'ctx']}
    grad_fn = _jax.value_and_grad(_loss, argnums=(0, 1))

    def one_microbatch(ex, loss_target):
        ex = dict(ex)
        diff = ex.pop(TWIN_DIFF_INPUT)
        return grad_fn(weights, diff, {**shared, **ex}, loss_target)

    if N_MICROBATCH == 1:
        loss, (grad_w, grad_x) = one_microbatch(per_example, given["loss_target"])
    else:
        def body(carry, xs):
            loss_sum, grad_sum = carry
            l_k, (gw_k, gx_k) = one_microbatch(xs[0], xs[1])
            with _jax.named_scope("update"):
                return (loss_sum + l_k, _jax.tree.map(_jnp.add, grad_sum, gw_k)), gx_k

        init = (_jnp.zeros((), _jnp.float32), _jax.tree.map(_jnp.zeros_like, weights))
        (loss, grad_w), grad_x = _jax.lax.scan(body, init, (per_example, given["loss_target"]))
    with _jax.named_scope("update"):
        delta_w, new_m, new_v = {}, {}, {}
        for n in TWIN_WEIGHTS:
            delta_w[n], new_m[n], new_v[n] = _adamw(weights[n], grad_w[n], given["m_" + n], given["v_" + n])
    return (loss, grad_x, *[grad_w[n] for n in TWIN_WEIGHTS], *[delta_w[n] for n in TWIN_WEIGHTS],
            *[new_m[n] for n in TWIN_WEIGHTS], *[new_v[n] for n in TWIN_WEIGHTS])
```

```python
import functools

import jax
import jax.numpy as jnp
from jax import lax
from jax.experimental import pallas as pl
from jax.experimental.pallas import tpu as pltpu

F32 = jnp.float32
BF16 = jnp.bfloat16

D = 1024
NH = 8
DH = 128
CHUNK = 64
GRID_W = 64
NGRP = 9
NDEV = 8
RG_C = 8.0
ALPHA = 2.0 ** 0.25
LN_EPS = 1e-5
RMS_EPS = 1e-6
Q_SCALE = DH ** -0.5
ADAM_LR, ADAM_B1, ADAM_B2, ADAM_EPS, ADAM_WD, ADAM_STEP = 1e-3, 0.9, 0.999, 1e-8, 0.01, 10
ADAM_C1 = 1.0 / (1.0 - ADAM_B1 ** ADAM_STEP)
ADAM_C2 = 1.0 / (1.0 - ADAM_B2 ** ADAM_STEP)

ANY = pl.BlockSpec(memory_space=pl.ANY)


def _sigmoid(t):
    return 1.0 / (1.0 + jnp.exp(-t))


def _dsilu(t, s):
    return s * (1.0 + t * (1.0 - s))


def _dot(a, b):
    return jnp.dot(a, b, preferred_element_type=F32)


def _dot_nt(a, b):
    return lax.dot_general(a, b, (((1,), (1,)), ((), ())), preferred_element_type=F32)


def _dot_tn(a, b):
    return lax.dot_general(a, b, (((0,), (0,)), ((), ())), preferred_element_type=F32)


def _my_index():
    return 4 * lax.axis_index("x") + 2 * lax.axis_index("y") + lax.axis_index("c")


def _dev_tuple(j):
    return (j >> 2, (j >> 1) & 1, j & 1)


def _all_gather(shards, name):
    n = len(shards)

    def body(*refs):
        ins, outs = refs[:n], refs[n:2 * n]
        send_sems, recv_sems, loc_sems = refs[2 * n:]
        me = _my_index()
        for a in range(n):
            pltpu.make_async_copy(ins[a], outs[a].at[me], loc_sems.at[a]).start()
        for j in range(NDEV):
            @pl.when(me != j)
            def _():
                for a in range(n):
                    pltpu.make_async_remote_copy(
                        src_ref=ins[a], dst_ref=outs[a].at[me],
                        send_sem=send_sems.at[a * NDEV + j], recv_sem=recv_sems.at[a * NDEV + me],
                        device_id=_dev_tuple(j), device_id_type=pl.DeviceIdType.MESH).start()
        for j in range(NDEV):
            @pl.when(me != j)
            def _():
                for a in range(n):
                    pltpu.make_async_remote_copy(
                        src_ref=ins[a], dst_ref=outs[a].at[j],
                        send_sem=send_sems.at[a * NDEV + j], recv_sem=recv_sems.at[a * NDEV + j],
                        device_id=_dev_tuple(j), device_id_type=pl.DeviceIdType.MESH).wait()
        for a in range(n):
            pltpu.make_async_copy(ins[a], outs[a].at[me], loc_sems.at[a]).wait()

    return pl.pallas_call(
        body, name=name,
        out_shape=[jax.ShapeDtypeStruct((NDEV,) + s.shape, s.dtype) for s in shards],
        in_specs=[ANY] * n, out_specs=[ANY] * n,
        scratch_shapes=[pltpu.SemaphoreType.DMA((n * NDEV,)), pltpu.SemaphoreType.DMA((n * NDEV,)),
                        pltpu.SemaphoreType.DMA((n,))],
    )(*shards)


def _reduce_scatter_exchange(parts, splits, name):
    n = len(parts)
    pieces = []
    for a in range(n):
        r, c = parts[a].shape
        pieces.append((r // NDEV, c) if splits[a] == 0 else (r, c // NDEV))

    def piece_ref(ref, a, j):
        pr, pc = pieces[a]
        if splits[a] == 0:
            return ref.at[pl.ds(j * pr, pr), :]
        return ref.at[:, pl.ds(j * pc, pc)]

    def body(*refs):
        ins, outs = refs[:n], refs[n:2 * n]
        send_sems, recv_sems, loc_sems = refs[2 * n:]
        me = _my_index()
        for j in range(NDEV):
            @pl.when(me == j)
            def _():
                for a in range(n):
                    pltpu.make_async_copy(piece_ref(ins[a], a, j), outs[a].at[j], loc_sems.at[a]).start()

            @pl.when(me != j)
            def _():
                for a in range(n):
                    pltpu.make_async_remote_copy(
                        src_ref=piece_ref(ins[a], a, j), dst_ref=outs[a].at[me],
                        send_sem=send_sems.at[a * NDEV + j], recv_sem=recv_sems.at[a * NDEV + me],
                        device_id=_dev_tuple(j), device_id_type=pl.DeviceIdType.MESH).start()
        for j in range(NDEV):
            @pl.when(me == j)
            def _():
                for a in range(n):
                    pltpu.make_async_copy(piece_ref(ins[a], a, j), outs[a].at[j], loc_sems.at[a]).wait()

            @pl.when(me != j)
            def _():
                for a in range(n):
                    pltpu.make_async_remote_copy(
                        src_ref=piece_ref(ins[a], a, j), dst_ref=outs[a].at[j],
                        send_sem=send_sems.at[a * NDEV + j], recv_sem=recv_sems.at[a * NDEV + j],
                        device_id=_dev_tuple(j), device_id_type=pl.DeviceIdType.MESH).wait()

    return pl.pallas_call(
        body, name=name,
        out_shape=[jax.ShapeDtypeStruct((NDEV,) + pieces[a], parts[a].dtype) for a in range(n)],
        in_specs=[ANY] * n, out_specs=[ANY] * n,
        scratch_shapes=[pltpu.SemaphoreType.DMA((n * NDEV,)), pltpu.SemaphoreType.DMA((n * NDEV,)),
                        pltpu.SemaphoreType.DMA((n,))],
    )(*parts)


def _adam_math(g, w, m, v):
    m2 = ADAM_B1 * m + (1.0 - ADAM_B1) * g
    v2 = ADAM_B2 * v + (1.0 - ADAM_B2) * (g * g)
    delta = -ADAM_LR * ((m2 * ADAM_C1) / (jnp.sqrt(v2 * ADAM_C2) + ADAM_EPS) + ADAM_WD * w)
    return delta, m2, v2


def _row_tile(r, cap):
    t = min(r, cap)
    while r % t:
        t //= 2
    return t


def _adamw(g, w, m, v, name):
    shape = w.shape
    cols = shape[-1] if w.ndim >= 2 and shape[-1] % 128 == 0 else 128
    g2, w2, m2, v2 = (t.reshape(-1, cols) for t in (g, w, m, v))
    r = g2.shape[0]
    tr = _row_tile(r, 256)

    def body(g_ref, w_ref, m_ref, v_ref, d_ref, mo_ref, vo_ref):
        d, mm, vv = _adam_math(g_ref[...], w_ref[...], m_ref[...], v_ref[...])
        d_ref[...] = d
        mo_ref[...] = mm
        vo_ref[...] = vv

    spec = pl.BlockSpec((tr, cols), lambda i: (i, 0))
    outs = pl.pallas_call(
        body, name=name, grid=(r // tr,),
        out_shape=[jax.ShapeDtypeStruct((r, cols), F32)] * 3,
        in_specs=[spec] * 4, out_specs=[spec] * 3,
    )(g2, w2, m2, v2)
    return tuple(o.reshape(shape) for o in outs)


def _sum_adamw(parts, w, m, v, name):
    _, r, c = parts.shape
    shape = w.shape
    w2, m2, v2 = (t.reshape(r, c) for t in (w, m, v))
    tr = _row_tile(r, 128)

    def body(p_ref, w_ref, m_ref, v_ref, g_ref, d_ref, mo_ref, vo_ref):
        g = p_ref[0].astype(F32)
        for k in range(1, NDEV):
            g = g + p_ref[k].astype(F32)
        d, mm, vv = _adam_math(g, w_ref[...], m_ref[...], v_ref[...])
        g_ref[...] = g
        d_ref[...] = d
        mo_ref[...] = mm
        vo_ref[...] = vv

    spec = pl.BlockSpec((tr, c), lambda i: (i, 0))
    outs = pl.pallas_call(
        body, name=name, grid=(r // tr,),
        out_shape=[jax.ShapeDtypeStruct((r, c), F32)] * 4,
        in_specs=[pl.BlockSpec((NDEV, tr, c), lambda i: (0, i, 0))] + [spec] * 3, out_specs=[spec] * 4,
    )(parts, w2, m2, v2)
    return tuple(o.reshape(shape) for o in outs)


def _sum_rows(parts, name):
    _, r, c = parts.shape

    def body(p_ref, o_ref):
        g = p_ref[0]
        for k in range(1, NDEV):
            g = g + p_ref[k]
        o_ref[...] = g

    return pl.pallas_call(
        body, name=name, out_shape=jax.ShapeDtypeStruct((r, c), F32),
    )(parts)


def _prep(cc, w_mod_full, b_mod, lbl):
    def body(cc_ref, w_ref, b_ref, l_ref, mod_ref, lb_ref):
        t = cc_ref[...]
        s = (t * _sigmoid(t)).astype(BF16)
        mod_ref[...] = _dot(s, w_ref[...]) + b_ref[...]
        lb_ref[...] = _sigmoid(l_ref[0] - l_ref[1])

    return pl.pallas_call(
        body, name="prep",
        out_shape=[jax.ShapeDtypeStruct((8, 3 * D), F32), jax.ShapeDtypeStruct((8, D), F32)],
    )(cc, w_mod_full, b_mod, lbl)


def _modulate(xin, mod, row, name):
    n = xin.shape[0]
    tm = _row_tile(n, 512)

    def body(x_ref, mod_ref, u_ref):
        sh = mod_ref[row:row + 1, 0:D]
        sc = mod_ref[row:row + 1, D:2 * D]
        u_ref[...] = (x_ref[...] * (1.0 + sc) + sh).astype(BF16)

    return pl.pallas_call(
        body, name=name, grid=(n // tm,),
        out_shape=jax.ShapeDtypeStruct((n, D), BF16),
        in_specs=[pl.BlockSpec((tm, D), lambda i: (i, 0)), pl.BlockSpec((8, 3 * D), lambda i: (0, 0))],
        out_specs=pl.BlockSpec((tm, D), lambda i: (i, 0)),
    )(xin, mod)


def _mm_bias(a, b, bias, name):
    m, k = a.shape
    n = b.shape[1]
    tm = _row_tile(m, 512)
    tn = 1024

    def body(a_ref, b_ref, bias_ref, o_ref):
        o_ref[...] = _dot(a_ref[...], b_ref[...]) + bias_ref[...]

    return pl.pallas_call(
        body, name=name, grid=(n // tn, m // tm),
        out_shape=jax.ShapeDtypeStruct((m, n), F32),
        in_specs=[pl.BlockSpec((tm, k), lambda j, i: (i, 0)), pl.BlockSpec((k, tn), lambda j, i: (0, j)),
                  pl.BlockSpec((1, tn), lambda j, i: (0, j))],
        out_specs=pl.BlockSpec((tm, tn), lambda j, i: (i, j)),
    )(a, b, bias)


def _mm_tn(a, b, init, name, with_colsum=False, colsum_init=None):
    m, ka = a.shape
    n = b.shape[1]
    tk = _row_tile(m, 512)
    tn = 1024
    nk = m // tk
    has_init = init is not None

    def body(*refs):
        a_ref, b_ref = refs[0], refs[1]
        pos = 2
        init_ref = cs_init_ref = None
        if has_init:
            init_ref = refs[pos]
            pos += 1
            if with_colsum:
                cs_init_ref = refs[pos]
                pos += 1
        o_ref = refs[pos]
        cs_ref = refs[pos + 1] if with_colsum else None
        k = pl.program_id(1)

        @pl.when(k == 0)
        def _():
            if has_init:
                o_ref[...] = init_ref[...]
                if with_colsum:
                    cs_ref[...] = cs_init_ref[...]
            else:
                o_ref[...] = jnp.zeros_like(o_ref)
                if with_colsum:
                    cs_ref[...] = jnp.zeros_like(cs_ref)

        bv = b_ref[...]
        o_ref[...] += _dot_tn(a_ref[...], bv)
        if with_colsum:
            cs_ref[...] += jnp.sum(bv.astype(F32), axis=0, keepdims=True)

    in_specs = [pl.BlockSpec((tk, ka), lambda j, k: (k, 0)), pl.BlockSpec((tk, tn), lambda j, k: (k, j))]
    args = [a, b]
    if has_init:
        in_specs.append(pl.BlockSpec((ka, tn), lambda j, k: (0, j)))
        args.append(init)
        if with_colsum:
            in_specs.append(pl.BlockSpec((1, tn), lambda j, k: (0, j)))
            args.append(colsum_init)
    out_shape = [jax.ShapeDtypeStruct((ka, n), F32)]
    out_specs = [pl.BlockSpec((ka, tn), lambda j, k: (0, j))]
    if with_colsum:
        out_shape.append(jax.ShapeDtypeStruct((1, n), F32))
        out_specs.append(pl.BlockSpec((1, tn), lambda j, k: (0, j)))
    outs = pl.pallas_call(
        body, name=name, grid=(n // tn, nk), out_shape=out_shape, in_specs=in_specs, out_specs=out_specs,
    )(*args)
    return outs if with_colsum else outs[0]


def _input_grad(dz, w_full, xin, dr, mod, row, name):
    m, n = dz.shape
    tm = _row_tile(m, 512)
    tk = 1024
    nk = n // tk
    has_dr = dr is not None

    def body(*refs):
        if has_dr:
            dz_ref, w_ref, x_ref, dr_ref, mod_ref, gx_ref, vec_ref, acc = refs
        else:
            dz_ref, w_ref, x_ref, mod_ref, vec_ref, acc = refs
        i, k = pl.program_id(0), pl.program_id(1)

        @pl.when(k == 0)
        def _():
            acc[...] = jnp.zeros_like(acc)

        @pl.when((i == 0) & (k == 0))
        def _():
            vec_ref[...] = jnp.zeros_like(vec_ref)

        acc[...] += _dot_nt(dz_ref[...], w_ref[...])

        @pl.when(k == nk - 1)
        def _():
            du = acc[...]
            xv = x_ref[...]
            if has_dr:
                sc = mod_ref[row:row + 1, D:2 * D]
                gx_ref[...] = ALPHA * dr_ref[...] + du * (1.0 + sc)
            vec_ref[0:1, :] += jnp.sum(du, axis=0, keepdims=True)
            vec_ref[1:2, :] += jnp.sum(du * xv, axis=0, keepdims=True)

    row_spec = pl.BlockSpec((tm, D), lambda i, k: (i, 0))
    in_specs = [pl.BlockSpec((tm, tk), lambda i, k: (i, k)), pl.BlockSpec((D, tk), lambda i, k: (0, k)), row_spec]
    args = [dz, w_full, xin]
    if has_dr:
        in_specs.append(row_spec)
        args.append(dr)
    in_specs.append(pl.BlockSpec((8, 3 * D), lambda i, k: (0, 0)))
    args.append(mod)
    out_shape, out_specs = [], []
    if has_dr:
        out_shape.append(jax.ShapeDtypeStruct((m, D), F32))
        out_specs.append(row_spec)
    out_shape.append(jax.ShapeDtypeStruct((8, D), F32))
    out_specs.append(pl.BlockSpec((8, D), lambda i, k: (0, 0)))
    outs = pl.pallas_call(
        body, name=name, grid=(m // tm, nk), out_shape=out_shape, in_specs=in_specs, out_specs=out_specs,
        scratch_shapes=[pltpu.VMEM((tm, D), F32)],
    )(*args)
    return outs if has_dr else (None, outs[0])


def _tri(reverse):
    r = lax.broadcasted_iota(jnp.int32, (CHUNK, CHUNK), 0)
    c = lax.broadcasted_iota(jnp.int32, (CHUNK, CHUNK), 1)
    return (c >= r) if reverse else (c <= r)


def _cum_f32(tri_b, t):
    hi = t.astype(BF16)
    r1 = t - hi.astype(F32)
    mid = r1.astype(BF16)
    lo = (r1 - mid.astype(F32)).astype(BF16)
    return _dot(tri_b, hi) + _dot(tri_b, mid) + _dot(tri_b, lo)


def _gla_features(zq, zf, lb):
    sq = _sigmoid(zq)
    q = zq * sq * Q_SCALE
    sf = _sigmoid(zf)
    f = lb + (1.0 - lb) * sf
    return q, sq, f, sf


def _gla_decays(f, tri_b, last):
    lf = jnp.log(f)
    g = _cum_f32(tri_b, lf)
    gl = g[last:last + 1, :]
    return g, gl


def _gla_block(n):
    return 256 if n % 256 == 0 else CHUNK


def _gla_fwd(z, lb, s0, d, name):
    n = z.shape[0]
    blk = _gla_block(n)
    nb, npb = n // blk, blk // CHUNK
    reverse = d == 1
    last = 0 if reverse else CHUNK - 1
    order = list(range(npb))[::-1] if reverse else list(range(npb))

    def bmap(i):
        return nb - 1 - i if reverse else i

    def body(zq_ref, zf_ref, zv_ref, lb_ref, s0_ref, o_ref, ss_ref, sf_ref, st):
        i = pl.program_id(1)

        @pl.when(i == 0)
        def _():
            st[...] = s0_ref[0]

        mask = _tri(reverse)
        tri_b = jnp.where(mask, 1.0, 0.0).astype(BF16)
        lbv = lb_ref[d:d + 1, :]
        for cidx in order:
            rows = pl.ds(cidx * CHUNK, CHUNK)
            q, _, f, _ = _gla_features(zq_ref[rows, :], zf_ref[rows, :], lbv)
            k = 1.0 - f
            g, gl = _gla_decays(f, tri_b, last)
            qd = (q * jnp.exp(g)).astype(BF16)
            ki = (k * jnp.exp(-g)).astype(BF16)
            ke = (k * jnp.exp(gl - g)).astype(BF16)
            vb = zv_ref[rows, :].astype(BF16)
            s_in = st[...]
            a = jnp.where(mask, _dot_nt(qd, ki), 0.0).astype(BF16)
            o_ref[rows, :] = _dot(a, vb) + _dot_nt(qd, s_in.astype(BF16))
            ss_ref[0, cidx] = s_in
            st[...] = s_in * jnp.exp(gl) + _dot_tn(vb, ke)
        sf_ref[0] = st[...]

    def col(g):
        return lambda h, i: (bmap(i), g * NH + h)

    return pl.pallas_call(
        body, name=name, grid=(NH, nb),
        out_shape=[jax.ShapeDtypeStruct((n, D), F32), jax.ShapeDtypeStruct((NH, n // CHUNK, DH, DH), F32),
                   jax.ShapeDtypeStruct((NH, DH, DH), F32)],
        in_specs=[pl.BlockSpec((blk, DH), col(0)), pl.BlockSpec((blk, DH), col(1 + d)),
                  pl.BlockSpec((blk, DH), col(3)), pl.BlockSpec((8, DH), lambda h, i: (0, h)),
                  pl.BlockSpec((1, DH, DH), lambda h, i: (h, 0, 0))],
        out_specs=[pl.BlockSpec((blk, DH), lambda h, i: (bmap(i), h)),
                   pl.BlockSpec((1, npb, DH, DH), lambda h, i: (h, bmap(i), 0, 0)),
                   pl.BlockSpec((1, DH, DH), lambda h, i: (h, 0, 0))],
        scratch_shapes=[pltpu.VMEM((DH, DH), F32)],
    )(z, z, z, lb, s0)


def _gla_bwd(z, lb, s_start, do, ds_fin, acc_q, acc_v, d, name):
    n = z.shape[0]
    blk = _gla_block(n)
    nb, npb = n // blk, blk // CHUNK
    reverse = d == 1
    last = 0 if reverse else CHUNK - 1
    order = list(range(npb)) if reverse else list(range(npb))[::-1]
    has_do = do is not None
    has_acc = acc_q is not None

    def bmap(i):
        return i if reverse else nb - 1 - i

    def body(*refs):
        zq_ref, zf_ref, zv_ref, lb_ref, ss_ref, dsf_ref = refs[:6]
        pos = 6
        do_ref = aq_ref = av_ref = None
        if has_do:
            do_ref = refs[pos]
            pos += 1
        if has_acc:
            aq_ref, av_ref = refs[pos], refs[pos + 1]
            pos += 2
        dzq_ref, dzf_ref, dzv_ref, dlb_ref, ds0_ref, dst = refs[pos:]
        i = pl.program_id(1)

        @pl.when(i == 0)
        def _():
            dst[...] = dsf_ref[0]
            dlb_ref[...] = jnp.zeros_like(dlb_ref)

        mask = _tri(reverse)
        tri_b = jnp.where(mask, 1.0, 0.0).astype(BF16)
        tri_t = jnp.where(_tri(not reverse), 1.0, 0.0).astype(BF16)
        lbv = lb_ref[d:d + 1, :]
        for cidx in order:
            rows = pl.ds(cidx * CHUNK, CHUNK)
            zq, zf = zq_ref[rows, :], zf_ref[rows, :]
            q, sq, f, sf = _gla_features(zq, zf, lbv)
            k = 1.0 - f
            g, gl = _gla_decays(f, tri_b, last)
            eg, egi, ege, egl = jnp.exp(g), jnp.exp(-g), jnp.exp(gl - g), jnp.exp(gl)
            qd, ki, ke = q * eg, k * egi, k * ege
            qdb, kib, keb = qd.astype(BF16), ki.astype(BF16), ke.astype(BF16)
            v = zv_ref[rows, :]
            vb = v.astype(BF16)
            s_in = ss_ref[0, cidx]
            ds = dst[...]
            dsb = ds.astype(BF16)
            dke = _dot(vb, dsb)
            dv = _dot_nt(keb, dsb)
            dgl = egl * jnp.sum(s_in * ds, axis=0, keepdims=True)
            ds_new = ds * egl
            dg = -dke * ke
            dk = dke * ege
            dgl = dgl + jnp.sum(dke * ke, axis=0, keepdims=True)
            if has_do:
                dob = do_ref[rows, :].astype(BF16)
                a = jnp.where(mask, _dot_nt(qdb, kib), 0.0).astype(BF16)
                da = jnp.where(mask, _dot_nt(dob, vb), 0.0).astype(BF16)
                dv = dv + _dot_tn(a, dob)
                dqd = _dot(da, kib) + _dot(dob, s_in.astype(BF16))
                dki = _dot_tn(da, qdb)
                ds_new = ds_new + _dot_tn(dob, qdb)
                dg = dg + dqd * qd - dki * ki
                dk = dk + dki * egi
                dq = dqd * eg
                dzq = dq * (Q_SCALE * _dsilu(zq, sq))
            else:
                dzq = jnp.zeros((CHUNK, DH), F32)
            dlf = _cum_f32(tri_t, dg) + dgl
            df = dlf / f - dk
            dzf_ref[rows, :] = df * (1.0 - lbv) * sf * (1.0 - sf)
            dlb_ref[0:1, :] += jnp.sum(df * (1.0 - sf), axis=0, keepdims=True)
            if has_acc:
                dzq = dzq + aq_ref[rows, :]
                dv = dv + av_ref[rows, :]
            dzq_ref[rows, :] = dzq
            dzv_ref[rows, :] = dv
            dst[...] = ds_new
        ds0_ref[0] = dst[...]

    def col(g):
        return lambda h, i: (bmap(i), g * NH + h)

    tok = pl.BlockSpec((blk, DH), lambda h, i: (bmap(i), h))
    state = pl.BlockSpec((1, DH, DH), lambda h, i: (h, 0, 0))
    in_specs = [pl.BlockSpec((blk, DH), col(0)), pl.BlockSpec((blk, DH), col(1 + d)), pl.BlockSpec((blk, DH), col(3)),
                pl.BlockSpec((8, DH), lambda h, i: (0, h)),
                pl.BlockSpec((1, npb, DH, DH), lambda h, i: (h, bmap(i), 0, 0)), state]
    args = [z, z, z, lb, s_start, ds_fin]
    if has_do:
        in_specs.append(tok)
        args.append(do)
    if has_acc:
        in_specs += [tok, tok]
        args += [acc_q, acc_v]
    return pl.pallas_call(
        body, name=name, grid=(NH, nb),
        out_shape=[jax.ShapeDtypeStruct((n, D), F32)] * 3 + [jax.ShapeDtypeStruct((8, D), F32),
                                                            jax.ShapeDtypeStruct((NH, DH, DH), F32)],
        in_specs=in_specs,
        out_specs=[tok, tok, tok, pl.BlockSpec((8, DH), lambda h, i: (0, h)), state],
        scratch_shapes=[pltpu.VMEM((DH, DH), F32)],
    )(*args)


def _shift(t, s, fill, down):
    n = t.shape[0]
    rows = lax.broadcasted_iota(jnp.int32, t.shape, 0)
    if down:
        return jnp.where(rows >= s, pltpu.roll(t, s, 0), fill)
    return jnp.where(rows < n - s, pltpu.roll(t, n - s, 0), fill)


def _chain_scan(a, b, down):
    n = a.shape[0]
    s = 1
    while s < n:
        b = b + a * _shift(b, s, 0.0, down)
        a = a * _shift(a, s, 1.0, down)
        s *= 2
    return a, b


def _conv(xv, cw, cb):
    return (cb + cw[0:1, :] * _shift(xv, 1, 0.0, True) + cw[1:2, :] * xv
            + cw[2:3, :] * _shift(xv, 1, 0.0, False) + cw[3:4, :] * _shift(xv, 2, 0.0, False))


def _neg_expm1(t):
    series = -t * (1.0 + t * (0.5 + t * (1.0 / 6.0 + t * (1.0 / 24.0 + t * (1.0 / 120.0)))))
    return jnp.where(t > -0.1, series, 1.0 - jnp.exp(t))


def _block_diag(tb, w_ref):
    return jnp.concatenate([_dot(tb[:, g * DH:(g + 1) * DH], w_ref[g]) for g in range(NH)], axis=1)


def _block_diag_nt(tb, w_ref):
    return jnp.concatenate([_dot_nt(tb[:, g * DH:(g + 1) * DH], w_ref[g]) for g in range(NH)], axis=1)


def _lru_gates(xc, wr_ref, br, wi_ref, bi, lam):
    xcb = xc.astype(BF16)
    r = _sigmoid(_block_diag(xcb, wr_ref) + br)
    gi = _sigmoid(_block_diag(xcb, wi_ref) + bi)
    sp = jnp.maximum(-lam, 0.0) + jnp.log(1.0 + jnp.exp(-jnp.abs(lam)))
    la = -RG_C * r * sp
    a = jnp.exp(la)
    mult = jnp.sqrt(_neg_expm1(2.0 * la))
    return xcb, r, gi, sp, a, mult


def _lru_fwd(xin, blk, cw, cb, wr, br, wi, bi, lam, h0, d, name):
    n = xin.shape[0]
    nb = n // blk
    reverse = d == 1
    down = not reverse
    last = 0 if reverse else blk - 1

    def bmap(i):
        return nb - 1 - i if reverse else i

    def body(x_ref, cw_ref, cb_ref, wr_ref, br_ref, wi_ref, bi_ref, lam_ref, h0_ref, h_ref, hin_ref, hfin_ref, carry):
        i = pl.program_id(0)

        @pl.when(i == 0)
        def _():
            carry[...] = h0_ref[...]

        xc = _conv(x_ref[...], cw_ref[...], cb_ref[...])
        _, _, gi, _, a, mult = _lru_gates(xc, wr_ref, br_ref[...], wi_ref, bi_ref[...], lam_ref[...])
        aa, bb = _chain_scan(a, mult * gi * xc, down)
        hin = carry[...]
        h = bb + aa * hin
        h_ref[...] = h
        hin_ref[0] = hin
        carry[...] = h[last:last + 1, :]
        hfin_ref[...] = h[last:last + 1, :]

    vec = pl.BlockSpec((1, D), lambda i: (0, 0))
    wsp = pl.BlockSpec((NH, DH, DH), lambda i: (0, 0, 0))
    return pl.pallas_call(
        body, name=name, grid=(nb,),
        out_shape=[jax.ShapeDtypeStruct((n, D), F32), jax.ShapeDtypeStruct((nb, 1, D), F32),
                   jax.ShapeDtypeStruct((1, D), F32)],
        in_specs=[pl.BlockSpec((blk, D), lambda i: (bmap(i), 0)), pl.BlockSpec((4, D), lambda i: (0, 0)), vec,
                  wsp, vec, wsp, vec, vec, vec],
        out_specs=[pl.BlockSpec((blk, D), lambda i: (bmap(i), 0)), pl.BlockSpec((1, 1, D), lambda i: (bmap(i), 0, 0)), vec],
        scratch_shapes=[pltpu.VMEM((1, D), F32)],
    )(xin, cw, cb, wr, br, wi, bi, lam, h0)


def _lru_bwd(xin, blk, cw, cb, wr, br, wi, bi, lam, h, hin, dh, cg_fin, acc_dx, init, d, name):
    n = xin.shape[0]
    nb = n // blk
    reverse = d == 1
    down = not reverse
    first = blk - 1 if reverse else 0
    has_dh = dh is not None
    has_acc = acc_dx is not None
    has_init = init is not None

    def bmap(i):
        return i if reverse else nb - 1 - i

    def body(*refs):
        (x_ref, cw_ref, cb_ref, wr_ref, br_ref, wi_ref, bi_ref, lam_ref, h_ref, hin_ref, cgf_ref) = refs[:11]
        pos = 11
        dh_ref = acc_ref = None
        iwr_ref = iwi_ref = ivec_ref = None
        if has_dh:
            dh_ref = refs[pos]
            pos += 1
        if has_acc:
            acc_ref = refs[pos]
            pos += 1
        if has_init:
            iwr_ref, iwi_ref, ivec_ref = refs[pos:pos + 3]
            pos += 3
        dx_ref, dwr_ref, dwi_ref, vec_ref, cg0_ref, carry = refs[pos:]
        i = pl.program_id(0)

        @pl.when(i == 0)
        def _():
            carry[...] = cgf_ref[...]
            if has_init:
                dwr_ref[...] = iwr_ref[...]
                dwi_ref[...] = iwi_ref[...]
                vec_ref[...] = ivec_ref[...]
            else:
                dwr_ref[...] = jnp.zeros_like(dwr_ref)
                dwi_ref[...] = jnp.zeros_like(dwi_ref)
                vec_ref[...] = jnp.zeros_like(vec_ref)

        xv = x_ref[...]
        cwv = cw_ref[...]
        lam_v = lam_ref[...]
        xc = _conv(xv, cwv, cb_ref[...])
        xcb, r, gi, sp, a, mult = _lru_gates(xc, wr_ref, br_ref[...], wi_ref, bi_ref[...], lam_v)
        hv = h_ref[...]
        hprev = _shift(hv, 1, hin_ref[0], down)
        a_next = _shift(a, 1, 1.0, not down)
        dhv = dh_ref[...] if has_dh else jnp.zeros_like(a)
        aa, bb = _chain_scan(a_next, dhv, not down)
        e = bb + aa * carry[...]
        carry[...] = a[first:first + 1, :] * e[first:first + 1, :]
        cg0_ref[...] = a[first:first + 1, :] * e[first:first + 1, :]
        da = e * hprev
        db = e
        ix = gi * xc
        dmult = db * ix
        dgi = db * mult * xc
        dxc = db * mult * gi
        dla = da * a - dmult * (a * a) / mult
        dr = dla * (-RG_C * sp)
        sneg = 1.0 - _sigmoid(lam_v)
        dlam = jnp.sum(dla * r, axis=0, keepdims=True) * (RG_C * sneg)
        dpr = (dr * r * (1.0 - r))
        dpi = (dgi * gi * (1.0 - gi))
        dprb, dpib = dpr.astype(BF16), dpi.astype(BF16)
        dxc = dxc + _block_diag_nt(dprb, wr_ref) + _block_diag_nt(dpib, wi_ref)
        for g in range(NH):
            sl = slice(g * DH, (g + 1) * DH)
            dwr_ref[g] += _dot_tn(xcb[:, sl], dprb[:, sl])
            dwi_ref[g] += _dot_tn(xcb[:, sl], dpib[:, sl])
        dx = (cwv[0:1, :] * _shift(dxc, 1, 0.0, False) + cwv[1:2, :] * dxc
              + cwv[2:3, :] * _shift(dxc, 1, 0.0, True) + cwv[3:4, :] * _shift(dxc, 2, 0.0, True))
        if has_acc:
            dx = dx + acc_ref[...]
        dx_ref[...] = dx
        vec_ref[0:1, :] += jnp.sum(dpr, axis=0, keepdims=True)
        vec_ref[1:2, :] += jnp.sum(dpi, axis=0, keepdims=True)
        vec_ref[2:3, :] += dlam
        vec_ref[3:4, :] += jnp.sum(dxc, axis=0, keepdims=True)
        vec_ref[4:5, :] += jnp.sum(dxc * _shift(xv, 1, 0.0, True), axis=0, keepdims=True)
        vec_ref[5:6, :] += jnp.sum(dxc * xv, axis=0, keepdims=True)
        vec_ref[6:7, :] += jnp.sum(dxc * _shift(xv, 1, 0.0, False), axis=0, keepdims=True)
        vec_ref[7:8, :] += jnp.sum(dxc * _shift(xv, 2, 0.0, False), axis=0, keepdims=True)

    vec = pl.BlockSpec((1, D), lambda i: (0, 0))
    wsp = pl.BlockSpec((NH, DH, DH), lambda i: (0, 0, 0))
    tok = pl.BlockSpec((blk, D), lambda i: (bmap(i), 0))
    vec16 = pl.BlockSpec((16, D), lambda i: (0, 0))
    in_specs = [tok, pl.BlockSpec((4, D), lambda i: (0, 0)), vec, wsp, vec, wsp, vec, vec, tok,
                pl.BlockSpec((1, 1, D), lambda i: (bmap(i), 0, 0)), vec]
    args = [xin, cw, cb, wr, br, wi, bi, lam, h, hin, cg_fin]
    if has_dh:
        in_specs.append(tok)
        args.append(dh)
    if has_acc:
        in_specs.append(tok)
        args.append(acc_dx)
    if has_init:
        in_specs += [wsp, wsp, vec16]
        args += list(init)
    return pl.pallas_call(
        body, name=name, grid=(nb,),
        out_shape=[jax.ShapeDtypeStruct((n, D), F32), jax.ShapeDtypeStruct((NH, DH, DH), F32),
                   jax.ShapeDtypeStruct((NH, DH, DH), F32), jax.ShapeDtypeStruct((16, D), F32),
                   jax.ShapeDtypeStruct((1, D), F32)],
        in_specs=in_specs, out_specs=[tok, wsp, wsp, vec16, vec],
        scratch_shapes=[pltpu.VMEM((1, D), F32)],
    )(*args)


def _merge(z, o_f, o_b, hx, xin, tgt, mod, gn, p_a, p_b, w_out, ln_g, ln_b):
    n = xin.shape[0]
    tm = _row_tile(n, 128)

    def body(z4_ref, z6_ref, z7_ref, z8_ref, of_ref, ob_ref, hx_ref, x_ref, t_ref, mod_ref, gn_ref,
             pa_ref, pb_ref, wo_ref, lg_ref, lnb_ref,
             dr_ref, do_ref, dhx_ref, dz4_ref, dz678_ref, oa_o, obb_o, y_o, dya_o, dyb_o, dout_o, vec_ref):
        @pl.when(pl.program_id(0) == 0)
        def _():
            vec_ref[...] = jnp.zeros_like(vec_ref)

        gt = mod_ref[0:1, 2 * D:3 * D]
        gnv = gn_ref[...]
        o = of_ref[...] + ob_ref[...]
        rs = jnp.concatenate(
            [jnp.broadcast_to(lax.rsqrt(jnp.mean(jnp.square(o[:, h * DH:(h + 1) * DH]), axis=1, keepdims=True)
                                        + RMS_EPS), (tm, DH)) for h in range(NH)], axis=1)
        nrm = o * rs
        rn = nrm * gnv
        z4, z6, z7, z8 = z4_ref[...], z6_ref[...], z7_ref[...], z8_ref[...]
        s4, s6, s7, s8 = _sigmoid(z4), _sigmoid(z6), _sigmoid(z7), _sigmoid(z8)
        sg4, sg6 = z4 * s4, z6 * s6
        hxv = hx_ref[...]
        oa = (rn * sg4).astype(BF16)
        obb = (hxv * sg6).astype(BF16)
        ya = _dot(oa, pa_ref[...])
        yb = _dot(obb, pb_ref[...])
        y = (s7 * ya + s8 * yb).astype(BF16)
        out = _dot(y, wo_ref[...])
        xv = x_ref[...]
        rr = ALPHA * xv + gt * out
        mu = jnp.mean(rr, axis=1, keepdims=True)
        cen = rr - mu
        rstd = lax.rsqrt(jnp.mean(cen * cen, axis=1, keepdims=True) + LN_EPS)
        xhat = cen * rstd
        lg = lg_ref[...]
        err = xhat * lg + lnb_ref[...] - t_ref[...]
        loss_rows = jnp.sum(err * err, axis=1, keepdims=True)
        dxn = err * (1.0 / D)
        dxh = dxn * lg
        dr = rstd * (dxh - jnp.mean(dxh, axis=1, keepdims=True)
                     - xhat * jnp.mean(dxh * xhat, axis=1, keepdims=True))
        dout = (dr * gt).astype(BF16)
        dy = _dot_nt(dout, wo_ref[...])
        dya = (dy * s7).astype(BF16)
        dyb = (dy * s8).astype(BF16)
        doa = _dot_nt(dya, pa_ref[...])
        dob = _dot_nt(dyb, pb_ref[...])
        drn = doa * sg4
        dn = drn * gnv
        dnn = dn * nrm
        corr = jnp.concatenate(
            [jnp.broadcast_to(jnp.mean(dnn[:, h * DH:(h + 1) * DH], axis=1, keepdims=True), (tm, DH))
             for h in range(NH)], axis=1)
        dr_ref[...] = dr
        do_ref[...] = rs * (dn - nrm * corr)
        dhx_ref[...] = dob * sg6
        dz4_ref[...] = (doa * rn * _dsilu(z4, s4)).astype(BF16)
        dz678_ref[:, 0:D] = (dob * hxv * _dsilu(z6, s6)).astype(BF16)
        dz678_ref[:, D:2 * D] = (dy * ya * s7 * (1.0 - s7)).astype(BF16)
        dz678_ref[:, 2 * D:3 * D] = (dy * yb * s8 * (1.0 - s8)).astype(BF16)
        oa_o[...] = oa
        obb_o[...] = obb
        y_o[...] = y
        dya_o[...] = dya
        dyb_o[...] = dyb
        dout_o[...] = dout
        vec_ref[0:1, :] += jnp.sum(dr * out, axis=0, keepdims=True)
        vec_ref[1:2, :] += jnp.sum(dxn * xhat, axis=0, keepdims=True)
        vec_ref[2:3, :] += jnp.sum(dxn, axis=0, keepdims=True)
        vec_ref[3:4, :] += jnp.sum(drn * nrm, axis=0, keepdims=True)
        vec_ref[4:5, :] += jnp.broadcast_to(jnp.sum(loss_rows, axis=0, keepdims=True) * (0.5 / D), (1, D))

    def grp(g):
        return pl.BlockSpec((tm, D), lambda i: (i, g))

    tok = pl.BlockSpec((tm, D), lambda i: (i, 0))
    vec = pl.BlockSpec((1, D), lambda i: (0, 0))
    wsp = pl.BlockSpec((D, D), lambda i: (0, 0))
    return pl.pallas_call(
        body, name="merge", grid=(n // tm,),
        out_shape=[jax.ShapeDtypeStruct((n, D), F32)] * 3
        + [jax.ShapeDtypeStruct((n, D), BF16), jax.ShapeDtypeStruct((n, 3 * D), BF16)]
        + [jax.ShapeDtypeStruct((n, D), BF16)] * 6 + [jax.ShapeDtypeStruct((8, D), F32)],
        in_specs=[grp(4), grp(6), grp(7), grp(8), tok, tok, tok, tok, tok,
                  pl.BlockSpec((8, 3 * D), lambda i: (0, 0)), vec, wsp, wsp, wsp, vec, vec],
        out_specs=[tok, tok, tok, tok, pl.BlockSpec((tm, 3 * D), lambda i: (i, 0))] + [tok] * 6
        + [pl.BlockSpec((8, D), lambda i: (0, 0))],
    )(z, z, z, z, o_f, o_b, hx, xin, tgt, mod, gn, p_a, p_b, w_out, ln_g, ln_b)


def _wmod_grad(c_t, cctx_t, dmx_loc, dmc_loc, name):
    n = dmx_loc.shape[1]

    def body(ct_ref, cc_ref, dmx_ref, dmc_ref, o_ref):
        ct = ct_ref[...]
        sct = ct * _sigmoid(ct)
        cc = cc_ref[...]
        scc = cc * _sigmoid(cc)
        dmc = dmc_ref[0:1, :]
        for b in range(1, NDEV):
            dmc = dmc + dmc_ref[b:b + 1, :]
        acc = scc * dmc
        for b in range(NDEV):
            acc = acc + sct[:, b:b + 1] * dmx_ref[b:b + 1, :]
        o_ref[...] = acc

    return pl.pallas_call(body, name=name, out_shape=jax.ShapeDtypeStruct((D, n), F32))(c_t, cctx_t, dmx_loc, dmc_loc)


def _cctx_grad(dmc_sum, w_mod_full, cctx_row, name):
    def body(d_ref, w_ref, c_ref, o_ref):
        cv = c_ref[...]
        s = _sigmoid(cv)
        o_ref[...] = _dot_nt(d_ref[...].astype(BF16), w_ref[...]) * _dsilu(cv, s)

    return pl.pallas_call(body, name=name, out_shape=jax.ShapeDtypeStruct((8, D), F32))(dmc_sum, w_mod_full, cctx_row)


def _to_colmajor(t, rows):
    return t.reshape(rows, GRID_W, D).transpose(1, 0, 2).reshape(rows * GRID_W, D)


def _to_raster(t, rows):
    return t.reshape(GRID_W, rows, D).transpose(1, 0, 2).reshape(rows * GRID_W, D)


def _local_cols(t, me, width):
    return lax.dynamic_slice_in_dim(t, me * width, width, axis=t.ndim - 1)


def kernel(x, c, ctx, c_ctx, w_mod, b_mod, w_in, b_in, lb_logits, norm_a_g, conv_w, conv_b, w_r, b_r, w_i, b_i, lam, p_a, p_b, w_out, ln_g, ln_b, loss_target, m_c_ctx, m_w_mod, m_b_mod, m_w_in, m_b_in, m_lb_logits, m_norm_a_g, m_conv_w, m_conv_b, m_w_r, m_b_r, m_w_i, m_b_i, m_lam, m_p_a, m_p_b, m_w_out, m_ln_g, m_ln_b, v_c_ctx, v_w_mod, v_b_mod, v_w_in, v_b_in, v_lb_logits, v_norm_a_g, v_conv_w, v_conv_b, v_w_r, v_b_r, v_w_i, v_b_i, v_lam, v_p_a, v_p_b, v_w_out, v_ln_g, v_ln_b):
    me = _my_index()
    xs, cs, tgt = x[0], ctx[0], loss_target[0]
    t_len, c_len = xs.shape[0], cs.shape[0]
    rows = t_len // GRID_W
    wcols = w_in.shape[2]
    mcols = w_mod.shape[2]

    small = jnp.concatenate([lb_logits.reshape(4, DH), conv_w[0], b_r[0], b_i[0], lam[0], jnp.zeros((2, DH), F32),
                             c.reshape(8, DH)], axis=0)
    g_small, g_win, g_wmod, g_pa, g_pb, g_wo = _all_gather(
        [small, w_in[0].astype(BF16), w_mod[0].astype(BF16), p_a[0].astype(BF16), p_b[0].astype(BF16),
         w_out[0].astype(BF16)], "gather_params")

    def full_rows(lo, hi):
        return g_small[:, lo:hi, :].transpose(1, 0, 2).reshape(hi - lo, D)

    lbl_f, cw_f, br_f, bi_f, lam_f = full_rows(0, 4), full_rows(4, 8), full_rows(8, 10), full_rows(10, 12), full_rows(12, 14)
    c_all = g_small[:, 16:24, :].reshape(NDEV, D)
    w_in_f = g_win.transpose(1, 0, 2).reshape(D, NGRP * D)
    w_mod_f = g_wmod.transpose(1, 0, 2).reshape(D, 3 * D)
    p_a_f, p_b_f, w_out_f = g_pa.reshape(D, D), g_pb.reshape(D, D), g_wo.reshape(D, D)
    w_r_b, w_i_b = w_r[0].astype(BF16), w_i[0].astype(BF16)

    cc = jnp.concatenate([c.reshape(1, D), c_ctx.reshape(1, D), jnp.zeros((6, D), F32)], axis=0)
    lbl_p = jnp.concatenate([lbl_f.reshape(2, 2, D), jnp.zeros((2, 6, D), F32)], axis=1)
    mod, lb = _prep(cc, w_mod_f, b_mod, lbl_p)
    u_x = _modulate(xs, mod, 0, "modulate_x")
    u_c = _modulate(cs, mod, 1, "modulate_c")
    z_x = _mm_bias(u_x, w_in_f, b_in, "inproj_x")
    z_c = _mm_bias(u_c, w_in_f, b_in, "inproj_c")

    zero_s = jnp.zeros((NH, DH, DH), F32)
    zero_v = jnp.zeros((1, D), F32)
    gla = {}
    for d in (0, 1):
        _, ssc, sfc = _gla_fwd(z_c, lb, zero_s, d, f"gla_fwd_c{d}")
        o_d, ssx, _ = _gla_fwd(z_x, lb, sfc, d, f"gla_fwd_x{d}")
        gla[d] = (ssc, ssx, o_d)

    x5_c = z_c[:, 5 * D:6 * D]
    x5_x = _to_colmajor(z_x[:, 5 * D:6 * D], rows)
    cb2 = conv_b.reshape(1, D)
    lru = {}
    for d in (0, 1):
        prm = (cw_f, cb2, w_r_b[d], br_f[d:d + 1], w_i_b[d], bi_f[d:d + 1], lam_f[d:d + 1])
        h_c, hin_c, hfin_c = _lru_fwd(x5_c, c_len, *prm, zero_v, d, f"lru_fwd_c{d}")
        h_x, hin_x, _ = _lru_fwd(x5_x, rows, *prm, hfin_c, d, f"lru_fwd_x{d}")
        lru[d] = (prm, h_c, hin_c, h_x, hin_x)
    hx = _to_raster(lru[0][3] + lru[1][3], rows)

    gn = jnp.tile(norm_a_g.reshape(1, DH), (1, NH))
    (dr, do, dhx, dz4, dz678, oa, obb, yb16, dya, dyb, dout, mvec) = _merge(
        z_x, gla[0][2], gla[1][2], hx, xs, tgt, mod, gn, p_a_f, p_b_f, w_out_f, ln_g, ln_b)

    dhx_cm = _to_colmajor(dhx, rows)
    lru_dx_x = lru_dx_c = None
    lru_init = None
    for d in (0, 1):
        prm, h_c, hin_c, h_x, hin_x = lru[d]
        lru_dx_x, dwr, dwi, lvec, cg0 = _lru_bwd(x5_x, rows, *prm, h_x, hin_x, dhx_cm, zero_v, lru_dx_x, None, d,
                                                 f"lru_bwd_x{d}")
        lru_dx_c, dwr, dwi, lvec, _ = _lru_bwd(x5_c, c_len, *prm, h_c, hin_c, None, cg0, lru_dx_c, (dwr, dwi, lvec), d,
                                               f"lru_bwd_c{d}")
        lru[d] = (dwr, dwi, lvec)
    dz5_x = _to_raster(lru_dx_x, rows).astype(BF16)
    dz5_c = lru_dx_c.astype(BF16)

    gq_x = gv_x = gq_c = gv_c = None
    dzf_x, dzf_c, dlb = {}, {}, {}
    for d in (0, 1):
        ssc, ssx, _ = gla[d]
        gq_x, dzf_x[d], gv_x, dlb_x, ds0 = _gla_bwd(z_x, lb, ssx, do, zero_s, gq_x, gv_x, d, f"gla_bwd_x{d}")
        gq_c, dzf_c[d], gv_c, dlb_c, _ = _gla_bwd(z_c, lb, ssc, None, ds0, gq_c, gv_c, d, f"gla_bwd_c{d}")
        dlb[d] = dlb_x[0:1] + dlb_c[0:1]

    bf = lambda t: t.astype(BF16)
    dz_x = jnp.concatenate([bf(gq_x), bf(dzf_x[0]), bf(dzf_x[1]), bf(gv_x), dz4, dz5_x, dz678], axis=1)
    zc0 = jnp.zeros((c_len, D), BF16)
    dz_c = jnp.concatenate([bf(gq_c), bf(dzf_c[0]), bf(dzf_c[1]), bf(gv_c), zc0, dz5_c, zc0, zc0, zc0], axis=1)
    dwin_c, dbin_c = _mm_tn(u_c, dz_c, None, "dwin_c", with_colsum=True)
    dwin, dbin = _mm_tn(u_x, dz_x, dwin_c, "dwin_x", with_colsum=True, colsum_init=dbin_c)
    grad_x, xvec = _input_grad(dz_x, w_in_f, xs, dr, mod, 0, "input_grad_x")
    _, cvec = _input_grad(dz_c, w_in_f, cs, None, mod, 1, "input_grad_c")
    dpa = _mm_tn(oa, dya, None, "dpa")
    dpb = _mm_tn(obb, dyb, None, "dpb")
    dwo = _mm_tn(yb16, dout, None, "dwout")

    r_win, r_pa, r_pb, r_wo = _reduce_scatter_exchange([dwin, dpa, dpb, dwo], [1, 0, 0, 0], "exchange_wgrads")
    g_w_in, d_w_in, nm_w_in, nv_w_in = _sum_adamw(r_win, w_in, m_w_in, v_w_in, "update_w_in")
    g_p_a, d_p_a, nm_p_a, nv_p_a = _sum_adamw(r_pa, p_a, m_p_a, v_p_a, "update_p_a")
    g_p_b, d_p_b, nm_p_b, nv_p_b = _sum_adamw(r_pb, p_b, m_p_b, v_p_b, "update_p_b")
    g_w_out, d_w_out, nm_w_out, nv_w_out = _sum_adamw(r_wo, w_out, m_w_out, v_w_out, "update_w_out")

    dlb_rows = jnp.concatenate([dlb[0], dlb[1]], axis=0)
    pack = jnp.concatenate([
        xvec[0:1], xvec[1:2], mvec[0:1],
        cvec[0:1], cvec[1:2], jnp.zeros((1, D), F32),
        dbin.reshape(NGRP, D),
        mvec[3:4], mvec[1:2], mvec[2:3],
        lru[0][2][0:8], lru[1][2][0:3],
        lru[1][2][3:8],
        dlb_rows,
        mvec[4:5],
        jnp.zeros((3, D), F32)], axis=0)
    wr_pack = jnp.concatenate([lru[0][0], lru[1][0], lru[0][1], lru[1][1]], axis=0).reshape(4 * NH * DH, DH)
    g_pack, g_wri = _all_gather([pack, wr_pack], "gather_small_grads")
    tot = _sum_rows(g_pack, "sum_small_grads")

    loss = tot[36, 0]
    dmx = g_pack[:, 0:3, :].reshape(NDEV, 3 * D)
    dmc = g_pack[:, 3:6, :].reshape(NDEV, 3 * D)
    grad_w_mod = _wmod_grad(c_all.T, c_ctx.reshape(D, 1), _local_cols(dmx, me, mcols), _local_cols(dmc, me, mcols),
                            "grad_w_mod").reshape(1, D, mcols)
    grad_b_mod = (tot[0:3] + tot[3:6]).reshape(1, 3 * D)
    dmc_sum = jnp.concatenate([tot[3:6].reshape(1, 3 * D), jnp.zeros((7, 3 * D), F32)], axis=0)
    grad_c_ctx = _cctx_grad(dmc_sum, w_mod_f, cc[1:2], "grad_c_ctx")[0]
    grad_b_in = tot[6:15].reshape(1, NGRP * D)
    grad_norm_a_g = tot[15].reshape(NH, DH).sum(axis=0).reshape(1, DH)
    grad_ln_g, grad_ln_b = tot[16:17], tot[17:18]
    grad_conv_b = tot[21:22] + tot[29:30]
    grad_conv_w = _local_cols(tot[22:26] + tot[30:34], me, DH).reshape(1, 4, DH)
    grad_b_r = _local_cols(jnp.stack([tot[18], tot[26]]), me, DH).reshape(1, 2, DH)
    grad_b_i = _local_cols(jnp.stack([tot[19], tot[27]]), me, DH).reshape(1, 2, DH)
    grad_lam = _local_cols(jnp.stack([tot[20], tot[28]]), me, DH).reshape(1, 2, DH)
    lb_loc = _local_cols(lb[0:2], me, DH)
    dl0 = _local_cols(tot[34:36], me, DH) * lb_loc * (1.0 - lb_loc)
    grad_lb_logits = jnp.stack([dl0, -dl0])

    wri_shape = (1, 2, NH, DH, DH)
    w_ri = jnp.concatenate([w_r.reshape(-1, DH), w_i.reshape(-1, DH)], axis=0)
    m_ri = jnp.concatenate([m_w_r.reshape(-1, DH), m_w_i.reshape(-1, DH)], axis=0)
    v_ri = jnp.concatenate([v_w_r.reshape(-1, DH), v_w_i.reshape(-1, DH)], axis=0)
    g_ri, d_ri, nm_ri, nv_ri = _sum_adamw(g_wri, w_ri, m_ri, v_ri, "update_w_ri")
    half = 2 * NH * DH
    split_ri = lambda t: (t[:half].reshape(wri_shape), t[half:].reshape(wri_shape))
    (grad_w_r, grad_w_i), (d_w_r, d_w_i) = split_ri(g_ri), split_ri(d_ri)
    (nm_w_r, nm_w_i), (nv_w_r, nv_w_i) = split_ri(nm_ri), split_ri(nv_ri)

    d_w_mod, nm_w_mod, nv_w_mod = _adamw(grad_w_mod, w_mod, m_w_mod, v_w_mod, "update_w_mod")

    small_items = [
        (grad_c_ctx, c_ctx, m_c_ctx, v_c_ctx), (grad_b_mod, b_mod, m_b_mod, v_b_mod),
        (grad_b_in, b_in, m_b_in, v_b_in), (grad_lb_logits, lb_logits, m_lb_logits, v_lb_logits),
        (grad_norm_a_g, norm_a_g, m_norm_a_g, v_norm_a_g), (grad_conv_w, conv_w, m_conv_w, v_conv_w),
        (grad_conv_b, conv_b, m_conv_b, v_conv_b), (grad_b_r, b_r, m_b_r, v_b_r), (grad_b_i, b_i, m_b_i, v_b_i),
        (grad_lam, lam, m_lam, v_lam), (grad_ln_g, ln_g, m_ln_g, v_ln_g), (grad_ln_b, ln_b, m_ln_b, v_ln_b)]
    cat = lambda k: jnp.concatenate([it[k].reshape(-1, DH) for it in small_items], axis=0)
    sd, sm, sv = _adamw(cat(0), cat(1), cat(2), cat(3), "update_small")
    small_out = []
    off = 0
    for it in small_items:
        shp = it[1].shape
        nrow = it[1].size // DH
        small_out.append(tuple(t[off:off + nrow].reshape(shp) for t in (sd, sm, sv)))
        off += nrow
    (o_c_ctx, o_b_mod, o_b_in, o_lb, o_norm, o_conv_w, o_conv_b, o_b_r, o_b_i, o_lam, o_ln_g, o_ln_b) = small_out

    grads = [grad_c_ctx.reshape(c_ctx.shape), grad_w_mod, grad_b_mod, g_w_in, grad_b_in, grad_lb_logits, grad_norm_a_g,
             grad_conv_w, grad_conv_b, grad_w_r, grad_b_r, grad_w_i, grad_b_i, grad_lam, g_p_a, g_p_b, g_w_out,
             grad_ln_g, grad_ln_b]
    per_kind = []
    for k in range(3):
        per_kind.append([
            o_c_ctx[k], (d_w_mod, nm_w_mod, nv_w_mod)[k], o_b_mod[k], (d_w_in, nm_w_in, nv_w_in)[k], o_b_in[k], o_lb[k],
            o_norm[k], o_conv_w[k], o_conv_b[k], (d_w_r, nm_w_r, nv_w_r)[k], o_b_r[k], (d_w_i, nm_w_i, nv_w_i)[k],
            o_b_i[k], o_lam[k], (d_p_a, nm_p_a, nv_p_a)[k], (d_p_b, nm_p_b, nv_p_b)[k], (d_w_out, nm_w_out, nv_w_out)[k],
            o_ln_g[k], o_ln_b[k]])
    return (loss, grad_x.reshape(x.shape), *grads, *per_kind[0], *per_kind[1], *per_kind[2])
```

```python
import functools

import jax
import jax.numpy as jnp
from jax import lax
from jax.experimental import pallas as pl
from jax.experimental.pallas import tpu as pltpu

F32 = jnp.float32
BF16 = jnp.bfloat16

D = 1024
NH = 8
DH = 128
CHUNK = 64
GLA_HEADS_PER_STEP = 2
GRID_W = 64
NGRP = 9
NDEV = 8
RG_C = 8.0
ALPHA = 2.0 ** 0.25
LN_EPS = 1e-5
RMS_EPS = 1e-6
Q_SCALE = DH ** -0.5
ADAM_LR, ADAM_B1, ADAM_B2, ADAM_EPS, ADAM_WD, ADAM_STEP = 1e-3, 0.9, 0.999, 1e-8, 0.01, 10
ADAM_C1 = 1.0 / (1.0 - ADAM_B1 ** ADAM_STEP)
ADAM_C2 = 1.0 / (1.0 - ADAM_B2 ** ADAM_STEP)

ANY = pl.BlockSpec(memory_space=pl.ANY)


def _sigmoid(t):
    return 1.0 / (1.0 + jnp.exp(-t))


def _dsilu(t, s):
    return s * (1.0 + t * (1.0 - s))


def _dot(a, b):
    return jnp.dot(a, b, preferred_element_type=F32)


def _dot_nt(a, b):
    return lax.dot_general(a, b, (((1,), (1,)), ((), ())), preferred_element_type=F32)


def _dot_tn(a, b):
    return lax.dot_general(a, b, (((0,), (0,)), ((), ())), preferred_element_type=F32)


def _my_index():
    return 4 * lax.axis_index("x") + 2 * lax.axis_index("y") + lax.axis_index("c")


def _dev_tuple(j):
    return (j >> 2, (j >> 1) & 1, j & 1)


def _all_gather(shards, name):
    n = len(shards)

    def body(*refs):
        ins, outs = refs[:n], refs[n:2 * n]
        send_sems, recv_sems, loc_sems = refs[2 * n:]
        me = _my_index()
        for a in range(n):
            pltpu.make_async_copy(ins[a], outs[a].at[me], loc_sems.at[a]).start()
        for j in range(NDEV):
            @pl.when(me != j)
            def _():
                for a in range(n):
                    pltpu.make_async_remote_copy(
                        src_ref=ins[a], dst_ref=outs[a].at[me],
                        send_sem=send_sems.at[a * NDEV + j], recv_sem=recv_sems.at[a * NDEV + me],
                        device_id=_dev_tuple(j), device_id_type=pl.DeviceIdType.MESH).start()
        for j in range(NDEV):
            @pl.when(me != j)
            def _():
                for a in range(n):
                    pltpu.make_async_remote_copy(
                        src_ref=ins[a], dst_ref=outs[a].at[j],
                        send_sem=send_sems.at[a * NDEV + j], recv_sem=recv_sems.at[a * NDEV + j],
                        device_id=_dev_tuple(j), device_id_type=pl.DeviceIdType.MESH).wait()
        for a in range(n):
            pltpu.make_async_copy(ins[a], outs[a].at[me], loc_sems.at[a]).wait()

    return pl.pallas_call(
        body, name=name,
        out_shape=[jax.ShapeDtypeStruct((NDEV,) + s.shape, s.dtype) for s in shards],
        in_specs=[ANY] * n, out_specs=[ANY] * n,
        scratch_shapes=[pltpu.SemaphoreType.DMA((n * NDEV,)), pltpu.SemaphoreType.DMA((n * NDEV,)),
                        pltpu.SemaphoreType.DMA((n,))],
    )(*shards)


def _reduce_scatter_exchange(parts, splits, name):
    n = len(parts)
    pieces = []
    for a in range(n):
        r, c = parts[a].shape
        pieces.append((r // NDEV, c) if splits[a] == 0 else (r, c // NDEV))

    def piece_ref(ref, a, j):
        pr, pc = pieces[a]
        if splits[a] == 0:
            return ref.at[pl.ds(j * pr, pr), :]
        return ref.at[:, pl.ds(j * pc, pc)]

    def body(*refs):
        ins, outs = refs[:n], refs[n:2 * n]
        send_sems, recv_sems, loc_sems = refs[2 * n:]
        me = _my_index()
        for j in range(NDEV):
            @pl.when(me == j)
            def _():
                for a in range(n):
                    pltpu.make_async_copy(piece_ref(ins[a], a, j), outs[a].at[j], loc_sems.at[a]).start()

            @pl.when(me != j)
            def _():
                for a in range(n):
                    pltpu.make_async_remote_copy(
                        src_ref=piece_ref(ins[a], a, j), dst_ref=outs[a].at[me],
                        send_sem=send_sems.at[a * NDEV + j], recv_sem=recv_sems.at[a * NDEV + me],
                        device_id=_dev_tuple(j), device_id_type=pl.DeviceIdType.MESH).start()
        for j in range(NDEV):
            @pl.when(me == j)
            def _():
                for a in range(n):
                    pltpu.make_async_copy(piece_ref(ins[a], a, j), outs[a].at[j], loc_sems.at[a]).wait()

            @pl.when(me != j)
            def _():
                for a in range(n):
                    pltpu.make_async_remote_copy(
                        src_ref=piece_ref(ins[a], a, j), dst_ref=outs[a].at[j],
                        send_sem=send_sems.at[a * NDEV + j], recv_sem=recv_sems.at[a * NDEV + j],
                        device_id=_dev_tuple(j), device_id_type=pl.DeviceIdType.MESH).wait()

    return pl.pallas_call(
        body, name=name,
        out_shape=[jax.ShapeDtypeStruct((NDEV,) + pieces[a], parts[a].dtype) for a in range(n)],
        in_specs=[ANY] * n, out_specs=[ANY] * n,
        scratch_shapes=[pltpu.SemaphoreType.DMA((n * NDEV,)), pltpu.SemaphoreType.DMA((n * NDEV,)),
                        pltpu.SemaphoreType.DMA((n,))],
    )(*parts)


def _adam_math(g, w, m, v):
    m2 = ADAM_B1 * m + (1.0 - ADAM_B1) * g
    v2 = ADAM_B2 * v + (1.0 - ADAM_B2) * (g * g)
    delta = -ADAM_LR * ((m2 * ADAM_C1) / (jnp.sqrt(v2 * ADAM_C2) + ADAM_EPS) + ADAM_WD * w)
    return delta, m2, v2


def _row_tile(r, cap):
    t = min(r, cap)
    while r % t:
        t //= 2
    return t


def _adamw(g, w, m, v, name):
    shape = w.shape
    cols = shape[-1] if w.ndim >= 2 and shape[-1] % 128 == 0 else 128
    g2, w2, m2, v2 = (t.reshape(-1, cols) for t in (g, w, m, v))
    r = g2.shape[0]
    tr = _row_tile(r, 256)

    def body(g_ref, w_ref, m_ref, v_ref, d_ref, mo_ref, vo_ref):
        d, mm, vv = _adam_math(g_ref[...], w_ref[...], m_ref[...], v_ref[...])
        d_ref[...] = d
        mo_ref[...] = mm
        vo_ref[...] = vv

    spec = pl.BlockSpec((tr, cols), lambda i: (i, 0))
    outs = pl.pallas_call(
        body, name=name, grid=(r // tr,),
        out_shape=[jax.ShapeDtypeStruct((r, cols), F32)] * 3,
        in_specs=[spec] * 4, out_specs=[spec] * 3,
    )(g2, w2, m2, v2)
    return tuple(o.reshape(shape) for o in outs)


def _sum_adamw(parts, w, m, v, name):
    _, r, c = parts.shape
    shape = w.shape
    w2, m2, v2 = (t.reshape(r, c) for t in (w, m, v))
    tr = _row_tile(r, 128)

    def body(p_ref, w_ref, m_ref, v_ref, g_ref, d_ref, mo_ref, vo_ref):
        g = p_ref[0].astype(F32)
        for k in range(1, NDEV):
            g = g + p_ref[k].astype(F32)
        d, mm, vv = _adam_math(g, w_ref[...], m_ref[...], v_ref[...])
        g_ref[...] = g
        d_ref[...] = d
        mo_ref[...] = mm
        vo_ref[...] = vv

    spec = pl.BlockSpec((tr, c), lambda i: (i, 0))
    outs = pl.pallas_call(
        body, name=name, grid=(r // tr,),
        out_shape=[jax.ShapeDtypeStruct((r, c), F32)] * 4,
        in_specs=[pl.BlockSpec((NDEV, tr, c), lambda i: (0, i, 0))] + [spec] * 3, out_specs=[spec] * 4,
    )(parts, w2, m2, v2)
    return tuple(o.reshape(shape) for o in outs)


def _sum_rows(parts, name):
    _, r, c = parts.shape

    def body(p_ref, o_ref):
        g = p_ref[0]
        for k in range(1, NDEV):
            g = g + p_ref[k]
        o_ref[...] = g

    return pl.pallas_call(
        body, name=name, out_shape=jax.ShapeDtypeStruct((r, c), F32),
    )(parts)


def _prep(cc, w_mod_full, b_mod, lbl):
    def body(cc_ref, w_ref, b_ref, l_ref, mod_ref, lb_ref):
        t = cc_ref[...]
        s = (t * _sigmoid(t)).astype(BF16)
        mod_ref[...] = _dot(s, w_ref[...]) + b_ref[...]
        lb_ref[...] = _sigmoid(l_ref[0] - l_ref[1])

    return pl.pallas_call(
        body, name="prep",
        out_shape=[jax.ShapeDtypeStruct((8, 3 * D), F32), jax.ShapeDtypeStruct((8, D), F32)],
    )(cc, w_mod_full, b_mod, lbl)


def _modulate(xin, mod, row, name):
    n = xin.shape[0]
    tm = _row_tile(n, 512)

    def body(x_ref, mod_ref, u_ref):
        sh = mod_ref[row:row + 1, 0:D]
        sc = mod_ref[row:row + 1, D:2 * D]
        u_ref[...] = (x_ref[...] * (1.0 + sc) + sh).astype(BF16)

    return pl.pallas_call(
        body, name=name, grid=(n // tm,),
        out_shape=jax.ShapeDtypeStruct((n, D), BF16),
        in_specs=[pl.BlockSpec((tm, D), lambda i: (i, 0)), pl.BlockSpec((8, 3 * D), lambda i: (0, 0))],
        out_specs=pl.BlockSpec((tm, D), lambda i: (i, 0)),
    )(xin, mod)


def _mm_bias(a, b, bias, name):
    m, k = a.shape
    n = b.shape[1]
    tm = _row_tile(m, 512)
    tn = 1024

    def body(a_ref, b_ref, bias_ref, o_ref):
        o_ref[...] = _dot(a_ref[...], b_ref[...]) + bias_ref[...]

    return pl.pallas_call(
        body, name=name, grid=(n // tn, m // tm),
        out_shape=jax.ShapeDtypeStruct((m, n), F32),
        in_specs=[pl.BlockSpec((tm, k), lambda j, i: (i, 0)), pl.BlockSpec((k, tn), lambda j, i: (0, j)),
                  pl.BlockSpec((1, tn), lambda j, i: (0, j))],
        out_specs=pl.BlockSpec((tm, tn), lambda j, i: (i, j)),
    )(a, b, bias)


def _mm_tn(a, b, init, name, with_colsum=False, colsum_init=None, out_dtype=F32):
    m, ka = a.shape
    n = b.shape[1]
    tk = _row_tile(m, 512)
    tn = 1024
    nk = m // tk
    has_init = init is not None

    def body(*refs):
        a_ref, b_ref = refs[0], refs[1]
        pos = 2
        init_ref = cs_init_ref = None
        if has_init:
            init_ref = refs[pos]
            pos += 1
            if with_colsum:
                cs_init_ref = refs[pos]
                pos += 1
        o_ref = refs[pos]
        cs_ref = refs[pos + 1] if with_colsum else None
        acc = refs[-1]
        k = pl.program_id(1)

        @pl.when(k == 0)
        def _():
            if has_init:
                acc[...] = init_ref[...]
                if with_colsum:
                    cs_ref[...] = cs_init_ref[...]
            else:
                acc[...] = jnp.zeros_like(acc)
                if with_colsum:
                    cs_ref[...] = jnp.zeros_like(cs_ref)

        bv = b_ref[...]
        acc[...] += _dot_tn(a_ref[...], bv)
        if with_colsum:
            cs_ref[...] += jnp.sum(bv.astype(F32), axis=0, keepdims=True)

        @pl.when(k == nk - 1)
        def _():
            o_ref[...] = acc[...].astype(out_dtype)

    in_specs = [pl.BlockSpec((tk, ka), lambda j, k: (k, 0)), pl.BlockSpec((tk, tn), lambda j, k: (k, j))]
    args = [a, b]
    if has_init:
        in_specs.append(pl.BlockSpec((ka, tn), lambda j, k: (0, j)))
        args.append(init)
        if with_colsum:
            in_specs.append(pl.BlockSpec((1, tn), lambda j, k: (0, j)))
            args.append(colsum_init)
    out_shape = [jax.ShapeDtypeStruct((ka, n), out_dtype)]
    out_specs = [pl.BlockSpec((ka, tn), lambda j, k: (0, j))]
    if with_colsum:
        out_shape.append(jax.ShapeDtypeStruct((1, n), F32))
        out_specs.append(pl.BlockSpec((1, tn), lambda j, k: (0, j)))
    outs = pl.pallas_call(
        body, name=name, grid=(n // tn, nk), out_shape=out_shape, in_specs=in_specs, out_specs=out_specs,
        scratch_shapes=[pltpu.VMEM((ka, tn), F32)],
    )(*args)
    return outs if with_colsum else outs[0]


def _input_grad(dz, w_full, xin, dr, mod, row, name):
    m, n = dz.shape
    tm = _row_tile(m, 512)
    tk = 1024
    nk = n // tk
    has_dr = dr is not None

    def body(*refs):
        if has_dr:
            dz_ref, w_ref, x_ref, dr_ref, mod_ref, gx_ref, vec_ref, acc = refs
        else:
            dz_ref, w_ref, x_ref, mod_ref, vec_ref, acc = refs
        i, k = pl.program_id(0), pl.program_id(1)

        @pl.when(k == 0)
        def _():
            acc[...] = jnp.zeros_like(acc)

        @pl.when((i == 0) & (k == 0))
        def _():
            vec_ref[...] = jnp.zeros_like(vec_ref)

        acc[...] += _dot_nt(dz_ref[...], w_ref[...])

        @pl.when(k == nk - 1)
        def _():
            du = acc[...]
            xv = x_ref[...]
            if has_dr:
                sc = mod_ref[row:row + 1, D:2 * D]
                gx_ref[...] = ALPHA * dr_ref[...] + du * (1.0 + sc)
            vec_ref[0:1, :] += jnp.sum(du, axis=0, keepdims=True)
            vec_ref[1:2, :] += jnp.sum(du * xv, axis=0, keepdims=True)

    row_spec = pl.BlockSpec((tm, D), lambda i, k: (i, 0))
    in_specs = [pl.BlockSpec((tm, tk), lambda i, k: (i, k)), pl.BlockSpec((D, tk), lambda i, k: (0, k)), row_spec]
    args = [dz, w_full, xin]
    if has_dr:
        in_specs.append(row_spec)
        args.append(dr)
    in_specs.append(pl.BlockSpec((8, 3 * D), lambda i, k: (0, 0)))
    args.append(mod)
    out_shape, out_specs = [], []
    if has_dr:
        out_shape.append(jax.ShapeDtypeStruct((m, D), F32))
        out_specs.append(row_spec)
    out_shape.append(jax.ShapeDtypeStruct((8, D), F32))
    out_specs.append(pl.BlockSpec((8, D), lambda i, k: (0, 0)))
    outs = pl.pallas_call(
        body, name=name, grid=(m // tm, nk), out_shape=out_shape, in_specs=in_specs, out_specs=out_specs,
        scratch_shapes=[pltpu.VMEM((tm, D), F32)],
    )(*args)
    return outs if has_dr else (None, outs[0])


def _tri(reverse):
    r = lax.broadcasted_iota(jnp.int32, (CHUNK, CHUNK), 0)
    c = lax.broadcasted_iota(jnp.int32, (CHUNK, CHUNK), 1)
    return (c >= r) if reverse else (c <= r)


def _cum_f32(tri_b, t):
    hi = t.astype(BF16)
    r1 = t - hi.astype(F32)
    mid = r1.astype(BF16)
    lo = (r1 - mid.astype(F32)).astype(BF16)
    return _dot(tri_b, hi) + _dot(tri_b, mid) + _dot(tri_b, lo)


def _gla_features(zq, zf, lb):
    sq = _sigmoid(zq)
    q = zq * sq * Q_SCALE
    sf = _sigmoid(zf)
    f = lb + (1.0 - lb) * sf
    return q, sq, f, sf


def _gla_decays(f, tri_b, last):
    lf = jnp.log(f)
    g = _cum_f32(tri_b, lf)
    gl = g[last:last + 1, :]
    return g, gl


def _gla_block(n):
    return 256 if n % 256 == 0 else CHUNK


def _gla_fwd(z, lb, s0, d, name):
    n = z.shape[0]
    blk = _gla_block(n)
    nb, npb = n // blk, blk // CHUNK
    reverse = d == 1
    last = 0 if reverse else CHUNK - 1
    order = list(range(npb))[::-1] if reverse else list(range(npb))

    def bmap(i):
        return nb - 1 - i if reverse else i

    hp = GLA_HEADS_PER_STEP
    hw = hp * DH
    units = [(hh, cidx) for hh in range(hp) for cidx in order]

    def body(zq_ref, zf_ref, zv_ref, lb_ref, s0_ref, o_ref, ss_ref, sf_ref, st):
        i = pl.program_id(1)

        @pl.when(i == 0)
        def _():
            st[...] = s0_ref[...]

        mask = _tri(reverse)
        tri_b = jnp.where(mask, 1.0, 0.0).astype(BF16)
        feat = {}
        for u in units:
            hh, cidx = u
            rows, cols = pl.ds(cidx * CHUNK, CHUNK), pl.ds(hh * DH, DH)
            q, _, f, _ = _gla_features(zq_ref[rows, cols], zf_ref[rows, cols], lb_ref[d:d + 1, cols])
            feat[u] = (q, 1.0 - f, jnp.log(f), zv_ref[rows, cols].astype(BF16))
        dec = {u: _cum_f32(tri_b, feat[u][2]) for u in units}
        ops = {}
        for u in units:
            q, k, _, vb = feat[u]
            g = dec[u]
            gl = g[last:last + 1, :]
            ops[u] = ((q * jnp.exp(g)).astype(BF16), (k * jnp.exp(-g)).astype(BF16),
                      (k * jnp.exp(gl - g)).astype(BF16), jnp.exp(gl), vb)
        att = {u: jnp.where(mask, _dot_nt(ops[u][0], ops[u][1]), 0.0).astype(BF16) for u in units}
        upd = {u: _dot_tn(ops[u][4], ops[u][2]) for u in units}
        intra = {u: _dot(att[u], ops[u][4]) for u in units}
        s_in = {}
        for hh in range(hp):
            s = st[hh]
            for cidx in order:
                s_in[(hh, cidx)] = s
                s = s * ops[(hh, cidx)][3] + upd[(hh, cidx)]
            st[hh] = s
            sf_ref[hh] = s
        for u in units:
            hh, cidx = u
            rows, cols = pl.ds(cidx * CHUNK, CHUNK), pl.ds(hh * DH, DH)
            o_ref[rows, cols] = intra[u] + _dot_nt(ops[u][0], s_in[u].astype(BF16))
            ss_ref[hh, cidx] = s_in[u]

    def col(g):
        return lambda h, i: (bmap(i), g * (NH // hp) + h)

    return pl.pallas_call(
        body, name=name, grid=(NH // hp, nb),
        out_shape=[jax.ShapeDtypeStruct((n, D), F32), jax.ShapeDtypeStruct((NH, n // CHUNK, DH, DH), F32),
                   jax.ShapeDtypeStruct((NH, DH, DH), F32)],
        in_specs=[pl.BlockSpec((blk, hw), col(0)), pl.BlockSpec((blk, hw), col(1 + d)),
                  pl.BlockSpec((blk, hw), col(3)), pl.BlockSpec((8, hw), lambda h, i: (0, h)),
                  pl.BlockSpec((hp, DH, DH), lambda h, i: (h, 0, 0))],
        out_specs=[pl.BlockSpec((blk, hw), lambda h, i: (bmap(i), h)),
                   pl.BlockSpec((hp, npb, DH, DH), lambda h, i: (h, bmap(i), 0, 0)),
                   pl.BlockSpec((hp, DH, DH), lambda h, i: (h, 0, 0))],
        scratch_shapes=[pltpu.VMEM((hp, DH, DH), F32)],
    )(z, z, z, lb, s0)


def _gla_bwd(z, lb, s_start, do, ds_fin, acc_q, acc_v, d, name):
    n = z.shape[0]
    blk = _gla_block(n)
    nb, npb = n // blk, blk // CHUNK
    reverse = d == 1
    last = 0 if reverse else CHUNK - 1
    order = list(range(npb)) if reverse else list(range(npb))[::-1]
    has_do = do is not None
    has_acc = acc_q is not None
    hp = GLA_HEADS_PER_STEP
    hw = hp * DH
    units = [(hh, cidx) for hh in range(hp) for cidx in order]

    def bmap(i):
        return i if reverse else nb - 1 - i

    def body(*refs):
        zq_ref, zf_ref, zv_ref, lb_ref, ss_ref, dsf_ref = refs[:6]
        pos = 6
        do_ref = aq_ref = av_ref = None
        if has_do:
            do_ref = refs[pos]
            pos += 1
        if has_acc:
            aq_ref, av_ref = refs[pos], refs[pos + 1]
            pos += 2
        dzq_ref, dzf_ref, dzv_ref, dlb_ref, ds0_ref, dst = refs[pos:]
        i = pl.program_id(1)

        @pl.when(i == 0)
        def _():
            dst[...] = dsf_ref[...]
            dlb_ref[...] = jnp.zeros_like(dlb_ref)

        mask = _tri(reverse)
        tri_b = jnp.where(mask, 1.0, 0.0).astype(BF16)
        tri_t = jnp.where(_tri(not reverse), 1.0, 0.0).astype(BF16)

        def where(u):
            return pl.ds(u[1] * CHUNK, CHUNK), pl.ds(u[0] * DH, DH)

        feat = {}
        for u in units:
            rows, cols = where(u)
            zq, zf = zq_ref[rows, cols], zf_ref[rows, cols]
            lbv = lb_ref[d:d + 1, cols]
            q, sq, f, sf = _gla_features(zq, zf, lbv)
            feat[u] = dict(zq=zq, q=q, sq=sq, f=f, sf=sf, lbv=lbv, k=1.0 - f, vb=zv_ref[rows, cols].astype(BF16))
        dec = {u: _cum_f32(tri_b, jnp.log(feat[u]["f"])) for u in units}
        for u in units:
            w = feat[u]
            g = dec[u]
            gl = g[last:last + 1, :]
            w["eg"], w["egi"], w["ege"], w["egl"] = jnp.exp(g), jnp.exp(-g), jnp.exp(gl - g), jnp.exp(gl)
            w["qd"], w["ki"], w["ke"] = w["q"] * w["eg"], w["k"] * w["egi"], w["k"] * w["ege"]
            w["qdb"], w["kib"], w["keb"] = w["qd"].astype(BF16), w["ki"].astype(BF16), w["ke"].astype(BF16)
            w["s_in"] = ss_ref[u[0], u[1]]
        if has_do:
            for u in units:
                w = feat[u]
                rows, cols = where(u)
                w["dob"] = do_ref[rows, cols].astype(BF16)
            for u in units:
                w = feat[u]
                w["a"] = jnp.where(mask, _dot_nt(w["qdb"], w["kib"]), 0.0).astype(BF16)
                w["da"] = jnp.where(mask, _dot_nt(w["dob"], w["vb"]), 0.0).astype(BF16)
                w["m"] = _dot_tn(w["dob"], w["qdb"])
        for hh in range(hp):
            ds = dst[hh]
            for cidx in order:
                w = feat[(hh, cidx)]
                w["ds"] = ds
                ds = ds * w["egl"]
                if has_do:
                    ds = ds + w["m"]
            dst[hh] = ds
            ds0_ref[hh] = ds
        for u in units:
            w = feat[u]
            dsb = w["ds"].astype(BF16)
            w["dke"] = _dot(w["vb"], dsb)
            w["dv"] = _dot_nt(w["keb"], dsb)
            if has_do:
                w["dv"] = w["dv"] + _dot_tn(w["a"], w["dob"])
                w["dqd"] = _dot(w["da"], w["kib"]) + _dot(w["dob"], w["s_in"].astype(BF16))
                w["dki"] = _dot_tn(w["da"], w["qdb"])
        for u in units:
            w = feat[u]
            dkeke = w["dke"] * w["ke"]
            w["dgl"] = (w["egl"] * jnp.sum(w["s_in"] * w["ds"], axis=0, keepdims=True)
                        + jnp.sum(dkeke, axis=0, keepdims=True))
            dg = -dkeke
            dk = w["dke"] * w["ege"]
            if has_do:
                dg = dg + w["dqd"] * w["qd"] - w["dki"] * w["ki"]
                dk = dk + w["dki"] * w["egi"]
            w["dg"], w["dk"] = dg, dk
        dlf = {u: _cum_f32(tri_t, feat[u]["dg"]) for u in units}
        for u in units:
            w = feat[u]
            rows, cols = where(u)
            df = (dlf[u] + w["dgl"]) / w["f"] - w["dk"]
            sf = w["sf"]
            dzf_ref[rows, cols] = df * (1.0 - w["lbv"]) * sf * (1.0 - sf)
            dlb_ref[0:1, cols] += jnp.sum(df * (1.0 - sf), axis=0, keepdims=True)
            if has_do:
                dzq = w["dqd"] * w["eg"] * (Q_SCALE * _dsilu(w["zq"], w["sq"]))
            else:
                dzq = jnp.zeros((CHUNK, DH), F32)
            dv = w["dv"]
            if has_acc:
                dzq = dzq + aq_ref[rows, cols]
                dv = dv + av_ref[rows, cols]
            dzq_ref[rows, cols] = dzq
            dzv_ref[rows, cols] = dv

    def col(g):
        return lambda h, i: (bmap(i), g * (NH // hp) + h)

    tok = pl.BlockSpec((blk, hw), lambda h, i: (bmap(i), h))
    state = pl.BlockSpec((hp, DH, DH), lambda h, i: (h, 0, 0))
    in_specs = [pl.BlockSpec((blk, hw), col(0)), pl.BlockSpec((blk, hw), col(1 + d)), pl.BlockSpec((blk, hw), col(3)),
                pl.BlockSpec((8, hw), lambda h, i: (0, h)),
                pl.BlockSpec((hp, npb, DH, DH), lambda h, i: (h, bmap(i), 0, 0)), state]
    args = [z, z, z, lb, s_start, ds_fin]
    if has_do:
        in_specs.append(tok)
        args.append(do)
    if has_acc:
        in_specs += [tok, tok]
        args += [acc_q, acc_v]
    return pl.pallas_call(
        body, name=name, grid=(NH // hp, nb),
        out_shape=[jax.ShapeDtypeStruct((n, D), F32)] * 3 + [jax.ShapeDtypeStruct((8, D), F32),
                                                            jax.ShapeDtypeStruct((NH, DH, DH), F32)],
        in_specs=in_specs,
        out_specs=[tok, tok, tok, pl.BlockSpec((8, hw), lambda h, i: (0, h)), state],
        scratch_shapes=[pltpu.VMEM((hp, DH, DH), F32)],
    )(*args)


def _shift(t, s, fill, down):
    n = t.shape[0]
    rows = lax.broadcasted_iota(jnp.int32, t.shape, 0)
    if down:
        return jnp.where(rows >= s, pltpu.roll(t, s, 0), fill)
    return jnp.where(rows < n - s, pltpu.roll(t, n - s, 0), fill)


def _chain_scan(a, b, down):
    n = a.shape[0]
    s = 1
    while s < n:
        b = b + a * _shift(b, s, 0.0, down)
        a = a * _shift(a, s, 1.0, down)
        s *= 2
    return a, b


def _conv(xv, cw, cb):
    return (cb + cw[0:1, :] * _shift(xv, 1, 0.0, True) + cw[1:2, :] * xv
            + cw[2:3, :] * _shift(xv, 1, 0.0, False) + cw[3:4, :] * _shift(xv, 2, 0.0, False))


def _neg_expm1(t):
    series = -t * (1.0 + t * (0.5 + t * (1.0 / 6.0 + t * (1.0 / 24.0 + t * (1.0 / 120.0)))))
    return jnp.where(t > -0.1, series, 1.0 - jnp.exp(t))


def _block_diag(tb, w_ref):
    return jnp.concatenate([_dot(tb[:, g * DH:(g + 1) * DH], w_ref[g]) for g in range(NH)], axis=1)


def _block_diag_nt(tb, w_ref):
    return jnp.concatenate([_dot_nt(tb[:, g * DH:(g + 1) * DH], w_ref[g]) for g in range(NH)], axis=1)


def _lru_gates(xc, wr_ref, br, wi_ref, bi, lam):
    xcb = xc.astype(BF16)
    r = _sigmoid(_block_diag(xcb, wr_ref) + br)
    gi = _sigmoid(_block_diag(xcb, wi_ref) + bi)
    sp = jnp.maximum(-lam, 0.0) + jnp.log(1.0 + jnp.exp(-jnp.abs(lam)))
    la = -RG_C * r * sp
    a = jnp.exp(la)
    mult = jnp.sqrt(_neg_expm1(2.0 * la))
    return xcb, r, gi, sp, a, mult


def _lru_fwd(xin, blk, cw, cb, wr, br, wi, bi, lam, h0, d, name):
    n = xin.shape[0]
    nb = n // blk
    reverse = d == 1
    down = not reverse
    last = 0 if reverse else blk - 1

    def bmap(i):
        return nb - 1 - i if reverse else i

    def body(x_ref, cw_ref, cb_ref, wr_ref, br_ref, wi_ref, bi_ref, lam_ref, h0_ref, h_ref, hin_ref, hfin_ref, carry):
        i = pl.program_id(0)

        @pl.when(i == 0)
        def _():
            carry[...] = h0_ref[...]

        xc = _conv(x_ref[...], cw_ref[...], cb_ref[...])
        _, _, gi, _, a, mult = _lru_gates(xc, wr_ref, br_ref[...], wi_ref, bi_ref[...], lam_ref[...])
        aa, bb = _chain_scan(a, mult * gi * xc, down)
        hin = carry[...]
        h = bb + aa * hin
        h_ref[...] = h
        hin_ref[0] = hin
        carry[...] = h[last:last + 1, :]
        hfin_ref[...] = h[last:last + 1, :]

    vec = pl.BlockSpec((1, D), lambda i: (0, 0))
    wsp = pl.BlockSpec((NH, DH, DH), lambda i: (0, 0, 0))
    return pl.pallas_call(
        body, name=name, grid=(nb,),
        out_shape=[jax.ShapeDtypeStruct((n, D), F32), jax.ShapeDtypeStruct((nb, 1, D), F32),
                   jax.ShapeDtypeStruct((1, D), F32)],
        in_specs=[pl.BlockSpec((blk, D), lambda i: (bmap(i), 0)), pl.BlockSpec((4, D), lambda i: (0, 0)), vec,
                  wsp, vec, wsp, vec, vec, vec],
        out_specs=[pl.BlockSpec((blk, D), lambda i: (bmap(i), 0)), pl.BlockSpec((1, 1, D), lambda i: (bmap(i), 0, 0)), vec],
        scratch_shapes=[pltpu.VMEM((1, D), F32)],
    )(xin, cw, cb, wr, br, wi, bi, lam, h0)


def _lru_bwd(xin, blk, cw, cb, wr, br, wi, bi, lam, h, hin, dh, cg_fin, acc_dx, init, d, name):
    n = xin.shape[0]
    nb = n // blk
    reverse = d == 1
    down = not reverse
    first = blk - 1 if reverse else 0
    has_dh = dh is not None
    has_acc = acc_dx is not None
    has_init = init is not None

    def bmap(i):
        return i if reverse else nb - 1 - i

    def body(*refs):
        (x_ref, cw_ref, cb_ref, wr_ref, br_ref, wi_ref, bi_ref, lam_ref, h_ref, hin_ref, cgf_ref) = refs[:11]
        pos = 11
        dh_ref = acc_ref = None
        iwr_ref = iwi_ref = ivec_ref = None
        if has_dh:
            dh_ref = refs[pos]
            pos += 1
        if has_acc:
            acc_ref = refs[pos]
            pos += 1
        if has_init:
            iwr_ref, iwi_ref, ivec_ref = refs[pos:pos + 3]
            pos += 3
        dx_ref, dwr_ref, dwi_ref, vec_ref, cg0_ref, carry = refs[pos:]
        i = pl.program_id(0)

        @pl.when(i == 0)
        def _():
            carry[...] = cgf_ref[...]
            if has_init:
                dwr_ref[...] = iwr_ref[...]
                dwi_ref[...] = iwi_ref[...]
                vec_ref[...] = ivec_ref[...]
            else:
                dwr_ref[...] = jnp.zeros_like(dwr_ref)
                dwi_ref[...] = jnp.zeros_like(dwi_ref)
                vec_ref[...] = jnp.zeros_like(vec_ref)

        xv = x_ref[...]
        cwv = cw_ref[...]
        lam_v = lam_ref[...]
        xc = _conv(xv, cwv, cb_ref[...])
        xcb, r, gi, sp, a, mult = _lru_gates(xc, wr_ref, br_ref[...], wi_ref, bi_ref[...], lam_v)
        hv = h_ref[...]
        hprev = _shift(hv, 1, hin_ref[0], down)
        a_next = _shift(a, 1, 1.0, not down)
        dhv = dh_ref[...] if has_dh else jnp.zeros_like(a)
        aa, bb = _chain_scan(a_next, dhv, not down)
        e = bb + aa * carry[...]
        carry[...] = a[first:first + 1, :] * e[first:first + 1, :]
        cg0_ref[...] = a[first:first + 1, :] * e[first:first + 1, :]
        da = e * hprev
        db = e
        ix = gi * xc
        dmult = db * ix
        dgi = db * mult * xc
        dxc = db * mult * gi
        dla = da * a - dmult * (a * a) / mult
        dr = dla * (-RG_C * sp)
        sneg = 1.0 - _sigmoid(lam_v)
        dlam = jnp.sum(dla * r, axis=0, keepdims=True) * (RG_C * sneg)
        dpr = (dr * r * (1.0 - r))
        dpi = (dgi * gi * (1.0 - gi))
        dprb, dpib = dpr.astype(BF16), dpi.astype(BF16)
        dxc = dxc + _block_diag_nt(dprb, wr_ref) + _block_diag_nt(dpib, wi_ref)
        for g in range(NH):
            sl = slice(g * DH, (g + 1) * DH)
            dwr_ref[g] += _dot_tn(xcb[:, sl], dprb[:, sl])
            dwi_ref[g] += _dot_tn(xcb[:, sl], dpib[:, sl])
        dx = (cwv[0:1, :] * _shift(dxc, 1, 0.0, False) + cwv[1:2, :] * dxc
              + cwv[2:3, :] * _shift(dxc, 1, 0.0, True) + cwv[3:4, :] * _shift(dxc, 2, 0.0, True))
        if has_acc:
            dx = dx + acc_ref[...]
        dx_ref[...] = dx
        vec_ref[0:1, :] += jnp.sum(dpr, axis=0, keepdims=True)
        vec_ref[1:2, :] += jnp.sum(dpi, axis=0, keepdims=True)
        vec_ref[2:3, :] += dlam
        vec_ref[3:4, :] += jnp.sum(dxc, axis=0, keepdims=True)
        vec_ref[4:5, :] += jnp.sum(dxc * _shift(xv, 1, 0.0, True), axis=0, keepdims=True)
        vec_ref[5:6, :] += jnp.sum(dxc * xv, axis=0, keepdims=True)
        vec_ref[6:7, :] += jnp.sum(dxc * _shift(xv, 1, 0.0, False), axis=0, keepdims=True)
        vec_ref[7:8, :] += jnp.sum(dxc * _shift(xv, 2, 0.0, False), axis=0, keepdims=True)

    vec = pl.BlockSpec((1, D), lambda i: (0, 0))
    wsp = pl.BlockSpec((NH, DH, DH), lambda i: (0, 0, 0))
    tok = pl.BlockSpec((blk, D), lambda i: (bmap(i), 0))
    vec16 = pl.BlockSpec((16, D), lambda i: (0, 0))
    in_specs = [tok, pl.BlockSpec((4, D), lambda i: (0, 0)), vec, wsp, vec, wsp, vec, vec, tok,
                pl.BlockSpec((1, 1, D), lambda i: (bmap(i), 0, 0)), vec]
    args = [xin, cw, cb, wr, br, wi, bi, lam, h, hin, cg_fin]
    if has_dh:
        in_specs.append(tok)
        args.append(dh)
    if has_acc:
        in_specs.append(tok)
        args.append(acc_dx)
    if has_init:
        in_specs += [wsp, wsp, vec16]
        args += list(init)
    return pl.pallas_call(
        body, name=name, grid=(nb,),
        out_shape=[jax.ShapeDtypeStruct((n, D), F32), jax.ShapeDtypeStruct((NH, DH, DH), F32),
                   jax.ShapeDtypeStruct((NH, DH, DH), F32), jax.ShapeDtypeStruct((16, D), F32),
                   jax.ShapeDtypeStruct((1, D), F32)],
        in_specs=in_specs, out_specs=[tok, wsp, wsp, vec16, vec],
        scratch_shapes=[pltpu.VMEM((1, D), F32)],
    )(*args)


def _merge(z, o_f, o_b, hx, xin, tgt, mod, gn, p_a, p_b, w_out, ln_g, ln_b):
    n = xin.shape[0]
    tm = _row_tile(n, 128)

    def body(z4_ref, z6_ref, z7_ref, z8_ref, of_ref, ob_ref, hx_ref, x_ref, t_ref, mod_ref, gn_ref,
             pa_ref, pb_ref, wo_ref, lg_ref, lnb_ref,
             dr_ref, do_ref, dhx_ref, dz4_ref, dz678_ref, oa_o, obb_o, y_o, dya_o, dyb_o, dout_o, vec_ref):
        @pl.when(pl.program_id(0) == 0)
        def _():
            vec_ref[...] = jnp.zeros_like(vec_ref)

        gt = mod_ref[0:1, 2 * D:3 * D]
        gnv = gn_ref[...]
        o = of_ref[...] + ob_ref[...]
        rs = jnp.concatenate(
            [jnp.broadcast_to(lax.rsqrt(jnp.mean(jnp.square(o[:, h * DH:(h + 1) * DH]), axis=1, keepdims=True)
                                        + RMS_EPS), (tm, DH)) for h in range(NH)], axis=1)
        nrm = o * rs
        rn = nrm * gnv
        z4, z6, z7, z8 = z4_ref[...], z6_ref[...], z7_ref[...], z8_ref[...]
        s4, s6, s7, s8 = _sigmoid(z4), _sigmoid(z6), _sigmoid(z7), _sigmoid(z8)
        sg4, sg6 = z4 * s4, z6 * s6
        hxv = hx_ref[...]
        oa = (rn * sg4).astype(BF16)
        obb = (hxv * sg6).astype(BF16)
        ya = _dot(oa, pa_ref[...])
        yb = _dot(obb, pb_ref[...])
        y = (s7 * ya + s8 * yb).astype(BF16)
        out = _dot(y, wo_ref[...])
        xv = x_ref[...]
        rr = ALPHA * xv + gt * out
        mu = jnp.mean(rr, axis=1, keepdims=True)
        cen = rr - mu
        rstd = lax.rsqrt(jnp.mean(cen * cen, axis=1, keepdims=True) + LN_EPS)
        xhat = cen * rstd
        lg = lg_ref[...]
        err = xhat * lg + lnb_ref[...] - t_ref[...]
        loss_rows = jnp.sum(err * err, axis=1, keepdims=True)
        dxn = err * (1.0 / D)
        dxh = dxn * lg
        dr = rstd * (dxh - jnp.mean(dxh, axis=1, keepdims=True)
                     - xhat * jnp.mean(dxh * xhat, axis=1, keepdims=True))
        dout = (dr * gt).astype(BF16)
        dy = _dot_nt(dout, wo_ref[...])
        dya = (dy * s7).astype(BF16)
        dyb = (dy * s8).astype(BF16)
        doa = _dot_nt(dya, pa_ref[...])
        dob = _dot_nt(dyb, pb_ref[...])
        drn = doa * sg4
        dn = drn * gnv
        dnn = dn * nrm
        corr = jnp.concatenate(
            [jnp.broadcast_to(jnp.mean(dnn[:, h * DH:(h + 1) * DH], axis=1, keepdims=True), (tm, DH))
             for h in range(NH)], axis=1)
        dr_ref[...] = dr
        do_ref[...] = rs * (dn - nrm * corr)
        dhx_ref[...] = dob * sg6
        dz4_ref[...] = (doa * rn * _dsilu(z4, s4)).astype(BF16)
        dz678_ref[:, 0:D] = (dob * hxv * _dsilu(z6, s6)).astype(BF16)
        dz678_ref[:, D:2 * D] = (dy * ya * s7 * (1.0 - s7)).astype(BF16)
        dz678_ref[:, 2 * D:3 * D] = (dy * yb * s8 * (1.0 - s8)).astype(BF16)
        oa_o[...] = oa
        obb_o[...] = obb
        y_o[...] = y
        dya_o[...] = dya
        dyb_o[...] = dyb
        dout_o[...] = dout
        vec_ref[0:1, :] += jnp.sum(dr * out, axis=0, keepdims=True)
        vec_ref[1:2, :] += jnp.sum(dxn * xhat, axis=0, keepdims=True)
        vec_ref[2:3, :] += jnp.sum(dxn, axis=0, keepdims=True)
        vec_ref[3:4, :] += jnp.sum(drn * nrm, axis=0, keepdims=True)
        vec_ref[4:5, :] += jnp.broadcast_to(jnp.sum(loss_rows, axis=0, keepdims=True) * (0.5 / D), (1, D))

    def grp(g):
        return pl.BlockSpec((tm, D), lambda i: (i, g))

    tok = pl.BlockSpec((tm, D), lambda i: (i, 0))
    vec = pl.BlockSpec((1, D), lambda i: (0, 0))
    wsp = pl.BlockSpec((D, D), lambda i: (0, 0))
    return pl.pallas_call(
        body, name="merge", grid=(n // tm,),
        out_shape=[jax.ShapeDtypeStruct((n, D), F32)] * 3
        + [jax.ShapeDtypeStruct((n, D), BF16), jax.ShapeDtypeStruct((n, 3 * D), BF16)]
        + [jax.ShapeDtypeStruct((n, D), BF16)] * 6 + [jax.ShapeDtypeStruct((8, D), F32)],
        in_specs=[grp(4), grp(6), grp(7), grp(8), tok, tok, tok, tok, tok,
                  pl.BlockSpec((8, 3 * D), lambda i: (0, 0)), vec, wsp, wsp, wsp, vec, vec],
        out_specs=[tok, tok, tok, tok, pl.BlockSpec((tm, 3 * D), lambda i: (i, 0))] + [tok] * 6
        + [pl.BlockSpec((8, D), lambda i: (0, 0))],
    )(z, z, z, z, o_f, o_b, hx, xin, tgt, mod, gn, p_a, p_b, w_out, ln_g, ln_b)


def _wmod_grad(c_t, cctx_t, dmx_loc, dmc_loc, name):
    n = dmx_loc.shape[1]

    def body(ct_ref, cc_ref, dmx_ref, dmc_ref, o_ref):
        ct = ct_ref[...]
        sct = ct * _sigmoid(ct)
        cc = cc_ref[...]
        scc = cc * _sigmoid(cc)
        dmc = dmc_ref[0:1, :]
        for b in range(1, NDEV):
            dmc = dmc + dmc_ref[b:b + 1, :]
        acc = scc * dmc
        for b in range(NDEV):
            acc = acc + sct[:, b:b + 1] * dmx_ref[b:b + 1, :]
        o_ref[...] = acc

    return pl.pallas_call(body, name=name, out_shape=jax.ShapeDtypeStruct((D, n), F32))(c_t, cctx_t, dmx_loc, dmc_loc)


def _cctx_grad(dmc_sum, w_mod_full, cctx_row, name):
    def body(d_ref, w_ref, c_ref, o_ref):
        cv = c_ref[...]
        s = _sigmoid(cv)
        o_ref[...] = _dot_nt(d_ref[...].astype(BF16), w_ref[...]) * _dsilu(cv, s)

    return pl.pallas_call(body, name=name, out_shape=jax.ShapeDtypeStruct((8, D), F32))(dmc_sum, w_mod_full, cctx_row)


def _to_colmajor(t, rows):
    return t.reshape(rows, GRID_W, D).transpose(1, 0, 2).reshape(rows * GRID_W, D)


def _to_raster(t, rows):
    return t.reshape(GRID_W, rows, D).transpose(1, 0, 2).reshape(rows * GRID_W, D)


def _local_cols(t, me, width):
    return lax.dynamic_slice_in_dim(t, me * width, width, axis=t.ndim - 1)


def kernel(x, c, ctx, c_ctx, w_mod, b_mod, w_in, b_in, lb_logits, norm_a_g, conv_w, conv_b, w_r, b_r, w_i, b_i, lam, p_a, p_b, w_out, ln_g, ln_b, loss_target, m_c_ctx, m_w_mod, m_b_mod, m_w_in, m_b_in, m_lb_logits, m_norm_a_g, m_conv_w, m_conv_b, m_w_r, m_b_r, m_w_i, m_b_i, m_lam, m_p_a, m_p_b, m_w_out, m_ln_g, m_ln_b, v_c_ctx, v_w_mod, v_b_mod, v_w_in, v_b_in, v_lb_logits, v_norm_a_g, v_conv_w, v_conv_b, v_w_r, v_b_r, v_w_i, v_b_i, v_lam, v_p_a, v_p_b, v_w_out, v_ln_g, v_ln_b):
    me = _my_index()
    xs, cs, tgt = x[0], ctx[0], loss_target[0]
    t_len, c_len = xs.shape[0], cs.shape[0]
    rows = t_len // GRID_W
    wcols = w_in.shape[2]
    mcols = w_mod.shape[2]

    small = jnp.concatenate([lb_logits.reshape(4, DH), conv_w[0], b_r[0], b_i[0], lam[0], jnp.zeros((2, DH), F32),
                             c.reshape(8, DH)], axis=0)
    g_small, g_win, g_wmod, g_pa, g_pb, g_wo = _all_gather(
        [small, w_in[0].astype(BF16), w_mod[0].astype(BF16), p_a[0].astype(BF16), p_b[0].astype(BF16),
         w_out[0].astype(BF16)], "gather_params")

    def full_rows(lo, hi):
        return g_small[:, lo:hi, :].transpose(1, 0, 2).reshape(hi - lo, D)

    lbl_f, cw_f, br_f, bi_f, lam_f = full_rows(0, 4), full_rows(4, 8), full_rows(8, 10), full_rows(10, 12), full_rows(12, 14)
    c_all = g_small[:, 16:24, :].reshape(NDEV, D)
    w_in_f = g_win.transpose(1, 0, 2).reshape(D, NGRP * D)
    w_mod_f = g_wmod.transpose(1, 0, 2).reshape(D, 3 * D)
    p_a_f, p_b_f, w_out_f = g_pa.reshape(D, D), g_pb.reshape(D, D), g_wo.reshape(D, D)
    w_r_b, w_i_b = w_r[0].astype(BF16), w_i[0].astype(BF16)

    cc = jnp.concatenate([c.reshape(1, D), c_ctx.reshape(1, D), jnp.zeros((6, D), F32)], axis=0)
    lbl_p = jnp.concatenate([lbl_f.reshape(2, 2, D), jnp.zeros((2, 6, D), F32)], axis=1)
    mod, lb = _prep(cc, w_mod_f, b_mod, lbl_p)
    u_x = _modulate(xs, mod, 0, "modulate_x")
    u_c = _modulate(cs, mod, 1, "modulate_c")
    z_x = _mm_bias(u_x, w_in_f, b_in, "inproj_x")
    z_c = _mm_bias(u_c, w_in_f, b_in, "inproj_c")

    zero_s = jnp.zeros((NH, DH, DH), F32)
    zero_v = jnp.zeros((1, D), F32)
    gla = {}
    for d in (0, 1):
        _, ssc, sfc = _gla_fwd(z_c, lb, zero_s, d, f"gla_fwd_c{d}")
        o_d, ssx, _ = _gla_fwd(z_x, lb, sfc, d, f"gla_fwd_x{d}")
        gla[d] = (ssc, ssx, o_d)

    x5_c = z_c[:, 5 * D:6 * D]
    x5_x = _to_colmajor(z_x[:, 5 * D:6 * D], rows)
    cb2 = conv_b.reshape(1, D)
    lru = {}
    for d in (0, 1):
        prm = (cw_f, cb2, w_r_b[d], br_f[d:d + 1], w_i_b[d], bi_f[d:d + 1], lam_f[d:d + 1])
        h_c, hin_c, hfin_c = _lru_fwd(x5_c, c_len, *prm, zero_v, d, f"lru_fwd_c{d}")
        h_x, hin_x, _ = _lru_fwd(x5_x, rows, *prm, hfin_c, d, f"lru_fwd_x{d}")
        lru[d] = (prm, h_c, hin_c, h_x, hin_x)
    hx = _to_raster(lru[0][3] + lru[1][3], rows)

    gn = jnp.tile(norm_a_g.reshape(1, DH), (1, NH))
    (dr, do, dhx, dz4, dz678, oa, obb, yb16, dya, dyb, dout, mvec) = _merge(
        z_x, gla[0][2], gla[1][2], hx, xs, tgt, mod, gn, p_a_f, p_b_f, w_out_f, ln_g, ln_b)

    dhx_cm = _to_colmajor(dhx, rows)
    lru_dx_x = lru_dx_c = None
    lru_init = None
    for d in (0, 1):
        prm, h_c, hin_c, h_x, hin_x = lru[d]
        lru_dx_x, dwr, dwi, lvec, cg0 = _lru_bwd(x5_x, rows, *prm, h_x, hin_x, dhx_cm, zero_v, lru_dx_x, None, d,
                                                 f"lru_bwd_x{d}")
        lru_dx_c, dwr, dwi, lvec, _ = _lru_bwd(x5_c, c_len, *prm, h_c, hin_c, None, cg0, lru_dx_c, (dwr, dwi, lvec), d,
                                               f"lru_bwd_c{d}")
        lru[d] = (dwr, dwi, lvec)
    dz5_x = _to_raster(lru_dx_x, rows).astype(BF16)
    dz5_c = lru_dx_c.astype(BF16)

    gq_x = gv_x = gq_c = gv_c = None
    dzf_x, dzf_c, dlb = {}, {}, {}
    for d in (0, 1):
        ssc, ssx, _ = gla[d]
        gq_x, dzf_x[d], gv_x, dlb_x, ds0 = _gla_bwd(z_x, lb, ssx, do, zero_s, gq_x, gv_x, d, f"gla_bwd_x{d}")
        gq_c, dzf_c[d], gv_c, dlb_c, _ = _gla_bwd(z_c, lb, ssc, None, ds0, gq_c, gv_c, d, f"gla_bwd_c{d}")
        dlb[d] = dlb_x[0:1] + dlb_c[0:1]

    bf = lambda t: t.astype(BF16)
    dz_x = jnp.concatenate([bf(gq_x), bf(dzf_x[0]), bf(dzf_x[1]), bf(gv_x), dz4, dz5_x, dz678], axis=1)
    zc0 = jnp.zeros((c_len, D), BF16)
    dz_c = jnp.concatenate([bf(gq_c), bf(dzf_c[0]), bf(dzf_c[1]), bf(gv_c), zc0, dz5_c, zc0, zc0, zc0], axis=1)
    dwin_c, dbin_c = _mm_tn(u_c, dz_c, None, "dwin_c", with_colsum=True)
    dwin, dbin = _mm_tn(u_x, dz_x, dwin_c, "dwin_x", with_colsum=True, colsum_init=dbin_c, out_dtype=BF16)
    grad_x, xvec = _input_grad(dz_x, w_in_f, xs, dr, mod, 0, "input_grad_x")
    _, cvec = _input_grad(dz_c, w_in_f, cs, None, mod, 1, "input_grad_c")
    dpa = _mm_tn(oa, dya, None, "dpa", out_dtype=BF16)
    dpb = _mm_tn(obb, dyb, None, "dpb", out_dtype=BF16)
    dwo = _mm_tn(yb16, dout, None, "dwout", out_dtype=BF16)

    wr_pack = jnp.concatenate([lru[0][0], lru[1][0], lru[0][1], lru[1][1]], axis=0).reshape(4 * NH * DH, DH)
    r_win, r_pa, r_pb, r_wo, r_wri = _reduce_scatter_exchange(
        [dwin, dpa, dpb, dwo, wr_pack], [1, 0, 0, 0, 0], "exchange_wgrads")
    wri_piece = _sum_rows(r_wri, "sum_w_ri_piece")
    g_w_in, d_w_in, nm_w_in, nv_w_in = _sum_adamw(r_win, w_in, m_w_in, v_w_in, "update_w_in")
    g_p_a, d_p_a, nm_p_a, nv_p_a = _sum_adamw(r_pa, p_a, m_p_a, v_p_a, "update_p_a")
    g_p_b, d_p_b, nm_p_b, nv_p_b = _sum_adamw(r_pb, p_b, m_p_b, v_p_b, "update_p_b")
    g_w_out, d_w_out, nm_w_out, nv_w_out = _sum_adamw(r_wo, w_out, m_w_out, v_w_out, "update_w_out")

    dlb_rows = jnp.concatenate([dlb[0], dlb[1]], axis=0)
    pack = jnp.concatenate([
        xvec[0:1], xvec[1:2], mvec[0:1],
        cvec[0:1], cvec[1:2], jnp.zeros((1, D), F32),
        dbin.reshape(NGRP, D),
        mvec[3:4], mvec[1:2], mvec[2:3],
        lru[0][2][0:8], lru[1][2][0:3],
        lru[1][2][3:8],
        dlb_rows,
        mvec[4:5],
        jnp.zeros((3, D), F32)], axis=0)
    g_pack, g_wri = _all_gather([pack, wri_piece], "gather_small_grads")
    tot = _sum_rows(g_pack, "sum_small_grads")

    loss = tot[36, 0]
    dmx = g_pack[:, 0:3, :].reshape(NDEV, 3 * D)
    dmc = g_pack[:, 3:6, :].reshape(NDEV, 3 * D)
    grad_w_mod = _wmod_grad(c_all.T, c_ctx.reshape(D, 1), _local_cols(dmx, me, mcols), _local_cols(dmc, me, mcols),
                            "grad_w_mod").reshape(1, D, mcols)
    grad_b_mod = (tot[0:3] + tot[3:6]).reshape(1, 3 * D)
    dmc_sum = jnp.concatenate([tot[3:6].reshape(1, 3 * D), jnp.zeros((7, 3 * D), F32)], axis=0)
    grad_c_ctx = _cctx_grad(dmc_sum, w_mod_f, cc[1:2], "grad_c_ctx")[0]
    grad_b_in = tot[6:15].reshape(1, NGRP * D)
    grad_norm_a_g = tot[15].reshape(NH, DH).sum(axis=0).reshape(1, DH)
    grad_ln_g, grad_ln_b = tot[16:17], tot[17:18]
    grad_conv_b = tot[21:22] + tot[29:30]
    grad_conv_w = _local_cols(tot[22:26] + tot[30:34], me, DH).reshape(1, 4, DH)
    grad_b_r = _local_cols(jnp.stack([tot[18], tot[26]]), me, DH).reshape(1, 2, DH)
    grad_b_i = _local_cols(jnp.stack([tot[19], tot[27]]), me, DH).reshape(1, 2, DH)
    grad_lam = _local_cols(jnp.stack([tot[20], tot[28]]), me, DH).reshape(1, 2, DH)
    lb_loc = _local_cols(lb[0:2], me, DH)
    dl0 = _local_cols(tot[34:36], me, DH) * lb_loc * (1.0 - lb_loc)
    grad_lb_logits = jnp.stack([dl0, -dl0])

    half = 2 * NH * DH
    g_ri = g_wri.reshape(2 * half, DH)
    grad_w_r, grad_w_i = g_ri[:half].reshape(w_r.shape), g_ri[half:].reshape(w_i.shape)
    d_w_r, nm_w_r, nv_w_r = _adamw(grad_w_r, w_r, m_w_r, v_w_r, "update_w_r")
    d_w_i, nm_w_i, nv_w_i = _adamw(grad_w_i, w_i, m_w_i, v_w_i, "update_w_i")

    d_w_mod, nm_w_mod, nv_w_mod = _adamw(grad_w_mod, w_mod, m_w_mod, v_w_mod, "update_w_mod")

    small_items = [
        (grad_c_ctx, c_ctx, m_c_ctx, v_c_ctx), (grad_b_mod, b_mod, m_b_mod, v_b_mod),
        (grad_b_in, b_in, m_b_in, v_b_in), (grad_lb_logits, lb_logits, m_lb_logits, v_lb_logits),
        (grad_norm_a_g, norm_a_g, m_norm_a_g, v_norm_a_g), (grad_conv_w, conv_w, m_conv_w, v_conv_w),
        (grad_conv_b, conv_b, m_conv_b, v_conv_b), (grad_b_r, b_r, m_b_r, v_b_r), (grad_b_i, b_i, m_b_i, v_b_i),
        (grad_lam, lam, m_lam, v_lam), (grad_ln_g, ln_g, m_ln_g, v_ln_g), (grad_ln_b, ln_b, m_ln_b, v_ln_b)]
    cat = lambda k: jnp.concatenate([it[k].reshape(-1, DH) for it in small_items], axis=0)
    sd, sm, sv = _adamw(cat(0), cat(1), cat(2), cat(3), "update_small")
    small_out = []
    off = 0
    for it in small_items:
        shp = it[1].shape
        nrow = it[1].size // DH
        small_out.append(tuple(t[off:off + nrow].reshape(shp) for t in (sd, sm, sv)))
        off += nrow
    (o_c_ctx, o_b_mod, o_b_in, o_lb, o_norm, o_conv_w, o_conv_b, o_b_r, o_b_i, o_lam, o_ln_g, o_ln_b) = small_out

    grads = [grad_c_ctx.reshape(c_ctx.shape), grad_w_mod, grad_b_mod, g_w_in, grad_b_in, grad_lb_logits, grad_norm_a_g,
             grad_conv_w, grad_conv_b, grad_w_r, grad_b_r, grad_w_i, grad_b_i, grad_lam, g_p_a, g_p_b, g_w_out,
             grad_ln_g, grad_ln_b]
    per_kind = []
    for k in range(3):
        per_kind.append([
            o_c_ctx[k], (d_w_mod, nm_w_mod, nv_w_mod)[k], o_b_mod[k], (d_w_in, nm_w_in, nv_w_in)[k], o_b_in[k], o_lb[k],
            o_norm[k], o_conv_w[k], o_conv_b[k], (d_w_r, nm_w_r, nv_w_r)[k], o_b_r[k], (d_w_i, nm_w_i, nv_w_i)[k],
            o_b_i[k], o_lam[k], (d_p_a, nm_p_a, nv_p_a)[k], (d_p_b, nm_p_b, nv_p_b)[k], (d_w_out, nm_w_out, nv_w_out)[k],
            o_ln_g[k], o_ln_b[k]])
    return (loss, grad_x.reshape(x.shape), *grads, *per_kind[0], *per_kind[1], *per_kind[2])
```

```python
import functools

import jax
import jax.numpy as jnp
from jax import lax
from jax.experimental import pallas as pl
from jax.experimental.pallas import tpu as pltpu

F32 = jnp.float32
BF16 = jnp.bfloat16

D = 1024
NH = 8
DH = 128
CHUNK = 64
GLA_HEADS_PER_STEP = 4
GRID_W = 64
NGRP = 9
NDEV = 8
RG_C = 8.0
ALPHA = 2.0 ** 0.25
LN_EPS = 1e-5
RMS_EPS = 1e-6
Q_SCALE = DH ** -0.5
ADAM_LR, ADAM_B1, ADAM_B2, ADAM_EPS, ADAM_WD, ADAM_STEP = 1e-3, 0.9, 0.999, 1e-8, 0.01, 10
ADAM_C1 = 1.0 / (1.0 - ADAM_B1 ** ADAM_STEP)
ADAM_C2 = 1.0 / (1.0 - ADAM_B2 ** ADAM_STEP)

ANY = pl.BlockSpec(memory_space=pl.ANY)


def _sigmoid(t):
    return 1.0 / (1.0 + jnp.exp(-t))


def _dsilu(t, s):
    return s * (1.0 + t * (1.0 - s))


def _dot(a, b):
    return jnp.dot(a, b, preferred_element_type=F32)


def _dot_nt(a, b):
    return lax.dot_general(a, b, (((1,), (1,)), ((), ())), preferred_element_type=F32)


def _dot_tn(a, b):
    return lax.dot_general(a, b, (((0,), (0,)), ((), ())), preferred_element_type=F32)


def _my_index():
    return 4 * lax.axis_index("x") + 2 * lax.axis_index("y") + lax.axis_index("c")


def _dev_tuple(j):
    return (j >> 2, (j >> 1) & 1, j & 1)


def _all_gather(shards, name):
    n = len(shards)

    def body(*refs):
        ins, outs = refs[:n], refs[n:2 * n]
        send_sems, recv_sems, loc_sems = refs[2 * n:]
        me = _my_index()
        for a in range(n):
            pltpu.make_async_copy(ins[a], outs[a].at[me], loc_sems.at[a]).start()
        for j in range(NDEV):
            @pl.when(me != j)
            def _():
                for a in range(n):
                    pltpu.make_async_remote_copy(
                        src_ref=ins[a], dst_ref=outs[a].at[me],
                        send_sem=send_sems.at[a * NDEV + j], recv_sem=recv_sems.at[a * NDEV + me],
                        device_id=_dev_tuple(j), device_id_type=pl.DeviceIdType.MESH).start()
        for j in range(NDEV):
            @pl.when(me != j)
            def _():
                for a in range(n):
                    pltpu.make_async_remote_copy(
                        src_ref=ins[a], dst_ref=outs[a].at[j],
                        send_sem=send_sems.at[a * NDEV + j], recv_sem=recv_sems.at[a * NDEV + j],
                        device_id=_dev_tuple(j), device_id_type=pl.DeviceIdType.MESH).wait()
        for a in range(n):
            pltpu.make_async_copy(ins[a], outs[a].at[me], loc_sems.at[a]).wait()

    return pl.pallas_call(
        body, name=name,
        out_shape=[jax.ShapeDtypeStruct((NDEV,) + s.shape, s.dtype) for s in shards],
        in_specs=[ANY] * n, out_specs=[ANY] * n,
        scratch_shapes=[pltpu.SemaphoreType.DMA((n * NDEV,)), pltpu.SemaphoreType.DMA((n * NDEV,)),
                        pltpu.SemaphoreType.DMA((n,))],
    )(*shards)


_STEP_MASKS = ((2, 4, 6, 3, 5, 7, 1, 0), (4, 2, 6, 5, 3, 7, 1, 0))


def _peer_schedule():
    tab = jnp.array(_STEP_MASKS, jnp.int32)
    return jnp.bitwise_xor(_my_index(), tab[lax.axis_index("c")])


def _step_peer(s):
    def pick(row):
        m = jnp.int32(row[NDEV - 1])
        for t in range(NDEV - 2, -1, -1):
            m = jnp.where(s == t, jnp.int32(row[t]), m)
        return m
    mask = jnp.where(lax.axis_index("c") == 0, pick(_STEP_MASKS[0]), pick(_STEP_MASKS[1]))
    return jnp.bitwise_xor(_my_index(), mask)


def _dev_of(p):
    return (p // 4, (p // 2) % 2, p % 2)


def _dwin_exchange(u, dz, init, cs_init, extras, splits, name):
    m, ka = u.shape
    n = dz.shape[1]
    pc = n // NDEV
    tk = _row_tile(m, 512)
    nk = m // tk
    ne = len(extras)
    pieces = []
    for a in range(ne):
        r, c = extras[a].shape
        pieces.append((r // NDEV, c) if splits[a] == 0 else (r, c // NDEV))

    def piece_ref(ref, a, j):
        pr, pcol = pieces[a]
        if splits[a] == 0:
            return ref.at[pl.ds(j * pr, pr), :]
        return ref.at[:, pl.ds(j * pcol, pcol)]

    def body(pidx_ref, u_ref, dz_ref, init_ref, csi_ref, *rest):
        ex_in = rest[:ne]
        rwin, cs_ref = rest[ne], rest[ne + 1]
        ex_out = rest[ne + 2:2 * ne + 2]
        acc, sbuf, wsend, wrecv, wloc, esend, erecv, eloc = rest[2 * ne + 2:]
        s, k = pl.program_id(0), pl.program_id(1)
        me = _my_index()

        def extra_local(a, j):
            return pltpu.make_async_copy(piece_ref(ex_in[a], a, j), ex_out[a].at[j], eloc.at[a])

        def extra_remote(a, j, slot):
            return pltpu.make_async_remote_copy(
                src_ref=piece_ref(ex_in[a], a, j), dst_ref=ex_out[a].at[slot],
                send_sem=esend.at[a * NDEV + j], recv_sem=erecv.at[a * NDEV + slot],
                device_id=_dev_tuple(j), device_id_type=pl.DeviceIdType.MESH)

        def slab_copy(slot, p):
            return pltpu.make_async_remote_copy(
                src_ref=sbuf.at[slot], dst_ref=rwin.at[me], send_sem=wsend.at[slot], recv_sem=wrecv.at[me],
                device_id=_dev_of(p), device_id_type=pl.DeviceIdType.MESH)

        @pl.when((s == 0) & (k == 0))
        def _():
            for j in range(NDEV):
                @pl.when(me == j)
                def _():
                    for a in range(ne):
                        extra_local(a, j).start()

                @pl.when(me != j)
                def _():
                    for a in range(ne):
                        extra_remote(a, j, me).start()

        @pl.when(k == 0)
        def _():
            acc[...] = init_ref[...]
            cs_ref[...] = csi_ref[...]

        bv = dz_ref[...]
        acc[...] += _dot_tn(u_ref[...], bv)
        cs_ref[...] += jnp.sum(bv.astype(F32), axis=0, keepdims=True)

        @pl.when(k == nk - 1)
        def _():
            slot = s % 2

            @pl.when(s >= 2)
            def _():
                slab_copy(slot, me).wait_send()

            sbuf[slot] = acc[...].astype(BF16)

            @pl.when(s < NDEV - 1)
            def _():
                slab_copy(slot, _step_peer(s)).start()

            @pl.when(s == NDEV - 1)
            def _():
                own = pltpu.make_async_copy(sbuf.at[slot], rwin.at[me], wloc.at[0])
                own.start()
                slab_copy(1 - slot, me).wait_send()
                for j in range(NDEV):
                    @pl.when(me != j)
                    def _():
                        pltpu.make_async_remote_copy(
                            src_ref=sbuf.at[0], dst_ref=rwin.at[j], send_sem=wsend.at[0], recv_sem=wrecv.at[j],
                            device_id=_dev_tuple(j), device_id_type=pl.DeviceIdType.MESH).wait_recv()
                        for a in range(ne):
                            extra_remote(a, j, j).wait()

                    @pl.when(me == j)
                    def _():
                        for a in range(ne):
                            extra_local(a, j).wait()
                own.wait()

    grid_spec = pltpu.PrefetchScalarGridSpec(
        num_scalar_prefetch=1, grid=(NDEV, nk),
        in_specs=[pl.BlockSpec((tk, ka), lambda s, k, pidx: (k, 0)),
                  pl.BlockSpec((tk, pc), lambda s, k, pidx: (k, pidx[s])),
                  pl.BlockSpec((ka, pc), lambda s, k, pidx: (0, pidx[s])),
                  pl.BlockSpec((1, pc), lambda s, k, pidx: (0, pidx[s]))] + [ANY] * ne,
        out_specs=[ANY, pl.BlockSpec((1, pc), lambda s, k, pidx: (0, pidx[s]))] + [ANY] * ne,
        scratch_shapes=[pltpu.VMEM((ka, pc), F32), pltpu.VMEM((2, ka, pc), BF16),
                        pltpu.SemaphoreType.DMA((2,)), pltpu.SemaphoreType.DMA((NDEV,)), pltpu.SemaphoreType.DMA((1,)),
                        pltpu.SemaphoreType.DMA((ne * NDEV,)), pltpu.SemaphoreType.DMA((ne * NDEV,)),
                        pltpu.SemaphoreType.DMA((ne,))])
    return pl.pallas_call(
        body, name=name, grid_spec=grid_spec,
        out_shape=[jax.ShapeDtypeStruct((NDEV, ka, pc), BF16), jax.ShapeDtypeStruct((1, n), F32)]
        + [jax.ShapeDtypeStruct((NDEV,) + pieces[a], extras[a].dtype) for a in range(ne)],
    )(_peer_schedule(), u, dz, init, cs_init, *extras)


def _adam_math(g, w, m, v):
    m2 = ADAM_B1 * m + (1.0 - ADAM_B1) * g
    v2 = ADAM_B2 * v + (1.0 - ADAM_B2) * (g * g)
    delta = -ADAM_LR * ((m2 * ADAM_C1) / (jnp.sqrt(v2 * ADAM_C2) + ADAM_EPS) + ADAM_WD * w)
    return delta, m2, v2


def _row_tile(r, cap):
    t = min(r, cap)
    while r % t:
        t //= 2
    return t


def _adamw(g, w, m, v, name):
    shape = w.shape
    cols = shape[-1] if w.ndim >= 2 and shape[-1] % 128 == 0 else 128
    g2, w2, m2, v2 = (t.reshape(-1, cols) for t in (g, w, m, v))
    r = g2.shape[0]
    tr = _row_tile(r, 256)

    def body(g_ref, w_ref, m_ref, v_ref, d_ref, mo_ref, vo_ref):
        d, mm, vv = _adam_math(g_ref[...], w_ref[...], m_ref[...], v_ref[...])
        d_ref[...] = d
        mo_ref[...] = mm
        vo_ref[...] = vv

    spec = pl.BlockSpec((tr, cols), lambda i: (i, 0))
    outs = pl.pallas_call(
        body, name=name, grid=(r // tr,),
        out_shape=[jax.ShapeDtypeStruct((r, cols), F32)] * 3,
        in_specs=[spec] * 4, out_specs=[spec] * 3,
    )(g2, w2, m2, v2)
    return tuple(o.reshape(shape) for o in outs)


def _sum_adamw(parts, w, m, v, name):
    _, r, c = parts.shape
    shape = w.shape
    w2, m2, v2 = (t.reshape(r, c) for t in (w, m, v))
    tr = _row_tile(r, 128)

    def body(p_ref, w_ref, m_ref, v_ref, g_ref, d_ref, mo_ref, vo_ref):
        g = p_ref[0].astype(F32)
        for k in range(1, NDEV):
            g = g + p_ref[k].astype(F32)
        d, mm, vv = _adam_math(g, w_ref[...], m_ref[...], v_ref[...])
        g_ref[...] = g
        d_ref[...] = d
        mo_ref[...] = mm
        vo_ref[...] = vv

    spec = pl.BlockSpec((tr, c), lambda i: (i, 0))
    outs = pl.pallas_call(
        body, name=name, grid=(r // tr,),
        out_shape=[jax.ShapeDtypeStruct((r, c), F32)] * 4,
        in_specs=[pl.BlockSpec((NDEV, tr, c), lambda i: (0, i, 0))] + [spec] * 3, out_specs=[spec] * 4,
    )(parts, w2, m2, v2)
    return tuple(o.reshape(shape) for o in outs)


def _sum_rows(parts, name):
    _, r, c = parts.shape

    def body(p_ref, o_ref):
        g = p_ref[0]
        for k in range(1, NDEV):
            g = g + p_ref[k]
        o_ref[...] = g

    return pl.pallas_call(
        body, name=name, out_shape=jax.ShapeDtypeStruct((r, c), F32),
    )(parts)


def _prep(cc, w_mod_full, b_mod, lbl):
    def body(cc_ref, w_ref, b_ref, l_ref, mod_ref, lb_ref):
        t = cc_ref[...]
        s = (t * _sigmoid(t)).astype(BF16)
        mod_ref[...] = _dot(s, w_ref[...]) + b_ref[...]
        lb_ref[...] = _sigmoid(l_ref[0] - l_ref[1])

    return pl.pallas_call(
        body, name="prep",
        out_shape=[jax.ShapeDtypeStruct((8, 3 * D), F32), jax.ShapeDtypeStruct((8, D), F32)],
    )(cc, w_mod_full, b_mod, lbl)


def _modulate(xin, mod, row, name):
    n = xin.shape[0]
    tm = _row_tile(n, 512)

    def body(x_ref, mod_ref, u_ref):
        sh = mod_ref[row:row + 1, 0:D]
        sc = mod_ref[row:row + 1, D:2 * D]
        u_ref[...] = (x_ref[...] * (1.0 + sc) + sh).astype(BF16)

    return pl.pallas_call(
        body, name=name, grid=(n // tm,),
        out_shape=jax.ShapeDtypeStruct((n, D), BF16),
        in_specs=[pl.BlockSpec((tm, D), lambda i: (i, 0)), pl.BlockSpec((8, 3 * D), lambda i: (0, 0))],
        out_specs=pl.BlockSpec((tm, D), lambda i: (i, 0)),
    )(xin, mod)


def _mm_bias(a, b, bias, name):
    m, k = a.shape
    n = b.shape[1]
    tm = _row_tile(m, 512)
    tn = 1024

    def body(a_ref, b_ref, bias_ref, o_ref):
        o_ref[...] = _dot(a_ref[...], b_ref[...]) + bias_ref[...]

    return pl.pallas_call(
        body, name=name, grid=(n // tn, m // tm),
        out_shape=jax.ShapeDtypeStruct((m, n), F32),
        in_specs=[pl.BlockSpec((tm, k), lambda j, i: (i, 0)), pl.BlockSpec((k, tn), lambda j, i: (0, j)),
                  pl.BlockSpec((1, tn), lambda j, i: (0, j))],
        out_specs=pl.BlockSpec((tm, tn), lambda j, i: (i, j)),
    )(a, b, bias)


def _mm_tn(a, b, init, name, with_colsum=False, colsum_init=None, out_dtype=F32):
    m, ka = a.shape
    n = b.shape[1]
    tk = _row_tile(m, 512)
    tn = 1024
    nk = m // tk
    has_init = init is not None

    def body(*refs):
        a_ref, b_ref = refs[0], refs[1]
        pos = 2
        init_ref = cs_init_ref = None
        if has_init:
            init_ref = refs[pos]
            pos += 1
            if with_colsum:
                cs_init_ref = refs[pos]
                pos += 1
        o_ref = refs[pos]
        cs_ref = refs[pos + 1] if with_colsum else None
        acc = refs[-1]
        k = pl.program_id(1)

        @pl.when(k == 0)
        def _():
            if has_init:
                acc[...] = init_ref[...]
                if with_colsum:
                    cs_ref[...] = cs_init_ref[...]
            else:
                acc[...] = jnp.zeros_like(acc)
                if with_colsum:
                    cs_ref[...] = jnp.zeros_like(cs_ref)

        bv = b_ref[...]
        acc[...] += _dot_tn(a_ref[...], bv)
        if with_colsum:
            cs_ref[...] += jnp.sum(bv.astype(F32), axis=0, keepdims=True)

        @pl.when(k == nk - 1)
        def _():
            o_ref[...] = acc[...].astype(out_dtype)

    in_specs = [pl.BlockSpec((tk, ka), lambda j, k: (k, 0)), pl.BlockSpec((tk, tn), lambda j, k: (k, j))]
    args = [a, b]
    if has_init:
        in_specs.append(pl.BlockSpec((ka, tn), lambda j, k: (0, j)))
        args.append(init)
        if with_colsum:
            in_specs.append(pl.BlockSpec((1, tn), lambda j, k: (0, j)))
            args.append(colsum_init)
    out_shape = [jax.ShapeDtypeStruct((ka, n), out_dtype)]
    out_specs = [pl.BlockSpec((ka, tn), lambda j, k: (0, j))]
    if with_colsum:
        out_shape.append(jax.ShapeDtypeStruct((1, n), F32))
        out_specs.append(pl.BlockSpec((1, tn), lambda j, k: (0, j)))
    outs = pl.pallas_call(
        body, name=name, grid=(n // tn, nk), out_shape=out_shape, in_specs=in_specs, out_specs=out_specs,
        scratch_shapes=[pltpu.VMEM((ka, tn), F32)],
    )(*args)
    return outs if with_colsum else outs[0]


def _input_grad(dz, w_full, xin, dr, mod, row, name):
    m, n = dz.shape
    tm = _row_tile(m, 512)
    tk = 1024
    nk = n // tk
    has_dr = dr is not None

    def body(*refs):
        if has_dr:
            dz_ref, w_ref, x_ref, dr_ref, mod_ref, gx_ref, vec_ref, acc = refs
        else:
            dz_ref, w_ref, x_ref, mod_ref, vec_ref, acc = refs
        i, k = pl.program_id(0), pl.program_id(1)

        @pl.when(k == 0)
        def _():
            acc[...] = jnp.zeros_like(acc)

        @pl.when((i == 0) & (k == 0))
        def _():
            vec_ref[...] = jnp.zeros_like(vec_ref)

        acc[...] += _dot_nt(dz_ref[...], w_ref[...])

        @pl.when(k == nk - 1)
        def _():
            du = acc[...]
            xv = x_ref[...]
            if has_dr:
                sc = mod_ref[row:row + 1, D:2 * D]
                gx_ref[...] = ALPHA * dr_ref[...] + du * (1.0 + sc)
            vec_ref[0:1, :] += jnp.sum(du, axis=0, keepdims=True)
            vec_ref[1:2, :] += jnp.sum(du * xv, axis=0, keepdims=True)

    row_spec = pl.BlockSpec((tm, D), lambda i, k: (i, 0))
    in_specs = [pl.BlockSpec((tm, tk), lambda i, k: (i, k)), pl.BlockSpec((D, tk), lambda i, k: (0, k)), row_spec]
    args = [dz, w_full, xin]
    if has_dr:
        in_specs.append(row_spec)
        args.append(dr)
    in_specs.append(pl.BlockSpec((8, 3 * D), lambda i, k: (0, 0)))
    args.append(mod)
    out_shape, out_specs = [], []
    if has_dr:
        out_shape.append(jax.ShapeDtypeStruct((m, D), F32))
        out_specs.append(row_spec)
    out_shape.append(jax.ShapeDtypeStruct((8, D), F32))
    out_specs.append(pl.BlockSpec((8, D), lambda i, k: (0, 0)))
    outs = pl.pallas_call(
        body, name=name, grid=(m // tm, nk), out_shape=out_shape, in_specs=in_specs, out_specs=out_specs,
        scratch_shapes=[pltpu.VMEM((tm, D), F32)],
    )(*args)
    return outs if has_dr else (None, outs[0])


def _tri(reverse):
    r = lax.broadcasted_iota(jnp.int32, (CHUNK, CHUNK), 0)
    c = lax.broadcasted_iota(jnp.int32, (CHUNK, CHUNK), 1)
    return (c >= r) if reverse else (c <= r)


def _cum_f32(tri_b, t):
    hi = t.astype(BF16)
    r1 = t - hi.astype(F32)
    mid = r1.astype(BF16)
    lo = (r1 - mid.astype(F32)).astype(BF16)
    return _dot(tri_b, hi) + _dot(tri_b, mid) + _dot(tri_b, lo)


def _gla_features(zq, zf, lb):
    sq = _sigmoid(zq)
    q = zq * sq * Q_SCALE
    sf = _sigmoid(zf)
    f = lb + (1.0 - lb) * sf
    return q, sq, f, sf


def _gla_decays(f, tri_b, last):
    lf = jnp.log(f)
    g = _cum_f32(tri_b, lf)
    gl = g[last:last + 1, :]
    return g, gl


def _gla_block(n):
    return 256 if n % 256 == 0 else CHUNK


def _gla_fwd(z, lb, s0, d, name):
    n = z.shape[0]
    blk = _gla_block(n)
    nb, npb = n // blk, blk // CHUNK
    reverse = d == 1
    last = 0 if reverse else CHUNK - 1
    order = list(range(npb))[::-1] if reverse else list(range(npb))

    def bmap(i):
        return nb - 1 - i if reverse else i

    hp = GLA_HEADS_PER_STEP
    hw = hp * DH
    units = [(hh, cidx) for hh in range(hp) for cidx in order]

    def body(zq_ref, zf_ref, zv_ref, lb_ref, s0_ref, o_ref, ss_ref, sf_ref, st):
        i = pl.program_id(1)

        @pl.when(i == 0)
        def _():
            st[...] = s0_ref[...]

        mask = _tri(reverse)
        tri_b = jnp.where(mask, 1.0, 0.0).astype(BF16)
        feat = {}
        for u in units:
            hh, cidx = u
            rows, cols = pl.ds(cidx * CHUNK, CHUNK), pl.ds(hh * DH, DH)
            q, _, f, _ = _gla_features(zq_ref[rows, cols], zf_ref[rows, cols], lb_ref[d:d + 1, cols])
            feat[u] = (q, 1.0 - f, jnp.log(f), zv_ref[rows, cols].astype(BF16))
        dec = {u: _cum_f32(tri_b, feat[u][2]) for u in units}
        ops = {}
        for u in units:
            q, k, _, vb = feat[u]
            g = dec[u]
            gl = g[last:last + 1, :]
            ops[u] = ((q * jnp.exp(g)).astype(BF16), (k * jnp.exp(-g)).astype(BF16),
                      (k * jnp.exp(gl - g)).astype(BF16), jnp.exp(gl), vb)
        att = {u: jnp.where(mask, _dot_nt(ops[u][0], ops[u][1]), 0.0).astype(BF16) for u in units}
        upd = {u: _dot_tn(ops[u][4], ops[u][2]) for u in units}
        intra = {u: _dot(att[u], ops[u][4]) for u in units}
        s_in = {}
        for hh in range(hp):
            s = st[hh]
            for cidx in order:
                s_in[(hh, cidx)] = s
                s = s * ops[(hh, cidx)][3] + upd[(hh, cidx)]
            st[hh] = s
            sf_ref[hh] = s
        for u in units:
            hh, cidx = u
            rows, cols = pl.ds(cidx * CHUNK, CHUNK), pl.ds(hh * DH, DH)
            o_ref[rows, cols] = intra[u] + _dot_nt(ops[u][0], s_in[u].astype(BF16))
            ss_ref[hh, cidx] = s_in[u]

    def col(g):
        return lambda h, i: (bmap(i), g * (NH // hp) + h)

    return pl.pallas_call(
        body, name=name, grid=(NH // hp, nb),
        out_shape=[jax.ShapeDtypeStruct((n, D), F32), jax.ShapeDtypeStruct((NH, n // CHUNK, DH, DH), F32),
                   jax.ShapeDtypeStruct((NH, DH, DH), F32)],
        in_specs=[pl.BlockSpec((blk, hw), col(0)), pl.BlockSpec((blk, hw), col(1 + d)),
                  pl.BlockSpec((blk, hw), col(3)), pl.BlockSpec((8, hw), lambda h, i: (0, h)),
                  pl.BlockSpec((hp, DH, DH), lambda h, i: (h, 0, 0))],
        out_specs=[pl.BlockSpec((blk, hw), lambda h, i: (bmap(i), h)),
                   pl.BlockSpec((hp, npb, DH, DH), lambda h, i: (h, bmap(i), 0, 0)),
                   pl.BlockSpec((hp, DH, DH), lambda h, i: (h, 0, 0))],
        scratch_shapes=[pltpu.VMEM((hp, DH, DH), F32)],
    )(z, z, z, lb, s0)


def _gla_bwd(z, lb, s_start, do, ds_fin, acc_q, acc_v, d, name):
    n = z.shape[0]
    blk = _gla_block(n)
    nb, npb = n // blk, blk // CHUNK
    reverse = d == 1
    last = 0 if reverse else CHUNK - 1
    order = list(range(npb)) if reverse else list(range(npb))[::-1]
    has_do = do is not None
    has_acc = acc_q is not None
    hp = GLA_HEADS_PER_STEP
    hw = hp * DH
    units = [(hh, cidx) for hh in range(hp) for cidx in order]

    def bmap(i):
        return i if reverse else nb - 1 - i

    def body(*refs):
        zq_ref, zf_ref, zv_ref, lb_ref, ss_ref, dsf_ref = refs[:6]
        pos = 6
        do_ref = aq_ref = av_ref = None
        if has_do:
            do_ref = refs[pos]
            pos += 1
        if has_acc:
            aq_ref, av_ref = refs[pos], refs[pos + 1]
            pos += 2
        dzq_ref, dzf_ref, dzv_ref, dlb_ref, ds0_ref, dst = refs[pos:]
        i = pl.program_id(1)

        @pl.when(i == 0)
        def _():
            dst[...] = dsf_ref[...]
            dlb_ref[...] = jnp.zeros_like(dlb_ref)

        mask = _tri(reverse)
        tri_b = jnp.where(mask, 1.0, 0.0).astype(BF16)
        tri_t = jnp.where(_tri(not reverse), 1.0, 0.0).astype(BF16)

        def where(u):
            return pl.ds(u[1] * CHUNK, CHUNK), pl.ds(u[0] * DH, DH)

        feat = {}
        for u in units:
            rows, cols = where(u)
            zq, zf = zq_ref[rows, cols], zf_ref[rows, cols]
            lbv = lb_ref[d:d + 1, cols]
            q, sq, f, sf = _gla_features(zq, zf, lbv)
            feat[u] = dict(zq=zq, q=q, sq=sq, f=f, sf=sf, lbv=lbv, k=1.0 - f, vb=zv_ref[rows, cols].astype(BF16))
        dec = {u: _cum_f32(tri_b, jnp.log(feat[u]["f"])) for u in units}
        for u in units:
            w = feat[u]
            g = dec[u]
            gl = g[last:last + 1, :]
            w["eg"], w["egi"], w["ege"], w["egl"] = jnp.exp(g), jnp.exp(-g), jnp.exp(gl - g), jnp.exp(gl)
            w["qd"], w["ki"], w["ke"] = w["q"] * w["eg"], w["k"] * w["egi"], w["k"] * w["ege"]
            w["qdb"], w["kib"], w["keb"] = w["qd"].astype(BF16), w["ki"].astype(BF16), w["ke"].astype(BF16)
            w["s_in"] = ss_ref[u[0], u[1]]
        if has_do:
            for u in units:
                w = feat[u]
                rows, cols = where(u)
                w["dob"] = do_ref[rows, cols].astype(BF16)
            for u in units:
                w = feat[u]
                w["a"] = jnp.where(mask, _dot_nt(w["qdb"], w["kib"]), 0.0).astype(BF16)
                w["da"] = jnp.where(mask, _dot_nt(w["dob"], w["vb"]), 0.0).astype(BF16)
                w["m"] = _dot_tn(w["dob"], w["qdb"])
        for hh in range(hp):
            ds = dst[hh]
            for cidx in order:
                w = feat[(hh, cidx)]
                w["ds"] = ds
                ds = ds * w["egl"]
                if has_do:
                    ds = ds + w["m"]
            dst[hh] = ds
            ds0_ref[hh] = ds
        for u in units:
            w = feat[u]
            dsb = w["ds"].astype(BF16)
            w["dke"] = _dot(w["vb"], dsb)
            w["dv"] = _dot_nt(w["keb"], dsb)
            if has_do:
                w["dv"] = w["dv"] + _dot_tn(w["a"], w["dob"])
                w["dqd"] = _dot(w["da"], w["kib"]) + _dot(w["dob"], w["s_in"].astype(BF16))
                w["dki"] = _dot_tn(w["da"], w["qdb"])
        for u in units:
            w = feat[u]
            dkeke = w["dke"] * w["ke"]
            w["dgl"] = (w["egl"] * jnp.sum(w["s_in"] * w["ds"], axis=0, keepdims=True)
                        + jnp.sum(dkeke, axis=0, keepdims=True))
            dg = -dkeke
            dk = w["dke"] * w["ege"]
            if has_do:
                dg = dg + w["dqd"] * w["qd"] - w["dki"] * w["ki"]
                dk = dk + w["dki"] * w["egi"]
            w["dg"], w["dk"] = dg, dk
        dlf = {u: _cum_f32(tri_t, feat[u]["dg"]) for u in units}
        for u in units:
            w = feat[u]
            rows, cols = where(u)
            df = (dlf[u] + w["dgl"]) / w["f"] - w["dk"]
            sf = w["sf"]
            dzf_ref[rows, cols] = df * (1.0 - w["lbv"]) * sf * (1.0 - sf)
            dlb_ref[0:1, cols] += jnp.sum(df * (1.0 - sf), axis=0, keepdims=True)
            if has_do:
                dzq = w["dqd"] * w["eg"] * (Q_SCALE * _dsilu(w["zq"], w["sq"]))
            else:
                dzq = jnp.zeros((CHUNK, DH), F32)
            dv = w["dv"]
            if has_acc:
                dzq = dzq + aq_ref[rows, cols]
                dv = dv + av_ref[rows, cols]
            dzq_ref[rows, cols] = dzq
            dzv_ref[rows, cols] = dv

    def col(g):
        return lambda h, i: (bmap(i), g * (NH // hp) + h)

    tok = pl.BlockSpec((blk, hw), lambda h, i: (bmap(i), h))
    state = pl.BlockSpec((hp, DH, DH), lambda h, i: (h, 0, 0))
    in_specs = [pl.BlockSpec((blk, hw), col(0)), pl.BlockSpec((blk, hw), col(1 + d)), pl.BlockSpec((blk, hw), col(3)),
                pl.BlockSpec((8, hw), lambda h, i: (0, h)),
                pl.BlockSpec((hp, npb, DH, DH), lambda h, i: (h, bmap(i), 0, 0)), state]
    args = [z, z, z, lb, s_start, ds_fin]
    if has_do:
        in_specs.append(tok)
        args.append(do)
    if has_acc:
        in_specs += [tok, tok]
        args += [acc_q, acc_v]
    return pl.pallas_call(
        body, name=name, grid=(NH // hp, nb),
        out_shape=[jax.ShapeDtypeStruct((n, D), F32)] * 3 + [jax.ShapeDtypeStruct((8, D), F32),
                                                            jax.ShapeDtypeStruct((NH, DH, DH), F32)],
        in_specs=in_specs,
        out_specs=[tok, tok, tok, pl.BlockSpec((8, hw), lambda h, i: (0, h)), state],
        scratch_shapes=[pltpu.VMEM((hp, DH, DH), F32)],
    )(*args)


def _shift(t, s, fill, down):
    n = t.shape[0]
    rows = lax.broadcasted_iota(jnp.int32, t.shape, 0)
    if down:
        return jnp.where(rows >= s, pltpu.roll(t, s, 0), fill)
    return jnp.where(rows < n - s, pltpu.roll(t, n - s, 0), fill)


SUBLANES = 8


def _chain_scan(a, b, h_in, down):
    n = a.shape[0]
    ng = n // SUBLANES
    rows = lax.broadcasted_iota(jnp.int32, (SUBLANES, a.shape[1]), 0)
    local = []
    for g in range(ng):
        aa, bb = a[g * SUBLANES:(g + 1) * SUBLANES], b[g * SUBLANES:(g + 1) * SUBLANES]
        for s in (1, 2, 4):
            if down:
                keep, amt = rows >= s, s
            else:
                keep, amt = rows < SUBLANES - s, SUBLANES - s
            bb = bb + aa * jnp.where(keep, pltpu.roll(bb, amt, 0), 0.0)
            aa = aa * jnp.where(keep, pltpu.roll(aa, amt, 0), 1.0)
        local.append((aa, bb))
    out = [None] * ng
    carry = h_in
    for g in (range(ng) if down else range(ng - 1, -1, -1)):
        aa, bb = local[g]
        hg = bb + aa * carry
        out[g] = hg
        carry = hg[SUBLANES - 1:SUBLANES] if down else hg[0:1]
    return (jnp.concatenate(out, axis=0) if ng > 1 else out[0]), carry


def _conv_taps(xv):
    return (_shift(xv, 1, 0.0, True), xv, _shift(xv, 1, 0.0, False), _shift(xv, 2, 0.0, False))


def _conv(taps, cw, cb):
    return cb + cw[0:1, :] * taps[0] + cw[1:2, :] * taps[1] + cw[2:3, :] * taps[2] + cw[3:4, :] * taps[3]


def _neg_expm1(t):
    series = -t * (1.0 + t * (0.5 + t * (1.0 / 6.0 + t * (1.0 / 24.0 + t * (1.0 / 120.0)))))
    return jnp.where(t > -0.1, series, 1.0 - jnp.exp(t))


def _lru_gates(xc, wr, br, wi, bi, lam):
    xcb = xc.astype(BF16)
    r = _sigmoid(_dot(xcb, wr) + br)
    gi = _sigmoid(_dot(xcb, wi) + bi)
    sp = jnp.maximum(-lam, 0.0) + jnp.log(1.0 + jnp.exp(-jnp.abs(lam)))
    la = -RG_C * r * sp
    a = jnp.exp(la)
    mult = jnp.sqrt(_neg_expm1(2.0 * la))
    return xcb, r, gi, sp, a, mult


def _lru_fwd(xin, blk, cw, cb, wr, br, wi, bi, lam, h0, acc_h, d, name):
    n = xin.shape[0]
    nb = n // blk
    reverse = d == 1
    down = not reverse
    has_acc = acc_h is not None

    def bmap(i):
        return nb - 1 - i if reverse else i

    def body(*refs):
        x_ref, cw_ref, cb_ref, wr_ref, br_ref, wi_ref, bi_ref, lam_ref, h0_ref = refs[:9]
        acc_ref = refs[9] if has_acc else None
        outs = refs[10:] if has_acc else refs[9:]
        h_ref, hin_ref, hfin_ref = outs[:3]
        hsum_ref = outs[3] if has_acc else None
        carry = outs[-1]
        i = pl.program_id(0)

        @pl.when(i == 0)
        def _():
            carry[...] = h0_ref[...]

        for g in range(NH):
            cols = pl.ds(g * DH, DH)
            xc = _conv(_conv_taps(x_ref[:, cols]), cw_ref[:, cols], cb_ref[:, cols])
            _, _, gi, _, a, mult = _lru_gates(xc, wr_ref[g], br_ref[:, cols], wi_ref[g], bi_ref[:, cols],
                                              lam_ref[:, cols])
            hin = carry[:, cols]
            h, h_last = _chain_scan(a, mult * gi * xc, hin, down)
            h_ref[:, cols] = h
            if has_acc:
                hsum_ref[:, cols] = h + acc_ref[:, cols]
            hin_ref[0, :, cols] = hin
            carry[:, cols] = h_last
            hfin_ref[:, cols] = h_last

    vec = pl.BlockSpec((1, D), lambda i: (0, 0))
    wsp = pl.BlockSpec((NH, DH, DH), lambda i: (0, 0, 0))
    tok = pl.BlockSpec((blk, D), lambda i: (bmap(i), 0))
    in_specs = [tok, pl.BlockSpec((4, D), lambda i: (0, 0)), vec, wsp, vec, wsp, vec, vec, vec]
    args = [xin, cw, cb, wr, br, wi, bi, lam, h0]
    out_shape = [jax.ShapeDtypeStruct((n, D), F32), jax.ShapeDtypeStruct((nb, 1, D), F32),
                 jax.ShapeDtypeStruct((1, D), F32)]
    out_specs = [tok, pl.BlockSpec((1, 1, D), lambda i: (bmap(i), 0, 0)), vec]
    if has_acc:
        in_specs.append(tok)
        args.append(acc_h)
        out_shape.append(jax.ShapeDtypeStruct((n, D), F32))
        out_specs.append(tok)
    return pl.pallas_call(
        body, name=name, grid=(nb,), out_shape=out_shape, in_specs=in_specs, out_specs=out_specs,
        scratch_shapes=[pltpu.VMEM((1, D), F32)],
    )(*args)


def _lru_bwd(xin, blk, cw, cb, wr, br, wi, bi, lam, h, hin, dh, cg_fin, acc_dx, init, d, name):
    n = xin.shape[0]
    nb = n // blk
    reverse = d == 1
    down = not reverse
    first = blk - 1 if reverse else 0
    has_dh = dh is not None
    has_acc = acc_dx is not None
    has_init = init is not None

    def bmap(i):
        return i if reverse else nb - 1 - i

    def body(*refs):
        (x_ref, cw_ref, cb_ref, wr_ref, br_ref, wi_ref, bi_ref, lam_ref, h_ref, hin_ref, cgf_ref) = refs[:11]
        pos = 11
        dh_ref = acc_ref = None
        iwr_ref = iwi_ref = ivec_ref = None
        if has_dh:
            dh_ref = refs[pos]
            pos += 1
        if has_acc:
            acc_ref = refs[pos]
            pos += 1
        if has_init:
            iwr_ref, iwi_ref, ivec_ref = refs[pos:pos + 3]
            pos += 3
        dx_ref, dwr_ref, dwi_ref, vec_ref, cg0_ref, carry = refs[pos:]
        i = pl.program_id(0)

        @pl.when(i == 0)
        def _():
            carry[...] = cgf_ref[...]
            if has_init:
                dwr_ref[...] = iwr_ref[...]
                dwi_ref[...] = iwi_ref[...]
                vec_ref[...] = ivec_ref[...]
            else:
                dwr_ref[...] = jnp.zeros_like(dwr_ref)
                dwi_ref[...] = jnp.zeros_like(dwi_ref)
                vec_ref[...] = jnp.zeros_like(vec_ref)

        for g in range(NH):
            cols = pl.ds(g * DH, DH)
            cwv = cw_ref[:, cols]
            lam_v = lam_ref[:, cols]
            taps = _conv_taps(x_ref[:, cols])
            xc = _conv(taps, cwv, cb_ref[:, cols])
            wr_g, wi_g = wr_ref[g], wi_ref[g]
            xcb, r, gi, sp, a, mult = _lru_gates(xc, wr_g, br_ref[:, cols], wi_g, bi_ref[:, cols], lam_v)
            hprev = _shift(h_ref[:, cols], 1, hin_ref[0, :, cols], down)
            a_next = _shift(a, 1, 1.0, not down)
            dhv = dh_ref[:, cols] if has_dh else jnp.zeros_like(a)
            e, _ = _chain_scan(a_next, dhv, carry[:, cols], not down)
            cg = a[first:first + 1, :] * e[first:first + 1, :]
            carry[:, cols] = cg
            cg0_ref[:, cols] = cg
            da = e * hprev
            emult = e * mult
            dgi = emult * xc
            dxc = emult * gi
            dla = da * a - (e * gi * xc) * (a * a) / mult
            dr = dla * (-RG_C * sp)
            sneg = 1.0 - _sigmoid(lam_v)
            dpr = dr * r * (1.0 - r)
            dpi = dgi * gi * (1.0 - gi)
            dprb, dpib = dpr.astype(BF16), dpi.astype(BF16)
            dxc = dxc + _dot_nt(dprb, wr_g) + _dot_nt(dpib, wi_g)
            dwr_ref[g] += _dot_tn(xcb, dprb)
            dwi_ref[g] += _dot_tn(xcb, dpib)
            dx = (cwv[0:1, :] * _shift(dxc, 1, 0.0, False) + cwv[1:2, :] * dxc
                  + cwv[2:3, :] * _shift(dxc, 1, 0.0, True) + cwv[3:4, :] * _shift(dxc, 2, 0.0, True))
            if has_acc:
                dx = dx + acc_ref[:, cols]
            dx_ref[:, cols] = dx
            vec_ref[0:1, cols] += jnp.sum(dpr, axis=0, keepdims=True)
            vec_ref[1:2, cols] += jnp.sum(dpi, axis=0, keepdims=True)
            vec_ref[2:3, cols] += jnp.sum(dla * r, axis=0, keepdims=True) * (RG_C * sneg)
            vec_ref[3:4, cols] += jnp.sum(dxc, axis=0, keepdims=True)
            for kk in range(4):
                vec_ref[4 + kk:5 + kk, cols] += jnp.sum(dxc * taps[kk], axis=0, keepdims=True)

    vec = pl.BlockSpec((1, D), lambda i: (0, 0))
    wsp = pl.BlockSpec((NH, DH, DH), lambda i: (0, 0, 0))
    tok = pl.BlockSpec((blk, D), lambda i: (bmap(i), 0))
    vec16 = pl.BlockSpec((16, D), lambda i: (0, 0))
    in_specs = [tok, pl.BlockSpec((4, D), lambda i: (0, 0)), vec, wsp, vec, wsp, vec, vec, tok,
                pl.BlockSpec((1, 1, D), lambda i: (bmap(i), 0, 0)), vec]
    args = [xin, cw, cb, wr, br, wi, bi, lam, h, hin, cg_fin]
    if has_dh:
        in_specs.append(tok)
        args.append(dh)
    if has_acc:
        in_specs.append(tok)
        args.append(acc_dx)
    if has_init:
        in_specs += [wsp, wsp, vec16]
        args += list(init)
    return pl.pallas_call(
        body, name=name, grid=(nb,),
        out_shape=[jax.ShapeDtypeStruct((n, D), F32), jax.ShapeDtypeStruct((NH, DH, DH), F32),
                   jax.ShapeDtypeStruct((NH, DH, DH), F32), jax.ShapeDtypeStruct((16, D), F32),
                   jax.ShapeDtypeStruct((1, D), F32)],
        in_specs=in_specs, out_specs=[tok, wsp, wsp, vec16, vec],
        scratch_shapes=[pltpu.VMEM((1, D), F32)],
    )(*args)


def _merge(z, o_f, o_b, hx, xin, tgt, mod, gn, p_a, p_b, w_out, ln_g, ln_b):
    n = xin.shape[0]
    tm = _row_tile(n, 128)

    def body(z4_ref, z6_ref, z7_ref, z8_ref, of_ref, ob_ref, hx_ref, x_ref, t_ref, mod_ref, gn_ref,
             pa_ref, pb_ref, wo_ref, lg_ref, lnb_ref,
             dr_ref, do_ref, dhx_ref, dz4_ref, dz678_ref, oa_o, obb_o, y_o, dya_o, dyb_o, dout_o, vec_ref):
        @pl.when(pl.program_id(0) == 0)
        def _():
            vec_ref[...] = jnp.zeros_like(vec_ref)

        gt = mod_ref[0:1, 2 * D:3 * D]
        gnv = gn_ref[...]
        o = of_ref[...] + ob_ref[...]
        rs = jnp.concatenate(
            [jnp.broadcast_to(lax.rsqrt(jnp.mean(jnp.square(o[:, h * DH:(h + 1) * DH]), axis=1, keepdims=True)
                                        + RMS_EPS), (tm, DH)) for h in range(NH)], axis=1)
        nrm = o * rs
        rn = nrm * gnv
        z4, z6, z7, z8 = z4_ref[...], z6_ref[...], z7_ref[...], z8_ref[...]
        s4, s6, s7, s8 = _sigmoid(z4), _sigmoid(z6), _sigmoid(z7), _sigmoid(z8)
        sg4, sg6 = z4 * s4, z6 * s6
        hxv = hx_ref[...]
        oa = (rn * sg4).astype(BF16)
        obb = (hxv * sg6).astype(BF16)
        ya = _dot(oa, pa_ref[...])
        yb = _dot(obb, pb_ref[...])
        y = (s7 * ya + s8 * yb).astype(BF16)
        out = _dot(y, wo_ref[...])
        xv = x_ref[...]
        rr = ALPHA * xv + gt * out
        mu = jnp.mean(rr, axis=1, keepdims=True)
        cen = rr - mu
        rstd = lax.rsqrt(jnp.mean(cen * cen, axis=1, keepdims=True) + LN_EPS)
        xhat = cen * rstd
        lg = lg_ref[...]
        err = xhat * lg + lnb_ref[...] - t_ref[...]
        loss_rows = jnp.sum(err * err, axis=1, keepdims=True)
        dxn = err * (1.0 / D)
        dxh = dxn * lg
        dr = rstd * (dxh - jnp.mean(dxh, axis=1, keepdims=True)
                     - xhat * jnp.mean(dxh * xhat, axis=1, keepdims=True))
        dout = (dr * gt).astype(BF16)
        dy = _dot_nt(dout, wo_ref[...])
        dya = (dy * s7).astype(BF16)
        dyb = (dy * s8).astype(BF16)
        doa = _dot_nt(dya, pa_ref[...])
        dob = _dot_nt(dyb, pb_ref[...])
        drn = doa * sg4
        dn = drn * gnv
        dnn = dn * nrm
        corr = jnp.concatenate(
            [jnp.broadcast_to(jnp.mean(dnn[:, h * DH:(h + 1) * DH], axis=1, keepdims=True), (tm, DH))
             for h in range(NH)], axis=1)
        dr_ref[...] = dr
        do_ref[...] = rs * (dn - nrm * corr)
        dhx_ref[...] = dob * sg6
        dz4_ref[...] = (doa * rn * _dsilu(z4, s4)).astype(BF16)
        dz678_ref[:, 0:D] = (dob * hxv * _dsilu(z6, s6)).astype(BF16)
        dz678_ref[:, D:2 * D] = (dy * ya * s7 * (1.0 - s7)).astype(BF16)
        dz678_ref[:, 2 * D:3 * D] = (dy * yb * s8 * (1.0 - s8)).astype(BF16)
        oa_o[...] = oa
        obb_o[...] = obb
        y_o[...] = y
        dya_o[...] = dya
        dyb_o[...] = dyb
        dout_o[...] = dout
        vec_ref[0:1, :] += jnp.sum(dr * out, axis=0, keepdims=True)
        vec_ref[1:2, :] += jnp.sum(dxn * xhat, axis=0, keepdims=True)
        vec_ref[2:3, :] += jnp.sum(dxn, axis=0, keepdims=True)
        vec_ref[3:4, :] += jnp.sum(drn * nrm, axis=0, keepdims=True)
        vec_ref[4:5, :] += jnp.broadcast_to(jnp.sum(loss_rows, axis=0, keepdims=True) * (0.5 / D), (1, D))

    def grp(g):
        return pl.BlockSpec((tm, D), lambda i: (i, g))

    tok = pl.BlockSpec((tm, D), lambda i: (i, 0))
    vec = pl.BlockSpec((1, D), lambda i: (0, 0))
    wsp = pl.BlockSpec((D, D), lambda i: (0, 0))
    return pl.pallas_call(
        body, name="merge", grid=(n // tm,),
        out_shape=[jax.ShapeDtypeStruct((n, D), F32)] * 3
        + [jax.ShapeDtypeStruct((n, D), BF16), jax.ShapeDtypeStruct((n, 3 * D), BF16)]
        + [jax.ShapeDtypeStruct((n, D), BF16)] * 6 + [jax.ShapeDtypeStruct((8, D), F32)],
        in_specs=[grp(4), grp(6), grp(7), grp(8), tok, tok, tok, tok, tok,
                  pl.BlockSpec((8, 3 * D), lambda i: (0, 0)), vec, wsp, wsp, wsp, vec, vec],
        out_specs=[tok, tok, tok, tok, pl.BlockSpec((tm, 3 * D), lambda i: (i, 0))] + [tok] * 6
        + [pl.BlockSpec((8, D), lambda i: (0, 0))],
    )(z, z, z, z, o_f, o_b, hx, xin, tgt, mod, gn, p_a, p_b, w_out, ln_g, ln_b)


def _wmod_grad(c_t, cctx_t, dmx_loc, dmc_loc, name):
    n = dmx_loc.shape[1]

    def body(ct_ref, cc_ref, dmx_ref, dmc_ref, o_ref):
        ct = ct_ref[...]
        sct = ct * _sigmoid(ct)
        cc = cc_ref[...]
        scc = cc * _sigmoid(cc)
        dmc = dmc_ref[0:1, :]
        for b in range(1, NDEV):
            dmc = dmc + dmc_ref[b:b + 1, :]
        acc = scc * dmc
        for b in range(NDEV):
            acc = acc + sct[:, b:b + 1] * dmx_ref[b:b + 1, :]
        o_ref[...] = acc

    return pl.pallas_call(body, name=name, out_shape=jax.ShapeDtypeStruct((D, n), F32))(c_t, cctx_t, dmx_loc, dmc_loc)


def _cctx_grad(dmc_sum, w_mod_full, cctx_row, name):
    def body(d_ref, w_ref, c_ref, o_ref):
        cv = c_ref[...]
        s = _sigmoid(cv)
        o_ref[...] = _dot_nt(d_ref[...].astype(BF16), w_ref[...]) * _dsilu(cv, s)

    return pl.pallas_call(body, name=name, out_shape=jax.ShapeDtypeStruct((8, D), F32))(dmc_sum, w_mod_full, cctx_row)


def _to_colmajor(t, rows):
    return t.reshape(rows, GRID_W, D).transpose(1, 0, 2).reshape(rows * GRID_W, D)


def _to_raster(t, rows):
    return t.reshape(GRID_W, rows, D).transpose(1, 0, 2).reshape(rows * GRID_W, D)


def _local_cols(t, me, width):
    return lax.dynamic_slice_in_dim(t, me * width, width, axis=t.ndim - 1)


def kernel(x, c, ctx, c_ctx, w_mod, b_mod, w_in, b_in, lb_logits, norm_a_g, conv_w, conv_b, w_r, b_r, w_i, b_i, lam, p_a, p_b, w_out, ln_g, ln_b, loss_target, m_c_ctx, m_w_mod, m_b_mod, m_w_in, m_b_in, m_lb_logits, m_norm_a_g, m_conv_w, m_conv_b, m_w_r, m_b_r, m_w_i, m_b_i, m_lam, m_p_a, m_p_b, m_w_out, m_ln_g, m_ln_b, v_c_ctx, v_w_mod, v_b_mod, v_w_in, v_b_in, v_lb_logits, v_norm_a_g, v_conv_w, v_conv_b, v_w_r, v_b_r, v_w_i, v_b_i, v_lam, v_p_a, v_p_b, v_w_out, v_ln_g, v_ln_b):
    me = _my_index()
    xs, cs, tgt = x[0], ctx[0], loss_target[0]
    t_len, c_len = xs.shape[0], cs.shape[0]
    rows = t_len // GRID_W
    wcols = w_in.shape[2]
    mcols = w_mod.shape[2]

    small = jnp.concatenate([lb_logits.reshape(4, DH), conv_w[0], b_r[0], b_i[0], lam[0], jnp.zeros((2, DH), F32),
                             c.reshape(8, DH)], axis=0)
    g_small, g_win, g_wmod, g_pa, g_pb, g_wo = _all_gather(
        [small, w_in[0].astype(BF16), w_mod[0].astype(BF16), p_a[0].astype(BF16), p_b[0].astype(BF16),
         w_out[0].astype(BF16)], "gather_params")

    def full_rows(lo, hi):
        return g_small[:, lo:hi, :].transpose(1, 0, 2).reshape(hi - lo, D)

    lbl_f, cw_f, br_f, bi_f, lam_f = full_rows(0, 4), full_rows(4, 8), full_rows(8, 10), full_rows(10, 12), full_rows(12, 14)
    c_all = g_small[:, 16:24, :].reshape(NDEV, D)
    w_in_f = g_win.transpose(1, 0, 2).reshape(D, NGRP * D)
    w_mod_f = g_wmod.transpose(1, 0, 2).reshape(D, 3 * D)
    p_a_f, p_b_f, w_out_f = g_pa.reshape(D, D), g_pb.reshape(D, D), g_wo.reshape(D, D)
    w_r_b, w_i_b = w_r[0].astype(BF16), w_i[0].astype(BF16)

    cc = jnp.concatenate([c.reshape(1, D), c_ctx.reshape(1, D), jnp.zeros((6, D), F32)], axis=0)
    lbl_p = jnp.concatenate([lbl_f.reshape(2, 2, D), jnp.zeros((2, 6, D), F32)], axis=1)
    mod, lb = _prep(cc, w_mod_f, b_mod, lbl_p)
    u_x = _modulate(xs, mod, 0, "modulate_x")
    u_c = _modulate(cs, mod, 1, "modulate_c")
    z_x = _mm_bias(u_x, w_in_f, b_in, "inproj_x")
    z_c = _mm_bias(u_c, w_in_f, b_in, "inproj_c")

    zero_s = jnp.zeros((NH, DH, DH), F32)
    zero_v = jnp.zeros((1, D), F32)
    gla = {}
    for d in (0, 1):
        _, ssc, sfc = _gla_fwd(z_c, lb, zero_s, d, f"gla_fwd_c{d}")
        o_d, ssx, _ = _gla_fwd(z_x, lb, sfc, d, f"gla_fwd_x{d}")
        gla[d] = (ssc, ssx, o_d)

    x5_c = z_c[:, 5 * D:6 * D]
    x5_x = _to_colmajor(z_x[:, 5 * D:6 * D], rows)
    cb2 = conv_b.reshape(1, D)
    lru = {}
    h_sum = None
    for d in (0, 1):
        prm = (cw_f, cb2, w_r_b[d], br_f[d:d + 1], w_i_b[d], bi_f[d:d + 1], lam_f[d:d + 1])
        h_c, hin_c, hfin_c = _lru_fwd(x5_c, c_len, *prm, zero_v, None, d, f"lru_fwd_c{d}")
        h_x, hin_x, _, *h_sum = _lru_fwd(x5_x, rows, *prm, hfin_c, lru[0][3] if d else None, d, f"lru_fwd_x{d}")
        lru[d] = (prm, h_c, hin_c, h_x, hin_x)
    hx = _to_raster(h_sum[0], rows)

    gn = jnp.tile(norm_a_g.reshape(1, DH), (1, NH))
    (dr, do, dhx, dz4, dz678, oa, obb, yb16, dya, dyb, dout, mvec) = _merge(
        z_x, gla[0][2], gla[1][2], hx, xs, tgt, mod, gn, p_a_f, p_b_f, w_out_f, ln_g, ln_b)

    dhx_cm = _to_colmajor(dhx, rows)
    lru_dx_x = lru_dx_c = None
    lru_init = None
    for d in (0, 1):
        prm, h_c, hin_c, h_x, hin_x = lru[d]
        lru_dx_x, dwr, dwi, lvec, cg0 = _lru_bwd(x5_x, rows, *prm, h_x, hin_x, dhx_cm, zero_v, lru_dx_x, None, d,
                                                 f"lru_bwd_x{d}")
        lru_dx_c, dwr, dwi, lvec, _ = _lru_bwd(x5_c, c_len, *prm, h_c, hin_c, None, cg0, lru_dx_c, (dwr, dwi, lvec), d,
                                               f"lru_bwd_c{d}")
        lru[d] = (dwr, dwi, lvec)
    dz5_x = _to_raster(lru_dx_x, rows).astype(BF16)
    dz5_c = lru_dx_c.astype(BF16)

    gq_x = gv_x = gq_c = gv_c = None
    dzf_x, dzf_c, dlb = {}, {}, {}
    for d in (0, 1):
        ssc, ssx, _ = gla[d]
        gq_x, dzf_x[d], gv_x, dlb_x, ds0 = _gla_bwd(z_x, lb, ssx, do, zero_s, gq_x, gv_x, d, f"gla_bwd_x{d}")
        gq_c, dzf_c[d], gv_c, dlb_c, _ = _gla_bwd(z_c, lb, ssc, None, ds0, gq_c, gv_c, d, f"gla_bwd_c{d}")
        dlb[d] = dlb_x[0:1] + dlb_c[0:1]

    bf = lambda t: t.astype(BF16)
    dz_x = jnp.concatenate([bf(gq_x), bf(dzf_x[0]), bf(dzf_x[1]), bf(gv_x), dz4, dz5_x, dz678], axis=1)
    zc0 = jnp.zeros((c_len, D), BF16)
    dz_c = jnp.concatenate([bf(gq_c), bf(dzf_c[0]), bf(dzf_c[1]), bf(gv_c), zc0, dz5_c, zc0, zc0, zc0], axis=1)
    dwin_c, dbin_c = _mm_tn(u_c, dz_c, None, "dwin_c", with_colsum=True)
    dpa = _mm_tn(oa, dya, None, "dpa", out_dtype=BF16)
    dpb = _mm_tn(obb, dyb, None, "dpb", out_dtype=BF16)
    dwo = _mm_tn(yb16, dout, None, "dwout", out_dtype=BF16)

    wr_pack = jnp.concatenate([lru[0][0], lru[1][0], lru[0][1], lru[1][1]], axis=0).reshape(4 * NH * DH, DH)
    r_win, dbin, r_pa, r_pb, r_wo, r_wri = _dwin_exchange(
        u_x, dz_x, dwin_c, dbin_c, [dpa, dpb, dwo, wr_pack], [0, 0, 0, 0], "dwin_exchange")
    grad_x, xvec = _input_grad(dz_x, w_in_f, xs, dr, mod, 0, "input_grad_x")
    _, cvec = _input_grad(dz_c, w_in_f, cs, None, mod, 1, "input_grad_c")
    wri_piece = _sum_rows(r_wri, "sum_w_ri_piece")
    g_w_in, d_w_in, nm_w_in, nv_w_in = _sum_adamw(r_win, w_in, m_w_in, v_w_in, "update_w_in")
    g_p_a, d_p_a, nm_p_a, nv_p_a = _sum_adamw(r_pa, p_a, m_p_a, v_p_a, "update_p_a")
    g_p_b, d_p_b, nm_p_b, nv_p_b = _sum_adamw(r_pb, p_b, m_p_b, v_p_b, "update_p_b")
    g_w_out, d_w_out, nm_w_out, nv_w_out = _sum_adamw(r_wo, w_out, m_w_out, v_w_out, "update_w_out")

    dlb_rows = jnp.concatenate([dlb[0], dlb[1]], axis=0)
    pack = jnp.concatenate([
        xvec[0:1], xvec[1:2], mvec[0:1],
        cvec[0:1], cvec[1:2], jnp.zeros((1, D), F32),
        dbin.reshape(NGRP, D),
        mvec[3:4], mvec[1:2], mvec[2:3],
        lru[0][2][0:8], lru[1][2][0:3],
        lru[1][2][3:8],
        dlb_rows,
        mvec[4:5],
        jnp.zeros((3, D), F32)], axis=0)
    g_pack, g_wri = _all_gather([pack, wri_piece], "gather_small_grads")
    tot = _sum_rows(g_pack, "sum_small_grads")

    loss = tot[36, 0]
    dmx = g_pack[:, 0:3, :].reshape(NDEV, 3 * D)
    dmc = g_pack[:, 3:6, :].reshape(NDEV, 3 * D)
    grad_w_mod = _wmod_grad(c_all.T, c_ctx.reshape(D, 1), _local_cols(dmx, me, mcols), _local_cols(dmc, me, mcols),
                            "grad_w_mod").reshape(1, D, mcols)
    grad_b_mod = (tot[0:3] + tot[3:6]).reshape(1, 3 * D)
    dmc_sum = jnp.concatenate([tot[3:6].reshape(1, 3 * D), jnp.zeros((7, 3 * D), F32)], axis=0)
    grad_c_ctx = _cctx_grad(dmc_sum, w_mod_f, cc[1:2], "grad_c_ctx")[0]
    grad_b_in = tot[6:15].reshape(1, NGRP * D)
    grad_norm_a_g = tot[15].reshape(NH, DH).sum(axis=0).reshape(1, DH)
    grad_ln_g, grad_ln_b = tot[16:17], tot[17:18]
    grad_conv_b = tot[21:22] + tot[29:30]
    grad_conv_w = _local_cols(tot[22:26] + tot[30:34], me, DH).reshape(1, 4, DH)
    grad_b_r = _local_cols(jnp.stack([tot[18], tot[26]]), me, DH).reshape(1, 2, DH)
    grad_b_i = _local_cols(jnp.stack([tot[19], tot[27]]), me, DH).reshape(1, 2, DH)
    grad_lam = _local_cols(jnp.stack([tot[20], tot[28]]), me, DH).reshape(1, 2, DH)
    lb_loc = _local_cols(lb[0:2], me, DH)
    dl0 = _local_cols(tot[34:36], me, DH) * lb_loc * (1.0 - lb_loc)
    grad_lb_logits = jnp.stack([dl0, -dl0])

    half = 2 * NH * DH
    g_ri = g_wri.reshape(2 * half, DH)
    grad_w_r, grad_w_i = g_ri[:half].reshape(w_r.shape), g_ri[half:].reshape(w_i.shape)
    d_w_r, nm_w_r, nv_w_r = _adamw(grad_w_r, w_r, m_w_r, v_w_r, "update_w_r")
    d_w_i, nm_w_i, nv_w_i = _adamw(grad_w_i, w_i, m_w_i, v_w_i, "update_w_i")

    d_w_mod, nm_w_mod, nv_w_mod = _adamw(grad_w_mod, w_mod, m_w_mod, v_w_mod, "update_w_mod")

    small_items = [
        (grad_c_ctx, c_ctx, m_c_ctx, v_c_ctx), (grad_b_mod, b_mod, m_b_mod, v_b_mod),
        (grad_b_in, b_in, m_b_in, v_b_in), (grad_lb_logits, lb_logits, m_lb_logits, v_lb_logits),
        (grad_norm_a_g, norm_a_g, m_norm_a_g, v_norm_a_g), (grad_conv_w, conv_w, m_conv_w, v_conv_w),
        (grad_conv_b, conv_b, m_conv_b, v_conv_b), (grad_b_r, b_r, m_b_r, v_b_r), (grad_b_i, b_i, m_b_i, v_b_i),
        (grad_lam, lam, m_lam, v_lam), (grad_ln_g, ln_g, m_ln_g, v_ln_g), (grad_ln_b, ln_b, m_ln_b, v_ln_b)]
    cat = lambda k: jnp.concatenate([it[k].reshape(-1, DH) for it in small_items], axis=0)
    sd, sm, sv = _adamw(cat(0), cat(1), cat(2), cat(3), "update_small")
    small_out = []
    off = 0
    for it in small_items:
        shp = it[1].shape
        nrow = it[1].size // DH
        small_out.append(tuple(t[off:off + nrow].reshape(shp) for t in (sd, sm, sv)))
        off += nrow
    (o_c_ctx, o_b_mod, o_b_in, o_lb, o_norm, o_conv_w, o_conv_b, o_b_r, o_b_i, o_lam, o_ln_g, o_ln_b) = small_out

    grads = [grad_c_ctx.reshape(c_ctx.shape), grad_w_mod, grad_b_mod, g_w_in, grad_b_in, grad_lb_logits, grad_norm_a_g,
             grad_conv_w, grad_conv_b, grad_w_r, grad_b_r, grad_w_i, grad_b_i, grad_lam, g_p_a, g_p_b, g_w_out,
             grad_ln_g, grad_ln_b]
    per_kind = []
    for k in range(3):
        per_kind.append([
            o_c_ctx[k], (d_w_mod, nm_w_mod, nv_w_mod)[k], o_b_mod[k], (d_w_in, nm_w_in, nv_w_in)[k], o_b_in[k], o_lb[k],
            o_norm[k], o_conv_w[k], o_conv_b[k], (d_w_r, nm_w_r, nv_w_r)[k], o_b_r[k], (d_w_i, nm_w_i, nv_w_i)[k],
            o_b_i[k], o_lam[k], (d_p_a, nm_p_a, nv_p_a)[k], (d_p_b, nm_p_b, nv_p_b)[k], (d_w_out, nm_w_out, nv_w_out)[k],
            o_ln_g[k], o_ln_b[k]])
    return (loss, grad_x.reshape(x.shape), *grads, *per_kind[0], *per_kind[1], *per_kind[2])
```

```python
import functools

import jax
import jax.numpy as jnp
from jax import lax
from jax.experimental import pallas as pl
from jax.experimental.pallas import tpu as pltpu

F32 = jnp.float32
BF16 = jnp.bfloat16

D = 1024
NH = 8
DH = 128
CHUNK = 64
GLA_HEADS_PER_STEP = 4
GRID_W = 64
NGRP = 9
NDEV = 8
RG_C = 8.0
ALPHA = 2.0 ** 0.25
LN_EPS = 1e-5
RMS_EPS = 1e-6
Q_SCALE = DH ** -0.5
ADAM_LR, ADAM_B1, ADAM_B2, ADAM_EPS, ADAM_WD, ADAM_STEP = 1e-3, 0.9, 0.999, 1e-8, 0.01, 10
ADAM_C1 = 1.0 / (1.0 - ADAM_B1 ** ADAM_STEP)
ADAM_C2 = 1.0 / (1.0 - ADAM_B2 ** ADAM_STEP)

ANY = pl.BlockSpec(memory_space=pl.ANY)


def _sigmoid(t):
    return 1.0 / (1.0 + jnp.exp(-t))


def _dsilu(t, s):
    return s * (1.0 + t * (1.0 - s))


def _dot(a, b):
    return jnp.dot(a, b, preferred_element_type=F32)


def _dot_nt(a, b):
    return lax.dot_general(a, b, (((1,), (1,)), ((), ())), preferred_element_type=F32)


def _dot_tn(a, b):
    return lax.dot_general(a, b, (((0,), (0,)), ((), ())), preferred_element_type=F32)


def _my_index():
    return 4 * lax.axis_index("x") + 2 * lax.axis_index("y") + lax.axis_index("c")


def _dev_tuple(j):
    return (j >> 2, (j >> 1) & 1, j & 1)


def _all_gather(shards, name):
    n = len(shards)

    def body(*refs):
        ins, outs = refs[:n], refs[n:2 * n]
        send_sems, recv_sems, loc_sems = refs[2 * n:]
        me = _my_index()
        for a in range(n):
            pltpu.make_async_copy(ins[a], outs[a].at[me], loc_sems.at[a]).start()
        for j in range(NDEV):
            @pl.when(me != j)
            def _():
                for a in range(n):
                    pltpu.make_async_remote_copy(
                        src_ref=ins[a], dst_ref=outs[a].at[me],
                        send_sem=send_sems.at[a * NDEV + j], recv_sem=recv_sems.at[a * NDEV + me],
                        device_id=_dev_tuple(j), device_id_type=pl.DeviceIdType.MESH).start()
        for j in range(NDEV):
            @pl.when(me != j)
            def _():
                for a in range(n):
                    pltpu.make_async_remote_copy(
                        src_ref=ins[a], dst_ref=outs[a].at[j],
                        send_sem=send_sems.at[a * NDEV + j], recv_sem=recv_sems.at[a * NDEV + j],
                        device_id=_dev_tuple(j), device_id_type=pl.DeviceIdType.MESH).wait()
        for a in range(n):
            pltpu.make_async_copy(ins[a], outs[a].at[me], loc_sems.at[a]).wait()

    return pl.pallas_call(
        body, name=name,
        out_shape=[jax.ShapeDtypeStruct((NDEV,) + s.shape, s.dtype) for s in shards],
        in_specs=[ANY] * n, out_specs=[ANY] * n,
        scratch_shapes=[pltpu.SemaphoreType.DMA((n * NDEV,)), pltpu.SemaphoreType.DMA((n * NDEV,)),
                        pltpu.SemaphoreType.DMA((n,))],
    )(*shards)


_STEP_MASKS = ((2, 4, 6, 3, 5, 7, 1, 0), (4, 2, 6, 5, 3, 7, 1, 0))
_GATHER_MASKS = ((0, 1, 2, 4, 3, 5, 6, 7), (0, 1, 4, 2, 5, 3, 6, 7))


def _peer_schedule(table):
    tab = jnp.array(table, jnp.int32)
    return jnp.bitwise_xor(_my_index(), tab[lax.axis_index("c")])


def _step_peer(s, table=_STEP_MASKS):
    def pick(row):
        if isinstance(s, int):
            return jnp.int32(row[s])
        m = jnp.int32(row[NDEV - 1])
        for t in range(NDEV - 2, -1, -1):
            m = jnp.where(s == t, jnp.int32(row[t]), m)
        return m
    mask = jnp.where(lax.axis_index("c") == 0, pick(table[0]), pick(table[1]))
    return jnp.bitwise_xor(_my_index(), mask)


def _dev_of(p):
    return (p // 4, (p // 2) % 2, p % 2)


def _dwin_exchange(u, dz, init, cs_init, extras, splits, name):
    m, ka = u.shape
    n = dz.shape[1]
    pc = n // NDEV
    tk = _row_tile(m, 512)
    nk = m // tk
    ne = len(extras)
    pieces = []
    for a in range(ne):
        r, c = extras[a].shape
        pieces.append((r // NDEV, c) if splits[a] == 0 else (r, c // NDEV))

    def piece_ref(ref, a, j):
        pr, pcol = pieces[a]
        if splits[a] == 0:
            return ref.at[pl.ds(j * pr, pr), :]
        return ref.at[:, pl.ds(j * pcol, pcol)]

    def body(pidx_ref, u_ref, dz_ref, init_ref, csi_ref, *rest):
        ex_in = rest[:ne]
        rwin, cs_ref = rest[ne], rest[ne + 1]
        ex_out = rest[ne + 2:2 * ne + 2]
        acc, sbuf, wsend, wrecv, wloc, esend, erecv, eloc = rest[2 * ne + 2:]
        s, k = pl.program_id(0), pl.program_id(1)
        me = _my_index()

        def extra_local(a, j):
            return pltpu.make_async_copy(piece_ref(ex_in[a], a, j), ex_out[a].at[j], eloc.at[a])

        def extra_remote(a, j, slot):
            return pltpu.make_async_remote_copy(
                src_ref=piece_ref(ex_in[a], a, j), dst_ref=ex_out[a].at[slot],
                send_sem=esend.at[a * NDEV + j], recv_sem=erecv.at[a * NDEV + slot],
                device_id=_dev_tuple(j), device_id_type=pl.DeviceIdType.MESH)

        def slab_copy(slot, p):
            return pltpu.make_async_remote_copy(
                src_ref=sbuf.at[slot], dst_ref=rwin.at[me], send_sem=wsend.at[slot], recv_sem=wrecv.at[me],
                device_id=_dev_of(p), device_id_type=pl.DeviceIdType.MESH)

        @pl.when((s == 0) & (k == 0))
        def _():
            for j in range(NDEV):
                @pl.when(me == j)
                def _():
                    for a in range(ne):
                        extra_local(a, j).start()

                @pl.when(me != j)
                def _():
                    for a in range(ne):
                        extra_remote(a, j, me).start()

        @pl.when(k == 0)
        def _():
            acc[...] = init_ref[...]
            cs_ref[...] = csi_ref[...]

        bv = dz_ref[...]
        acc[...] += _dot_tn(u_ref[...], bv)
        cs_ref[...] += jnp.sum(bv.astype(F32), axis=0, keepdims=True)

        @pl.when(k == nk - 1)
        def _():
            slot = s % 2

            @pl.when(s >= 2)
            def _():
                slab_copy(slot, me).wait_send()

            sbuf[slot] = acc[...].astype(BF16)

            @pl.when(s < NDEV - 1)
            def _():
                slab_copy(slot, _step_peer(s)).start()

            @pl.when(s == NDEV - 1)
            def _():
                own = pltpu.make_async_copy(sbuf.at[slot], rwin.at[me], wloc.at[0])
                own.start()
                slab_copy(1 - slot, me).wait_send()
                for j in range(NDEV):
                    @pl.when(me != j)
                    def _():
                        pltpu.make_async_remote_copy(
                            src_ref=sbuf.at[0], dst_ref=rwin.at[j], send_sem=wsend.at[0], recv_sem=wrecv.at[j],
                            device_id=_dev_tuple(j), device_id_type=pl.DeviceIdType.MESH).wait_recv()
                        for a in range(ne):
                            extra_remote(a, j, j).wait()

                    @pl.when(me == j)
                    def _():
                        for a in range(ne):
                            extra_local(a, j).wait()
                own.wait()

    grid_spec = pltpu.PrefetchScalarGridSpec(
        num_scalar_prefetch=1, grid=(NDEV, nk),
        in_specs=[pl.BlockSpec((tk, ka), lambda s, k, pidx: (k, 0)),
                  pl.BlockSpec((tk, pc), lambda s, k, pidx: (k, pidx[s])),
                  pl.BlockSpec((ka, pc), lambda s, k, pidx: (0, pidx[s])),
                  pl.BlockSpec((1, pc), lambda s, k, pidx: (0, pidx[s]))] + [ANY] * ne,
        out_specs=[ANY, pl.BlockSpec((1, pc), lambda s, k, pidx: (0, pidx[s]))] + [ANY] * ne,
        scratch_shapes=[pltpu.VMEM((ka, pc), F32), pltpu.VMEM((2, ka, pc), BF16),
                        pltpu.SemaphoreType.DMA((2,)), pltpu.SemaphoreType.DMA((NDEV,)), pltpu.SemaphoreType.DMA((1,)),
                        pltpu.SemaphoreType.DMA((ne * NDEV,)), pltpu.SemaphoreType.DMA((ne * NDEV,)),
                        pltpu.SemaphoreType.DMA((ne,))])
    return pl.pallas_call(
        body, name=name, grid_spec=grid_spec,
        out_shape=[jax.ShapeDtypeStruct((NDEV, ka, pc), BF16), jax.ShapeDtypeStruct((1, n), F32)]
        + [jax.ShapeDtypeStruct((NDEV,) + pieces[a], extras[a].dtype) for a in range(ne)],
    )(_peer_schedule(_STEP_MASKS), u, dz, init, cs_init, *extras)


def _inproj_gather(u, w_loc, bias, extras, name):
    m, k = u.shape
    pc = w_loc.shape[1]
    n = pc * NDEV
    tm = _row_tile(m, 512)
    ni = m // tm
    ne = len(extras)

    def body(pidx_ref, u_ref, b_ref, wl_ref, *rest):
        ex_in = rest[:ne]
        z_ref, wall = rest[ne], rest[ne + 1]
        ex_out = rest[ne + 2:2 * ne + 2]
        wbuf, wsend, wrecv, ldsem, ownsem, esend, erecv, eloc = rest[2 * ne + 2:]
        s, i = pl.program_id(0), pl.program_id(1)
        me = _my_index()

        def shard_push(t):
            return pltpu.make_async_remote_copy(
                src_ref=wl_ref, dst_ref=wall.at[me], send_sem=wsend.at[t], recv_sem=wrecv.at[me],
                device_id=_dev_of(_step_peer(t, _GATHER_MASKS)), device_id_type=pl.DeviceIdType.MESH)

        def extra_push(a, t):
            return pltpu.make_async_remote_copy(
                src_ref=ex_in[a], dst_ref=ex_out[a].at[me], send_sem=esend.at[a * NDEV + t],
                recv_sem=erecv.at[a * NDEV + me],
                device_id=_dev_of(_step_peer(t, _GATHER_MASKS)), device_id_type=pl.DeviceIdType.MESH)

        def load(slot, src):
            return pltpu.make_async_copy(src, wbuf.at[slot], ldsem.at[slot])

        own = pltpu.make_async_copy(wl_ref, wall.at[me], ownsem.at[0])

        @pl.when((s == 0) & (i == 0))
        def _():
            own.start()
            load(0, wl_ref).start()
            for t in range(1, NDEV):
                shard_push(t).start()
            for a in range(ne):
                pltpu.make_async_copy(ex_in[a], ex_out[a].at[me], eloc.at[a]).start()
                for t in range(1, NDEV):
                    extra_push(a, t).start()

        @pl.when((i == ni // 2) & (s < NDEV - 1))
        def _():
            nxt = _step_peer(s + 1, _GATHER_MASKS)
            pltpu.make_async_remote_copy(
                src_ref=wl_ref, dst_ref=wall.at[nxt], send_sem=wsend.at[0], recv_sem=wrecv.at[nxt],
                device_id=_dev_of(nxt), device_id_type=pl.DeviceIdType.MESH).wait_recv()
            load((s + 1) % 2, wall.at[nxt]).start()

        @pl.when(i == 0)
        def _():
            load(s % 2, wl_ref).wait()

        z_ref[...] = _dot(u_ref[...], wbuf[s % 2]) + b_ref[...]

        @pl.when((s == NDEV - 1) & (i == ni - 1))
        def _():
            own.wait()
            for t in range(1, NDEV):
                shard_push(t).wait_send()
            for a in range(ne):
                pltpu.make_async_copy(ex_in[a], ex_out[a].at[me], eloc.at[a]).wait()
                for t in range(1, NDEV):
                    extra_push(a, t).wait_send()
            for j in range(NDEV):
                @pl.when(me != j)
                def _():
                    for a in range(ne):
                        pltpu.make_async_remote_copy(
                            src_ref=ex_in[a], dst_ref=ex_out[a].at[j], send_sem=esend.at[a * NDEV],
                            recv_sem=erecv.at[a * NDEV + j],
                            device_id=_dev_tuple(j), device_id_type=pl.DeviceIdType.MESH).wait_recv()

    grid_spec = pltpu.PrefetchScalarGridSpec(
        num_scalar_prefetch=1, grid=(NDEV, ni),
        in_specs=[pl.BlockSpec((tm, k), lambda s, i, pidx: (i, 0)),
                  pl.BlockSpec((1, pc), lambda s, i, pidx: (0, pidx[s])), ANY] + [ANY] * ne,
        out_specs=[pl.BlockSpec((tm, pc), lambda s, i, pidx: (i, pidx[s])), ANY] + [ANY] * ne,
        scratch_shapes=[pltpu.VMEM((2, k, pc), BF16),
                        pltpu.SemaphoreType.DMA((NDEV,)), pltpu.SemaphoreType.DMA((NDEV,)),
                        pltpu.SemaphoreType.DMA((2,)), pltpu.SemaphoreType.DMA((1,)),
                        pltpu.SemaphoreType.DMA((ne * NDEV,)), pltpu.SemaphoreType.DMA((ne * NDEV,)),
                        pltpu.SemaphoreType.DMA((ne,))])
    return pl.pallas_call(
        body, name=name, grid_spec=grid_spec,
        out_shape=[jax.ShapeDtypeStruct((m, n), F32), jax.ShapeDtypeStruct((NDEV, k, pc), w_loc.dtype)]
        + [jax.ShapeDtypeStruct((NDEV,) + e.shape, e.dtype) for e in extras],
    )(_peer_schedule(_GATHER_MASKS), u, bias, w_loc, *extras)


def _adam_math(g, w, m, v):
    m2 = ADAM_B1 * m + (1.0 - ADAM_B1) * g
    v2 = ADAM_B2 * v + (1.0 - ADAM_B2) * (g * g)
    delta = -ADAM_LR * ((m2 * ADAM_C1) / (jnp.sqrt(v2 * ADAM_C2) + ADAM_EPS) + ADAM_WD * w)
    return delta, m2, v2


def _row_tile(r, cap):
    t = min(r, cap)
    while r % t:
        t //= 2
    return t


def _adamw(g, w, m, v, name):
    shape = w.shape
    cols = shape[-1] if w.ndim >= 2 and shape[-1] % 128 == 0 else 128
    g2, w2, m2, v2 = (t.reshape(-1, cols) for t in (g, w, m, v))
    r = g2.shape[0]
    tr = _row_tile(r, 256)

    def body(g_ref, w_ref, m_ref, v_ref, d_ref, mo_ref, vo_ref):
        d, mm, vv = _adam_math(g_ref[...], w_ref[...], m_ref[...], v_ref[...])
        d_ref[...] = d
        mo_ref[...] = mm
        vo_ref[...] = vv

    spec = pl.BlockSpec((tr, cols), lambda i: (i, 0))
    outs = pl.pallas_call(
        body, name=name, grid=(r // tr,),
        out_shape=[jax.ShapeDtypeStruct((r, cols), F32)] * 3,
        in_specs=[spec] * 4, out_specs=[spec] * 3,
    )(g2, w2, m2, v2)
    return tuple(o.reshape(shape) for o in outs)


def _sum_adamw(parts, w, m, v, name):
    _, r, c = parts.shape
    shape = w.shape
    w2, m2, v2 = (t.reshape(r, c) for t in (w, m, v))
    tr = _row_tile(r, 128)

    def body(p_ref, w_ref, m_ref, v_ref, g_ref, d_ref, mo_ref, vo_ref):
        g = p_ref[0].astype(F32)
        for k in range(1, NDEV):
            g = g + p_ref[k].astype(F32)
        d, mm, vv = _adam_math(g, w_ref[...], m_ref[...], v_ref[...])
        g_ref[...] = g
        d_ref[...] = d
        mo_ref[...] = mm
        vo_ref[...] = vv

    spec = pl.BlockSpec((tr, c), lambda i: (i, 0))
    outs = pl.pallas_call(
        body, name=name, grid=(r // tr,),
        out_shape=[jax.ShapeDtypeStruct((r, c), F32)] * 4,
        in_specs=[pl.BlockSpec((NDEV, tr, c), lambda i: (0, i, 0))] + [spec] * 3, out_specs=[spec] * 4,
    )(parts, w2, m2, v2)
    return tuple(o.reshape(shape) for o in outs)


def _sum_rows(parts, name):
    _, r, c = parts.shape

    def body(p_ref, o_ref):
        g = p_ref[0]
        for k in range(1, NDEV):
            g = g + p_ref[k]
        o_ref[...] = g

    return pl.pallas_call(
        body, name=name, out_shape=jax.ShapeDtypeStruct((r, c), F32),
    )(parts)


def _prep(cc, w_mod_full, b_mod, lbl):
    def body(cc_ref, w_ref, b_ref, l_ref, mod_ref, lb_ref):
        t = cc_ref[...]
        s = (t * _sigmoid(t)).astype(BF16)
        mod_ref[...] = _dot(s, w_ref[...]) + b_ref[...]
        lb_ref[...] = _sigmoid(l_ref[0] - l_ref[1])

    return pl.pallas_call(
        body, name="prep",
        out_shape=[jax.ShapeDtypeStruct((8, 3 * D), F32), jax.ShapeDtypeStruct((8, D), F32)],
    )(cc, w_mod_full, b_mod, lbl)


def _modulate(xin, mod, row, name):
    n = xin.shape[0]
    tm = _row_tile(n, 512)

    def body(x_ref, mod_ref, u_ref):
        sh = mod_ref[row:row + 1, 0:D]
        sc = mod_ref[row:row + 1, D:2 * D]
        u_ref[...] = (x_ref[...] * (1.0 + sc) + sh).astype(BF16)

    return pl.pallas_call(
        body, name=name, grid=(n // tm,),
        out_shape=jax.ShapeDtypeStruct((n, D), BF16),
        in_specs=[pl.BlockSpec((tm, D), lambda i: (i, 0)), pl.BlockSpec((8, 3 * D), lambda i: (0, 0))],
        out_specs=pl.BlockSpec((tm, D), lambda i: (i, 0)),
    )(xin, mod)


def _mm_bias(a, w_all, bias, name):
    m, k = a.shape
    tn = w_all.shape[2]
    n = tn * NDEV
    tm = _row_tile(m, 512)

    def body(a_ref, b_ref, bias_ref, o_ref):
        o_ref[...] = _dot(a_ref[...], b_ref[0]) + bias_ref[...]

    return pl.pallas_call(
        body, name=name, grid=(NDEV, m // tm),
        out_shape=jax.ShapeDtypeStruct((m, n), F32),
        in_specs=[pl.BlockSpec((tm, k), lambda j, i: (i, 0)), pl.BlockSpec((1, k, tn), lambda j, i: (j, 0, 0)),
                  pl.BlockSpec((1, tn), lambda j, i: (0, j))],
        out_specs=pl.BlockSpec((tm, tn), lambda j, i: (i, j)),
    )(a, w_all, bias)


def _mm_tn(a, b, init, name, with_colsum=False, colsum_init=None, out_dtype=F32):
    m, ka = a.shape
    n = b.shape[1]
    tk = _row_tile(m, 512)
    tn = 1024
    nk = m // tk
    has_init = init is not None

    def body(*refs):
        a_ref, b_ref = refs[0], refs[1]
        pos = 2
        init_ref = cs_init_ref = None
        if has_init:
            init_ref = refs[pos]
            pos += 1
            if with_colsum:
                cs_init_ref = refs[pos]
                pos += 1
        o_ref = refs[pos]
        cs_ref = refs[pos + 1] if with_colsum else None
        acc = refs[-1]
        k = pl.program_id(1)

        @pl.when(k == 0)
        def _():
            if has_init:
                acc[...] = init_ref[...]
                if with_colsum:
                    cs_ref[...] = cs_init_ref[...]
            else:
                acc[...] = jnp.zeros_like(acc)
                if with_colsum:
                    cs_ref[...] = jnp.zeros_like(cs_ref)

        bv = b_ref[...]
        acc[...] += _dot_tn(a_ref[...], bv)
        if with_colsum:
            cs_ref[...] += jnp.sum(bv.astype(F32), axis=0, keepdims=True)

        @pl.when(k == nk - 1)
        def _():
            o_ref[...] = acc[...].astype(out_dtype)

    in_specs = [pl.BlockSpec((tk, ka), lambda j, k: (k, 0)), pl.BlockSpec((tk, tn), lambda j, k: (k, j))]
    args = [a, b]
    if has_init:
        in_specs.append(pl.BlockSpec((ka, tn), lambda j, k: (0, j)))
        args.append(init)
        if with_colsum:
            in_specs.append(pl.BlockSpec((1, tn), lambda j, k: (0, j)))
            args.append(colsum_init)
    out_shape = [jax.ShapeDtypeStruct((ka, n), out_dtype)]
    out_specs = [pl.BlockSpec((ka, tn), lambda j, k: (0, j))]
    if with_colsum:
        out_shape.append(jax.ShapeDtypeStruct((1, n), F32))
        out_specs.append(pl.BlockSpec((1, tn), lambda j, k: (0, j)))
    outs = pl.pallas_call(
        body, name=name, grid=(n // tn, nk), out_shape=out_shape, in_specs=in_specs, out_specs=out_specs,
        scratch_shapes=[pltpu.VMEM((ka, tn), F32)],
    )(*args)
    return outs if with_colsum else outs[0]


def _input_grad(dz, w_all, xin, dr, mod, row, name):
    m, n = dz.shape
    tm = _row_tile(m, 512)
    tk = w_all.shape[2]
    nk = NDEV
    has_dr = dr is not None

    def body(*refs):
        if has_dr:
            dz_ref, w_ref, x_ref, dr_ref, mod_ref, gx_ref, vec_ref, acc = refs
        else:
            dz_ref, w_ref, x_ref, mod_ref, vec_ref, acc = refs
        i, k = pl.program_id(0), pl.program_id(1)

        @pl.when(k == 0)
        def _():
            acc[...] = jnp.zeros_like(acc)

        @pl.when((i == 0) & (k == 0))
        def _():
            vec_ref[...] = jnp.zeros_like(vec_ref)

        acc[...] += _dot_nt(dz_ref[...], w_ref[0])

        @pl.when(k == nk - 1)
        def _():
            du = acc[...]
            xv = x_ref[...]
            if has_dr:
                sc = mod_ref[row:row + 1, D:2 * D]
                gx_ref[...] = ALPHA * dr_ref[...] + du * (1.0 + sc)
            vec_ref[0:1, :] += jnp.sum(du, axis=0, keepdims=True)
            vec_ref[1:2, :] += jnp.sum(du * xv, axis=0, keepdims=True)

    row_spec = pl.BlockSpec((tm, D), lambda i, k: (i, 0))
    in_specs = [pl.BlockSpec((tm, tk), lambda i, k: (i, k)), pl.BlockSpec((1, D, tk), lambda i, k: (k, 0, 0)), row_spec]
    args = [dz, w_all, xin]
    if has_dr:
        in_specs.append(row_spec)
        args.append(dr)
    in_specs.append(pl.BlockSpec((8, 3 * D), lambda i, k: (0, 0)))
    args.append(mod)
    out_shape, out_specs = [], []
    if has_dr:
        out_shape.append(jax.ShapeDtypeStruct((m, D), F32))
        out_specs.append(row_spec)
    out_shape.append(jax.ShapeDtypeStruct((8, D), F32))
    out_specs.append(pl.BlockSpec((8, D), lambda i, k: (0, 0)))
    outs = pl.pallas_call(
        body, name=name, grid=(m // tm, nk), out_shape=out_shape, in_specs=in_specs, out_specs=out_specs,
        scratch_shapes=[pltpu.VMEM((tm, D), F32)],
    )(*args)
    return outs if has_dr else (None, outs[0])


def _tri(reverse):
    r = lax.broadcasted_iota(jnp.int32, (CHUNK, CHUNK), 0)
    c = lax.broadcasted_iota(jnp.int32, (CHUNK, CHUNK), 1)
    return (c >= r) if reverse else (c <= r)


def _cum_f32(tri_b, t):
    hi = t.astype(BF16)
    r1 = t - hi.astype(F32)
    mid = r1.astype(BF16)
    lo = (r1 - mid.astype(F32)).astype(BF16)
    return _dot(tri_b, hi) + _dot(tri_b, mid) + _dot(tri_b, lo)


def _gla_features(zq, zf, lb):
    sq = _sigmoid(zq)
    q = zq * sq * Q_SCALE
    sf = _sigmoid(zf)
    f = lb + (1.0 - lb) * sf
    return q, sq, f, sf


def _gla_decays(f, tri_b, last):
    lf = jnp.log(f)
    g = _cum_f32(tri_b, lf)
    gl = g[last:last + 1, :]
    return g, gl


def _gla_block(n):
    return 256 if n % 256 == 0 else CHUNK


def _gla_fwd(z, lb, s0, d, name):
    n = z.shape[0]
    blk = _gla_block(n)
    nb, npb = n // blk, blk // CHUNK
    reverse = d == 1
    last = 0 if reverse else CHUNK - 1
    order = list(range(npb))[::-1] if reverse else list(range(npb))

    def bmap(i):
        return nb - 1 - i if reverse else i

    hp = GLA_HEADS_PER_STEP
    hw = hp * DH
    units = [(hh, cidx) for hh in range(hp) for cidx in order]

    def body(zq_ref, zf_ref, zv_ref, lb_ref, s0_ref, o_ref, ss_ref, sf_ref, st):
        i = pl.program_id(1)

        @pl.when(i == 0)
        def _():
            st[...] = s0_ref[...]

        mask = _tri(reverse)
        tri_b = jnp.where(mask, 1.0, 0.0).astype(BF16)
        feat = {}
        for u in units:
            hh, cidx = u
            rows, cols = pl.ds(cidx * CHUNK, CHUNK), pl.ds(hh * DH, DH)
            q, _, f, _ = _gla_features(zq_ref[rows, cols], zf_ref[rows, cols], lb_ref[d:d + 1, cols])
            feat[u] = (q, 1.0 - f, jnp.log(f), zv_ref[rows, cols].astype(BF16))
        dec = {u: _cum_f32(tri_b, feat[u][2]) for u in units}
        ops = {}
        for u in units:
            q, k, _, vb = feat[u]
            g = dec[u]
            gl = g[last:last + 1, :]
            ops[u] = ((q * jnp.exp(g)).astype(BF16), (k * jnp.exp(-g)).astype(BF16),
                      (k * jnp.exp(gl - g)).astype(BF16), jnp.exp(gl), vb)
        att = {u: jnp.where(mask, _dot_nt(ops[u][0], ops[u][1]), 0.0).astype(BF16) for u in units}
        upd = {u: _dot_tn(ops[u][4], ops[u][2]) for u in units}
        intra = {u: _dot(att[u], ops[u][4]) for u in units}
        s_in = {}
        for hh in range(hp):
            s = st[hh]
            for cidx in order:
                s_in[(hh, cidx)] = s
                s = s * ops[(hh, cidx)][3] + upd[(hh, cidx)]
            st[hh] = s
            sf_ref[hh] = s
        for u in units:
            hh, cidx = u
            rows, cols = pl.ds(cidx * CHUNK, CHUNK), pl.ds(hh * DH, DH)
            o_ref[rows, cols] = intra[u] + _dot_nt(ops[u][0], s_in[u].astype(BF16))
            ss_ref[hh, cidx] = s_in[u]

    def col(g):
        return lambda h, i: (bmap(i), g * (NH // hp) + h)

    return pl.pallas_call(
        body, name=name, grid=(NH // hp, nb),
        out_shape=[jax.ShapeDtypeStruct((n, D), F32), jax.ShapeDtypeStruct((NH, n // CHUNK, DH, DH), F32),
                   jax.ShapeDtypeStruct((NH, DH, DH), F32)],
        in_specs=[pl.BlockSpec((blk, hw), col(0)), pl.BlockSpec((blk, hw), col(1 + d)),
                  pl.BlockSpec((blk, hw), col(3)), pl.BlockSpec((8, hw), lambda h, i: (0, h)),
                  pl.BlockSpec((hp, DH, DH), lambda h, i: (h, 0, 0))],
        out_specs=[pl.BlockSpec((blk, hw), lambda h, i: (bmap(i), h)),
                   pl.BlockSpec((hp, npb, DH, DH), lambda h, i: (h, bmap(i), 0, 0)),
                   pl.BlockSpec((hp, DH, DH), lambda h, i: (h, 0, 0))],
        scratch_shapes=[pltpu.VMEM((hp, DH, DH), F32)],
    )(z, z, z, lb, s0)


def _gla_bwd(z, lb, s_start, do, ds_fin, acc_q, acc_v, d, name):
    n = z.shape[0]
    blk = _gla_block(n)
    nb, npb = n // blk, blk // CHUNK
    reverse = d == 1
    last = 0 if reverse else CHUNK - 1
    order = list(range(npb)) if reverse else list(range(npb))[::-1]
    has_do = do is not None
    has_acc = acc_q is not None
    hp = GLA_HEADS_PER_STEP
    hw = hp * DH
    units = [(hh, cidx) for hh in range(hp) for cidx in order]

    def bmap(i):
        return i if reverse else nb - 1 - i

    def body(*refs):
        zq_ref, zf_ref, zv_ref, lb_ref, ss_ref, dsf_ref = refs[:6]
        pos = 6
        do_ref = aq_ref = av_ref = None
        if has_do:
            do_ref = refs[pos]
            pos += 1
        if has_acc:
            aq_ref, av_ref = refs[pos], refs[pos + 1]
            pos += 2
        dzq_ref, dzf_ref, dzv_ref, dlb_ref, ds0_ref, dst = refs[pos:]
        i = pl.program_id(1)

        @pl.when(i == 0)
        def _():
            dst[...] = dsf_ref[...]
            dlb_ref[...] = jnp.zeros_like(dlb_ref)

        mask = _tri(reverse)
        tri_b = jnp.where(mask, 1.0, 0.0).astype(BF16)
        tri_t = jnp.where(_tri(not reverse), 1.0, 0.0).astype(BF16)

        def where(u):
            return pl.ds(u[1] * CHUNK, CHUNK), pl.ds(u[0] * DH, DH)

        feat = {}
        for u in units:
            rows, cols = where(u)
            zq, zf = zq_ref[rows, cols], zf_ref[rows, cols]
            lbv = lb_ref[d:d + 1, cols]
            q, sq, f, sf = _gla_features(zq, zf, lbv)
            feat[u] = dict(zq=zq, q=q, sq=sq, f=f, sf=sf, lbv=lbv, k=1.0 - f, vb=zv_ref[rows, cols].astype(BF16))
        dec = {u: _cum_f32(tri_b, jnp.log(feat[u]["f"])) for u in units}
        for u in units:
            w = feat[u]
            g = dec[u]
            gl = g[last:last + 1, :]
            w["eg"], w["egi"], w["ege"], w["egl"] = jnp.exp(g), jnp.exp(-g), jnp.exp(gl - g), jnp.exp(gl)
            w["qd"], w["ki"], w["ke"] = w["q"] * w["eg"], w["k"] * w["egi"], w["k"] * w["ege"]
            w["qdb"], w["kib"], w["keb"] = w["qd"].astype(BF16), w["ki"].astype(BF16), w["ke"].astype(BF16)
            w["s_in"] = ss_ref[u[0], u[1]]
        if has_do:
            for u in units:
                w = feat[u]
                rows, cols = where(u)
                w["dob"] = do_ref[rows, cols].astype(BF16)
            for u in units:
                w = feat[u]
                w["a"] = jnp.where(mask, _dot_nt(w["qdb"], w["kib"]), 0.0).astype(BF16)
                w["da"] = jnp.where(mask, _dot_nt(w["dob"], w["vb"]), 0.0).astype(BF16)
                w["m"] = _dot_tn(w["dob"], w["qdb"])
        for hh in range(hp):
            ds = dst[hh]
            for cidx in order:
                w = feat[(hh, cidx)]
                w["ds"] = ds
                ds = ds * w["egl"]
                if has_do:
                    ds = ds + w["m"]
            dst[hh] = ds
            ds0_ref[hh] = ds
        for u in units:
            w = feat[u]
            dsb = w["ds"].astype(BF16)
            w["dke"] = _dot(w["vb"], dsb)
            w["dv"] = _dot_nt(w["keb"], dsb)
            if has_do:
                w["dv"] = w["dv"] + _dot_tn(w["a"], w["dob"])
                w["dqd"] = _dot(w["da"], w["kib"]) + _dot(w["dob"], w["s_in"].astype(BF16))
                w["dki"] = _dot_tn(w["da"], w["qdb"])
        for u in units:
            w = feat[u]
            dkeke = w["dke"] * w["ke"]
            w["dgl"] = (w["egl"] * jnp.sum(w["s_in"] * w["ds"], axis=0, keepdims=True)
                        + jnp.sum(dkeke, axis=0, keepdims=True))
            dg = -dkeke
            dk = w["dke"] * w["ege"]
            if has_do:
                dg = dg + w["dqd"] * w["qd"] - w["dki"] * w["ki"]
                dk = dk + w["dki"] * w["egi"]
            w["dg"], w["dk"] = dg, dk
        dlf = {u: _cum_f32(tri_t, feat[u]["dg"]) for u in units}
        for u in units:
            w = feat[u]
            rows, cols = where(u)
            df = (dlf[u] + w["dgl"]) / w["f"] - w["dk"]
            sf = w["sf"]
            dzf_ref[rows, cols] = df * (1.0 - w["lbv"]) * sf * (1.0 - sf)
            dlb_ref[0:1, cols] += jnp.sum(df * (1.0 - sf), axis=0, keepdims=True)
            if has_do:
                dzq = w["dqd"] * w["eg"] * (Q_SCALE * _dsilu(w["zq"], w["sq"]))
            else:
                dzq = jnp.zeros((CHUNK, DH), F32)
            dv = w["dv"]
            if has_acc:
                dzq = dzq + aq_ref[rows, cols]
                dv = dv + av_ref[rows, cols]
            dzq_ref[rows, cols] = dzq
            dzv_ref[rows, cols] = dv

    def col(g):
        return lambda h, i: (bmap(i), g * (NH // hp) + h)

    tok = pl.BlockSpec((blk, hw), lambda h, i: (bmap(i), h))
    state = pl.BlockSpec((hp, DH, DH), lambda h, i: (h, 0, 0))
    in_specs = [pl.BlockSpec((blk, hw), col(0)), pl.BlockSpec((blk, hw), col(1 + d)), pl.BlockSpec((blk, hw), col(3)),
                pl.BlockSpec((8, hw), lambda h, i: (0, h)),
                pl.BlockSpec((hp, npb, DH, DH), lambda h, i: (h, bmap(i), 0, 0)), state]
    args = [z, z, z, lb, s_start, ds_fin]
    if has_do:
        in_specs.append(tok)
        args.append(do)
    if has_acc:
        in_specs += [tok, tok]
        args += [acc_q, acc_v]
    return pl.pallas_call(
        body, name=name, grid=(NH // hp, nb),
        out_shape=[jax.ShapeDtypeStruct((n, D), F32)] * 3 + [jax.ShapeDtypeStruct((8, D), F32),
                                                            jax.ShapeDtypeStruct((NH, DH, DH), F32)],
        in_specs=in_specs,
        out_specs=[tok, tok, tok, pl.BlockSpec((8, hw), lambda h, i: (0, h)), state],
        scratch_shapes=[pltpu.VMEM((hp, DH, DH), F32)],
    )(*args)


def _shift(t, s, fill, down):
    n = t.shape[0]
    rows = lax.broadcasted_iota(jnp.int32, t.shape, 0)
    if down:
        return jnp.where(rows >= s, pltpu.roll(t, s, 0), fill)
    return jnp.where(rows < n - s, pltpu.roll(t, n - s, 0), fill)


SUBLANES = 8


def _chain_scan(a, b, h_in, down):
    n = a.shape[0]
    ng = n // SUBLANES
    rows = lax.broadcasted_iota(jnp.int32, (SUBLANES, a.shape[1]), 0)
    local = []
    for g in range(ng):
        aa, bb = a[g * SUBLANES:(g + 1) * SUBLANES], b[g * SUBLANES:(g + 1) * SUBLANES]
        for s in (1, 2, 4):
            if down:
                keep, amt = rows >= s, s
            else:
                keep, amt = rows < SUBLANES - s, SUBLANES - s
            bb = bb + aa * jnp.where(keep, pltpu.roll(bb, amt, 0), 0.0)
            aa = aa * jnp.where(keep, pltpu.roll(aa, amt, 0), 1.0)
        local.append((aa, bb))
    out = [None] * ng
    carry = h_in
    for g in (range(ng) if down else range(ng - 1, -1, -1)):
        aa, bb = local[g]
        hg = bb + aa * carry
        out[g] = hg
        carry = hg[SUBLANES - 1:SUBLANES] if down else hg[0:1]
    return (jnp.concatenate(out, axis=0) if ng > 1 else out[0]), carry


def _conv_taps(xv):
    return (_shift(xv, 1, 0.0, True), xv, _shift(xv, 1, 0.0, False), _shift(xv, 2, 0.0, False))


def _conv(taps, cw, cb):
    return cb + cw[0:1, :] * taps[0] + cw[1:2, :] * taps[1] + cw[2:3, :] * taps[2] + cw[3:4, :] * taps[3]


def _neg_expm1(t):
    series = -t * (1.0 + t * (0.5 + t * (1.0 / 6.0 + t * (1.0 / 24.0 + t * (1.0 / 120.0)))))
    return jnp.where(t > -0.1, series, 1.0 - jnp.exp(t))


def _lru_gates(xc, wr, br, wi, bi, lam):
    xcb = xc.astype(BF16)
    r = _sigmoid(_dot(xcb, wr) + br)
    gi = _sigmoid(_dot(xcb, wi) + bi)
    sp = jnp.maximum(-lam, 0.0) + jnp.log(1.0 + jnp.exp(-jnp.abs(lam)))
    la = -RG_C * r * sp
    a = jnp.exp(la)
    mult = jnp.sqrt(_neg_expm1(2.0 * la))
    return xcb, r, gi, sp, a, mult


def _lru_fwd(xin, blk, cw, cb, wr, br, wi, bi, lam, h0, acc_h, d, name):
    n = xin.shape[0]
    nb = n // blk
    reverse = d == 1
    down = not reverse
    has_acc = acc_h is not None

    def bmap(i):
        return nb - 1 - i if reverse else i

    def body(*refs):
        x_ref, cw_ref, cb_ref, wr_ref, br_ref, wi_ref, bi_ref, lam_ref, h0_ref = refs[:9]
        acc_ref = refs[9] if has_acc else None
        outs = refs[10:] if has_acc else refs[9:]
        h_ref, hin_ref, hfin_ref = outs[:3]
        hsum_ref = outs[3] if has_acc else None
        carry = outs[-1]
        i = pl.program_id(0)

        @pl.when(i == 0)
        def _():
            carry[...] = h0_ref[...]

        for g in range(NH):
            cols = pl.ds(g * DH, DH)
            xc = _conv(_conv_taps(x_ref[:, cols]), cw_ref[:, cols], cb_ref[:, cols])
            _, _, gi, _, a, mult = _lru_gates(xc, wr_ref[g], br_ref[:, cols], wi_ref[g], bi_ref[:, cols],
                                              lam_ref[:, cols])
            hin = carry[:, cols]
            h, h_last = _chain_scan(a, mult * gi * xc, hin, down)
            h_ref[:, cols] = h
            if has_acc:
                hsum_ref[:, cols] = h + acc_ref[:, cols]
            hin_ref[0, :, cols] = hin
            carry[:, cols] = h_last
            hfin_ref[:, cols] = h_last

    vec = pl.BlockSpec((1, D), lambda i: (0, 0))
    wsp = pl.BlockSpec((NH, DH, DH), lambda i: (0, 0, 0))
    tok = pl.BlockSpec((blk, D), lambda i: (bmap(i), 0))
    in_specs = [tok, pl.BlockSpec((4, D), lambda i: (0, 0)), vec, wsp, vec, wsp, vec, vec, vec]
    args = [xin, cw, cb, wr, br, wi, bi, lam, h0]
    out_shape = [jax.ShapeDtypeStruct((n, D), F32), jax.ShapeDtypeStruct((nb, 1, D), F32),
                 jax.ShapeDtypeStruct((1, D), F32)]
    out_specs = [tok, pl.BlockSpec((1, 1, D), lambda i: (bmap(i), 0, 0)), vec]
    if has_acc:
        in_specs.append(tok)
        args.append(acc_h)
        out_shape.append(jax.ShapeDtypeStruct((n, D), F32))
        out_specs.append(tok)
    return pl.pallas_call(
        body, name=name, grid=(nb,), out_shape=out_shape, in_specs=in_specs, out_specs=out_specs,
        scratch_shapes=[pltpu.VMEM((1, D), F32)],
    )(*args)


def _lru_bwd(xin, blk, cw, cb, wr, br, wi, bi, lam, h, hin, dh, cg_fin, acc_dx, init, d, name):
    n = xin.shape[0]
    nb = n // blk
    reverse = d == 1
    down = not reverse
    first = blk - 1 if reverse else 0
    has_dh = dh is not None
    has_acc = acc_dx is not None
    has_init = init is not None

    def bmap(i):
        return i if reverse else nb - 1 - i

    def body(*refs):
        (x_ref, cw_ref, cb_ref, wr_ref, br_ref, wi_ref, bi_ref, lam_ref, h_ref, hin_ref, cgf_ref) = refs[:11]
        pos = 11
        dh_ref = acc_ref = None
        iwr_ref = iwi_ref = ivec_ref = None
        if has_dh:
            dh_ref = refs[pos]
            pos += 1
        if has_acc:
            acc_ref = refs[pos]
            pos += 1
        if has_init:
            iwr_ref, iwi_ref, ivec_ref = refs[pos:pos + 3]
            pos += 3
        dx_ref, dwr_ref, dwi_ref, vec_ref, cg0_ref, carry = refs[pos:]
        i = pl.program_id(0)

        @pl.when(i == 0)
        def _():
            carry[...] = cgf_ref[...]
            if has_init:
                dwr_ref[...] = iwr_ref[...]
                dwi_ref[...] = iwi_ref[...]
                vec_ref[...] = ivec_ref[...]
            else:
                dwr_ref[...] = jnp.zeros_like(dwr_ref)
                dwi_ref[...] = jnp.zeros_like(dwi_ref)
                vec_ref[...] = jnp.zeros_like(vec_ref)

        for g in range(NH):
            cols = pl.ds(g * DH, DH)
            cwv = cw_ref[:, cols]
            lam_v = lam_ref[:, cols]
            taps = _conv_taps(x_ref[:, cols])
            xc = _conv(taps, cwv, cb_ref[:, cols])
            wr_g, wi_g = wr_ref[g], wi_ref[g]
            xcb, r, gi, sp, a, mult = _lru_gates(xc, wr_g, br_ref[:, cols], wi_g, bi_ref[:, cols], lam_v)
            hprev = _shift(h_ref[:, cols], 1, hin_ref[0, :, cols], down)
            a_next = _shift(a, 1, 1.0, not down)
            dhv = dh_ref[:, cols] if has_dh else jnp.zeros_like(a)
            e, _ = _chain_scan(a_next, dhv, carry[:, cols], not down)
            cg = a[first:first + 1, :] * e[first:first + 1, :]
            carry[:, cols] = cg
            cg0_ref[:, cols] = cg
            da = e * hprev
            emult = e * mult
            dgi = emult * xc
            dxc = emult * gi
            dla = da * a - (e * gi * xc) * (a * a) / mult
            dr = dla * (-RG_C * sp)
            sneg = 1.0 - _sigmoid(lam_v)
            dpr = dr * r * (1.0 - r)
            dpi = dgi * gi * (1.0 - gi)
            dprb, dpib = dpr.astype(BF16), dpi.astype(BF16)
            dxc = dxc + _dot_nt(dprb, wr_g) + _dot_nt(dpib, wi_g)
            dwr_ref[g] += _dot_tn(xcb, dprb)
            dwi_ref[g] += _dot_tn(xcb, dpib)
            dx = (cwv[0:1, :] * _shift(dxc, 1, 0.0, False) + cwv[1:2, :] * dxc
                  + cwv[2:3, :] * _shift(dxc, 1, 0.0, True) + cwv[3:4, :] * _shift(dxc, 2, 0.0, True))
            if has_acc:
                dx = dx + acc_ref[:, cols]
            dx_ref[:, cols] = dx
            vec_ref[0:1, cols] += jnp.sum(dpr, axis=0, keepdims=True)
            vec_ref[1:2, cols] += jnp.sum(dpi, axis=0, keepdims=True)
            vec_ref[2:3, cols] += jnp.sum(dla * r, axis=0, keepdims=True) * (RG_C * sneg)
            vec_ref[3:4, cols] += jnp.sum(dxc, axis=0, keepdims=True)
            for kk in range(4):
                vec_ref[4 + kk:5 + kk, cols] += jnp.sum(dxc * taps[kk], axis=0, keepdims=True)

    vec = pl.BlockSpec((1, D), lambda i: (0, 0))
    wsp = pl.BlockSpec((NH, DH, DH), lambda i: (0, 0, 0))
    tok = pl.BlockSpec((blk, D), lambda i: (bmap(i), 0))
    vec16 = pl.BlockSpec((16, D), lambda i: (0, 0))
    in_specs = [tok, pl.BlockSpec((4, D), lambda i: (0, 0)), vec, wsp, vec, wsp, vec, vec, tok,
                pl.BlockSpec((1, 1, D), lambda i: (bmap(i), 0, 0)), vec]
    args = [xin, cw, cb, wr, br, wi, bi, lam, h, hin, cg_fin]
    if has_dh:
        in_specs.append(tok)
        args.append(dh)
    if has_acc:
        in_specs.append(tok)
        args.append(acc_dx)
    if has_init:
        in_specs += [wsp, wsp, vec16]
        args += list(init)
    return pl.pallas_call(
        body, name=name, grid=(nb,),
        out_shape=[jax.ShapeDtypeStruct((n, D), F32), jax.ShapeDtypeStruct((NH, DH, DH), F32),
                   jax.ShapeDtypeStruct((NH, DH, DH), F32), jax.ShapeDtypeStruct((16, D), F32),
                   jax.ShapeDtypeStruct((1, D), F32)],
        in_specs=in_specs, out_specs=[tok, wsp, wsp, vec16, vec],
        scratch_shapes=[pltpu.VMEM((1, D), F32)],
    )(*args)


def _merge(z, o_f, o_b, hx, xin, tgt, mod, gn, p_a, p_b, w_out, ln_g, ln_b):
    n = xin.shape[0]
    tm = _row_tile(n, 128)

    def body(z4_ref, z6_ref, z7_ref, z8_ref, of_ref, ob_ref, hx_ref, x_ref, t_ref, mod_ref, gn_ref,
             pa_ref, pb_ref, wo_ref, lg_ref, lnb_ref,
             dr_ref, do_ref, dhx_ref, dz4_ref, dz678_ref, oa_o, obb_o, y_o, dya_o, dyb_o, dout_o, vec_ref):
        @pl.when(pl.program_id(0) == 0)
        def _():
            vec_ref[...] = jnp.zeros_like(vec_ref)

        gt = mod_ref[0:1, 2 * D:3 * D]
        gnv = gn_ref[...]
        o = of_ref[...] + ob_ref[...]
        rs = jnp.concatenate(
            [jnp.broadcast_to(lax.rsqrt(jnp.mean(jnp.square(o[:, h * DH:(h + 1) * DH]), axis=1, keepdims=True)
                                        + RMS_EPS), (tm, DH)) for h in range(NH)], axis=1)
        nrm = o * rs
        rn = nrm * gnv
        z4, z6, z7, z8 = z4_ref[...], z6_ref[...], z7_ref[...], z8_ref[...]
        s4, s6, s7, s8 = _sigmoid(z4), _sigmoid(z6), _sigmoid(z7), _sigmoid(z8)
        sg4, sg6 = z4 * s4, z6 * s6
        hxv = hx_ref[...]
        oa = (rn * sg4).astype(BF16)
        obb = (hxv * sg6).astype(BF16)
        ya = _dot(oa, pa_ref[...])
        yb = _dot(obb, pb_ref[...])
        y = (s7 * ya + s8 * yb).astype(BF16)
        out = _dot(y, wo_ref[...])
        xv = x_ref[...]
        rr = ALPHA * xv + gt * out
        mu = jnp.mean(rr, axis=1, keepdims=True)
        cen = rr - mu
        rstd = lax.rsqrt(jnp.mean(cen * cen, axis=1, keepdims=True) + LN_EPS)
        xhat = cen * rstd
        lg = lg_ref[...]
        err = xhat * lg + lnb_ref[...] - t_ref[...]
        loss_rows = jnp.sum(err * err, axis=1, keepdims=True)
        dxn = err * (1.0 / D)
        dxh = dxn * lg
        dr = rstd * (dxh - jnp.mean(dxh, axis=1, keepdims=True)
                     - xhat * jnp.mean(dxh * xhat, axis=1, keepdims=True))
        dout = (dr * gt).astype(BF16)
        dy = _dot_nt(dout, wo_ref[...])
        dya = (dy * s7).astype(BF16)
        dyb = (dy * s8).astype(BF16)
        doa = _dot_nt(dya, pa_ref[...])
        dob = _dot_nt(dyb, pb_ref[...])
        drn = doa * sg4
        dn = drn * gnv
        dnn = dn * nrm
        corr = jnp.concatenate(
            [jnp.broadcast_to(jnp.mean(dnn[:, h * DH:(h + 1) * DH], axis=1, keepdims=True), (tm, DH))
             for h in range(NH)], axis=1)
        dr_ref[...] = dr
        do_ref[...] = rs * (dn - nrm * corr)
        dhx_ref[...] = dob * sg6
        dz4_ref[...] = (doa * rn * _dsilu(z4, s4)).astype(BF16)
        dz678_ref[:, 0:D] = (dob * hxv * _dsilu(z6, s6)).astype(BF16)
        dz678_ref[:, D:2 * D] = (dy * ya * s7 * (1.0 - s7)).astype(BF16)
        dz678_ref[:, 2 * D:3 * D] = (dy * yb * s8 * (1.0 - s8)).astype(BF16)
        oa_o[...] = oa
        obb_o[...] = obb
        y_o[...] = y
        dya_o[...] = dya
        dyb_o[...] = dyb
        dout_o[...] = dout
        vec_ref[0:1, :] += jnp.sum(dr * out, axis=0, keepdims=True)
        vec_ref[1:2, :] += jnp.sum(dxn * xhat, axis=0, keepdims=True)
        vec_ref[2:3, :] += jnp.sum(dxn, axis=0, keepdims=True)
        vec_ref[3:4, :] += jnp.sum(drn * nrm, axis=0, keepdims=True)
        vec_ref[4:5, :] += jnp.broadcast_to(jnp.sum(loss_rows, axis=0, keepdims=True) * (0.5 / D), (1, D))

    def grp(g):
        return pl.BlockSpec((tm, D), lambda i: (i, g))

    tok = pl.BlockSpec((tm, D), lambda i: (i, 0))
    vec = pl.BlockSpec((1, D), lambda i: (0, 0))
    wsp = pl.BlockSpec((D, D), lambda i: (0, 0))
    return pl.pallas_call(
        body, name="merge", grid=(n // tm,),
        out_shape=[jax.ShapeDtypeStruct((n, D), F32)] * 3
        + [jax.ShapeDtypeStruct((n, D), BF16), jax.ShapeDtypeStruct((n, 3 * D), BF16)]
        + [jax.ShapeDtypeStruct((n, D), BF16)] * 6 + [jax.ShapeDtypeStruct((8, D), F32)],
        in_specs=[grp(4), grp(6), grp(7), grp(8), tok, tok, tok, tok, tok,
                  pl.BlockSpec((8, 3 * D), lambda i: (0, 0)), vec, wsp, wsp, wsp, vec, vec],
        out_specs=[tok, tok, tok, tok, pl.BlockSpec((tm, 3 * D), lambda i: (i, 0))] + [tok] * 6
        + [pl.BlockSpec((8, D), lambda i: (0, 0))],
    )(z, z, z, z, o_f, o_b, hx, xin, tgt, mod, gn, p_a, p_b, w_out, ln_g, ln_b)


def _wmod_grad(c_t, cctx_t, dmx_loc, dmc_loc, name):
    n = dmx_loc.shape[1]

    def body(ct_ref, cc_ref, dmx_ref, dmc_ref, o_ref):
        ct = ct_ref[...]
        sct = ct * _sigmoid(ct)
        cc = cc_ref[...]
        scc = cc * _sigmoid(cc)
        dmc = dmc_ref[0:1, :]
        for b in range(1, NDEV):
            dmc = dmc + dmc_ref[b:b + 1, :]
        acc = scc * dmc
        for b in range(NDEV):
            acc = acc + sct[:, b:b + 1] * dmx_ref[b:b + 1, :]
        o_ref[...] = acc

    return pl.pallas_call(body, name=name, out_shape=jax.ShapeDtypeStruct((D, n), F32))(c_t, cctx_t, dmx_loc, dmc_loc)


def _cctx_grad(dmc_sum, w_mod_full, cctx_row, name):
    def body(d_ref, w_ref, c_ref, o_ref):
        cv = c_ref[...]
        s = _sigmoid(cv)
        o_ref[...] = _dot_nt(d_ref[...].astype(BF16), w_ref[...]) * _dsilu(cv, s)

    return pl.pallas_call(body, name=name, out_shape=jax.ShapeDtypeStruct((8, D), F32))(dmc_sum, w_mod_full, cctx_row)


def _to_colmajor(t, rows):
    return t.reshape(rows, GRID_W, D).transpose(1, 0, 2).reshape(rows * GRID_W, D)


def _to_raster(t, rows):
    return t.reshape(GRID_W, rows, D).transpose(1, 0, 2).reshape(rows * GRID_W, D)


def _local_cols(t, me, width):
    return lax.dynamic_slice_in_dim(t, me * width, width, axis=t.ndim - 1)


def kernel(x, c, ctx, c_ctx, w_mod, b_mod, w_in, b_in, lb_logits, norm_a_g, conv_w, conv_b, w_r, b_r, w_i, b_i, lam, p_a, p_b, w_out, ln_g, ln_b, loss_target, m_c_ctx, m_w_mod, m_b_mod, m_w_in, m_b_in, m_lb_logits, m_norm_a_g, m_conv_w, m_conv_b, m_w_r, m_b_r, m_w_i, m_b_i, m_lam, m_p_a, m_p_b, m_w_out, m_ln_g, m_ln_b, v_c_ctx, v_w_mod, v_b_mod, v_w_in, v_b_in, v_lb_logits, v_norm_a_g, v_conv_w, v_conv_b, v_w_r, v_b_r, v_w_i, v_b_i, v_lam, v_p_a, v_p_b, v_w_out, v_ln_g, v_ln_b):
    me = _my_index()
    xs, cs, tgt = x[0], ctx[0], loss_target[0]
    t_len, c_len = xs.shape[0], cs.shape[0]
    rows = t_len // GRID_W
    wcols = w_in.shape[2]
    mcols = w_mod.shape[2]

    small = jnp.concatenate([lb_logits.reshape(4, DH), conv_w[0], b_r[0], b_i[0], lam[0], jnp.zeros((2, DH), F32),
                             c.reshape(8, DH)], axis=0)
    g_small, g_wmod = _all_gather([small, w_mod[0].astype(BF16)], "gather_params")

    def full_rows(lo, hi):
        return g_small[:, lo:hi, :].transpose(1, 0, 2).reshape(hi - lo, D)

    lbl_f, cw_f, br_f, bi_f, lam_f = full_rows(0, 4), full_rows(4, 8), full_rows(8, 10), full_rows(10, 12), full_rows(12, 14)
    c_all = g_small[:, 16:24, :].reshape(NDEV, D)
    w_mod_f = g_wmod.transpose(1, 0, 2).reshape(D, 3 * D)
    w_r_b, w_i_b = w_r[0].astype(BF16), w_i[0].astype(BF16)

    cc = jnp.concatenate([c.reshape(1, D), c_ctx.reshape(1, D), jnp.zeros((6, D), F32)], axis=0)
    lbl_p = jnp.concatenate([lbl_f.reshape(2, 2, D), jnp.zeros((2, 6, D), F32)], axis=1)
    mod, lb = _prep(cc, w_mod_f, b_mod, lbl_p)
    u_x = _modulate(xs, mod, 0, "modulate_x")
    u_c = _modulate(cs, mod, 1, "modulate_c")
    z_x, w_in_f, g_pa, g_pb, g_wo = _inproj_gather(
        u_x, w_in[0].astype(BF16), b_in, [p_a[0].astype(BF16), p_b[0].astype(BF16), w_out[0].astype(BF16)],
        "inproj_gather")
    p_a_f, p_b_f, w_out_f = g_pa.reshape(D, D), g_pb.reshape(D, D), g_wo.reshape(D, D)
    z_c = _mm_bias(u_c, w_in_f, b_in, "inproj_c")

    zero_s = jnp.zeros((NH, DH, DH), F32)
    zero_v = jnp.zeros((1, D), F32)
    gla = {}
    for d in (0, 1):
        _, ssc, sfc = _gla_fwd(z_c, lb, zero_s, d, f"gla_fwd_c{d}")
        o_d, ssx, _ = _gla_fwd(z_x, lb, sfc, d, f"gla_fwd_x{d}")
        gla[d] = (ssc, ssx, o_d)

    x5_c = z_c[:, 5 * D:6 * D]
    x5_x = _to_colmajor(z_x[:, 5 * D:6 * D], rows)
    cb2 = conv_b.reshape(1, D)
    lru = {}
    h_sum = None
    for d in (0, 1):
        prm = (cw_f, cb2, w_r_b[d], br_f[d:d + 1], w_i_b[d], bi_f[d:d + 1], lam_f[d:d + 1])
        h_c, hin_c, hfin_c = _lru_fwd(x5_c, c_len, *prm, zero_v, None, d, f"lru_fwd_c{d}")
        h_x, hin_x, _, *h_sum = _lru_fwd(x5_x, rows, *prm, hfin_c, lru[0][3] if d else None, d, f"lru_fwd_x{d}")
        lru[d] = (prm, h_c, hin_c, h_x, hin_x)
    hx = _to_raster(h_sum[0], rows)

    gn = jnp.tile(norm_a_g.reshape(1, DH), (1, NH))
    (dr, do, dhx, dz4, dz678, oa, obb, yb16, dya, dyb, dout, mvec) = _merge(
        z_x, gla[0][2], gla[1][2], hx, xs, tgt, mod, gn, p_a_f, p_b_f, w_out_f, ln_g, ln_b)

    dhx_cm = _to_colmajor(dhx, rows)
    lru_dx_x = lru_dx_c = None
    lru_init = None
    for d in (0, 1):
        prm, h_c, hin_c, h_x, hin_x = lru[d]
        lru_dx_x, dwr, dwi, lvec, cg0 = _lru_bwd(x5_x, rows, *prm, h_x, hin_x, dhx_cm, zero_v, lru_dx_x, None, d,
                                                 f"lru_bwd_x{d}")
        lru_dx_c, dwr, dwi, lvec, _ = _lru_bwd(x5_c, c_len, *prm, h_c, hin_c, None, cg0, lru_dx_c, (dwr, dwi, lvec), d,
                                               f"lru_bwd_c{d}")
        lru[d] = (dwr, dwi, lvec)
    dz5_x = _to_raster(lru_dx_x, rows).astype(BF16)
    dz5_c = lru_dx_c.astype(BF16)

    gq_x = gv_x = gq_c = gv_c = None
    dzf_x, dzf_c, dlb = {}, {}, {}
    for d in (0, 1):
        ssc, ssx, _ = gla[d]
        gq_x, dzf_x[d], gv_x, dlb_x, ds0 = _gla_bwd(z_x, lb, ssx, do, zero_s, gq_x, gv_x, d, f"gla_bwd_x{d}")
        gq_c, dzf_c[d], gv_c, dlb_c, _ = _gla_bwd(z_c, lb, ssc, None, ds0, gq_c, gv_c, d, f"gla_bwd_c{d}")
        dlb[d] = dlb_x[0:1] + dlb_c[0:1]

    bf = lambda t: t.astype(BF16)
    dz_x = jnp.concatenate([bf(gq_x), bf(dzf_x[0]), bf(dzf_x[1]), bf(gv_x), dz4, dz5_x, dz678], axis=1)
    zc0 = jnp.zeros((c_len, D), BF16)
    dz_c = jnp.concatenate([bf(gq_c), bf(dzf_c[0]), bf(dzf_c[1]), bf(gv_c), zc0, dz5_c, zc0, zc0, zc0], axis=1)
    dwin_c, dbin_c = _mm_tn(u_c, dz_c, None, "dwin_c", with_colsum=True)
    dpa = _mm_tn(oa, dya, None, "dpa", out_dtype=BF16)
    dpb = _mm_tn(obb, dyb, None, "dpb", out_dtype=BF16)
    dwo = _mm_tn(yb16, dout, None, "dwout", out_dtype=BF16)

    wr_pack = jnp.concatenate([lru[0][0], lru[1][0], lru[0][1], lru[1][1]], axis=0).reshape(4 * NH * DH, DH)
    r_win, dbin, r_pa, r_pb, r_wo, r_wri = _dwin_exchange(
        u_x, dz_x, dwin_c, dbin_c, [dpa, dpb, dwo, wr_pack], [0, 0, 0, 0], "dwin_exchange")
    grad_x, xvec = _input_grad(dz_x, w_in_f, xs, dr, mod, 0, "input_grad_x")
    _, cvec = _input_grad(dz_c, w_in_f, cs, None, mod, 1, "input_grad_c")
    wri_piece = _sum_rows(r_wri, "sum_w_ri_piece")
    g_w_in, d_w_in, nm_w_in, nv_w_in = _sum_adamw(r_win, w_in, m_w_in, v_w_in, "update_w_in")
    g_p_a, d_p_a, nm_p_a, nv_p_a = _sum_adamw(r_pa, p_a, m_p_a, v_p_a, "update_p_a")
    g_p_b, d_p_b, nm_p_b, nv_p_b = _sum_adamw(r_pb, p_b, m_p_b, v_p_b, "update_p_b")
    g_w_out, d_w_out, nm_w_out, nv_w_out = _sum_adamw(r_wo, w_out, m_w_out, v_w_out, "update_w_out")

    dlb_rows = jnp.concatenate([dlb[0], dlb[1]], axis=0)
    pack = jnp.concatenate([
        xvec[0:1], xvec[1:2], mvec[0:1],
        cvec[0:1], cvec[1:2], jnp.zeros((1, D), F32),
        dbin.reshape(NGRP, D),
        mvec[3:4], mvec[1:2], mvec[2:3],
        lru[0][2][0:8], lru[1][2][0:3],
        lru[1][2][3:8],
        dlb_rows,
        mvec[4:5],
        jnp.zeros((3, D), F32)], axis=0)
    g_pack, g_wri = _all_gather([pack, wri_piece], "gather_small_grads")
    tot = _sum_rows(g_pack, "sum_small_grads")

    loss = tot[36, 0]
    dmx = g_pack[:, 0:3, :].reshape(NDEV, 3 * D)
    dmc = g_pack[:, 3:6, :].reshape(NDEV, 3 * D)
    grad_w_mod = _wmod_grad(c_all.T, c_ctx.reshape(D, 1), _local_cols(dmx, me, mcols), _local_cols(dmc, me, mcols),
                            "grad_w_mod").reshape(1, D, mcols)
    grad_b_mod = (tot[0:3] + tot[3:6]).reshape(1, 3 * D)
    dmc_sum = jnp.concatenate([tot[3:6].reshape(1, 3 * D), jnp.zeros((7, 3 * D), F32)], axis=0)
    grad_c_ctx = _cctx_grad(dmc_sum, w_mod_f, cc[1:2], "grad_c_ctx")[0]
    grad_b_in = tot[6:15].reshape(1, NGRP * D)
    grad_norm_a_g = tot[15].reshape(NH, DH).sum(axis=0).reshape(1, DH)
    grad_ln_g, grad_ln_b = tot[16:17], tot[17:18]
    grad_conv_b = tot[21:22] + tot[29:30]
    grad_conv_w = _local_cols(tot[22:26] + tot[30:34], me, DH).reshape(1, 4, DH)
    grad_b_r = _local_cols(jnp.stack([tot[18], tot[26]]), me, DH).reshape(1, 2, DH)
    grad_b_i = _local_cols(jnp.stack([tot[19], tot[27]]), me, DH).reshape(1, 2, DH)
    grad_lam = _local_cols(jnp.stack([tot[20], tot[28]]), me, DH).reshape(1, 2, DH)
    lb_loc = _local_cols(lb[0:2], me, DH)
    dl0 = _local_cols(tot[34:36], me, DH) * lb_loc * (1.0 - lb_loc)
    grad_lb_logits = jnp.stack([dl0, -dl0])

    half = 2 * NH * DH
    g_ri = g_wri.reshape(2 * half, DH)
    grad_w_r, grad_w_i = g_ri[:half].reshape(w_r.shape), g_ri[half:].reshape(w_i.shape)
    d_w_r, nm_w_r, nv_w_r = _adamw(grad_w_r, w_r, m_w_r, v_w_r, "update_w_r")
    d_w_i, nm_w_i, nv_w_i = _adamw(grad_w_i, w_i, m_w_i, v_w_i, "update_w_i")

    d_w_mod, nm_w_mod, nv_w_mod = _adamw(grad_w_mod, w_mod, m_w_mod, v_w_mod, "update_w_mod")

    small_items = [
        (grad_c_ctx, c_ctx, m_c_ctx, v_c_ctx), (grad_b_mod, b_mod, m_b_mod, v_b_mod),
        (grad_b_in, b_in, m_b_in, v_b_in), (grad_lb_logits, lb_logits, m_lb_logits, v_lb_logits),
        (grad_norm_a_g, norm_a_g, m_norm_a_g, v_norm_a_g), (grad_conv_w, conv_w, m_conv_w, v_conv_w),
        (grad_conv_b, conv_b, m_conv_b, v_conv_b), (grad_b_r, b_r, m_b_r, v_b_r), (grad_b_i, b_i, m_b_i, v_b_i),
        (grad_lam, lam, m_lam, v_lam), (grad_ln_g, ln_g, m_ln_g, v_ln_g), (grad_ln_b, ln_b, m_ln_b, v_ln_b)]
    cat = lambda k: jnp.concatenate([it[k].reshape(-1, DH) for it in small_items], axis=0)
    sd, sm, sv = _adamw(cat(0), cat(1), cat(2), cat(3), "update_small")
    small_out = []
    off = 0
    for it in small_items:
        shp = it[1].shape
        nrow = it[1].size // DH
        small_out.append(tuple(t[off:off + nrow].reshape(shp) for t in (sd, sm, sv)))
        off += nrow
    (o_c_ctx, o_b_mod, o_b_in, o_lb, o_norm, o_conv_w, o_conv_b, o_b_r, o_b_i, o_lam, o_ln_g, o_ln_b) = small_out

    grads = [grad_c_ctx.reshape(c_ctx.shape), grad_w_mod, grad_b_mod, g_w_in, grad_b_in, grad_lb_logits, grad_norm_a_g,
             grad_conv_w, grad_conv_b, grad_w_r, grad_b_r, grad_w_i, grad_b_i, grad_lam, g_p_a, g_p_b, g_w_out,
             grad_ln_g, grad_ln_b]
    per_kind = []
    for k in range(3):
        per_kind.append([
            o_c_ctx[k], (d_w_mod, nm_w_mod, nv_w_mod)[k], o_b_mod[k], (d_w_in, nm_w_in, nv_w_in)[k], o_b_in[k], o_lb[k],
            o_norm[k], o_conv_w[k], o_conv_b[k], (d_w_r, nm_w_r, nv_w_r)[k], o_b_r[k], (d_w_i, nm_w_i, nv_w_i)[k],
            o_b_i[k], o_lam[k], (d_p_a, nm_p_a, nv_p_a)[k], (d_p_b, nm_p_b, nv_p_b)[k], (d_w_out, nm_w_out, nv_w_out)[k],
            o_ln_g[k], o_ln_b[k]])
    return (loss, grad_x.reshape(x.shape), *grads, *per_kind[0], *per_kind[1], *per_kind[2])
```

```python
import functools

import jax
import jax.numpy as jnp
from jax import lax
from jax.experimental import pallas as pl
from jax.experimental.pallas import tpu as pltpu

F32 = jnp.float32
BF16 = jnp.bfloat16

D = 1024
NH = 8
DH = 128
CHUNK = 64
GLA_HEADS_PER_STEP = 4
GRID_W = 64
NGRP = 9
NDEV = 8
RG_C = 8.0
ALPHA = 2.0 ** 0.25
LN_EPS = 1e-5
RMS_EPS = 1e-6
Q_SCALE = DH ** -0.5
ADAM_LR, ADAM_B1, ADAM_B2, ADAM_EPS, ADAM_WD, ADAM_STEP = 1e-3, 0.9, 0.999, 1e-8, 0.01, 10
ADAM_C1 = 1.0 / (1.0 - ADAM_B1 ** ADAM_STEP)
ADAM_C2 = 1.0 / (1.0 - ADAM_B2 ** ADAM_STEP)

ANY = pl.BlockSpec(memory_space=pl.ANY)


def _sigmoid(t):
    return 1.0 / (1.0 + jnp.exp(-t))


def _dsilu(t, s):
    return s * (1.0 + t * (1.0 - s))


def _dot(a, b):
    return jnp.dot(a, b, preferred_element_type=F32)


def _dot_nt(a, b):
    return lax.dot_general(a, b, (((1,), (1,)), ((), ())), preferred_element_type=F32)


def _dot_tn(a, b):
    return lax.dot_general(a, b, (((0,), (0,)), ((), ())), preferred_element_type=F32)


def _my_index():
    return 4 * lax.axis_index("x") + 2 * lax.axis_index("y") + lax.axis_index("c")


def _dev_tuple(j):
    return (j >> 2, (j >> 1) & 1, j & 1)


def _all_gather(shards, name):
    n = len(shards)

    def body(*refs):
        ins, outs = refs[:n], refs[n:2 * n]
        send_sems, recv_sems, loc_sems = refs[2 * n:]
        me = _my_index()
        for a in range(n):
            pltpu.make_async_copy(ins[a], outs[a].at[me], loc_sems.at[a]).start()
        for j in range(NDEV):
            @pl.when(me != j)
            def _():
                for a in range(n):
                    pltpu.make_async_remote_copy(
                        src_ref=ins[a], dst_ref=outs[a].at[me],
                        send_sem=send_sems.at[a * NDEV + j], recv_sem=recv_sems.at[a * NDEV + me],
                        device_id=_dev_tuple(j), device_id_type=pl.DeviceIdType.MESH).start()
        for j in range(NDEV):
            @pl.when(me != j)
            def _():
                for a in range(n):
                    pltpu.make_async_remote_copy(
                        src_ref=ins[a], dst_ref=outs[a].at[j],
                        send_sem=send_sems.at[a * NDEV + j], recv_sem=recv_sems.at[a * NDEV + j],
                        device_id=_dev_tuple(j), device_id_type=pl.DeviceIdType.MESH).wait()
        for a in range(n):
            pltpu.make_async_copy(ins[a], outs[a].at[me], loc_sems.at[a]).wait()

    return pl.pallas_call(
        body, name=name,
        out_shape=[jax.ShapeDtypeStruct((NDEV,) + s.shape, s.dtype) for s in shards],
        in_specs=[ANY] * n, out_specs=[ANY] * n,
        scratch_shapes=[pltpu.SemaphoreType.DMA((n * NDEV,)), pltpu.SemaphoreType.DMA((n * NDEV,)),
                        pltpu.SemaphoreType.DMA((n,))],
    )(*shards)


_STEP_MASKS = ((2, 4, 6, 3, 5, 7, 1, 0), (4, 2, 6, 5, 3, 7, 1, 0))
_GATHER_MASKS = ((0, 1, 2, 4, 3, 5, 6, 7), (0, 1, 4, 2, 5, 3, 6, 7))


def _peer_schedule(table):
    tab = jnp.array(table, jnp.int32)
    return jnp.bitwise_xor(_my_index(), tab[lax.axis_index("c")])


def _step_peer(s, table=_STEP_MASKS):
    def pick(row):
        if isinstance(s, int):
            return jnp.int32(row[s])
        m = jnp.int32(row[NDEV - 1])
        for t in range(NDEV - 2, -1, -1):
            m = jnp.where(s == t, jnp.int32(row[t]), m)
        return m
    mask = jnp.where(lax.axis_index("c") == 0, pick(table[0]), pick(table[1]))
    return jnp.bitwise_xor(_my_index(), mask)


def _dev_of(p):
    return (p // 4, (p // 2) % 2, p % 2)


def _dwin_exchange(u, dz, init, cs_init, extras, splits, name):
    m, ka = u.shape
    n = dz.shape[1]
    pc = n // NDEV
    tk = _row_tile(m, 512)
    nk = m // tk
    ne = len(extras)
    pieces = []
    for a in range(ne):
        r, c = extras[a].shape
        pieces.append((r // NDEV, c) if splits[a] == 0 else (r, c // NDEV))

    def piece_ref(ref, a, j):
        pr, pcol = pieces[a]
        if splits[a] == 0:
            return ref.at[pl.ds(j * pr, pr), :]
        return ref.at[:, pl.ds(j * pcol, pcol)]

    def body(pidx_ref, u_ref, dz_ref, init_ref, csi_ref, *rest):
        ex_in = rest[:ne]
        rwin, cs_ref = rest[ne], rest[ne + 1]
        ex_out = rest[ne + 2:2 * ne + 2]
        acc, sbuf, wsend, wrecv, wloc, esend, erecv, eloc = rest[2 * ne + 2:]
        s, k = pl.program_id(0), pl.program_id(1)
        me = _my_index()

        def extra_local(a, j):
            return pltpu.make_async_copy(piece_ref(ex_in[a], a, j), ex_out[a].at[j], eloc.at[a])

        def extra_remote(a, j, slot):
            return pltpu.make_async_remote_copy(
                src_ref=piece_ref(ex_in[a], a, j), dst_ref=ex_out[a].at[slot],
                send_sem=esend.at[a * NDEV + j], recv_sem=erecv.at[a * NDEV + slot],
                device_id=_dev_tuple(j), device_id_type=pl.DeviceIdType.MESH)

        def slab_copy(slot, p):
            return pltpu.make_async_remote_copy(
                src_ref=sbuf.at[slot], dst_ref=rwin.at[me], send_sem=wsend.at[slot], recv_sem=wrecv.at[me],
                device_id=_dev_of(p), device_id_type=pl.DeviceIdType.MESH)

        @pl.when((s == 0) & (k == 0))
        def _():
            for j in range(NDEV):
                @pl.when(me == j)
                def _():
                    for a in range(ne):
                        extra_local(a, j).start()

                @pl.when(me != j)
                def _():
                    for a in range(ne):
                        extra_remote(a, j, me).start()

        @pl.when(k == 0)
        def _():
            acc[...] = init_ref[...]
            cs_ref[...] = csi_ref[...]

        bv = dz_ref[...]
        acc[...] += _dot_tn(u_ref[...], bv)
        cs_ref[...] += jnp.sum(bv.astype(F32), axis=0, keepdims=True)

        @pl.when(k == nk - 1)
        def _():
            slot = s % 2

            @pl.when(s >= 2)
            def _():
                slab_copy(slot, me).wait_send()

            sbuf[slot] = acc[...].astype(BF16)

            @pl.when(s < NDEV - 1)
            def _():
                slab_copy(slot, _step_peer(s)).start()

            @pl.when(s == NDEV - 1)
            def _():
                own = pltpu.make_async_copy(sbuf.at[slot], rwin.at[me], wloc.at[0])
                own.start()
                slab_copy(1 - slot, me).wait_send()
                for j in range(NDEV):
                    @pl.when(me != j)
                    def _():
                        pltpu.make_async_remote_copy(
                            src_ref=sbuf.at[0], dst_ref=rwin.at[j], send_sem=wsend.at[0], recv_sem=wrecv.at[j],
                            device_id=_dev_tuple(j), device_id_type=pl.DeviceIdType.MESH).wait_recv()
                        for a in range(ne):
                            extra_remote(a, j, j).wait()

                    @pl.when(me == j)
                    def _():
                        for a in range(ne):
                            extra_local(a, j).wait()
                own.wait()

    grid_spec = pltpu.PrefetchScalarGridSpec(
        num_scalar_prefetch=1, grid=(NDEV, nk),
        in_specs=[pl.BlockSpec((tk, ka), lambda s, k, pidx: (k, 0)),
                  pl.BlockSpec((tk, pc), lambda s, k, pidx: (k, pidx[s])),
                  pl.BlockSpec((ka, pc), lambda s, k, pidx: (0, pidx[s])),
                  pl.BlockSpec((1, pc), lambda s, k, pidx: (0, pidx[s]))] + [ANY] * ne,
        out_specs=[ANY, pl.BlockSpec((1, pc), lambda s, k, pidx: (0, pidx[s]))] + [ANY] * ne,
        scratch_shapes=[pltpu.VMEM((ka, pc), F32), pltpu.VMEM((2, ka, pc), BF16),
                        pltpu.SemaphoreType.DMA((2,)), pltpu.SemaphoreType.DMA((NDEV,)), pltpu.SemaphoreType.DMA((1,)),
                        pltpu.SemaphoreType.DMA((ne * NDEV,)), pltpu.SemaphoreType.DMA((ne * NDEV,)),
                        pltpu.SemaphoreType.DMA((ne,))])
    return pl.pallas_call(
        body, name=name, grid_spec=grid_spec,
        out_shape=[jax.ShapeDtypeStruct((NDEV, ka, pc), BF16), jax.ShapeDtypeStruct((1, n), F32)]
        + [jax.ShapeDtypeStruct((NDEV,) + pieces[a], extras[a].dtype) for a in range(ne)],
    )(_peer_schedule(_STEP_MASKS), u, dz, init, cs_init, *extras)


def _inproj_gather(u, w_loc, bias, extras, name):
    m, k = u.shape
    pc = w_loc.shape[1]
    n = pc * NDEV
    tm = _row_tile(m, 512)
    ni = m // tm
    ne = len(extras)

    def body(pidx_ref, u_ref, b_ref, wl_ref, *rest):
        ex_in = rest[:ne]
        z_ref, wall = rest[ne], rest[ne + 1]
        ex_out = rest[ne + 2:2 * ne + 2]
        wbuf, wsend, wrecv, ldsem, ownsem, esend, erecv, eloc = rest[2 * ne + 2:]
        s, i = pl.program_id(0), pl.program_id(1)
        me = _my_index()

        def shard_push(t):
            return pltpu.make_async_remote_copy(
                src_ref=wl_ref, dst_ref=wall.at[me], send_sem=wsend.at[t], recv_sem=wrecv.at[me],
                device_id=_dev_of(_step_peer(t, _GATHER_MASKS)), device_id_type=pl.DeviceIdType.MESH)

        def extra_push(a, t):
            return pltpu.make_async_remote_copy(
                src_ref=ex_in[a], dst_ref=ex_out[a].at[me], send_sem=esend.at[a * NDEV + t],
                recv_sem=erecv.at[a * NDEV + me],
                device_id=_dev_of(_step_peer(t, _GATHER_MASKS)), device_id_type=pl.DeviceIdType.MESH)

        def load(slot, src):
            return pltpu.make_async_copy(src, wbuf.at[slot], ldsem.at[slot])

        own = pltpu.make_async_copy(wl_ref, wall.at[me], ownsem.at[0])

        @pl.when((s == 0) & (i == 0))
        def _():
            own.start()
            load(0, wl_ref).start()
            for t in range(1, NDEV):
                shard_push(t).start()
            for a in range(ne):
                pltpu.make_async_copy(ex_in[a], ex_out[a].at[me], eloc.at[a]).start()
                for t in range(1, NDEV):
                    extra_push(a, t).start()

        @pl.when((i == ni // 2) & (s < NDEV - 1))
        def _():
            nxt = _step_peer(s + 1, _GATHER_MASKS)
            pltpu.make_async_remote_copy(
                src_ref=wl_ref, dst_ref=wall.at[nxt], send_sem=wsend.at[0], recv_sem=wrecv.at[nxt],
                device_id=_dev_of(nxt), device_id_type=pl.DeviceIdType.MESH).wait_recv()
            load((s + 1) % 2, wall.at[nxt]).start()

        @pl.when(i == 0)
        def _():
            load(s % 2, wl_ref).wait()

        z_ref[...] = _dot(u_ref[...], wbuf[s % 2]) + b_ref[...]

        @pl.when((s == NDEV - 1) & (i == ni - 1))
        def _():
            own.wait()
            for t in range(1, NDEV):
                shard_push(t).wait_send()
            for a in range(ne):
                pltpu.make_async_copy(ex_in[a], ex_out[a].at[me], eloc.at[a]).wait()
                for t in range(1, NDEV):
                    extra_push(a, t).wait_send()
            for j in range(NDEV):
                @pl.when(me != j)
                def _():
                    for a in range(ne):
                        pltpu.make_async_remote_copy(
                            src_ref=ex_in[a], dst_ref=ex_out[a].at[j], send_sem=esend.at[a * NDEV],
                            recv_sem=erecv.at[a * NDEV + j],
                            device_id=_dev_tuple(j), device_id_type=pl.DeviceIdType.MESH).wait_recv()

    grid_spec = pltpu.PrefetchScalarGridSpec(
        num_scalar_prefetch=1, grid=(NDEV, ni),
        in_specs=[pl.BlockSpec((tm, k), lambda s, i, pidx: (i, 0)),
                  pl.BlockSpec((1, pc), lambda s, i, pidx: (0, pidx[s])), ANY] + [ANY] * ne,
        out_specs=[pl.BlockSpec((tm, pc), lambda s, i, pidx: (i, pidx[s])), ANY] + [ANY] * ne,
        scratch_shapes=[pltpu.VMEM((2, k, pc), BF16),
                        pltpu.SemaphoreType.DMA((NDEV,)), pltpu.SemaphoreType.DMA((NDEV,)),
                        pltpu.SemaphoreType.DMA((2,)), pltpu.SemaphoreType.DMA((1,)),
                        pltpu.SemaphoreType.DMA((ne * NDEV,)), pltpu.SemaphoreType.DMA((ne * NDEV,)),
                        pltpu.SemaphoreType.DMA((ne,))])
    return pl.pallas_call(
        body, name=name, grid_spec=grid_spec,
        out_shape=[jax.ShapeDtypeStruct((m, n), F32), jax.ShapeDtypeStruct((NDEV, k, pc), w_loc.dtype)]
        + [jax.ShapeDtypeStruct((NDEV,) + e.shape, e.dtype) for e in extras],
    )(_peer_schedule(_GATHER_MASKS), u, bias, w_loc, *extras)


def _adam_math(g, w, m, v):
    m2 = ADAM_B1 * m + (1.0 - ADAM_B1) * g
    v2 = ADAM_B2 * v + (1.0 - ADAM_B2) * (g * g)
    delta = -ADAM_LR * ((m2 * ADAM_C1) / (jnp.sqrt(v2 * ADAM_C2) + ADAM_EPS) + ADAM_WD * w)
    return delta, m2, v2


def _row_tile(r, cap):
    t = min(r, cap)
    while r % t:
        t //= 2
    return t


def _adamw(g, w, m, v, name):
    shape = w.shape
    cols = shape[-1] if w.ndim >= 2 and shape[-1] % 128 == 0 else 128
    g2, w2, m2, v2 = (t.reshape(-1, cols) for t in (g, w, m, v))
    r = g2.shape[0]
    tr = _row_tile(r, 256)

    def body(g_ref, w_ref, m_ref, v_ref, d_ref, mo_ref, vo_ref):
        d, mm, vv = _adam_math(g_ref[...], w_ref[...], m_ref[...], v_ref[...])
        d_ref[...] = d
        mo_ref[...] = mm
        vo_ref[...] = vv

    spec = pl.BlockSpec((tr, cols), lambda i: (i, 0))
    outs = pl.pallas_call(
        body, name=name, grid=(r // tr,),
        out_shape=[jax.ShapeDtypeStruct((r, cols), F32)] * 3,
        in_specs=[spec] * 4, out_specs=[spec] * 3,
    )(g2, w2, m2, v2)
    return tuple(o.reshape(shape) for o in outs)


def _sum_adamw(parts, w, m, v, name):
    _, r, c = parts.shape
    shape = w.shape
    w2, m2, v2 = (t.reshape(r, c) for t in (w, m, v))
    tr = _row_tile(r, 128)

    def body(p_ref, w_ref, m_ref, v_ref, g_ref, d_ref, mo_ref, vo_ref):
        g = p_ref[0].astype(F32)
        for k in range(1, NDEV):
            g = g + p_ref[k].astype(F32)
        d, mm, vv = _adam_math(g, w_ref[...], m_ref[...], v_ref[...])
        g_ref[...] = g
        d_ref[...] = d
        mo_ref[...] = mm
        vo_ref[...] = vv

    spec = pl.BlockSpec((tr, c), lambda i: (i, 0))
    outs = pl.pallas_call(
        body, name=name, grid=(r // tr,),
        out_shape=[jax.ShapeDtypeStruct((r, c), F32)] * 4,
        in_specs=[pl.BlockSpec((NDEV, tr, c), lambda i: (0, i, 0))] + [spec] * 3, out_specs=[spec] * 4,
    )(parts, w2, m2, v2)
    return tuple(o.reshape(shape) for o in outs)


def _sum_rows(parts, name):
    _, r, c = parts.shape

    def body(p_ref, o_ref):
        g = p_ref[0]
        for k in range(1, NDEV):
            g = g + p_ref[k]
        o_ref[...] = g

    return pl.pallas_call(
        body, name=name, out_shape=jax.ShapeDtypeStruct((r, c), F32),
    )(parts)


def _prep(cc, w_mod_full, b_mod, lbl):
    def body(cc_ref, w_ref, b_ref, l_ref, mod_ref, lb_ref):
        t = cc_ref[...]
        s = (t * _sigmoid(t)).astype(BF16)
        mod_ref[...] = _dot(s, w_ref[...]) + b_ref[...]
        lb_ref[...] = _sigmoid(l_ref[0] - l_ref[1])

    return pl.pallas_call(
        body, name="prep",
        out_shape=[jax.ShapeDtypeStruct((8, 3 * D), F32), jax.ShapeDtypeStruct((8, D), F32)],
    )(cc, w_mod_full, b_mod, lbl)


def _modulate(xin, mod, row, name):
    n = xin.shape[0]
    tm = _row_tile(n, 512)

    def body(x_ref, mod_ref, u_ref):
        sh = mod_ref[row:row + 1, 0:D]
        sc = mod_ref[row:row + 1, D:2 * D]
        u_ref[...] = (x_ref[...] * (1.0 + sc) + sh).astype(BF16)

    return pl.pallas_call(
        body, name=name, grid=(n // tm,),
        out_shape=jax.ShapeDtypeStruct((n, D), BF16),
        in_specs=[pl.BlockSpec((tm, D), lambda i: (i, 0)), pl.BlockSpec((8, 3 * D), lambda i: (0, 0))],
        out_specs=pl.BlockSpec((tm, D), lambda i: (i, 0)),
    )(xin, mod)


def _mm_bias(a, w_all, bias, name):
    m, k = a.shape
    tn = w_all.shape[2]
    n = tn * NDEV
    tm = _row_tile(m, 512)

    def body(a_ref, b_ref, bias_ref, o_ref):
        o_ref[...] = _dot(a_ref[...], b_ref[0]) + bias_ref[...]

    return pl.pallas_call(
        body, name=name, grid=(NDEV, m // tm),
        out_shape=jax.ShapeDtypeStruct((m, n), F32),
        in_specs=[pl.BlockSpec((tm, k), lambda j, i: (i, 0)), pl.BlockSpec((1, k, tn), lambda j, i: (j, 0, 0)),
                  pl.BlockSpec((1, tn), lambda j, i: (0, j))],
        out_specs=pl.BlockSpec((tm, tn), lambda j, i: (i, j)),
    )(a, w_all, bias)


def _mm_tn(a, b, init, name, with_colsum=False, colsum_init=None, out_dtype=F32):
    m, ka = a.shape
    n = b.shape[1]
    tk = _row_tile(m, 512)
    tn = 1024
    nk = m // tk
    has_init = init is not None

    def body(*refs):
        a_ref, b_ref = refs[0], refs[1]
        pos = 2
        init_ref = cs_init_ref = None
        if has_init:
            init_ref = refs[pos]
            pos += 1
            if with_colsum:
                cs_init_ref = refs[pos]
                pos += 1
        o_ref = refs[pos]
        cs_ref = refs[pos + 1] if with_colsum else None
        acc = refs[-1]
        k = pl.program_id(1)

        @pl.when(k == 0)
        def _():
            if has_init:
                acc[...] = init_ref[...]
                if with_colsum:
                    cs_ref[...] = cs_init_ref[...]
            else:
                acc[...] = jnp.zeros_like(acc)
                if with_colsum:
                    cs_ref[...] = jnp.zeros_like(cs_ref)

        bv = b_ref[...]
        acc[...] += _dot_tn(a_ref[...], bv)
        if with_colsum:
            cs_ref[...] += jnp.sum(bv.astype(F32), axis=0, keepdims=True)

        @pl.when(k == nk - 1)
        def _():
            o_ref[...] = acc[...].astype(out_dtype)

    in_specs = [pl.BlockSpec((tk, ka), lambda j, k: (k, 0)), pl.BlockSpec((tk, tn), lambda j, k: (k, j))]
    args = [a, b]
    if has_init:
        in_specs.append(pl.BlockSpec((ka, tn), lambda j, k: (0, j)))
        args.append(init)
        if with_colsum:
            in_specs.append(pl.BlockSpec((1, tn), lambda j, k: (0, j)))
            args.append(colsum_init)
    out_shape = [jax.ShapeDtypeStruct((ka, n), out_dtype)]
    out_specs = [pl.BlockSpec((ka, tn), lambda j, k: (0, j))]
    if with_colsum:
        out_shape.append(jax.ShapeDtypeStruct((1, n), F32))
        out_specs.append(pl.BlockSpec((1, tn), lambda j, k: (0, j)))
    outs = pl.pallas_call(
        body, name=name, grid=(n // tn, nk), out_shape=out_shape, in_specs=in_specs, out_specs=out_specs,
        scratch_shapes=[pltpu.VMEM((ka, tn), F32)],
    )(*args)
    return outs if with_colsum else outs[0]


def _input_grad(dz, w_all, xin, dr, mod, row, name):
    m, n = dz.shape
    tm = _row_tile(m, 512)
    tk = w_all.shape[2]
    nk = NDEV
    has_dr = dr is not None

    def body(*refs):
        if has_dr:
            dz_ref, w_ref, x_ref, dr_ref, mod_ref, gx_ref, vec_ref, acc = refs
        else:
            dz_ref, w_ref, x_ref, mod_ref, vec_ref, acc = refs
        i, k = pl.program_id(0), pl.program_id(1)

        @pl.when(k == 0)
        def _():
            acc[...] = jnp.zeros_like(acc)

        @pl.when((i == 0) & (k == 0))
        def _():
            vec_ref[...] = jnp.zeros_like(vec_ref)

        acc[...] += _dot_nt(dz_ref[...], w_ref[0])

        @pl.when(k == nk - 1)
        def _():
            du = acc[...]
            xv = x_ref[...]
            if has_dr:
                sc = mod_ref[row:row + 1, D:2 * D]
                gx_ref[...] = ALPHA * dr_ref[...] + du * (1.0 + sc)
            vec_ref[0:1, :] += jnp.sum(du, axis=0, keepdims=True)
            vec_ref[1:2, :] += jnp.sum(du * xv, axis=0, keepdims=True)

    row_spec = pl.BlockSpec((tm, D), lambda i, k: (i, 0))
    in_specs = [pl.BlockSpec((tm, tk), lambda i, k: (i, k)), pl.BlockSpec((1, D, tk), lambda i, k: (k, 0, 0)), row_spec]
    args = [dz, w_all, xin]
    if has_dr:
        in_specs.append(row_spec)
        args.append(dr)
    in_specs.append(pl.BlockSpec((8, 3 * D), lambda i, k: (0, 0)))
    args.append(mod)
    out_shape, out_specs = [], []
    if has_dr:
        out_shape.append(jax.ShapeDtypeStruct((m, D), F32))
        out_specs.append(row_spec)
    out_shape.append(jax.ShapeDtypeStruct((8, D), F32))
    out_specs.append(pl.BlockSpec((8, D), lambda i, k: (0, 0)))
    outs = pl.pallas_call(
        body, name=name, grid=(m // tm, nk), out_shape=out_shape, in_specs=in_specs, out_specs=out_specs,
        scratch_shapes=[pltpu.VMEM((tm, D), F32)],
    )(*args)
    return outs if has_dr else (None, outs[0])


def _tri(reverse):
    r = lax.broadcasted_iota(jnp.int32, (CHUNK, CHUNK), 0)
    c = lax.broadcasted_iota(jnp.int32, (CHUNK, CHUNK), 1)
    return (c >= r) if reverse else (c <= r)


def _cum_f32(tri_b, t):
    hi = t.astype(BF16)
    r1 = t - hi.astype(F32)
    mid = r1.astype(BF16)
    lo = (r1 - mid.astype(F32)).astype(BF16)
    return _dot(tri_b, hi) + _dot(tri_b, mid) + _dot(tri_b, lo)


def _gla_features(zq, zf, lb):
    sq = _sigmoid(zq)
    q = zq * sq * Q_SCALE
    sf = _sigmoid(zf)
    f = lb + (1.0 - lb) * sf
    return q, sq, f, sf


def _gla_decays(f, tri_b, last):
    lf = jnp.log(f)
    g = _cum_f32(tri_b, lf)
    gl = g[last:last + 1, :]
    return g, gl


def _gla_block(n):
    return 256 if n % 256 == 0 else CHUNK


def _gla_fwd(z, lb, s0, d, name):
    n = z.shape[0]
    blk = _gla_block(n)
    nb, npb = n // blk, blk // CHUNK
    reverse = d == 1
    last = 0 if reverse else CHUNK - 1
    order = list(range(npb))[::-1] if reverse else list(range(npb))

    def bmap(i):
        return nb - 1 - i if reverse else i

    hp = GLA_HEADS_PER_STEP
    hw = hp * DH
    units = [(hh, cidx) for hh in range(hp) for cidx in order]

    def body(zq_ref, zf_ref, zv_ref, lb_ref, s0_ref, o_ref, ss_ref, sf_ref, st):
        i = pl.program_id(1)

        @pl.when(i == 0)
        def _():
            st[...] = s0_ref[...]

        mask = _tri(reverse)
        tri_b = jnp.where(mask, 1.0, 0.0).astype(BF16)
        feat = {}
        for u in units:
            hh, cidx = u
            rows, cols = pl.ds(cidx * CHUNK, CHUNK), pl.ds(hh * DH, DH)
            q, _, f, _ = _gla_features(zq_ref[rows, cols], zf_ref[rows, cols], lb_ref[d:d + 1, cols])
            feat[u] = (q, 1.0 - f, jnp.log(f), zv_ref[rows, cols].astype(BF16))
        dec = {u: _cum_f32(tri_b, feat[u][2]) for u in units}
        ops = {}
        for u in units:
            q, k, _, vb = feat[u]
            g = dec[u]
            gl = g[last:last + 1, :]
            ops[u] = ((q * jnp.exp(g)).astype(BF16), (k * jnp.exp(-g)).astype(BF16),
                      (k * jnp.exp(gl - g)).astype(BF16), jnp.exp(gl), vb)
        att = {u: jnp.where(mask, _dot_nt(ops[u][0], ops[u][1]), 0.0).astype(BF16) for u in units}
        upd = {u: _dot_tn(ops[u][4], ops[u][2]) for u in units}
        intra = {u: _dot(att[u], ops[u][4]) for u in units}
        s_in = {}
        for hh in range(hp):
            s = st[hh]
            for cidx in order:
                s_in[(hh, cidx)] = s
                s = s * ops[(hh, cidx)][3] + upd[(hh, cidx)]
            st[hh] = s
            sf_ref[hh] = s
        for u in units:
            hh, cidx = u
            rows, cols = pl.ds(cidx * CHUNK, CHUNK), pl.ds(hh * DH, DH)
            o_ref[rows, cols] = intra[u] + _dot_nt(ops[u][0], s_in[u].astype(BF16))
            ss_ref[hh, cidx] = s_in[u]

    def col(g):
        return lambda h, i: (bmap(i), g * (NH // hp) + h)

    return pl.pallas_call(
        body, name=name, grid=(NH // hp, nb),
        out_shape=[jax.ShapeDtypeStruct((n, D), F32), jax.ShapeDtypeStruct((NH, n // CHUNK, DH, DH), F32),
                   jax.ShapeDtypeStruct((NH, DH, DH), F32)],
        in_specs=[pl.BlockSpec((blk, hw), col(0)), pl.BlockSpec((blk, hw), col(1 + d)),
                  pl.BlockSpec((blk, hw), col(3)), pl.BlockSpec((8, hw), lambda h, i: (0, h)),
                  pl.BlockSpec((hp, DH, DH), lambda h, i: (h, 0, 0))],
        out_specs=[pl.BlockSpec((blk, hw), lambda h, i: (bmap(i), h)),
                   pl.BlockSpec((hp, npb, DH, DH), lambda h, i: (h, bmap(i), 0, 0)),
                   pl.BlockSpec((hp, DH, DH), lambda h, i: (h, 0, 0))],
        scratch_shapes=[pltpu.VMEM((hp, DH, DH), F32)],
    )(z, z, z, lb, s0)


def _gla_bwd(z, lb, s_start, do, ds_fin, acc_q, acc_v, d, name, f_dtype=F32, into=None):
    n = z.shape[0]
    blk = _gla_block(n)
    nb, npb = n // blk, blk // CHUNK
    reverse = d == 1
    last = 0 if reverse else CHUNK - 1
    order = list(range(npb)) if reverse else list(range(npb))[::-1]
    has_do = do is not None
    has_acc = acc_q is not None
    fused = into is not None
    assert not fused or d == 1
    hp = NH if fused else GLA_HEADS_PER_STEP
    hw = hp * DH
    units = [(hh, cidx) for hh in range(hp) for cidx in order]

    def bmap(i):
        return i if reverse else nb - 1 - i

    def body(*refs):
        zq_ref, zf_ref, zv_ref, lb_ref, ss_ref, dsf_ref = refs[:6]
        pos = 6
        do_ref = aq_ref = av_ref = None
        if has_do:
            do_ref = refs[pos]
            pos += 1
        if has_acc:
            aq_ref, av_ref = refs[pos], refs[pos + 1]
            pos += 2
        if fused:
            other_ref = refs[pos + 1]
            dz_ref, dlb_ref, ds0_ref, dst = refs[pos + 2:]
            dz_ref[:, D:2 * D] = other_ref[...]
        else:
            dzq_ref, dzf_ref, dzv_ref, dlb_ref, ds0_ref, dst = refs[pos:]
        i = pl.program_id(1)

        @pl.when(i == 0)
        def _():
            dst[...] = dsf_ref[...]
            dlb_ref[...] = jnp.zeros_like(dlb_ref)

        mask = _tri(reverse)
        tri_b = jnp.where(mask, 1.0, 0.0).astype(BF16)
        tri_t = jnp.where(_tri(not reverse), 1.0, 0.0).astype(BF16)

        def where(u):
            return pl.ds(u[1] * CHUNK, CHUNK), pl.ds(u[0] * DH, DH)

        feat = {}
        for u in units:
            rows, cols = where(u)
            zq, zf = zq_ref[rows, cols], zf_ref[rows, cols]
            lbv = lb_ref[d:d + 1, cols]
            q, sq, f, sf = _gla_features(zq, zf, lbv)
            feat[u] = dict(zq=zq, q=q, sq=sq, f=f, sf=sf, lbv=lbv, k=1.0 - f, vb=zv_ref[rows, cols].astype(BF16))
        dec = {u: _cum_f32(tri_b, jnp.log(feat[u]["f"])) for u in units}
        for u in units:
            w = feat[u]
            g = dec[u]
            gl = g[last:last + 1, :]
            w["eg"], w["egi"], w["ege"], w["egl"] = jnp.exp(g), jnp.exp(-g), jnp.exp(gl - g), jnp.exp(gl)
            w["qd"], w["ki"], w["ke"] = w["q"] * w["eg"], w["k"] * w["egi"], w["k"] * w["ege"]
            w["qdb"], w["kib"], w["keb"] = w["qd"].astype(BF16), w["ki"].astype(BF16), w["ke"].astype(BF16)
            w["s_in"] = ss_ref[u[0], u[1]]
        if has_do:
            for u in units:
                w = feat[u]
                rows, cols = where(u)
                w["dob"] = do_ref[rows, cols].astype(BF16)
            for u in units:
                w = feat[u]
                w["a"] = jnp.where(mask, _dot_nt(w["qdb"], w["kib"]), 0.0).astype(BF16)
                w["da"] = jnp.where(mask, _dot_nt(w["dob"], w["vb"]), 0.0).astype(BF16)
                w["m"] = _dot_tn(w["dob"], w["qdb"])
        for hh in range(hp):
            ds = dst[hh]
            for cidx in order:
                w = feat[(hh, cidx)]
                w["ds"] = ds
                ds = ds * w["egl"]
                if has_do:
                    ds = ds + w["m"]
            dst[hh] = ds
            ds0_ref[hh] = ds
        for u in units:
            w = feat[u]
            dsb = w["ds"].astype(BF16)
            w["dke"] = _dot(w["vb"], dsb)
            w["dv"] = _dot_nt(w["keb"], dsb)
            if has_do:
                w["dv"] = w["dv"] + _dot_tn(w["a"], w["dob"])
                w["dqd"] = _dot(w["da"], w["kib"]) + _dot(w["dob"], w["s_in"].astype(BF16))
                w["dki"] = _dot_tn(w["da"], w["qdb"])
        for u in units:
            w = feat[u]
            dkeke = w["dke"] * w["ke"]
            w["dgl"] = (w["egl"] * jnp.sum(w["s_in"] * w["ds"], axis=0, keepdims=True)
                        + jnp.sum(dkeke, axis=0, keepdims=True))
            dg = -dkeke
            dk = w["dke"] * w["ege"]
            if has_do:
                dg = dg + w["dqd"] * w["qd"] - w["dki"] * w["ki"]
                dk = dk + w["dki"] * w["egi"]
            w["dg"], w["dk"] = dg, dk
        dlf = {u: _cum_f32(tri_t, feat[u]["dg"]) for u in units}
        for u in units:
            w = feat[u]
            rows, cols = where(u)
            df = (dlf[u] + w["dgl"]) / w["f"] - w["dk"]
            sf = w["sf"]
            dzf = df * (1.0 - w["lbv"]) * sf * (1.0 - sf)
            dlb_ref[0:1, cols] += jnp.sum(df * (1.0 - sf), axis=0, keepdims=True)
            if has_do:
                dzq = w["dqd"] * w["eg"] * (Q_SCALE * _dsilu(w["zq"], w["sq"]))
            else:
                dzq = jnp.zeros((CHUNK, DH), F32)
            dv = w["dv"]
            if has_acc:
                dzq = dzq + aq_ref[rows, cols]
                dv = dv + av_ref[rows, cols]
            if fused:
                lane = u[0] * DH
                dz_ref[rows, pl.ds(lane, DH)] = dzq.astype(BF16)
                dz_ref[rows, pl.ds(2 * D + lane, DH)] = dzf.astype(BF16)
                dz_ref[rows, pl.ds(3 * D + lane, DH)] = dv.astype(BF16)
            else:
                dzq_ref[rows, cols] = dzq
                dzf_ref[rows, cols] = dzf.astype(f_dtype)
                dzv_ref[rows, cols] = dv

    def col(g):
        return lambda h, i: (bmap(i), g * (NH // hp) + h)

    tok = pl.BlockSpec((blk, hw), lambda h, i: (bmap(i), h))
    state = pl.BlockSpec((hp, DH, DH), lambda h, i: (h, 0, 0))
    in_specs = [pl.BlockSpec((blk, hw), col(0)), pl.BlockSpec((blk, hw), col(1 + d)), pl.BlockSpec((blk, hw), col(3)),
                pl.BlockSpec((8, hw), lambda h, i: (0, h)),
                pl.BlockSpec((hp, npb, DH, DH), lambda h, i: (h, bmap(i), 0, 0)), state]
    args = [z, z, z, lb, s_start, ds_fin]
    if has_do:
        in_specs.append(tok)
        args.append(do)
    if has_acc:
        in_specs += [tok, tok]
        args += [acc_q, acc_v]
    tail_shape = [jax.ShapeDtypeStruct((8, D), F32), jax.ShapeDtypeStruct((NH, DH, DH), F32)]
    tail_specs = [pl.BlockSpec((8, hw), lambda h, i: (0, h)), state]
    if fused:
        buf, other = into
        aliases = {len(args): 0}
        in_specs += [ANY, tok]
        args += [buf, other]
        out_shape = [jax.ShapeDtypeStruct(buf.shape, buf.dtype)] + tail_shape
        out_specs = [pl.BlockSpec((blk, 4 * D), lambda h, i: (bmap(i), 0))] + tail_specs
    else:
        aliases = {}
        out_shape = [jax.ShapeDtypeStruct((n, D), F32), jax.ShapeDtypeStruct((n, D), f_dtype),
                     jax.ShapeDtypeStruct((n, D), F32)] + tail_shape
        out_specs = [tok, tok, tok] + tail_specs
    return pl.pallas_call(
        body, name=name, grid=(NH // hp, nb), out_shape=out_shape, in_specs=in_specs, out_specs=out_specs,
        input_output_aliases=aliases, scratch_shapes=[pltpu.VMEM((hp, DH, DH), F32)],
    )(*args)


def _shift(t, s, fill, down):
    n = t.shape[0]
    rows = lax.broadcasted_iota(jnp.int32, t.shape, 0)
    if down:
        return jnp.where(rows >= s, pltpu.roll(t, s, 0), fill)
    return jnp.where(rows < n - s, pltpu.roll(t, n - s, 0), fill)


SUBLANES = 8
LRU_SAVED = 5


def _chain_scan(a, b, h_in, down):
    n = a.shape[0]
    ng = n // SUBLANES
    rows = lax.broadcasted_iota(jnp.int32, (SUBLANES, a.shape[1]), 0)
    local = []
    for g in range(ng):
        aa, bb = a[g * SUBLANES:(g + 1) * SUBLANES], b[g * SUBLANES:(g + 1) * SUBLANES]
        for s in (1, 2, 4):
            if down:
                keep, amt = rows >= s, s
            else:
                keep, amt = rows < SUBLANES - s, SUBLANES - s
            bb = bb + aa * jnp.where(keep, pltpu.roll(bb, amt, 0), 0.0)
            aa = aa * jnp.where(keep, pltpu.roll(aa, amt, 0), 1.0)
        local.append((aa, bb))
    out = [None] * ng
    carry = h_in
    for g in (range(ng) if down else range(ng - 1, -1, -1)):
        aa, bb = local[g]
        hg = bb + aa * carry
        out[g] = hg
        carry = hg[SUBLANES - 1:SUBLANES] if down else hg[0:1]
    return (jnp.concatenate(out, axis=0) if ng > 1 else out[0]), carry


def _conv_taps(xv):
    return (_shift(xv, 1, 0.0, True), xv, _shift(xv, 1, 0.0, False), _shift(xv, 2, 0.0, False))


def _conv(taps, cw, cb):
    return cb + cw[0:1, :] * taps[0] + cw[1:2, :] * taps[1] + cw[2:3, :] * taps[2] + cw[3:4, :] * taps[3]


def _neg_expm1(t):
    series = -t * (1.0 + t * (0.5 + t * (1.0 / 6.0 + t * (1.0 / 24.0 + t * (1.0 / 120.0)))))
    return jnp.where(t > -0.1, series, 1.0 - jnp.exp(t))


def _lru_gates(xc, wr, br, wi, bi, lam):
    xcb = xc.astype(BF16)
    r = _sigmoid(_dot(xcb, wr) + br)
    gi = _sigmoid(_dot(xcb, wi) + bi)
    sp = jnp.maximum(-lam, 0.0) + jnp.log(1.0 + jnp.exp(-jnp.abs(lam)))
    la = -RG_C * r * sp
    a = jnp.exp(la)
    mult = jnp.sqrt(_neg_expm1(2.0 * la))
    return xcb, r, gi, sp, a, mult


def _lru_fwd(xin, blk, cw, cb, wr, br, wi, bi, lam, h0, acc_h, d, name):
    n = xin.shape[0]
    nb = n // blk
    reverse = d == 1
    down = not reverse
    has_acc = acc_h is not None

    def bmap(i):
        return nb - 1 - i if reverse else i

    def body(*refs):
        x_ref, cw_ref, cb_ref, wr_ref, br_ref, wi_ref, bi_ref, lam_ref, h0_ref = refs[:9]
        acc_ref = refs[9] if has_acc else None
        outs = refs[10:] if has_acc else refs[9:]
        h_ref, hin_ref, hfin_ref, sav_ref = outs[:4]
        hsum_ref = outs[4] if has_acc else None
        carry = outs[-1]
        i = pl.program_id(0)

        @pl.when(i == 0)
        def _():
            carry[...] = h0_ref[...]

        for g in range(NH):
            cols = pl.ds(g * DH, DH)
            xc = _conv(_conv_taps(x_ref[:, cols]), cw_ref[:, cols], cb_ref[:, cols])
            _, r, gi, _, a, mult = _lru_gates(xc, wr_ref[g], br_ref[:, cols], wi_ref[g], bi_ref[:, cols],
                                              lam_ref[:, cols])
            for slot, val in enumerate((xc, r, gi, a, mult)):
                sav_ref[slot, :, cols] = val
            hin = carry[:, cols]
            h, h_last = _chain_scan(a, mult * gi * xc, hin, down)
            h_ref[:, cols] = h
            if has_acc:
                hsum_ref[:, cols] = h + acc_ref[:, cols]
            hin_ref[0, :, cols] = hin
            carry[:, cols] = h_last
            hfin_ref[:, cols] = h_last

    vec = pl.BlockSpec((1, D), lambda i: (0, 0))
    wsp = pl.BlockSpec((NH, DH, DH), lambda i: (0, 0, 0))
    tok = pl.BlockSpec((blk, D), lambda i: (bmap(i), 0))
    in_specs = [tok, pl.BlockSpec((4, D), lambda i: (0, 0)), vec, wsp, vec, wsp, vec, vec, vec]
    args = [xin, cw, cb, wr, br, wi, bi, lam, h0]
    out_shape = [jax.ShapeDtypeStruct((n, D), F32), jax.ShapeDtypeStruct((nb, 1, D), F32),
                 jax.ShapeDtypeStruct((1, D), F32), jax.ShapeDtypeStruct((LRU_SAVED, n, D), F32)]
    out_specs = [tok, pl.BlockSpec((1, 1, D), lambda i: (bmap(i), 0, 0)), vec,
                 pl.BlockSpec((LRU_SAVED, blk, D), lambda i: (0, bmap(i), 0))]
    if has_acc:
        in_specs.append(tok)
        args.append(acc_h)
        out_shape.append(jax.ShapeDtypeStruct((n, D), F32))
        out_specs.append(tok)
    return pl.pallas_call(
        body, name=name, grid=(nb,), out_shape=out_shape, in_specs=in_specs, out_specs=out_specs,
        scratch_shapes=[pltpu.VMEM((1, D), F32)],
    )(*args)


def _lru_bwd(xin, blk, cw, wr, wi, lam, sav, h, hin, dh, cg_fin, acc_dx, init, d, name):
    n = xin.shape[0]
    nb = n // blk
    reverse = d == 1
    down = not reverse
    first = blk - 1 if reverse else 0
    has_dh = dh is not None
    has_acc = acc_dx is not None
    has_init = init is not None

    def bmap(i):
        return i if reverse else nb - 1 - i

    def body(*refs):
        (x_ref, cw_ref, wr_ref, wi_ref, lam_ref, sav_ref, h_ref, hin_ref, cgf_ref) = refs[:9]
        pos = 9
        dh_ref = acc_ref = None
        iwr_ref = iwi_ref = ivec_ref = None
        if has_dh:
            dh_ref = refs[pos]
            pos += 1
        if has_acc:
            acc_ref = refs[pos]
            pos += 1
        if has_init:
            iwr_ref, iwi_ref, ivec_ref = refs[pos:pos + 3]
            pos += 3
        dx_ref, dwr_ref, dwi_ref, vec_ref, cg0_ref, carry = refs[pos:]
        i = pl.program_id(0)

        @pl.when(i == 0)
        def _():
            carry[...] = cgf_ref[...]
            if has_init:
                dwr_ref[...] = iwr_ref[...]
                dwi_ref[...] = iwi_ref[...]
                vec_ref[...] = ivec_ref[...]
            else:
                dwr_ref[...] = jnp.zeros_like(dwr_ref)
                dwi_ref[...] = jnp.zeros_like(dwi_ref)
                vec_ref[...] = jnp.zeros_like(vec_ref)

        for g in range(NH):
            cols = pl.ds(g * DH, DH)
            cwv = cw_ref[:, cols]
            lam_v = lam_ref[:, cols]
            taps = _conv_taps(x_ref[:, cols])
            wr_g, wi_g = wr_ref[g], wi_ref[g]
            xc, r, gi, a, mult = (sav_ref[slot, :, cols] for slot in range(LRU_SAVED))
            xcb = xc.astype(BF16)
            sp = jnp.maximum(-lam_v, 0.0) + jnp.log(1.0 + jnp.exp(-jnp.abs(lam_v)))
            hprev = _shift(h_ref[:, cols], 1, hin_ref[0, :, cols], down)
            a_next = _shift(a, 1, 1.0, not down)
            dhv = dh_ref[:, cols] if has_dh else jnp.zeros_like(a)
            e, _ = _chain_scan(a_next, dhv, carry[:, cols], not down)
            cg = a[first:first + 1, :] * e[first:first + 1, :]
            carry[:, cols] = cg
            cg0_ref[:, cols] = cg
            da = e * hprev
            emult = e * mult
            dgi = emult * xc
            dxc = emult * gi
            dla = da * a - (e * gi * xc) * (a * a) / mult
            dr = dla * (-RG_C * sp)
            sneg = 1.0 - _sigmoid(lam_v)
            dpr = dr * r * (1.0 - r)
            dpi = dgi * gi * (1.0 - gi)
            dprb, dpib = dpr.astype(BF16), dpi.astype(BF16)
            dxc = dxc + _dot_nt(dprb, wr_g) + _dot_nt(dpib, wi_g)
            dwr_ref[g] += _dot_tn(xcb, dprb)
            dwi_ref[g] += _dot_tn(xcb, dpib)
            dx = (cwv[0:1, :] * _shift(dxc, 1, 0.0, False) + cwv[1:2, :] * dxc
                  + cwv[2:3, :] * _shift(dxc, 1, 0.0, True) + cwv[3:4, :] * _shift(dxc, 2, 0.0, True))
            if has_acc:
                dx = dx + acc_ref[:, cols]
            dx_ref[:, cols] = dx
            vec_ref[0:1, cols] += jnp.sum(dpr, axis=0, keepdims=True)
            vec_ref[1:2, cols] += jnp.sum(dpi, axis=0, keepdims=True)
            vec_ref[2:3, cols] += jnp.sum(dla * r, axis=0, keepdims=True) * (RG_C * sneg)
            vec_ref[3:4, cols] += jnp.sum(dxc, axis=0, keepdims=True)
            for kk in range(4):
                vec_ref[4 + kk:5 + kk, cols] += jnp.sum(dxc * taps[kk], axis=0, keepdims=True)

    vec = pl.BlockSpec((1, D), lambda i: (0, 0))
    wsp = pl.BlockSpec((NH, DH, DH), lambda i: (0, 0, 0))
    tok = pl.BlockSpec((blk, D), lambda i: (bmap(i), 0))
    vec16 = pl.BlockSpec((16, D), lambda i: (0, 0))
    in_specs = [tok, pl.BlockSpec((4, D), lambda i: (0, 0)), wsp, wsp, vec,
                pl.BlockSpec((LRU_SAVED, blk, D), lambda i: (0, bmap(i), 0)), tok,
                pl.BlockSpec((1, 1, D), lambda i: (bmap(i), 0, 0)), vec]
    args = [xin, cw, wr, wi, lam, sav, h, hin, cg_fin]
    if has_dh:
        in_specs.append(tok)
        args.append(dh)
    if has_acc:
        in_specs.append(tok)
        args.append(acc_dx)
    if has_init:
        in_specs += [wsp, wsp, vec16]
        args += list(init)
    return pl.pallas_call(
        body, name=name, grid=(nb,),
        out_shape=[jax.ShapeDtypeStruct((n, D), F32), jax.ShapeDtypeStruct((NH, DH, DH), F32),
                   jax.ShapeDtypeStruct((NH, DH, DH), F32), jax.ShapeDtypeStruct((16, D), F32),
                   jax.ShapeDtypeStruct((1, D), F32)],
        in_specs=in_specs, out_specs=[tok, wsp, wsp, vec16, vec],
        scratch_shapes=[pltpu.VMEM((1, D), F32)],
    )(*args)


def _merge(z, o_f, o_b, hx, xin, tgt, mod, gn, p_a, p_b, w_out, ln_g, ln_b):
    n = xin.shape[0]
    tm = _row_tile(n, 128)

    def body(z4_ref, z6_ref, z7_ref, z8_ref, of_ref, ob_ref, hx_ref, x_ref, t_ref, mod_ref, gn_ref,
             pa_ref, pb_ref, wo_ref, lg_ref, lnb_ref,
             dr_ref, do_ref, dhx_ref, dz_ref, oa_o, obb_o, y_o, dya_o, dyb_o, dout_o, vec_ref):
        @pl.when(pl.program_id(0) == 0)
        def _():
            vec_ref[...] = jnp.zeros_like(vec_ref)

        gt = mod_ref[0:1, 2 * D:3 * D]
        gnv = gn_ref[...]
        o = of_ref[...] + ob_ref[...]
        rs = jnp.concatenate(
            [jnp.broadcast_to(lax.rsqrt(jnp.mean(jnp.square(o[:, h * DH:(h + 1) * DH]), axis=1, keepdims=True)
                                        + RMS_EPS), (tm, DH)) for h in range(NH)], axis=1)
        nrm = o * rs
        rn = nrm * gnv
        z4, z6, z7, z8 = z4_ref[...], z6_ref[...], z7_ref[...], z8_ref[...]
        s4, s6, s7, s8 = _sigmoid(z4), _sigmoid(z6), _sigmoid(z7), _sigmoid(z8)
        sg4, sg6 = z4 * s4, z6 * s6
        hxv = hx_ref[...]
        oa = (rn * sg4).astype(BF16)
        obb = (hxv * sg6).astype(BF16)
        ya = _dot(oa, pa_ref[...])
        yb = _dot(obb, pb_ref[...])
        y = (s7 * ya + s8 * yb).astype(BF16)
        out = _dot(y, wo_ref[...])
        xv = x_ref[...]
        rr = ALPHA * xv + gt * out
        mu = jnp.mean(rr, axis=1, keepdims=True)
        cen = rr - mu
        rstd = lax.rsqrt(jnp.mean(cen * cen, axis=1, keepdims=True) + LN_EPS)
        xhat = cen * rstd
        lg = lg_ref[...]
        err = xhat * lg + lnb_ref[...] - t_ref[...]
        loss_rows = jnp.sum(err * err, axis=1, keepdims=True)
        dxn = err * (1.0 / D)
        dxh = dxn * lg
        dr = rstd * (dxh - jnp.mean(dxh, axis=1, keepdims=True)
                     - xhat * jnp.mean(dxh * xhat, axis=1, keepdims=True))
        dout = (dr * gt).astype(BF16)
        dy = _dot_nt(dout, wo_ref[...])
        dya = (dy * s7).astype(BF16)
        dyb = (dy * s8).astype(BF16)
        doa = _dot_nt(dya, pa_ref[...])
        dob = _dot_nt(dyb, pb_ref[...])
        drn = doa * sg4
        dn = drn * gnv
        dnn = dn * nrm
        corr = jnp.concatenate(
            [jnp.broadcast_to(jnp.mean(dnn[:, h * DH:(h + 1) * DH], axis=1, keepdims=True), (tm, DH))
             for h in range(NH)], axis=1)
        dr_ref[...] = dr
        do_ref[...] = rs * (dn - nrm * corr)
        dhx_ref[...] = dob * sg6
        dz_ref[:, 0:4 * D] = jnp.zeros((tm, 4 * D), BF16)
        dz_ref[:, 4 * D:5 * D] = (doa * rn * _dsilu(z4, s4)).astype(BF16)
        dz_ref[:, 5 * D:6 * D] = jnp.zeros((tm, D), BF16)
        dz_ref[:, 6 * D:7 * D] = (dob * hxv * _dsilu(z6, s6)).astype(BF16)
        dz_ref[:, 7 * D:8 * D] = (dy * ya * s7 * (1.0 - s7)).astype(BF16)
        dz_ref[:, 8 * D:9 * D] = (dy * yb * s8 * (1.0 - s8)).astype(BF16)
        oa_o[...] = oa
        obb_o[...] = obb
        y_o[...] = y
        dya_o[...] = dya
        dyb_o[...] = dyb
        dout_o[...] = dout
        vec_ref[0:1, :] += jnp.sum(dr * out, axis=0, keepdims=True)
        vec_ref[1:2, :] += jnp.sum(dxn * xhat, axis=0, keepdims=True)
        vec_ref[2:3, :] += jnp.sum(dxn, axis=0, keepdims=True)
        vec_ref[3:4, :] += jnp.sum(drn * nrm, axis=0, keepdims=True)
        vec_ref[4:5, :] += jnp.broadcast_to(jnp.sum(loss_rows, axis=0, keepdims=True) * (0.5 / D), (1, D))

    def grp(g):
        return pl.BlockSpec((tm, D), lambda i: (i, g))

    tok = pl.BlockSpec((tm, D), lambda i: (i, 0))
    vec = pl.BlockSpec((1, D), lambda i: (0, 0))
    wsp = pl.BlockSpec((D, D), lambda i: (0, 0))
    return pl.pallas_call(
        body, name="merge", grid=(n // tm,),
        out_shape=[jax.ShapeDtypeStruct((n, D), F32)] * 3
        + [jax.ShapeDtypeStruct((n, NGRP * D), BF16)]
        + [jax.ShapeDtypeStruct((n, D), BF16)] * 6 + [jax.ShapeDtypeStruct((8, D), F32)],
        in_specs=[grp(4), grp(6), grp(7), grp(8), tok, tok, tok, tok, tok,
                  pl.BlockSpec((8, 3 * D), lambda i: (0, 0)), vec, wsp, wsp, wsp, vec, vec],
        out_specs=[tok, tok, tok, pl.BlockSpec((tm, NGRP * D), lambda i: (i, 0))] + [tok] * 6
        + [pl.BlockSpec((8, D), lambda i: (0, 0))],
    )(z, z, z, z, o_f, o_b, hx, xin, tgt, mod, gn, p_a, p_b, w_out, ln_g, ln_b)


def _wmod_grad(c_t, cctx_t, dmx_loc, dmc_loc, name):
    n = dmx_loc.shape[1]

    def body(ct_ref, cc_ref, dmx_ref, dmc_ref, o_ref):
        ct = ct_ref[...]
        sct = ct * _sigmoid(ct)
        cc = cc_ref[...]
        scc = cc * _sigmoid(cc)
        dmc = dmc_ref[0:1, :]
        for b in range(1, NDEV):
            dmc = dmc + dmc_ref[b:b + 1, :]
        acc = scc * dmc
        for b in range(NDEV):
            acc = acc + sct[:, b:b + 1] * dmx_ref[b:b + 1, :]
        o_ref[...] = acc

    return pl.pallas_call(body, name=name, out_shape=jax.ShapeDtypeStruct((D, n), F32))(c_t, cctx_t, dmx_loc, dmc_loc)


PACK_ROWS = 40


def _finalize_small(g_pack, lb, w_mod_full, params):
    npar = len(params)

    def body(*refs):
        gp_ref, lb_ref, wm_ref = refs[:3]
        wmv = refs[3:3 + 3 * npar]
        loss_ref = refs[3 + 3 * npar]
        g_refs = refs[4 + 3 * npar:4 + 4 * npar]
        upd = refs[4 + 4 * npar:4 + 7 * npar]
        tot = refs[-1]
        acc = gp_ref[0]
        for k in range(1, NDEV):
            acc = acc + gp_ref[k]
        tot[...] = acc
        mine = pl.ds(pl.multiple_of(_my_index() * DH, DH), DH)
        (g_cctx, g_bmod, g_bin, g_lbl, g_norm, g_cw, g_cb, g_br, g_bi, g_lam, g_lng, g_lnb) = g_refs

        loss_ref[...] = jnp.broadcast_to(tot[36:37, 0:DH], (8, DH))
        for k in range(3):
            g_bmod[:, k * D:(k + 1) * D] = tot[k:k + 1, :] + tot[3 + k:4 + k, :]
        dmc = jnp.concatenate([tot[3:4, :], tot[4:5, :], tot[5:6, :]], axis=1)
        cv = wmv[0][...]
        proj = _dot_nt(jnp.broadcast_to(dmc, (8, 3 * D)).astype(BF16), wm_ref[...])
        g_cctx[...] = proj[0:1, :] * _dsilu(cv, _sigmoid(cv))
        for k in range(NGRP):
            g_bin[:, k * D:(k + 1) * D] = tot[6 + k:7 + k, :]
        nrm = tot[15:16, 0:DH]
        for h in range(1, NH):
            nrm = nrm + tot[15:16, h * DH:(h + 1) * DH]
        g_norm[...] = nrm
        g_lng[...] = tot[16:17, :]
        g_lnb[...] = tot[17:18, :]
        g_cb[...] = tot[21:22, :] + tot[29:30, :]
        g_cw[0] = tot[22:26, mine] + tot[30:34, mine]
        for ref, row in ((g_br, 18), (g_bi, 19), (g_lam, 20)):
            ref[0, 0:1, :] = tot[row:row + 1, mine]
            ref[0, 1:2, :] = tot[row + 8:row + 9, mine]
        lbl = lb_ref[0:2, mine]
        dl0 = tot[34:36, mine] * lbl * (1.0 - lbl)
        g_lbl[0] = dl0
        g_lbl[1] = -dl0
        for p in range(npar):
            d, mm, vv = _adam_math(g_refs[p][...], wmv[3 * p][...], wmv[3 * p + 1][...], wmv[3 * p + 2][...])
            upd[3 * p][...] = d
            upd[3 * p + 1][...] = mm
            upd[3 * p + 2][...] = vv

    flat = [t for wmv in params for t in wmv]
    shapes = [jax.ShapeDtypeStruct(wmv[0].shape, F32) for wmv in params]
    outs = pl.pallas_call(
        body, name="finalize_small",
        out_shape=[jax.ShapeDtypeStruct((8, DH), F32)] + shapes + [s for s in shapes for _ in range(3)],
        scratch_shapes=[pltpu.VMEM((PACK_ROWS, D), F32)],
    )(g_pack, lb, w_mod_full, *flat)
    grads = list(outs[1:1 + npar])
    upd = [tuple(outs[1 + npar + 3 * p:4 + npar + 3 * p]) for p in range(npar)]
    return outs[0], grads, upd


def _to_colmajor(t, rows):
    return t.reshape(rows, GRID_W, D).transpose(1, 0, 2).reshape(rows * GRID_W, D)


def _to_raster(t, rows):
    return t.reshape(GRID_W, rows, D).transpose(1, 0, 2).reshape(rows * GRID_W, D)


def _local_cols(t, me, width):
    return lax.dynamic_slice_in_dim(t, me * width, width, axis=t.ndim - 1)


def kernel(x, c, ctx, c_ctx, w_mod, b_mod, w_in, b_in, lb_logits, norm_a_g, conv_w, conv_b, w_r, b_r, w_i, b_i, lam, p_a, p_b, w_out, ln_g, ln_b, loss_target, m_c_ctx, m_w_mod, m_b_mod, m_w_in, m_b_in, m_lb_logits, m_norm_a_g, m_conv_w, m_conv_b, m_w_r, m_b_r, m_w_i, m_b_i, m_lam, m_p_a, m_p_b, m_w_out, m_ln_g, m_ln_b, v_c_ctx, v_w_mod, v_b_mod, v_w_in, v_b_in, v_lb_logits, v_norm_a_g, v_conv_w, v_conv_b, v_w_r, v_b_r, v_w_i, v_b_i, v_lam, v_p_a, v_p_b, v_w_out, v_ln_g, v_ln_b):
    me = _my_index()
    xs, cs, tgt = x[0], ctx[0], loss_target[0]
    t_len, c_len = xs.shape[0], cs.shape[0]
    rows = t_len // GRID_W
    wcols = w_in.shape[2]
    mcols = w_mod.shape[2]

    small = jnp.concatenate([lb_logits.reshape(4, DH), conv_w[0], b_r[0], b_i[0], lam[0], jnp.zeros((2, DH), F32),
                             c.reshape(8, DH)], axis=0)
    g_small, g_wmod = _all_gather([small, w_mod[0].astype(BF16)], "gather_params")

    def full_rows(lo, hi):
        return g_small[:, lo:hi, :].transpose(1, 0, 2).reshape(hi - lo, D)

    lbl_f, cw_f, br_f, bi_f, lam_f = full_rows(0, 4), full_rows(4, 8), full_rows(8, 10), full_rows(10, 12), full_rows(12, 14)
    c_all = g_small[:, 16:24, :].reshape(NDEV, D)
    w_mod_f = g_wmod.transpose(1, 0, 2).reshape(D, 3 * D)
    w_r_b, w_i_b = w_r[0].astype(BF16), w_i[0].astype(BF16)

    cc = jnp.concatenate([c.reshape(1, D), c_ctx.reshape(1, D), jnp.zeros((6, D), F32)], axis=0)
    lbl_p = jnp.concatenate([lbl_f.reshape(2, 2, D), jnp.zeros((2, 6, D), F32)], axis=1)
    mod, lb = _prep(cc, w_mod_f, b_mod, lbl_p)
    u_x = _modulate(xs, mod, 0, "modulate_x")
    u_c = _modulate(cs, mod, 1, "modulate_c")
    z_x, w_in_f, g_pa, g_pb, g_wo = _inproj_gather(
        u_x, w_in[0].astype(BF16), b_in, [p_a[0].astype(BF16), p_b[0].astype(BF16), w_out[0].astype(BF16)],
        "inproj_gather")
    p_a_f, p_b_f, w_out_f = g_pa.reshape(D, D), g_pb.reshape(D, D), g_wo.reshape(D, D)
    z_c = _mm_bias(u_c, w_in_f, b_in, "inproj_c")

    zero_s = jnp.zeros((NH, DH, DH), F32)
    zero_v = jnp.zeros((1, D), F32)
    gla = {}
    for d in (0, 1):
        _, ssc, sfc = _gla_fwd(z_c, lb, zero_s, d, f"gla_fwd_c{d}")
        o_d, ssx, _ = _gla_fwd(z_x, lb, sfc, d, f"gla_fwd_x{d}")
        gla[d] = (ssc, ssx, o_d)

    x5_c = z_c[:, 5 * D:6 * D]
    x5_x = _to_colmajor(z_x[:, 5 * D:6 * D], rows)
    cb2 = conv_b.reshape(1, D)
    lru = {}
    h_sum = None
    for d in (0, 1):
        prm = (cw_f, cb2, w_r_b[d], br_f[d:d + 1], w_i_b[d], bi_f[d:d + 1], lam_f[d:d + 1])
        h_c, hin_c, hfin_c, sav_c = _lru_fwd(x5_c, c_len, *prm, zero_v, None, d, f"lru_fwd_c{d}")
        h_x, hin_x, _, sav_x, *h_sum = _lru_fwd(x5_x, rows, *prm, hfin_c, lru[0][3] if d else None, d,
                                                f"lru_fwd_x{d}")
        lru[d] = ((cw_f, w_r_b[d], w_i_b[d], lam_f[d:d + 1]), h_c, hin_c, h_x, hin_x, sav_c, sav_x)
    hx = _to_raster(h_sum[0], rows)

    gn = jnp.tile(norm_a_g.reshape(1, DH), (1, NH))
    (dr, do, dhx, dz_m, oa, obb, yb16, dya, dyb, dout, mvec) = _merge(
        z_x, gla[0][2], gla[1][2], hx, xs, tgt, mod, gn, p_a_f, p_b_f, w_out_f, ln_g, ln_b)

    dhx_cm = _to_colmajor(dhx, rows)
    lru_dx_x = lru_dx_c = None
    for d in (0, 1):
        prm, h_c, hin_c, h_x, hin_x, sav_c, sav_x = lru[d]
        lru_dx_x, dwr, dwi, lvec, cg0 = _lru_bwd(x5_x, rows, *prm, sav_x, h_x, hin_x, dhx_cm, zero_v, lru_dx_x, None,
                                                 d, f"lru_bwd_x{d}")
        lru_dx_c, dwr, dwi, lvec, _ = _lru_bwd(x5_c, c_len, *prm, sav_c, h_c, hin_c, None, cg0, lru_dx_c,
                                               (dwr, dwi, lvec), d, f"lru_bwd_c{d}")
        lru[d] = (dwr, dwi, lvec)
    dz5_x = _to_raster(lru_dx_x, rows).astype(BF16)
    dz5_c = lru_dx_c.astype(BF16)

    gq_c = gv_c = None
    dzf_c, dlb = {}, {}
    gq_x, dzf_x0, gv_x, dlb_x, ds0 = _gla_bwd(z_x, lb, gla[0][1], do, zero_s, None, None, 0, "gla_bwd_x0", f_dtype=BF16)
    gq_c, dzf_c[0], gv_c, dlb_c, _ = _gla_bwd(z_c, lb, gla[0][0], None, ds0, None, None, 0, "gla_bwd_c0")
    dlb[0] = dlb_x[0:1] + dlb_c[0:1]
    dz_g, dlb_x, ds0 = _gla_bwd(z_x, lb, gla[1][1], do, zero_s, gq_x, gv_x, 1, "gla_bwd_x1", into=(dz_m, dzf_x0))
    gq_c, dzf_c[1], gv_c, dlb_c, _ = _gla_bwd(z_c, lb, gla[1][0], None, ds0, gq_c, gv_c, 1, "gla_bwd_c1")
    dlb[1] = dlb_x[0:1] + dlb_c[0:1]

    bf = lambda t: t.astype(BF16)
    dz_x = lax.dynamic_update_slice(dz_g, dz5_x, (0, 5 * D))
    zc0 = jnp.zeros((c_len, D), BF16)
    dz_c = jnp.concatenate([bf(gq_c), bf(dzf_c[0]), bf(dzf_c[1]), bf(gv_c), zc0, dz5_c, zc0, zc0, zc0], axis=1)
    dwin_c, dbin_c = _mm_tn(u_c, dz_c, None, "dwin_c", with_colsum=True)
    dpa = _mm_tn(oa, dya, None, "dpa", out_dtype=BF16)
    dpb = _mm_tn(obb, dyb, None, "dpb", out_dtype=BF16)
    dwo = _mm_tn(yb16, dout, None, "dwout", out_dtype=BF16)

    wr_pack = jnp.concatenate([lru[0][0], lru[1][0], lru[0][1], lru[1][1]], axis=0).reshape(4 * NH * DH, DH)
    r_win, dbin, r_pa, r_pb, r_wo, r_wri = _dwin_exchange(
        u_x, dz_x, dwin_c, dbin_c, [dpa, dpb, dwo, wr_pack], [0, 0, 0, 0], "dwin_exchange")
    grad_x, xvec = _input_grad(dz_x, w_in_f, xs, dr, mod, 0, "input_grad_x")
    _, cvec = _input_grad(dz_c, w_in_f, cs, None, mod, 1, "input_grad_c")
    wri_piece = _sum_rows(r_wri, "sum_w_ri_piece")
    g_w_in, d_w_in, nm_w_in, nv_w_in = _sum_adamw(r_win, w_in, m_w_in, v_w_in, "update_w_in")
    g_p_a, d_p_a, nm_p_a, nv_p_a = _sum_adamw(r_pa, p_a, m_p_a, v_p_a, "update_p_a")
    g_p_b, d_p_b, nm_p_b, nv_p_b = _sum_adamw(r_pb, p_b, m_p_b, v_p_b, "update_p_b")
    g_w_out, d_w_out, nm_w_out, nv_w_out = _sum_adamw(r_wo, w_out, m_w_out, v_w_out, "update_w_out")

    dlb_rows = jnp.concatenate([dlb[0], dlb[1]], axis=0)
    pack = jnp.concatenate([
        xvec[0:1], xvec[1:2], mvec[0:1],
        cvec[0:1], cvec[1:2], jnp.zeros((1, D), F32),
        dbin.reshape(NGRP, D),
        mvec[3:4], mvec[1:2], mvec[2:3],
        lru[0][2][0:8], lru[1][2][0:3],
        lru[1][2][3:8],
        dlb_rows,
        mvec[4:5],
        jnp.zeros((3, D), F32)], axis=0)
    g_pack, g_wri = _all_gather([pack, wri_piece], "gather_small_grads")

    dmx = g_pack[:, 0:3, :].reshape(NDEV, 3 * D)
    dmc = g_pack[:, 3:6, :].reshape(NDEV, 3 * D)
    grad_w_mod = _wmod_grad(c_all.T, c_ctx.reshape(D, 1), _local_cols(dmx, me, mcols), _local_cols(dmc, me, mcols),
                            "grad_w_mod").reshape(1, D, mcols)
    small_params = [(c_ctx.reshape(1, D), m_c_ctx.reshape(1, D), v_c_ctx.reshape(1, D)), (b_mod, m_b_mod, v_b_mod),
                    (b_in, m_b_in, v_b_in), (lb_logits, m_lb_logits, v_lb_logits), (norm_a_g, m_norm_a_g, v_norm_a_g),
                    (conv_w, m_conv_w, v_conv_w), (conv_b, m_conv_b, v_conv_b), (b_r, m_b_r, v_b_r),
                    (b_i, m_b_i, v_b_i), (lam, m_lam, v_lam), (ln_g, m_ln_g, v_ln_g), (ln_b, m_ln_b, v_ln_b)]
    loss_tile, small_g, small_upd = _finalize_small(g_pack, lb, w_mod_f, small_params)
    loss = loss_tile[0, 0]
    (grad_c_ctx, grad_b_mod, grad_b_in, grad_lb_logits, grad_norm_a_g, grad_conv_w, grad_conv_b, grad_b_r, grad_b_i,
     grad_lam, grad_ln_g, grad_ln_b) = small_g
    small_upd[0] = tuple(t.reshape(c_ctx.shape) for t in small_upd[0])
    (o_c_ctx, o_b_mod, o_b_in, o_lb, o_norm, o_conv_w, o_conv_b, o_b_r, o_b_i, o_lam, o_ln_g, o_ln_b) = small_upd

    half = 2 * NH * DH
    g_ri = g_wri.reshape(2 * half, DH)
    grad_w_r, grad_w_i = g_ri[:half].reshape(w_r.shape), g_ri[half:].reshape(w_i.shape)
    d_w_r, nm_w_r, nv_w_r = _adamw(grad_w_r, w_r, m_w_r, v_w_r, "update_w_r")
    d_w_i, nm_w_i, nv_w_i = _adamw(grad_w_i, w_i, m_w_i, v_w_i, "update_w_i")

    d_w_mod, nm_w_mod, nv_w_mod = _adamw(grad_w_mod, w_mod, m_w_mod, v_w_mod, "update_w_mod")

    grads = [grad_c_ctx.reshape(c_ctx.shape), grad_w_mod, grad_b_mod, g_w_in, grad_b_in, grad_lb_logits, grad_norm_a_g,
             grad_conv_w, grad_conv_b, grad_w_r, grad_b_r, grad_w_i, grad_b_i, grad_lam, g_p_a, g_p_b, g_w_out,
             grad_ln_g, grad_ln_b]
    per_kind = []
    for k in range(3):
        per_kind.append([
            o_c_ctx[k], (d_w_mod, nm_w_mod, nv_w_mod)[k], o_b_mod[k], (d_w_in, nm_w_in, nv_w_in)[k], o_b_in[k], o_lb[k],
            o_norm[k], o_conv_w[k], o_conv_b[k], (d_w_r, nm_w_r, nv_w_r)[k], o_b_r[k], (d_w_i, nm_w_i, nv_w_i)[k],
            o_b_i[k], o_lam[k], (d_p_a, nm_p_a, nv_p_a)[k], (d_p_b, nm_p_b, nv_p_b)[k], (d_w_out, nm_w_out, nv_w_out)[k],
            o_ln_g[k], o_ln_b[k]])
    return (loss, grad_x.reshape(x.shape), *grads, *per_kind[0], *per_kind[1], *per_kind[2])
```

```python
import functools

import jax
import jax.numpy as jnp
from jax import lax
from jax.experimental import pallas as pl
from jax.experimental.pallas import tpu as pltpu

F32 = jnp.float32
BF16 = jnp.bfloat16

D = 1024
NH = 8
DH = 128
CHUNK = 64
GLA_HEADS_PER_STEP = 8
GRID_W = 64
NGRP = 9
NDEV = 8
RG_C = 8.0
ALPHA = 2.0 ** 0.25
LN_EPS = 1e-5
RMS_EPS = 1e-6
Q_SCALE = DH ** -0.5
ADAM_LR, ADAM_B1, ADAM_B2, ADAM_EPS, ADAM_WD, ADAM_STEP = 1e-3, 0.9, 0.999, 1e-8, 0.01, 10
ADAM_C1 = 1.0 / (1.0 - ADAM_B1 ** ADAM_STEP)
ADAM_C2 = 1.0 / (1.0 - ADAM_B2 ** ADAM_STEP)

ANY = pl.BlockSpec(memory_space=pl.ANY)


def _sigmoid(t):
    return 1.0 / (1.0 + jnp.exp(-t))


def _dsilu(t, s):
    return s * (1.0 + t * (1.0 - s))


def _dot(a, b):
    return jnp.dot(a, b, preferred_element_type=F32)


def _dot_nt(a, b):
    return lax.dot_general(a, b, (((1,), (1,)), ((), ())), preferred_element_type=F32)


def _dot_tn(a, b):
    return lax.dot_general(a, b, (((0,), (0,)), ((), ())), preferred_element_type=F32)


def _my_index():
    return 4 * lax.axis_index("x") + 2 * lax.axis_index("y") + lax.axis_index("c")


def _dev_tuple(j):
    return (j >> 2, (j >> 1) & 1, j & 1)


def _exchange_sems(n):
    return [pltpu.SemaphoreType.DMA((n * NDEV,)), pltpu.SemaphoreType.DMA((n * NDEV,)), pltpu.SemaphoreType.DMA((n,))]


def _exchange(ins, outs, sems, piece_of=None):
    send_sems, recv_sems, loc_sems = sems
    n = len(ins)

    def src(a, j):
        return ins[a] if piece_of is None else piece_of(ins[a], a, j)

    def remote(a, j, slot):
        return pltpu.make_async_remote_copy(
            src_ref=src(a, j), dst_ref=outs[a].at[slot],
            send_sem=send_sems.at[a * NDEV + j], recv_sem=recv_sems.at[a * NDEV + slot],
            device_id=_dev_tuple(j), device_id_type=pl.DeviceIdType.MESH)

    def local(a, j):
        return pltpu.make_async_copy(src(a, j), outs[a].at[j], loc_sems.at[a])

    def start():
        me = _my_index()
        for j in range(NDEV):
            @pl.when(me == j)
            def _():
                for a in range(n):
                    local(a, j).start()

            @pl.when(me != j)
            def _():
                for a in range(n):
                    remote(a, j, me).start()

    def finish():
        me = _my_index()
        for j in range(NDEV):
            @pl.when(me == j)
            def _():
                for a in range(n):
                    local(a, j).wait()

            @pl.when(me != j)
            def _():
                for a in range(n):
                    remote(a, j, j).wait()

    return start, finish


def _all_gather(shards, name):
    n = len(shards)

    def body(*refs):
        start, finish = _exchange(refs[:n], refs[n:2 * n], refs[2 * n:])
        start()
        finish()

    return pl.pallas_call(
        body, name=name,
        out_shape=[jax.ShapeDtypeStruct((NDEV,) + s.shape, s.dtype) for s in shards],
        in_specs=[ANY] * n, out_specs=[ANY] * n, scratch_shapes=_exchange_sems(n),
    )(*shards)


def _pieces(parts, splits):
    shapes = []
    for part, split in zip(parts, splits):
        r, c = part.shape
        shapes.append((r // NDEV, c) if split == 0 else (r, c // NDEV))

    def piece_of(ref, a, j):
        pr, pc = shapes[a]
        if splits[a] == 0:
            return ref.at[pl.ds(j * pr, pr), :]
        return ref.at[:, pl.ds(j * pc, pc)]

    return shapes, piece_of


_STEP_MASKS = ((2, 4, 6, 3, 5, 7, 1, 0), (4, 2, 6, 5, 3, 7, 1, 0))
_GATHER_MASKS = ((0, 1, 2, 4, 3, 5, 6, 7), (0, 1, 4, 2, 5, 3, 6, 7))


def _peer_schedule(table):
    tab = jnp.array(table, jnp.int32)
    return jnp.bitwise_xor(_my_index(), tab[lax.axis_index("c")])


def _step_peer(s, table=_STEP_MASKS):
    def pick(row):
        if isinstance(s, int):
            return jnp.int32(row[s])
        m = jnp.int32(row[NDEV - 1])
        for t in range(NDEV - 2, -1, -1):
            m = jnp.where(s == t, jnp.int32(row[t]), m)
        return m
    mask = jnp.where(lax.axis_index("c") == 0, pick(table[0]), pick(table[1]))
    return jnp.bitwise_xor(_my_index(), mask)


def _dev_of(p):
    return (p // 4, (p // 2) % 2, p % 2)


def _dwin_exchange(u, dz, init, cs_init, name):
    m, ka = u.shape
    n = dz.shape[1]
    pc = n // NDEV
    tk = _row_tile(m, 512)
    nk = m // tk

    def body(pidx_ref, u_ref, dz_ref, init_ref, csi_ref, rwin, cs_ref, acc, sbuf, wsend, wrecv, wloc):
        s, k = pl.program_id(0), pl.program_id(1)
        me = _my_index()

        def slab_copy(slot, p):
            return pltpu.make_async_remote_copy(
                src_ref=sbuf.at[slot], dst_ref=rwin.at[me], send_sem=wsend.at[slot], recv_sem=wrecv.at[me],
                device_id=_dev_of(p), device_id_type=pl.DeviceIdType.MESH)

        @pl.when(k == 0)
        def _():
            acc[...] = init_ref[...]
            cs_ref[...] = csi_ref[...]

        bv = dz_ref[...]
        acc[...] += _dot_tn(u_ref[...], bv)
        cs_ref[...] += jnp.sum(bv.astype(F32), axis=0, keepdims=True)

        @pl.when(k == nk - 1)
        def _():
            slot = s % 2

            @pl.when(s >= 2)
            def _():
                slab_copy(slot, me).wait_send()

            sbuf[slot] = acc[...].astype(BF16)

            @pl.when(s < NDEV - 1)
            def _():
                slab_copy(slot, _step_peer(s)).start()

            @pl.when(s == NDEV - 1)
            def _():
                own = pltpu.make_async_copy(sbuf.at[slot], rwin.at[me], wloc.at[0])
                own.start()
                slab_copy(1 - slot, me).wait_send()
                for j in range(NDEV):
                    @pl.when(me != j)
                    def _():
                        pltpu.make_async_remote_copy(
                            src_ref=sbuf.at[0], dst_ref=rwin.at[j], send_sem=wsend.at[0], recv_sem=wrecv.at[j],
                            device_id=_dev_tuple(j), device_id_type=pl.DeviceIdType.MESH).wait_recv()
                own.wait()

    grid_spec = pltpu.PrefetchScalarGridSpec(
        num_scalar_prefetch=1, grid=(NDEV, nk),
        in_specs=[pl.BlockSpec((tk, ka), lambda s, k, pidx: (k, 0)),
                  pl.BlockSpec((tk, pc), lambda s, k, pidx: (k, pidx[s])),
                  pl.BlockSpec((ka, pc), lambda s, k, pidx: (0, pidx[s])),
                  pl.BlockSpec((1, pc), lambda s, k, pidx: (0, pidx[s]))],
        out_specs=[ANY, pl.BlockSpec((1, pc), lambda s, k, pidx: (0, pidx[s]))],
        scratch_shapes=[pltpu.VMEM((ka, pc), F32), pltpu.VMEM((2, ka, pc), BF16),
                        pltpu.SemaphoreType.DMA((2,)), pltpu.SemaphoreType.DMA((NDEV,)), pltpu.SemaphoreType.DMA((1,))])
    return pl.pallas_call(
        body, name=name, grid_spec=grid_spec,
        out_shape=[jax.ShapeDtypeStruct((NDEV, ka, pc), BF16), jax.ShapeDtypeStruct((1, n), F32)],
    )(_peer_schedule(_STEP_MASKS), u, dz, init, cs_init)


def _inproj_gather(u, w_loc, bias, name):
    m, k = u.shape
    pc = w_loc.shape[1]
    n = pc * NDEV
    tm = _row_tile(m, 512)
    ni = m // tm

    def body(pidx_ref, u_ref, b_ref, wl_ref, z_ref, wall, wbuf, wsend, wrecv, ldsem, ownsem):
        s, i = pl.program_id(0), pl.program_id(1)
        me = _my_index()

        def shard_push(t):
            return pltpu.make_async_remote_copy(
                src_ref=wl_ref, dst_ref=wall.at[me], send_sem=wsend.at[t], recv_sem=wrecv.at[me],
                device_id=_dev_of(_step_peer(t, _GATHER_MASKS)), device_id_type=pl.DeviceIdType.MESH)

        def load(slot, src):
            return pltpu.make_async_copy(src, wbuf.at[slot], ldsem.at[slot])

        own = pltpu.make_async_copy(wl_ref, wall.at[me], ownsem.at[0])

        @pl.when((s == 0) & (i == 0))
        def _():
            own.start()
            load(0, wl_ref).start()
            for t in range(1, NDEV):
                shard_push(t).start()

        @pl.when((i == ni // 2) & (s < NDEV - 1))
        def _():
            nxt = _step_peer(s + 1, _GATHER_MASKS)
            pltpu.make_async_remote_copy(
                src_ref=wl_ref, dst_ref=wall.at[nxt], send_sem=wsend.at[0], recv_sem=wrecv.at[nxt],
                device_id=_dev_of(nxt), device_id_type=pl.DeviceIdType.MESH).wait_recv()
            load((s + 1) % 2, wall.at[nxt]).start()

        @pl.when(i == 0)
        def _():
            load(s % 2, wl_ref).wait()

        z_ref[...] = _dot(u_ref[...], wbuf[s % 2]) + b_ref[...]

        @pl.when((s == NDEV - 1) & (i == ni - 1))
        def _():
            own.wait()
            for t in range(1, NDEV):
                shard_push(t).wait_send()

    grid_spec = pltpu.PrefetchScalarGridSpec(
        num_scalar_prefetch=1, grid=(NDEV, ni),
        in_specs=[pl.BlockSpec((tm, k), lambda s, i, pidx: (i, 0)),
                  pl.BlockSpec((1, pc), lambda s, i, pidx: (0, pidx[s])), ANY],
        out_specs=[pl.BlockSpec((tm, pc), lambda s, i, pidx: (i, pidx[s])), ANY],
        scratch_shapes=[pltpu.VMEM((2, k, pc), BF16),
                        pltpu.SemaphoreType.DMA((NDEV,)), pltpu.SemaphoreType.DMA((NDEV,)),
                        pltpu.SemaphoreType.DMA((2,)), pltpu.SemaphoreType.DMA((1,))])
    return pl.pallas_call(
        body, name=name, grid_spec=grid_spec,
        out_shape=[jax.ShapeDtypeStruct((m, n), F32), jax.ShapeDtypeStruct((NDEV, k, pc), w_loc.dtype)],
    )(_peer_schedule(_GATHER_MASKS), u, bias, w_loc)


def _adam_math(g, w, m, v):
    m2 = ADAM_B1 * m + (1.0 - ADAM_B1) * g
    v2 = ADAM_B2 * v + (1.0 - ADAM_B2) * (g * g)
    delta = -ADAM_LR * ((m2 * ADAM_C1) / (jnp.sqrt(v2 * ADAM_C2) + ADAM_EPS) + ADAM_WD * w)
    return delta, m2, v2


def _row_tile(r, cap):
    t = min(r, cap)
    while r % t:
        t //= 2
    return t


def _adamw(g, w, m, v, name):
    shape = w.shape
    cols = shape[-1] if w.ndim >= 2 and shape[-1] % 128 == 0 else 128
    g2, w2, m2, v2 = (t.reshape(-1, cols) for t in (g, w, m, v))
    r = g2.shape[0]
    tr = _row_tile(r, 256)

    def body(g_ref, w_ref, m_ref, v_ref, d_ref, mo_ref, vo_ref):
        d, mm, vv = _adam_math(g_ref[...], w_ref[...], m_ref[...], v_ref[...])
        d_ref[...] = d
        mo_ref[...] = mm
        vo_ref[...] = vv

    spec = pl.BlockSpec((tr, cols), lambda i: (i, 0))
    outs = pl.pallas_call(
        body, name=name, grid=(r // tr,),
        out_shape=[jax.ShapeDtypeStruct((r, cols), F32)] * 3,
        in_specs=[spec] * 4, out_specs=[spec] * 3,
    )(g2, w2, m2, v2)
    return tuple(o.reshape(shape) for o in outs)


def _sum_adamw(parts, w, m, v, name):
    _, r, c = parts.shape
    shape = w.shape
    w2, m2, v2 = (t.reshape(r, c) for t in (w, m, v))
    tr = _row_tile(r, 128)

    def body(p_ref, w_ref, m_ref, v_ref, g_ref, d_ref, mo_ref, vo_ref):
        g = p_ref[0].astype(F32)
        for k in range(1, NDEV):
            g = g + p_ref[k].astype(F32)
        d, mm, vv = _adam_math(g, w_ref[...], m_ref[...], v_ref[...])
        g_ref[...] = g
        d_ref[...] = d
        mo_ref[...] = mm
        vo_ref[...] = vv

    spec = pl.BlockSpec((tr, c), lambda i: (i, 0))
    outs = pl.pallas_call(
        body, name=name, grid=(r // tr,),
        out_shape=[jax.ShapeDtypeStruct((r, c), F32)] * 4,
        in_specs=[pl.BlockSpec((NDEV, tr, c), lambda i: (0, i, 0))] + [spec] * 3, out_specs=[spec] * 4,
    )(parts, w2, m2, v2)
    return tuple(o.reshape(shape) for o in outs)


def _sum_rows(parts, name):
    _, r, c = parts.shape

    def body(p_ref, o_ref):
        g = p_ref[0]
        for k in range(1, NDEV):
            g = g + p_ref[k]
        o_ref[...] = g

    return pl.pallas_call(
        body, name=name, out_shape=jax.ShapeDtypeStruct((r, c), F32),
    )(parts)


def _prep(cc, w_mod_full, b_mod, lbl):
    def body(cc_ref, w_ref, b_ref, l_ref, mod_ref, lb_ref):
        t = cc_ref[...]
        s = (t * _sigmoid(t)).astype(BF16)
        mod_ref[...] = _dot(s, w_ref[...]) + b_ref[...]
        lb_ref[...] = _sigmoid(l_ref[0] - l_ref[1])

    return pl.pallas_call(
        body, name="prep",
        out_shape=[jax.ShapeDtypeStruct((8, 3 * D), F32), jax.ShapeDtypeStruct((8, D), F32)],
    )(cc, w_mod_full, b_mod, lbl)


def _modulate(xin, mod, row, name):
    n = xin.shape[0]
    tm = _row_tile(n, 512)

    def body(x_ref, mod_ref, u_ref):
        sh = mod_ref[row:row + 1, 0:D]
        sc = mod_ref[row:row + 1, D:2 * D]
        u_ref[...] = (x_ref[...] * (1.0 + sc) + sh).astype(BF16)

    return pl.pallas_call(
        body, name=name, grid=(n // tm,),
        out_shape=jax.ShapeDtypeStruct((n, D), BF16),
        in_specs=[pl.BlockSpec((tm, D), lambda i: (i, 0)), pl.BlockSpec((8, 3 * D), lambda i: (0, 0))],
        out_specs=pl.BlockSpec((tm, D), lambda i: (i, 0)),
    )(xin, mod)


def _mm_bias(a, w_all, bias, name):
    m, k = a.shape
    tn = w_all.shape[2]
    n = tn * NDEV
    tm = _row_tile(m, 512)

    def body(a_ref, b_ref, bias_ref, o_ref):
        o_ref[...] = _dot(a_ref[...], b_ref[0]) + bias_ref[...]

    return pl.pallas_call(
        body, name=name, grid=(NDEV, m // tm),
        out_shape=jax.ShapeDtypeStruct((m, n), F32),
        in_specs=[pl.BlockSpec((tm, k), lambda j, i: (i, 0)), pl.BlockSpec((1, k, tn), lambda j, i: (j, 0, 0)),
                  pl.BlockSpec((1, tn), lambda j, i: (0, j))],
        out_specs=pl.BlockSpec((tm, tn), lambda j, i: (i, j)),
    )(a, w_all, bias)


def _mm_tn(a, b, init, name, with_colsum=False, colsum_init=None, out_dtype=F32):
    m, ka = a.shape
    n = b.shape[1]
    tk = _row_tile(m, 512)
    tn = 1024
    nk = m // tk
    has_init = init is not None

    def body(*refs):
        a_ref, b_ref = refs[0], refs[1]
        pos = 2
        init_ref = cs_init_ref = None
        if has_init:
            init_ref = refs[pos]
            pos += 1
            if with_colsum:
                cs_init_ref = refs[pos]
                pos += 1
        o_ref = refs[pos]
        cs_ref = refs[pos + 1] if with_colsum else None
        acc = refs[-1]
        k = pl.program_id(1)

        @pl.when(k == 0)
        def _():
            if has_init:
                acc[...] = init_ref[...]
                if with_colsum:
                    cs_ref[...] = cs_init_ref[...]
            else:
                acc[...] = jnp.zeros_like(acc)
                if with_colsum:
                    cs_ref[...] = jnp.zeros_like(cs_ref)

        bv = b_ref[...]
        acc[...] += _dot_tn(a_ref[...], bv)
        if with_colsum:
            cs_ref[...] += jnp.sum(bv.astype(F32), axis=0, keepdims=True)

        @pl.when(k == nk - 1)
        def _():
            o_ref[...] = acc[...].astype(out_dtype)

    in_specs = [pl.BlockSpec((tk, ka), lambda j, k: (k, 0)), pl.BlockSpec((tk, tn), lambda j, k: (k, j))]
    args = [a, b]
    if has_init:
        in_specs.append(pl.BlockSpec((ka, tn), lambda j, k: (0, j)))
        args.append(init)
        if with_colsum:
            in_specs.append(pl.BlockSpec((1, tn), lambda j, k: (0, j)))
            args.append(colsum_init)
    out_shape = [jax.ShapeDtypeStruct((ka, n), out_dtype)]
    out_specs = [pl.BlockSpec((ka, tn), lambda j, k: (0, j))]
    if with_colsum:
        out_shape.append(jax.ShapeDtypeStruct((1, n), F32))
        out_specs.append(pl.BlockSpec((1, tn), lambda j, k: (0, j)))
    outs = pl.pallas_call(
        body, name=name, grid=(n // tn, nk), out_shape=out_shape, in_specs=in_specs, out_specs=out_specs,
        scratch_shapes=[pltpu.VMEM((ka, tn), F32)],
    )(*args)
    return outs if with_colsum else outs[0]


def _input_grad(dz, w_all, xin, dr, mod, row, name, side=(), side_splits=()):
    m, n = dz.shape
    tm = _row_tile(m, 512)
    tk = w_all.shape[2]
    nk = NDEV
    ni = m // tm
    has_dr = dr is not None
    ns = len(side)
    piece_shapes, piece_of = _pieces(side, side_splits)

    def body(*refs):
        dz_ref, w_ref, x_ref = refs[:3]
        pos = 3
        dr_ref = refs[pos] if has_dr else None
        pos += int(has_dr)
        mod_ref = refs[pos]
        side_in = refs[pos + 1:pos + 1 + ns]
        pos += 1 + ns
        gx_ref = refs[pos] if has_dr else None
        pos += int(has_dr)
        vec_ref = refs[pos]
        side_out = refs[pos + 1:pos + 1 + ns]
        acc = refs[pos + 1 + ns]
        i, k = pl.program_id(0), pl.program_id(1)
        if ns:
            side_start, side_finish = _exchange(side_in, side_out, refs[pos + 2 + ns:], piece_of)

            @pl.when((i == 0) & (k == 0))
            def _():
                side_start()

        @pl.when(k == 0)
        def _():
            acc[...] = jnp.zeros_like(acc)

        @pl.when((i == 0) & (k == 0))
        def _():
            vec_ref[...] = jnp.zeros_like(vec_ref)

        acc[...] += _dot_nt(dz_ref[...], w_ref[0])

        @pl.when(k == nk - 1)
        def _():
            du = acc[...]
            xv = x_ref[...]
            if has_dr:
                sc = mod_ref[row:row + 1, D:2 * D]
                gx_ref[...] = ALPHA * dr_ref[...] + du * (1.0 + sc)
            vec_ref[0:1, :] += jnp.sum(du, axis=0, keepdims=True)
            vec_ref[1:2, :] += jnp.sum(du * xv, axis=0, keepdims=True)

        if ns:
            @pl.when((i == ni - 1) & (k == nk - 1))
            def _():
                side_finish()

    row_spec = pl.BlockSpec((tm, D), lambda i, k: (i, 0))
    in_specs = [pl.BlockSpec((tm, tk), lambda i, k: (i, k)), pl.BlockSpec((1, D, tk), lambda i, k: (k, 0, 0)), row_spec]
    args = [dz, w_all, xin]
    if has_dr:
        in_specs.append(row_spec)
        args.append(dr)
    in_specs.append(pl.BlockSpec((8, 3 * D), lambda i, k: (0, 0)))
    args.append(mod)
    in_specs += [ANY] * ns
    args += list(side)
    out_shape, out_specs = [], []
    if has_dr:
        out_shape.append(jax.ShapeDtypeStruct((m, D), F32))
        out_specs.append(row_spec)
    out_shape.append(jax.ShapeDtypeStruct((8, D), F32))
    out_specs.append(pl.BlockSpec((8, D), lambda i, k: (0, 0)))
    out_shape += [jax.ShapeDtypeStruct((NDEV,) + piece_shapes[a], side[a].dtype) for a in range(ns)]
    out_specs += [ANY] * ns
    outs = pl.pallas_call(
        body, name=name, grid=(ni, nk), out_shape=out_shape, in_specs=in_specs, out_specs=out_specs,
        scratch_shapes=[pltpu.VMEM((tm, D), F32)] + (_exchange_sems(ns) if ns else []),
    )(*args)
    return tuple(outs) if has_dr else (None, *outs)


def _tri(reverse):
    r = lax.broadcasted_iota(jnp.int32, (CHUNK, CHUNK), 0)
    c = lax.broadcasted_iota(jnp.int32, (CHUNK, CHUNK), 1)
    return (c >= r) if reverse else (c <= r)


def _cum_f32(tri_b, t):
    hi = t.astype(BF16)
    r1 = t - hi.astype(F32)
    mid = r1.astype(BF16)
    lo = (r1 - mid.astype(F32)).astype(BF16)
    return _dot(tri_b, hi) + _dot(tri_b, mid) + _dot(tri_b, lo)


def _gla_features(zq, zf, lb):
    sq = _sigmoid(zq)
    q = zq * sq * Q_SCALE
    sf = _sigmoid(zf)
    f = lb + (1.0 - lb) * sf
    return q, sq, f, sf


def _gla_decays(f, tri_b, last):
    lf = jnp.log(f)
    g = _cum_f32(tri_b, lf)
    gl = g[last:last + 1, :]
    return g, gl


def _gla_block(n):
    return 256 if n % 256 == 0 else CHUNK


def _gla_fwd(z, lb, s0, d, name):
    n = z.shape[0]
    blk = _gla_block(n)
    nb, npb = n // blk, blk // CHUNK
    reverse = d == 1
    last = 0 if reverse else CHUNK - 1
    order = list(range(npb))[::-1] if reverse else list(range(npb))

    def bmap(i):
        return nb - 1 - i if reverse else i

    hp = GLA_HEADS_PER_STEP
    hw = hp * DH
    units = [(hh, cidx) for hh in range(hp) for cidx in order]

    def body(zq_ref, zf_ref, zv_ref, lb_ref, s0_ref, o_ref, ss_ref, sf_ref, st):
        i = pl.program_id(1)

        @pl.when(i == 0)
        def _():
            st[...] = s0_ref[...]

        mask = _tri(reverse)
        tri_b = jnp.where(mask, 1.0, 0.0).astype(BF16)
        feat = {}
        for u in units:
            hh, cidx = u
            rows, cols = pl.ds(cidx * CHUNK, CHUNK), pl.ds(hh * DH, DH)
            q, _, f, _ = _gla_features(zq_ref[rows, cols], zf_ref[rows, cols], lb_ref[d:d + 1, cols])
            feat[u] = (q, 1.0 - f, jnp.log(f), zv_ref[rows, cols].astype(BF16))
        dec = {u: _cum_f32(tri_b, feat[u][2]) for u in units}
        ops = {}
        for u in units:
            q, k, _, vb = feat[u]
            g = dec[u]
            gl = g[last:last + 1, :]
            ops[u] = ((q * jnp.exp(g)).astype(BF16), (k * jnp.exp(-g)).astype(BF16),
                      (k * jnp.exp(gl - g)).astype(BF16), jnp.exp(gl), vb)
        att = {u: jnp.where(mask, _dot_nt(ops[u][0], ops[u][1]), 0.0).astype(BF16) for u in units}
        upd = {u: _dot_tn(ops[u][4], ops[u][2]) for u in units}
        intra = {u: _dot(att[u], ops[u][4]) for u in units}
        s_in = {}
        for hh in range(hp):
            s = st[hh]
            for cidx in order:
                s_in[(hh, cidx)] = s
                s = s * ops[(hh, cidx)][3] + upd[(hh, cidx)]
            st[hh] = s
            sf_ref[hh] = s
        for u in units:
            hh, cidx = u
            rows, cols = pl.ds(cidx * CHUNK, CHUNK), pl.ds(hh * DH, DH)
            o_ref[rows, cols] = intra[u] + _dot_nt(ops[u][0], s_in[u].astype(BF16))
            ss_ref[hh, cidx] = s_in[u]

    def col(g):
        return lambda h, i: (bmap(i), g * (NH // hp) + h)

    return pl.pallas_call(
        body, name=name, grid=(NH // hp, nb),
        out_shape=[jax.ShapeDtypeStruct((n, D), F32), jax.ShapeDtypeStruct((NH, n // CHUNK, DH, DH), F32),
                   jax.ShapeDtypeStruct((NH, DH, DH), F32)],
        in_specs=[pl.BlockSpec((blk, hw), col(0)), pl.BlockSpec((blk, hw), col(1 + d)),
                  pl.BlockSpec((blk, hw), col(3)), pl.BlockSpec((8, hw), lambda h, i: (0, h)),
                  pl.BlockSpec((hp, DH, DH), lambda h, i: (h, 0, 0))],
        out_specs=[pl.BlockSpec((blk, hw), lambda h, i: (bmap(i), h)),
                   pl.BlockSpec((hp, npb, DH, DH), lambda h, i: (h, bmap(i), 0, 0)),
                   pl.BlockSpec((hp, DH, DH), lambda h, i: (h, 0, 0))],
        scratch_shapes=[pltpu.VMEM((hp, DH, DH), F32)],
    )(z, z, z, lb, s0)


def _gla_bwd(z, lb, s_start, do, ds_fin, acc_q, acc_v, d, name, f_dtype=F32, into=None):
    n = z.shape[0]
    blk = _gla_block(n)
    nb, npb = n // blk, blk // CHUNK
    reverse = d == 1
    last = 0 if reverse else CHUNK - 1
    order = list(range(npb)) if reverse else list(range(npb))[::-1]
    has_do = do is not None
    has_acc = acc_q is not None
    fused = into is not None
    assert not fused or d == 1
    hp = NH if fused else GLA_HEADS_PER_STEP
    hw = hp * DH
    units = [(hh, cidx) for hh in range(hp) for cidx in order]

    def bmap(i):
        return i if reverse else nb - 1 - i

    def body(*refs):
        zq_ref, zf_ref, zv_ref, lb_ref, ss_ref, dsf_ref = refs[:6]
        pos = 6
        do_ref = aq_ref = av_ref = None
        if has_do:
            do_ref = refs[pos]
            pos += 1
        if has_acc:
            aq_ref, av_ref = refs[pos], refs[pos + 1]
            pos += 2
        if fused:
            other_ref = refs[pos + 1]
            dz_ref, dlb_ref, ds0_ref, dst = refs[pos + 2:]
            dz_ref[:, D:2 * D] = other_ref[...]
        else:
            dzq_ref, dzf_ref, dzv_ref, dlb_ref, ds0_ref, dst = refs[pos:]
        i = pl.program_id(1)

        @pl.when(i == 0)
        def _():
            dst[...] = dsf_ref[...]
            dlb_ref[...] = jnp.zeros_like(dlb_ref)

        mask = _tri(reverse)
        tri_b = jnp.where(mask, 1.0, 0.0).astype(BF16)
        tri_t = jnp.where(_tri(not reverse), 1.0, 0.0).astype(BF16)

        def where(u):
            return pl.ds(u[1] * CHUNK, CHUNK), pl.ds(u[0] * DH, DH)

        feat = {}
        for u in units:
            rows, cols = where(u)
            zq, zf = zq_ref[rows, cols], zf_ref[rows, cols]
            lbv = lb_ref[d:d + 1, cols]
            q, sq, f, sf = _gla_features(zq, zf, lbv)
            feat[u] = dict(zq=zq, q=q, sq=sq, f=f, sf=sf, lbv=lbv, k=1.0 - f, vb=zv_ref[rows, cols].astype(BF16))
        dec = {u: _cum_f32(tri_b, jnp.log(feat[u]["f"])) for u in units}
        for u in units:
            w = feat[u]
            g = dec[u]
            gl = g[last:last + 1, :]
            w["eg"], w["egi"], w["ege"], w["egl"] = jnp.exp(g), jnp.exp(-g), jnp.exp(gl - g), jnp.exp(gl)
            w["qd"], w["ki"], w["ke"] = w["q"] * w["eg"], w["k"] * w["egi"], w["k"] * w["ege"]
            w["qdb"], w["kib"], w["keb"] = w["qd"].astype(BF16), w["ki"].astype(BF16), w["ke"].astype(BF16)
            w["s_in"] = ss_ref[u[0], u[1]]
        if has_do:
            for u in units:
                w = feat[u]
                rows, cols = where(u)
                w["dob"] = do_ref[rows, cols].astype(BF16)
            for u in units:
                w = feat[u]
                w["a"] = jnp.where(mask, _dot_nt(w["qdb"], w["kib"]), 0.0).astype(BF16)
                w["da"] = jnp.where(mask, _dot_nt(w["dob"], w["vb"]), 0.0).astype(BF16)
                w["m"] = _dot_tn(w["dob"], w["qdb"])
        for hh in range(hp):
            ds = dst[hh]
            for cidx in order:
                w = feat[(hh, cidx)]
                w["ds"] = ds
                ds = ds * w["egl"]
                if has_do:
                    ds = ds + w["m"]
            dst[hh] = ds
            ds0_ref[hh] = ds
        for u in units:
            w = feat[u]
            dsb = w["ds"].astype(BF16)
            w["dke"] = _dot(w["vb"], dsb)
            w["dv"] = _dot_nt(w["keb"], dsb)
            if has_do:
                w["dv"] = w["dv"] + _dot_tn(w["a"], w["dob"])
                w["dqd"] = _dot(w["da"], w["kib"]) + _dot(w["dob"], w["s_in"].astype(BF16))
                w["dki"] = _dot_tn(w["da"], w["qdb"])
        for u in units:
            w = feat[u]
            dkeke = w["dke"] * w["ke"]
            w["dgl"] = (w["egl"] * jnp.sum(w["s_in"] * w["ds"], axis=0, keepdims=True)
                        + jnp.sum(dkeke, axis=0, keepdims=True))
            dg = -dkeke
            dk = w["dke"] * w["ege"]
            if has_do:
                dg = dg + w["dqd"] * w["qd"] - w["dki"] * w["ki"]
                dk = dk + w["dki"] * w["egi"]
            w["dg"], w["dk"] = dg, dk
        dlf = {u: _cum_f32(tri_t, feat[u]["dg"]) for u in units}
        for u in units:
            w = feat[u]
            rows, cols = where(u)
            df = (dlf[u] + w["dgl"]) / w["f"] - w["dk"]
            sf = w["sf"]
            dzf = df * (1.0 - w["lbv"]) * sf * (1.0 - sf)
            dlb_ref[0:1, cols] += jnp.sum(df * (1.0 - sf), axis=0, keepdims=True)
            if has_do:
                dzq = w["dqd"] * w["eg"] * (Q_SCALE * _dsilu(w["zq"], w["sq"]))
            else:
                dzq = jnp.zeros((CHUNK, DH), F32)
            dv = w["dv"]
            if has_acc:
                dzq = dzq + aq_ref[rows, cols]
                dv = dv + av_ref[rows, cols]
            if fused:
                lane = u[0] * DH
                dz_ref[rows, pl.ds(lane, DH)] = dzq.astype(BF16)
                dz_ref[rows, pl.ds(2 * D + lane, DH)] = dzf.astype(BF16)
                dz_ref[rows, pl.ds(3 * D + lane, DH)] = dv.astype(BF16)
            else:
                dzq_ref[rows, cols] = dzq
                dzf_ref[rows, cols] = dzf.astype(f_dtype)
                dzv_ref[rows, cols] = dv

    def col(g):
        return lambda h, i: (bmap(i), g * (NH // hp) + h)

    tok = pl.BlockSpec((blk, hw), lambda h, i: (bmap(i), h))
    state = pl.BlockSpec((hp, DH, DH), lambda h, i: (h, 0, 0))
    in_specs = [pl.BlockSpec((blk, hw), col(0)), pl.BlockSpec((blk, hw), col(1 + d)), pl.BlockSpec((blk, hw), col(3)),
                pl.BlockSpec((8, hw), lambda h, i: (0, h)),
                pl.BlockSpec((hp, npb, DH, DH), lambda h, i: (h, bmap(i), 0, 0)), state]
    args = [z, z, z, lb, s_start, ds_fin]
    if has_do:
        in_specs.append(tok)
        args.append(do)
    if has_acc:
        in_specs += [tok, tok]
        args += [acc_q, acc_v]
    tail_shape = [jax.ShapeDtypeStruct((8, D), F32), jax.ShapeDtypeStruct((NH, DH, DH), F32)]
    tail_specs = [pl.BlockSpec((8, hw), lambda h, i: (0, h)), state]
    if fused:
        buf, other = into
        aliases = {len(args): 0}
        in_specs += [ANY, tok]
        args += [buf, other]
        out_shape = [jax.ShapeDtypeStruct(buf.shape, buf.dtype)] + tail_shape
        out_specs = [pl.BlockSpec((blk, 4 * D), lambda h, i: (bmap(i), 0))] + tail_specs
    else:
        aliases = {}
        out_shape = [jax.ShapeDtypeStruct((n, D), F32), jax.ShapeDtypeStruct((n, D), f_dtype),
                     jax.ShapeDtypeStruct((n, D), F32)] + tail_shape
        out_specs = [tok, tok, tok] + tail_specs
    return pl.pallas_call(
        body, name=name, grid=(NH // hp, nb), out_shape=out_shape, in_specs=in_specs, out_specs=out_specs,
        input_output_aliases=aliases, scratch_shapes=[pltpu.VMEM((hp, DH, DH), F32)],
    )(*args)


def _shift(t, s, fill, down):
    n = t.shape[0]
    rows = lax.broadcasted_iota(jnp.int32, t.shape, 0)
    if down:
        return jnp.where(rows >= s, pltpu.roll(t, s, 0), fill)
    return jnp.where(rows < n - s, pltpu.roll(t, n - s, 0), fill)


SUBLANES = 8
LRU_SAVED = 5


def _chain_scan(a, b, h_in, down):
    n = a.shape[0]
    ng = n // SUBLANES
    rows = lax.broadcasted_iota(jnp.int32, (SUBLANES, a.shape[1]), 0)
    local = []
    for g in range(ng):
        aa, bb = a[g * SUBLANES:(g + 1) * SUBLANES], b[g * SUBLANES:(g + 1) * SUBLANES]
        for s in (1, 2, 4):
            if down:
                keep, amt = rows >= s, s
            else:
                keep, amt = rows < SUBLANES - s, SUBLANES - s
            bb = bb + aa * jnp.where(keep, pltpu.roll(bb, amt, 0), 0.0)
            aa = aa * jnp.where(keep, pltpu.roll(aa, amt, 0), 1.0)
        local.append((aa, bb))
    out = [None] * ng
    carry = h_in
    for g in (range(ng) if down else range(ng - 1, -1, -1)):
        aa, bb = local[g]
        hg = bb + aa * carry
        out[g] = hg
        carry = hg[SUBLANES - 1:SUBLANES] if down else hg[0:1]
    return (jnp.concatenate(out, axis=0) if ng > 1 else out[0]), carry


def _conv_taps(xv):
    return (_shift(xv, 1, 0.0, True), xv, _shift(xv, 1, 0.0, False), _shift(xv, 2, 0.0, False))


def _conv(taps, cw, cb):
    return cb + cw[0:1, :] * taps[0] + cw[1:2, :] * taps[1] + cw[2:3, :] * taps[2] + cw[3:4, :] * taps[3]


def _neg_expm1(t):
    series = -t * (1.0 + t * (0.5 + t * (1.0 / 6.0 + t * (1.0 / 24.0 + t * (1.0 / 120.0)))))
    return jnp.where(t > -0.1, series, 1.0 - jnp.exp(t))


def _lru_gates(xc, wr, br, wi, bi, lam):
    xcb = xc.astype(BF16)
    r = _sigmoid(_dot(xcb, wr) + br)
    gi = _sigmoid(_dot(xcb, wi) + bi)
    sp = jnp.maximum(-lam, 0.0) + jnp.log(1.0 + jnp.exp(-jnp.abs(lam)))
    la = -RG_C * r * sp
    a = jnp.exp(la)
    mult = jnp.sqrt(_neg_expm1(2.0 * la))
    return xcb, r, gi, sp, a, mult


def _lru_fwd(xin, blk, cw, cb, wr, br, wi, bi, lam, h0, acc_h, d, name, side=()):
    n = xin.shape[0]
    nb = n // blk
    reverse = d == 1
    down = not reverse
    has_acc = acc_h is not None
    ns = len(side)

    def bmap(i):
        return nb - 1 - i if reverse else i

    def body(*refs):
        x_ref, cw_ref, cb_ref, wr_ref, br_ref, wi_ref, bi_ref, lam_ref, h0_ref = refs[:9]
        pos = 9
        acc_ref = refs[pos] if has_acc else None
        pos += int(has_acc)
        side_in = refs[pos:pos + ns]
        pos += ns
        h_ref, hin_ref, hfin_ref, sav_ref = refs[pos:pos + 4]
        pos += 4
        hsum_ref = refs[pos] if has_acc else None
        pos += int(has_acc)
        side_out = refs[pos:pos + ns]
        carry = refs[pos + ns]
        i = pl.program_id(0)
        if ns:
            side_start, side_finish = _exchange(side_in, side_out, refs[pos + ns + 1:])

            @pl.when(i == 0)
            def _():
                side_start()

        @pl.when(i == 0)
        def _():
            carry[...] = h0_ref[...]

        for g in range(NH):
            cols = pl.ds(g * DH, DH)
            xc = _conv(_conv_taps(x_ref[:, cols]), cw_ref[:, cols], cb_ref[:, cols])
            _, r, gi, _, a, mult = _lru_gates(xc, wr_ref[g], br_ref[:, cols], wi_ref[g], bi_ref[:, cols],
                                              lam_ref[:, cols])
            for slot, val in enumerate((xc, r, gi, a, mult)):
                sav_ref[slot, :, cols] = val
            hin = carry[:, cols]
            h, h_last = _chain_scan(a, mult * gi * xc, hin, down)
            h_ref[:, cols] = h
            if has_acc:
                hsum_ref[:, cols] = h + acc_ref[:, cols]
            hin_ref[0, :, cols] = hin
            carry[:, cols] = h_last
            hfin_ref[:, cols] = h_last

        if ns:
            @pl.when(i == nb - 1)
            def _():
                side_finish()

    vec = pl.BlockSpec((1, D), lambda i: (0, 0))
    wsp = pl.BlockSpec((NH, DH, DH), lambda i: (0, 0, 0))
    tok = pl.BlockSpec((blk, D), lambda i: (bmap(i), 0))
    in_specs = [tok, pl.BlockSpec((4, D), lambda i: (0, 0)), vec, wsp, vec, wsp, vec, vec, vec]
    args = [xin, cw, cb, wr, br, wi, bi, lam, h0]
    out_shape = [jax.ShapeDtypeStruct((n, D), F32), jax.ShapeDtypeStruct((nb, 1, D), F32),
                 jax.ShapeDtypeStruct((1, D), F32), jax.ShapeDtypeStruct((LRU_SAVED, n, D), F32)]
    out_specs = [tok, pl.BlockSpec((1, 1, D), lambda i: (bmap(i), 0, 0)), vec,
                 pl.BlockSpec((LRU_SAVED, blk, D), lambda i: (0, bmap(i), 0))]
    if has_acc:
        in_specs.append(tok)
        args.append(acc_h)
        out_shape.append(jax.ShapeDtypeStruct((n, D), F32))
        out_specs.append(tok)
    in_specs += [ANY] * ns
    args += list(side)
    out_shape += [jax.ShapeDtypeStruct((NDEV,) + t.shape, t.dtype) for t in side]
    out_specs += [ANY] * ns
    return pl.pallas_call(
        body, name=name, grid=(nb,), out_shape=out_shape, in_specs=in_specs, out_specs=out_specs,
        scratch_shapes=[pltpu.VMEM((1, D), F32)] + (_exchange_sems(ns) if ns else []),
    )(*args)


def _lru_bwd(xin, blk, cw, wr, wi, lam, sav, h, hin, dh, cg_fin, acc_dx, init, d, name):
    n = xin.shape[0]
    nb = n // blk
    reverse = d == 1
    down = not reverse
    first = blk - 1 if reverse else 0
    has_dh = dh is not None
    has_acc = acc_dx is not None
    has_init = init is not None

    def bmap(i):
        return i if reverse else nb - 1 - i

    def body(*refs):
        (x_ref, cw_ref, wr_ref, wi_ref, lam_ref, sav_ref, h_ref, hin_ref, cgf_ref) = refs[:9]
        pos = 9
        dh_ref = acc_ref = None
        iwr_ref = iwi_ref = ivec_ref = None
        if has_dh:
            dh_ref = refs[pos]
            pos += 1
        if has_acc:
            acc_ref = refs[pos]
            pos += 1
        if has_init:
            iwr_ref, iwi_ref, ivec_ref = refs[pos:pos + 3]
            pos += 3
        dx_ref, dwr_ref, dwi_ref, vec_ref, cg0_ref, carry = refs[pos:]
        i = pl.program_id(0)

        @pl.when(i == 0)
        def _():
            carry[...] = cgf_ref[...]
            if has_init:
                dwr_ref[...] = iwr_ref[...]
                dwi_ref[...] = iwi_ref[...]
                vec_ref[...] = ivec_ref[...]
            else:
                dwr_ref[...] = jnp.zeros_like(dwr_ref)
                dwi_ref[...] = jnp.zeros_like(dwi_ref)
                vec_ref[...] = jnp.zeros_like(vec_ref)

        for g in range(NH):
            cols = pl.ds(g * DH, DH)
            cwv = cw_ref[:, cols]
            lam_v = lam_ref[:, cols]
            taps = _conv_taps(x_ref[:, cols])
            wr_g, wi_g = wr_ref[g], wi_ref[g]
            xc, r, gi, a, mult = (sav_ref[slot, :, cols] for slot in range(LRU_SAVED))
            xcb = xc.astype(BF16)
            sp = jnp.maximum(-lam_v, 0.0) + jnp.log(1.0 + jnp.exp(-jnp.abs(lam_v)))
            hprev = _shift(h_ref[:, cols], 1, hin_ref[0, :, cols], down)
            a_next = _shift(a, 1, 1.0, not down)
            dhv = dh_ref[:, cols] if has_dh else jnp.zeros_like(a)
            e, _ = _chain_scan(a_next, dhv, carry[:, cols], not down)
            cg = a[first:first + 1, :] * e[first:first + 1, :]
            carry[:, cols] = cg
            cg0_ref[:, cols] = cg
            da = e * hprev
            emult = e * mult
            dgi = emult * xc
            dxc = emult * gi
            dla = da * a - (e * gi * xc) * (a * a) / mult
            dr = dla * (-RG_C * sp)
            sneg = 1.0 - _sigmoid(lam_v)
            dpr = dr * r * (1.0 - r)
            dpi = dgi * gi * (1.0 - gi)
            dprb, dpib = dpr.astype(BF16), dpi.astype(BF16)
            dxc = dxc + _dot_nt(dprb, wr_g) + _dot_nt(dpib, wi_g)
            dwr_ref[g] += _dot_tn(xcb, dprb)
            dwi_ref[g] += _dot_tn(xcb, dpib)
            dx = (cwv[0:1, :] * _shift(dxc, 1, 0.0, False) + cwv[1:2, :] * dxc
                  + cwv[2:3, :] * _shift(dxc, 1, 0.0, True) + cwv[3:4, :] * _shift(dxc, 2, 0.0, True))
            if has_acc:
                dx = dx + acc_ref[:, cols]
            dx_ref[:, cols] = dx
            vec_ref[0:1, cols] += jnp.sum(dpr, axis=0, keepdims=True)
            vec_ref[1:2, cols] += jnp.sum(dpi, axis=0, keepdims=True)
            vec_ref[2:3, cols] += jnp.sum(dla * r, axis=0, keepdims=True) * (RG_C * sneg)
            vec_ref[3:4, cols] += jnp.sum(dxc, axis=0, keepdims=True)
            for kk in range(4):
                vec_ref[4 + kk:5 + kk, cols] += jnp.sum(dxc * taps[kk], axis=0, keepdims=True)

    vec = pl.BlockSpec((1, D), lambda i: (0, 0))
    wsp = pl.BlockSpec((NH, DH, DH), lambda i: (0, 0, 0))
    tok = pl.BlockSpec((blk, D), lambda i: (bmap(i), 0))
    vec16 = pl.BlockSpec((16, D), lambda i: (0, 0))
    in_specs = [tok, pl.BlockSpec((4, D), lambda i: (0, 0)), wsp, wsp, vec,
                pl.BlockSpec((LRU_SAVED, blk, D), lambda i: (0, bmap(i), 0)), tok,
                pl.BlockSpec((1, 1, D), lambda i: (bmap(i), 0, 0)), vec]
    args = [xin, cw, wr, wi, lam, sav, h, hin, cg_fin]
    if has_dh:
        in_specs.append(tok)
        args.append(dh)
    if has_acc:
        in_specs.append(tok)
        args.append(acc_dx)
    if has_init:
        in_specs += [wsp, wsp, vec16]
        args += list(init)
    return pl.pallas_call(
        body, name=name, grid=(nb,),
        out_shape=[jax.ShapeDtypeStruct((n, D), F32), jax.ShapeDtypeStruct((NH, DH, DH), F32),
                   jax.ShapeDtypeStruct((NH, DH, DH), F32), jax.ShapeDtypeStruct((16, D), F32),
                   jax.ShapeDtypeStruct((1, D), F32)],
        in_specs=in_specs, out_specs=[tok, wsp, wsp, vec16, vec],
        scratch_shapes=[pltpu.VMEM((1, D), F32)],
    )(*args)


def _merge(z, o_f, o_b, hx, xin, tgt, mod, gn, p_a, p_b, w_out, ln_g, ln_b):
    n = xin.shape[0]
    tm = _row_tile(n, 128)

    def body(z4_ref, z6_ref, z7_ref, z8_ref, of_ref, ob_ref, hx_ref, x_ref, t_ref, mod_ref, gn_ref,
             pa_ref, pb_ref, wo_ref, lg_ref, lnb_ref,
             dr_ref, do_ref, dhx_ref, dz_ref, oa_o, obb_o, y_o, dya_o, dyb_o, dout_o, vec_ref):
        @pl.when(pl.program_id(0) == 0)
        def _():
            vec_ref[...] = jnp.zeros_like(vec_ref)

        gt = mod_ref[0:1, 2 * D:3 * D]
        gnv = gn_ref[...]
        o = of_ref[...] + ob_ref[...]
        rs = jnp.concatenate(
            [jnp.broadcast_to(lax.rsqrt(jnp.mean(jnp.square(o[:, h * DH:(h + 1) * DH]), axis=1, keepdims=True)
                                        + RMS_EPS), (tm, DH)) for h in range(NH)], axis=1)
        nrm = o * rs
        rn = nrm * gnv
        z4, z6, z7, z8 = z4_ref[...], z6_ref[...], z7_ref[...], z8_ref[...]
        s4, s6, s7, s8 = _sigmoid(z4), _sigmoid(z6), _sigmoid(z7), _sigmoid(z8)
        sg4, sg6 = z4 * s4, z6 * s6
        hxv = hx_ref[...]
        oa = (rn * sg4).astype(BF16)
        obb = (hxv * sg6).astype(BF16)
        ya = _dot(oa, pa_ref[...])
        yb = _dot(obb, pb_ref[...])
        y = (s7 * ya + s8 * yb).astype(BF16)
        out = _dot(y, wo_ref[...])
        xv = x_ref[...]
        rr = ALPHA * xv + gt * out
        mu = jnp.mean(rr, axis=1, keepdims=True)
        cen = rr - mu
        rstd = lax.rsqrt(jnp.mean(cen * cen, axis=1, keepdims=True) + LN_EPS)
        xhat = cen * rstd
        lg = lg_ref[...]
        err = xhat * lg + lnb_ref[...] - t_ref[...]
        loss_rows = jnp.sum(err * err, axis=1, keepdims=True)
        dxn = err * (1.0 / D)
        dxh = dxn * lg
        dr = rstd * (dxh - jnp.mean(dxh, axis=1, keepdims=True)
                     - xhat * jnp.mean(dxh * xhat, axis=1, keepdims=True))
        dout = (dr * gt).astype(BF16)
        dy = _dot_nt(dout, wo_ref[...])
        dya = (dy * s7).astype(BF16)
        dyb = (dy * s8).astype(BF16)
        doa = _dot_nt(dya, pa_ref[...])
        dob = _dot_nt(dyb, pb_ref[...])
        drn = doa * sg4
        dn = drn * gnv
        dnn = dn * nrm
        corr = jnp.concatenate(
            [jnp.broadcast_to(jnp.mean(dnn[:, h * DH:(h + 1) * DH], axis=1, keepdims=True), (tm, DH))
             for h in range(NH)], axis=1)
        dr_ref[...] = dr
        do_ref[...] = rs * (dn - nrm * corr)
        dhx_ref[...] = dob * sg6
        dz_ref[:, 0:4 * D] = jnp.zeros((tm, 4 * D), BF16)
        dz_ref[:, 4 * D:5 * D] = (doa * rn * _dsilu(z4, s4)).astype(BF16)
        dz_ref[:, 5 * D:6 * D] = jnp.zeros((tm, D), BF16)
        dz_ref[:, 6 * D:7 * D] = (dob * hxv * _dsilu(z6, s6)).astype(BF16)
        dz_ref[:, 7 * D:8 * D] = (dy * ya * s7 * (1.0 - s7)).astype(BF16)
        dz_ref[:, 8 * D:9 * D] = (dy * yb * s8 * (1.0 - s8)).astype(BF16)
        oa_o[...] = oa
        obb_o[...] = obb
        y_o[...] = y
        dya_o[...] = dya
        dyb_o[...] = dyb
        dout_o[...] = dout
        vec_ref[0:1, :] += jnp.sum(dr * out, axis=0, keepdims=True)
        vec_ref[1:2, :] += jnp.sum(dxn * xhat, axis=0, keepdims=True)
        vec_ref[2:3, :] += jnp.sum(dxn, axis=0, keepdims=True)
        vec_ref[3:4, :] += jnp.sum(drn * nrm, axis=0, keepdims=True)
        vec_ref[4:5, :] += jnp.broadcast_to(jnp.sum(loss_rows, axis=0, keepdims=True) * (0.5 / D), (1, D))

    def grp(g):
        return pl.BlockSpec((tm, D), lambda i: (i, g))

    tok = pl.BlockSpec((tm, D), lambda i: (i, 0))
    vec = pl.BlockSpec((1, D), lambda i: (0, 0))
    wsp = pl.BlockSpec((D, D), lambda i: (0, 0))
    return pl.pallas_call(
        body, name="merge", grid=(n // tm,),
        out_shape=[jax.ShapeDtypeStruct((n, D), F32)] * 3
        + [jax.ShapeDtypeStruct((n, NGRP * D), BF16)]
        + [jax.ShapeDtypeStruct((n, D), BF16)] * 6 + [jax.ShapeDtypeStruct((8, D), F32)],
        in_specs=[grp(4), grp(6), grp(7), grp(8), tok, tok, tok, tok, tok,
                  pl.BlockSpec((8, 3 * D), lambda i: (0, 0)), vec, wsp, wsp, wsp, vec, vec],
        out_specs=[tok, tok, tok, pl.BlockSpec((tm, NGRP * D), lambda i: (i, 0))] + [tok] * 6
        + [pl.BlockSpec((8, D), lambda i: (0, 0))],
    )(z, z, z, z, o_f, o_b, hx, xin, tgt, mod, gn, p_a, p_b, w_out, ln_g, ln_b)


def _wmod_grad(c_t, cctx_t, dmx_loc, dmc_loc, name):
    n = dmx_loc.shape[1]

    def body(ct_ref, cc_ref, dmx_ref, dmc_ref, o_ref):
        ct = ct_ref[...]
        sct = ct * _sigmoid(ct)
        cc = cc_ref[...]
        scc = cc * _sigmoid(cc)
        dmc = dmc_ref[0:1, :]
        for b in range(1, NDEV):
            dmc = dmc + dmc_ref[b:b + 1, :]
        acc = scc * dmc
        for b in range(NDEV):
            acc = acc + sct[:, b:b + 1] * dmx_ref[b:b + 1, :]
        o_ref[...] = acc

    return pl.pallas_call(body, name=name, out_shape=jax.ShapeDtypeStruct((D, n), F32))(c_t, cctx_t, dmx_loc, dmc_loc)


PACK_ROWS = 40


def _finalize_small(g_pack, lb, w_mod_full, params):
    npar = len(params)

    def body(*refs):
        gp_ref, lb_ref, wm_ref = refs[:3]
        wmv = refs[3:3 + 3 * npar]
        loss_ref = refs[3 + 3 * npar]
        g_refs = refs[4 + 3 * npar:4 + 4 * npar]
        upd = refs[4 + 4 * npar:4 + 7 * npar]
        tot = refs[-1]
        acc = gp_ref[0]
        for k in range(1, NDEV):
            acc = acc + gp_ref[k]
        tot[...] = acc
        mine = pl.ds(pl.multiple_of(_my_index() * DH, DH), DH)
        (g_cctx, g_bmod, g_bin, g_lbl, g_norm, g_cw, g_cb, g_br, g_bi, g_lam, g_lng, g_lnb) = g_refs

        loss_ref[...] = jnp.broadcast_to(tot[36:37, 0:DH], (8, DH))
        for k in range(3):
            g_bmod[:, k * D:(k + 1) * D] = tot[k:k + 1, :] + tot[3 + k:4 + k, :]
        dmc = jnp.concatenate([tot[3:4, :], tot[4:5, :], tot[5:6, :]], axis=1)
        cv = wmv[0][...]
        proj = _dot_nt(jnp.broadcast_to(dmc, (8, 3 * D)).astype(BF16), wm_ref[...])
        g_cctx[...] = proj[0:1, :] * _dsilu(cv, _sigmoid(cv))
        for k in range(NGRP):
            g_bin[:, k * D:(k + 1) * D] = tot[6 + k:7 + k, :]
        nrm = tot[15:16, 0:DH]
        for h in range(1, NH):
            nrm = nrm + tot[15:16, h * DH:(h + 1) * DH]
        g_norm[...] = nrm
        g_lng[...] = tot[16:17, :]
        g_lnb[...] = tot[17:18, :]
        g_cb[...] = tot[21:22, :] + tot[29:30, :]
        g_cw[0] = tot[22:26, mine] + tot[30:34, mine]
        for ref, row in ((g_br, 18), (g_bi, 19), (g_lam, 20)):
            ref[0, 0:1, :] = tot[row:row + 1, mine]
            ref[0, 1:2, :] = tot[row + 8:row + 9, mine]
        lbl = lb_ref[0:2, mine]
        dl0 = tot[34:36, mine] * lbl * (1.0 - lbl)
        g_lbl[0] = dl0
        g_lbl[1] = -dl0
        for p in range(npar):
            d, mm, vv = _adam_math(g_refs[p][...], wmv[3 * p][...], wmv[3 * p + 1][...], wmv[3 * p + 2][...])
            upd[3 * p][...] = d
            upd[3 * p + 1][...] = mm
            upd[3 * p + 2][...] = vv

    flat = [t for wmv in params for t in wmv]
    shapes = [jax.ShapeDtypeStruct(wmv[0].shape, F32) for wmv in params]
    outs = pl.pallas_call(
        body, name="finalize_small",
        out_shape=[jax.ShapeDtypeStruct((8, DH), F32)] + shapes + [s for s in shapes for _ in range(3)],
        scratch_shapes=[pltpu.VMEM((PACK_ROWS, D), F32)],
    )(g_pack, lb, w_mod_full, *flat)
    grads = list(outs[1:1 + npar])
    upd = [tuple(outs[1 + npar + 3 * p:4 + npar + 3 * p]) for p in range(npar)]
    return outs[0], grads, upd


def _to_colmajor(t, rows):
    return t.reshape(rows, GRID_W, D).transpose(1, 0, 2).reshape(rows * GRID_W, D)


def _to_raster(t, rows):
    return t.reshape(GRID_W, rows, D).transpose(1, 0, 2).reshape(rows * GRID_W, D)


def _local_cols(t, me, width):
    return lax.dynamic_slice_in_dim(t, me * width, width, axis=t.ndim - 1)


def kernel(x, c, ctx, c_ctx, w_mod, b_mod, w_in, b_in, lb_logits, norm_a_g, conv_w, conv_b, w_r, b_r, w_i, b_i, lam, p_a, p_b, w_out, ln_g, ln_b, loss_target, m_c_ctx, m_w_mod, m_b_mod, m_w_in, m_b_in, m_lb_logits, m_norm_a_g, m_conv_w, m_conv_b, m_w_r, m_b_r, m_w_i, m_b_i, m_lam, m_p_a, m_p_b, m_w_out, m_ln_g, m_ln_b, v_c_ctx, v_w_mod, v_b_mod, v_w_in, v_b_in, v_lb_logits, v_norm_a_g, v_conv_w, v_conv_b, v_w_r, v_b_r, v_w_i, v_b_i, v_lam, v_p_a, v_p_b, v_w_out, v_ln_g, v_ln_b):
    me = _my_index()
    xs, cs, tgt = x[0], ctx[0], loss_target[0]
    t_len, c_len = xs.shape[0], cs.shape[0]
    rows = t_len // GRID_W
    wcols = w_in.shape[2]
    mcols = w_mod.shape[2]

    small = jnp.concatenate([lb_logits.reshape(4, DH), conv_w[0], b_r[0], b_i[0], lam[0], jnp.zeros((2, DH), F32),
                             c.reshape(8, DH)], axis=0)
    g_small, g_wmod = _all_gather([small, w_mod[0].astype(BF16)], "gather_params")

    def full_rows(lo, hi):
        return g_small[:, lo:hi, :].transpose(1, 0, 2).reshape(hi - lo, D)

    lbl_f, cw_f, br_f, bi_f, lam_f = full_rows(0, 4), full_rows(4, 8), full_rows(8, 10), full_rows(10, 12), full_rows(12, 14)
    c_all = g_small[:, 16:24, :].reshape(NDEV, D)
    w_mod_f = g_wmod.transpose(1, 0, 2).reshape(D, 3 * D)
    w_r_b, w_i_b = w_r[0].astype(BF16), w_i[0].astype(BF16)

    cc = jnp.concatenate([c.reshape(1, D), c_ctx.reshape(1, D), jnp.zeros((6, D), F32)], axis=0)
    lbl_p = jnp.concatenate([lbl_f.reshape(2, 2, D), jnp.zeros((2, 6, D), F32)], axis=1)
    mod, lb = _prep(cc, w_mod_f, b_mod, lbl_p)
    u_x = _modulate(xs, mod, 0, "modulate_x")
    u_c = _modulate(cs, mod, 1, "modulate_c")
    z_x, w_in_f = _inproj_gather(u_x, w_in[0].astype(BF16), b_in, "inproj_gather")
    z_c = _mm_bias(u_c, w_in_f, b_in, "inproj_c")

    zero_s = jnp.zeros((NH, DH, DH), F32)
    zero_v = jnp.zeros((1, D), F32)
    gla = {}
    for d in (0, 1):
        _, ssc, sfc = _gla_fwd(z_c, lb, zero_s, d, f"gla_fwd_c{d}")
        o_d, ssx, _ = _gla_fwd(z_x, lb, sfc, d, f"gla_fwd_x{d}")
        gla[d] = (ssc, ssx, o_d)

    x5_c = z_c[:, 5 * D:6 * D]
    x5_x = _to_colmajor(z_x[:, 5 * D:6 * D], rows)
    cb2 = conv_b.reshape(1, D)
    lru = {}
    out_w = [p_a[0].astype(BF16), p_b[0].astype(BF16), w_out[0].astype(BF16)]
    for d in (0, 1):
        prm = (cw_f, cb2, w_r_b[d], br_f[d:d + 1], w_i_b[d], bi_f[d:d + 1], lam_f[d:d + 1])
        h_c, hin_c, hfin_c, sav_c = _lru_fwd(x5_c, c_len, *prm, zero_v, None, d, f"lru_fwd_c{d}")
        h_x, hin_x, _, sav_x, *more = _lru_fwd(x5_x, rows, *prm, hfin_c, lru[0][3] if d else None, d,
                                               f"lru_fwd_x{d}", side=() if d else out_w)
        if d == 0:
            g_pa, g_pb, g_wo = more
        else:
            h_sum = more[0]
        lru[d] = ((cw_f, w_r_b[d], w_i_b[d], lam_f[d:d + 1]), h_c, hin_c, h_x, hin_x, sav_c, sav_x)
    p_a_f, p_b_f, w_out_f = g_pa.reshape(D, D), g_pb.reshape(D, D), g_wo.reshape(D, D)
    hx = _to_raster(h_sum, rows)

    gn = jnp.tile(norm_a_g.reshape(1, DH), (1, NH))
    (dr, do, dhx, dz_m, oa, obb, yb16, dya, dyb, dout, mvec) = _merge(
        z_x, gla[0][2], gla[1][2], hx, xs, tgt, mod, gn, p_a_f, p_b_f, w_out_f, ln_g, ln_b)

    dhx_cm = _to_colmajor(dhx, rows)
    lru_dx_x = lru_dx_c = None
    for d in (0, 1):
        prm, h_c, hin_c, h_x, hin_x, sav_c, sav_x = lru[d]
        lru_dx_x, dwr, dwi, lvec, cg0 = _lru_bwd(x5_x, rows, *prm, sav_x, h_x, hin_x, dhx_cm, zero_v, lru_dx_x, None,
                                                 d, f"lru_bwd_x{d}")
        lru_dx_c, dwr, dwi, lvec, _ = _lru_bwd(x5_c, c_len, *prm, sav_c, h_c, hin_c, None, cg0, lru_dx_c,
                                               (dwr, dwi, lvec), d, f"lru_bwd_c{d}")
        lru[d] = (dwr, dwi, lvec)
    dz5_x = _to_raster(lru_dx_x, rows).astype(BF16)
    dz5_c = lru_dx_c.astype(BF16)

    gq_c = gv_c = None
    dzf_c, dlb = {}, {}
    gq_x, dzf_x0, gv_x, dlb_x, ds0 = _gla_bwd(z_x, lb, gla[0][1], do, zero_s, None, None, 0, "gla_bwd_x0", f_dtype=BF16)
    gq_c, dzf_c[0], gv_c, dlb_c, _ = _gla_bwd(z_c, lb, gla[0][0], None, ds0, None, None, 0, "gla_bwd_c0")
    dlb[0] = dlb_x[0:1] + dlb_c[0:1]
    dz_g, dlb_x, ds0 = _gla_bwd(z_x, lb, gla[1][1], do, zero_s, gq_x, gv_x, 1, "gla_bwd_x1", into=(dz_m, dzf_x0))
    gq_c, dzf_c[1], gv_c, dlb_c, _ = _gla_bwd(z_c, lb, gla[1][0], None, ds0, gq_c, gv_c, 1, "gla_bwd_c1")
    dlb[1] = dlb_x[0:1] + dlb_c[0:1]

    bf = lambda t: t.astype(BF16)
    dz_x = lax.dynamic_update_slice(dz_g, dz5_x, (0, 5 * D))
    zc0 = jnp.zeros((c_len, D), BF16)
    dz_c = jnp.concatenate([bf(gq_c), bf(dzf_c[0]), bf(dzf_c[1]), bf(gv_c), zc0, dz5_c, zc0, zc0, zc0], axis=1)
    dwin_c, dbin_c = _mm_tn(u_c, dz_c, None, "dwin_c", with_colsum=True)
    dpa = _mm_tn(oa, dya, None, "dpa", out_dtype=BF16)
    dpb = _mm_tn(obb, dyb, None, "dpb", out_dtype=BF16)
    dwo = _mm_tn(yb16, dout, None, "dwout", out_dtype=BF16)

    wr_pack = jnp.concatenate([lru[0][0], lru[1][0], lru[0][1], lru[1][1]], axis=0).reshape(4 * NH * DH, DH)
    grad_x, xvec, r_pa, r_pb, r_wo, r_wri = _input_grad(
        dz_x, w_in_f, xs, dr, mod, 0, "input_grad_x", side=[dpa, dpb, dwo, wr_pack], side_splits=[0, 0, 0, 0])
    r_win, dbin = _dwin_exchange(u_x, dz_x, dwin_c, dbin_c, "dwin_exchange")
    _, cvec = _input_grad(dz_c, w_in_f, cs, None, mod, 1, "input_grad_c")
    wri_piece = _sum_rows(r_wri, "sum_w_ri_piece")
    g_w_in, d_w_in, nm_w_in, nv_w_in = _sum_adamw(r_win, w_in, m_w_in, v_w_in, "update_w_in")
    g_p_a, d_p_a, nm_p_a, nv_p_a = _sum_adamw(r_pa, p_a, m_p_a, v_p_a, "update_p_a")
    g_p_b, d_p_b, nm_p_b, nv_p_b = _sum_adamw(r_pb, p_b, m_p_b, v_p_b, "update_p_b")
    g_w_out, d_w_out, nm_w_out, nv_w_out = _sum_adamw(r_wo, w_out, m_w_out, v_w_out, "update_w_out")

    dlb_rows = jnp.concatenate([dlb[0], dlb[1]], axis=0)
    pack = jnp.concatenate([
        xvec[0:1], xvec[1:2], mvec[0:1],
        cvec[0:1], cvec[1:2], jnp.zeros((1, D), F32),
        dbin.reshape(NGRP, D),
        mvec[3:4], mvec[1:2], mvec[2:3],
        lru[0][2][0:8], lru[1][2][0:3],
        lru[1][2][3:8],
        dlb_rows,
        mvec[4:5],
        jnp.zeros((3, D), F32)], axis=0)
    g_pack, g_wri = _all_gather([pack, wri_piece], "gather_small_grads")

    dmx = g_pack[:, 0:3, :].reshape(NDEV, 3 * D)
    dmc = g_pack[:, 3:6, :].reshape(NDEV, 3 * D)
    grad_w_mod = _wmod_grad(c_all.T, c_ctx.reshape(D, 1), _local_cols(dmx, me, mcols), _local_cols(dmc, me, mcols),
                            "grad_w_mod").reshape(1, D, mcols)
    small_params = [(c_ctx.reshape(1, D), m_c_ctx.reshape(1, D), v_c_ctx.reshape(1, D)), (b_mod, m_b_mod, v_b_mod),
                    (b_in, m_b_in, v_b_in), (lb_logits, m_lb_logits, v_lb_logits), (norm_a_g, m_norm_a_g, v_norm_a_g),
                    (conv_w, m_conv_w, v_conv_w), (conv_b, m_conv_b, v_conv_b), (b_r, m_b_r, v_b_r),
                    (b_i, m_b_i, v_b_i), (lam, m_lam, v_lam), (ln_g, m_ln_g, v_ln_g), (ln_b, m_ln_b, v_ln_b)]
    loss_tile, small_g, small_upd = _finalize_small(g_pack, lb, w_mod_f, small_params)
    loss = loss_tile[0, 0]
    (grad_c_ctx, grad_b_mod, grad_b_in, grad_lb_logits, grad_norm_a_g, grad_conv_w, grad_conv_b, grad_b_r, grad_b_i,
     grad_lam, grad_ln_g, grad_ln_b) = small_g
    small_upd[0] = tuple(t.reshape(c_ctx.shape) for t in small_upd[0])
    (o_c_ctx, o_b_mod, o_b_in, o_lb, o_norm, o_conv_w, o_conv_b, o_b_r, o_b_i, o_lam, o_ln_g, o_ln_b) = small_upd

    half = 2 * NH * DH
    g_ri = g_wri.reshape(2 * half, DH)
    grad_w_r, grad_w_i = g_ri[:half].reshape(w_r.shape), g_ri[half:].reshape(w_i.shape)
    d_w_r, nm_w_r, nv_w_r = _adamw(grad_w_r, w_r, m_w_r, v_w_r, "update_w_r")
    d_w_i, nm_w_i, nv_w_i = _adamw(grad_w_i, w_i, m_w_i, v_w_i, "update_w_i")

    d_w_mod, nm_w_mod, nv_w_mod = _adamw(grad_w_mod, w_mod, m_w_mod, v_w_mod, "update_w_mod")

    grads = [grad_c_ctx.reshape(c_ctx.shape), grad_w_mod, grad_b_mod, g_w_in, grad_b_in, grad_lb_logits, grad_norm_a_g,
             grad_conv_w, grad_conv_b, grad_w_r, grad_b_r, grad_w_i, grad_b_i, grad_lam, g_p_a, g_p_b, g_w_out,
             grad_ln_g, grad_ln_b]
    per_kind = []
    for k in range(3):
        per_kind.append([
            o_c_ctx[k], (d_w_mod, nm_w_mod, nv_w_mod)[k], o_b_mod[k], (d_w_in, nm_w_in, nv_w_in)[k], o_b_in[k], o_lb[k],
            o_norm[k], o_conv_w[k], o_conv_b[k], (d_w_r, nm_w_r, nv_w_r)[k], o_b_r[k], (d_w_i, nm_w_i, nv_w_i)[k],
            o_b_i[k], o_lam[k], (d_p_a, nm_p_a, nv_p_a)[k], (d_p_b, nm_p_b, nv_p_b)[k], (d_w_out, nm_w_out, nv_w_out)[k],
            o_ln_g[k], o_ln_b[k]])
    return (loss, grad_x.reshape(x.shape), *grads, *per_kind[0], *per_kind[1], *per_kind[2])
```

```python
import functools

import jax
import jax.numpy as jnp
from jax import lax
from jax.experimental import pallas as pl
from jax.experimental.pallas import tpu as pltpu

F32 = jnp.float32
BF16 = jnp.bfloat16

D = 1024
NH = 8
DH = 128
CHUNK = 64
GLA_HEADS_PER_STEP = 8
GRID_W = 64
NGRP = 9
NDEV = 8
RG_C = 8.0
ALPHA = 2.0 ** 0.25
LN_EPS = 1e-5
RMS_EPS = 1e-6
Q_SCALE = DH ** -0.5
ADAM_LR, ADAM_B1, ADAM_B2, ADAM_EPS, ADAM_WD, ADAM_STEP = 1e-3, 0.9, 0.999, 1e-8, 0.01, 10
ADAM_C1 = 1.0 / (1.0 - ADAM_B1 ** ADAM_STEP)
ADAM_C2 = 1.0 / (1.0 - ADAM_B2 ** ADAM_STEP)

ANY = pl.BlockSpec(memory_space=pl.ANY)


def _sigmoid(t):
    return 1.0 / (1.0 + jnp.exp(-t))


def _dsilu(t, s):
    return s * (1.0 + t * (1.0 - s))


def _dot(a, b):
    return jnp.dot(a, b, preferred_element_type=F32)


def _dot_nt(a, b):
    return lax.dot_general(a, b, (((1,), (1,)), ((), ())), preferred_element_type=F32)


def _dot_tn(a, b):
    return lax.dot_general(a, b, (((0,), (0,)), ((), ())), preferred_element_type=F32)


def _my_index():
    return 4 * lax.axis_index("x") + 2 * lax.axis_index("y") + lax.axis_index("c")


def _dev_tuple(j):
    return (j >> 2, (j >> 1) & 1, j & 1)


def _exchange_sems(n):
    return [pltpu.SemaphoreType.DMA((n * NDEV,)), pltpu.SemaphoreType.DMA((n * NDEV,)), pltpu.SemaphoreType.DMA((n,))]


def _exchange(ins, outs, sems, piece_of=None):
    send_sems, recv_sems, loc_sems = sems
    n = len(ins)

    def src(a, p):
        return ins[a] if piece_of is None else piece_of(ins[a], a, p)

    def push(a, t):
        me, p = _my_index(), _step_peer(t)
        return pltpu.make_async_remote_copy(
            src_ref=src(a, p), dst_ref=outs[a].at[me],
            send_sem=send_sems.at[a * NDEV + t], recv_sem=recv_sems.at[a * NDEV + me],
            device_id=_dev_of(p), device_id_type=pl.DeviceIdType.MESH)

    def local(a):
        me = _my_index()
        return pltpu.make_async_copy(src(a, me), outs[a].at[me], loc_sems.at[a])

    def start():
        for a in range(n):
            local(a).start()
        for t in range(NDEV - 1):
            for a in range(n):
                push(a, t).start()

    def finish():
        me = _my_index()
        for t in range(NDEV - 1):
            for a in range(n):
                push(a, t).wait_send()
        for j in range(NDEV):
            @pl.when(me != j)
            def _():
                for a in range(n):
                    pltpu.make_async_remote_copy(
                        src_ref=src(a, j), dst_ref=outs[a].at[j],
                        send_sem=send_sems.at[a * NDEV], recv_sem=recv_sems.at[a * NDEV + j],
                        device_id=_dev_tuple(j), device_id_type=pl.DeviceIdType.MESH).wait_recv()
        for a in range(n):
            local(a).wait()

    return start, finish


def _all_gather(shards, name):
    n = len(shards)

    def body(*refs):
        start, finish = _exchange(refs[:n], refs[n:2 * n], refs[2 * n:])
        start()
        finish()

    return pl.pallas_call(
        body, name=name,
        out_shape=[jax.ShapeDtypeStruct((NDEV,) + s.shape, s.dtype) for s in shards],
        in_specs=[ANY] * n, out_specs=[ANY] * n, scratch_shapes=_exchange_sems(n),
    )(*shards)


def _pieces(parts, splits):
    shapes = []
    for part, split in zip(parts, splits):
        r, c = part.shape
        shapes.append((r // NDEV, c) if split == 0 else (r, c // NDEV))

    def piece_of(ref, a, j):
        pr, pc = shapes[a]
        if splits[a] == 0:
            start = j * pr if isinstance(j, int) else pl.multiple_of(j * pr, pr)
            return ref.at[pl.ds(start, pr), :]
        start = j * pc if isinstance(j, int) else pl.multiple_of(j * pc, pc)
        return ref.at[:, pl.ds(start, pc)]

    return shapes, piece_of


_STEP_MASKS = ((2, 4, 6, 3, 5, 7, 1, 0), (4, 2, 6, 5, 3, 7, 1, 0))
_GATHER_MASKS = ((0, 1, 2, 4, 3, 5, 6, 7), (0, 1, 4, 2, 5, 3, 6, 7))


def _peer_schedule(table):
    tab = jnp.array(table, jnp.int32)
    return jnp.bitwise_xor(_my_index(), tab[lax.axis_index("c")])


def _step_peer(s, table=_STEP_MASKS):
    def pick(row):
        if isinstance(s, int):
            return jnp.int32(row[s])
        m = jnp.int32(row[NDEV - 1])
        for t in range(NDEV - 2, -1, -1):
            m = jnp.where(s == t, jnp.int32(row[t]), m)
        return m
    mask = jnp.where(lax.axis_index("c") == 0, pick(table[0]), pick(table[1]))
    return jnp.bitwise_xor(_my_index(), mask)


def _dev_of(p):
    return (p // 4, (p // 2) % 2, p % 2)


def _dwin_exchange(u, dz, init, cs_init, name):
    m, ka = u.shape
    n = dz.shape[1]
    pc = n // NDEV
    tk = _row_tile(m, 512)
    nk = m // tk

    def body(pidx_ref, u_ref, dz_ref, init_ref, csi_ref, rwin, cs_ref, acc, sbuf, wsend, wrecv, wloc):
        s, k = pl.program_id(0), pl.program_id(1)
        me = _my_index()

        def slab_copy(slot, p):
            return pltpu.make_async_remote_copy(
                src_ref=sbuf.at[slot], dst_ref=rwin.at[me], send_sem=wsend.at[slot], recv_sem=wrecv.at[me],
                device_id=_dev_of(p), device_id_type=pl.DeviceIdType.MESH)

        @pl.when(k == 0)
        def _():
            acc[...] = init_ref[...]
            cs_ref[...] = csi_ref[...]

        bv = dz_ref[...]
        acc[...] += _dot_tn(u_ref[...], bv)
        cs_ref[...] += jnp.sum(bv.astype(F32), axis=0, keepdims=True)

        @pl.when(k == nk - 1)
        def _():
            slot = s % 2

            @pl.when(s >= 2)
            def _():
                slab_copy(slot, me).wait_send()

            sbuf[slot] = acc[...].astype(BF16)

            @pl.when(s < NDEV - 1)
            def _():
                slab_copy(slot, _step_peer(s)).start()

            @pl.when(s == NDEV - 1)
            def _():
                own = pltpu.make_async_copy(sbuf.at[slot], rwin.at[me], wloc.at[0])
                own.start()
                slab_copy(1 - slot, me).wait_send()
                for j in range(NDEV):
                    @pl.when(me != j)
                    def _():
                        pltpu.make_async_remote_copy(
                            src_ref=sbuf.at[0], dst_ref=rwin.at[j], send_sem=wsend.at[0], recv_sem=wrecv.at[j],
                            device_id=_dev_tuple(j), device_id_type=pl.DeviceIdType.MESH).wait_recv()
                own.wait()

    grid_spec = pltpu.PrefetchScalarGridSpec(
        num_scalar_prefetch=1, grid=(NDEV, nk),
        in_specs=[pl.BlockSpec((tk, ka), lambda s, k, pidx: (k, 0)),
                  pl.BlockSpec((tk, pc), lambda s, k, pidx: (k, pidx[s])),
                  pl.BlockSpec((ka, pc), lambda s, k, pidx: (0, pidx[s])),
                  pl.BlockSpec((1, pc), lambda s, k, pidx: (0, pidx[s]))],
        out_specs=[ANY, pl.BlockSpec((1, pc), lambda s, k, pidx: (0, pidx[s]))],
        scratch_shapes=[pltpu.VMEM((ka, pc), F32), pltpu.VMEM((2, ka, pc), BF16),
                        pltpu.SemaphoreType.DMA((2,)), pltpu.SemaphoreType.DMA((NDEV,)), pltpu.SemaphoreType.DMA((1,))])
    return pl.pallas_call(
        body, name=name, grid_spec=grid_spec,
        out_shape=[jax.ShapeDtypeStruct((NDEV, ka, pc), BF16), jax.ShapeDtypeStruct((1, n), F32)],
    )(_peer_schedule(_STEP_MASKS), u, dz, init, cs_init)


def _inproj_gather(u, w_loc, bias, name):
    m, k = u.shape
    pc = w_loc.shape[1]
    n = pc * NDEV
    tm = _row_tile(m, 512)
    ni = m // tm

    def body(pidx_ref, u_ref, b_ref, wl_ref, z_ref, wall, wbuf, wsend, wrecv, ldsem, ownsem):
        s, i = pl.program_id(0), pl.program_id(1)
        me = _my_index()

        def shard_push(t):
            return pltpu.make_async_remote_copy(
                src_ref=wl_ref, dst_ref=wall.at[me], send_sem=wsend.at[t], recv_sem=wrecv.at[me],
                device_id=_dev_of(_step_peer(t, _GATHER_MASKS)), device_id_type=pl.DeviceIdType.MESH)

        def load(slot, src):
            return pltpu.make_async_copy(src, wbuf.at[slot], ldsem.at[slot])

        own = pltpu.make_async_copy(wl_ref, wall.at[me], ownsem.at[0])

        @pl.when((s == 0) & (i == 0))
        def _():
            own.start()
            load(0, wl_ref).start()
            for t in range(1, NDEV):
                shard_push(t).start()

        @pl.when((i == ni // 2) & (s < NDEV - 1))
        def _():
            nxt = _step_peer(s + 1, _GATHER_MASKS)
            pltpu.make_async_remote_copy(
                src_ref=wl_ref, dst_ref=wall.at[nxt], send_sem=wsend.at[0], recv_sem=wrecv.at[nxt],
                device_id=_dev_of(nxt), device_id_type=pl.DeviceIdType.MESH).wait_recv()
            load((s + 1) % 2, wall.at[nxt]).start()

        @pl.when(i == 0)
        def _():
            load(s % 2, wl_ref).wait()

        z_ref[...] = _dot(u_ref[...], wbuf[s % 2]) + b_ref[...]

        @pl.when((s == NDEV - 1) & (i == ni - 1))
        def _():
            own.wait()
            for t in range(1, NDEV):
                shard_push(t).wait_send()

    grid_spec = pltpu.PrefetchScalarGridSpec(
        num_scalar_prefetch=1, grid=(NDEV, ni),
        in_specs=[pl.BlockSpec((tm, k), lambda s, i, pidx: (i, 0)),
                  pl.BlockSpec((1, pc), lambda s, i, pidx: (0, pidx[s])), ANY],
        out_specs=[pl.BlockSpec((tm, pc), lambda s, i, pidx: (i, pidx[s])), ANY],
        scratch_shapes=[pltpu.VMEM((2, k, pc), BF16),
                        pltpu.SemaphoreType.DMA((NDEV,)), pltpu.SemaphoreType.DMA((NDEV,)),
                        pltpu.SemaphoreType.DMA((2,)), pltpu.SemaphoreType.DMA((1,))])
    return pl.pallas_call(
        body, name=name, grid_spec=grid_spec,
        out_shape=[jax.ShapeDtypeStruct((m, n), F32), jax.ShapeDtypeStruct((NDEV, k, pc), w_loc.dtype)],
    )(_peer_schedule(_GATHER_MASKS), u, bias, w_loc)


def _adam_math(g, w, m, v):
    m2 = ADAM_B1 * m + (1.0 - ADAM_B1) * g
    v2 = ADAM_B2 * v + (1.0 - ADAM_B2) * (g * g)
    delta = -ADAM_LR * ((m2 * ADAM_C1) / (jnp.sqrt(v2 * ADAM_C2) + ADAM_EPS) + ADAM_WD * w)
    return delta, m2, v2


def _row_tile(r, cap):
    t = min(r, cap)
    while r % t:
        t //= 2
    return t


def _adamw(g, w, m, v, name):
    shape = w.shape
    cols = shape[-1] if w.ndim >= 2 and shape[-1] % 128 == 0 else 128
    g2, w2, m2, v2 = (t.reshape(-1, cols) for t in (g, w, m, v))
    r = g2.shape[0]
    tr = _row_tile(r, 256)

    def body(g_ref, w_ref, m_ref, v_ref, d_ref, mo_ref, vo_ref):
        d, mm, vv = _adam_math(g_ref[...], w_ref[...], m_ref[...], v_ref[...])
        d_ref[...] = d
        mo_ref[...] = mm
        vo_ref[...] = vv

    spec = pl.BlockSpec((tr, cols), lambda i: (i, 0))
    outs = pl.pallas_call(
        body, name=name, grid=(r // tr,),
        out_shape=[jax.ShapeDtypeStruct((r, cols), F32)] * 3,
        in_specs=[spec] * 4, out_specs=[spec] * 3,
    )(g2, w2, m2, v2)
    return tuple(o.reshape(shape) for o in outs)


def _sum_adamw(parts, w, m, v, name):
    _, r, c = parts.shape
    shape = w.shape
    w2, m2, v2 = (t.reshape(r, c) for t in (w, m, v))
    tr = _row_tile(r, 128)

    def body(p_ref, w_ref, m_ref, v_ref, g_ref, d_ref, mo_ref, vo_ref):
        g = p_ref[0].astype(F32)
        for k in range(1, NDEV):
            g = g + p_ref[k].astype(F32)
        d, mm, vv = _adam_math(g, w_ref[...], m_ref[...], v_ref[...])
        g_ref[...] = g
        d_ref[...] = d
        mo_ref[...] = mm
        vo_ref[...] = vv

    spec = pl.BlockSpec((tr, c), lambda i: (i, 0))
    outs = pl.pallas_call(
        body, name=name, grid=(r // tr,),
        out_shape=[jax.ShapeDtypeStruct((r, c), F32)] * 4,
        in_specs=[pl.BlockSpec((NDEV, tr, c), lambda i: (0, i, 0))] + [spec] * 3, out_specs=[spec] * 4,
    )(parts, w2, m2, v2)
    return tuple(o.reshape(shape) for o in outs)


def _sum_rows(parts, name):
    _, r, c = parts.shape

    def body(p_ref, o_ref):
        g = p_ref[0]
        for k in range(1, NDEV):
            g = g + p_ref[k]
        o_ref[...] = g

    return pl.pallas_call(
        body, name=name, out_shape=jax.ShapeDtypeStruct((r, c), F32),
    )(parts)


def _prep(cc, w_mod_full, b_mod, lbl):
    def body(cc_ref, w_ref, b_ref, l_ref, mod_ref, lb_ref):
        t = cc_ref[...]
        s = (t * _sigmoid(t)).astype(BF16)
        mod_ref[...] = _dot(s, w_ref[...]) + b_ref[...]
        lb_ref[...] = _sigmoid(l_ref[0] - l_ref[1])

    return pl.pallas_call(
        body, name="prep",
        out_shape=[jax.ShapeDtypeStruct((8, 3 * D), F32), jax.ShapeDtypeStruct((8, D), F32)],
    )(cc, w_mod_full, b_mod, lbl)


def _modulate(xin, mod, row, name):
    n = xin.shape[0]
    tm = _row_tile(n, 512)

    def body(x_ref, mod_ref, u_ref):
        sh = mod_ref[row:row + 1, 0:D]
        sc = mod_ref[row:row + 1, D:2 * D]
        u_ref[...] = (x_ref[...] * (1.0 + sc) + sh).astype(BF16)

    return pl.pallas_call(
        body, name=name, grid=(n // tm,),
        out_shape=jax.ShapeDtypeStruct((n, D), BF16),
        in_specs=[pl.BlockSpec((tm, D), lambda i: (i, 0)), pl.BlockSpec((8, 3 * D), lambda i: (0, 0))],
        out_specs=pl.BlockSpec((tm, D), lambda i: (i, 0)),
    )(xin, mod)


def _mm_bias(a, w_all, bias, name):
    m, k = a.shape
    tn = w_all.shape[2]
    n = tn * NDEV
    tm = _row_tile(m, 512)

    def body(a_ref, b_ref, bias_ref, o_ref):
        o_ref[...] = _dot(a_ref[...], b_ref[0]) + bias_ref[...]

    return pl.pallas_call(
        body, name=name, grid=(NDEV, m // tm),
        out_shape=jax.ShapeDtypeStruct((m, n), F32),
        in_specs=[pl.BlockSpec((tm, k), lambda j, i: (i, 0)), pl.BlockSpec((1, k, tn), lambda j, i: (j, 0, 0)),
                  pl.BlockSpec((1, tn), lambda j, i: (0, j))],
        out_specs=pl.BlockSpec((tm, tn), lambda j, i: (i, j)),
    )(a, w_all, bias)


def _mm_tn(a, b, init, name, with_colsum=False, colsum_init=None, out_dtype=F32):
    m, ka = a.shape
    n = b.shape[1]
    tk = _row_tile(m, 512)
    tn = 1024
    nk = m // tk
    has_init = init is not None

    def body(*refs):
        a_ref, b_ref = refs[0], refs[1]
        pos = 2
        init_ref = cs_init_ref = None
        if has_init:
            init_ref = refs[pos]
            pos += 1
            if with_colsum:
                cs_init_ref = refs[pos]
                pos += 1
        o_ref = refs[pos]
        cs_ref = refs[pos + 1] if with_colsum else None
        acc = refs[-1]
        k = pl.program_id(1)

        @pl.when(k == 0)
        def _():
            if has_init:
                acc[...] = init_ref[...]
                if with_colsum:
                    cs_ref[...] = cs_init_ref[...]
            else:
                acc[...] = jnp.zeros_like(acc)
                if with_colsum:
                    cs_ref[...] = jnp.zeros_like(cs_ref)

        bv = b_ref[...]
        acc[...] += _dot_tn(a_ref[...], bv)
        if with_colsum:
            cs_ref[...] += jnp.sum(bv.astype(F32), axis=0, keepdims=True)

        @pl.when(k == nk - 1)
        def _():
            o_ref[...] = acc[...].astype(out_dtype)

    in_specs = [pl.BlockSpec((tk, ka), lambda j, k: (k, 0)), pl.BlockSpec((tk, tn), lambda j, k: (k, j))]
    args = [a, b]
    if has_init:
        in_specs.append(pl.BlockSpec((ka, tn), lambda j, k: (0, j)))
        args.append(init)
        if with_colsum:
            in_specs.append(pl.BlockSpec((1, tn), lambda j, k: (0, j)))
            args.append(colsum_init)
    out_shape = [jax.ShapeDtypeStruct((ka, n), out_dtype)]
    out_specs = [pl.BlockSpec((ka, tn), lambda j, k: (0, j))]
    if with_colsum:
        out_shape.append(jax.ShapeDtypeStruct((1, n), F32))
        out_specs.append(pl.BlockSpec((1, tn), lambda j, k: (0, j)))
    outs = pl.pallas_call(
        body, name=name, grid=(n // tn, nk), out_shape=out_shape, in_specs=in_specs, out_specs=out_specs,
        scratch_shapes=[pltpu.VMEM((ka, tn), F32)],
    )(*args)
    return outs if with_colsum else outs[0]


def _input_grad(dz, w_all, xin, dr, mod, row, name, side=(), side_splits=()):
    m, n = dz.shape
    tm = _row_tile(m, 512)
    tk = w_all.shape[2]
    nk = NDEV
    ni = m // tm
    has_dr = dr is not None
    ns = len(side)
    piece_shapes, piece_of = _pieces(side, side_splits)

    def body(*refs):
        dz_ref, w_ref, x_ref = refs[:3]
        pos = 3
        dr_ref = refs[pos] if has_dr else None
        pos += int(has_dr)
        mod_ref = refs[pos]
        side_in = refs[pos + 1:pos + 1 + ns]
        pos += 1 + ns
        gx_ref = refs[pos] if has_dr else None
        pos += int(has_dr)
        vec_ref = refs[pos]
        side_out = refs[pos + 1:pos + 1 + ns]
        acc = refs[pos + 1 + ns]
        i, k = pl.program_id(0), pl.program_id(1)
        if ns:
            side_start, side_finish = _exchange(side_in, side_out, refs[pos + 2 + ns:], piece_of)

            @pl.when((i == 0) & (k == 0))
            def _():
                side_start()

        @pl.when(k == 0)
        def _():
            acc[...] = jnp.zeros_like(acc)

        @pl.when((i == 0) & (k == 0))
        def _():
            vec_ref[...] = jnp.zeros_like(vec_ref)

        acc[...] += _dot_nt(dz_ref[...], w_ref[0])

        @pl.when(k == nk - 1)
        def _():
            du = acc[...]
            xv = x_ref[...]
            if has_dr:
                sc = mod_ref[row:row + 1, D:2 * D]
                gx_ref[...] = ALPHA * dr_ref[...] + du * (1.0 + sc)
            vec_ref[0:1, :] += jnp.sum(du, axis=0, keepdims=True)
            vec_ref[1:2, :] += jnp.sum(du * xv, axis=0, keepdims=True)

        if ns:
            @pl.when((i == ni - 1) & (k == nk - 1))
            def _():
                side_finish()

    row_spec = pl.BlockSpec((tm, D), lambda i, k: (i, 0))
    in_specs = [pl.BlockSpec((tm, tk), lambda i, k: (i, k)), pl.BlockSpec((1, D, tk), lambda i, k: (k, 0, 0)), row_spec]
    args = [dz, w_all, xin]
    if has_dr:
        in_specs.append(row_spec)
        args.append(dr)
    in_specs.append(pl.BlockSpec((8, 3 * D), lambda i, k: (0, 0)))
    args.append(mod)
    in_specs += [ANY] * ns
    args += list(side)
    out_shape, out_specs = [], []
    if has_dr:
        out_shape.append(jax.ShapeDtypeStruct((m, D), F32))
        out_specs.append(row_spec)
    out_shape.append(jax.ShapeDtypeStruct((8, D), F32))
    out_specs.append(pl.BlockSpec((8, D), lambda i, k: (0, 0)))
    out_shape += [jax.ShapeDtypeStruct((NDEV,) + piece_shapes[a], side[a].dtype) for a in range(ns)]
    out_specs += [ANY] * ns
    outs = pl.pallas_call(
        body, name=name, grid=(ni, nk), out_shape=out_shape, in_specs=in_specs, out_specs=out_specs,
        scratch_shapes=[pltpu.VMEM((tm, D), F32)] + (_exchange_sems(ns) if ns else []),
    )(*args)
    return tuple(outs) if has_dr else (None, *outs)


def _tri(reverse):
    r = lax.broadcasted_iota(jnp.int32, (CHUNK, CHUNK), 0)
    c = lax.broadcasted_iota(jnp.int32, (CHUNK, CHUNK), 1)
    return (c >= r) if reverse else (c <= r)


def _cum_f32(tri_b, t):
    hi = t.astype(BF16)
    r1 = t - hi.astype(F32)
    mid = r1.astype(BF16)
    lo = (r1 - mid.astype(F32)).astype(BF16)
    return _dot(tri_b, hi) + _dot(tri_b, mid) + _dot(tri_b, lo)


def _gla_features(zq, zf, lb):
    sq = _sigmoid(zq)
    q = zq * sq * Q_SCALE
    sf = _sigmoid(zf)
    f = lb + (1.0 - lb) * sf
    return q, sq, f, sf


def _gla_decays(f, tri_b, last):
    lf = jnp.log(f)
    g = _cum_f32(tri_b, lf)
    gl = g[last:last + 1, :]
    return g, gl


def _gla_block(n):
    return 256 if n % 256 == 0 else CHUNK


def _gla_fwd(z, lb, s0, d, name):
    n = z.shape[0]
    blk = _gla_block(n)
    nb, npb = n // blk, blk // CHUNK
    reverse = d == 1
    last = 0 if reverse else CHUNK - 1
    order = list(range(npb))[::-1] if reverse else list(range(npb))

    def bmap(i):
        return nb - 1 - i if reverse else i

    hp = GLA_HEADS_PER_STEP
    hw = hp * DH
    units = [(hh, cidx) for hh in range(hp) for cidx in order]

    def body(zq_ref, zf_ref, zv_ref, lb_ref, s0_ref, o_ref, ss_ref, sf_ref, st):
        i = pl.program_id(1)

        @pl.when(i == 0)
        def _():
            st[...] = s0_ref[...]

        mask = _tri(reverse)
        tri_b = jnp.where(mask, 1.0, 0.0).astype(BF16)
        feat = {}
        for u in units:
            hh, cidx = u
            rows, cols = pl.ds(cidx * CHUNK, CHUNK), pl.ds(hh * DH, DH)
            q, _, f, _ = _gla_features(zq_ref[rows, cols], zf_ref[rows, cols], lb_ref[d:d + 1, cols])
            feat[u] = (q, 1.0 - f, jnp.log(f), zv_ref[rows, cols].astype(BF16))
        dec = {u: _cum_f32(tri_b, feat[u][2]) for u in units}
        ops = {}
        for u in units:
            q, k, _, vb = feat[u]
            g = dec[u]
            gl = g[last:last + 1, :]
            ops[u] = ((q * jnp.exp(g)).astype(BF16), (k * jnp.exp(-g)).astype(BF16),
                      (k * jnp.exp(gl - g)).astype(BF16), jnp.exp(gl), vb)
        att = {u: jnp.where(mask, _dot_nt(ops[u][0], ops[u][1]), 0.0).astype(BF16) for u in units}
        upd = {u: _dot_tn(ops[u][4], ops[u][2]) for u in units}
        intra = {u: _dot(att[u], ops[u][4]) for u in units}
        s_in = {}
        for hh in range(hp):
            s = st[hh]
            for cidx in order:
                s_in[(hh, cidx)] = s
                s = s * ops[(hh, cidx)][3] + upd[(hh, cidx)]
            st[hh] = s
            sf_ref[hh] = s
        for u in units:
            hh, cidx = u
            rows, cols = pl.ds(cidx * CHUNK, CHUNK), pl.ds(hh * DH, DH)
            o_ref[rows, cols] = intra[u] + _dot_nt(ops[u][0], s_in[u].astype(BF16))
            ss_ref[hh, cidx] = s_in[u]

    def col(g):
        return lambda h, i: (bmap(i), g * (NH // hp) + h)

    return pl.pallas_call(
        body, name=name, grid=(NH // hp, nb),
        out_shape=[jax.ShapeDtypeStruct((n, D), F32), jax.ShapeDtypeStruct((NH, n // CHUNK, DH, DH), F32),
                   jax.ShapeDtypeStruct((NH, DH, DH), F32)],
        in_specs=[pl.BlockSpec((blk, hw), col(0)), pl.BlockSpec((blk, hw), col(1 + d)),
                  pl.BlockSpec((blk, hw), col(3)), pl.BlockSpec((8, hw), lambda h, i: (0, h)),
                  pl.BlockSpec((hp, DH, DH), lambda h, i: (h, 0, 0))],
        out_specs=[pl.BlockSpec((blk, hw), lambda h, i: (bmap(i), h)),
                   pl.BlockSpec((hp, npb, DH, DH), lambda h, i: (h, bmap(i), 0, 0)),
                   pl.BlockSpec((hp, DH, DH), lambda h, i: (h, 0, 0))],
        scratch_shapes=[pltpu.VMEM((hp, DH, DH), F32)],
    )(z, z, z, lb, s0)


def _gla_bwd(z, lb, s_start, do, ds_fin, acc_q, acc_v, d, name, f_dtype=F32, into=None, side=(), side_splits=()):
    n = z.shape[0]
    blk = _gla_block(n)
    nb, npb = n // blk, blk // CHUNK
    reverse = d == 1
    last = 0 if reverse else CHUNK - 1
    order = list(range(npb)) if reverse else list(range(npb))[::-1]
    has_do = do is not None
    has_acc = acc_q is not None
    fused = into is not None
    assert not fused or d == 1
    ns = len(side)
    assert not (fused and ns)
    piece_shapes, piece_of = _pieces(side, side_splits)
    hp = NH if fused else GLA_HEADS_PER_STEP
    hw = hp * DH
    units = [(hh, cidx) for hh in range(hp) for cidx in order]

    def bmap(i):
        return i if reverse else nb - 1 - i

    def body(*refs):
        zq_ref, zf_ref, zv_ref, lb_ref, ss_ref, dsf_ref = refs[:6]
        pos = 6
        do_ref = aq_ref = av_ref = None
        if has_do:
            do_ref = refs[pos]
            pos += 1
        if has_acc:
            aq_ref, av_ref = refs[pos], refs[pos + 1]
            pos += 2
        if fused:
            other_ref = refs[pos + 1]
            dz_ref, dlb_ref, ds0_ref, dst = refs[pos + 2:]
            dz_ref[:, D:2 * D] = other_ref[...]
        else:
            side_in = refs[pos:pos + ns]
            dzq_ref, dzf_ref, dzv_ref, dlb_ref, ds0_ref = refs[pos + ns:pos + ns + 5]
            side_out = refs[pos + ns + 5:pos + 2 * ns + 5]
            dst = refs[pos + 2 * ns + 5]
        i = pl.program_id(1)
        if ns:
            side_start, side_finish = _exchange(side_in, side_out, refs[pos + 2 * ns + 6:], piece_of)

            @pl.when((pl.program_id(0) == 0) & (i == 0))
            def _():
                side_start()

        @pl.when(i == 0)
        def _():
            dst[...] = dsf_ref[...]
            dlb_ref[...] = jnp.zeros_like(dlb_ref)

        mask = _tri(reverse)
        tri_b = jnp.where(mask, 1.0, 0.0).astype(BF16)
        tri_t = jnp.where(_tri(not reverse), 1.0, 0.0).astype(BF16)

        def where(u):
            return pl.ds(u[1] * CHUNK, CHUNK), pl.ds(u[0] * DH, DH)

        feat = {}
        for u in units:
            rows, cols = where(u)
            zq, zf = zq_ref[rows, cols], zf_ref[rows, cols]
            lbv = lb_ref[d:d + 1, cols]
            q, sq, f, sf = _gla_features(zq, zf, lbv)
            feat[u] = dict(zq=zq, q=q, sq=sq, f=f, sf=sf, lbv=lbv, k=1.0 - f, vb=zv_ref[rows, cols].astype(BF16))
        dec = {u: _cum_f32(tri_b, jnp.log(feat[u]["f"])) for u in units}
        for u in units:
            w = feat[u]
            g = dec[u]
            gl = g[last:last + 1, :]
            w["eg"], w["egi"], w["ege"], w["egl"] = jnp.exp(g), jnp.exp(-g), jnp.exp(gl - g), jnp.exp(gl)
            w["qd"], w["ki"], w["ke"] = w["q"] * w["eg"], w["k"] * w["egi"], w["k"] * w["ege"]
            w["qdb"], w["kib"], w["keb"] = w["qd"].astype(BF16), w["ki"].astype(BF16), w["ke"].astype(BF16)
            w["s_in"] = ss_ref[u[0], u[1]]
        if has_do:
            for u in units:
                w = feat[u]
                rows, cols = where(u)
                w["dob"] = do_ref[rows, cols].astype(BF16)
            for u in units:
                w = feat[u]
                w["a"] = jnp.where(mask, _dot_nt(w["qdb"], w["kib"]), 0.0).astype(BF16)
                w["da"] = jnp.where(mask, _dot_nt(w["dob"], w["vb"]), 0.0).astype(BF16)
                w["m"] = _dot_tn(w["dob"], w["qdb"])
        for hh in range(hp):
            ds = dst[hh]
            for cidx in order:
                w = feat[(hh, cidx)]
                w["ds"] = ds
                ds = ds * w["egl"]
                if has_do:
                    ds = ds + w["m"]
            dst[hh] = ds
            ds0_ref[hh] = ds
        for u in units:
            w = feat[u]
            dsb = w["ds"].astype(BF16)
            w["dke"] = _dot(w["vb"], dsb)
            w["dv"] = _dot_nt(w["keb"], dsb)
            if has_do:
                w["dv"] = w["dv"] + _dot_tn(w["a"], w["dob"])
                w["dqd"] = _dot(w["da"], w["kib"]) + _dot(w["dob"], w["s_in"].astype(BF16))
                w["dki"] = _dot_tn(w["da"], w["qdb"])
        for u in units:
            w = feat[u]
            dkeke = w["dke"] * w["ke"]
            w["dgl"] = (w["egl"] * jnp.sum(w["s_in"] * w["ds"], axis=0, keepdims=True)
                        + jnp.sum(dkeke, axis=0, keepdims=True))
            dg = -dkeke
            dk = w["dke"] * w["ege"]
            if has_do:
                dg = dg + w["dqd"] * w["qd"] - w["dki"] * w["ki"]
                dk = dk + w["dki"] * w["egi"]
            w["dg"], w["dk"] = dg, dk
        dlf = {u: _cum_f32(tri_t, feat[u]["dg"]) for u in units}
        for u in units:
            w = feat[u]
            rows, cols = where(u)
            df = (dlf[u] + w["dgl"]) / w["f"] - w["dk"]
            sf = w["sf"]
            dzf = df * (1.0 - w["lbv"]) * sf * (1.0 - sf)
            dlb_ref[0:1, cols] += jnp.sum(df * (1.0 - sf), axis=0, keepdims=True)
            if has_do:
                dzq = w["dqd"] * w["eg"] * (Q_SCALE * _dsilu(w["zq"], w["sq"]))
            else:
                dzq = jnp.zeros((CHUNK, DH), F32)
            dv = w["dv"]
            if has_acc:
                dzq = dzq + aq_ref[rows, cols]
                dv = dv + av_ref[rows, cols]
            if fused:
                lane = u[0] * DH
                dz_ref[rows, pl.ds(lane, DH)] = dzq.astype(BF16)
                dz_ref[rows, pl.ds(2 * D + lane, DH)] = dzf.astype(BF16)
                dz_ref[rows, pl.ds(3 * D + lane, DH)] = dv.astype(BF16)
            else:
                dzq_ref[rows, cols] = dzq
                dzf_ref[rows, cols] = dzf.astype(f_dtype)
                dzv_ref[rows, cols] = dv

        if ns:
            @pl.when((pl.program_id(0) == NH // hp - 1) & (i == nb - 1))
            def _():
                side_finish()

    def col(g):
        return lambda h, i: (bmap(i), g * (NH // hp) + h)

    tok = pl.BlockSpec((blk, hw), lambda h, i: (bmap(i), h))
    state = pl.BlockSpec((hp, DH, DH), lambda h, i: (h, 0, 0))
    in_specs = [pl.BlockSpec((blk, hw), col(0)), pl.BlockSpec((blk, hw), col(1 + d)), pl.BlockSpec((blk, hw), col(3)),
                pl.BlockSpec((8, hw), lambda h, i: (0, h)),
                pl.BlockSpec((hp, npb, DH, DH), lambda h, i: (h, bmap(i), 0, 0)), state]
    args = [z, z, z, lb, s_start, ds_fin]
    if has_do:
        in_specs.append(tok)
        args.append(do)
    if has_acc:
        in_specs += [tok, tok]
        args += [acc_q, acc_v]
    tail_shape = [jax.ShapeDtypeStruct((8, D), F32), jax.ShapeDtypeStruct((NH, DH, DH), F32)]
    tail_specs = [pl.BlockSpec((8, hw), lambda h, i: (0, h)), state]
    if fused:
        buf, other = into
        aliases = {len(args): 0}
        in_specs += [ANY, tok]
        args += [buf, other]
        out_shape = [jax.ShapeDtypeStruct(buf.shape, buf.dtype)] + tail_shape
        out_specs = [pl.BlockSpec((blk, 4 * D), lambda h, i: (bmap(i), 0))] + tail_specs
    else:
        aliases = {}
        in_specs += [ANY] * ns
        args += list(side)
        out_shape = [jax.ShapeDtypeStruct((n, D), F32), jax.ShapeDtypeStruct((n, D), f_dtype),
                     jax.ShapeDtypeStruct((n, D), F32)] + tail_shape
        out_shape += [jax.ShapeDtypeStruct((NDEV,) + piece_shapes[a], side[a].dtype) for a in range(ns)]
        out_specs = [tok, tok, tok] + tail_specs + [ANY] * ns
    return pl.pallas_call(
        body, name=name, grid=(NH // hp, nb), out_shape=out_shape, in_specs=in_specs, out_specs=out_specs,
        input_output_aliases=aliases,
        scratch_shapes=[pltpu.VMEM((hp, DH, DH), F32)] + (_exchange_sems(ns) if ns else []),
    )(*args)


def _shift(t, s, fill, down):
    n = t.shape[0]
    rows = lax.broadcasted_iota(jnp.int32, t.shape, 0)
    if down:
        return jnp.where(rows >= s, pltpu.roll(t, s, 0), fill)
    return jnp.where(rows < n - s, pltpu.roll(t, n - s, 0), fill)


SUBLANES = 8
LRU_SAVED = 5


def _chain_scan(a, b, h_in, down):
    n = a.shape[0]
    ng = n // SUBLANES
    rows = lax.broadcasted_iota(jnp.int32, (SUBLANES, a.shape[1]), 0)
    local = []
    for g in range(ng):
        aa, bb = a[g * SUBLANES:(g + 1) * SUBLANES], b[g * SUBLANES:(g + 1) * SUBLANES]
        for s in (1, 2, 4):
            if down:
                keep, amt = rows >= s, s
            else:
                keep, amt = rows < SUBLANES - s, SUBLANES - s
            bb = bb + aa * jnp.where(keep, pltpu.roll(bb, amt, 0), 0.0)
            aa = aa * jnp.where(keep, pltpu.roll(aa, amt, 0), 1.0)
        local.append((aa, bb))
    out = [None] * ng
    carry = h_in
    for g in (range(ng) if down else range(ng - 1, -1, -1)):
        aa, bb = local[g]
        hg = bb + aa * carry
        out[g] = hg
        carry = hg[SUBLANES - 1:SUBLANES] if down else hg[0:1]
    return (jnp.concatenate(out, axis=0) if ng > 1 else out[0]), carry


def _conv_taps(xv):
    return (_shift(xv, 1, 0.0, True), xv, _shift(xv, 1, 0.0, False), _shift(xv, 2, 0.0, False))


def _conv(taps, cw, cb):
    return cb + cw[0:1, :] * taps[0] + cw[1:2, :] * taps[1] + cw[2:3, :] * taps[2] + cw[3:4, :] * taps[3]


def _neg_expm1(t):
    series = -t * (1.0 + t * (0.5 + t * (1.0 / 6.0 + t * (1.0 / 24.0 + t * (1.0 / 120.0)))))
    return jnp.where(t > -0.1, series, 1.0 - jnp.exp(t))


def _lru_gates(xc, wr, br, wi, bi, lam):
    xcb = xc.astype(BF16)
    r = _sigmoid(_dot(xcb, wr) + br)
    gi = _sigmoid(_dot(xcb, wi) + bi)
    sp = jnp.maximum(-lam, 0.0) + jnp.log(1.0 + jnp.exp(-jnp.abs(lam)))
    la = -RG_C * r * sp
    a = jnp.exp(la)
    mult = jnp.sqrt(_neg_expm1(2.0 * la))
    return xcb, r, gi, sp, a, mult


def _lru_fwd(xin, blk, cw, cb, wr, br, wi, bi, lam, h0, acc_h, d, name, side=()):
    n = xin.shape[0]
    nb = n // blk
    reverse = d == 1
    down = not reverse
    has_acc = acc_h is not None
    ns = len(side)

    def bmap(i):
        return nb - 1 - i if reverse else i

    def body(*refs):
        x_ref, cw_ref, cb_ref, wr_ref, br_ref, wi_ref, bi_ref, lam_ref, h0_ref = refs[:9]
        pos = 9
        acc_ref = refs[pos] if has_acc else None
        pos += int(has_acc)
        side_in = refs[pos:pos + ns]
        pos += ns
        h_ref, hin_ref, hfin_ref, sav_ref = refs[pos:pos + 4]
        pos += 4
        hsum_ref = refs[pos] if has_acc else None
        pos += int(has_acc)
        side_out = refs[pos:pos + ns]
        carry = refs[pos + ns]
        i = pl.program_id(0)
        if ns:
            side_start, side_finish = _exchange(side_in, side_out, refs[pos + ns + 1:])

            @pl.when(i == 0)
            def _():
                side_start()

        @pl.when(i == 0)
        def _():
            carry[...] = h0_ref[...]

        for g in range(NH):
            cols = pl.ds(g * DH, DH)
            xc = _conv(_conv_taps(x_ref[:, cols]), cw_ref[:, cols], cb_ref[:, cols])
            _, r, gi, _, a, mult = _lru_gates(xc, wr_ref[g], br_ref[:, cols], wi_ref[g], bi_ref[:, cols],
                                              lam_ref[:, cols])
            for slot, val in enumerate((xc, r, gi, a, mult)):
                sav_ref[slot, :, cols] = val
            hin = carry[:, cols]
            h, h_last = _chain_scan(a, mult * gi * xc, hin, down)
            h_ref[:, cols] = h
            if has_acc:
                hsum_ref[:, cols] = h + acc_ref[:, cols]
            hin_ref[0, :, cols] = hin
            carry[:, cols] = h_last
            hfin_ref[:, cols] = h_last

        if ns:
            @pl.when(i == nb - 1)
            def _():
                side_finish()

    vec = pl.BlockSpec((1, D), lambda i: (0, 0))
    wsp = pl.BlockSpec((NH, DH, DH), lambda i: (0, 0, 0))
    tok = pl.BlockSpec((blk, D), lambda i: (bmap(i), 0))
    in_specs = [tok, pl.BlockSpec((4, D), lambda i: (0, 0)), vec, wsp, vec, wsp, vec, vec, vec]
    args = [xin, cw, cb, wr, br, wi, bi, lam, h0]
    out_shape = [jax.ShapeDtypeStruct((n, D), F32), jax.ShapeDtypeStruct((nb, 1, D), F32),
                 jax.ShapeDtypeStruct((1, D), F32), jax.ShapeDtypeStruct((LRU_SAVED, n, D), F32)]
    out_specs = [tok, pl.BlockSpec((1, 1, D), lambda i: (bmap(i), 0, 0)), vec,
                 pl.BlockSpec((LRU_SAVED, blk, D), lambda i: (0, bmap(i), 0))]
    if has_acc:
        in_specs.append(tok)
        args.append(acc_h)
        out_shape.append(jax.ShapeDtypeStruct((n, D), F32))
        out_specs.append(tok)
    in_specs += [ANY] * ns
    args += list(side)
    out_shape += [jax.ShapeDtypeStruct((NDEV,) + t.shape, t.dtype) for t in side]
    out_specs += [ANY] * ns
    return pl.pallas_call(
        body, name=name, grid=(nb,), out_shape=out_shape, in_specs=in_specs, out_specs=out_specs,
        scratch_shapes=[pltpu.VMEM((1, D), F32)] + (_exchange_sems(ns) if ns else []),
    )(*args)


def _lru_bwd(xin, blk, cw, wr, wi, lam, sav, h, hin, dh, cg_fin, acc_dx, init, d, name):
    n = xin.shape[0]
    nb = n // blk
    reverse = d == 1
    down = not reverse
    first = blk - 1 if reverse else 0
    has_dh = dh is not None
    has_acc = acc_dx is not None
    has_init = init is not None

    def bmap(i):
        return i if reverse else nb - 1 - i

    def body(*refs):
        (x_ref, cw_ref, wr_ref, wi_ref, lam_ref, sav_ref, h_ref, hin_ref, cgf_ref) = refs[:9]
        pos = 9
        dh_ref = acc_ref = None
        iwr_ref = iwi_ref = ivec_ref = None
        if has_dh:
            dh_ref = refs[pos]
            pos += 1
        if has_acc:
            acc_ref = refs[pos]
            pos += 1
        if has_init:
            iwr_ref, iwi_ref, ivec_ref = refs[pos:pos + 3]
            pos += 3
        dx_ref, dwr_ref, dwi_ref, vec_ref, cg0_ref, carry = refs[pos:]
        i = pl.program_id(0)

        @pl.when(i == 0)
        def _():
            carry[...] = cgf_ref[...]
            if has_init:
                dwr_ref[...] = iwr_ref[...]
                dwi_ref[...] = iwi_ref[...]
                vec_ref[...] = ivec_ref[...]
            else:
                dwr_ref[...] = jnp.zeros_like(dwr_ref)
                dwi_ref[...] = jnp.zeros_like(dwi_ref)
                vec_ref[...] = jnp.zeros_like(vec_ref)

        for g in range(NH):
            cols = pl.ds(g * DH, DH)
            cwv = cw_ref[:, cols]
            lam_v = lam_ref[:, cols]
            taps = _conv_taps(x_ref[:, cols])
            wr_g, wi_g = wr_ref[g], wi_ref[g]
            xc, r, gi, a, mult = (sav_ref[slot, :, cols] for slot in range(LRU_SAVED))
            xcb = xc.astype(BF16)
            sp = jnp.maximum(-lam_v, 0.0) + jnp.log(1.0 + jnp.exp(-jnp.abs(lam_v)))
            hprev = _shift(h_ref[:, cols], 1, hin_ref[0, :, cols], down)
            a_next = _shift(a, 1, 1.0, not down)
            dhv = dh_ref[:, cols] if has_dh else jnp.zeros_like(a)
            e, _ = _chain_scan(a_next, dhv, carry[:, cols], not down)
            cg = a[first:first + 1, :] * e[first:first + 1, :]
            carry[:, cols] = cg
            cg0_ref[:, cols] = cg
            da = e * hprev
            emult = e * mult
            dgi = emult * xc
            dxc = emult * gi
            dla = da * a - (e * gi * xc) * (a * a) / mult
            dr = dla * (-RG_C * sp)
            sneg = 1.0 - _sigmoid(lam_v)
            dpr = dr * r * (1.0 - r)
            dpi = dgi * gi * (1.0 - gi)
            dprb, dpib = dpr.astype(BF16), dpi.astype(BF16)
            dxc = dxc + _dot_nt(dprb, wr_g) + _dot_nt(dpib, wi_g)
            dwr_ref[g] += _dot_tn(xcb, dprb)
            dwi_ref[g] += _dot_tn(xcb, dpib)
            dx = (cwv[0:1, :] * _shift(dxc, 1, 0.0, False) + cwv[1:2, :] * dxc
                  + cwv[2:3, :] * _shift(dxc, 1, 0.0, True) + cwv[3:4, :] * _shift(dxc, 2, 0.0, True))
            if has_acc:
                dx = dx + acc_ref[:, cols]
            dx_ref[:, cols] = dx
            vec_ref[0:1, cols] += jnp.sum(dpr, axis=0, keepdims=True)
            vec_ref[1:2, cols] += jnp.sum(dpi, axis=0, keepdims=True)
            vec_ref[2:3, cols] += jnp.sum(dla * r, axis=0, keepdims=True) * (RG_C * sneg)
            vec_ref[3:4, cols] += jnp.sum(dxc, axis=0, keepdims=True)
            for kk in range(4):
                vec_ref[4 + kk:5 + kk, cols] += jnp.sum(dxc * taps[kk], axis=0, keepdims=True)

    vec = pl.BlockSpec((1, D), lambda i: (0, 0))
    wsp = pl.BlockSpec((NH, DH, DH), lambda i: (0, 0, 0))
    tok = pl.BlockSpec((blk, D), lambda i: (bmap(i), 0))
    vec16 = pl.BlockSpec((16, D), lambda i: (0, 0))
    in_specs = [tok, pl.BlockSpec((4, D), lambda i: (0, 0)), wsp, wsp, vec,
                pl.BlockSpec((LRU_SAVED, blk, D), lambda i: (0, bmap(i), 0)), tok,
                pl.BlockSpec((1, 1, D), lambda i: (bmap(i), 0, 0)), vec]
    args = [xin, cw, wr, wi, lam, sav, h, hin, cg_fin]
    if has_dh:
        in_specs.append(tok)
        args.append(dh)
    if has_acc:
        in_specs.append(tok)
        args.append(acc_dx)
    if has_init:
        in_specs += [wsp, wsp, vec16]
        args += list(init)
    return pl.pallas_call(
        body, name=name, grid=(nb,),
        out_shape=[jax.ShapeDtypeStruct((n, D), F32), jax.ShapeDtypeStruct((NH, DH, DH), F32),
                   jax.ShapeDtypeStruct((NH, DH, DH), F32), jax.ShapeDtypeStruct((16, D), F32),
                   jax.ShapeDtypeStruct((1, D), F32)],
        in_specs=in_specs, out_specs=[tok, wsp, wsp, vec16, vec],
        scratch_shapes=[pltpu.VMEM((1, D), F32)],
    )(*args)


def _merge(z, o_f, o_b, hx, xin, tgt, mod, gn, p_a, p_b, w_out, ln_g, ln_b):
    n = xin.shape[0]
    tm = _row_tile(n, 128)

    def body(z4_ref, z6_ref, z7_ref, z8_ref, of_ref, ob_ref, hx_ref, x_ref, t_ref, mod_ref, gn_ref,
             pa_ref, pb_ref, wo_ref, lg_ref, lnb_ref,
             dr_ref, do_ref, dhx_ref, dz_ref, oa_o, obb_o, y_o, dya_o, dyb_o, dout_o, vec_ref):
        @pl.when(pl.program_id(0) == 0)
        def _():
            vec_ref[...] = jnp.zeros_like(vec_ref)

        gt = mod_ref[0:1, 2 * D:3 * D]
        gnv = gn_ref[...]
        o = of_ref[...] + ob_ref[...]
        rs = jnp.concatenate(
            [jnp.broadcast_to(lax.rsqrt(jnp.mean(jnp.square(o[:, h * DH:(h + 1) * DH]), axis=1, keepdims=True)
                                        + RMS_EPS), (tm, DH)) for h in range(NH)], axis=1)
        nrm = o * rs
        rn = nrm * gnv
        z4, z6, z7, z8 = z4_ref[...], z6_ref[...], z7_ref[...], z8_ref[...]
        s4, s6, s7, s8 = _sigmoid(z4), _sigmoid(z6), _sigmoid(z7), _sigmoid(z8)
        sg4, sg6 = z4 * s4, z6 * s6
        hxv = hx_ref[...]
        oa = (rn * sg4).astype(BF16)
        obb = (hxv * sg6).astype(BF16)
        ya = _dot(oa, pa_ref[...])
        yb = _dot(obb, pb_ref[...])
        y = (s7 * ya + s8 * yb).astype(BF16)
        out = _dot(y, wo_ref[...])
        xv = x_ref[...]
        rr = ALPHA * xv + gt * out
        mu = jnp.mean(rr, axis=1, keepdims=True)
        cen = rr - mu
        rstd = lax.rsqrt(jnp.mean(cen * cen, axis=1, keepdims=True) + LN_EPS)
        xhat = cen * rstd
        lg = lg_ref[...]
        err = xhat * lg + lnb_ref[...] - t_ref[...]
        loss_rows = jnp.sum(err * err, axis=1, keepdims=True)
        dxn = err * (1.0 / D)
        dxh = dxn * lg
        dr = rstd * (dxh - jnp.mean(dxh, axis=1, keepdims=True)
                     - xhat * jnp.mean(dxh * xhat, axis=1, keepdims=True))
        dout = (dr * gt).astype(BF16)
        dy = _dot_nt(dout, wo_ref[...])
        dya = (dy * s7).astype(BF16)
        dyb = (dy * s8).astype(BF16)
        doa = _dot_nt(dya, pa_ref[...])
        dob = _dot_nt(dyb, pb_ref[...])
        drn = doa * sg4
        dn = drn * gnv
        dnn = dn * nrm
        corr = jnp.concatenate(
            [jnp.broadcast_to(jnp.mean(dnn[:, h * DH:(h + 1) * DH], axis=1, keepdims=True), (tm, DH))
             for h in range(NH)], axis=1)
        dr_ref[...] = dr
        do_ref[...] = rs * (dn - nrm * corr)
        dhx_ref[...] = dob * sg6
        dz_ref[:, 0:4 * D] = jnp.zeros((tm, 4 * D), BF16)
        dz_ref[:, 4 * D:5 * D] = (doa * rn * _dsilu(z4, s4)).astype(BF16)
        dz_ref[:, 5 * D:6 * D] = jnp.zeros((tm, D), BF16)
        dz_ref[:, 6 * D:7 * D] = (dob * hxv * _dsilu(z6, s6)).astype(BF16)
        dz_ref[:, 7 * D:8 * D] = (dy * ya * s7 * (1.0 - s7)).astype(BF16)
        dz_ref[:, 8 * D:9 * D] = (dy * yb * s8 * (1.0 - s8)).astype(BF16)
        oa_o[...] = oa
        obb_o[...] = obb
        y_o[...] = y
        dya_o[...] = dya
        dyb_o[...] = dyb
        dout_o[...] = dout
        vec_ref[0:1, :] += jnp.sum(dr * out, axis=0, keepdims=True)
        vec_ref[1:2, :] += jnp.sum(dxn * xhat, axis=0, keepdims=True)
        vec_ref[2:3, :] += jnp.sum(dxn, axis=0, keepdims=True)
        vec_ref[3:4, :] += jnp.sum(drn * nrm, axis=0, keepdims=True)
        vec_ref[4:5, :] += jnp.broadcast_to(jnp.sum(loss_rows, axis=0, keepdims=True) * (0.5 / D), (1, D))

    def grp(g):
        return pl.BlockSpec((tm, D), lambda i: (i, g))

    tok = pl.BlockSpec((tm, D), lambda i: (i, 0))
    vec = pl.BlockSpec((1, D), lambda i: (0, 0))
    wsp = pl.BlockSpec((D, D), lambda i: (0, 0))
    return pl.pallas_call(
        body, name="merge", grid=(n // tm,),
        out_shape=[jax.ShapeDtypeStruct((n, D), F32)] * 3
        + [jax.ShapeDtypeStruct((n, NGRP * D), BF16)]
        + [jax.ShapeDtypeStruct((n, D), BF16)] * 6 + [jax.ShapeDtypeStruct((8, D), F32)],
        in_specs=[grp(4), grp(6), grp(7), grp(8), tok, tok, tok, tok, tok,
                  pl.BlockSpec((8, 3 * D), lambda i: (0, 0)), vec, wsp, wsp, wsp, vec, vec],
        out_specs=[tok, tok, tok, pl.BlockSpec((tm, NGRP * D), lambda i: (i, 0))] + [tok] * 6
        + [pl.BlockSpec((8, D), lambda i: (0, 0))],
    )(z, z, z, z, o_f, o_b, hx, xin, tgt, mod, gn, p_a, p_b, w_out, ln_g, ln_b)


def _wmod_grad(c_t, cctx_t, dmx_loc, dmc_loc, name):
    n = dmx_loc.shape[1]

    def body(ct_ref, cc_ref, dmx_ref, dmc_ref, o_ref):
        ct = ct_ref[...]
        sct = ct * _sigmoid(ct)
        cc = cc_ref[...]
        scc = cc * _sigmoid(cc)
        dmc = dmc_ref[0:1, :]
        for b in range(1, NDEV):
            dmc = dmc + dmc_ref[b:b + 1, :]
        acc = scc * dmc
        for b in range(NDEV):
            acc = acc + sct[:, b:b + 1] * dmx_ref[b:b + 1, :]
        o_ref[...] = acc

    return pl.pallas_call(body, name=name, out_shape=jax.ShapeDtypeStruct((D, n), F32))(c_t, cctx_t, dmx_loc, dmc_loc)


PACK_ROWS = 40


def _finalize_small(g_pack, lb, w_mod_full, params):
    npar = len(params)

    def body(*refs):
        gp_ref, lb_ref, wm_ref = refs[:3]
        wmv = refs[3:3 + 3 * npar]
        loss_ref = refs[3 + 3 * npar]
        g_refs = refs[4 + 3 * npar:4 + 4 * npar]
        upd = refs[4 + 4 * npar:4 + 7 * npar]
        tot = refs[-1]
        acc = gp_ref[0]
        for k in range(1, NDEV):
            acc = acc + gp_ref[k]
        tot[...] = acc
        mine = pl.ds(pl.multiple_of(_my_index() * DH, DH), DH)
        (g_cctx, g_bmod, g_bin, g_lbl, g_norm, g_cw, g_cb, g_br, g_bi, g_lam, g_lng, g_lnb) = g_refs

        loss_ref[...] = jnp.broadcast_to(tot[36:37, 0:DH], (8, DH))
        for k in range(3):
            g_bmod[:, k * D:(k + 1) * D] = tot[k:k + 1, :] + tot[3 + k:4 + k, :]
        dmc = jnp.concatenate([tot[3:4, :], tot[4:5, :], tot[5:6, :]], axis=1)
        cv = wmv[0][...]
        proj = _dot_nt(jnp.broadcast_to(dmc, (8, 3 * D)).astype(BF16), wm_ref[...])
        g_cctx[...] = proj[0:1, :] * _dsilu(cv, _sigmoid(cv))
        for k in range(NGRP):
            g_bin[:, k * D:(k + 1) * D] = tot[6 + k:7 + k, :]
        nrm = tot[15:16, 0:DH]
        for h in range(1, NH):
            nrm = nrm + tot[15:16, h * DH:(h + 1) * DH]
        g_norm[...] = nrm
        g_lng[...] = tot[16:17, :]
        g_lnb[...] = tot[17:18, :]
        g_cb[...] = tot[21:22, :] + tot[29:30, :]
        g_cw[0] = tot[22:26, mine] + tot[30:34, mine]
        for ref, row in ((g_br, 18), (g_bi, 19), (g_lam, 20)):
            ref[0, 0:1, :] = tot[row:row + 1, mine]
            ref[0, 1:2, :] = tot[row + 8:row + 9, mine]
        lbl = lb_ref[0:2, mine]
        dl0 = tot[34:36, mine] * lbl * (1.0 - lbl)
        g_lbl[0] = dl0
        g_lbl[1] = -dl0
        for p in range(npar):
            d, mm, vv = _adam_math(g_refs[p][...], wmv[3 * p][...], wmv[3 * p + 1][...], wmv[3 * p + 2][...])
            upd[3 * p][...] = d
            upd[3 * p + 1][...] = mm
            upd[3 * p + 2][...] = vv

    flat = [t for wmv in params for t in wmv]
    shapes = [jax.ShapeDtypeStruct(wmv[0].shape, F32) for wmv in params]
    outs = pl.pallas_call(
        body, name="finalize_small",
        out_shape=[jax.ShapeDtypeStruct((8, DH), F32)] + shapes + [s for s in shapes for _ in range(3)],
        scratch_shapes=[pltpu.VMEM((PACK_ROWS, D), F32)],
    )(g_pack, lb, w_mod_full, *flat)
    grads = list(outs[1:1 + npar])
    upd = [tuple(outs[1 + npar + 3 * p:4 + npar + 3 * p]) for p in range(npar)]
    return outs[0], grads, upd


def _to_colmajor(t, rows):
    return t.reshape(rows, GRID_W, D).transpose(1, 0, 2).reshape(rows * GRID_W, D)


def _to_raster(t, rows):
    return t.reshape(GRID_W, rows, D).transpose(1, 0, 2).reshape(rows * GRID_W, D)


def _local_cols(t, me, width):
    return lax.dynamic_slice_in_dim(t, me * width, width, axis=t.ndim - 1)


def kernel(x, c, ctx, c_ctx, w_mod, b_mod, w_in, b_in, lb_logits, norm_a_g, conv_w, conv_b, w_r, b_r, w_i, b_i, lam, p_a, p_b, w_out, ln_g, ln_b, loss_target, m_c_ctx, m_w_mod, m_b_mod, m_w_in, m_b_in, m_lb_logits, m_norm_a_g, m_conv_w, m_conv_b, m_w_r, m_b_r, m_w_i, m_b_i, m_lam, m_p_a, m_p_b, m_w_out, m_ln_g, m_ln_b, v_c_ctx, v_w_mod, v_b_mod, v_w_in, v_b_in, v_lb_logits, v_norm_a_g, v_conv_w, v_conv_b, v_w_r, v_b_r, v_w_i, v_b_i, v_lam, v_p_a, v_p_b, v_w_out, v_ln_g, v_ln_b):
    me = _my_index()
    xs, cs, tgt = x[0], ctx[0], loss_target[0]
    t_len, c_len = xs.shape[0], cs.shape[0]
    rows = t_len // GRID_W
    wcols = w_in.shape[2]
    mcols = w_mod.shape[2]

    small = jnp.concatenate([lb_logits.reshape(4, DH), conv_w[0], b_r[0], b_i[0], lam[0], jnp.zeros((2, DH), F32),
                             c.reshape(8, DH)], axis=0)
    g_small, g_wmod = _all_gather([small, w_mod[0].astype(BF16)], "gather_params")

    def full_rows(lo, hi):
        return g_small[:, lo:hi, :].transpose(1, 0, 2).reshape(hi - lo, D)

    lbl_f, cw_f, br_f, bi_f, lam_f = full_rows(0, 4), full_rows(4, 8), full_rows(8, 10), full_rows(10, 12), full_rows(12, 14)
    c_all = g_small[:, 16:24, :].reshape(NDEV, D)
    w_mod_f = g_wmod.transpose(1, 0, 2).reshape(D, 3 * D)
    w_r_b, w_i_b = w_r[0].astype(BF16), w_i[0].astype(BF16)

    cc = jnp.concatenate([c.reshape(1, D), c_ctx.reshape(1, D), jnp.zeros((6, D), F32)], axis=0)
    lbl_p = jnp.concatenate([lbl_f.reshape(2, 2, D), jnp.zeros((2, 6, D), F32)], axis=1)
    mod, lb = _prep(cc, w_mod_f, b_mod, lbl_p)
    u_x = _modulate(xs, mod, 0, "modulate_x")
    u_c = _modulate(cs, mod, 1, "modulate_c")
    z_x, w_in_f = _inproj_gather(u_x, w_in[0].astype(BF16), b_in, "inproj_gather")
    z_c = _mm_bias(u_c, w_in_f, b_in, "inproj_c")

    zero_s = jnp.zeros((NH, DH, DH), F32)
    zero_v = jnp.zeros((1, D), F32)
    gla = {}
    for d in (0, 1):
        _, ssc, sfc = _gla_fwd(z_c, lb, zero_s, d, f"gla_fwd_c{d}")
        o_d, ssx, _ = _gla_fwd(z_x, lb, sfc, d, f"gla_fwd_x{d}")
        gla[d] = (ssc, ssx, o_d)

    x5_c = z_c[:, 5 * D:6 * D]
    x5_x = _to_colmajor(z_x[:, 5 * D:6 * D], rows)
    cb2 = conv_b.reshape(1, D)
    lru = {}
    out_w = [p_a[0].astype(BF16), p_b[0].astype(BF16), w_out[0].astype(BF16)]
    for d in (0, 1):
        prm = (cw_f, cb2, w_r_b[d], br_f[d:d + 1], w_i_b[d], bi_f[d:d + 1], lam_f[d:d + 1])
        h_c, hin_c, hfin_c, sav_c = _lru_fwd(x5_c, c_len, *prm, zero_v, None, d, f"lru_fwd_c{d}")
        h_x, hin_x, _, sav_x, *more = _lru_fwd(x5_x, rows, *prm, hfin_c, lru[0][3] if d else None, d,
                                               f"lru_fwd_x{d}", side=() if d else out_w)
        if d == 0:
            g_pa, g_pb, g_wo = more
        else:
            h_sum = more[0]
        lru[d] = ((cw_f, w_r_b[d], w_i_b[d], lam_f[d:d + 1]), h_c, hin_c, h_x, hin_x, sav_c, sav_x)
    p_a_f, p_b_f, w_out_f = g_pa.reshape(D, D), g_pb.reshape(D, D), g_wo.reshape(D, D)
    hx = _to_raster(h_sum, rows)

    gn = jnp.tile(norm_a_g.reshape(1, DH), (1, NH))
    (dr, do, dhx, dz_m, oa, obb, yb16, dya, dyb, dout, mvec) = _merge(
        z_x, gla[0][2], gla[1][2], hx, xs, tgt, mod, gn, p_a_f, p_b_f, w_out_f, ln_g, ln_b)

    dhx_cm = _to_colmajor(dhx, rows)
    lru_dx_x = lru_dx_c = None
    for d in (0, 1):
        prm, h_c, hin_c, h_x, hin_x, sav_c, sav_x = lru[d]
        lru_dx_x, dwr, dwi, lvec, cg0 = _lru_bwd(x5_x, rows, *prm, sav_x, h_x, hin_x, dhx_cm, zero_v, lru_dx_x, None,
                                                 d, f"lru_bwd_x{d}")
        lru_dx_c, dwr, dwi, lvec, _ = _lru_bwd(x5_c, c_len, *prm, sav_c, h_c, hin_c, None, cg0, lru_dx_c,
                                               (dwr, dwi, lvec), d, f"lru_bwd_c{d}")
        lru[d] = (dwr, dwi, lvec)
    dz5_x = _to_raster(lru_dx_x, rows).astype(BF16)
    dz5_c = lru_dx_c.astype(BF16)

    dpa = _mm_tn(oa, dya, None, "dpa", out_dtype=BF16)
    dpb = _mm_tn(obb, dyb, None, "dpb", out_dtype=BF16)
    dwo = _mm_tn(yb16, dout, None, "dwout", out_dtype=BF16)
    wr_pack = jnp.concatenate([lru[0][0], lru[1][0], lru[0][1], lru[1][1]], axis=0).reshape(4 * NH * DH, DH)

    gq_c = gv_c = None
    dzf_c, dlb = {}, {}
    gq_x, dzf_x0, gv_x, dlb_x, ds0, r_pa, r_pb, r_wo, r_wri = _gla_bwd(
        z_x, lb, gla[0][1], do, zero_s, None, None, 0, "gla_bwd_x0", f_dtype=BF16,
        side=[dpa, dpb, dwo, wr_pack], side_splits=[0, 0, 0, 0])
    gq_c, dzf_c[0], gv_c, dlb_c, _ = _gla_bwd(z_c, lb, gla[0][0], None, ds0, None, None, 0, "gla_bwd_c0")
    dlb[0] = dlb_x[0:1] + dlb_c[0:1]
    dz_g, dlb_x, ds0 = _gla_bwd(z_x, lb, gla[1][1], do, zero_s, gq_x, gv_x, 1, "gla_bwd_x1", into=(dz_m, dzf_x0))
    gq_c, dzf_c[1], gv_c, dlb_c, _ = _gla_bwd(z_c, lb, gla[1][0], None, ds0, gq_c, gv_c, 1, "gla_bwd_c1")
    dlb[1] = dlb_x[0:1] + dlb_c[0:1]

    bf = lambda t: t.astype(BF16)
    dz_x = lax.dynamic_update_slice(dz_g, dz5_x, (0, 5 * D))
    zc0 = jnp.zeros((c_len, D), BF16)
    dz_c = jnp.concatenate([bf(gq_c), bf(dzf_c[0]), bf(dzf_c[1]), bf(gv_c), zc0, dz5_c, zc0, zc0, zc0], axis=1)
    dwin_c, dbin_c = _mm_tn(u_c, dz_c, None, "dwin_c", with_colsum=True)

    grad_x, xvec = _input_grad(dz_x, w_in_f, xs, dr, mod, 0, "input_grad_x")
    r_win, dbin = _dwin_exchange(u_x, dz_x, dwin_c, dbin_c, "dwin_exchange")
    _, cvec = _input_grad(dz_c, w_in_f, cs, None, mod, 1, "input_grad_c")
    wri_piece = _sum_rows(r_wri, "sum_w_ri_piece")
    g_w_in, d_w_in, nm_w_in, nv_w_in = _sum_adamw(r_win, w_in, m_w_in, v_w_in, "update_w_in")
    g_p_a, d_p_a, nm_p_a, nv_p_a = _sum_adamw(r_pa, p_a, m_p_a, v_p_a, "update_p_a")
    g_p_b, d_p_b, nm_p_b, nv_p_b = _sum_adamw(r_pb, p_b, m_p_b, v_p_b, "update_p_b")
    g_w_out, d_w_out, nm_w_out, nv_w_out = _sum_adamw(r_wo, w_out, m_w_out, v_w_out, "update_w_out")

    dlb_rows = jnp.concatenate([dlb[0], dlb[1]], axis=0)
    pack = jnp.concatenate([
        xvec[0:1], xvec[1:2], mvec[0:1],
        cvec[0:1], cvec[1:2], jnp.zeros((1, D), F32),
        dbin.reshape(NGRP, D),
        mvec[3:4], mvec[1:2], mvec[2:3],
        lru[0][2][0:8], lru[1][2][0:3],
        lru[1][2][3:8],
        dlb_rows,
        mvec[4:5],
        jnp.zeros((3, D), F32)], axis=0)
    g_pack, g_wri = _all_gather([pack, wri_piece], "gather_small_grads")

    dmx = g_pack[:, 0:3, :].reshape(NDEV, 3 * D)
    dmc = g_pack[:, 3:6, :].reshape(NDEV, 3 * D)
    grad_w_mod = _wmod_grad(c_all.T, c_ctx.reshape(D, 1), _local_cols(dmx, me, mcols), _local_cols(dmc, me, mcols),
                            "grad_w_mod").reshape(1, D, mcols)
    small_params = [(c_ctx.reshape(1, D), m_c_ctx.reshape(1, D), v_c_ctx.reshape(1, D)), (b_mod, m_b_mod, v_b_mod),
                    (b_in, m_b_in, v_b_in), (lb_logits, m_lb_logits, v_lb_logits), (norm_a_g, m_norm_a_g, v_norm_a_g),
                    (conv_w, m_conv_w, v_conv_w), (conv_b, m_conv_b, v_conv_b), (b_r, m_b_r, v_b_r),
                    (b_i, m_b_i, v_b_i), (lam, m_lam, v_lam), (ln_g, m_ln_g, v_ln_g), (ln_b, m_ln_b, v_ln_b)]
    loss_tile, small_g, small_upd = _finalize_small(g_pack, lb, w_mod_f, small_params)
    loss = loss_tile[0, 0]
    (grad_c_ctx, grad_b_mod, grad_b_in, grad_lb_logits, grad_norm_a_g, grad_conv_w, grad_conv_b, grad_b_r, grad_b_i,
     grad_lam, grad_ln_g, grad_ln_b) = small_g
    small_upd[0] = tuple(t.reshape(c_ctx.shape) for t in small_upd[0])
    (o_c_ctx, o_b_mod, o_b_in, o_lb, o_norm, o_conv_w, o_conv_b, o_b_r, o_b_i, o_lam, o_ln_g, o_ln_b) = small_upd

    half = 2 * NH * DH
    g_ri = g_wri.reshape(2 * half, DH)
    grad_w_r, grad_w_i = g_ri[:half].reshape(w_r.shape), g_ri[half:].reshape(w_i.shape)
    d_w_r, nm_w_r, nv_w_r = _adamw(grad_w_r, w_r, m_w_r, v_w_r, "update_w_r")
    d_w_i, nm_w_i, nv_w_i = _adamw(grad_w_i, w_i, m_w_i, v_w_i, "update_w_i")

    d_w_mod, nm_w_mod, nv_w_mod = _adamw(grad_w_mod, w_mod, m_w_mod, v_w_mod, "update_w_mod")

    grads = [grad_c_ctx.reshape(c_ctx.shape), grad_w_mod, grad_b_mod, g_w_in, grad_b_in, grad_lb_logits, grad_norm_a_g,
             grad_conv_w, grad_conv_b, grad_w_r, grad_b_r, grad_w_i, grad_b_i, grad_lam, g_p_a, g_p_b, g_w_out,
             grad_ln_g, grad_ln_b]
    per_kind = []
    for k in range(3):
        per_kind.append([
            o_c_ctx[k], (d_w_mod, nm_w_mod, nv_w_mod)[k], o_b_mod[k], (d_w_in, nm_w_in, nv_w_in)[k], o_b_in[k], o_lb[k],
            o_norm[k], o_conv_w[k], o_conv_b[k], (d_w_r, nm_w_r, nv_w_r)[k], o_b_r[k], (d_w_i, nm_w_i, nv_w_i)[k],
            o_b_i[k], o_lam[k], (d_p_a, nm_p_a, nv_p_a)[k], (d_p_b, nm_p_b, nv_p_b)[k], (d_w_out, nm_w_out, nv_w_out)[k],
            o_ln_g[k], o_ln_b[k]])
    return (loss, grad_x.reshape(x.shape), *grads, *per_kind[0], *per_kind[1], *per_kind[2])
```

```python
import functools

import jax
import jax.numpy as jnp
from jax import lax
from jax.experimental import pallas as pl
from jax.experimental.pallas import tpu as pltpu

F32 = jnp.float32
BF16 = jnp.bfloat16

D = 1024
NH = 8
DH = 128
CHUNK = 64
GLA_HEADS_PER_STEP = 8
GRID_W = 64
NGRP = 9
NDEV = 8
RG_C = 8.0
ALPHA = 2.0 ** 0.25
LN_EPS = 1e-5
RMS_EPS = 1e-6
Q_SCALE = DH ** -0.5
ADAM_LR, ADAM_B1, ADAM_B2, ADAM_EPS, ADAM_WD, ADAM_STEP = 1e-3, 0.9, 0.999, 1e-8, 0.01, 10
ADAM_C1 = 1.0 / (1.0 - ADAM_B1 ** ADAM_STEP)
ADAM_C2 = 1.0 / (1.0 - ADAM_B2 ** ADAM_STEP)

ANY = pl.BlockSpec(memory_space=pl.ANY)


def _sigmoid(t):
    return 1.0 / (1.0 + jnp.exp(-t))


def _dsilu(t, s):
    return s * (1.0 + t * (1.0 - s))


def _dot(a, b):
    return jnp.dot(a, b, preferred_element_type=F32)


def _dot_nt(a, b):
    return lax.dot_general(a, b, (((1,), (1,)), ((), ())), preferred_element_type=F32)


def _dot_tn(a, b):
    return lax.dot_general(a, b, (((0,), (0,)), ((), ())), preferred_element_type=F32)


def _my_index():
    return 4 * lax.axis_index("x") + 2 * lax.axis_index("y") + lax.axis_index("c")


def _dev_tuple(j):
    return (j >> 2, (j >> 1) & 1, j & 1)


def _exchange_sems(n):
    return [pltpu.SemaphoreType.DMA((n * NDEV,)), pltpu.SemaphoreType.DMA((n * NDEV,)), pltpu.SemaphoreType.DMA((n,))]


def _exchange(ins, outs, sems, piece_of=None):
    send_sems, recv_sems, loc_sems = sems
    n = len(ins)

    def src(a, p):
        return ins[a] if piece_of is None else piece_of(ins[a], a, p)

    def push(a, t):
        me, p = _my_index(), _step_peer(t)
        return pltpu.make_async_remote_copy(
            src_ref=src(a, p), dst_ref=outs[a].at[me],
            send_sem=send_sems.at[a * NDEV + t], recv_sem=recv_sems.at[a * NDEV + me],
            device_id=_dev_of(p), device_id_type=pl.DeviceIdType.MESH)

    def local(a):
        me = _my_index()
        return pltpu.make_async_copy(src(a, me), outs[a].at[me], loc_sems.at[a])

    def start():
        for a in range(n):
            local(a).start()
        for t in range(NDEV - 1):
            for a in range(n):
                push(a, t).start()

    def finish():
        me = _my_index()
        for t in range(NDEV - 1):
            for a in range(n):
                push(a, t).wait_send()
        for j in range(NDEV):
            @pl.when(me != j)
            def _():
                for a in range(n):
                    pltpu.make_async_remote_copy(
                        src_ref=src(a, j), dst_ref=outs[a].at[j],
                        send_sem=send_sems.at[a * NDEV], recv_sem=recv_sems.at[a * NDEV + j],
                        device_id=_dev_tuple(j), device_id_type=pl.DeviceIdType.MESH).wait_recv()
        for a in range(n):
            local(a).wait()

    return start, finish


def _all_gather(shards, name):
    n = len(shards)

    def body(*refs):
        start, finish = _exchange(refs[:n], refs[n:2 * n], refs[2 * n:])
        start()
        finish()

    return pl.pallas_call(
        body, name=name,
        out_shape=[jax.ShapeDtypeStruct((NDEV,) + s.shape, s.dtype) for s in shards],
        in_specs=[ANY] * n, out_specs=[ANY] * n, scratch_shapes=_exchange_sems(n),
    )(*shards)


def _pieces(parts, splits):
    shapes = []
    for part, split in zip(parts, splits):
        r, c = part.shape
        shapes.append((r // NDEV, c) if split == 0 else (r, c // NDEV))

    def piece_of(ref, a, j):
        pr, pc = shapes[a]
        if splits[a] == 0:
            start = j * pr if isinstance(j, int) else pl.multiple_of(j * pr, pr)
            return ref.at[pl.ds(start, pr), :]
        start = j * pc if isinstance(j, int) else pl.multiple_of(j * pc, pc)
        return ref.at[:, pl.ds(start, pc)]

    return shapes, piece_of


_STEP_MASKS = ((2, 4, 6, 3, 5, 7, 1, 0), (4, 2, 6, 5, 3, 7, 1, 0))
_GATHER_MASKS = ((0, 1, 2, 4, 3, 5, 6, 7), (0, 1, 4, 2, 5, 3, 6, 7))


def _peer_schedule(table):
    tab = jnp.array(table, jnp.int32)
    return jnp.bitwise_xor(_my_index(), tab[lax.axis_index("c")])


def _step_peer(s, table=_STEP_MASKS):
    def pick(row):
        if isinstance(s, int):
            return jnp.int32(row[s])
        m = jnp.int32(row[NDEV - 1])
        for t in range(NDEV - 2, -1, -1):
            m = jnp.where(s == t, jnp.int32(row[t]), m)
        return m
    mask = jnp.where(lax.axis_index("c") == 0, pick(table[0]), pick(table[1]))
    return jnp.bitwise_xor(_my_index(), mask)


def _dev_of(p):
    return (p // 4, (p // 2) % 2, p % 2)


def _dwin_exchange(u, dz, init, cs_init, name):
    m, ka = u.shape
    n = dz.shape[1]
    pc = n // NDEV
    tk = _row_tile(m, 512)
    nk = m // tk

    def body(pidx_ref, u_ref, dz_ref, init_ref, csi_ref, rwin, cs_ref, acc, sbuf, wsend, wrecv, wloc):
        s, k = pl.program_id(0), pl.program_id(1)
        me = _my_index()

        def slab_copy(slot, p):
            return pltpu.make_async_remote_copy(
                src_ref=sbuf.at[slot], dst_ref=rwin.at[me], send_sem=wsend.at[slot], recv_sem=wrecv.at[me],
                device_id=_dev_of(p), device_id_type=pl.DeviceIdType.MESH)

        @pl.when(k == 0)
        def _():
            acc[...] = init_ref[...]
            cs_ref[...] = csi_ref[...]

        bv = dz_ref[...]
        acc[...] += _dot_tn(u_ref[...], bv)
        cs_ref[...] += jnp.sum(bv.astype(F32), axis=0, keepdims=True)

        @pl.when(k == nk - 1)
        def _():
            slot = s % 2

            @pl.when(s >= 2)
            def _():
                slab_copy(slot, me).wait_send()

            sbuf[slot] = acc[...].astype(BF16)

            @pl.when(s < NDEV - 1)
            def _():
                slab_copy(slot, _step_peer(s)).start()

            @pl.when(s == NDEV - 1)
            def _():
                own = pltpu.make_async_copy(sbuf.at[slot], rwin.at[me], wloc.at[0])
                own.start()
                slab_copy(1 - slot, me).wait_send()
                for j in range(NDEV):
                    @pl.when(me != j)
                    def _():
                        pltpu.make_async_remote_copy(
                            src_ref=sbuf.at[0], dst_ref=rwin.at[j], send_sem=wsend.at[0], recv_sem=wrecv.at[j],
                            device_id=_dev_tuple(j), device_id_type=pl.DeviceIdType.MESH).wait_recv()
                own.wait()

    grid_spec = pltpu.PrefetchScalarGridSpec(
        num_scalar_prefetch=1, grid=(NDEV, nk),
        in_specs=[pl.BlockSpec((tk, ka), lambda s, k, pidx: (k, 0)),
                  pl.BlockSpec((tk, pc), lambda s, k, pidx: (k, pidx[s])),
                  pl.BlockSpec((ka, pc), lambda s, k, pidx: (0, pidx[s])),
                  pl.BlockSpec((1, pc), lambda s, k, pidx: (0, pidx[s]))],
        out_specs=[ANY, pl.BlockSpec((1, pc), lambda s, k, pidx: (0, pidx[s]))],
        scratch_shapes=[pltpu.VMEM((ka, pc), F32), pltpu.VMEM((2, ka, pc), BF16),
                        pltpu.SemaphoreType.DMA((2,)), pltpu.SemaphoreType.DMA((NDEV,)), pltpu.SemaphoreType.DMA((1,))])
    return pl.pallas_call(
        body, name=name, grid_spec=grid_spec,
        out_shape=[jax.ShapeDtypeStruct((NDEV, ka, pc), BF16), jax.ShapeDtypeStruct((1, n), F32)],
    )(_peer_schedule(_STEP_MASKS), u, dz, init, cs_init)


def _inproj_gather(u, w_loc, bias, name):
    m, k = u.shape
    pc = w_loc.shape[1]
    n = pc * NDEV
    tm = _row_tile(m, 512)
    ni = m // tm

    def body(pidx_ref, u_ref, b_ref, wl_ref, z_ref, wall, wbuf, wsend, wrecv, ldsem, ownsem):
        s, i = pl.program_id(0), pl.program_id(1)
        me = _my_index()

        def shard_push(t):
            return pltpu.make_async_remote_copy(
                src_ref=wl_ref, dst_ref=wall.at[me], send_sem=wsend.at[t], recv_sem=wrecv.at[me],
                device_id=_dev_of(_step_peer(t, _GATHER_MASKS)), device_id_type=pl.DeviceIdType.MESH)

        def load(slot, src):
            return pltpu.make_async_copy(src, wbuf.at[slot], ldsem.at[slot])

        own = pltpu.make_async_copy(wl_ref, wall.at[me], ownsem.at[0])

        @pl.when((s == 0) & (i == 0))
        def _():
            own.start()
            load(0, wl_ref).start()
            for t in range(1, NDEV):
                shard_push(t).start()

        @pl.when((i == ni // 2) & (s < NDEV - 1))
        def _():
            nxt = _step_peer(s + 1, _GATHER_MASKS)
            pltpu.make_async_remote_copy(
                src_ref=wl_ref, dst_ref=wall.at[nxt], send_sem=wsend.at[0], recv_sem=wrecv.at[nxt],
                device_id=_dev_of(nxt), device_id_type=pl.DeviceIdType.MESH).wait_recv()
            load((s + 1) % 2, wall.at[nxt]).start()

        @pl.when(i == 0)
        def _():
            load(s % 2, wl_ref).wait()

        z_ref[...] = _dot(u_ref[...], wbuf[s % 2]) + b_ref[...]

        @pl.when((s == NDEV - 1) & (i == ni - 1))
        def _():
            own.wait()
            for t in range(1, NDEV):
                shard_push(t).wait_send()

    grid_spec = pltpu.PrefetchScalarGridSpec(
        num_scalar_prefetch=1, grid=(NDEV, ni),
        in_specs=[pl.BlockSpec((tm, k), lambda s, i, pidx: (i, 0)),
                  pl.BlockSpec((1, pc), lambda s, i, pidx: (0, pidx[s])), ANY],
        out_specs=[pl.BlockSpec((tm, pc), lambda s, i, pidx: (i, pidx[s])), ANY],
        scratch_shapes=[pltpu.VMEM((2, k, pc), BF16),
                        pltpu.SemaphoreType.DMA((NDEV,)), pltpu.SemaphoreType.DMA((NDEV,)),
                        pltpu.SemaphoreType.DMA((2,)), pltpu.SemaphoreType.DMA((1,))])
    return pl.pallas_call(
        body, name=name, grid_spec=grid_spec,
        out_shape=[jax.ShapeDtypeStruct((m, n), F32), jax.ShapeDtypeStruct((NDEV, k, pc), w_loc.dtype)],
    )(_peer_schedule(_GATHER_MASKS), u, bias, w_loc)


def _adam_math(g, w, m, v):
    m2 = ADAM_B1 * m + (1.0 - ADAM_B1) * g
    v2 = ADAM_B2 * v + (1.0 - ADAM_B2) * (g * g)
    delta = -ADAM_LR * ((m2 * ADAM_C1) / (jnp.sqrt(v2 * ADAM_C2) + ADAM_EPS) + ADAM_WD * w)
    return delta, m2, v2


def _row_tile(r, cap):
    t = min(r, cap)
    while r % t:
        t //= 2
    return t


def _adamw(g, w, m, v, name):
    shape = w.shape
    cols = shape[-1] if w.ndim >= 2 and shape[-1] % 128 == 0 else 128
    g2, w2, m2, v2 = (t.reshape(-1, cols) for t in (g, w, m, v))
    r = g2.shape[0]
    tr = _row_tile(r, 256)

    def body(g_ref, w_ref, m_ref, v_ref, d_ref, mo_ref, vo_ref):
        d, mm, vv = _adam_math(g_ref[...], w_ref[...], m_ref[...], v_ref[...])
        d_ref[...] = d
        mo_ref[...] = mm
        vo_ref[...] = vv

    spec = pl.BlockSpec((tr, cols), lambda i: (i, 0))
    outs = pl.pallas_call(
        body, name=name, grid=(r // tr,),
        out_shape=[jax.ShapeDtypeStruct((r, cols), F32)] * 3,
        in_specs=[spec] * 4, out_specs=[spec] * 3,
    )(g2, w2, m2, v2)
    return tuple(o.reshape(shape) for o in outs)


def _sum_adamw(parts, w, m, v, name):
    _, r, c = parts.shape
    shape = w.shape
    w2, m2, v2 = (t.reshape(r, c) for t in (w, m, v))
    tr = _row_tile(r, 128)

    def body(p_ref, w_ref, m_ref, v_ref, g_ref, d_ref, mo_ref, vo_ref):
        g = p_ref[0].astype(F32)
        for k in range(1, NDEV):
            g = g + p_ref[k].astype(F32)
        d, mm, vv = _adam_math(g, w_ref[...], m_ref[...], v_ref[...])
        g_ref[...] = g
        d_ref[...] = d
        mo_ref[...] = mm
        vo_ref[...] = vv

    spec = pl.BlockSpec((tr, c), lambda i: (i, 0))
    outs = pl.pallas_call(
        body, name=name, grid=(r // tr,),
        out_shape=[jax.ShapeDtypeStruct((r, c), F32)] * 4,
        in_specs=[pl.BlockSpec((NDEV, tr, c), lambda i: (0, i, 0))] + [spec] * 3, out_specs=[spec] * 4,
    )(parts, w2, m2, v2)
    return tuple(o.reshape(shape) for o in outs)


def _sum_rows(parts, name):
    _, r, c = parts.shape

    def body(p_ref, o_ref):
        g = p_ref[0]
        for k in range(1, NDEV):
            g = g + p_ref[k]
        o_ref[...] = g

    return pl.pallas_call(
        body, name=name, out_shape=jax.ShapeDtypeStruct((r, c), F32),
    )(parts)


def _prep(cc, w_mod_full, b_mod, lbl):
    def body(cc_ref, w_ref, b_ref, l_ref, mod_ref, lb_ref):
        t = cc_ref[...]
        s = (t * _sigmoid(t)).astype(BF16)
        mod_ref[...] = _dot(s, w_ref[...]) + b_ref[...]
        lb_ref[...] = _sigmoid(l_ref[0] - l_ref[1])

    return pl.pallas_call(
        body, name="prep",
        out_shape=[jax.ShapeDtypeStruct((8, 3 * D), F32), jax.ShapeDtypeStruct((8, D), F32)],
    )(cc, w_mod_full, b_mod, lbl)


def _modulate(xin, mod, row, name):
    n = xin.shape[0]
    tm = _row_tile(n, 512)

    def body(x_ref, mod_ref, u_ref):
        sh = mod_ref[row:row + 1, 0:D]
        sc = mod_ref[row:row + 1, D:2 * D]
        u_ref[...] = (x_ref[...] * (1.0 + sc) + sh).astype(BF16)

    return pl.pallas_call(
        body, name=name, grid=(n // tm,),
        out_shape=jax.ShapeDtypeStruct((n, D), BF16),
        in_specs=[pl.BlockSpec((tm, D), lambda i: (i, 0)), pl.BlockSpec((8, 3 * D), lambda i: (0, 0))],
        out_specs=pl.BlockSpec((tm, D), lambda i: (i, 0)),
    )(xin, mod)


def _mm_bias(a, w_all, bias, name):
    m, k = a.shape
    tn = w_all.shape[2]
    n = tn * NDEV
    tm = _row_tile(m, 512)

    def body(a_ref, b_ref, bias_ref, o_ref):
        o_ref[...] = _dot(a_ref[...], b_ref[0]) + bias_ref[...]

    return pl.pallas_call(
        body, name=name, grid=(NDEV, m // tm),
        out_shape=jax.ShapeDtypeStruct((m, n), F32),
        in_specs=[pl.BlockSpec((tm, k), lambda j, i: (i, 0)), pl.BlockSpec((1, k, tn), lambda j, i: (j, 0, 0)),
                  pl.BlockSpec((1, tn), lambda j, i: (0, j))],
        out_specs=pl.BlockSpec((tm, tn), lambda j, i: (i, j)),
    )(a, w_all, bias)


def _mm_tn(a, b, init, name, with_colsum=False, colsum_init=None, out_dtype=F32):
    m, ka = a.shape
    n = b.shape[1]
    tk = _row_tile(m, 512)
    tn = 1024
    nk = m // tk
    has_init = init is not None

    def body(*refs):
        a_ref, b_ref = refs[0], refs[1]
        pos = 2
        init_ref = cs_init_ref = None
        if has_init:
            init_ref = refs[pos]
            pos += 1
            if with_colsum:
                cs_init_ref = refs[pos]
                pos += 1
        o_ref = refs[pos]
        cs_ref = refs[pos + 1] if with_colsum else None
        acc = refs[-1]
        k = pl.program_id(1)

        @pl.when(k == 0)
        def _():
            if has_init:
                acc[...] = init_ref[...]
                if with_colsum:
                    cs_ref[...] = cs_init_ref[...]
            else:
                acc[...] = jnp.zeros_like(acc)
                if with_colsum:
                    cs_ref[...] = jnp.zeros_like(cs_ref)

        bv = b_ref[...]
        acc[...] += _dot_tn(a_ref[...], bv)
        if with_colsum:
            cs_ref[...] += jnp.sum(bv.astype(F32), axis=0, keepdims=True)

        @pl.when(k == nk - 1)
        def _():
            o_ref[...] = acc[...].astype(out_dtype)

    in_specs = [pl.BlockSpec((tk, ka), lambda j, k: (k, 0)), pl.BlockSpec((tk, tn), lambda j, k: (k, j))]
    args = [a, b]
    if has_init:
        in_specs.append(pl.BlockSpec((ka, tn), lambda j, k: (0, j)))
        args.append(init)
        if with_colsum:
            in_specs.append(pl.BlockSpec((1, tn), lambda j, k: (0, j)))
            args.append(colsum_init)
    out_shape = [jax.ShapeDtypeStruct((ka, n), out_dtype)]
    out_specs = [pl.BlockSpec((ka, tn), lambda j, k: (0, j))]
    if with_colsum:
        out_shape.append(jax.ShapeDtypeStruct((1, n), F32))
        out_specs.append(pl.BlockSpec((1, tn), lambda j, k: (0, j)))
    outs = pl.pallas_call(
        body, name=name, grid=(n // tn, nk), out_shape=out_shape, in_specs=in_specs, out_specs=out_specs,
        scratch_shapes=[pltpu.VMEM((ka, tn), F32)],
    )(*args)
    return outs if with_colsum else outs[0]


def _input_grad(dz, w_all, xin, dr, mod, row, name, side=(), side_splits=()):
    m, n = dz.shape
    tm = _row_tile(m, 512)
    tk = w_all.shape[2]
    nk = NDEV
    ni = m // tm
    has_dr = dr is not None
    ns = len(side)
    piece_shapes, piece_of = _pieces(side, side_splits)

    def body(*refs):
        dz_ref, w_ref, x_ref = refs[:3]
        pos = 3
        dr_ref = refs[pos] if has_dr else None
        pos += int(has_dr)
        mod_ref = refs[pos]
        side_in = refs[pos + 1:pos + 1 + ns]
        pos += 1 + ns
        gx_ref = refs[pos] if has_dr else None
        pos += int(has_dr)
        vec_ref = refs[pos]
        side_out = refs[pos + 1:pos + 1 + ns]
        acc = refs[pos + 1 + ns]
        i, k = pl.program_id(0), pl.program_id(1)
        if ns:
            side_start, side_finish = _exchange(side_in, side_out, refs[pos + 2 + ns:], piece_of)

            @pl.when((i == 0) & (k == 0))
            def _():
                side_start()

        @pl.when(k == 0)
        def _():
            acc[...] = jnp.zeros_like(acc)

        @pl.when((i == 0) & (k == 0))
        def _():
            vec_ref[...] = jnp.zeros_like(vec_ref)

        acc[...] += _dot_nt(dz_ref[...], w_ref[0])

        @pl.when(k == nk - 1)
        def _():
            du = acc[...]
            xv = x_ref[...]
            if has_dr:
                sc = mod_ref[row:row + 1, D:2 * D]
                gx_ref[...] = ALPHA * dr_ref[...] + du * (1.0 + sc)
            vec_ref[0:1, :] += jnp.sum(du, axis=0, keepdims=True)
            vec_ref[1:2, :] += jnp.sum(du * xv, axis=0, keepdims=True)

        if ns:
            @pl.when((i == ni - 1) & (k == nk - 1))
            def _():
                side_finish()

    row_spec = pl.BlockSpec((tm, D), lambda i, k: (i, 0))
    in_specs = [pl.BlockSpec((tm, tk), lambda i, k: (i, k)), pl.BlockSpec((1, D, tk), lambda i, k: (k, 0, 0)), row_spec]
    args = [dz, w_all, xin]
    if has_dr:
        in_specs.append(row_spec)
        args.append(dr)
    in_specs.append(pl.BlockSpec((8, 3 * D), lambda i, k: (0, 0)))
    args.append(mod)
    in_specs += [ANY] * ns
    args += list(side)
    out_shape, out_specs = [], []
    if has_dr:
        out_shape.append(jax.ShapeDtypeStruct((m, D), F32))
        out_specs.append(row_spec)
    out_shape.append(jax.ShapeDtypeStruct((8, D), F32))
    out_specs.append(pl.BlockSpec((8, D), lambda i, k: (0, 0)))
    out_shape += [jax.ShapeDtypeStruct((NDEV,) + piece_shapes[a], side[a].dtype) for a in range(ns)]
    out_specs += [ANY] * ns
    outs = pl.pallas_call(
        body, name=name, grid=(ni, nk), out_shape=out_shape, in_specs=in_specs, out_specs=out_specs,
        scratch_shapes=[pltpu.VMEM((tm, D), F32)] + (_exchange_sems(ns) if ns else []),
    )(*args)
    return tuple(outs) if has_dr else (None, *outs)


def _tri(reverse):
    r = lax.broadcasted_iota(jnp.int32, (CHUNK, CHUNK), 0)
    c = lax.broadcasted_iota(jnp.int32, (CHUNK, CHUNK), 1)
    return (c >= r) if reverse else (c <= r)


def _cum_f32(tri_b, t):
    hi = t.astype(BF16)
    r1 = t - hi.astype(F32)
    mid = r1.astype(BF16)
    lo = (r1 - mid.astype(F32)).astype(BF16)
    return _dot(tri_b, hi) + _dot(tri_b, mid) + _dot(tri_b, lo)


def _gla_features(zq, zf, lb):
    sq = _sigmoid(zq)
    q = zq * sq * Q_SCALE
    sf = _sigmoid(zf)
    f = lb + (1.0 - lb) * sf
    return q, sq, f, sf


def _gla_decays(f, tri_b, last):
    lf = jnp.log(f)
    g = _cum_f32(tri_b, lf)
    gl = g[last:last + 1, :]
    return g, gl


def _gla_block(n):
    return 256 if n % 256 == 0 else CHUNK


def _gla_fwd(z, lb, s0, d, name, side=()):
    n = z.shape[0]
    blk = _gla_block(n)
    nb, npb = n // blk, blk // CHUNK
    reverse = d == 1
    last = 0 if reverse else CHUNK - 1
    order = list(range(npb))[::-1] if reverse else list(range(npb))
    ns = len(side)

    def bmap(i):
        return nb - 1 - i if reverse else i

    hp = GLA_HEADS_PER_STEP
    hw = hp * DH
    units = [(hh, cidx) for hh in range(hp) for cidx in order]

    def body(zq_ref, zf_ref, zv_ref, lb_ref, s0_ref, *rest):
        side_in = rest[:ns]
        o_ref, ss_ref, sf_ref = rest[ns:ns + 3]
        side_out = rest[ns + 3:2 * ns + 3]
        st = rest[2 * ns + 3]
        i = pl.program_id(1)
        if ns:
            side_start, side_finish = _exchange(side_in, side_out, rest[2 * ns + 4:])

            @pl.when((pl.program_id(0) == 0) & (i == 0))
            def _():
                side_start()

        @pl.when(i == 0)
        def _():
            st[...] = s0_ref[...]

        mask = _tri(reverse)
        tri_b = jnp.where(mask, 1.0, 0.0).astype(BF16)
        feat = {}
        for u in units:
            hh, cidx = u
            rows, cols = pl.ds(cidx * CHUNK, CHUNK), pl.ds(hh * DH, DH)
            q, _, f, _ = _gla_features(zq_ref[rows, cols], zf_ref[rows, cols], lb_ref[d:d + 1, cols])
            feat[u] = (q, 1.0 - f, jnp.log(f), zv_ref[rows, cols].astype(BF16))
        dec = {u: _cum_f32(tri_b, feat[u][2]) for u in units}
        ops = {}
        for u in units:
            q, k, _, vb = feat[u]
            g = dec[u]
            gl = g[last:last + 1, :]
            ops[u] = ((q * jnp.exp(g)).astype(BF16), (k * jnp.exp(-g)).astype(BF16),
                      (k * jnp.exp(gl - g)).astype(BF16), jnp.exp(gl), vb)
        att = {u: jnp.where(mask, _dot_nt(ops[u][0], ops[u][1]), 0.0).astype(BF16) for u in units}
        upd = {u: _dot_tn(ops[u][4], ops[u][2]) for u in units}
        intra = {u: _dot(att[u], ops[u][4]) for u in units}
        s_in = {}
        for hh in range(hp):
            s = st[hh]
            ss_ref[hh, 0] = s
            for cidx in order:
                s_in[(hh, cidx)] = s
                s = s * ops[(hh, cidx)][3] + upd[(hh, cidx)]
            st[hh] = s
            sf_ref[hh] = s
        for u in units:
            hh, cidx = u
            rows, cols = pl.ds(cidx * CHUNK, CHUNK), pl.ds(hh * DH, DH)
            o_ref[rows, cols] = intra[u] + _dot_nt(ops[u][0], s_in[u].astype(BF16))

        if ns:
            @pl.when((pl.program_id(0) == NH // hp - 1) & (i == nb - 1))
            def _():
                side_finish()

    def col(g):
        return lambda h, i: (bmap(i), g * (NH // hp) + h)

    return pl.pallas_call(
        body, name=name, grid=(NH // hp, nb),
        out_shape=[jax.ShapeDtypeStruct((n, D), F32), jax.ShapeDtypeStruct((NH, nb, DH, DH), F32),
                   jax.ShapeDtypeStruct((NH, DH, DH), F32)]
        + [jax.ShapeDtypeStruct((NDEV,) + t.shape, t.dtype) for t in side],
        in_specs=[pl.BlockSpec((blk, hw), col(0)), pl.BlockSpec((blk, hw), col(1 + d)),
                  pl.BlockSpec((blk, hw), col(3)), pl.BlockSpec((8, hw), lambda h, i: (0, h)),
                  pl.BlockSpec((hp, DH, DH), lambda h, i: (h, 0, 0))] + [ANY] * ns,
        out_specs=[pl.BlockSpec((blk, hw), lambda h, i: (bmap(i), h)),
                   pl.BlockSpec((hp, 1, DH, DH), lambda h, i: (h, bmap(i), 0, 0)),
                   pl.BlockSpec((hp, DH, DH), lambda h, i: (h, 0, 0))] + [ANY] * ns,
        scratch_shapes=[pltpu.VMEM((hp, DH, DH), F32)] + (_exchange_sems(ns) if ns else []),
    )(z, z, z, lb, s0, *side)


def _gla_bwd(z, lb, s_start, do, ds_fin, acc_q, acc_v, d, name, f_dtype=F32, into=None, side=(), side_splits=()):
    n = z.shape[0]
    blk = _gla_block(n)
    nb, npb = n // blk, blk // CHUNK
    reverse = d == 1
    last = 0 if reverse else CHUNK - 1
    order = list(range(npb)) if reverse else list(range(npb))[::-1]
    has_do = do is not None
    has_acc = acc_q is not None
    fused = into is not None
    assert not fused or d == 1
    ns = len(side)
    assert not (fused and ns)
    piece_shapes, piece_of = _pieces(side, side_splits)
    hp = NH if fused else GLA_HEADS_PER_STEP
    hw = hp * DH
    units = [(hh, cidx) for hh in range(hp) for cidx in order]

    def bmap(i):
        return i if reverse else nb - 1 - i

    def body(*refs):
        zq_ref, zf_ref, zv_ref, lb_ref, ss_ref, dsf_ref = refs[:6]
        pos = 6
        do_ref = aq_ref = av_ref = None
        if has_do:
            do_ref = refs[pos]
            pos += 1
        if has_acc:
            aq_ref, av_ref = refs[pos], refs[pos + 1]
            pos += 2
        if fused:
            other_ref = refs[pos + 1]
            dz_ref, dlb_ref, ds0_ref, dst = refs[pos + 2:]
            dz_ref[:, D:2 * D] = other_ref[...]
        else:
            side_in = refs[pos:pos + ns]
            dzq_ref, dzf_ref, dzv_ref, dlb_ref, ds0_ref = refs[pos + ns:pos + ns + 5]
            side_out = refs[pos + ns + 5:pos + 2 * ns + 5]
            dst = refs[pos + 2 * ns + 5]
        i = pl.program_id(1)
        if ns:
            side_start, side_finish = _exchange(side_in, side_out, refs[pos + 2 * ns + 6:], piece_of)

            @pl.when((pl.program_id(0) == 0) & (i == 0))
            def _():
                side_start()

        @pl.when(i == 0)
        def _():
            dst[...] = dsf_ref[...]
            dlb_ref[...] = jnp.zeros_like(dlb_ref)

        mask = _tri(reverse)
        tri_b = jnp.where(mask, 1.0, 0.0).astype(BF16)
        tri_t = jnp.where(_tri(not reverse), 1.0, 0.0).astype(BF16)

        def where(u):
            return pl.ds(u[1] * CHUNK, CHUNK), pl.ds(u[0] * DH, DH)

        feat = {}
        for u in units:
            rows, cols = where(u)
            zq, zf = zq_ref[rows, cols], zf_ref[rows, cols]
            lbv = lb_ref[d:d + 1, cols]
            q, sq, f, sf = _gla_features(zq, zf, lbv)
            feat[u] = dict(zq=zq, q=q, sq=sq, f=f, sf=sf, lbv=lbv, k=1.0 - f, vb=zv_ref[rows, cols].astype(BF16))
        dec = {u: _cum_f32(tri_b, jnp.log(feat[u]["f"])) for u in units}
        for u in units:
            w = feat[u]
            g = dec[u]
            gl = g[last:last + 1, :]
            w["eg"], w["egi"], w["ege"], w["egl"] = jnp.exp(g), jnp.exp(-g), jnp.exp(gl - g), jnp.exp(gl)
            w["qd"], w["ki"], w["ke"] = w["q"] * w["eg"], w["k"] * w["egi"], w["k"] * w["ege"]
            w["qdb"], w["kib"], w["keb"] = w["qd"].astype(BF16), w["ki"].astype(BF16), w["ke"].astype(BF16)
        fwd_order = order[::-1]
        for hh in range(hp):
            s = ss_ref[hh, 0]
            for cidx in fwd_order:
                w = feat[(hh, cidx)]
                w["s_in"] = s
                if cidx != fwd_order[-1]:
                    s = s * w["egl"] + _dot_tn(w["vb"], w["keb"])
        if has_do:
            for u in units:
                w = feat[u]
                rows, cols = where(u)
                w["dob"] = do_ref[rows, cols].astype(BF16)
            for u in units:
                w = feat[u]
                w["a"] = jnp.where(mask, _dot_nt(w["qdb"], w["kib"]), 0.0).astype(BF16)
                w["da"] = jnp.where(mask, _dot_nt(w["dob"], w["vb"]), 0.0).astype(BF16)
                w["m"] = _dot_tn(w["dob"], w["qdb"])
        for hh in range(hp):
            ds = dst[hh]
            for cidx in order:
                w = feat[(hh, cidx)]
                w["ds"] = ds
                ds = ds * w["egl"]
                if has_do:
                    ds = ds + w["m"]
            dst[hh] = ds
            ds0_ref[hh] = ds
        for u in units:
            w = feat[u]
            dsb = w["ds"].astype(BF16)
            w["dke"] = _dot(w["vb"], dsb)
            w["dv"] = _dot_nt(w["keb"], dsb)
            if has_do:
                w["dv"] = w["dv"] + _dot_tn(w["a"], w["dob"])
                w["dqd"] = _dot(w["da"], w["kib"]) + _dot(w["dob"], w["s_in"].astype(BF16))
                w["dki"] = _dot_tn(w["da"], w["qdb"])
        for u in units:
            w = feat[u]
            dkeke = w["dke"] * w["ke"]
            w["dgl"] = (w["egl"] * jnp.sum(w["s_in"] * w["ds"], axis=0, keepdims=True)
                        + jnp.sum(dkeke, axis=0, keepdims=True))
            dg = -dkeke
            dk = w["dke"] * w["ege"]
            if has_do:
                dg = dg + w["dqd"] * w["qd"] - w["dki"] * w["ki"]
                dk = dk + w["dki"] * w["egi"]
            w["dg"], w["dk"] = dg, dk
        dlf = {u: _cum_f32(tri_t, feat[u]["dg"]) for u in units}
        for u in units:
            w = feat[u]
            rows, cols = where(u)
            df = (dlf[u] + w["dgl"]) / w["f"] - w["dk"]
            sf = w["sf"]
            dzf = df * (1.0 - w["lbv"]) * sf * (1.0 - sf)
            dlb_ref[0:1, cols] += jnp.sum(df * (1.0 - sf), axis=0, keepdims=True)
            if has_do:
                dzq = w["dqd"] * w["eg"] * (Q_SCALE * _dsilu(w["zq"], w["sq"]))
            else:
                dzq = jnp.zeros((CHUNK, DH), F32)
            dv = w["dv"]
            if has_acc:
                dzq = dzq + aq_ref[rows, cols]
                dv = dv + av_ref[rows, cols]
            if fused:
                lane = u[0] * DH
                dz_ref[rows, pl.ds(lane, DH)] = dzq.astype(BF16)
                dz_ref[rows, pl.ds(2 * D + lane, DH)] = dzf.astype(BF16)
                dz_ref[rows, pl.ds(3 * D + lane, DH)] = dv.astype(BF16)
            else:
                dzq_ref[rows, cols] = dzq
                dzf_ref[rows, cols] = dzf.astype(f_dtype)
                dzv_ref[rows, cols] = dv

        if ns:
            @pl.when((pl.program_id(0) == NH // hp - 1) & (i == nb - 1))
            def _():
                side_finish()

    def col(g):
        return lambda h, i: (bmap(i), g * (NH // hp) + h)

    tok = pl.BlockSpec((blk, hw), lambda h, i: (bmap(i), h))
    state = pl.BlockSpec((hp, DH, DH), lambda h, i: (h, 0, 0))
    in_specs = [pl.BlockSpec((blk, hw), col(0)), pl.BlockSpec((blk, hw), col(1 + d)), pl.BlockSpec((blk, hw), col(3)),
                pl.BlockSpec((8, hw), lambda h, i: (0, h)),
                pl.BlockSpec((hp, 1, DH, DH), lambda h, i: (h, bmap(i), 0, 0)), state]
    args = [z, z, z, lb, s_start, ds_fin]
    if has_do:
        in_specs.append(tok)
        args.append(do)
    if has_acc:
        in_specs += [tok, tok]
        args += [acc_q, acc_v]
    tail_shape = [jax.ShapeDtypeStruct((8, D), F32), jax.ShapeDtypeStruct((NH, DH, DH), F32)]
    tail_specs = [pl.BlockSpec((8, hw), lambda h, i: (0, h)), state]
    if fused:
        buf, other = into
        aliases = {len(args): 0}
        in_specs += [ANY, tok]
        args += [buf, other]
        out_shape = [jax.ShapeDtypeStruct(buf.shape, buf.dtype)] + tail_shape
        out_specs = [pl.BlockSpec((blk, 4 * D), lambda h, i: (bmap(i), 0))] + tail_specs
    else:
        aliases = {}
        in_specs += [ANY] * ns
        args += list(side)
        out_shape = [jax.ShapeDtypeStruct((n, D), F32), jax.ShapeDtypeStruct((n, D), f_dtype),
                     jax.ShapeDtypeStruct((n, D), F32)] + tail_shape
        out_shape += [jax.ShapeDtypeStruct((NDEV,) + piece_shapes[a], side[a].dtype) for a in range(ns)]
        out_specs = [tok, tok, tok] + tail_specs + [ANY] * ns
    return pl.pallas_call(
        body, name=name, grid=(NH // hp, nb), out_shape=out_shape, in_specs=in_specs, out_specs=out_specs,
        input_output_aliases=aliases,
        scratch_shapes=[pltpu.VMEM((hp, DH, DH), F32)] + (_exchange_sems(ns) if ns else []),
    )(*args)


def _shift(t, s, fill, down):
    n = t.shape[0]
    rows = lax.broadcasted_iota(jnp.int32, t.shape, 0)
    if down:
        return jnp.where(rows >= s, pltpu.roll(t, s, 0), fill)
    return jnp.where(rows < n - s, pltpu.roll(t, n - s, 0), fill)


SUBLANES = 8
LRU_SAVED = 4


def _chain_scan(a, b, h_in, down):
    n = a.shape[0]
    ng = n // SUBLANES
    rows = lax.broadcasted_iota(jnp.int32, (SUBLANES, a.shape[1]), 0)
    local = []
    for g in range(ng):
        aa, bb = a[g * SUBLANES:(g + 1) * SUBLANES], b[g * SUBLANES:(g + 1) * SUBLANES]
        for s in (1, 2, 4):
            if down:
                keep, amt = rows >= s, s
            else:
                keep, amt = rows < SUBLANES - s, SUBLANES - s
            bb = bb + aa * jnp.where(keep, pltpu.roll(bb, amt, 0), 0.0)
            aa = aa * jnp.where(keep, pltpu.roll(aa, amt, 0), 1.0)
        local.append((aa, bb))
    out = [None] * ng
    carry = h_in
    for g in (range(ng) if down else range(ng - 1, -1, -1)):
        aa, bb = local[g]
        hg = bb + aa * carry
        out[g] = hg
        carry = hg[SUBLANES - 1:SUBLANES] if down else hg[0:1]
    return (jnp.concatenate(out, axis=0) if ng > 1 else out[0]), carry


def _conv_taps(xv):
    return (_shift(xv, 1, 0.0, True), xv, _shift(xv, 1, 0.0, False), _shift(xv, 2, 0.0, False))


def _conv(taps, cw, cb):
    return cb + cw[0:1, :] * taps[0] + cw[1:2, :] * taps[1] + cw[2:3, :] * taps[2] + cw[3:4, :] * taps[3]


def _neg_expm1(t):
    series = -t * (1.0 + t * (0.5 + t * (1.0 / 6.0 + t * (1.0 / 24.0 + t * (1.0 / 120.0)))))
    return jnp.where(t > -0.1, series, 1.0 - jnp.exp(t))


def _lru_gates(xc, wr, br, wi, bi, lam):
    xcb = xc.astype(BF16)
    r = _sigmoid(_dot(xcb, wr) + br)
    gi = _sigmoid(_dot(xcb, wi) + bi)
    sp = jnp.maximum(-lam, 0.0) + jnp.log(1.0 + jnp.exp(-jnp.abs(lam)))
    la = -RG_C * r * sp
    a = jnp.exp(la)
    mult = jnp.sqrt(_neg_expm1(2.0 * la))
    return xcb, r, gi, sp, a, mult


def _lru_fwd(xin, blk, cw, cb, wr, br, wi, bi, lam, h0, acc_h, d, name, side=()):
    n = xin.shape[0]
    nb = n // blk
    reverse = d == 1
    down = not reverse
    has_acc = acc_h is not None
    ns = len(side)

    def bmap(i):
        return nb - 1 - i if reverse else i

    def body(*refs):
        x_ref, cw_ref, cb_ref, wr_ref, br_ref, wi_ref, bi_ref, lam_ref, h0_ref = refs[:9]
        pos = 9
        acc_ref = refs[pos] if has_acc else None
        pos += int(has_acc)
        side_in = refs[pos:pos + ns]
        pos += ns
        h_ref, hin_ref, hfin_ref, sav_a_ref, sav_ref = refs[pos:pos + 5]
        pos += 5
        hsum_ref = refs[pos] if has_acc else None
        pos += int(has_acc)
        side_out = refs[pos:pos + ns]
        carry = refs[pos + ns]
        i = pl.program_id(0)
        if ns:
            side_start, side_finish = _exchange(side_in, side_out, refs[pos + ns + 1:])

            @pl.when(i == 0)
            def _():
                side_start()

        @pl.when(i == 0)
        def _():
            carry[...] = h0_ref[...]

        for g in range(NH):
            cols = pl.ds(g * DH, DH)
            xc = _conv(_conv_taps(x_ref[:, cols]), cw_ref[:, cols], cb_ref[:, cols])
            _, r, gi, _, a, mult = _lru_gates(xc, wr_ref[g], br_ref[:, cols], wi_ref[g], bi_ref[:, cols],
                                              lam_ref[:, cols])
            sav_a_ref[:, cols] = a
            for slot, val in enumerate((xc, r, gi, mult)):
                sav_ref[slot, :, cols] = val.astype(BF16)
            hin = carry[:, cols]
            h, h_last = _chain_scan(a, mult * gi * xc, hin, down)
            h_ref[:, cols] = h
            if has_acc:
                hsum_ref[:, cols] = h + acc_ref[:, cols]
            hin_ref[0, :, cols] = hin
            carry[:, cols] = h_last
            hfin_ref[:, cols] = h_last

        if ns:
            @pl.when(i == nb - 1)
            def _():
                side_finish()

    vec = pl.BlockSpec((1, D), lambda i: (0, 0))
    wsp = pl.BlockSpec((NH, DH, DH), lambda i: (0, 0, 0))
    tok = pl.BlockSpec((blk, D), lambda i: (bmap(i), 0))
    in_specs = [tok, pl.BlockSpec((4, D), lambda i: (0, 0)), vec, wsp, vec, wsp, vec, vec, vec]
    args = [xin, cw, cb, wr, br, wi, bi, lam, h0]
    out_shape = [jax.ShapeDtypeStruct((n, D), F32), jax.ShapeDtypeStruct((nb, 1, D), F32),
                 jax.ShapeDtypeStruct((1, D), F32), jax.ShapeDtypeStruct((n, D), F32),
                 jax.ShapeDtypeStruct((LRU_SAVED, n, D), BF16)]
    out_specs = [tok, pl.BlockSpec((1, 1, D), lambda i: (bmap(i), 0, 0)), vec, tok,
                 pl.BlockSpec((LRU_SAVED, blk, D), lambda i: (0, bmap(i), 0))]
    if has_acc:
        in_specs.append(tok)
        args.append(acc_h)
        out_shape.append(jax.ShapeDtypeStruct((n, D), F32))
        out_specs.append(tok)
    in_specs += [ANY] * ns
    args += list(side)
    out_shape += [jax.ShapeDtypeStruct((NDEV,) + t.shape, t.dtype) for t in side]
    out_specs += [ANY] * ns
    return pl.pallas_call(
        body, name=name, grid=(nb,), out_shape=out_shape, in_specs=in_specs, out_specs=out_specs,
        scratch_shapes=[pltpu.VMEM((1, D), F32)] + (_exchange_sems(ns) if ns else []),
    )(*args)


def _lru_bwd(xin, blk, cw, wr, wi, lam, sav, h, hin, dh, cg_fin, acc_dx, init, d, name):
    n = xin.shape[0]
    nb = n // blk
    reverse = d == 1
    down = not reverse
    first = blk - 1 if reverse else 0
    has_dh = dh is not None
    has_acc = acc_dx is not None
    has_init = init is not None

    def bmap(i):
        return i if reverse else nb - 1 - i

    def body(*refs):
        (x_ref, cw_ref, wr_ref, wi_ref, lam_ref, sav_a_ref, sav_ref, h_ref, hin_ref, cgf_ref) = refs[:10]
        pos = 10
        dh_ref = acc_ref = None
        iwr_ref = iwi_ref = ivec_ref = None
        if has_dh:
            dh_ref = refs[pos]
            pos += 1
        if has_acc:
            acc_ref = refs[pos]
            pos += 1
        if has_init:
            iwr_ref, iwi_ref, ivec_ref = refs[pos:pos + 3]
            pos += 3
        dx_ref, dwr_ref, dwi_ref, vec_ref, cg0_ref, carry = refs[pos:]
        i = pl.program_id(0)

        @pl.when(i == 0)
        def _():
            carry[...] = cgf_ref[...]
            if has_init:
                dwr_ref[...] = iwr_ref[...]
                dwi_ref[...] = iwi_ref[...]
                vec_ref[...] = ivec_ref[...]
            else:
                dwr_ref[...] = jnp.zeros_like(dwr_ref)
                dwi_ref[...] = jnp.zeros_like(dwi_ref)
                vec_ref[...] = jnp.zeros_like(vec_ref)

        for g in range(NH):
            cols = pl.ds(g * DH, DH)
            cwv = cw_ref[:, cols]
            lam_v = lam_ref[:, cols]
            taps = _conv_taps(x_ref[:, cols])
            wr_g, wi_g = wr_ref[g], wi_ref[g]
            a = sav_a_ref[:, cols]
            xcb = sav_ref[0, :, cols]
            xc, r, gi, mult = (sav_ref[slot, :, cols].astype(F32) for slot in range(LRU_SAVED))
            sp = jnp.maximum(-lam_v, 0.0) + jnp.log(1.0 + jnp.exp(-jnp.abs(lam_v)))
            hprev = _shift(h_ref[:, cols], 1, hin_ref[0, :, cols], down)
            a_next = _shift(a, 1, 1.0, not down)
            dhv = dh_ref[:, cols] if has_dh else jnp.zeros_like(a)
            e, _ = _chain_scan(a_next, dhv, carry[:, cols], not down)
            cg = a[first:first + 1, :] * e[first:first + 1, :]
            carry[:, cols] = cg
            cg0_ref[:, cols] = cg
            da = e * hprev
            emult = e * mult
            dgi = emult * xc
            dxc = emult * gi
            dla = da * a - (e * gi * xc) * (a * a) / mult
            dr = dla * (-RG_C * sp)
            sneg = 1.0 - _sigmoid(lam_v)
            dpr = dr * r * (1.0 - r)
            dpi = dgi * gi * (1.0 - gi)
            dprb, dpib = dpr.astype(BF16), dpi.astype(BF16)
            dxc = dxc + _dot_nt(dprb, wr_g) + _dot_nt(dpib, wi_g)
            dwr_ref[g] += _dot_tn(xcb, dprb)
            dwi_ref[g] += _dot_tn(xcb, dpib)
            dx = (cwv[0:1, :] * _shift(dxc, 1, 0.0, False) + cwv[1:2, :] * dxc
                  + cwv[2:3, :] * _shift(dxc, 1, 0.0, True) + cwv[3:4, :] * _shift(dxc, 2, 0.0, True))
            if has_acc:
                dx = dx + acc_ref[:, cols]
            dx_ref[:, cols] = dx
            vec_ref[0:1, cols] += jnp.sum(dpr, axis=0, keepdims=True)
            vec_ref[1:2, cols] += jnp.sum(dpi, axis=0, keepdims=True)
            vec_ref[2:3, cols] += jnp.sum(dla * r, axis=0, keepdims=True) * (RG_C * sneg)
            vec_ref[3:4, cols] += jnp.sum(dxc, axis=0, keepdims=True)
            for kk in range(4):
                vec_ref[4 + kk:5 + kk, cols] += jnp.sum(dxc * taps[kk], axis=0, keepdims=True)

    vec = pl.BlockSpec((1, D), lambda i: (0, 0))
    wsp = pl.BlockSpec((NH, DH, DH), lambda i: (0, 0, 0))
    tok = pl.BlockSpec((blk, D), lambda i: (bmap(i), 0))
    vec16 = pl.BlockSpec((16, D), lambda i: (0, 0))
    in_specs = [tok, pl.BlockSpec((4, D), lambda i: (0, 0)), wsp, wsp, vec, tok,
                pl.BlockSpec((LRU_SAVED, blk, D), lambda i: (0, bmap(i), 0)), tok,
                pl.BlockSpec((1, 1, D), lambda i: (bmap(i), 0, 0)), vec]
    args = [xin, cw, wr, wi, lam, sav[0], sav[1], h, hin, cg_fin]
    if has_dh:
        in_specs.append(tok)
        args.append(dh)
    if has_acc:
        in_specs.append(tok)
        args.append(acc_dx)
    if has_init:
        in_specs += [wsp, wsp, vec16]
        args += list(init)
    return pl.pallas_call(
        body, name=name, grid=(nb,),
        out_shape=[jax.ShapeDtypeStruct((n, D), F32), jax.ShapeDtypeStruct((NH, DH, DH), F32),
                   jax.ShapeDtypeStruct((NH, DH, DH), F32), jax.ShapeDtypeStruct((16, D), F32),
                   jax.ShapeDtypeStruct((1, D), F32)],
        in_specs=in_specs, out_specs=[tok, wsp, wsp, vec16, vec],
        scratch_shapes=[pltpu.VMEM((1, D), F32)],
    )(*args)


def _merge(z, o_f, o_b, hx, xin, tgt, mod, gn, p_a, p_b, w_out, ln_g, ln_b):
    n = xin.shape[0]
    tm = _row_tile(n, 128)

    def body(z4_ref, z6_ref, z7_ref, z8_ref, of_ref, ob_ref, hx_ref, x_ref, t_ref, mod_ref, gn_ref,
             pa_ref, pb_ref, wo_ref, lg_ref, lnb_ref,
             dr_ref, do_ref, dhx_ref, dz_ref, oa_o, obb_o, y_o, dya_o, dyb_o, dout_o, vec_ref):
        @pl.when(pl.program_id(0) == 0)
        def _():
            vec_ref[...] = jnp.zeros_like(vec_ref)

        gt = mod_ref[0:1, 2 * D:3 * D]
        gnv = gn_ref[...]
        o = of_ref[...] + ob_ref[...]
        rs = jnp.concatenate(
            [jnp.broadcast_to(lax.rsqrt(jnp.mean(jnp.square(o[:, h * DH:(h + 1) * DH]), axis=1, keepdims=True)
                                        + RMS_EPS), (tm, DH)) for h in range(NH)], axis=1)
        nrm = o * rs
        rn = nrm * gnv
        z4, z6, z7, z8 = z4_ref[...], z6_ref[...], z7_ref[...], z8_ref[...]
        s4, s6, s7, s8 = _sigmoid(z4), _sigmoid(z6), _sigmoid(z7), _sigmoid(z8)
        sg4, sg6 = z4 * s4, z6 * s6
        hxv = hx_ref[...]
        oa = (rn * sg4).astype(BF16)
        obb = (hxv * sg6).astype(BF16)
        ya = _dot(oa, pa_ref[...])
        yb = _dot(obb, pb_ref[...])
        y = (s7 * ya + s8 * yb).astype(BF16)
        out = _dot(y, wo_ref[...])
        xv = x_ref[...]
        rr = ALPHA * xv + gt * out
        mu = jnp.mean(rr, axis=1, keepdims=True)
        cen = rr - mu
        rstd = lax.rsqrt(jnp.mean(cen * cen, axis=1, keepdims=True) + LN_EPS)
        xhat = cen * rstd
        lg = lg_ref[...]
        err = xhat * lg + lnb_ref[...] - t_ref[...]
        loss_rows = jnp.sum(err * err, axis=1, keepdims=True)
        dxn = err * (1.0 / D)
        dxh = dxn * lg
        dr = rstd * (dxh - jnp.mean(dxh, axis=1, keepdims=True)
                     - xhat * jnp.mean(dxh * xhat, axis=1, keepdims=True))
        dout = (dr * gt).astype(BF16)
        dy = _dot_nt(dout, wo_ref[...])
        dya = (dy * s7).astype(BF16)
        dyb = (dy * s8).astype(BF16)
        doa = _dot_nt(dya, pa_ref[...])
        dob = _dot_nt(dyb, pb_ref[...])
        drn = doa * sg4
        dn = drn * gnv
        dnn = dn * nrm
        corr = jnp.concatenate(
            [jnp.broadcast_to(jnp.mean(dnn[:, h * DH:(h + 1) * DH], axis=1, keepdims=True), (tm, DH))
             for h in range(NH)], axis=1)
        dr_ref[...] = dr
        do_ref[...] = rs * (dn - nrm * corr)
        dhx_ref[...] = dob * sg6
        dz_ref[:, 0:4 * D] = jnp.zeros((tm, 4 * D), BF16)
        dz_ref[:, 4 * D:5 * D] = (doa * rn * _dsilu(z4, s4)).astype(BF16)
        dz_ref[:, 5 * D:6 * D] = jnp.zeros((tm, D), BF16)
        dz_ref[:, 6 * D:7 * D] = (dob * hxv * _dsilu(z6, s6)).astype(BF16)
        dz_ref[:, 7 * D:8 * D] = (dy * ya * s7 * (1.0 - s7)).astype(BF16)
        dz_ref[:, 8 * D:9 * D] = (dy * yb * s8 * (1.0 - s8)).astype(BF16)
        oa_o[...] = oa
        obb_o[...] = obb
        y_o[...] = y
        dya_o[...] = dya
        dyb_o[...] = dyb
        dout_o[...] = dout
        vec_ref[0:1, :] += jnp.sum(dr * out, axis=0, keepdims=True)
        vec_ref[1:2, :] += jnp.sum(dxn * xhat, axis=0, keepdims=True)
        vec_ref[2:3, :] += jnp.sum(dxn, axis=0, keepdims=True)
        vec_ref[3:4, :] += jnp.sum(drn * nrm, axis=0, keepdims=True)
        vec_ref[4:5, :] += jnp.broadcast_to(jnp.sum(loss_rows, axis=0, keepdims=True) * (0.5 / D), (1, D))

    def grp(g):
        return pl.BlockSpec((tm, D), lambda i: (i, g))

    tok = pl.BlockSpec((tm, D), lambda i: (i, 0))
    vec = pl.BlockSpec((1, D), lambda i: (0, 0))
    wsp = pl.BlockSpec((D, D), lambda i: (0, 0))
    return pl.pallas_call(
        body, name="merge", grid=(n // tm,),
        out_shape=[jax.ShapeDtypeStruct((n, D), F32)] * 3
        + [jax.ShapeDtypeStruct((n, NGRP * D), BF16)]
        + [jax.ShapeDtypeStruct((n, D), BF16)] * 6 + [jax.ShapeDtypeStruct((8, D), F32)],
        in_specs=[grp(4), grp(6), grp(7), grp(8), tok, tok, tok, tok, tok,
                  pl.BlockSpec((8, 3 * D), lambda i: (0, 0)), vec, wsp, wsp, wsp, vec, vec],
        out_specs=[tok, tok, tok, pl.BlockSpec((tm, NGRP * D), lambda i: (i, 0))] + [tok] * 6
        + [pl.BlockSpec((8, D), lambda i: (0, 0))],
    )(z, z, z, z, o_f, o_b, hx, xin, tgt, mod, gn, p_a, p_b, w_out, ln_g, ln_b)


def _wmod_grad(c_t, cctx_t, dmx_loc, dmc_loc, name):
    n = dmx_loc.shape[1]

    def body(ct_ref, cc_ref, dmx_ref, dmc_ref, o_ref):
        ct = ct_ref[...]
        sct = ct * _sigmoid(ct)
        cc = cc_ref[...]
        scc = cc * _sigmoid(cc)
        dmc = dmc_ref[0:1, :]
        for b in range(1, NDEV):
            dmc = dmc + dmc_ref[b:b + 1, :]
        acc = scc * dmc
        for b in range(NDEV):
            acc = acc + sct[:, b:b + 1] * dmx_ref[b:b + 1, :]
        o_ref[...] = acc

    return pl.pallas_call(body, name=name, out_shape=jax.ShapeDtypeStruct((D, n), F32))(c_t, cctx_t, dmx_loc, dmc_loc)


PACK_ROWS = 40


def _finalize_small(g_pack, lb, w_mod_full, params):
    npar = len(params)

    def body(*refs):
        gp_ref, lb_ref, wm_ref = refs[:3]
        wmv = refs[3:3 + 3 * npar]
        loss_ref = refs[3 + 3 * npar]
        g_refs = refs[4 + 3 * npar:4 + 4 * npar]
        upd = refs[4 + 4 * npar:4 + 7 * npar]
        tot = refs[-1]
        acc = gp_ref[0]
        for k in range(1, NDEV):
            acc = acc + gp_ref[k]
        tot[...] = acc
        mine = pl.ds(pl.multiple_of(_my_index() * DH, DH), DH)
        (g_cctx, g_bmod, g_bin, g_lbl, g_norm, g_cw, g_cb, g_br, g_bi, g_lam, g_lng, g_lnb) = g_refs

        loss_ref[...] = jnp.broadcast_to(tot[36:37, 0:DH], (8, DH))
        for k in range(3):
            g_bmod[:, k * D:(k + 1) * D] = tot[k:k + 1, :] + tot[3 + k:4 + k, :]
        dmc = jnp.concatenate([tot[3:4, :], tot[4:5, :], tot[5:6, :]], axis=1)
        cv = wmv[0][...]
        proj = _dot_nt(jnp.broadcast_to(dmc, (8, 3 * D)).astype(BF16), wm_ref[...])
        g_cctx[...] = proj[0:1, :] * _dsilu(cv, _sigmoid(cv))
        for k in range(NGRP):
            g_bin[:, k * D:(k + 1) * D] = tot[6 + k:7 + k, :]
        nrm = tot[15:16, 0:DH]
        for h in range(1, NH):
            nrm = nrm + tot[15:16, h * DH:(h + 1) * DH]
        g_norm[...] = nrm
        g_lng[...] = tot[16:17, :]
        g_lnb[...] = tot[17:18, :]
        g_cb[...] = tot[21:22, :] + tot[29:30, :]
        g_cw[0] = tot[22:26, mine] + tot[30:34, mine]
        for ref, row in ((g_br, 18), (g_bi, 19), (g_lam, 20)):
            ref[0, 0:1, :] = tot[row:row + 1, mine]
            ref[0, 1:2, :] = tot[row + 8:row + 9, mine]
        lbl = lb_ref[0:2, mine]
        dl0 = tot[34:36, mine] * lbl * (1.0 - lbl)
        g_lbl[0] = dl0
        g_lbl[1] = -dl0
        for p in range(npar):
            d, mm, vv = _adam_math(g_refs[p][...], wmv[3 * p][...], wmv[3 * p + 1][...], wmv[3 * p + 2][...])
            upd[3 * p][...] = d
            upd[3 * p + 1][...] = mm
            upd[3 * p + 2][...] = vv

    flat = [t for wmv in params for t in wmv]
    shapes = [jax.ShapeDtypeStruct(wmv[0].shape, F32) for wmv in params]
    outs = pl.pallas_call(
        body, name="finalize_small",
        out_shape=[jax.ShapeDtypeStruct((8, DH), F32)] + shapes + [s for s in shapes for _ in range(3)],
        scratch_shapes=[pltpu.VMEM((PACK_ROWS, D), F32)],
    )(g_pack, lb, w_mod_full, *flat)
    grads = list(outs[1:1 + npar])
    upd = [tuple(outs[1 + npar + 3 * p:4 + npar + 3 * p]) for p in range(npar)]
    return outs[0], grads, upd


def _to_colmajor(t, rows):
    return t.reshape(rows, GRID_W, D).transpose(1, 0, 2).reshape(rows * GRID_W, D)


def _to_raster(t, rows):
    return t.reshape(GRID_W, rows, D).transpose(1, 0, 2).reshape(rows * GRID_W, D)


def _local_cols(t, me, width):
    return lax.dynamic_slice_in_dim(t, me * width, width, axis=t.ndim - 1)


def kernel(x, c, ctx, c_ctx, w_mod, b_mod, w_in, b_in, lb_logits, norm_a_g, conv_w, conv_b, w_r, b_r, w_i, b_i, lam, p_a, p_b, w_out, ln_g, ln_b, loss_target, m_c_ctx, m_w_mod, m_b_mod, m_w_in, m_b_in, m_lb_logits, m_norm_a_g, m_conv_w, m_conv_b, m_w_r, m_b_r, m_w_i, m_b_i, m_lam, m_p_a, m_p_b, m_w_out, m_ln_g, m_ln_b, v_c_ctx, v_w_mod, v_b_mod, v_w_in, v_b_in, v_lb_logits, v_norm_a_g, v_conv_w, v_conv_b, v_w_r, v_b_r, v_w_i, v_b_i, v_lam, v_p_a, v_p_b, v_w_out, v_ln_g, v_ln_b):
    me = _my_index()
    xs, cs, tgt = x[0], ctx[0], loss_target[0]
    t_len, c_len = xs.shape[0], cs.shape[0]
    rows = t_len // GRID_W
    wcols = w_in.shape[2]
    mcols = w_mod.shape[2]

    small = jnp.concatenate([lb_logits.reshape(4, DH), conv_w[0], b_r[0], b_i[0], lam[0], jnp.zeros((2, DH), F32),
                             c.reshape(8, DH)], axis=0)
    g_small, g_wmod = _all_gather([small, w_mod[0].astype(BF16)], "gather_params")

    def full_rows(lo, hi):
        return g_small[:, lo:hi, :].transpose(1, 0, 2).reshape(hi - lo, D)

    lbl_f, cw_f, br_f, bi_f, lam_f = full_rows(0, 4), full_rows(4, 8), full_rows(8, 10), full_rows(10, 12), full_rows(12, 14)
    c_all = g_small[:, 16:24, :].reshape(NDEV, D)
    w_mod_f = g_wmod.transpose(1, 0, 2).reshape(D, 3 * D)
    w_r_b, w_i_b = w_r[0].astype(BF16), w_i[0].astype(BF16)

    cc = jnp.concatenate([c.reshape(1, D), c_ctx.reshape(1, D), jnp.zeros((6, D), F32)], axis=0)
    lbl_p = jnp.concatenate([lbl_f.reshape(2, 2, D), jnp.zeros((2, 6, D), F32)], axis=1)
    mod, lb = _prep(cc, w_mod_f, b_mod, lbl_p)
    u_x = _modulate(xs, mod, 0, "modulate_x")
    u_c = _modulate(cs, mod, 1, "modulate_c")
    z_x, w_in_f = _inproj_gather(u_x, w_in[0].astype(BF16), b_in, "inproj_gather")
    z_c = _mm_bias(u_c, w_in_f, b_in, "inproj_c")

    zero_s = jnp.zeros((NH, DH, DH), F32)
    zero_v = jnp.zeros((1, D), F32)
    gla = {}
    out_w = [p_a[0].astype(BF16), p_b[0].astype(BF16), w_out[0].astype(BF16)]
    for d in (0, 1):
        _, ssc, sfc = _gla_fwd(z_c, lb, zero_s, d, f"gla_fwd_c{d}")
        o_d, ssx, _, *gathered = _gla_fwd(z_x, lb, sfc, d, f"gla_fwd_x{d}", side=() if d else out_w)
        if d == 0:
            p_a_f, p_b_f, w_out_f = (t.reshape(D, D) for t in gathered)
        gla[d] = (ssc, ssx, o_d)

    x5_c = z_c[:, 5 * D:6 * D]
    x5_x = _to_colmajor(z_x[:, 5 * D:6 * D], rows)
    cb2 = conv_b.reshape(1, D)
    lru = {}
    h_sum = None
    for d in (0, 1):
        prm = (cw_f, cb2, w_r_b[d], br_f[d:d + 1], w_i_b[d], bi_f[d:d + 1], lam_f[d:d + 1])
        h_c, hin_c, hfin_c, *sav_c = _lru_fwd(x5_c, c_len, *prm, zero_v, None, d, f"lru_fwd_c{d}")
        h_x, hin_x, _, sav_a, sav_h, *h_sum = _lru_fwd(x5_x, rows, *prm, hfin_c, lru[0][3] if d else None, d,
                                                       f"lru_fwd_x{d}")
        lru[d] = ((cw_f, w_r_b[d], w_i_b[d], lam_f[d:d + 1]), h_c, hin_c, h_x, hin_x, tuple(sav_c), (sav_a, sav_h))
    hx = _to_raster(h_sum[0], rows)

    gn = jnp.tile(norm_a_g.reshape(1, DH), (1, NH))
    (dr, do, dhx, dz_m, oa, obb, yb16, dya, dyb, dout, mvec) = _merge(
        z_x, gla[0][2], gla[1][2], hx, xs, tgt, mod, gn, p_a_f, p_b_f, w_out_f, ln_g, ln_b)

    dhx_cm = _to_colmajor(dhx, rows)
    lru_dx_x = lru_dx_c = None
    for d in (0, 1):
        prm, h_c, hin_c, h_x, hin_x, sav_c, sav_x = lru[d]
        lru_dx_x, dwr, dwi, lvec, cg0 = _lru_bwd(x5_x, rows, *prm, sav_x, h_x, hin_x, dhx_cm, zero_v, lru_dx_x, None,
                                                 d, f"lru_bwd_x{d}")
        lru_dx_c, dwr, dwi, lvec, _ = _lru_bwd(x5_c, c_len, *prm, sav_c, h_c, hin_c, None, cg0, lru_dx_c,
                                               (dwr, dwi, lvec), d, f"lru_bwd_c{d}")
        lru[d] = (dwr, dwi, lvec)
    dz5_x = _to_raster(lru_dx_x, rows).astype(BF16)
    dz5_c = lru_dx_c.astype(BF16)

    dpa = _mm_tn(oa, dya, None, "dpa", out_dtype=BF16)
    dpb = _mm_tn(obb, dyb, None, "dpb", out_dtype=BF16)
    dwo = _mm_tn(yb16, dout, None, "dwout", out_dtype=BF16)
    wr_pack = jnp.concatenate([lru[0][0], lru[1][0], lru[0][1], lru[1][1]], axis=0).reshape(4 * NH * DH, DH)

    gq_c = gv_c = None
    dzf_c, dlb = {}, {}
    gq_x, dzf_x0, gv_x, dlb_x, ds0, r_pa, r_pb, r_wo, r_wri = _gla_bwd(
        z_x, lb, gla[0][1], do, zero_s, None, None, 0, "gla_bwd_x0", f_dtype=BF16,
        side=[dpa, dpb, dwo, wr_pack], side_splits=[0, 0, 0, 0])
    gq_c, dzf_c[0], gv_c, dlb_c, _ = _gla_bwd(z_c, lb, gla[0][0], None, ds0, None, None, 0, "gla_bwd_c0")
    dlb[0] = dlb_x[0:1] + dlb_c[0:1]
    dz_g, dlb_x, ds0 = _gla_bwd(z_x, lb, gla[1][1], do, zero_s, gq_x, gv_x, 1, "gla_bwd_x1", into=(dz_m, dzf_x0))
    gq_c, dzf_c[1], gv_c, dlb_c, _ = _gla_bwd(z_c, lb, gla[1][0], None, ds0, gq_c, gv_c, 1, "gla_bwd_c1")
    dlb[1] = dlb_x[0:1] + dlb_c[0:1]

    bf = lambda t: t.astype(BF16)
    dz_x = lax.dynamic_update_slice(dz_g, dz5_x, (0, 5 * D))
    zc0 = jnp.zeros((c_len, D), BF16)
    dz_c = jnp.concatenate([bf(gq_c), bf(dzf_c[0]), bf(dzf_c[1]), bf(gv_c), zc0, dz5_c, zc0, zc0, zc0], axis=1)
    dwin_c, dbin_c = _mm_tn(u_c, dz_c, None, "dwin_c", with_colsum=True)

    grad_x, xvec = _input_grad(dz_x, w_in_f, xs, dr, mod, 0, "input_grad_x")
    r_win, dbin = _dwin_exchange(u_x, dz_x, dwin_c, dbin_c, "dwin_exchange")
    _, cvec = _input_grad(dz_c, w_in_f, cs, None, mod, 1, "input_grad_c")
    wri_piece = _sum_rows(r_wri, "sum_w_ri_piece")
    g_w_in, d_w_in, nm_w_in, nv_w_in = _sum_adamw(r_win, w_in, m_w_in, v_w_in, "update_w_in")
    g_p_a, d_p_a, nm_p_a, nv_p_a = _sum_adamw(r_pa, p_a, m_p_a, v_p_a, "update_p_a")
    g_p_b, d_p_b, nm_p_b, nv_p_b = _sum_adamw(r_pb, p_b, m_p_b, v_p_b, "update_p_b")
    g_w_out, d_w_out, nm_w_out, nv_w_out = _sum_adamw(r_wo, w_out, m_w_out, v_w_out, "update_w_out")

    dlb_rows = jnp.concatenate([dlb[0], dlb[1]], axis=0)
    pack = jnp.concatenate([
        xvec[0:1], xvec[1:2], mvec[0:1],
        cvec[0:1], cvec[1:2], jnp.zeros((1, D), F32),
        dbin.reshape(NGRP, D),
        mvec[3:4], mvec[1:2], mvec[2:3],
        lru[0][2][0:8], lru[1][2][0:3],
        lru[1][2][3:8],
        dlb_rows,
        mvec[4:5],
        jnp.zeros((3, D), F32)], axis=0)
    g_pack, g_wri = _all_gather([pack, wri_piece], "gather_small_grads")

    dmx = g_pack[:, 0:3, :].reshape(NDEV, 3 * D)
    dmc = g_pack[:, 3:6, :].reshape(NDEV, 3 * D)
    grad_w_mod = _wmod_grad(c_all.T, c_ctx.reshape(D, 1), _local_cols(dmx, me, mcols), _local_cols(dmc, me, mcols),
                            "grad_w_mod").reshape(1, D, mcols)
    small_params = [(c_ctx.reshape(1, D), m_c_ctx.reshape(1, D), v_c_ctx.reshape(1, D)), (b_mod, m_b_mod, v_b_mod),
                    (b_in, m_b_in, v_b_in), (lb_logits, m_lb_logits, v_lb_logits), (norm_a_g, m_norm_a_g, v_norm_a_g),
                    (conv_w, m_conv_w, v_conv_w), (conv_b, m_conv_b, v_conv_b), (b_r, m_b_r, v_b_r),
                    (b_i, m_b_i, v_b_i), (lam, m_lam, v_lam), (ln_g, m_ln_g, v_ln_g), (ln_b, m_ln_b, v_ln_b)]
    loss_tile, small_g, small_upd = _finalize_small(g_pack, lb, w_mod_f, small_params)
    loss = loss_tile[0, 0]
    (grad_c_ctx, grad_b_mod, grad_b_in, grad_lb_logits, grad_norm_a_g, grad_conv_w, grad_conv_b, grad_b_r, grad_b_i,
     grad_lam, grad_ln_g, grad_ln_b) = small_g
    small_upd[0] = tuple(t.reshape(c_ctx.shape) for t in small_upd[0])
    (o_c_ctx, o_b_mod, o_b_in, o_lb, o_norm, o_conv_w, o_conv_b, o_b_r, o_b_i, o_lam, o_ln_g, o_ln_b) = small_upd

    half = 2 * NH * DH
    g_ri = g_wri.reshape(2 * half, DH)
    grad_w_r, grad_w_i = g_ri[:half].reshape(w_r.shape), g_ri[half:].reshape(w_i.shape)
    d_w_r, nm_w_r, nv_w_r = _adamw(grad_w_r, w_r, m_w_r, v_w_r, "update_w_r")
    d_w_i, nm_w_i, nv_w_i = _adamw(grad_w_i, w_i, m_w_i, v_w_i, "update_w_i")

    d_w_mod, nm_w_mod, nv_w_mod = _adamw(grad_w_mod, w_mod, m_w_mod, v_w_mod, "update_w_mod")

    grads = [grad_c_ctx.reshape(c_ctx.shape), grad_w_mod, grad_b_mod, g_w_in, grad_b_in, grad_lb_logits, grad_norm_a_g,
             grad_conv_w, grad_conv_b, grad_w_r, grad_b_r, grad_w_i, grad_b_i, grad_lam, g_p_a, g_p_b, g_w_out,
             grad_ln_g, grad_ln_b]
    per_kind = []
    for k in range(3):
        per_kind.append([
            o_c_ctx[k], (d_w_mod, nm_w_mod, nv_w_mod)[k], o_b_mod[k], (d_w_in, nm_w_in, nv_w_in)[k], o_b_in[k], o_lb[k],
            o_norm[k], o_conv_w[k], o_conv_b[k], (d_w_r, nm_w_r, nv_w_r)[k], o_b_r[k], (d_w_i, nm_w_i, nv_w_i)[k],
            o_b_i[k], o_lam[k], (d_p_a, nm_p_a, nv_p_a)[k], (d_p_b, nm_p_b, nv_p_b)[k], (d_w_out, nm_w_out, nv_w_out)[k],
            o_ln_g[k], o_ln_b[k]])
    return (loss, grad_x.reshape(x.shape), *grads, *per_kind[0], *per_kind[1], *per_kind[2])
```

```python
import functools

import jax
import jax.numpy as jnp
from jax import lax
from jax.experimental import pallas as pl
from jax.experimental.pallas import tpu as pltpu

F32 = jnp.float32
BF16 = jnp.bfloat16

D = 1024
NH = 8
DH = 128
CHUNK = 64
GLA_HEADS_PER_STEP = 8
GRID_W = 64
NGRP = 9
NDEV = 8
RG_C = 8.0
ALPHA = 2.0 ** 0.25
LN_EPS = 1e-5
RMS_EPS = 1e-6
Q_SCALE = DH ** -0.5
ADAM_LR, ADAM_B1, ADAM_B2, ADAM_EPS, ADAM_WD, ADAM_STEP = 1e-3, 0.9, 0.999, 1e-8, 0.01, 10
ADAM_C1 = 1.0 / (1.0 - ADAM_B1 ** ADAM_STEP)
ADAM_C2 = 1.0 / (1.0 - ADAM_B2 ** ADAM_STEP)

ANY = pl.BlockSpec(memory_space=pl.ANY)


def _sigmoid(t):
    return 1.0 / (1.0 + jnp.exp(-t))


def _dsilu(t, s):
    return s * (1.0 + t * (1.0 - s))


def _dot(a, b):
    return jnp.dot(a, b, preferred_element_type=F32)


def _dot_nt(a, b):
    return lax.dot_general(a, b, (((1,), (1,)), ((), ())), preferred_element_type=F32)


def _dot_tn(a, b):
    return lax.dot_general(a, b, (((0,), (0,)), ((), ())), preferred_element_type=F32)


def _my_index():
    return 4 * lax.axis_index("x") + 2 * lax.axis_index("y") + lax.axis_index("c")


def _dev_tuple(j):
    return (j >> 2, (j >> 1) & 1, j & 1)


def _exchange_sems(n):
    return [pltpu.SemaphoreType.DMA((n * NDEV,)), pltpu.SemaphoreType.DMA((n * NDEV,)), pltpu.SemaphoreType.DMA((n,))]


def _exchange(ins, outs, sems, piece_of=None):
    send_sems, recv_sems, loc_sems = sems
    n = len(ins)

    def src(a, p):
        return ins[a] if piece_of is None else piece_of(ins[a], a, p)

    def push(a, t):
        me, p = _my_index(), _step_peer(t)
        return pltpu.make_async_remote_copy(
            src_ref=src(a, p), dst_ref=outs[a].at[me],
            send_sem=send_sems.at[a * NDEV + t], recv_sem=recv_sems.at[a * NDEV + me],
            device_id=_dev_of(p), device_id_type=pl.DeviceIdType.MESH)

    def local(a):
        me = _my_index()
        return pltpu.make_async_copy(src(a, me), outs[a].at[me], loc_sems.at[a])

    def start():
        for a in range(n):
            local(a).start()
        for t in range(NDEV - 1):
            for a in range(n):
                push(a, t).start()

    def finish():
        me = _my_index()
        for t in range(NDEV - 1):
            for a in range(n):
                push(a, t).wait_send()
        for j in range(NDEV):
            @pl.when(me != j)
            def _():
                for a in range(n):
                    pltpu.make_async_remote_copy(
                        src_ref=src(a, j), dst_ref=outs[a].at[j],
                        send_sem=send_sems.at[a * NDEV], recv_sem=recv_sems.at[a * NDEV + j],
                        device_id=_dev_tuple(j), device_id_type=pl.DeviceIdType.MESH).wait_recv()
        for a in range(n):
            local(a).wait()

    return start, finish


def _all_gather(shards, name):
    n = len(shards)

    def body(*refs):
        start, finish = _exchange(refs[:n], refs[n:2 * n], refs[2 * n:])
        start()
        finish()

    return pl.pallas_call(
        body, name=name,
        out_shape=[jax.ShapeDtypeStruct((NDEV,) + s.shape, s.dtype) for s in shards],
        in_specs=[ANY] * n, out_specs=[ANY] * n, scratch_shapes=_exchange_sems(n),
    )(*shards)


def _pieces(parts, splits):
    shapes = []
    for part, split in zip(parts, splits):
        r, c = part.shape
        shapes.append((r // NDEV, c) if split == 0 else (r, c // NDEV))

    def piece_of(ref, a, j):
        pr, pc = shapes[a]
        if splits[a] == 0:
            start = j * pr if isinstance(j, int) else pl.multiple_of(j * pr, pr)
            return ref.at[pl.ds(start, pr), :]
        start = j * pc if isinstance(j, int) else pl.multiple_of(j * pc, pc)
        return ref.at[:, pl.ds(start, pc)]

    return shapes, piece_of


_STEP_MASKS = ((2, 4, 6, 3, 5, 7, 1, 0), (4, 2, 6, 5, 3, 7, 1, 0))
_GATHER_MASKS = ((0, 1, 2, 4, 3, 5, 6, 7), (0, 1, 4, 2, 5, 3, 6, 7))


def _peer_schedule(table):
    tab = jnp.array(table, jnp.int32)
    return jnp.bitwise_xor(_my_index(), tab[lax.axis_index("c")])


def _step_peer(s, table=_STEP_MASKS):
    def pick(row):
        if isinstance(s, int):
            return jnp.int32(row[s])
        m = jnp.int32(row[NDEV - 1])
        for t in range(NDEV - 2, -1, -1):
            m = jnp.where(s == t, jnp.int32(row[t]), m)
        return m
    mask = jnp.where(lax.axis_index("c") == 0, pick(table[0]), pick(table[1]))
    return jnp.bitwise_xor(_my_index(), mask)


def _dev_of(p):
    return (p // 4, (p // 2) % 2, p % 2)


def _dwin_exchange(u, dz, init, cs_init, name):
    m, ka = u.shape
    n = dz.shape[1]
    pc = n // NDEV
    tk = _row_tile(m, 512)
    nk = m // tk

    def body(pidx_ref, u_ref, dz_ref, init_ref, csi_ref, rwin, cs_ref, acc, sbuf, wsend, wrecv, wloc):
        s, k = pl.program_id(0), pl.program_id(1)
        me = _my_index()

        def slab_copy(slot, p):
            return pltpu.make_async_remote_copy(
                src_ref=sbuf.at[slot], dst_ref=rwin.at[me], send_sem=wsend.at[slot], recv_sem=wrecv.at[me],
                device_id=_dev_of(p), device_id_type=pl.DeviceIdType.MESH)

        @pl.when(k == 0)
        def _():
            acc[...] = init_ref[...]
            cs_ref[...] = csi_ref[...]

        bv = dz_ref[...]
        acc[...] += _dot_tn(u_ref[...], bv)
        cs_ref[...] += jnp.sum(bv.astype(F32), axis=0, keepdims=True)

        @pl.when(k == nk - 1)
        def _():
            slot = s % 2

            @pl.when(s >= 2)
            def _():
                slab_copy(slot, me).wait_send()

            sbuf[slot] = acc[...].astype(BF16)

            @pl.when(s < NDEV - 1)
            def _():
                slab_copy(slot, _step_peer(s)).start()

            @pl.when(s == NDEV - 1)
            def _():
                own = pltpu.make_async_copy(sbuf.at[slot], rwin.at[me], wloc.at[0])
                own.start()
                slab_copy(1 - slot, me).wait_send()
                for j in range(NDEV):
                    @pl.when(me != j)
                    def _():
                        pltpu.make_async_remote_copy(
                            src_ref=sbuf.at[0], dst_ref=rwin.at[j], send_sem=wsend.at[0], recv_sem=wrecv.at[j],
                            device_id=_dev_tuple(j), device_id_type=pl.DeviceIdType.MESH).wait_recv()
                own.wait()

    grid_spec = pltpu.PrefetchScalarGridSpec(
        num_scalar_prefetch=1, grid=(NDEV, nk),
        in_specs=[pl.BlockSpec((tk, ka), lambda s, k, pidx: (k, 0)),
                  pl.BlockSpec((tk, pc), lambda s, k, pidx: (k, pidx[s])),
                  pl.BlockSpec((ka, pc), lambda s, k, pidx: (0, pidx[s])),
                  pl.BlockSpec((1, pc), lambda s, k, pidx: (0, pidx[s]))],
        out_specs=[ANY, pl.BlockSpec((1, pc), lambda s, k, pidx: (0, pidx[s]))],
        scratch_shapes=[pltpu.VMEM((ka, pc), F32), pltpu.VMEM((2, ka, pc), BF16),
                        pltpu.SemaphoreType.DMA((2,)), pltpu.SemaphoreType.DMA((NDEV,)), pltpu.SemaphoreType.DMA((1,))])
    return pl.pallas_call(
        body, name=name, grid_spec=grid_spec,
        out_shape=[jax.ShapeDtypeStruct((NDEV, ka, pc), BF16), jax.ShapeDtypeStruct((1, n), F32)],
    )(_peer_schedule(_STEP_MASKS), u, dz, init, cs_init)


def _inproj_gather(u, w_loc, bias, name):
    m, k = u.shape
    pc = w_loc.shape[1]
    n = pc * NDEV
    tm = _row_tile(m, 512)
    ni = m // tm

    def body(pidx_ref, u_ref, b_ref, wl_ref, z_ref, wall, wbuf, wsend, wrecv, ldsem, ownsem):
        s, i = pl.program_id(0), pl.program_id(1)
        me = _my_index()

        def shard_push(t):
            return pltpu.make_async_remote_copy(
                src_ref=wl_ref, dst_ref=wall.at[me], send_sem=wsend.at[t], recv_sem=wrecv.at[me],
                device_id=_dev_of(_step_peer(t, _GATHER_MASKS)), device_id_type=pl.DeviceIdType.MESH)

        def load(slot, src):
            return pltpu.make_async_copy(src, wbuf.at[slot], ldsem.at[slot])

        own = pltpu.make_async_copy(wl_ref, wall.at[me], ownsem.at[0])

        @pl.when((s == 0) & (i == 0))
        def _():
            own.start()
            load(0, wl_ref).start()
            for t in range(1, NDEV):
                shard_push(t).start()

        @pl.when((i == ni // 2) & (s < NDEV - 1))
        def _():
            nxt = _step_peer(s + 1, _GATHER_MASKS)
            pltpu.make_async_remote_copy(
                src_ref=wl_ref, dst_ref=wall.at[nxt], send_sem=wsend.at[0], recv_sem=wrecv.at[nxt],
                device_id=_dev_of(nxt), device_id_type=pl.DeviceIdType.MESH).wait_recv()
            load((s + 1) % 2, wall.at[nxt]).start()

        @pl.when(i == 0)
        def _():
            load(s % 2, wl_ref).wait()

        z_ref[...] = _dot(u_ref[...], wbuf[s % 2]) + b_ref[...]

        @pl.when((s == NDEV - 1) & (i == ni - 1))
        def _():
            own.wait()
            for t in range(1, NDEV):
                shard_push(t).wait_send()

    grid_spec = pltpu.PrefetchScalarGridSpec(
        num_scalar_prefetch=1, grid=(NDEV, ni),
        in_specs=[pl.BlockSpec((tm, k), lambda s, i, pidx: (i, 0)),
                  pl.BlockSpec((1, pc), lambda s, i, pidx: (0, pidx[s])), ANY],
        out_specs=[pl.BlockSpec((tm, pc), lambda s, i, pidx: (i, pidx[s])), ANY],
        scratch_shapes=[pltpu.VMEM((2, k, pc), BF16),
                        pltpu.SemaphoreType.DMA((NDEV,)), pltpu.SemaphoreType.DMA((NDEV,)),
                        pltpu.SemaphoreType.DMA((2,)), pltpu.SemaphoreType.DMA((1,))])
    return pl.pallas_call(
        body, name=name, grid_spec=grid_spec,
        out_shape=[jax.ShapeDtypeStruct((m, n), F32), jax.ShapeDtypeStruct((NDEV, k, pc), w_loc.dtype)],
    )(_peer_schedule(_GATHER_MASKS), u, bias, w_loc)


def _adam_math(g, w, m, v):
    m2 = ADAM_B1 * m + (1.0 - ADAM_B1) * g
    v2 = ADAM_B2 * v + (1.0 - ADAM_B2) * (g * g)
    delta = -ADAM_LR * ((m2 * ADAM_C1) / (jnp.sqrt(v2 * ADAM_C2) + ADAM_EPS) + ADAM_WD * w)
    return delta, m2, v2


def _row_tile(r, cap):
    t = min(r, cap)
    while r % t:
        t //= 2
    return t


def _adamw(g, w, m, v, name):
    shape = w.shape
    cols = shape[-1] if w.ndim >= 2 and shape[-1] % 128 == 0 else 128
    g2, w2, m2, v2 = (t.reshape(-1, cols) for t in (g, w, m, v))
    r = g2.shape[0]
    tr = _row_tile(r, 256)

    def body(g_ref, w_ref, m_ref, v_ref, d_ref, mo_ref, vo_ref):
        d, mm, vv = _adam_math(g_ref[...], w_ref[...], m_ref[...], v_ref[...])
        d_ref[...] = d
        mo_ref[...] = mm
        vo_ref[...] = vv

    spec = pl.BlockSpec((tr, cols), lambda i: (i, 0))
    outs = pl.pallas_call(
        body, name=name, grid=(r // tr,),
        out_shape=[jax.ShapeDtypeStruct((r, cols), F32)] * 3,
        in_specs=[spec] * 4, out_specs=[spec] * 3,
    )(g2, w2, m2, v2)
    return tuple(o.reshape(shape) for o in outs)


def _sum_adamw(parts, w, m, v, name):
    _, r, c = parts.shape
    shape = w.shape
    w2, m2, v2 = (t.reshape(r, c) for t in (w, m, v))
    tr = _row_tile(r, 128)

    def body(p_ref, w_ref, m_ref, v_ref, g_ref, d_ref, mo_ref, vo_ref):
        g = p_ref[0].astype(F32)
        for k in range(1, NDEV):
            g = g + p_ref[k].astype(F32)
        d, mm, vv = _adam_math(g, w_ref[...], m_ref[...], v_ref[...])
        g_ref[...] = g
        d_ref[...] = d
        mo_ref[...] = mm
        vo_ref[...] = vv

    spec = pl.BlockSpec((tr, c), lambda i: (i, 0))
    outs = pl.pallas_call(
        body, name=name, grid=(r // tr,),
        out_shape=[jax.ShapeDtypeStruct((r, c), F32)] * 4,
        in_specs=[pl.BlockSpec((NDEV, tr, c), lambda i: (0, i, 0))] + [spec] * 3, out_specs=[spec] * 4,
    )(parts, w2, m2, v2)
    return tuple(o.reshape(shape) for o in outs)


def _sum_rows(parts, name):
    _, r, c = parts.shape

    def body(p_ref, o_ref):
        g = p_ref[0]
        for k in range(1, NDEV):
            g = g + p_ref[k]
        o_ref[...] = g

    return pl.pallas_call(
        body, name=name, out_shape=jax.ShapeDtypeStruct((r, c), F32),
    )(parts)


def _prep(cc, w_mod_full, b_mod, lbl):
    def body(cc_ref, w_ref, b_ref, l_ref, mod_ref, lb_ref):
        t = cc_ref[...]
        s = (t * _sigmoid(t)).astype(BF16)
        mod_ref[...] = _dot(s, w_ref[...]) + b_ref[...]
        lb_ref[...] = _sigmoid(l_ref[0] - l_ref[1])

    return pl.pallas_call(
        body, name="prep",
        out_shape=[jax.ShapeDtypeStruct((8, 3 * D), F32), jax.ShapeDtypeStruct((8, D), F32)],
    )(cc, w_mod_full, b_mod, lbl)


def _modulate(xin, mod, row, name):
    n = xin.shape[0]
    tm = _row_tile(n, 512)

    def body(x_ref, mod_ref, u_ref):
        sh = mod_ref[row:row + 1, 0:D]
        sc = mod_ref[row:row + 1, D:2 * D]
        u_ref[...] = (x_ref[...] * (1.0 + sc) + sh).astype(BF16)

    return pl.pallas_call(
        body, name=name, grid=(n // tm,),
        out_shape=jax.ShapeDtypeStruct((n, D), BF16),
        in_specs=[pl.BlockSpec((tm, D), lambda i: (i, 0)), pl.BlockSpec((8, 3 * D), lambda i: (0, 0))],
        out_specs=pl.BlockSpec((tm, D), lambda i: (i, 0)),
    )(xin, mod)


def _mm_bias(a, w_all, bias, name):
    m, k = a.shape
    tn = w_all.shape[2]
    n = tn * NDEV
    tm = _row_tile(m, 512)

    def body(a_ref, b_ref, bias_ref, o_ref):
        o_ref[...] = _dot(a_ref[...], b_ref[0]) + bias_ref[...]

    return pl.pallas_call(
        body, name=name, grid=(NDEV, m // tm),
        out_shape=jax.ShapeDtypeStruct((m, n), F32),
        in_specs=[pl.BlockSpec((tm, k), lambda j, i: (i, 0)), pl.BlockSpec((1, k, tn), lambda j, i: (j, 0, 0)),
                  pl.BlockSpec((1, tn), lambda j, i: (0, j))],
        out_specs=pl.BlockSpec((tm, tn), lambda j, i: (i, j)),
    )(a, w_all, bias)


def _mm_tn(a, b, init, name, with_colsum=False, colsum_init=None, out_dtype=F32):
    m, ka = a.shape
    n = b.shape[1]
    tk = _row_tile(m, 512)
    tn = 1024
    nk = m // tk
    has_init = init is not None

    def body(*refs):
        a_ref, b_ref = refs[0], refs[1]
        pos = 2
        init_ref = cs_init_ref = None
        if has_init:
            init_ref = refs[pos]
            pos += 1
            if with_colsum:
                cs_init_ref = refs[pos]
                pos += 1
        o_ref = refs[pos]
        cs_ref = refs[pos + 1] if with_colsum else None
        acc = refs[-1]
        k = pl.program_id(1)

        @pl.when(k == 0)
        def _():
            if has_init:
                acc[...] = init_ref[...]
                if with_colsum:
                    cs_ref[...] = cs_init_ref[...]
            else:
                acc[...] = jnp.zeros_like(acc)
                if with_colsum:
                    cs_ref[...] = jnp.zeros_like(cs_ref)

        bv = b_ref[...]
        acc[...] += _dot_tn(a_ref[...], bv)
        if with_colsum:
            cs_ref[...] += jnp.sum(bv.astype(F32), axis=0, keepdims=True)

        @pl.when(k == nk - 1)
        def _():
            o_ref[...] = acc[...].astype(out_dtype)

    in_specs = [pl.BlockSpec((tk, ka), lambda j, k: (k, 0)), pl.BlockSpec((tk, tn), lambda j, k: (k, j))]
    args = [a, b]
    if has_init:
        in_specs.append(pl.BlockSpec((ka, tn), lambda j, k: (0, j)))
        args.append(init)
        if with_colsum:
            in_specs.append(pl.BlockSpec((1, tn), lambda j, k: (0, j)))
            args.append(colsum_init)
    out_shape = [jax.ShapeDtypeStruct((ka, n), out_dtype)]
    out_specs = [pl.BlockSpec((ka, tn), lambda j, k: (0, j))]
    if with_colsum:
        out_shape.append(jax.ShapeDtypeStruct((1, n), F32))
        out_specs.append(pl.BlockSpec((1, tn), lambda j, k: (0, j)))
    outs = pl.pallas_call(
        body, name=name, grid=(n // tn, nk), out_shape=out_shape, in_specs=in_specs, out_specs=out_specs,
        scratch_shapes=[pltpu.VMEM((ka, tn), F32)],
    )(*args)
    return outs if with_colsum else outs[0]


def _input_grad(dz, w_all, xin, dr, mod, row, name, side=(), side_splits=()):
    m, n = dz.shape
    tm = _row_tile(m, 512)
    tk = w_all.shape[2]
    nk = NDEV
    ni = m // tm
    has_dr = dr is not None
    ns = len(side)
    piece_shapes, piece_of = _pieces(side, side_splits)

    def body(*refs):
        dz_ref, w_ref, x_ref = refs[:3]
        pos = 3
        dr_ref = refs[pos] if has_dr else None
        pos += int(has_dr)
        mod_ref = refs[pos]
        side_in = refs[pos + 1:pos + 1 + ns]
        pos += 1 + ns
        gx_ref = refs[pos] if has_dr else None
        pos += int(has_dr)
        vec_ref = refs[pos]
        side_out = refs[pos + 1:pos + 1 + ns]
        acc = refs[pos + 1 + ns]
        i, k = pl.program_id(0), pl.program_id(1)
        if ns:
            side_start, side_finish = _exchange(side_in, side_out, refs[pos + 2 + ns:], piece_of)

            @pl.when((i == 0) & (k == 0))
            def _():
                side_start()

        @pl.when(k == 0)
        def _():
            acc[...] = jnp.zeros_like(acc)

        @pl.when((i == 0) & (k == 0))
        def _():
            vec_ref[...] = jnp.zeros_like(vec_ref)

        acc[...] += _dot_nt(dz_ref[...], w_ref[0])

        @pl.when(k == nk - 1)
        def _():
            du = acc[...]
            xv = x_ref[...]
            if has_dr:
                sc = mod_ref[row:row + 1, D:2 * D]
                gx_ref[...] = ALPHA * dr_ref[...] + du * (1.0 + sc)
            vec_ref[0:1, :] += jnp.sum(du, axis=0, keepdims=True)
            vec_ref[1:2, :] += jnp.sum(du * xv, axis=0, keepdims=True)

        if ns:
            @pl.when((i == ni - 1) & (k == nk - 1))
            def _():
                side_finish()

    row_spec = pl.BlockSpec((tm, D), lambda i, k: (i, 0))
    in_specs = [pl.BlockSpec((tm, tk), lambda i, k: (i, k)), pl.BlockSpec((1, D, tk), lambda i, k: (k, 0, 0)), row_spec]
    args = [dz, w_all, xin]
    if has_dr:
        in_specs.append(row_spec)
        args.append(dr)
    in_specs.append(pl.BlockSpec((8, 3 * D), lambda i, k: (0, 0)))
    args.append(mod)
    in_specs += [ANY] * ns
    args += list(side)
    out_shape, out_specs = [], []
    if has_dr:
        out_shape.append(jax.ShapeDtypeStruct((m, D), F32))
        out_specs.append(row_spec)
    out_shape.append(jax.ShapeDtypeStruct((8, D), F32))
    out_specs.append(pl.BlockSpec((8, D), lambda i, k: (0, 0)))
    out_shape += [jax.ShapeDtypeStruct((NDEV,) + piece_shapes[a], side[a].dtype) for a in range(ns)]
    out_specs += [ANY] * ns
    outs = pl.pallas_call(
        body, name=name, grid=(ni, nk), out_shape=out_shape, in_specs=in_specs, out_specs=out_specs,
        scratch_shapes=[pltpu.VMEM((tm, D), F32)] + (_exchange_sems(ns) if ns else []),
    )(*args)
    return tuple(outs) if has_dr else (None, *outs)


def _tri(reverse):
    r = lax.broadcasted_iota(jnp.int32, (CHUNK, CHUNK), 0)
    c = lax.broadcasted_iota(jnp.int32, (CHUNK, CHUNK), 1)
    return (c >= r) if reverse else (c <= r)


def _cum_f32(tri_b, t):
    hi = t.astype(BF16)
    r1 = t - hi.astype(F32)
    mid = r1.astype(BF16)
    lo = (r1 - mid.astype(F32)).astype(BF16)
    return _dot(tri_b, hi) + _dot(tri_b, mid) + _dot(tri_b, lo)


def _gla_features(zq, zf, lb):
    sq = _sigmoid(zq)
    q = zq * sq * Q_SCALE
    sf = _sigmoid(zf)
    f = lb + (1.0 - lb) * sf
    return q, sq, f, sf


def _gla_decays(f, tri_b, last):
    lf = jnp.log(f)
    g = _cum_f32(tri_b, lf)
    gl = g[last:last + 1, :]
    return g, gl


def _gla_block(n):
    return 256 if n % 256 == 0 else CHUNK


def _gla_fwd(z, lb, s0, d, name, side=()):
    n = z.shape[0]
    blk = _gla_block(n)
    nb, npb = n // blk, blk // CHUNK
    reverse = d == 1
    last = 0 if reverse else CHUNK - 1
    order = list(range(npb))[::-1] if reverse else list(range(npb))
    ns = len(side)

    def bmap(i):
        return nb - 1 - i if reverse else i

    hp = GLA_HEADS_PER_STEP
    hw = hp * DH
    units = [(hh, cidx) for hh in range(hp) for cidx in order]

    def body(zq_ref, zf_ref, zv_ref, lb_ref, s0_ref, *rest):
        side_in = rest[:ns]
        o_ref, ss_ref, sf_ref = rest[ns:ns + 3]
        side_out = rest[ns + 3:2 * ns + 3]
        st = rest[2 * ns + 3]
        i = pl.program_id(1)
        if ns:
            side_start, side_finish = _exchange(side_in, side_out, rest[2 * ns + 4:])

            @pl.when((pl.program_id(0) == 0) & (i == 0))
            def _():
                side_start()

        @pl.when(i == 0)
        def _():
            st[...] = s0_ref[...]

        mask = _tri(reverse)
        tri_b = jnp.where(mask, 1.0, 0.0).astype(BF16)
        feat = {}
        for u in units:
            hh, cidx = u
            rows, cols = pl.ds(cidx * CHUNK, CHUNK), pl.ds(hh * DH, DH)
            q, _, f, _ = _gla_features(zq_ref[rows, cols], zf_ref[rows, cols], lb_ref[d:d + 1, cols])
            feat[u] = (q, 1.0 - f, jnp.log(f), zv_ref[rows, cols].astype(BF16))
        dec = {u: _cum_f32(tri_b, feat[u][2]) for u in units}
        ops = {}
        for u in units:
            q, k, _, vb = feat[u]
            g = dec[u]
            gl = g[last:last + 1, :]
            ops[u] = ((q * jnp.exp(g)).astype(BF16), (k * jnp.exp(-g)).astype(BF16),
                      (k * jnp.exp(gl - g)).astype(BF16), jnp.exp(gl), vb)
        att = {u: jnp.where(mask, _dot_nt(ops[u][0], ops[u][1]), 0.0).astype(BF16) for u in units}
        upd = {u: _dot_tn(ops[u][4], ops[u][2]) for u in units}
        intra = {u: _dot(att[u], ops[u][4]) for u in units}
        s_in = {}
        for hh in range(hp):
            s = st[hh]
            ss_ref[hh, 0] = s
            for cidx in order:
                s_in[(hh, cidx)] = s
                s = s * ops[(hh, cidx)][3] + upd[(hh, cidx)]
            st[hh] = s
            sf_ref[hh] = s
        for u in units:
            hh, cidx = u
            rows, cols = pl.ds(cidx * CHUNK, CHUNK), pl.ds(hh * DH, DH)
            o_ref[rows, cols] = intra[u] + _dot_nt(ops[u][0], s_in[u].astype(BF16))

        if ns:
            @pl.when((pl.program_id(0) == NH // hp - 1) & (i == nb - 1))
            def _():
                side_finish()

    def col(g):
        return lambda h, i: (bmap(i), g * (NH // hp) + h)

    return pl.pallas_call(
        body, name=name, grid=(NH // hp, nb),
        out_shape=[jax.ShapeDtypeStruct((n, D), F32), jax.ShapeDtypeStruct((NH, nb, DH, DH), F32),
                   jax.ShapeDtypeStruct((NH, DH, DH), F32)]
        + [jax.ShapeDtypeStruct((NDEV,) + t.shape, t.dtype) for t in side],
        in_specs=[pl.BlockSpec((blk, hw), col(0)), pl.BlockSpec((blk, hw), col(1 + d)),
                  pl.BlockSpec((blk, hw), col(3)), pl.BlockSpec((8, hw), lambda h, i: (0, h)),
                  pl.BlockSpec((hp, DH, DH), lambda h, i: (h, 0, 0))] + [ANY] * ns,
        out_specs=[pl.BlockSpec((blk, hw), lambda h, i: (bmap(i), h)),
                   pl.BlockSpec((hp, 1, DH, DH), lambda h, i: (h, bmap(i), 0, 0)),
                   pl.BlockSpec((hp, DH, DH), lambda h, i: (h, 0, 0))] + [ANY] * ns,
        scratch_shapes=[pltpu.VMEM((hp, DH, DH), F32)] + (_exchange_sems(ns) if ns else []),
    )(z, z, z, lb, s0, *side)


def _gla_bwd(z, lb, s_start, do, ds_fin, acc_q, acc_v, d, name, f_dtype=F32, into=None, side=(), side_splits=()):
    n = z.shape[0]
    blk = _gla_block(n)
    nb, npb = n // blk, blk // CHUNK
    reverse = d == 1
    last = 0 if reverse else CHUNK - 1
    order = list(range(npb)) if reverse else list(range(npb))[::-1]
    has_do = do is not None
    has_acc = acc_q is not None
    fused = into is not None
    assert not fused or d == 1
    ns = len(side)
    assert not (fused and ns)
    piece_shapes, piece_of = _pieces(side, side_splits)
    hp = NH if fused else GLA_HEADS_PER_STEP
    hw = hp * DH
    units = [(hh, cidx) for hh in range(hp) for cidx in order]

    def bmap(i):
        return i if reverse else nb - 1 - i

    def body(*refs):
        zq_ref, zf_ref, zv_ref, lb_ref, ss_ref, dsf_ref = refs[:6]
        pos = 6
        do_ref = aq_ref = av_ref = None
        if has_do:
            do_ref = refs[pos]
            pos += 1
        if has_acc:
            aq_ref, av_ref = refs[pos], refs[pos + 1]
            pos += 2
        if fused:
            other_ref = refs[pos + 1]
            dz_ref, dlb_ref, ds0_ref, dst = refs[pos + 2:]
            dz_ref[:, D:2 * D] = other_ref[...]
        else:
            side_in = refs[pos:pos + ns]
            dzq_ref, dzf_ref, dzv_ref, dlb_ref, ds0_ref = refs[pos + ns:pos + ns + 5]
            side_out = refs[pos + ns + 5:pos + 2 * ns + 5]
            dst = refs[pos + 2 * ns + 5]
        i = pl.program_id(1)
        if ns:
            side_start, side_finish = _exchange(side_in, side_out, refs[pos + 2 * ns + 6:], piece_of)

            @pl.when((pl.program_id(0) == 0) & (i == 0))
            def _():
                side_start()

        @pl.when(i == 0)
        def _():
            dst[...] = dsf_ref[...]
            dlb_ref[...] = jnp.zeros_like(dlb_ref)

        mask = _tri(reverse)
        tri_b = jnp.where(mask, 1.0, 0.0).astype(BF16)
        tri_t = jnp.where(_tri(not reverse), 1.0, 0.0).astype(BF16)

        def where(u):
            return pl.ds(u[1] * CHUNK, CHUNK), pl.ds(u[0] * DH, DH)

        feat = {}
        for u in units:
            rows, cols = where(u)
            zq, zf = zq_ref[rows, cols], zf_ref[rows, cols]
            lbv = lb_ref[d:d + 1, cols]
            q, sq, f, sf = _gla_features(zq, zf, lbv)
            feat[u] = dict(zq=zq, q=q, sq=sq, f=f, sf=sf, lbv=lbv, k=1.0 - f, vb=zv_ref[rows, cols].astype(BF16))
        dec = {u: _cum_f32(tri_b, jnp.log(feat[u]["f"])) for u in units}
        for u in units:
            w = feat[u]
            g = dec[u]
            gl = g[last:last + 1, :]
            w["eg"], w["egi"], w["ege"], w["egl"] = jnp.exp(g), jnp.exp(-g), jnp.exp(gl - g), jnp.exp(gl)
            w["qd"], w["ki"], w["ke"] = w["q"] * w["eg"], w["k"] * w["egi"], w["k"] * w["ege"]
            w["qdb"], w["kib"], w["keb"] = w["qd"].astype(BF16), w["ki"].astype(BF16), w["ke"].astype(BF16)
        fwd_order = order[::-1]
        for hh in range(hp):
            s = ss_ref[hh, 0]
            for cidx in fwd_order:
                w = feat[(hh, cidx)]
                w["s_in"] = s
                if cidx != fwd_order[-1]:
                    s = s * w["egl"] + _dot_tn(w["vb"], w["keb"])
        if has_do:
            for u in units:
                w = feat[u]
                rows, cols = where(u)
                w["dob"] = do_ref[rows, cols].astype(BF16)
            for u in units:
                w = feat[u]
                w["a"] = jnp.where(mask, _dot_nt(w["qdb"], w["kib"]), 0.0).astype(BF16)
                w["da"] = jnp.where(mask, _dot_nt(w["dob"], w["vb"]), 0.0).astype(BF16)
                w["m"] = _dot_tn(w["dob"], w["qdb"])
        for hh in range(hp):
            ds = dst[hh]
            for cidx in order:
                w = feat[(hh, cidx)]
                w["ds"] = ds
                ds = ds * w["egl"]
                if has_do:
                    ds = ds + w["m"]
            dst[hh] = ds
            ds0_ref[hh] = ds
        for u in units:
            w = feat[u]
            dsb = w["ds"].astype(BF16)
            w["dke"] = _dot(w["vb"], dsb)
            w["dv"] = _dot_nt(w["keb"], dsb)
            if has_do:
                w["dv"] = w["dv"] + _dot_tn(w["a"], w["dob"])
                w["dqd"] = _dot(w["da"], w["kib"]) + _dot(w["dob"], w["s_in"].astype(BF16))
                w["dki"] = _dot_tn(w["da"], w["qdb"])
        for u in units:
            w = feat[u]
            dkeke = w["dke"] * w["ke"]
            w["dgl"] = (w["egl"] * jnp.sum(w["s_in"] * w["ds"], axis=0, keepdims=True)
                        + jnp.sum(dkeke, axis=0, keepdims=True))
            dg = -dkeke
            dk = w["dke"] * w["ege"]
            if has_do:
                dg = dg + w["dqd"] * w["qd"] - w["dki"] * w["ki"]
                dk = dk + w["dki"] * w["egi"]
            w["dg"], w["dk"] = dg, dk
        dlf = {u: _cum_f32(tri_t, feat[u]["dg"]) for u in units}
        for u in units:
            w = feat[u]
            rows, cols = where(u)
            df = (dlf[u] + w["dgl"]) / w["f"] - w["dk"]
            sf = w["sf"]
            dzf = df * (1.0 - w["lbv"]) * sf * (1.0 - sf)
            dlb_ref[0:1, cols] += jnp.sum(df * (1.0 - sf), axis=0, keepdims=True)
            if has_do:
                dzq = w["dqd"] * w["eg"] * (Q_SCALE * _dsilu(w["zq"], w["sq"]))
            else:
                dzq = jnp.zeros((CHUNK, DH), F32)
            dv = w["dv"]
            if has_acc:
                dzq = dzq + aq_ref[rows, cols]
                dv = dv + av_ref[rows, cols]
            if fused:
                lane = u[0] * DH
                dz_ref[rows, pl.ds(lane, DH)] = dzq.astype(BF16)
                dz_ref[rows, pl.ds(2 * D + lane, DH)] = dzf.astype(BF16)
                dz_ref[rows, pl.ds(3 * D + lane, DH)] = dv.astype(BF16)
            else:
                dzq_ref[rows, cols] = dzq
                dzf_ref[rows, cols] = dzf.astype(f_dtype)
                dzv_ref[rows, cols] = dv

        if ns:
            @pl.when((pl.program_id(0) == NH // hp - 1) & (i == nb - 1))
            def _():
                side_finish()

    def col(g):
        return lambda h, i: (bmap(i), g * (NH // hp) + h)

    tok = pl.BlockSpec((blk, hw), lambda h, i: (bmap(i), h))
    state = pl.BlockSpec((hp, DH, DH), lambda h, i: (h, 0, 0))
    in_specs = [pl.BlockSpec((blk, hw), col(0)), pl.BlockSpec((blk, hw), col(1 + d)), pl.BlockSpec((blk, hw), col(3)),
                pl.BlockSpec((8, hw), lambda h, i: (0, h)),
                pl.BlockSpec((hp, 1, DH, DH), lambda h, i: (h, bmap(i), 0, 0)), state]
    args = [z, z, z, lb, s_start, ds_fin]
    if has_do:
        in_specs.append(tok)
        args.append(do)
    if has_acc:
        in_specs += [tok, tok]
        args += [acc_q, acc_v]
    tail_shape = [jax.ShapeDtypeStruct((8, D), F32), jax.ShapeDtypeStruct((NH, DH, DH), F32)]
    tail_specs = [pl.BlockSpec((8, hw), lambda h, i: (0, h)), state]
    if fused:
        buf, other = into
        aliases = {len(args): 0}
        in_specs += [ANY, tok]
        args += [buf, other]
        out_shape = [jax.ShapeDtypeStruct(buf.shape, buf.dtype)] + tail_shape
        out_specs = [pl.BlockSpec((blk, 4 * D), lambda h, i: (bmap(i), 0))] + tail_specs
    else:
        aliases = {}
        in_specs += [ANY] * ns
        args += list(side)
        out_shape = [jax.ShapeDtypeStruct((n, D), F32), jax.ShapeDtypeStruct((n, D), f_dtype),
                     jax.ShapeDtypeStruct((n, D), F32)] + tail_shape
        out_shape += [jax.ShapeDtypeStruct((NDEV,) + piece_shapes[a], side[a].dtype) for a in range(ns)]
        out_specs = [tok, tok, tok] + tail_specs + [ANY] * ns
    return pl.pallas_call(
        body, name=name, grid=(NH // hp, nb), out_shape=out_shape, in_specs=in_specs, out_specs=out_specs,
        input_output_aliases=aliases,
        scratch_shapes=[pltpu.VMEM((hp, DH, DH), F32)] + (_exchange_sems(ns) if ns else []),
    )(*args)


def _shift(t, s, fill, down):
    n = t.shape[0]
    rows = lax.broadcasted_iota(jnp.int32, t.shape, 0)
    if down:
        return jnp.where(rows >= s, pltpu.roll(t, s, 0), fill)
    return jnp.where(rows < n - s, pltpu.roll(t, n - s, 0), fill)


SUBLANES = 8
LRU_SAVED = 4


def _chain_scan(a, b, h_in, down):
    n = a.shape[0]
    ng = n // SUBLANES
    rows = lax.broadcasted_iota(jnp.int32, (SUBLANES, a.shape[1]), 0)
    local = []
    for g in range(ng):
        aa, bb = a[g * SUBLANES:(g + 1) * SUBLANES], b[g * SUBLANES:(g + 1) * SUBLANES]
        for s in (1, 2, 4):
            if down:
                keep, amt = rows >= s, s
            else:
                keep, amt = rows < SUBLANES - s, SUBLANES - s
            bb = bb + aa * jnp.where(keep, pltpu.roll(bb, amt, 0), 0.0)
            aa = aa * jnp.where(keep, pltpu.roll(aa, amt, 0), 1.0)
        local.append((aa, bb))
    out = [None] * ng
    carry = h_in
    for g in (range(ng) if down else range(ng - 1, -1, -1)):
        aa, bb = local[g]
        hg = bb + aa * carry
        out[g] = hg
        carry = hg[SUBLANES - 1:SUBLANES] if down else hg[0:1]
    return (jnp.concatenate(out, axis=0) if ng > 1 else out[0]), carry


def _conv_taps(xv):
    return (_shift(xv, 1, 0.0, True), xv, _shift(xv, 1, 0.0, False), _shift(xv, 2, 0.0, False))


def _conv(taps, cw, cb):
    return cb + cw[0:1, :] * taps[0] + cw[1:2, :] * taps[1] + cw[2:3, :] * taps[2] + cw[3:4, :] * taps[3]


def _neg_expm1(t):
    series = -t * (1.0 + t * (0.5 + t * (1.0 / 6.0 + t * (1.0 / 24.0 + t * (1.0 / 120.0)))))
    return jnp.where(t > -0.1, series, 1.0 - jnp.exp(t))


def _lru_gates(xc, wr, br, wi, bi, lam):
    xcb = xc.astype(BF16)
    r = _sigmoid(_dot(xcb, wr) + br)
    gi = _sigmoid(_dot(xcb, wi) + bi)
    sp = jnp.maximum(-lam, 0.0) + jnp.log(1.0 + jnp.exp(-jnp.abs(lam)))
    la = -RG_C * r * sp
    a = jnp.exp(la)
    mult = jnp.sqrt(_neg_expm1(2.0 * la))
    return xcb, r, gi, sp, a, mult


def _lru_fwd(xin, blk, cw, cb, wr, br, wi, bi, lam, h0, acc_h, d, name, side=()):
    n = xin.shape[0]
    nb = n // blk
    reverse = d == 1
    down = not reverse
    has_acc = acc_h is not None
    ns = len(side)

    def bmap(i):
        return nb - 1 - i if reverse else i

    def body(*refs):
        x_ref, cw_ref, cb_ref, wr_ref, br_ref, wi_ref, bi_ref, lam_ref, h0_ref = refs[:9]
        pos = 9
        acc_ref = refs[pos] if has_acc else None
        pos += int(has_acc)
        side_in = refs[pos:pos + ns]
        pos += ns
        h_ref, hin_ref, hfin_ref, sav_a_ref, sav_ref = refs[pos:pos + 5]
        pos += 5
        hsum_ref = refs[pos] if has_acc else None
        pos += int(has_acc)
        side_out = refs[pos:pos + ns]
        carry = refs[pos + ns]
        i = pl.program_id(0)
        if ns:
            side_start, side_finish = _exchange(side_in, side_out, refs[pos + ns + 1:])

            @pl.when(i == 0)
            def _():
                side_start()

        @pl.when(i == 0)
        def _():
            carry[...] = h0_ref[...]

        for g in range(NH):
            cols = pl.ds(g * DH, DH)
            xc = _conv(_conv_taps(x_ref[:, cols]), cw_ref[:, cols], cb_ref[:, cols])
            _, r, gi, _, a, mult = _lru_gates(xc, wr_ref[g], br_ref[:, cols], wi_ref[g], bi_ref[:, cols],
                                              lam_ref[:, cols])
            sav_a_ref[:, cols] = a
            for slot, val in enumerate((xc, r, gi, mult)):
                sav_ref[slot, :, cols] = val.astype(BF16)
            hin = carry[:, cols]
            h, h_last = _chain_scan(a, mult * gi * xc, hin, down)
            h_ref[:, cols] = h
            if has_acc:
                hsum_ref[:, cols] = h + acc_ref[:, cols]
            hin_ref[0, :, cols] = hin
            carry[:, cols] = h_last
            hfin_ref[:, cols] = h_last

        if ns:
            @pl.when(i == nb - 1)
            def _():
                side_finish()

    vec = pl.BlockSpec((1, D), lambda i: (0, 0))
    wsp = pl.BlockSpec((NH, DH, DH), lambda i: (0, 0, 0))
    tok = pl.BlockSpec((blk, D), lambda i: (bmap(i), 0))
    in_specs = [tok, pl.BlockSpec((4, D), lambda i: (0, 0)), vec, wsp, vec, wsp, vec, vec, vec]
    args = [xin, cw, cb, wr, br, wi, bi, lam, h0]
    out_shape = [jax.ShapeDtypeStruct((n, D), F32), jax.ShapeDtypeStruct((nb, 1, D), F32),
                 jax.ShapeDtypeStruct((1, D), F32), jax.ShapeDtypeStruct((n, D), F32),
                 jax.ShapeDtypeStruct((LRU_SAVED, n, D), BF16)]
    out_specs = [tok, pl.BlockSpec((1, 1, D), lambda i: (bmap(i), 0, 0)), vec, tok,
                 pl.BlockSpec((LRU_SAVED, blk, D), lambda i: (0, bmap(i), 0))]
    if has_acc:
        in_specs.append(tok)
        args.append(acc_h)
        out_shape.append(jax.ShapeDtypeStruct((n, D), F32))
        out_specs.append(tok)
    in_specs += [ANY] * ns
    args += list(side)
    out_shape += [jax.ShapeDtypeStruct((NDEV,) + t.shape, t.dtype) for t in side]
    out_specs += [ANY] * ns
    return pl.pallas_call(
        body, name=name, grid=(nb,), out_shape=out_shape, in_specs=in_specs, out_specs=out_specs,
        scratch_shapes=[pltpu.VMEM((1, D), F32)] + (_exchange_sems(ns) if ns else []),
    )(*args)


def _lru_bwd(xin, blk, cw, wr, wi, lam, sav, h, hin, dh, cg_fin, acc_dx, init, d, name):
    n = xin.shape[0]
    nb = n // blk
    reverse = d == 1
    down = not reverse
    first = blk - 1 if reverse else 0
    has_dh = dh is not None
    has_acc = acc_dx is not None
    has_init = init is not None

    def bmap(i):
        return i if reverse else nb - 1 - i

    def body(*refs):
        (x_ref, cw_ref, wr_ref, wi_ref, lam_ref, sav_a_ref, sav_ref, h_ref, hin_ref, cgf_ref) = refs[:10]
        pos = 10
        dh_ref = acc_ref = None
        iwr_ref = iwi_ref = ivec_ref = None
        if has_dh:
            dh_ref = refs[pos]
            pos += 1
        if has_acc:
            acc_ref = refs[pos]
            pos += 1
        if has_init:
            iwr_ref, iwi_ref, ivec_ref = refs[pos:pos + 3]
            pos += 3
        dx_ref, dwr_ref, dwi_ref, vec_ref, cg0_ref, carry = refs[pos:]
        i = pl.program_id(0)

        @pl.when(i == 0)
        def _():
            carry[...] = cgf_ref[...]
            if has_init:
                dwr_ref[...] = iwr_ref[...]
                dwi_ref[...] = iwi_ref[...]
                vec_ref[...] = ivec_ref[...]
            else:
                dwr_ref[...] = jnp.zeros_like(dwr_ref)
                dwi_ref[...] = jnp.zeros_like(dwi_ref)
                vec_ref[...] = jnp.zeros_like(vec_ref)

        for g in range(NH):
            cols = pl.ds(g * DH, DH)
            cwv = cw_ref[:, cols]
            lam_v = lam_ref[:, cols]
            taps = _conv_taps(x_ref[:, cols])
            wr_g, wi_g = wr_ref[g], wi_ref[g]
            a = sav_a_ref[:, cols]
            xcb = sav_ref[0, :, cols]
            xc, r, gi, mult = (sav_ref[slot, :, cols].astype(F32) for slot in range(LRU_SAVED))
            sp = jnp.maximum(-lam_v, 0.0) + jnp.log(1.0 + jnp.exp(-jnp.abs(lam_v)))
            hprev = _shift(h_ref[:, cols], 1, hin_ref[0, :, cols], down)
            a_next = _shift(a, 1, 1.0, not down)
            dhv = dh_ref[:, cols] if has_dh else jnp.zeros_like(a)
            e, _ = _chain_scan(a_next, dhv, carry[:, cols], not down)
            cg = a[first:first + 1, :] * e[first:first + 1, :]
            carry[:, cols] = cg
            cg0_ref[:, cols] = cg
            da = e * hprev
            emult = e * mult
            dgi = emult * xc
            dxc = emult * gi
            dla = da * a - (e * gi * xc) * (a * a) / mult
            dr = dla * (-RG_C * sp)
            sneg = 1.0 - _sigmoid(lam_v)
            dpr = dr * r * (1.0 - r)
            dpi = dgi * gi * (1.0 - gi)
            dprb, dpib = dpr.astype(BF16), dpi.astype(BF16)
            dxc = dxc + _dot_nt(dprb, wr_g) + _dot_nt(dpib, wi_g)
            dwr_ref[g] += _dot_tn(xcb, dprb)
            dwi_ref[g] += _dot_tn(xcb, dpib)
            dx = (cwv[0:1, :] * _shift(dxc, 1, 0.0, False) + cwv[1:2, :] * dxc
                  + cwv[2:3, :] * _shift(dxc, 1, 0.0, True) + cwv[3:4, :] * _shift(dxc, 2, 0.0, True))
            if has_acc:
                dx = dx + acc_ref[:, cols]
            dx_ref[:, cols] = dx
            vec_ref[0:1, cols] += jnp.sum(dpr, axis=0, keepdims=True)
            vec_ref[1:2, cols] += jnp.sum(dpi, axis=0, keepdims=True)
            vec_ref[2:3, cols] += jnp.sum(dla * r, axis=0, keepdims=True) * (RG_C * sneg)
            vec_ref[3:4, cols] += jnp.sum(dxc, axis=0, keepdims=True)
            for kk in range(4):
                vec_ref[4 + kk:5 + kk, cols] += jnp.sum(dxc * taps[kk], axis=0, keepdims=True)

    vec = pl.BlockSpec((1, D), lambda i: (0, 0))
    wsp = pl.BlockSpec((NH, DH, DH), lambda i: (0, 0, 0))
    tok = pl.BlockSpec((blk, D), lambda i: (bmap(i), 0))
    vec16 = pl.BlockSpec((16, D), lambda i: (0, 0))
    in_specs = [tok, pl.BlockSpec((4, D), lambda i: (0, 0)), wsp, wsp, vec, tok,
                pl.BlockSpec((LRU_SAVED, blk, D), lambda i: (0, bmap(i), 0)), tok,
                pl.BlockSpec((1, 1, D), lambda i: (bmap(i), 0, 0)), vec]
    args = [xin, cw, wr, wi, lam, sav[0], sav[1], h, hin, cg_fin]
    if has_dh:
        in_specs.append(tok)
        args.append(dh)
    if has_acc:
        in_specs.append(tok)
        args.append(acc_dx)
    if has_init:
        in_specs += [wsp, wsp, vec16]
        args += list(init)
    return pl.pallas_call(
        body, name=name, grid=(nb,),
        out_shape=[jax.ShapeDtypeStruct((n, D), F32), jax.ShapeDtypeStruct((NH, DH, DH), F32),
                   jax.ShapeDtypeStruct((NH, DH, DH), F32), jax.ShapeDtypeStruct((16, D), F32),
                   jax.ShapeDtypeStruct((1, D), F32)],
        in_specs=in_specs, out_specs=[tok, wsp, wsp, vec16, vec],
        scratch_shapes=[pltpu.VMEM((1, D), F32)],
    )(*args)


def _merge(z, o_f, o_b, hx, xin, tgt, mod, gn, p_a, p_b, w_out, ln_g, ln_b):
    n = xin.shape[0]
    tm = _row_tile(n, 128)

    def body(z4_ref, z6_ref, z7_ref, z8_ref, of_ref, ob_ref, hx_ref, x_ref, t_ref, mod_ref, gn_ref,
             pa_ref, pb_ref, wo_ref, lg_ref, lnb_ref,
             dr_ref, do_ref, dhx_ref, dz_ref, oa_o, obb_o, y_o, dya_o, dyb_o, dout_o, vec_ref):
        @pl.when(pl.program_id(0) == 0)
        def _():
            vec_ref[...] = jnp.zeros_like(vec_ref)

        gt = mod_ref[0:1, 2 * D:3 * D]
        gnv = gn_ref[...]
        o = of_ref[...] + ob_ref[...]
        rs = jnp.concatenate(
            [jnp.broadcast_to(lax.rsqrt(jnp.mean(jnp.square(o[:, h * DH:(h + 1) * DH]), axis=1, keepdims=True)
                                        + RMS_EPS), (tm, DH)) for h in range(NH)], axis=1)
        nrm = o * rs
        rn = nrm * gnv
        z4, z6, z7, z8 = z4_ref[...], z6_ref[...], z7_ref[...], z8_ref[...]
        s4, s6, s7, s8 = _sigmoid(z4), _sigmoid(z6), _sigmoid(z7), _sigmoid(z8)
        sg4, sg6 = z4 * s4, z6 * s6
        hxv = hx_ref[...]
        oa = (rn * sg4).astype(BF16)
        obb = (hxv * sg6).astype(BF16)
        ya = _dot(oa, pa_ref[...])
        yb = _dot(obb, pb_ref[...])
        y = (s7 * ya + s8 * yb).astype(BF16)
        out = _dot(y, wo_ref[...])
        xv = x_ref[...]
        rr = ALPHA * xv + gt * out
        mu = jnp.mean(rr, axis=1, keepdims=True)
        cen = rr - mu
        rstd = lax.rsqrt(jnp.mean(cen * cen, axis=1, keepdims=True) + LN_EPS)
        xhat = cen * rstd
        lg = lg_ref[...]
        err = xhat * lg + lnb_ref[...] - t_ref[...]
        loss_rows = jnp.sum(err * err, axis=1, keepdims=True)
        dxn = err * (1.0 / D)
        dxh = dxn * lg
        dr = rstd * (dxh - jnp.mean(dxh, axis=1, keepdims=True)
                     - xhat * jnp.mean(dxh * xhat, axis=1, keepdims=True))
        dout = (dr * gt).astype(BF16)
        dy = _dot_nt(dout, wo_ref[...])
        dya = (dy * s7).astype(BF16)
        dyb = (dy * s8).astype(BF16)
        doa = _dot_nt(dya, pa_ref[...])
        dob = _dot_nt(dyb, pb_ref[...])
        drn = doa * sg4
        dn = drn * gnv
        dnn = dn * nrm
        corr = jnp.concatenate(
            [jnp.broadcast_to(jnp.mean(dnn[:, h * DH:(h + 1) * DH], axis=1, keepdims=True), (tm, DH))
             for h in range(NH)], axis=1)
        dr_ref[...] = dr
        do_ref[...] = rs * (dn - nrm * corr)
        dhx_ref[...] = dob * sg6
        dz_ref[:, 0:4 * D] = jnp.zeros((tm, 4 * D), BF16)
        dz_ref[:, 4 * D:5 * D] = (doa * rn * _dsilu(z4, s4)).astype(BF16)
        dz_ref[:, 5 * D:6 * D] = jnp.zeros((tm, D), BF16)
        dz_ref[:, 6 * D:7 * D] = (dob * hxv * _dsilu(z6, s6)).astype(BF16)
        dz_ref[:, 7 * D:8 * D] = (dy * ya * s7 * (1.0 - s7)).astype(BF16)
        dz_ref[:, 8 * D:9 * D] = (dy * yb * s8 * (1.0 - s8)).astype(BF16)
        oa_o[...] = oa
        obb_o[...] = obb
        y_o[...] = y
        dya_o[...] = dya
        dyb_o[...] = dyb
        dout_o[...] = dout
        vec_ref[0:1, :] += jnp.sum(dr * out, axis=0, keepdims=True)
        vec_ref[1:2, :] += jnp.sum(dxn * xhat, axis=0, keepdims=True)
        vec_ref[2:3, :] += jnp.sum(dxn, axis=0, keepdims=True)
        vec_ref[3:4, :] += jnp.sum(drn * nrm, axis=0, keepdims=True)
        vec_ref[4:5, :] += jnp.broadcast_to(jnp.sum(loss_rows, axis=0, keepdims=True) * (0.5 / D), (1, D))

    def grp(g):
        return pl.BlockSpec((tm, D), lambda i: (i, g))

    tok = pl.BlockSpec((tm, D), lambda i: (i, 0))
    vec = pl.BlockSpec((1, D), lambda i: (0, 0))
    wsp = pl.BlockSpec((D, D), lambda i: (0, 0))
    return pl.pallas_call(
        body, name="merge", grid=(n // tm,),
        out_shape=[jax.ShapeDtypeStruct((n, D), F32)] * 3
        + [jax.ShapeDtypeStruct((n, NGRP * D), BF16)]
        + [jax.ShapeDtypeStruct((n, D), BF16)] * 6 + [jax.ShapeDtypeStruct((8, D), F32)],
        in_specs=[grp(4), grp(6), grp(7), grp(8), tok, tok, tok, tok, tok,
                  pl.BlockSpec((8, 3 * D), lambda i: (0, 0)), vec, wsp, wsp, wsp, vec, vec],
        out_specs=[tok, tok, tok, pl.BlockSpec((tm, NGRP * D), lambda i: (i, 0))] + [tok] * 6
        + [pl.BlockSpec((8, D), lambda i: (0, 0))],
    )(z, z, z, z, o_f, o_b, hx, xin, tgt, mod, gn, p_a, p_b, w_out, ln_g, ln_b)


def _wmod_grad(c_t, cctx_t, dmx_loc, dmc_loc, name):
    n = dmx_loc.shape[1]

    def body(ct_ref, cc_ref, dmx_ref, dmc_ref, o_ref):
        ct = ct_ref[...]
        sct = ct * _sigmoid(ct)
        cc = cc_ref[...]
        scc = cc * _sigmoid(cc)
        dmc = dmc_ref[0:1, :]
        for b in range(1, NDEV):
            dmc = dmc + dmc_ref[b:b + 1, :]
        acc = scc * dmc
        for b in range(NDEV):
            acc = acc + sct[:, b:b + 1] * dmx_ref[b:b + 1, :]
        o_ref[...] = acc

    return pl.pallas_call(body, name=name, out_shape=jax.ShapeDtypeStruct((D, n), F32))(c_t, cctx_t, dmx_loc, dmc_loc)


PACK_ROWS = 40


def _finalize_small(g_pack, lb, w_mod_full, params):
    npar = len(params)

    def body(*refs):
        gp_ref, lb_ref, wm_ref = refs[:3]
        wmv = refs[3:3 + 3 * npar]
        loss_ref = refs[3 + 3 * npar]
        g_refs = refs[4 + 3 * npar:4 + 4 * npar]
        upd = refs[4 + 4 * npar:4 + 7 * npar]
        tot = refs[-1]
        acc = gp_ref[0]
        for k in range(1, NDEV):
            acc = acc + gp_ref[k]
        tot[...] = acc
        mine = pl.ds(pl.multiple_of(_my_index() * DH, DH), DH)
        (g_cctx, g_bmod, g_bin, g_lbl, g_norm, g_cw, g_cb, g_br, g_bi, g_lam, g_lng, g_lnb) = g_refs

        loss_ref[...] = jnp.broadcast_to(tot[36:37, 0:DH], (8, DH))
        for k in range(3):
            g_bmod[:, k * D:(k + 1) * D] = tot[k:k + 1, :] + tot[3 + k:4 + k, :]
        dmc = jnp.concatenate([tot[3:4, :], tot[4:5, :], tot[5:6, :]], axis=1)
        cv = wmv[0][...]
        proj = _dot_nt(jnp.broadcast_to(dmc, (8, 3 * D)).astype(BF16), wm_ref[...])
        g_cctx[...] = proj[0:1, :] * _dsilu(cv, _sigmoid(cv))
        for k in range(NGRP):
            g_bin[:, k * D:(k + 1) * D] = tot[6 + k:7 + k, :]
        nrm = tot[15:16, 0:DH]
        for h in range(1, NH):
            nrm = nrm + tot[15:16, h * DH:(h + 1) * DH]
        g_norm[...] = nrm
        g_lng[...] = tot[16:17, :]
        g_lnb[...] = tot[17:18, :]
        g_cb[...] = tot[21:22, :] + tot[29:30, :]
        g_cw[0] = tot[22:26, mine] + tot[30:34, mine]
        for ref, row in ((g_br, 18), (g_bi, 19), (g_lam, 20)):
            ref[0, 0:1, :] = tot[row:row + 1, mine]
            ref[0, 1:2, :] = tot[row + 8:row + 9, mine]
        lbl = lb_ref[0:2, mine]
        dl0 = tot[34:36, mine] * lbl * (1.0 - lbl)
        g_lbl[0] = dl0
        g_lbl[1] = -dl0
        for p in range(npar):
            d, mm, vv = _adam_math(g_refs[p][...], wmv[3 * p][...], wmv[3 * p + 1][...], wmv[3 * p + 2][...])
            upd[3 * p][...] = d
            upd[3 * p + 1][...] = mm
            upd[3 * p + 2][...] = vv

    flat = [t for wmv in params for t in wmv]
    shapes = [jax.ShapeDtypeStruct(wmv[0].shape, F32) for wmv in params]
    outs = pl.pallas_call(
        body, name="finalize_small",
        out_shape=[jax.ShapeDtypeStruct((8, DH), F32)] + shapes + [s for s in shapes for _ in range(3)],
        scratch_shapes=[pltpu.VMEM((PACK_ROWS, D), F32)],
    )(g_pack, lb, w_mod_full, *flat)
    grads = list(outs[1:1 + npar])
    upd = [tuple(outs[1 + npar + 3 * p:4 + npar + 3 * p]) for p in range(npar)]
    return outs[0], grads, upd


def _to_colmajor(t, rows):
    return t.reshape(rows, GRID_W, D).transpose(1, 0, 2).reshape(rows * GRID_W, D)


def _to_raster(t, rows):
    return t.reshape(GRID_W, rows, D).transpose(1, 0, 2).reshape(rows * GRID_W, D)


def _local_cols(t, me, width):
    return lax.dynamic_slice_in_dim(t, me * width, width, axis=t.ndim - 1)


def kernel(x, c, ctx, c_ctx, w_mod, b_mod, w_in, b_in, lb_logits, norm_a_g, conv_w, conv_b, w_r, b_r, w_i, b_i, lam, p_a, p_b, w_out, ln_g, ln_b, loss_target, m_c_ctx, m_w_mod, m_b_mod, m_w_in, m_b_in, m_lb_logits, m_norm_a_g, m_conv_w, m_conv_b, m_w_r, m_b_r, m_w_i, m_b_i, m_lam, m_p_a, m_p_b, m_w_out, m_ln_g, m_ln_b, v_c_ctx, v_w_mod, v_b_mod, v_w_in, v_b_in, v_lb_logits, v_norm_a_g, v_conv_w, v_conv_b, v_w_r, v_b_r, v_w_i, v_b_i, v_lam, v_p_a, v_p_b, v_w_out, v_ln_g, v_ln_b):
    me = _my_index()
    xs, cs, tgt = x[0], ctx[0], loss_target[0]
    t_len, c_len = xs.shape[0], cs.shape[0]
    rows = t_len // GRID_W
    wcols = w_in.shape[2]
    mcols = w_mod.shape[2]

    small = jnp.concatenate([lb_logits.reshape(4, DH), conv_w[0], b_r[0], b_i[0], lam[0], jnp.zeros((2, DH), F32),
                             c.reshape(8, DH)], axis=0)
    g_small, g_wmod = _all_gather([small, w_mod[0].astype(BF16)], "gather_params")

    def full_rows(lo, hi):
        return g_small[:, lo:hi, :].transpose(1, 0, 2).reshape(hi - lo, D)

    lbl_f, cw_f, br_f, bi_f, lam_f = full_rows(0, 4), full_rows(4, 8), full_rows(8, 10), full_rows(10, 12), full_rows(12, 14)
    c_all = g_small[:, 16:24, :].reshape(NDEV, D)
    w_mod_f = g_wmod.transpose(1, 0, 2).reshape(D, 3 * D)
    w_r_b, w_i_b = w_r[0].astype(BF16), w_i[0].astype(BF16)

    cc = jnp.concatenate([c.reshape(1, D), c_ctx.reshape(1, D), jnp.zeros((6, D), F32)], axis=0)
    lbl_p = jnp.concatenate([lbl_f.reshape(2, 2, D), jnp.zeros((2, 6, D), F32)], axis=1)
    mod, lb = _prep(cc, w_mod_f, b_mod, lbl_p)
    u_x = _modulate(xs, mod, 0, "modulate_x")
    u_c = _modulate(cs, mod, 1, "modulate_c")
    z_x, w_in_f = _inproj_gather(u_x, w_in[0].astype(BF16), b_in, "inproj_gather")
    z_c = _mm_bias(u_c, w_in_f, b_in, "inproj_c")

    zero_s = jnp.zeros((NH, DH, DH), F32)
    zero_v = jnp.zeros((1, D), F32)
    gla = {}
    out_w = [p_a[0].astype(BF16), p_b[0].astype(BF16), w_out[0].astype(BF16)]
    out_w_f = []
    for d in (0, 1):
        _, ssc, sfc = _gla_fwd(z_c, lb, zero_s, d, f"gla_fwd_c{d}")
        o_d, ssx, _, gathered = _gla_fwd(z_x, lb, sfc, d, f"gla_fwd_x{d}", side=[out_w[d]])
        out_w_f.append(gathered.reshape(D, D))
        gla[d] = (ssc, ssx, o_d)

    x5_c = z_c[:, 5 * D:6 * D]
    x5_x = z_x.reshape(rows, GRID_W, NGRP, D)[:, :, 5, :].transpose(1, 0, 2).reshape(t_len, D)
    cb2 = conv_b.reshape(1, D)
    lru = {}
    h_sum = None
    for d in (0, 1):
        prm = (cw_f, cb2, w_r_b[d], br_f[d:d + 1], w_i_b[d], bi_f[d:d + 1], lam_f[d:d + 1])
        h_c, hin_c, hfin_c, *sav_c = _lru_fwd(x5_c, c_len, *prm, zero_v, None, d, f"lru_fwd_c{d}")
        h_x, hin_x, _, sav_a, sav_h, *more = _lru_fwd(x5_x, rows, *prm, hfin_c, lru[0][3] if d else None, d,
                                                      f"lru_fwd_x{d}", side=() if d else [out_w[2]])
        if d == 0:
            out_w_f.append(more[0].reshape(D, D))
        else:
            h_sum = more[0]
        lru[d] = ((cw_f, w_r_b[d], w_i_b[d], lam_f[d:d + 1]), h_c, hin_c, h_x, hin_x, tuple(sav_c), (sav_a, sav_h))
    p_a_f, p_b_f, w_out_f = out_w_f
    hx = _to_raster(h_sum, rows)

    gn = jnp.tile(norm_a_g.reshape(1, DH), (1, NH))
    (dr, do, dhx, dz_m, oa, obb, yb16, dya, dyb, dout, mvec) = _merge(
        z_x, gla[0][2], gla[1][2], hx, xs, tgt, mod, gn, p_a_f, p_b_f, w_out_f, ln_g, ln_b)

    dhx_cm = _to_colmajor(dhx, rows)
    lru_dx_x = lru_dx_c = None
    for d in (0, 1):
        prm, h_c, hin_c, h_x, hin_x, sav_c, sav_x = lru[d]
        lru_dx_x, dwr, dwi, lvec, cg0 = _lru_bwd(x5_x, rows, *prm, sav_x, h_x, hin_x, dhx_cm, zero_v, lru_dx_x, None,
                                                 d, f"lru_bwd_x{d}")
        lru_dx_c, dwr, dwi, lvec, _ = _lru_bwd(x5_c, c_len, *prm, sav_c, h_c, hin_c, None, cg0, lru_dx_c,
                                               (dwr, dwi, lvec), d, f"lru_bwd_c{d}")
        lru[d] = (dwr, dwi, lvec)
    dz5_x = _to_raster(lru_dx_x, rows).astype(BF16)
    dz5_c = lru_dx_c.astype(BF16)

    dpa = _mm_tn(oa, dya, None, "dpa", out_dtype=BF16)
    dpb = _mm_tn(obb, dyb, None, "dpb", out_dtype=BF16)
    dwo = _mm_tn(yb16, dout, None, "dwout", out_dtype=BF16)
    wr_pack = jnp.concatenate([lru[0][0], lru[1][0], lru[0][1], lru[1][1]], axis=0).reshape(4 * NH * DH, DH)

    gq_c = gv_c = None
    dzf_c, dlb = {}, {}
    gq_x, dzf_x0, gv_x, dlb_x, ds0, r_pa, r_pb, r_wo, r_wri = _gla_bwd(
        z_x, lb, gla[0][1], do, zero_s, None, None, 0, "gla_bwd_x0", f_dtype=BF16,
        side=[dpa, dpb, dwo, wr_pack], side_splits=[0, 0, 0, 0])
    gq_c, dzf_c[0], gv_c, dlb_c, _ = _gla_bwd(z_c, lb, gla[0][0], None, ds0, None, None, 0, "gla_bwd_c0")
    dlb[0] = dlb_x[0:1] + dlb_c[0:1]
    dz_g, dlb_x, ds0 = _gla_bwd(z_x, lb, gla[1][1], do, zero_s, gq_x, gv_x, 1, "gla_bwd_x1", into=(dz_m, dzf_x0))
    gq_c, dzf_c[1], gv_c, dlb_c, _ = _gla_bwd(z_c, lb, gla[1][0], None, ds0, gq_c, gv_c, 1, "gla_bwd_c1")
    dlb[1] = dlb_x[0:1] + dlb_c[0:1]

    bf = lambda t: t.astype(BF16)
    dz_x = lax.dynamic_update_slice(dz_g, dz5_x, (0, 5 * D))
    zc0 = jnp.zeros((c_len, D), BF16)
    dz_c = jnp.concatenate([bf(gq_c), bf(dzf_c[0]), bf(dzf_c[1]), bf(gv_c), zc0, dz5_c, zc0, zc0, zc0], axis=1)
    dwin_c, dbin_c = _mm_tn(u_c, dz_c, None, "dwin_c", with_colsum=True)

    grad_x, xvec = _input_grad(dz_x, w_in_f, xs, dr, mod, 0, "input_grad_x")
    r_win, dbin = _dwin_exchange(u_x, dz_x, dwin_c, dbin_c, "dwin_exchange")
    _, cvec = _input_grad(dz_c, w_in_f, cs, None, mod, 1, "input_grad_c")
    wri_piece = _sum_rows(r_wri, "sum_w_ri_piece")
    g_w_in, d_w_in, nm_w_in, nv_w_in = _sum_adamw(r_win, w_in, m_w_in, v_w_in, "update_w_in")
    g_p_a, d_p_a, nm_p_a, nv_p_a = _sum_adamw(r_pa, p_a, m_p_a, v_p_a, "update_p_a")
    g_p_b, d_p_b, nm_p_b, nv_p_b = _sum_adamw(r_pb, p_b, m_p_b, v_p_b, "update_p_b")
    g_w_out, d_w_out, nm_w_out, nv_w_out = _sum_adamw(r_wo, w_out, m_w_out, v_w_out, "update_w_out")

    dlb_rows = jnp.concatenate([dlb[0], dlb[1]], axis=0)
    pack = jnp.concatenate([
        xvec[0:1], xvec[1:2], mvec[0:1],
        cvec[0:1], cvec[1:2], jnp.zeros((1, D), F32),
        dbin.reshape(NGRP, D),
        mvec[3:4], mvec[1:2], mvec[2:3],
        lru[0][2][0:8], lru[1][2][0:3],
        lru[1][2][3:8],
        dlb_rows,
        mvec[4:5],
        jnp.zeros((3, D), F32)], axis=0)
    g_pack, g_wri = _all_gather([pack, wri_piece], "gather_small_grads")

    dmx = g_pack[:, 0:3, :].reshape(NDEV, 3 * D)
    dmc = g_pack[:, 3:6, :].reshape(NDEV, 3 * D)
    grad_w_mod = _wmod_grad(c_all.T, c_ctx.reshape(D, 1), _local_cols(dmx, me, mcols), _local_cols(dmc, me, mcols),
                            "grad_w_mod").reshape(1, D, mcols)
    small_params = [(c_ctx.reshape(1, D), m_c_ctx.reshape(1, D), v_c_ctx.reshape(1, D)), (b_mod, m_b_mod, v_b_mod),
                    (b_in, m_b_in, v_b_in), (lb_logits, m_lb_logits, v_lb_logits), (norm_a_g, m_norm_a_g, v_norm_a_g),
                    (conv_w, m_conv_w, v_conv_w), (conv_b, m_conv_b, v_conv_b), (b_r, m_b_r, v_b_r),
                    (b_i, m_b_i, v_b_i), (lam, m_lam, v_lam), (ln_g, m_ln_g, v_ln_g), (ln_b, m_ln_b, v_ln_b)]
    loss_tile, small_g, small_upd = _finalize_small(g_pack, lb, w_mod_f, small_params)
    loss = loss_tile[0, 0]
    (grad_c_ctx, grad_b_mod, grad_b_in, grad_lb_logits, grad_norm_a_g, grad_conv_w, grad_conv_b, grad_b_r, grad_b_i,
     grad_lam, grad_ln_g, grad_ln_b) = small_g
    small_upd[0] = tuple(t.reshape(c_ctx.shape) for t in small_upd[0])
    (o_c_ctx, o_b_mod, o_b_in, o_lb, o_norm, o_conv_w, o_conv_b, o_b_r, o_b_i, o_lam, o_ln_g, o_ln_b) = small_upd

    half = 2 * NH * DH
    g_ri = g_wri.reshape(2 * half, DH)
    grad_w_r, grad_w_i = g_ri[:half].reshape(w_r.shape), g_ri[half:].reshape(w_i.shape)
    d_w_r, nm_w_r, nv_w_r = _adamw(grad_w_r, w_r, m_w_r, v_w_r, "update_w_r")
    d_w_i, nm_w_i, nv_w_i = _adamw(grad_w_i, w_i, m_w_i, v_w_i, "update_w_i")

    d_w_mod, nm_w_mod, nv_w_mod = _adamw(grad_w_mod, w_mod, m_w_mod, v_w_mod, "update_w_mod")

    grads = [grad_c_ctx.reshape(c_ctx.shape), grad_w_mod, grad_b_mod, g_w_in, grad_b_in, grad_lb_logits, grad_norm_a_g,
             grad_conv_w, grad_conv_b, grad_w_r, grad_b_r, grad_w_i, grad_b_i, grad_lam, g_p_a, g_p_b, g_w_out,
             grad_ln_g, grad_ln_b]
    per_kind = []
    for k in range(3):
        per_kind.append([
            o_c_ctx[k], (d_w_mod, nm_w_mod, nv_w_mod)[k], o_b_mod[k], (d_w_in, nm_w_in, nv_w_in)[k], o_b_in[k], o_lb[k],
            o_norm[k], o_conv_w[k], o_conv_b[k], (d_w_r, nm_w_r, nv_w_r)[k], o_b_r[k], (d_w_i, nm_w_i, nv_w_i)[k],
            o_b_i[k], o_lam[k], (d_p_a, nm_p_a, nv_p_a)[k], (d_p_b, nm_p_b, nv_p_b)[k], (d_w_out, nm_w_out, nv_w_out)[k],
            o_ln_g[k], o_ln_b[k]])
    return (loss, grad_x.reshape(x.shape), *grads, *per_kind[0], *per_kind[1], *per_kind[2])
```

```python
import functools

import jax
import jax.numpy as jnp
from jax import lax
from jax.experimental import pallas as pl
from jax.experimental.pallas import tpu as pltpu

F32 = jnp.float32
BF16 = jnp.bfloat16

D = 1024
NH = 8
DH = 128
CHUNK = 64
GLA_HEADS_PER_STEP = 8
GRID_W = 64
NGRP = 9
NDEV = 8
RG_C = 8.0
ALPHA = 2.0 ** 0.25
LN_EPS = 1e-5
RMS_EPS = 1e-6
Q_SCALE = DH ** -0.5
ADAM_LR, ADAM_B1, ADAM_B2, ADAM_EPS, ADAM_WD, ADAM_STEP = 1e-3, 0.9, 0.999, 1e-8, 0.01, 10
ADAM_C1 = 1.0 / (1.0 - ADAM_B1 ** ADAM_STEP)
ADAM_C2 = 1.0 / (1.0 - ADAM_B2 ** ADAM_STEP)

ANY = pl.BlockSpec(memory_space=pl.ANY)


def _sigmoid(t):
    return 1.0 / (1.0 + jnp.exp(-t))


def _dsilu(t, s):
    return s * (1.0 + t * (1.0 - s))


def _dot(a, b):
    return jnp.dot(a, b, preferred_element_type=F32)


def _dot_nt(a, b):
    return lax.dot_general(a, b, (((1,), (1,)), ((), ())), preferred_element_type=F32)


def _dot_tn(a, b):
    return lax.dot_general(a, b, (((0,), (0,)), ((), ())), preferred_element_type=F32)


def _my_index():
    return 4 * lax.axis_index("x") + 2 * lax.axis_index("y") + lax.axis_index("c")


def _dev_tuple(j):
    return (j >> 2, (j >> 1) & 1, j & 1)


def _exchange_sems(n):
    return [pltpu.SemaphoreType.DMA((n * NDEV,)), pltpu.SemaphoreType.DMA((n * NDEV,)), pltpu.SemaphoreType.DMA((n,))]


def _exchange(ins, outs, sems, piece_of=None):
    send_sems, recv_sems, loc_sems = sems
    n = len(ins)

    def src(a, p):
        return ins[a] if piece_of is None else piece_of(ins[a], a, p)

    def push(a, t):
        me, p = _my_index(), _step_peer(t)
        return pltpu.make_async_remote_copy(
            src_ref=src(a, p), dst_ref=outs[a].at[me],
            send_sem=send_sems.at[a * NDEV + t], recv_sem=recv_sems.at[a * NDEV + me],
            device_id=_dev_of(p), device_id_type=pl.DeviceIdType.MESH)

    def local(a):
        me = _my_index()
        return pltpu.make_async_copy(src(a, me), outs[a].at[me], loc_sems.at[a])

    def start():
        for a in range(n):
            local(a).start()
        for t in range(NDEV - 1):
            for a in range(n):
                push(a, t).start()

    def finish():
        me = _my_index()
        for t in range(NDEV - 1):
            for a in range(n):
                push(a, t).wait_send()
        for j in range(NDEV):
            @pl.when(me != j)
            def _():
                for a in range(n):
                    pltpu.make_async_remote_copy(
                        src_ref=src(a, j), dst_ref=outs[a].at[j],
                        send_sem=send_sems.at[a * NDEV], recv_sem=recv_sems.at[a * NDEV + j],
                        device_id=_dev_tuple(j), device_id_type=pl.DeviceIdType.MESH).wait_recv()
        for a in range(n):
            local(a).wait()

    return start, finish


def _all_gather(shards, name):
    n = len(shards)

    def body(*refs):
        start, finish = _exchange(refs[:n], refs[n:2 * n], refs[2 * n:])
        start()
        finish()

    return pl.pallas_call(
        body, name=name,
        out_shape=[jax.ShapeDtypeStruct((NDEV,) + s.shape, s.dtype) for s in shards],
        in_specs=[ANY] * n, out_specs=[ANY] * n, scratch_shapes=_exchange_sems(n),
    )(*shards)


def _pieces(parts, splits):
    shapes = []
    for part, split in zip(parts, splits):
        r, c = part.shape
        shapes.append((r // NDEV, c) if split == 0 else (r, c // NDEV))

    def piece_of(ref, a, j):
        pr, pc = shapes[a]
        if splits[a] == 0:
            start = j * pr if isinstance(j, int) else pl.multiple_of(j * pr, pr)
            return ref.at[pl.ds(start, pr), :]
        start = j * pc if isinstance(j, int) else pl.multiple_of(j * pc, pc)
        return ref.at[:, pl.ds(start, pc)]

    return shapes, piece_of


_STEP_MASKS = ((2, 4, 6, 3, 5, 7, 1, 0), (4, 2, 6, 5, 3, 7, 1, 0))
_GATHER_MASKS = ((0, 1, 2, 4, 3, 5, 6, 7), (0, 1, 4, 2, 5, 3, 6, 7))


def _peer_schedule(table):
    tab = jnp.array(table, jnp.int32)
    return jnp.bitwise_xor(_my_index(), tab[lax.axis_index("c")])


def _step_peer(s, table=_STEP_MASKS):
    def pick(row):
        if isinstance(s, int):
            return jnp.int32(row[s])
        m = jnp.int32(row[NDEV - 1])
        for t in range(NDEV - 2, -1, -1):
            m = jnp.where(s == t, jnp.int32(row[t]), m)
        return m
    mask = jnp.where(lax.axis_index("c") == 0, pick(table[0]), pick(table[1]))
    return jnp.bitwise_xor(_my_index(), mask)


def _dev_of(p):
    return (p // 4, (p // 2) % 2, p % 2)


def _dwin_exchange(u, dz, init, cs_init, name):
    m, ka = u.shape
    n = dz.shape[1]
    pc = n // NDEV
    tk = _row_tile(m, 512)
    nk = m // tk

    def body(pidx_ref, u_ref, dz_ref, init_ref, csi_ref, rwin, cs_ref, acc, sbuf, wsend, wrecv, wloc):
        s, k = pl.program_id(0), pl.program_id(1)
        me = _my_index()

        def slab_copy(slot, p):
            return pltpu.make_async_remote_copy(
                src_ref=sbuf.at[slot], dst_ref=rwin.at[me], send_sem=wsend.at[slot], recv_sem=wrecv.at[me],
                device_id=_dev_of(p), device_id_type=pl.DeviceIdType.MESH)

        @pl.when(k == 0)
        def _():
            acc[...] = init_ref[...]
            cs_ref[...] = csi_ref[...]

        bv = dz_ref[...]
        acc[...] += _dot_tn(u_ref[...], bv)
        cs_ref[...] += jnp.sum(bv.astype(F32), axis=0, keepdims=True)

        @pl.when(k == nk - 1)
        def _():
            slot = s % 2

            @pl.when(s >= 2)
            def _():
                slab_copy(slot, me).wait_send()

            sbuf[slot] = acc[...].astype(BF16)

            @pl.when(s < NDEV - 1)
            def _():
                slab_copy(slot, _step_peer(s)).start()

            @pl.when(s == NDEV - 1)
            def _():
                own = pltpu.make_async_copy(sbuf.at[slot], rwin.at[me], wloc.at[0])
                own.start()
                slab_copy(1 - slot, me).wait_send()
                for j in range(NDEV):
                    @pl.when(me != j)
                    def _():
                        pltpu.make_async_remote_copy(
                            src_ref=sbuf.at[0], dst_ref=rwin.at[j], send_sem=wsend.at[0], recv_sem=wrecv.at[j],
                            device_id=_dev_tuple(j), device_id_type=pl.DeviceIdType.MESH).wait_recv()
                own.wait()

    grid_spec = pltpu.PrefetchScalarGridSpec(
        num_scalar_prefetch=1, grid=(NDEV, nk),
        in_specs=[pl.BlockSpec((tk, ka), lambda s, k, pidx: (k, 0)),
                  pl.BlockSpec((tk, pc), lambda s, k, pidx: (k, pidx[s])),
                  pl.BlockSpec((ka, pc), lambda s, k, pidx: (0, pidx[s])),
                  pl.BlockSpec((1, pc), lambda s, k, pidx: (0, pidx[s]))],
        out_specs=[ANY, pl.BlockSpec((1, pc), lambda s, k, pidx: (0, pidx[s]))],
        scratch_shapes=[pltpu.VMEM((ka, pc), F32), pltpu.VMEM((2, ka, pc), BF16),
                        pltpu.SemaphoreType.DMA((2,)), pltpu.SemaphoreType.DMA((NDEV,)), pltpu.SemaphoreType.DMA((1,))])
    return pl.pallas_call(
        body, name=name, grid_spec=grid_spec,
        out_shape=[jax.ShapeDtypeStruct((NDEV, ka, pc), BF16), jax.ShapeDtypeStruct((1, n), F32)],
    )(_peer_schedule(_STEP_MASKS), u, dz, init, cs_init)


def _inproj_gather(u, w_loc, bias, name):
    m, k = u.shape
    pc = w_loc.shape[1]
    n = pc * NDEV
    tm = _row_tile(m, 512)
    ni = m // tm

    def body(pidx_ref, u_ref, b_ref, wl_ref, z_ref, wall, wbuf, wsend, wrecv, ldsem, ownsem):
        s, i = pl.program_id(0), pl.program_id(1)
        me = _my_index()

        def shard_push(t):
            return pltpu.make_async_remote_copy(
                src_ref=wl_ref, dst_ref=wall.at[me], send_sem=wsend.at[t], recv_sem=wrecv.at[me],
                device_id=_dev_of(_step_peer(t, _GATHER_MASKS)), device_id_type=pl.DeviceIdType.MESH)

        def load(slot, src):
            return pltpu.make_async_copy(src, wbuf.at[slot], ldsem.at[slot])

        own = pltpu.make_async_copy(wl_ref, wall.at[me], ownsem.at[0])

        @pl.when((s == 0) & (i == 0))
        def _():
            own.start()
            load(0, wl_ref).start()
            for t in range(1, NDEV):
                shard_push(t).start()

        @pl.when((i == ni // 2) & (s < NDEV - 1))
        def _():
            nxt = _step_peer(s + 1, _GATHER_MASKS)
            pltpu.make_async_remote_copy(
                src_ref=wl_ref, dst_ref=wall.at[nxt], send_sem=wsend.at[0], recv_sem=wrecv.at[nxt],
                device_id=_dev_of(nxt), device_id_type=pl.DeviceIdType.MESH).wait_recv()
            load((s + 1) % 2, wall.at[nxt]).start()

        @pl.when(i == 0)
        def _():
            load(s % 2, wl_ref).wait()

        z_ref[...] = _dot(u_ref[...], wbuf[s % 2]) + b_ref[...]

        @pl.when((s == NDEV - 1) & (i == ni - 1))
        def _():
            own.wait()
            for t in range(1, NDEV):
                shard_push(t).wait_send()

    grid_spec = pltpu.PrefetchScalarGridSpec(
        num_scalar_prefetch=1, grid=(NDEV, ni),
        in_specs=[pl.BlockSpec((tm, k), lambda s, i, pidx: (i, 0)),
                  pl.BlockSpec((1, pc), lambda s, i, pidx: (0, pidx[s])), ANY],
        out_specs=[pl.BlockSpec((tm, pc), lambda s, i, pidx: (i, pidx[s])), ANY],
        scratch_shapes=[pltpu.VMEM((2, k, pc), BF16),
                        pltpu.SemaphoreType.DMA((NDEV,)), pltpu.SemaphoreType.DMA((NDEV,)),
                        pltpu.SemaphoreType.DMA((2,)), pltpu.SemaphoreType.DMA((1,))])
    return pl.pallas_call(
        body, name=name, grid_spec=grid_spec,
        out_shape=[jax.ShapeDtypeStruct((m, n), F32), jax.ShapeDtypeStruct((NDEV, k, pc), w_loc.dtype)],
    )(_peer_schedule(_GATHER_MASKS), u, bias, w_loc)


def _adam_math(g, w, m, v):
    m2 = ADAM_B1 * m + (1.0 - ADAM_B1) * g
    v2 = ADAM_B2 * v + (1.0 - ADAM_B2) * (g * g)
    delta = -ADAM_LR * ((m2 * ADAM_C1) / (jnp.sqrt(v2 * ADAM_C2) + ADAM_EPS) + ADAM_WD * w)
    return delta, m2, v2


def _row_tile(r, cap):
    t = min(r, cap)
    while r % t:
        t //= 2
    return t


def _adamw(g, w, m, v, name):
    shape = w.shape
    cols = shape[-1] if w.ndim >= 2 and shape[-1] % 128 == 0 else 128
    g2, w2, m2, v2 = (t.reshape(-1, cols) for t in (g, w, m, v))
    r = g2.shape[0]
    tr = _row_tile(r, 256)

    def body(g_ref, w_ref, m_ref, v_ref, d_ref, mo_ref, vo_ref):
        d, mm, vv = _adam_math(g_ref[...], w_ref[...], m_ref[...], v_ref[...])
        d_ref[...] = d
        mo_ref[...] = mm
        vo_ref[...] = vv

    spec = pl.BlockSpec((tr, cols), lambda i: (i, 0))
    outs = pl.pallas_call(
        body, name=name, grid=(r // tr,),
        out_shape=[jax.ShapeDtypeStruct((r, cols), F32)] * 3,
        in_specs=[spec] * 4, out_specs=[spec] * 3,
    )(g2, w2, m2, v2)
    return tuple(o.reshape(shape) for o in outs)


def _sum_adamw(parts, w, m, v, name):
    _, r, c = parts.shape
    shape = w.shape
    w2, m2, v2 = (t.reshape(r, c) for t in (w, m, v))
    tr = _row_tile(r, 128)

    def body(p_ref, w_ref, m_ref, v_ref, g_ref, d_ref, mo_ref, vo_ref):
        g = p_ref[0].astype(F32)
        for k in range(1, NDEV):
            g = g + p_ref[k].astype(F32)
        d, mm, vv = _adam_math(g, w_ref[...], m_ref[...], v_ref[...])
        g_ref[...] = g
        d_ref[...] = d
        mo_ref[...] = mm
        vo_ref[...] = vv

    spec = pl.BlockSpec((tr, c), lambda i: (i, 0))
    outs = pl.pallas_call(
        body, name=name, grid=(r // tr,),
        out_shape=[jax.ShapeDtypeStruct((r, c), F32)] * 4,
        in_specs=[pl.BlockSpec((NDEV, tr, c), lambda i: (0, i, 0))] + [spec] * 3, out_specs=[spec] * 4,
    )(parts, w2, m2, v2)
    return tuple(o.reshape(shape) for o in outs)


def _sum_rows(parts, name):
    _, r, c = parts.shape

    def body(p_ref, o_ref):
        g = p_ref[0]
        for k in range(1, NDEV):
            g = g + p_ref[k]
        o_ref[...] = g

    return pl.pallas_call(
        body, name=name, out_shape=jax.ShapeDtypeStruct((r, c), F32),
    )(parts)


def _prep(cc, w_mod_full, b_mod, lbl):
    def body(cc_ref, w_ref, b_ref, l_ref, mod_ref, lb_ref):
        t = cc_ref[...]
        s = (t * _sigmoid(t)).astype(BF16)
        mod_ref[...] = _dot(s, w_ref[...]) + b_ref[...]
        lb_ref[...] = _sigmoid(l_ref[0] - l_ref[1])

    return pl.pallas_call(
        body, name="prep",
        out_shape=[jax.ShapeDtypeStruct((8, 3 * D), F32), jax.ShapeDtypeStruct((8, D), F32)],
    )(cc, w_mod_full, b_mod, lbl)


def _modulate(xin, mod, row, name):
    n = xin.shape[0]
    tm = _row_tile(n, 512)

    def body(x_ref, mod_ref, u_ref):
        sh = mod_ref[row:row + 1, 0:D]
        sc = mod_ref[row:row + 1, D:2 * D]
        u_ref[...] = (x_ref[...] * (1.0 + sc) + sh).astype(BF16)

    return pl.pallas_call(
        body, name=name, grid=(n // tm,),
        out_shape=jax.ShapeDtypeStruct((n, D), BF16),
        in_specs=[pl.BlockSpec((tm, D), lambda i: (i, 0)), pl.BlockSpec((8, 3 * D), lambda i: (0, 0))],
        out_specs=pl.BlockSpec((tm, D), lambda i: (i, 0)),
    )(xin, mod)


def _mm_bias(a, w_all, bias, name):
    m, k = a.shape
    tn = w_all.shape[2]
    n = tn * NDEV
    tm = _row_tile(m, 512)

    def body(a_ref, b_ref, bias_ref, o_ref):
        o_ref[...] = _dot(a_ref[...], b_ref[0]) + bias_ref[...]

    return pl.pallas_call(
        body, name=name, grid=(NDEV, m // tm),
        out_shape=jax.ShapeDtypeStruct((m, n), F32),
        in_specs=[pl.BlockSpec((tm, k), lambda j, i: (i, 0)), pl.BlockSpec((1, k, tn), lambda j, i: (j, 0, 0)),
                  pl.BlockSpec((1, tn), lambda j, i: (0, j))],
        out_specs=pl.BlockSpec((tm, tn), lambda j, i: (i, j)),
    )(a, w_all, bias)


def _mm_tn(a, b, init, name, with_colsum=False, colsum_init=None, out_dtype=F32):
    m, ka = a.shape
    n = b.shape[1]
    tk = _row_tile(m, 512)
    tn = 1024
    nk = m // tk
    has_init = init is not None

    def body(*refs):
        a_ref, b_ref = refs[0], refs[1]
        pos = 2
        init_ref = cs_init_ref = None
        if has_init:
            init_ref = refs[pos]
            pos += 1
            if with_colsum:
                cs_init_ref = refs[pos]
                pos += 1
        o_ref = refs[pos]
        cs_ref = refs[pos + 1] if with_colsum else None
        acc = refs[-1]
        k = pl.program_id(1)

        @pl.when(k == 0)
        def _():
            if has_init:
                acc[...] = init_ref[...]
                if with_colsum:
                    cs_ref[...] = cs_init_ref[...]
            else:
                acc[...] = jnp.zeros_like(acc)
                if with_colsum:
                    cs_ref[...] = jnp.zeros_like(cs_ref)

        bv = b_ref[...]
        acc[...] += _dot_tn(a_ref[...], bv)
        if with_colsum:
            cs_ref[...] += jnp.sum(bv.astype(F32), axis=0, keepdims=True)

        @pl.when(k == nk - 1)
        def _():
            o_ref[...] = acc[...].astype(out_dtype)

    in_specs = [pl.BlockSpec((tk, ka), lambda j, k: (k, 0)), pl.BlockSpec((tk, tn), lambda j, k: (k, j))]
    args = [a, b]
    if has_init:
        in_specs.append(pl.BlockSpec((ka, tn), lambda j, k: (0, j)))
        args.append(init)
        if with_colsum:
            in_specs.append(pl.BlockSpec((1, tn), lambda j, k: (0, j)))
            args.append(colsum_init)
    out_shape = [jax.ShapeDtypeStruct((ka, n), out_dtype)]
    out_specs = [pl.BlockSpec((ka, tn), lambda j, k: (0, j))]
    if with_colsum:
        out_shape.append(jax.ShapeDtypeStruct((1, n), F32))
        out_specs.append(pl.BlockSpec((1, tn), lambda j, k: (0, j)))
    outs = pl.pallas_call(
        body, name=name, grid=(n // tn, nk), out_shape=out_shape, in_specs=in_specs, out_specs=out_specs,
        scratch_shapes=[pltpu.VMEM((ka, tn), F32)],
    )(*args)
    return outs if with_colsum else outs[0]


def _input_grad(dz, w_all, xin, dr, mod, row, name, side=(), side_splits=()):
    m, n = dz.shape
    tm = _row_tile(m, 512)
    tk = w_all.shape[2]
    nk = NDEV
    ni = m // tm
    has_dr = dr is not None
    ns = len(side)
    piece_shapes, piece_of = _pieces(side, side_splits)

    def body(*refs):
        dz_ref, w_ref, x_ref = refs[:3]
        pos = 3
        dr_ref = refs[pos] if has_dr else None
        pos += int(has_dr)
        mod_ref = refs[pos]
        side_in = refs[pos + 1:pos + 1 + ns]
        pos += 1 + ns
        gx_ref = refs[pos] if has_dr else None
        pos += int(has_dr)
        vec_ref = refs[pos]
        side_out = refs[pos + 1:pos + 1 + ns]
        acc = refs[pos + 1 + ns]
        i, k = pl.program_id(0), pl.program_id(1)
        if ns:
            side_start, side_finish = _exchange(side_in, side_out, refs[pos + 2 + ns:], piece_of)

            @pl.when((i == 0) & (k == 0))
            def _():
                side_start()

        @pl.when(k == 0)
        def _():
            acc[...] = jnp.zeros_like(acc)

        @pl.when((i == 0) & (k == 0))
        def _():
            vec_ref[...] = jnp.zeros_like(vec_ref)

        acc[...] += _dot_nt(dz_ref[...], w_ref[0])

        @pl.when(k == nk - 1)
        def _():
            du = acc[...]
            xv = x_ref[...]
            if has_dr:
                sc = mod_ref[row:row + 1, D:2 * D]
                gx_ref[...] = ALPHA * dr_ref[...] + du * (1.0 + sc)
            vec_ref[0:1, :] += jnp.sum(du, axis=0, keepdims=True)
            vec_ref[1:2, :] += jnp.sum(du * xv, axis=0, keepdims=True)

        if ns:
            @pl.when((i == ni - 1) & (k == nk - 1))
            def _():
                side_finish()

    row_spec = pl.BlockSpec((tm, D), lambda i, k: (i, 0))
    in_specs = [pl.BlockSpec((tm, tk), lambda i, k: (i, k)), pl.BlockSpec((1, D, tk), lambda i, k: (k, 0, 0)), row_spec]
    args = [dz, w_all, xin]
    if has_dr:
        in_specs.append(row_spec)
        args.append(dr)
    in_specs.append(pl.BlockSpec((8, 3 * D), lambda i, k: (0, 0)))
    args.append(mod)
    in_specs += [ANY] * ns
    args += list(side)
    out_shape, out_specs = [], []
    if has_dr:
        out_shape.append(jax.ShapeDtypeStruct((m, D), F32))
        out_specs.append(row_spec)
    out_shape.append(jax.ShapeDtypeStruct((8, D), F32))
    out_specs.append(pl.BlockSpec((8, D), lambda i, k: (0, 0)))
    out_shape += [jax.ShapeDtypeStruct((NDEV,) + piece_shapes[a], side[a].dtype) for a in range(ns)]
    out_specs += [ANY] * ns
    outs = pl.pallas_call(
        body, name=name, grid=(ni, nk), out_shape=out_shape, in_specs=in_specs, out_specs=out_specs,
        scratch_shapes=[pltpu.VMEM((tm, D), F32)] + (_exchange_sems(ns) if ns else []),
    )(*args)
    return tuple(outs) if has_dr else (None, *outs)


def _tri(reverse):
    r = lax.broadcasted_iota(jnp.int32, (CHUNK, CHUNK), 0)
    c = lax.broadcasted_iota(jnp.int32, (CHUNK, CHUNK), 1)
    return (c >= r) if reverse else (c <= r)


def _cum_f32(tri_b, t):
    hi = t.astype(BF16)
    r1 = t - hi.astype(F32)
    mid = r1.astype(BF16)
    lo = (r1 - mid.astype(F32)).astype(BF16)
    return _dot(tri_b, hi) + _dot(tri_b, mid) + _dot(tri_b, lo)


def _gla_features(zq, zf, lb):
    sq = _sigmoid(zq)
    q = zq * sq * Q_SCALE
    sf = _sigmoid(zf)
    f = lb + (1.0 - lb) * sf
    return q, sq, f, sf


def _gla_decays(f, tri_b, last):
    lf = jnp.log(f)
    g = _cum_f32(tri_b, lf)
    gl = g[last:last + 1, :]
    return g, gl


def _gla_block(n):
    return 256 if n % 256 == 0 else CHUNK


def _gla_fwd(z, lb, s0, d, name, side=()):
    n = z.shape[0]
    blk = _gla_block(n)
    nb, npb = n // blk, blk // CHUNK
    reverse = d == 1
    last = 0 if reverse else CHUNK - 1
    order = list(range(npb))[::-1] if reverse else list(range(npb))
    ns = len(side)

    def bmap(i):
        return nb - 1 - i if reverse else i

    hp = GLA_HEADS_PER_STEP
    hw = hp * DH
    units = [(hh, cidx) for hh in range(hp) for cidx in order]

    def body(zq_ref, zf_ref, zv_ref, lb_ref, s0_ref, *rest):
        side_in = rest[:ns]
        o_ref, ss_ref, sf_ref = rest[ns:ns + 3]
        side_out = rest[ns + 3:2 * ns + 3]
        st = rest[2 * ns + 3]
        i = pl.program_id(1)
        if ns:
            side_start, side_finish = _exchange(side_in, side_out, rest[2 * ns + 4:])

            @pl.when((pl.program_id(0) == 0) & (i == 0))
            def _():
                side_start()

        @pl.when(i == 0)
        def _():
            st[...] = s0_ref[...]

        mask = _tri(reverse)
        tri_b = jnp.where(mask, 1.0, 0.0).astype(BF16)
        feat = {}
        for u in units:
            hh, cidx = u
            rows, cols = pl.ds(cidx * CHUNK, CHUNK), pl.ds(hh * DH, DH)
            q, _, f, _ = _gla_features(zq_ref[rows, cols], zf_ref[rows, cols], lb_ref[d:d + 1, cols])
            feat[u] = (q, 1.0 - f, jnp.log(f), zv_ref[rows, cols].astype(BF16))
        dec = {u: _cum_f32(tri_b, feat[u][2]) for u in units}
        ops = {}
        for u in units:
            q, k, _, vb = feat[u]
            g = dec[u]
            gl = g[last:last + 1, :]
            ops[u] = ((q * jnp.exp(g)).astype(BF16), (k * jnp.exp(-g)).astype(BF16),
                      (k * jnp.exp(gl - g)).astype(BF16), jnp.exp(gl), vb)
        att = {u: jnp.where(mask, _dot_nt(ops[u][0], ops[u][1]), 0.0).astype(BF16) for u in units}
        upd = {u: _dot_tn(ops[u][4], ops[u][2]) for u in units}
        intra = {u: _dot(att[u], ops[u][4]) for u in units}
        s_in = {}
        for hh in range(hp):
            s = st[hh]
            ss_ref[hh, 0] = s
            for cidx in order:
                s_in[(hh, cidx)] = s
                s = s * ops[(hh, cidx)][3] + upd[(hh, cidx)]
            st[hh] = s
            sf_ref[hh] = s
        for u in units:
            hh, cidx = u
            rows, cols = pl.ds(cidx * CHUNK, CHUNK), pl.ds(hh * DH, DH)
            o_ref[rows, cols] = intra[u] + _dot_nt(ops[u][0], s_in[u].astype(BF16))

        if ns:
            @pl.when((pl.program_id(0) == NH // hp - 1) & (i == nb - 1))
            def _():
                side_finish()

    def col(g):
        return lambda h, i: (bmap(i), g * (NH // hp) + h)

    return pl.pallas_call(
        body, name=name, grid=(NH // hp, nb),
        out_shape=[jax.ShapeDtypeStruct((n, D), F32), jax.ShapeDtypeStruct((NH, nb, DH, DH), F32),
                   jax.ShapeDtypeStruct((NH, DH, DH), F32)]
        + [jax.ShapeDtypeStruct((NDEV,) + t.shape, t.dtype) for t in side],
        in_specs=[pl.BlockSpec((blk, hw), col(0)), pl.BlockSpec((blk, hw), col(1 + d)),
                  pl.BlockSpec((blk, hw), col(3)), pl.BlockSpec((8, hw), lambda h, i: (0, h)),
                  pl.BlockSpec((hp, DH, DH), lambda h, i: (h, 0, 0))] + [ANY] * ns,
        out_specs=[pl.BlockSpec((blk, hw), lambda h, i: (bmap(i), h)),
                   pl.BlockSpec((hp, 1, DH, DH), lambda h, i: (h, bmap(i), 0, 0)),
                   pl.BlockSpec((hp, DH, DH), lambda h, i: (h, 0, 0))] + [ANY] * ns,
        scratch_shapes=[pltpu.VMEM((hp, DH, DH), F32)] + (_exchange_sems(ns) if ns else []),
    )(z, z, z, lb, s0, *side)


def _gla_bwd(z, lb, s_start, do, ds_fin, acc_q, acc_v, d, name, f_dtype=F32, into=None, side=(), side_splits=()):
    n = z.shape[0]
    blk = _gla_block(n)
    nb, npb = n // blk, blk // CHUNK
    reverse = d == 1
    last = 0 if reverse else CHUNK - 1
    order = list(range(npb)) if reverse else list(range(npb))[::-1]
    has_do = do is not None
    has_acc = acc_q is not None
    fused = into is not None
    assert not fused or d == 1
    ns = len(side)
    assert not (fused and ns)
    piece_shapes, piece_of = _pieces(side, side_splits)
    hp = NH if fused else GLA_HEADS_PER_STEP
    hw = hp * DH
    units = [(hh, cidx) for hh in range(hp) for cidx in order]

    def bmap(i):
        return i if reverse else nb - 1 - i

    def body(*refs):
        zq_ref, zf_ref, zv_ref, lb_ref, ss_ref, dsf_ref = refs[:6]
        pos = 6
        do_ref = aq_ref = av_ref = None
        if has_do:
            do_ref = refs[pos]
            pos += 1
        if has_acc:
            aq_ref, av_ref = refs[pos], refs[pos + 1]
            pos += 2
        if fused:
            other_ref = refs[pos + 1]
            dz_ref, dlb_ref, ds0_ref, dst = refs[pos + 2:]
            dz_ref[:, D:2 * D] = other_ref[...]
        else:
            side_in = refs[pos:pos + ns]
            dzq_ref, dzf_ref, dzv_ref, dlb_ref, ds0_ref = refs[pos + ns:pos + ns + 5]
            side_out = refs[pos + ns + 5:pos + 2 * ns + 5]
            dst = refs[pos + 2 * ns + 5]
        i = pl.program_id(1)
        if ns:
            side_start, side_finish = _exchange(side_in, side_out, refs[pos + 2 * ns + 6:], piece_of)

            @pl.when((pl.program_id(0) == 0) & (i == 0))
            def _():
                side_start()

        @pl.when(i == 0)
        def _():
            dst[...] = dsf_ref[...]
            dlb_ref[...] = jnp.zeros_like(dlb_ref)

        mask = _tri(reverse)
        tri_b = jnp.where(mask, 1.0, 0.0).astype(BF16)
        tri_t = jnp.where(_tri(not reverse), 1.0, 0.0).astype(BF16)

        def where(u):
            return pl.ds(u[1] * CHUNK, CHUNK), pl.ds(u[0] * DH, DH)

        feat = {}
        for u in units:
            rows, cols = where(u)
            zq, zf = zq_ref[rows, cols], zf_ref[rows, cols]
            lbv = lb_ref[d:d + 1, cols]
            q, sq, f, sf = _gla_features(zq, zf, lbv)
            feat[u] = dict(zq=zq, q=q, sq=sq, f=f, sf=sf, lbv=lbv, k=1.0 - f, vb=zv_ref[rows, cols].astype(BF16))
        dec = {u: _cum_f32(tri_b, jnp.log(feat[u]["f"])) for u in units}
        for u in units:
            w = feat[u]
            g = dec[u]
            gl = g[last:last + 1, :]
            w["eg"], w["egi"], w["ege"], w["egl"] = jnp.exp(g), jnp.exp(-g), jnp.exp(gl - g), jnp.exp(gl)
            w["qd"], w["ki"], w["ke"] = w["q"] * w["eg"], w["k"] * w["egi"], w["k"] * w["ege"]
            w["qdb"], w["kib"], w["keb"] = w["qd"].astype(BF16), w["ki"].astype(BF16), w["ke"].astype(BF16)
        fwd_order = order[::-1]
        for hh in range(hp):
            s = ss_ref[hh, 0]
            for cidx in fwd_order:
                w = feat[(hh, cidx)]
                w["s_in"] = s
                if cidx != fwd_order[-1]:
                    s = s * w["egl"] + _dot_tn(w["vb"], w["keb"])
        if has_do:
            for u in units:
                w = feat[u]
                rows, cols = where(u)
                w["dob"] = do_ref[rows, cols].astype(BF16)
            for u in units:
                w = feat[u]
                w["a"] = jnp.where(mask, _dot_nt(w["qdb"], w["kib"]), 0.0).astype(BF16)
                w["da"] = jnp.where(mask, _dot_nt(w["dob"], w["vb"]), 0.0).astype(BF16)
                w["m"] = _dot_tn(w["dob"], w["qdb"])
        for hh in range(hp):
            ds = dst[hh]
            for cidx in order:
                w = feat[(hh, cidx)]
                w["ds"] = ds
                ds = ds * w["egl"]
                if has_do:
                    ds = ds + w["m"]
            dst[hh] = ds
            ds0_ref[hh] = ds
        for u in units:
            w = feat[u]
            dsb = w["ds"].astype(BF16)
            w["dke"] = _dot(w["vb"], dsb)
            w["dv"] = _dot_nt(w["keb"], dsb)
            if has_do:
                w["dv"] = w["dv"] + _dot_tn(w["a"], w["dob"])
                w["dqd"] = _dot(w["da"], w["kib"]) + _dot(w["dob"], w["s_in"].astype(BF16))
                w["dki"] = _dot_tn(w["da"], w["qdb"])
        for u in units:
            w = feat[u]
            dkeke = w["dke"] * w["ke"]
            w["dgl"] = (w["egl"] * jnp.sum(w["s_in"] * w["ds"], axis=0, keepdims=True)
                        + jnp.sum(dkeke, axis=0, keepdims=True))
            dg = -dkeke
            dk = w["dke"] * w["ege"]
            if has_do:
                dg = dg + w["dqd"] * w["qd"] - w["dki"] * w["ki"]
                dk = dk + w["dki"] * w["egi"]
            w["dg"], w["dk"] = dg, dk
        dlf = {u: _cum_f32(tri_t, feat[u]["dg"]) for u in units}
        for u in units:
            w = feat[u]
            rows, cols = where(u)
            df = (dlf[u] + w["dgl"]) / w["f"] - w["dk"]
            sf = w["sf"]
            dzf = df * (1.0 - w["lbv"]) * sf * (1.0 - sf)
            dlb_ref[0:1, cols] += jnp.sum(df * (1.0 - sf), axis=0, keepdims=True)
            if has_do:
                dzq = w["dqd"] * w["eg"] * (Q_SCALE * _dsilu(w["zq"], w["sq"]))
            else:
                dzq = jnp.zeros((CHUNK, DH), F32)
            dv = w["dv"]
            if has_acc:
                dzq = dzq + aq_ref[rows, cols]
                dv = dv + av_ref[rows, cols]
            if fused:
                lane = u[0] * DH
                dz_ref[rows, pl.ds(lane, DH)] = dzq.astype(BF16)
                dz_ref[rows, pl.ds(2 * D + lane, DH)] = dzf.astype(BF16)
                dz_ref[rows, pl.ds(3 * D + lane, DH)] = dv.astype(BF16)
            else:
                dzq_ref[rows, cols] = dzq
                dzf_ref[rows, cols] = dzf.astype(f_dtype)
                dzv_ref[rows, cols] = dv

        if ns:
            @pl.when((pl.program_id(0) == NH // hp - 1) & (i == nb - 1))
            def _():
                side_finish()

    def col(g):
        return lambda h, i: (bmap(i), g * (NH // hp) + h)

    tok = pl.BlockSpec((blk, hw), lambda h, i: (bmap(i), h))
    state = pl.BlockSpec((hp, DH, DH), lambda h, i: (h, 0, 0))
    in_specs = [pl.BlockSpec((blk, hw), col(0)), pl.BlockSpec((blk, hw), col(1 + d)), pl.BlockSpec((blk, hw), col(3)),
                pl.BlockSpec((8, hw), lambda h, i: (0, h)),
                pl.BlockSpec((hp, 1, DH, DH), lambda h, i: (h, bmap(i), 0, 0)), state]
    args = [z, z, z, lb, s_start, ds_fin]
    if has_do:
        in_specs.append(tok)
        args.append(do)
    if has_acc:
        in_specs += [tok, tok]
        args += [acc_q, acc_v]
    tail_shape = [jax.ShapeDtypeStruct((8, D), F32), jax.ShapeDtypeStruct((NH, DH, DH), F32)]
    tail_specs = [pl.BlockSpec((8, hw), lambda h, i: (0, h)), state]
    if fused:
        buf, other = into
        aliases = {len(args): 0}
        in_specs += [ANY, tok]
        args += [buf, other]
        out_shape = [jax.ShapeDtypeStruct(buf.shape, buf.dtype)] + tail_shape
        out_specs = [pl.BlockSpec((blk, 4 * D), lambda h, i: (bmap(i), 0))] + tail_specs
    else:
        aliases = {}
        in_specs += [ANY] * ns
        args += list(side)
        out_shape = [jax.ShapeDtypeStruct((n, D), F32), jax.ShapeDtypeStruct((n, D), f_dtype),
                     jax.ShapeDtypeStruct((n, D), F32)] + tail_shape
        out_shape += [jax.ShapeDtypeStruct((NDEV,) + piece_shapes[a], side[a].dtype) for a in range(ns)]
        out_specs = [tok, tok, tok] + tail_specs + [ANY] * ns
    return pl.pallas_call(
        body, name=name, grid=(NH // hp, nb), out_shape=out_shape, in_specs=in_specs, out_specs=out_specs,
        input_output_aliases=aliases,
        scratch_shapes=[pltpu.VMEM((hp, DH, DH), F32)] + (_exchange_sems(ns) if ns else []),
    )(*args)


def _shift(t, s, fill, down):
    n = t.shape[0]
    rows = lax.broadcasted_iota(jnp.int32, t.shape, 0)
    if down:
        return jnp.where(rows >= s, pltpu.roll(t, s, 0), fill)
    return jnp.where(rows < n - s, pltpu.roll(t, n - s, 0), fill)


SUBLANES = 8
LRU_SAVED = 4


def _chain_scan(a, b, h_in, down):
    n = a.shape[0]
    ng = n // SUBLANES
    rows = lax.broadcasted_iota(jnp.int32, (SUBLANES, a.shape[1]), 0)
    local = []
    for g in range(ng):
        aa, bb = a[g * SUBLANES:(g + 1) * SUBLANES], b[g * SUBLANES:(g + 1) * SUBLANES]
        for s in (1, 2, 4):
            if down:
                keep, amt = rows >= s, s
            else:
                keep, amt = rows < SUBLANES - s, SUBLANES - s
            bb = bb + aa * jnp.where(keep, pltpu.roll(bb, amt, 0), 0.0)
            aa = aa * jnp.where(keep, pltpu.roll(aa, amt, 0), 1.0)
        local.append((aa, bb))
    out = [None] * ng
    carry = h_in
    for g in (range(ng) if down else range(ng - 1, -1, -1)):
        aa, bb = local[g]
        hg = bb + aa * carry
        out[g] = hg
        carry = hg[SUBLANES - 1:SUBLANES] if down else hg[0:1]
    return (jnp.concatenate(out, axis=0) if ng > 1 else out[0]), carry


def _conv_taps(xv):
    return (_shift(xv, 1, 0.0, True), xv, _shift(xv, 1, 0.0, False), _shift(xv, 2, 0.0, False))


def _conv(taps, cw, cb):
    return cb + cw[0:1, :] * taps[0] + cw[1:2, :] * taps[1] + cw[2:3, :] * taps[2] + cw[3:4, :] * taps[3]


def _neg_expm1(t):
    series = -t * (1.0 + t * (0.5 + t * (1.0 / 6.0 + t * (1.0 / 24.0 + t * (1.0 / 120.0)))))
    return jnp.where(t > -0.1, series, 1.0 - jnp.exp(t))


def _lru_gates(xc, wr, br, wi, bi, lam):
    xcb = xc.astype(BF16)
    r = _sigmoid(_dot(xcb, wr) + br)
    gi = _sigmoid(_dot(xcb, wi) + bi)
    sp = jnp.maximum(-lam, 0.0) + jnp.log(1.0 + jnp.exp(-jnp.abs(lam)))
    la = -RG_C * r * sp
    a = jnp.exp(la)
    mult = jnp.sqrt(_neg_expm1(2.0 * la))
    return xcb, r, gi, sp, a, mult


def _lru_fwd(xin, blk, cw, cb, wr, br, wi, bi, lam, h0, acc_h, d, name, side=()):
    n = xin.shape[0]
    nb = n // blk
    reverse = d == 1
    down = not reverse
    has_acc = acc_h is not None
    ns = len(side)

    def bmap(i):
        return nb - 1 - i if reverse else i

    def body(*refs):
        x_ref, cw_ref, cb_ref, wr_ref, br_ref, wi_ref, bi_ref, lam_ref, h0_ref = refs[:9]
        pos = 9
        acc_ref = refs[pos] if has_acc else None
        pos += int(has_acc)
        side_in = refs[pos:pos + ns]
        pos += ns
        h_ref, hin_ref, hfin_ref, sav_a_ref, sav_ref = refs[pos:pos + 5]
        pos += 5
        hsum_ref = refs[pos] if has_acc else None
        pos += int(has_acc)
        side_out = refs[pos:pos + ns]
        carry = refs[pos + ns]
        i = pl.program_id(0)
        if ns:
            side_start, side_finish = _exchange(side_in, side_out, refs[pos + ns + 1:])

            @pl.when(i == 0)
            def _():
                side_start()

        @pl.when(i == 0)
        def _():
            carry[...] = h0_ref[...]

        for g in range(NH):
            cols = pl.ds(g * DH, DH)
            xc = _conv(_conv_taps(x_ref[:, cols]), cw_ref[:, cols], cb_ref[:, cols])
            _, r, gi, _, a, mult = _lru_gates(xc, wr_ref[g], br_ref[:, cols], wi_ref[g], bi_ref[:, cols],
                                              lam_ref[:, cols])
            sav_a_ref[:, cols] = a
            for slot, val in enumerate((xc, r, gi, mult)):
                sav_ref[slot, :, cols] = val.astype(BF16)
            hin = carry[:, cols]
            h, h_last = _chain_scan(a, mult * gi * xc, hin, down)
            h_ref[:, cols] = h
            if has_acc:
                hsum_ref[:, cols] = h + acc_ref[:, cols]
            hin_ref[0, :, cols] = hin
            carry[:, cols] = h_last
            hfin_ref[:, cols] = h_last

        if ns:
            @pl.when(i == nb - 1)
            def _():
                side_finish()

    vec = pl.BlockSpec((1, D), lambda i: (0, 0))
    wsp = pl.BlockSpec((NH, DH, DH), lambda i: (0, 0, 0))
    tok = pl.BlockSpec((blk, D), lambda i: (bmap(i), 0))
    in_specs = [tok, pl.BlockSpec((4, D), lambda i: (0, 0)), vec, wsp, vec, wsp, vec, vec, vec]
    args = [xin, cw, cb, wr, br, wi, bi, lam, h0]
    out_shape = [jax.ShapeDtypeStruct((n, D), F32), jax.ShapeDtypeStruct((nb, 1, D), F32),
                 jax.ShapeDtypeStruct((1, D), F32), jax.ShapeDtypeStruct((n, D), F32),
                 jax.ShapeDtypeStruct((LRU_SAVED, n, D), BF16)]
    out_specs = [tok, pl.BlockSpec((1, 1, D), lambda i: (bmap(i), 0, 0)), vec, tok,
                 pl.BlockSpec((LRU_SAVED, blk, D), lambda i: (0, bmap(i), 0))]
    if has_acc:
        in_specs.append(tok)
        args.append(acc_h)
        out_shape.append(jax.ShapeDtypeStruct((n, D), F32))
        out_specs.append(tok)
    in_specs += [ANY] * ns
    args += list(side)
    out_shape += [jax.ShapeDtypeStruct((NDEV,) + t.shape, t.dtype) for t in side]
    out_specs += [ANY] * ns
    return pl.pallas_call(
        body, name=name, grid=(nb,), out_shape=out_shape, in_specs=in_specs, out_specs=out_specs,
        scratch_shapes=[pltpu.VMEM((1, D), F32)] + (_exchange_sems(ns) if ns else []),
    )(*args)


def _lru_bwd(xin, blk, cw, wr, wi, lam, sav, h, hin, dh, cg_fin, acc_dx, init, d, name):
    n = xin.shape[0]
    nb = n // blk
    reverse = d == 1
    down = not reverse
    first = blk - 1 if reverse else 0
    has_dh = dh is not None
    has_acc = acc_dx is not None
    has_init = init is not None

    def bmap(i):
        return i if reverse else nb - 1 - i

    def body(*refs):
        (x_ref, cw_ref, wr_ref, wi_ref, lam_ref, sav_a_ref, sav_ref, h_ref, hin_ref, cgf_ref) = refs[:10]
        pos = 10
        dh_ref = acc_ref = None
        iwr_ref = iwi_ref = ivec_ref = None
        if has_dh:
            dh_ref = refs[pos]
            pos += 1
        if has_acc:
            acc_ref = refs[pos]
            pos += 1
        if has_init:
            iwr_ref, iwi_ref, ivec_ref = refs[pos:pos + 3]
            pos += 3
        dx_ref, dwr_ref, dwi_ref, vec_ref, cg0_ref, carry = refs[pos:]
        i = pl.program_id(0)

        @pl.when(i == 0)
        def _():
            carry[...] = cgf_ref[...]
            if has_init:
                dwr_ref[...] = iwr_ref[...]
                dwi_ref[...] = iwi_ref[...]
                vec_ref[...] = ivec_ref[...]
            else:
                dwr_ref[...] = jnp.zeros_like(dwr_ref)
                dwi_ref[...] = jnp.zeros_like(dwi_ref)
                vec_ref[...] = jnp.zeros_like(vec_ref)

        for g in range(NH):
            cols = pl.ds(g * DH, DH)
            cwv = cw_ref[:, cols]
            lam_v = lam_ref[:, cols]
            taps = _conv_taps(x_ref[:, cols])
            wr_g, wi_g = wr_ref[g], wi_ref[g]
            a = sav_a_ref[:, cols]
            xcb = sav_ref[0, :, cols]
            xc, r, gi, mult = (sav_ref[slot, :, cols].astype(F32) for slot in range(LRU_SAVED))
            sp = jnp.maximum(-lam_v, 0.0) + jnp.log(1.0 + jnp.exp(-jnp.abs(lam_v)))
            hprev = _shift(h_ref[:, cols], 1, hin_ref[0, :, cols], down)
            a_next = _shift(a, 1, 1.0, not down)
            dhv = dh_ref[:, cols] if has_dh else jnp.zeros_like(a)
            e, _ = _chain_scan(a_next, dhv, carry[:, cols], not down)
            cg = a[first:first + 1, :] * e[first:first + 1, :]
            carry[:, cols] = cg
            cg0_ref[:, cols] = cg
            da = e * hprev
            emult = e * mult
            dgi = emult * xc
            dxc = emult * gi
            dla = da * a - (e * gi * xc) * (a * a) / mult
            dr = dla * (-RG_C * sp)
            sneg = 1.0 - _sigmoid(lam_v)
            dpr = dr * r * (1.0 - r)
            dpi = dgi * gi * (1.0 - gi)
            dprb, dpib = dpr.astype(BF16), dpi.astype(BF16)
            dxc = dxc + _dot_nt(dprb, wr_g) + _dot_nt(dpib, wi_g)
            dwr_ref[g] += _dot_tn(xcb, dprb)
            dwi_ref[g] += _dot_tn(xcb, dpib)
            dx = (cwv[0:1, :] * _shift(dxc, 1, 0.0, False) + cwv[1:2, :] * dxc
                  + cwv[2:3, :] * _shift(dxc, 1, 0.0, True) + cwv[3:4, :] * _shift(dxc, 2, 0.0, True))
            if has_acc:
                dx = dx + acc_ref[:, cols]
            dx_ref[:, cols] = dx
            vec_ref[0:1, cols] += jnp.sum(dpr, axis=0, keepdims=True)
            vec_ref[1:2, cols] += jnp.sum(dpi, axis=0, keepdims=True)
            vec_ref[2:3, cols] += jnp.sum(dla * r, axis=0, keepdims=True) * (RG_C * sneg)
            vec_ref[3:4, cols] += jnp.sum(dxc, axis=0, keepdims=True)
            for kk in range(4):
                vec_ref[4 + kk:5 + kk, cols] += jnp.sum(dxc * taps[kk], axis=0, keepdims=True)

    vec = pl.BlockSpec((1, D), lambda i: (0, 0))
    wsp = pl.BlockSpec((NH, DH, DH), lambda i: (0, 0, 0))
    tok = pl.BlockSpec((blk, D), lambda i: (bmap(i), 0))
    vec16 = pl.BlockSpec((16, D), lambda i: (0, 0))
    in_specs = [tok, pl.BlockSpec((4, D), lambda i: (0, 0)), wsp, wsp, vec, tok,
                pl.BlockSpec((LRU_SAVED, blk, D), lambda i: (0, bmap(i), 0)), tok,
                pl.BlockSpec((1, 1, D), lambda i: (bmap(i), 0, 0)), vec]
    args = [xin, cw, wr, wi, lam, sav[0], sav[1], h, hin, cg_fin]
    if has_dh:
        in_specs.append(tok)
        args.append(dh)
    if has_acc:
        in_specs.append(tok)
        args.append(acc_dx)
    if has_init:
        in_specs += [wsp, wsp, vec16]
        args += list(init)
    return pl.pallas_call(
        body, name=name, grid=(nb,),
        out_shape=[jax.ShapeDtypeStruct((n, D), F32), jax.ShapeDtypeStruct((NH, DH, DH), F32),
                   jax.ShapeDtypeStruct((NH, DH, DH), F32), jax.ShapeDtypeStruct((16, D), F32),
                   jax.ShapeDtypeStruct((1, D), F32)],
        in_specs=in_specs, out_specs=[tok, wsp, wsp, vec16, vec],
        scratch_shapes=[pltpu.VMEM((1, D), F32)],
    )(*args)


def _merge(z, o_f, o_b, hx, xin, tgt, mod, gn, p_a, p_b, w_out, ln_g, ln_b):
    n = xin.shape[0]
    tm = _row_tile(n, 128)

    def body(z4_ref, z6_ref, z7_ref, z8_ref, of_ref, ob_ref, hx_ref, x_ref, t_ref, mod_ref, gn_ref,
             pa_ref, pb_ref, wo_ref, lg_ref, lnb_ref,
             dr_ref, do_ref, dhx_ref, dz_ref, oa_o, obb_o, y_o, dya_o, dyb_o, dout_o, vec_ref):
        @pl.when(pl.program_id(0) == 0)
        def _():
            vec_ref[...] = jnp.zeros_like(vec_ref)

        gt = mod_ref[0:1, 2 * D:3 * D]
        gnv = gn_ref[...]
        o = of_ref[...] + ob_ref[...]
        rs = jnp.concatenate(
            [jnp.broadcast_to(lax.rsqrt(jnp.mean(jnp.square(o[:, h * DH:(h + 1) * DH]), axis=1, keepdims=True)
                                        + RMS_EPS), (tm, DH)) for h in range(NH)], axis=1)
        nrm = o * rs
        rn = nrm * gnv
        z4, z6, z7, z8 = z4_ref[...], z6_ref[...], z7_ref[...], z8_ref[...]
        s4, s6, s7, s8 = _sigmoid(z4), _sigmoid(z6), _sigmoid(z7), _sigmoid(z8)
        sg4, sg6 = z4 * s4, z6 * s6
        hxv = hx_ref[...]
        oa = (rn * sg4).astype(BF16)
        obb = (hxv * sg6).astype(BF16)
        ya = _dot(oa, pa_ref[...])
        yb = _dot(obb, pb_ref[...])
        y = (s7 * ya + s8 * yb).astype(BF16)
        out = _dot(y, wo_ref[...])
        xv = x_ref[...]
        rr = ALPHA * xv + gt * out
        mu = jnp.mean(rr, axis=1, keepdims=True)
        cen = rr - mu
        rstd = lax.rsqrt(jnp.mean(cen * cen, axis=1, keepdims=True) + LN_EPS)
        xhat = cen * rstd
        lg = lg_ref[...]
        err = xhat * lg + lnb_ref[...] - t_ref[...]
        loss_rows = jnp.sum(err * err, axis=1, keepdims=True)
        dxn = err * (1.0 / D)
        dxh = dxn * lg
        dr = rstd * (dxh - jnp.mean(dxh, axis=1, keepdims=True)
                     - xhat * jnp.mean(dxh * xhat, axis=1, keepdims=True))
        dout = (dr * gt).astype(BF16)
        dy = _dot_nt(dout, wo_ref[...])
        dya = (dy * s7).astype(BF16)
        dyb = (dy * s8).astype(BF16)
        doa = _dot_nt(dya, pa_ref[...])
        dob = _dot_nt(dyb, pb_ref[...])
        drn = doa * sg4
        dn = drn * gnv
        dnn = dn * nrm
        corr = jnp.concatenate(
            [jnp.broadcast_to(jnp.mean(dnn[:, h * DH:(h + 1) * DH], axis=1, keepdims=True), (tm, DH))
             for h in range(NH)], axis=1)
        dr_ref[...] = dr
        do_ref[...] = rs * (dn - nrm * corr)
        dhx_ref[...] = dob * sg6
        dz_ref[:, 0:4 * D] = jnp.zeros((tm, 4 * D), BF16)
        dz_ref[:, 4 * D:5 * D] = (doa * rn * _dsilu(z4, s4)).astype(BF16)
        dz_ref[:, 5 * D:6 * D] = jnp.zeros((tm, D), BF16)
        dz_ref[:, 6 * D:7 * D] = (dob * hxv * _dsilu(z6, s6)).astype(BF16)
        dz_ref[:, 7 * D:8 * D] = (dy * ya * s7 * (1.0 - s7)).astype(BF16)
        dz_ref[:, 8 * D:9 * D] = (dy * yb * s8 * (1.0 - s8)).astype(BF16)
        oa_o[...] = oa
        obb_o[...] = obb
        y_o[...] = y
        dya_o[...] = dya
        dyb_o[...] = dyb
        dout_o[...] = dout
        vec_ref[0:1, :] += jnp.sum(dr * out, axis=0, keepdims=True)
        vec_ref[1:2, :] += jnp.sum(dxn * xhat, axis=0, keepdims=True)
        vec_ref[2:3, :] += jnp.sum(dxn, axis=0, keepdims=True)
        vec_ref[3:4, :] += jnp.sum(drn * nrm, axis=0, keepdims=True)
        vec_ref[4:5, :] += jnp.broadcast_to(jnp.sum(loss_rows, axis=0, keepdims=True) * (0.5 / D), (1, D))

    def grp(g):
        return pl.BlockSpec((tm, D), lambda i: (i, g))

    tok = pl.BlockSpec((tm, D), lambda i: (i, 0))
    vec = pl.BlockSpec((1, D), lambda i: (0, 0))
    wsp = pl.BlockSpec((D, D), lambda i: (0, 0))
    return pl.pallas_call(
        body, name="merge", grid=(n // tm,),
        out_shape=[jax.ShapeDtypeStruct((n, D), F32)] * 3
        + [jax.ShapeDtypeStruct((n, NGRP * D), BF16)]
        + [jax.ShapeDtypeStruct((n, D), BF16)] * 6 + [jax.ShapeDtypeStruct((8, D), F32)],
        in_specs=[grp(4), grp(6), grp(7), grp(8), tok, tok, tok, tok, tok,
                  pl.BlockSpec((8, 3 * D), lambda i: (0, 0)), vec, wsp, wsp, wsp, vec, vec],
        out_specs=[tok, tok, tok, pl.BlockSpec((tm, NGRP * D), lambda i: (i, 0))] + [tok] * 6
        + [pl.BlockSpec((8, D), lambda i: (0, 0))],
    )(z, z, z, z, o_f, o_b, hx, xin, tgt, mod, gn, p_a, p_b, w_out, ln_g, ln_b)


def _wmod_grad(c_t, cctx_t, dmx_loc, dmc_loc, name):
    n = dmx_loc.shape[1]

    def body(ct_ref, cc_ref, dmx_ref, dmc_ref, o_ref):
        ct = ct_ref[...]
        sct = ct * _sigmoid(ct)
        cc = cc_ref[...]
        scc = cc * _sigmoid(cc)
        dmc = dmc_ref[0:1, :]
        for b in range(1, NDEV):
            dmc = dmc + dmc_ref[b:b + 1, :]
        acc = scc * dmc
        for b in range(NDEV):
            acc = acc + sct[:, b:b + 1] * dmx_ref[b:b + 1, :]
        o_ref[...] = acc

    return pl.pallas_call(body, name=name, out_shape=jax.ShapeDtypeStruct((D, n), F32))(c_t, cctx_t, dmx_loc, dmc_loc)


PACK_ROWS = 40


def _finalize_small(g_pack, lb, w_mod_full, params):
    npar = len(params)

    def body(*refs):
        gp_ref, lb_ref, wm_ref = refs[:3]
        wmv = refs[3:3 + 3 * npar]
        loss_ref = refs[3 + 3 * npar]
        g_refs = refs[4 + 3 * npar:4 + 4 * npar]
        upd = refs[4 + 4 * npar:4 + 7 * npar]
        tot = refs[-1]
        acc = gp_ref[0]
        for k in range(1, NDEV):
            acc = acc + gp_ref[k]
        tot[...] = acc
        mine = pl.ds(pl.multiple_of(_my_index() * DH, DH), DH)
        (g_cctx, g_bmod, g_bin, g_lbl, g_norm, g_cw, g_cb, g_br, g_bi, g_lam, g_lng, g_lnb) = g_refs

        loss_ref[...] = jnp.broadcast_to(tot[36:37, 0:DH], (8, DH))
        for k in range(3):
            g_bmod[:, k * D:(k + 1) * D] = tot[k:k + 1, :] + tot[3 + k:4 + k, :]
        dmc = jnp.concatenate([tot[3:4, :], tot[4:5, :], tot[5:6, :]], axis=1)
        cv = wmv[0][...]
        proj = _dot_nt(jnp.broadcast_to(dmc, (8, 3 * D)).astype(BF16), wm_ref[...])
        g_cctx[...] = proj[0:1, :] * _dsilu(cv, _sigmoid(cv))
        for k in range(NGRP):
            g_bin[:, k * D:(k + 1) * D] = tot[6 + k:7 + k, :]
        nrm = tot[15:16, 0:DH]
        for h in range(1, NH):
            nrm = nrm + tot[15:16, h * DH:(h + 1) * DH]
        g_norm[...] = nrm
        g_lng[...] = tot[16:17, :]
        g_lnb[...] = tot[17:18, :]
        g_cb[...] = tot[21:22, :] + tot[29:30, :]
        g_cw[0] = tot[22:26, mine] + tot[30:34, mine]
        for ref, row in ((g_br, 18), (g_bi, 19), (g_lam, 20)):
            ref[0, 0:1, :] = tot[row:row + 1, mine]
            ref[0, 1:2, :] = tot[row + 8:row + 9, mine]
        lbl = lb_ref[0:2, mine]
        dl0 = tot[34:36, mine] * lbl * (1.0 - lbl)
        g_lbl[0] = dl0
        g_lbl[1] = -dl0
        for p in range(npar):
            d, mm, vv = _adam_math(g_refs[p][...], wmv[3 * p][...], wmv[3 * p + 1][...], wmv[3 * p + 2][...])
            upd[3 * p][...] = d
            upd[3 * p + 1][...] = mm
            upd[3 * p + 2][...] = vv

    flat = [t for wmv in params for t in wmv]
    shapes = [jax.ShapeDtypeStruct(wmv[0].shape, F32) for wmv in params]
    outs = pl.pallas_call(
        body, name="finalize_small",
        out_shape=[jax.ShapeDtypeStruct((8, DH), F32)] + shapes + [s for s in shapes for _ in range(3)],
        scratch_shapes=[pltpu.VMEM((PACK_ROWS, D), F32)],
    )(g_pack, lb, w_mod_full, *flat)
    grads = list(outs[1:1 + npar])
    upd = [tuple(outs[1 + npar + 3 * p:4 + npar + 3 * p]) for p in range(npar)]
    return outs[0], grads, upd


def _to_colmajor(t, rows):
    return t.reshape(rows, GRID_W, D).transpose(1, 0, 2).reshape(rows * GRID_W, D)


def _to_raster(t, rows):
    return t.reshape(GRID_W, rows, D).transpose(1, 0, 2).reshape(rows * GRID_W, D)


def _local_cols(t, me, width):
    return lax.dynamic_slice_in_dim(t, me * width, width, axis=t.ndim - 1)


def kernel(x, c, ctx, c_ctx, w_mod, b_mod, w_in, b_in, lb_logits, norm_a_g, conv_w, conv_b, w_r, b_r, w_i, b_i, lam, p_a, p_b, w_out, ln_g, ln_b, loss_target, m_c_ctx, m_w_mod, m_b_mod, m_w_in, m_b_in, m_lb_logits, m_norm_a_g, m_conv_w, m_conv_b, m_w_r, m_b_r, m_w_i, m_b_i, m_lam, m_p_a, m_p_b, m_w_out, m_ln_g, m_ln_b, v_c_ctx, v_w_mod, v_b_mod, v_w_in, v_b_in, v_lb_logits, v_norm_a_g, v_conv_w, v_conv_b, v_w_r, v_b_r, v_w_i, v_b_i, v_lam, v_p_a, v_p_b, v_w_out, v_ln_g, v_ln_b):
    me = _my_index()
    xs, cs, tgt = x[0], ctx[0], loss_target[0]
    t_len, c_len = xs.shape[0], cs.shape[0]
    rows = t_len // GRID_W
    wcols = w_in.shape[2]
    mcols = w_mod.shape[2]

    small = jnp.concatenate([lb_logits.reshape(4, DH), conv_w[0], b_r[0], b_i[0], lam[0], jnp.zeros((2, DH), F32),
                             c.reshape(8, DH)], axis=0)
    g_small, g_wmod = _all_gather([small, w_mod[0].astype(BF16)], "gather_params")

    def full_rows(lo, hi):
        return g_small[:, lo:hi, :].transpose(1, 0, 2).reshape(hi - lo, D)

    lbl_f, cw_f, br_f, bi_f, lam_f = full_rows(0, 4), full_rows(4, 8), full_rows(8, 10), full_rows(10, 12), full_rows(12, 14)
    c_all = g_small[:, 16:24, :].reshape(NDEV, D)
    w_mod_f = g_wmod.transpose(1, 0, 2).reshape(D, 3 * D)
    w_r_b, w_i_b = w_r[0].astype(BF16), w_i[0].astype(BF16)

    cc = jnp.concatenate([c.reshape(1, D), c_ctx.reshape(1, D), jnp.zeros((6, D), F32)], axis=0)
    lbl_p = jnp.concatenate([lbl_f.reshape(2, 2, D), jnp.zeros((2, 6, D), F32)], axis=1)
    mod, lb = _prep(cc, w_mod_f, b_mod, lbl_p)
    u_x = _modulate(xs, mod, 0, "modulate_x")
    u_c = _modulate(cs, mod, 1, "modulate_c")
    z_x, w_in_f = _inproj_gather(u_x, w_in[0].astype(BF16), b_in, "inproj_gather")
    z_c = _mm_bias(u_c, w_in_f, b_in, "inproj_c")

    zero_s = jnp.zeros((NH, DH, DH), F32)
    zero_v = jnp.zeros((1, D), F32)
    gla = {}
    out_w = [p_a[0].astype(BF16), p_b[0].astype(BF16), w_out[0].astype(BF16)]
    out_w_f = []
    for d in (0, 1):
        _, ssc, sfc = _gla_fwd(z_c, lb, zero_s, d, f"gla_fwd_c{d}")
        o_d, ssx, _, *gathered = _gla_fwd(z_x, lb, sfc, d, f"gla_fwd_x{d}", side=out_w[1:] if d else out_w[:1])
        out_w_f += [t.reshape(D, D) for t in gathered]
        gla[d] = (ssc, ssx, o_d)
    p_a_f, p_b_f, w_out_f = out_w_f

    x5_c = z_c[:, 5 * D:6 * D]
    x5_x = _to_colmajor(z_x[:, 5 * D:6 * D], rows)
    cb2 = conv_b.reshape(1, D)
    lru = {}
    h_sum = None
    for d in (0, 1):
        prm = (cw_f, cb2, w_r_b[d], br_f[d:d + 1], w_i_b[d], bi_f[d:d + 1], lam_f[d:d + 1])
        h_c, hin_c, hfin_c, *sav_c = _lru_fwd(x5_c, c_len, *prm, zero_v, None, d, f"lru_fwd_c{d}")
        h_x, hin_x, _, sav_a, sav_h, *h_sum = _lru_fwd(x5_x, rows, *prm, hfin_c, lru[0][3] if d else None, d,
                                                       f"lru_fwd_x{d}")
        lru[d] = ((cw_f, w_r_b[d], w_i_b[d], lam_f[d:d + 1]), h_c, hin_c, h_x, hin_x, tuple(sav_c), (sav_a, sav_h))
    hx = _to_raster(h_sum[0], rows)

    gn = jnp.tile(norm_a_g.reshape(1, DH), (1, NH))
    (dr, do, dhx, dz_m, oa, obb, yb16, dya, dyb, dout, mvec) = _merge(
        z_x, gla[0][2], gla[1][2], hx, xs, tgt, mod, gn, p_a_f, p_b_f, w_out_f, ln_g, ln_b)

    dhx_cm = _to_colmajor(dhx, rows)
    lru_dx_x = lru_dx_c = None
    for d in (0, 1):
        prm, h_c, hin_c, h_x, hin_x, sav_c, sav_x = lru[d]
        lru_dx_x, dwr, dwi, lvec, cg0 = _lru_bwd(x5_x, rows, *prm, sav_x, h_x, hin_x, dhx_cm, zero_v, lru_dx_x, None,
                                                 d, f"lru_bwd_x{d}")
        lru_dx_c, dwr, dwi, lvec, _ = _lru_bwd(x5_c, c_len, *prm, sav_c, h_c, hin_c, None, cg0, lru_dx_c,
                                               (dwr, dwi, lvec), d, f"lru_bwd_c{d}")
        lru[d] = (dwr, dwi, lvec)
    dz5_x = _to_raster(lru_dx_x, rows).astype(BF16)
    dz5_c = lru_dx_c.astype(BF16)

    dpa = _mm_tn(oa, dya, None, "dpa", out_dtype=BF16)
    dpb = _mm_tn(obb, dyb, None, "dpb", out_dtype=BF16)
    dwo = _mm_tn(yb16, dout, None, "dwout", out_dtype=BF16)
    wr_pack = jnp.concatenate([lru[0][0], lru[1][0], lru[0][1], lru[1][1]], axis=0).reshape(4 * NH * DH, DH)

    gq_c = gv_c = None
    dzf_c, dlb = {}, {}
    gq_x, dzf_x0, gv_x, dlb_x, ds0, r_pa, r_pb, r_wo, r_wri = _gla_bwd(
        z_x, lb, gla[0][1], do, zero_s, None, None, 0, "gla_bwd_x0", f_dtype=BF16,
        side=[dpa, dpb, dwo, wr_pack], side_splits=[0, 0, 0, 0])
    gq_c, dzf_c[0], gv_c, dlb_c, _ = _gla_bwd(z_c, lb, gla[0][0], None, ds0, None, None, 0, "gla_bwd_c0")
    dlb[0] = dlb_x[0:1] + dlb_c[0:1]
    dz_g, dlb_x, ds0 = _gla_bwd(z_x, lb, gla[1][1], do, zero_s, gq_x, gv_x, 1, "gla_bwd_x1", into=(dz_m, dzf_x0))
    gq_c, dzf_c[1], gv_c, dlb_c, _ = _gla_bwd(z_c, lb, gla[1][0], None, ds0, gq_c, gv_c, 1, "gla_bwd_c1")
    dlb[1] = dlb_x[0:1] + dlb_c[0:1]

    bf = lambda t: t.astype(BF16)
    dz_x = lax.dynamic_update_slice(dz_g, dz5_x, (0, 5 * D))
    zc0 = jnp.zeros((c_len, D), BF16)
    dz_c = jnp.concatenate([bf(gq_c), bf(dzf_c[0]), bf(dzf_c[1]), bf(gv_c), zc0, dz5_c, zc0, zc0, zc0], axis=1)
    dwin_c, dbin_c = _mm_tn(u_c, dz_c, None, "dwin_c", with_colsum=True)

    grad_x, xvec = _input_grad(dz_x, w_in_f, xs, dr, mod, 0, "input_grad_x")
    r_win, dbin = _dwin_exchange(u_x, dz_x, dwin_c, dbin_c, "dwin_exchange")
    _, cvec = _input_grad(dz_c, w_in_f, cs, None, mod, 1, "input_grad_c")
    wri_piece = _sum_rows(r_wri, "sum_w_ri_piece")
    g_w_in, d_w_in, nm_w_in, nv_w_in = _sum_adamw(r_win, w_in, m_w_in, v_w_in, "update_w_in")
    g_p_a, d_p_a, nm_p_a, nv_p_a = _sum_adamw(r_pa, p_a, m_p_a, v_p_a, "update_p_a")
    g_p_b, d_p_b, nm_p_b, nv_p_b = _sum_adamw(r_pb, p_b, m_p_b, v_p_b, "update_p_b")
    g_w_out, d_w_out, nm_w_out, nv_w_out = _sum_adamw(r_wo, w_out, m_w_out, v_w_out, "update_w_out")

    dlb_rows = jnp.concatenate([dlb[0], dlb[1]], axis=0)
    pack = jnp.concatenate([
        xvec[0:1], xvec[1:2], mvec[0:1],
        cvec[0:1], cvec[1:2], jnp.zeros((1, D), F32),
        dbin.reshape(NGRP, D),
        mvec[3:4], mvec[1:2], mvec[2:3],
        lru[0][2][0:8], lru[1][2][0:3],
        lru[1][2][3:8],
        dlb_rows,
        mvec[4:5],
        jnp.zeros((3, D), F32)], axis=0)
    g_pack, g_wri = _all_gather([pack, wri_piece], "gather_small_grads")

    dmx = g_pack[:, 0:3, :].reshape(NDEV, 3 * D)
    dmc = g_pack[:, 3:6, :].reshape(NDEV, 3 * D)
    grad_w_mod = _wmod_grad(c_all.T, c_ctx.reshape(D, 1), _local_cols(dmx, me, mcols), _local_cols(dmc, me, mcols),
                            "grad_w_mod").reshape(1, D, mcols)
    small_params = [(c_ctx.reshape(1, D), m_c_ctx.reshape(1, D), v_c_ctx.reshape(1, D)), (b_mod, m_b_mod, v_b_mod),
                    (b_in, m_b_in, v_b_in), (lb_logits, m_lb_logits, v_lb_logits), (norm_a_g, m_norm_a_g, v_norm_a_g),
                    (conv_w, m_conv_w, v_conv_w), (conv_b, m_conv_b, v_conv_b), (b_r, m_b_r, v_b_r),
                    (b_i, m_b_i, v_b_i), (lam, m_lam, v_lam), (ln_g, m_ln_g, v_ln_g), (ln_b, m_ln_b, v_ln_b)]
    loss_tile, small_g, small_upd = _finalize_small(g_pack, lb, w_mod_f, small_params)
    loss = loss_tile[0, 0]
    (grad_c_ctx, grad_b_mod, grad_b_in, grad_lb_logits, grad_norm_a_g, grad_conv_w, grad_conv_b, grad_b_r, grad_b_i,
     grad_lam, grad_ln_g, grad_ln_b) = small_g
    small_upd[0] = tuple(t.reshape(c_ctx.shape) for t in small_upd[0])
    (o_c_ctx, o_b_mod, o_b_in, o_lb, o_norm, o_conv_w, o_conv_b, o_b_r, o_b_i, o_lam, o_ln_g, o_ln_b) = small_upd

    half = 2 * NH * DH
    g_ri = g_wri.reshape(2 * half, DH)
    grad_w_r, grad_w_i = g_ri[:half].reshape(w_r.shape), g_ri[half:].reshape(w_i.shape)
    d_w_r, nm_w_r, nv_w_r = _adamw(grad_w_r, w_r, m_w_r, v_w_r, "update_w_r")
    d_w_i, nm_w_i, nv_w_i = _adamw(grad_w_i, w_i, m_w_i, v_w_i, "update_w_i")

    d_w_mod, nm_w_mod, nv_w_mod = _adamw(grad_w_mod, w_mod, m_w_mod, v_w_mod, "update_w_mod")

    grads = [grad_c_ctx.reshape(c_ctx.shape), grad_w_mod, grad_b_mod, g_w_in, grad_b_in, grad_lb_logits, grad_norm_a_g,
             grad_conv_w, grad_conv_b, grad_w_r, grad_b_r, grad_w_i, grad_b_i, grad_lam, g_p_a, g_p_b, g_w_out,
             grad_ln_g, grad_ln_b]
    per_kind = []
    for k in range(3):
        per_kind.append([
            o_c_ctx[k], (d_w_mod, nm_w_mod, nv_w_mod)[k], o_b_mod[k], (d_w_in, nm_w_in, nv_w_in)[k], o_b_in[k], o_lb[k],
            o_norm[k], o_conv_w[k], o_conv_b[k], (d_w_r, nm_w_r, nv_w_r)[k], o_b_r[k], (d_w_i, nm_w_i, nv_w_i)[k],
            o_b_i[k], o_lam[k], (d_p_a, nm_p_a, nv_p_a)[k], (d_p_b, nm_p_b, nv_p_b)[k], (d_w_out, nm_w_out, nv_w_out)[k],
            o_ln_g[k], o_ln_b[k]])
    return (loss, grad_x.reshape(x.shape), *grads, *per_kind[0], *per_kind[1], *per_kind[2])
```

```python
import functools

import jax
import jax.numpy as jnp
from jax import lax
from jax.experimental import pallas as pl
from jax.experimental.pallas import tpu as pltpu

F32 = jnp.float32
BF16 = jnp.bfloat16

D = 1024
NH = 8
DH = 128
CHUNK = 64
GLA_HEADS_PER_STEP = 8
GRID_W = 64
NGRP = 9
NDEV = 8
RG_C = 8.0
ALPHA = 2.0 ** 0.25
LN_EPS = 1e-5
RMS_EPS = 1e-6
Q_SCALE = DH ** -0.5
ADAM_LR, ADAM_B1, ADAM_B2, ADAM_EPS, ADAM_WD, ADAM_STEP = 1e-3, 0.9, 0.999, 1e-8, 0.01, 10
ADAM_C1 = 1.0 / (1.0 - ADAM_B1 ** ADAM_STEP)
ADAM_C2 = 1.0 / (1.0 - ADAM_B2 ** ADAM_STEP)

ANY = pl.BlockSpec(memory_space=pl.ANY)


def _sigmoid(t):
    return 1.0 / (1.0 + jnp.exp(-t))


def _dsilu(t, s):
    return s * (1.0 + t * (1.0 - s))


def _dot(a, b):
    return jnp.dot(a, b, preferred_element_type=F32)


def _dot_nt(a, b):
    return lax.dot_general(a, b, (((1,), (1,)), ((), ())), preferred_element_type=F32)


def _dot_tn(a, b):
    return lax.dot_general(a, b, (((0,), (0,)), ((), ())), preferred_element_type=F32)


def _my_index():
    return 4 * lax.axis_index("x") + 2 * lax.axis_index("y") + lax.axis_index("c")


def _dev_tuple(j):
    return (j >> 2, (j >> 1) & 1, j & 1)


def _exchange_sems(n):
    return [pltpu.SemaphoreType.DMA((n * NDEV,)), pltpu.SemaphoreType.DMA((n * NDEV,)), pltpu.SemaphoreType.DMA((n,))]


def _exchange(ins, outs, sems, piece_of=None):
    send_sems, recv_sems, loc_sems = sems
    n = len(ins)

    def src(a, p):
        return ins[a] if piece_of is None else piece_of(ins[a], a, p)

    def push(a, t):
        me, p = _my_index(), _step_peer(t)
        return pltpu.make_async_remote_copy(
            src_ref=src(a, p), dst_ref=outs[a].at[me],
            send_sem=send_sems.at[a * NDEV + t], recv_sem=recv_sems.at[a * NDEV + me],
            device_id=_dev_of(p), device_id_type=pl.DeviceIdType.MESH)

    def local(a):
        me = _my_index()
        return pltpu.make_async_copy(src(a, me), outs[a].at[me], loc_sems.at[a])

    def start():
        for a in range(n):
            local(a).start()
        for t in range(NDEV - 1):
            for a in range(n):
                push(a, t).start()

    def finish():
        me = _my_index()
        for t in range(NDEV - 1):
            for a in range(n):
                push(a, t).wait_send()
        for j in range(NDEV):
            @pl.when(me != j)
            def _():
                for a in range(n):
                    pltpu.make_async_remote_copy(
                        src_ref=src(a, j), dst_ref=outs[a].at[j],
                        send_sem=send_sems.at[a * NDEV], recv_sem=recv_sems.at[a * NDEV + j],
                        device_id=_dev_tuple(j), device_id_type=pl.DeviceIdType.MESH).wait_recv()
        for a in range(n):
            local(a).wait()

    return start, finish


def _all_gather(shards, name):
    n = len(shards)

    def body(*refs):
        start, finish = _exchange(refs[:n], refs[n:2 * n], refs[2 * n:])
        start()
        finish()

    return pl.pallas_call(
        body, name=name,
        out_shape=[jax.ShapeDtypeStruct((NDEV,) + s.shape, s.dtype) for s in shards],
        in_specs=[ANY] * n, out_specs=[ANY] * n, scratch_shapes=_exchange_sems(n),
    )(*shards)


def _pieces(parts, splits):
    shapes = []
    for part, split in zip(parts, splits):
        r, c = part.shape
        shapes.append((r // NDEV, c) if split == 0 else (r, c // NDEV))

    def piece_of(ref, a, j):
        pr, pc = shapes[a]
        if splits[a] == 0:
            start = j * pr if isinstance(j, int) else pl.multiple_of(j * pr, pr)
            return ref.at[pl.ds(start, pr), :]
        start = j * pc if isinstance(j, int) else pl.multiple_of(j * pc, pc)
        return ref.at[:, pl.ds(start, pc)]

    return shapes, piece_of


_STEP_MASKS = ((2, 4, 6, 3, 5, 7, 1, 0), (4, 2, 6, 5, 3, 7, 1, 0))
_GATHER_MASKS = ((0, 1, 2, 4, 3, 5, 6, 7), (0, 1, 4, 2, 5, 3, 6, 7))


def _peer_schedule(table):
    tab = jnp.array(table, jnp.int32)
    return jnp.bitwise_xor(_my_index(), tab[lax.axis_index("c")])


def _step_peer(s, table=_STEP_MASKS):
    def pick(row):
        if isinstance(s, int):
            return jnp.int32(row[s])
        m = jnp.int32(row[NDEV - 1])
        for t in range(NDEV - 2, -1, -1):
            m = jnp.where(s == t, jnp.int32(row[t]), m)
        return m
    mask = jnp.where(lax.axis_index("c") == 0, pick(table[0]), pick(table[1]))
    return jnp.bitwise_xor(_my_index(), mask)


def _dev_of(p):
    return (p // 4, (p // 2) % 2, p % 2)


def _dwin_exchange(u, dz, init, cs_init, name):
    m, ka = u.shape
    n = dz.shape[1]
    pc = n // NDEV
    tk = _row_tile(m, 512)
    nk = m // tk

    def body(pidx_ref, u_ref, dz_ref, init_ref, csi_ref, rwin, cs_ref, acc, sbuf, wsend, wrecv, wloc):
        s, k = pl.program_id(0), pl.program_id(1)
        me = _my_index()

        def slab_copy(slot, p):
            return pltpu.make_async_remote_copy(
                src_ref=sbuf.at[slot], dst_ref=rwin.at[me], send_sem=wsend.at[slot], recv_sem=wrecv.at[me],
                device_id=_dev_of(p), device_id_type=pl.DeviceIdType.MESH)

        @pl.when(k == 0)
        def _():
            acc[...] = init_ref[...]
            cs_ref[...] = csi_ref[...]

        bv = dz_ref[...]
        acc[...] += _dot_tn(u_ref[...], bv)
        cs_ref[...] += jnp.sum(bv.astype(F32), axis=0, keepdims=True)

        @pl.when(k == nk - 1)
        def _():
            slot = s % 2

            @pl.when(s >= 2)
            def _():
                slab_copy(slot, me).wait_send()

            sbuf[slot] = acc[...].astype(BF16)

            @pl.when(s < NDEV - 1)
            def _():
                slab_copy(slot, _step_peer(s)).start()

            @pl.when(s == NDEV - 1)
            def _():
                own = pltpu.make_async_copy(sbuf.at[slot], rwin.at[me], wloc.at[0])
                own.start()
                slab_copy(1 - slot, me).wait_send()
                for j in range(NDEV):
                    @pl.when(me != j)
                    def _():
                        pltpu.make_async_remote_copy(
                            src_ref=sbuf.at[0], dst_ref=rwin.at[j], send_sem=wsend.at[0], recv_sem=wrecv.at[j],
                            device_id=_dev_tuple(j), device_id_type=pl.DeviceIdType.MESH).wait_recv()
                own.wait()

    grid_spec = pltpu.PrefetchScalarGridSpec(
        num_scalar_prefetch=1, grid=(NDEV, nk),
        in_specs=[pl.BlockSpec((tk, ka), lambda s, k, pidx: (k, 0)),
                  pl.BlockSpec((tk, pc), lambda s, k, pidx: (k, pidx[s])),
                  pl.BlockSpec((ka, pc), lambda s, k, pidx: (0, pidx[s])),
                  pl.BlockSpec((1, pc), lambda s, k, pidx: (0, pidx[s]))],
        out_specs=[ANY, pl.BlockSpec((1, pc), lambda s, k, pidx: (0, pidx[s]))],
        scratch_shapes=[pltpu.VMEM((ka, pc), F32), pltpu.VMEM((2, ka, pc), BF16),
                        pltpu.SemaphoreType.DMA((2,)), pltpu.SemaphoreType.DMA((NDEV,)), pltpu.SemaphoreType.DMA((1,))])
    return pl.pallas_call(
        body, name=name, grid_spec=grid_spec,
        out_shape=[jax.ShapeDtypeStruct((NDEV, ka, pc), BF16), jax.ShapeDtypeStruct((1, n), F32)],
    )(_peer_schedule(_STEP_MASKS), u, dz, init, cs_init)


def _inproj_gather(u, w_loc, bias, name):
    m, k = u.shape
    pc = w_loc.shape[1]
    n = pc * NDEV
    tm = _row_tile(m, 512)
    ni = m // tm

    def body(pidx_ref, u_ref, b_ref, wl_ref, z_ref, wall, wbuf, wsend, wrecv, ldsem, ownsem):
        s, i = pl.program_id(0), pl.program_id(1)
        me = _my_index()

        def shard_push(t):
            return pltpu.make_async_remote_copy(
                src_ref=wl_ref, dst_ref=wall.at[me], send_sem=wsend.at[t], recv_sem=wrecv.at[me],
                device_id=_dev_of(_step_peer(t, _GATHER_MASKS)), device_id_type=pl.DeviceIdType.MESH)

        def load(slot, src):
            return pltpu.make_async_copy(src, wbuf.at[slot], ldsem.at[slot])

        own = pltpu.make_async_copy(wl_ref, wall.at[me], ownsem.at[0])

        @pl.when((s == 0) & (i == 0))
        def _():
            own.start()
            load(0, wl_ref).start()
            for t in range(1, NDEV):
                shard_push(t).start()

        @pl.when((i == ni // 2) & (s < NDEV - 1))
        def _():
            nxt = _step_peer(s + 1, _GATHER_MASKS)
            pltpu.make_async_remote_copy(
                src_ref=wl_ref, dst_ref=wall.at[nxt], send_sem=wsend.at[0], recv_sem=wrecv.at[nxt],
                device_id=_dev_of(nxt), device_id_type=pl.DeviceIdType.MESH).wait_recv()
            load((s + 1) % 2, wall.at[nxt]).start()

        @pl.when(i == 0)
        def _():
            load(s % 2, wl_ref).wait()

        z_ref[...] = _dot(u_ref[...], wbuf[s % 2]) + b_ref[...]

        @pl.when((s == NDEV - 1) & (i == ni - 1))
        def _():
            own.wait()
            for t in range(1, NDEV):
                shard_push(t).wait_send()

    grid_spec = pltpu.PrefetchScalarGridSpec(
        num_scalar_prefetch=1, grid=(NDEV, ni),
        in_specs=[pl.BlockSpec((tm, k), lambda s, i, pidx: (i, 0)),
                  pl.BlockSpec((1, pc), lambda s, i, pidx: (0, pidx[s])), ANY],
        out_specs=[pl.BlockSpec((tm, pc), lambda s, i, pidx: (i, pidx[s])), ANY],
        scratch_shapes=[pltpu.VMEM((2, k, pc), BF16),
                        pltpu.SemaphoreType.DMA((NDEV,)), pltpu.SemaphoreType.DMA((NDEV,)),
                        pltpu.SemaphoreType.DMA((2,)), pltpu.SemaphoreType.DMA((1,))])
    return pl.pallas_call(
        body, name=name, grid_spec=grid_spec,
        out_shape=[jax.ShapeDtypeStruct((m, n), F32), jax.ShapeDtypeStruct((NDEV, k, pc), w_loc.dtype)],
    )(_peer_schedule(_GATHER_MASKS), u, bias, w_loc)


def _adam_math(g, w, m, v):
    m2 = ADAM_B1 * m + (1.0 - ADAM_B1) * g
    v2 = ADAM_B2 * v + (1.0 - ADAM_B2) * (g * g)
    delta = -ADAM_LR * ((m2 * ADAM_C1) / (jnp.sqrt(v2 * ADAM_C2) + ADAM_EPS) + ADAM_WD * w)
    return delta, m2, v2


def _row_tile(r, cap):
    t = min(r, cap)
    while r % t:
        t //= 2
    return t


def _adamw(g, w, m, v, name):
    shape = w.shape
    cols = shape[-1] if w.ndim >= 2 and shape[-1] % 128 == 0 else 128
    g2, w2, m2, v2 = (t.reshape(-1, cols) for t in (g, w, m, v))
    r = g2.shape[0]
    tr = _row_tile(r, 256)

    def body(g_ref, w_ref, m_ref, v_ref, d_ref, mo_ref, vo_ref):
        d, mm, vv = _adam_math(g_ref[...], w_ref[...], m_ref[...], v_ref[...])
        d_ref[...] = d
        mo_ref[...] = mm
        vo_ref[...] = vv

    spec = pl.BlockSpec((tr, cols), lambda i: (i, 0))
    outs = pl.pallas_call(
        body, name=name, grid=(r // tr,),
        out_shape=[jax.ShapeDtypeStruct((r, cols), F32)] * 3,
        in_specs=[spec] * 4, out_specs=[spec] * 3,
    )(g2, w2, m2, v2)
    return tuple(o.reshape(shape) for o in outs)


def _sum_adamw(parts, w, m, v, name):
    _, r, c = parts.shape
    shape = w.shape
    w2, m2, v2 = (t.reshape(r, c) for t in (w, m, v))
    tr = _row_tile(r, 128)

    def body(p_ref, w_ref, m_ref, v_ref, g_ref, d_ref, mo_ref, vo_ref):
        g = p_ref[0].astype(F32)
        for k in range(1, NDEV):
            g = g + p_ref[k].astype(F32)
        d, mm, vv = _adam_math(g, w_ref[...], m_ref[...], v_ref[...])
        g_ref[...] = g
        d_ref[...] = d
        mo_ref[...] = mm
        vo_ref[...] = vv

    spec = pl.BlockSpec((tr, c), lambda i: (i, 0))
    outs = pl.pallas_call(
        body, name=name, grid=(r // tr,),
        out_shape=[jax.ShapeDtypeStruct((r, c), F32)] * 4,
        in_specs=[pl.BlockSpec((NDEV, tr, c), lambda i: (0, i, 0))] + [spec] * 3, out_specs=[spec] * 4,
    )(parts, w2, m2, v2)
    return tuple(o.reshape(shape) for o in outs)


def _sum_rows(parts, name):
    _, r, c = parts.shape

    def body(p_ref, o_ref):
        g = p_ref[0]
        for k in range(1, NDEV):
            g = g + p_ref[k]
        o_ref[...] = g

    return pl.pallas_call(
        body, name=name, out_shape=jax.ShapeDtypeStruct((r, c), F32),
    )(parts)


def _prep(cc, w_mod_full, b_mod, lbl):
    def body(cc_ref, w_ref, b_ref, l_ref, mod_ref, lb_ref):
        t = cc_ref[...]
        s = (t * _sigmoid(t)).astype(BF16)
        mod_ref[...] = _dot(s, w_ref[...]) + b_ref[...]
        lb_ref[...] = _sigmoid(l_ref[0] - l_ref[1])

    return pl.pallas_call(
        body, name="prep",
        out_shape=[jax.ShapeDtypeStruct((8, 3 * D), F32), jax.ShapeDtypeStruct((8, D), F32)],
    )(cc, w_mod_full, b_mod, lbl)


def _modulate(xin, mod, row, name):
    n = xin.shape[0]
    tm = _row_tile(n, 512)

    def body(x_ref, mod_ref, u_ref):
        sh = mod_ref[row:row + 1, 0:D]
        sc = mod_ref[row:row + 1, D:2 * D]
        u_ref[...] = (x_ref[...] * (1.0 + sc) + sh).astype(BF16)

    return pl.pallas_call(
        body, name=name, grid=(n // tm,),
        out_shape=jax.ShapeDtypeStruct((n, D), BF16),
        in_specs=[pl.BlockSpec((tm, D), lambda i: (i, 0)), pl.BlockSpec((8, 3 * D), lambda i: (0, 0))],
        out_specs=pl.BlockSpec((tm, D), lambda i: (i, 0)),
    )(xin, mod)


def _mm_bias(a, w_all, bias, name):
    m, k = a.shape
    tn = w_all.shape[2]
    n = tn * NDEV
    tm = _row_tile(m, 512)

    def body(a_ref, b_ref, bias_ref, o_ref):
        o_ref[...] = _dot(a_ref[...], b_ref[0]) + bias_ref[...]

    return pl.pallas_call(
        body, name=name, grid=(NDEV, m // tm),
        out_shape=jax.ShapeDtypeStruct((m, n), F32),
        in_specs=[pl.BlockSpec((tm, k), lambda j, i: (i, 0)), pl.BlockSpec((1, k, tn), lambda j, i: (j, 0, 0)),
                  pl.BlockSpec((1, tn), lambda j, i: (0, j))],
        out_specs=pl.BlockSpec((tm, tn), lambda j, i: (i, j)),
    )(a, w_all, bias)


def _mm_tn(a, b, init, name, with_colsum=False, colsum_init=None, out_dtype=F32):
    m, ka = a.shape
    n = b.shape[1]
    tk = _row_tile(m, 512)
    tn = 1024
    nk = m // tk
    has_init = init is not None

    def body(*refs):
        a_ref, b_ref = refs[0], refs[1]
        pos = 2
        init_ref = cs_init_ref = None
        if has_init:
            init_ref = refs[pos]
            pos += 1
            if with_colsum:
                cs_init_ref = refs[pos]
                pos += 1
        o_ref = refs[pos]
        cs_ref = refs[pos + 1] if with_colsum else None
        acc = refs[-1]
        k = pl.program_id(1)

        @pl.when(k == 0)
        def _():
            if has_init:
                acc[...] = init_ref[...]
                if with_colsum:
                    cs_ref[...] = cs_init_ref[...]
            else:
                acc[...] = jnp.zeros_like(acc)
                if with_colsum:
                    cs_ref[...] = jnp.zeros_like(cs_ref)

        bv = b_ref[...]
        acc[...] += _dot_tn(a_ref[...], bv)
        if with_colsum:
            cs_ref[...] += jnp.sum(bv.astype(F32), axis=0, keepdims=True)

        @pl.when(k == nk - 1)
        def _():
            o_ref[...] = acc[...].astype(out_dtype)

    in_specs = [pl.BlockSpec((tk, ka), lambda j, k: (k, 0)), pl.BlockSpec((tk, tn), lambda j, k: (k, j))]
    args = [a, b]
    if has_init:
        in_specs.append(pl.BlockSpec((ka, tn), lambda j, k: (0, j)))
        args.append(init)
        if with_colsum:
            in_specs.append(pl.BlockSpec((1, tn), lambda j, k: (0, j)))
            args.append(colsum_init)
    out_shape = [jax.ShapeDtypeStruct((ka, n), out_dtype)]
    out_specs = [pl.BlockSpec((ka, tn), lambda j, k: (0, j))]
    if with_colsum:
        out_shape.append(jax.ShapeDtypeStruct((1, n), F32))
        out_specs.append(pl.BlockSpec((1, tn), lambda j, k: (0, j)))
    outs = pl.pallas_call(
        body, name=name, grid=(n // tn, nk), out_shape=out_shape, in_specs=in_specs, out_specs=out_specs,
        scratch_shapes=[pltpu.VMEM((ka, tn), F32)],
    )(*args)
    return outs if with_colsum else outs[0]


def _input_grad(dz, w_all, xin, dr, mod, row, name, side=(), side_splits=()):
    m, n = dz.shape
    tm = _row_tile(m, 512)
    tk = w_all.shape[2]
    nk = NDEV
    ni = m // tm
    has_dr = dr is not None
    ns = len(side)
    piece_shapes, piece_of = _pieces(side, side_splits)

    def body(*refs):
        dz_ref, w_ref, x_ref = refs[:3]
        pos = 3
        dr_ref = refs[pos] if has_dr else None
        pos += int(has_dr)
        mod_ref = refs[pos]
        side_in = refs[pos + 1:pos + 1 + ns]
        pos += 1 + ns
        gx_ref = refs[pos] if has_dr else None
        pos += int(has_dr)
        vec_ref = refs[pos]
        side_out = refs[pos + 1:pos + 1 + ns]
        acc = refs[pos + 1 + ns]
        i, k = pl.program_id(0), pl.program_id(1)
        if ns:
            side_start, side_finish = _exchange(side_in, side_out, refs[pos + 2 + ns:], piece_of)

            @pl.when((i == 0) & (k == 0))
            def _():
                side_start()

        @pl.when(k == 0)
        def _():
            acc[...] = jnp.zeros_like(acc)

        @pl.when((i == 0) & (k == 0))
        def _():
            vec_ref[...] = jnp.zeros_like(vec_ref)

        acc[...] += _dot_nt(dz_ref[...], w_ref[0])

        @pl.when(k == nk - 1)
        def _():
            du = acc[...]
            xv = x_ref[...]
            if has_dr:
                sc = mod_ref[row:row + 1, D:2 * D]
                gx_ref[...] = ALPHA * dr_ref[...] + du * (1.0 + sc)
            vec_ref[0:1, :] += jnp.sum(du, axis=0, keepdims=True)
            vec_ref[1:2, :] += jnp.sum(du * xv, axis=0, keepdims=True)

        if ns:
            @pl.when((i == ni - 1) & (k == nk - 1))
            def _():
                side_finish()

    row_spec = pl.BlockSpec((tm, D), lambda i, k: (i, 0))
    in_specs = [pl.BlockSpec((tm, tk), lambda i, k: (i, k)), pl.BlockSpec((1, D, tk), lambda i, k: (k, 0, 0)), row_spec]
    args = [dz, w_all, xin]
    if has_dr:
        in_specs.append(row_spec)
        args.append(dr)
    in_specs.append(pl.BlockSpec((8, 3 * D), lambda i, k: (0, 0)))
    args.append(mod)
    in_specs += [ANY] * ns
    args += list(side)
    out_shape, out_specs = [], []
    if has_dr:
        out_shape.append(jax.ShapeDtypeStruct((m, D), F32))
        out_specs.append(row_spec)
    out_shape.append(jax.ShapeDtypeStruct((8, D), F32))
    out_specs.append(pl.BlockSpec((8, D), lambda i, k: (0, 0)))
    out_shape += [jax.ShapeDtypeStruct((NDEV,) + piece_shapes[a], side[a].dtype) for a in range(ns)]
    out_specs += [ANY] * ns
    outs = pl.pallas_call(
        body, name=name, grid=(ni, nk), out_shape=out_shape, in_specs=in_specs, out_specs=out_specs,
        scratch_shapes=[pltpu.VMEM((tm, D), F32)] + (_exchange_sems(ns) if ns else []),
    )(*args)
    return tuple(outs) if has_dr else (None, *outs)


def _tri(reverse):
    r = lax.broadcasted_iota(jnp.int32, (CHUNK, CHUNK), 0)
    c = lax.broadcasted_iota(jnp.int32, (CHUNK, CHUNK), 1)
    return (c >= r) if reverse else (c <= r)


def _cum_f32(tri_b, t):
    hi = t.astype(BF16)
    r1 = t - hi.astype(F32)
    mid = r1.astype(BF16)
    lo = (r1 - mid.astype(F32)).astype(BF16)
    return _dot(tri_b, hi) + _dot(tri_b, mid) + _dot(tri_b, lo)


def _gla_features(zq, zf, lb):
    sq = _sigmoid(zq)
    q = zq * sq * Q_SCALE
    sf = _sigmoid(zf)
    f = lb + (1.0 - lb) * sf
    return q, sq, f, sf


def _gla_block(n):
    return 256 if n % 256 == 0 else CHUNK


def _gla_fwd(z, lb, s0, d, name, side=()):
    n = z.shape[0]
    blk = _gla_block(n)
    nb, npb = n // blk, blk // CHUNK
    reverse = d == 1
    last = 0 if reverse else CHUNK - 1
    order = list(range(npb))[::-1] if reverse else list(range(npb))
    ns = len(side)

    def bmap(i):
        return nb - 1 - i if reverse else i

    hp = GLA_HEADS_PER_STEP
    hw = hp * DH
    units = [(hh, cidx) for hh in range(hp) for cidx in order]

    def body(zq_ref, zf_ref, zv_ref, lb_ref, s0_ref, *rest):
        side_in = rest[:ns]
        o_ref, ss_ref, sf_ref = rest[ns:ns + 3]
        side_out = rest[ns + 3:2 * ns + 3]
        st = rest[2 * ns + 3]
        i = pl.program_id(1)
        if ns:
            side_start, side_finish = _exchange(side_in, side_out, rest[2 * ns + 4:])

            @pl.when((pl.program_id(0) == 0) & (i == 0))
            def _():
                side_start()

        @pl.when(i == 0)
        def _():
            st[...] = s0_ref[...]

        mask = _tri(reverse)
        tri_b = jnp.where(mask, 1.0, 0.0).astype(BF16)
        feat = {}
        for u in units:
            hh, cidx = u
            rows, cols = pl.ds(cidx * CHUNK, CHUNK), pl.ds(hh * DH, DH)
            q, _, f, _ = _gla_features(zq_ref[rows, cols], zf_ref[rows, cols], lb_ref[d:d + 1, cols])
            feat[u] = (q, 1.0 - f, jnp.log(f), zv_ref[rows, cols].astype(BF16))
        dec = {u: _cum_f32(tri_b, feat[u][2]) for u in units}
        ops = {}
        for u in units:
            q, k, _, vb = feat[u]
            g = dec[u]
            gl = g[last:last + 1, :]
            ops[u] = ((q * jnp.exp(g)).astype(BF16), (k * jnp.exp(-g)).astype(BF16),
                      (k * jnp.exp(gl - g)).astype(BF16), jnp.exp(gl), vb)
        att = {u: jnp.where(mask, _dot_nt(ops[u][0], ops[u][1]), 0.0).astype(BF16) for u in units}
        upd = {u: _dot_tn(ops[u][4], ops[u][2]) for u in units}
        intra = {u: _dot(att[u], ops[u][4]) for u in units}
        s_in = {}
        for hh in range(hp):
            s = st[hh]
            for cidx in order:
                s_in[(hh, cidx)] = s
                s = s * ops[(hh, cidx)][3] + upd[(hh, cidx)]
            st[hh] = s
            sf_ref[hh] = s
        for u in units:
            hh, cidx = u
            rows, cols = pl.ds(cidx * CHUNK, CHUNK), pl.ds(hh * DH, DH)
            o_ref[rows, cols] = intra[u] + _dot_nt(ops[u][0], s_in[u].astype(BF16))
            ss_ref[hh, cidx] = s_in[u]

        if ns:
            @pl.when((pl.program_id(0) == NH // hp - 1) & (i == nb - 1))
            def _():
                side_finish()

    def col(g):
        return lambda h, i: (bmap(i), g * (NH // hp) + h)

    return pl.pallas_call(
        body, name=name, grid=(NH // hp, nb),
        out_shape=[jax.ShapeDtypeStruct((n, D), F32), jax.ShapeDtypeStruct((NH, n // CHUNK, DH, DH), F32),
                   jax.ShapeDtypeStruct((NH, DH, DH), F32)]
        + [jax.ShapeDtypeStruct((NDEV,) + t.shape, t.dtype) for t in side],
        in_specs=[pl.BlockSpec((blk, hw), col(0)), pl.BlockSpec((blk, hw), col(1 + d)),
                  pl.BlockSpec((blk, hw), col(3)), pl.BlockSpec((8, hw), lambda h, i: (0, h)),
                  pl.BlockSpec((hp, DH, DH), lambda h, i: (h, 0, 0))] + [ANY] * ns,
        out_specs=[pl.BlockSpec((blk, hw), lambda h, i: (bmap(i), h)),
                   pl.BlockSpec((hp, npb, DH, DH), lambda h, i: (h, bmap(i), 0, 0)),
                   pl.BlockSpec((hp, DH, DH), lambda h, i: (h, 0, 0))] + [ANY] * ns,
        scratch_shapes=[pltpu.VMEM((hp, DH, DH), F32)] + (_exchange_sems(ns) if ns else []),
    )(z, z, z, lb, s0, *side)


def _gla_bwd(z, lb, s_start, do, ds_fin, acc_q, acc_v, d, name, f_dtype=F32, into=None, side=(), side_splits=()):
    n = z.shape[0]
    blk = _gla_block(n)
    nb, npb = n // blk, blk // CHUNK
    reverse = d == 1
    last = 0 if reverse else CHUNK - 1
    order = list(range(npb)) if reverse else list(range(npb))[::-1]
    has_do = do is not None
    has_acc = acc_q is not None
    fused = into is not None
    assert not fused or d == 1
    ns = len(side)
    assert not (fused and ns)
    piece_shapes, piece_of = _pieces(side, side_splits)
    hp = NH if fused else GLA_HEADS_PER_STEP
    hw = hp * DH
    units = [(hh, cidx) for hh in range(hp) for cidx in order]

    def bmap(i):
        return i if reverse else nb - 1 - i

    def body(*refs):
        zq_ref, zf_ref, zv_ref, lb_ref, ss_ref, dsf_ref = refs[:6]
        pos = 6
        do_ref = aq_ref = av_ref = None
        if has_do:
            do_ref = refs[pos]
            pos += 1
        if has_acc:
            aq_ref, av_ref = refs[pos], refs[pos + 1]
            pos += 2
        if fused:
            other_ref = refs[pos + 1]
            dz_ref, dlb_ref, ds0_ref, dst = refs[pos + 2:]
            dz_ref[:, D:2 * D] = other_ref[...]
        else:
            side_in = refs[pos:pos + ns]
            dzq_ref, dzf_ref, dzv_ref, dlb_ref, ds0_ref = refs[pos + ns:pos + ns + 5]
            side_out = refs[pos + ns + 5:pos + 2 * ns + 5]
            dst = refs[pos + 2 * ns + 5]
        i = pl.program_id(1)
        if ns:
            side_start, side_finish = _exchange(side_in, side_out, refs[pos + 2 * ns + 6:], piece_of)

            @pl.when((pl.program_id(0) == 0) & (i == 0))
            def _():
                side_start()

        @pl.when(i == 0)
        def _():
            dst[...] = dsf_ref[...]
            dlb_ref[...] = jnp.zeros_like(dlb_ref)

        mask = _tri(reverse)
        tri_b = jnp.where(mask, 1.0, 0.0).astype(BF16)
        tri_t = jnp.where(_tri(not reverse), 1.0, 0.0).astype(BF16)

        def where(u):
            return pl.ds(u[1] * CHUNK, CHUNK), pl.ds(u[0] * DH, DH)

        feat = {}
        for u in units:
            rows, cols = where(u)
            zq, zf = zq_ref[rows, cols], zf_ref[rows, cols]
            lbv = lb_ref[d:d + 1, cols]
            q, sq, f, sf = _gla_features(zq, zf, lbv)
            feat[u] = dict(zq=zq, q=q, sq=sq, f=f, sf=sf, lbv=lbv, k=1.0 - f, vb=zv_ref[rows, cols].astype(BF16))
        dec = {u: _cum_f32(tri_b, jnp.log(feat[u]["f"])) for u in units}
        for u in units:
            w = feat[u]
            g = dec[u]
            gl = g[last:last + 1, :]
            w["eg"], w["egi"], w["ege"], w["egl"] = jnp.exp(g), jnp.exp(-g), jnp.exp(gl - g), jnp.exp(gl)
            w["qd"], w["ki"], w["ke"] = w["q"] * w["eg"], w["k"] * w["egi"], w["k"] * w["ege"]
            w["qdb"], w["kib"], w["keb"] = w["qd"].astype(BF16), w["ki"].astype(BF16), w["ke"].astype(BF16)
            w["s_in"] = ss_ref[u[0], u[1]]
        if has_do:
            for u in units:
                w = feat[u]
                rows, cols = where(u)
                w["dob"] = do_ref[rows, cols].astype(BF16)
            for u in units:
                w = feat[u]
                w["a"] = jnp.where(mask, _dot_nt(w["qdb"], w["kib"]), 0.0).astype(BF16)
                w["da"] = jnp.where(mask, _dot_nt(w["dob"], w["vb"]), 0.0).astype(BF16)
                w["m"] = _dot_tn(w["dob"], w["qdb"])
        for hh in range(hp):
            ds = dst[hh]
            for cidx in order:
                w = feat[(hh, cidx)]
                w["ds"] = ds
                ds = ds * w["egl"]
                if has_do:
                    ds = ds + w["m"]
            dst[hh] = ds
            ds0_ref[hh] = ds
        for u in units:
            w = feat[u]
            dsb = w["ds"].astype(BF16)
            w["dke"] = _dot(w["vb"], dsb)
            w["dv"] = _dot_nt(w["keb"], dsb)
            if has_do:
                w["dv"] = w["dv"] + _dot_tn(w["a"], w["dob"])
                w["dqd"] = _dot(w["da"], w["kib"]) + _dot(w["dob"], w["s_in"].astype(BF16))
                w["dki"] = _dot_tn(w["da"], w["qdb"])
        for u in units:
            w = feat[u]
            dkeke = w["dke"] * w["ke"]
            w["dgl"] = (w["egl"] * jnp.sum(w["s_in"] * w["ds"], axis=0, keepdims=True)
                        + jnp.sum(dkeke, axis=0, keepdims=True))
            dg = -dkeke
            dk = w["dke"] * w["ege"]
            if has_do:
                dg = dg + w["dqd"] * w["qd"] - w["dki"] * w["ki"]
                dk = dk + w["dki"] * w["egi"]
            w["dg"], w["dk"] = dg, dk
        dlf = {u: _cum_f32(tri_t, feat[u]["dg"]) for u in units}
        for u in units:
            w = feat[u]
            rows, cols = where(u)
            df = (dlf[u] + w["dgl"]) / w["f"] - w["dk"]
            sf = w["sf"]
            dzf = df * (1.0 - w["lbv"]) * sf * (1.0 - sf)
            dlb_ref[0:1, cols] += jnp.sum(df * (1.0 - sf), axis=0, keepdims=True)
            if has_do:
                dzq = w["dqd"] * w["eg"] * (Q_SCALE * _dsilu(w["zq"], w["sq"]))
            else:
                dzq = jnp.zeros((CHUNK, DH), F32)
            dv = w["dv"]
            if has_acc:
                dzq = dzq + aq_ref[rows, cols]
                dv = dv + av_ref[rows, cols]
            if fused:
                lane = u[0] * DH
                dz_ref[rows, pl.ds(lane, DH)] = dzq.astype(BF16)
                dz_ref[rows, pl.ds(2 * D + lane, DH)] = dzf.astype(BF16)
                dz_ref[rows, pl.ds(3 * D + lane, DH)] = dv.astype(BF16)
            else:
                dzq_ref[rows, cols] = dzq
                dzf_ref[rows, cols] = dzf.astype(f_dtype)
                dzv_ref[rows, cols] = dv

        if ns:
            @pl.when((pl.program_id(0) == NH // hp - 1) & (i == nb - 1))
            def _():
                side_finish()

    def col(g):
        return lambda h, i: (bmap(i), g * (NH // hp) + h)

    tok = pl.BlockSpec((blk, hw), lambda h, i: (bmap(i), h))
    state = pl.BlockSpec((hp, DH, DH), lambda h, i: (h, 0, 0))
    in_specs = [pl.BlockSpec((blk, hw), col(0)), pl.BlockSpec((blk, hw), col(1 + d)), pl.BlockSpec((blk, hw), col(3)),
                pl.BlockSpec((8, hw), lambda h, i: (0, h)),
                pl.BlockSpec((hp, npb, DH, DH), lambda h, i: (h, bmap(i), 0, 0)), state]
    args = [z, z, z, lb, s_start, ds_fin]
    if has_do:
        in_specs.append(tok)
        args.append(do)
    if has_acc:
        in_specs += [tok, tok]
        args += [acc_q, acc_v]
    tail_shape = [jax.ShapeDtypeStruct((8, D), F32), jax.ShapeDtypeStruct((NH, DH, DH), F32)]
    tail_specs = [pl.BlockSpec((8, hw), lambda h, i: (0, h)), state]
    if fused:
        buf, other = into
        aliases = {len(args): 0}
        in_specs += [ANY, tok]
        args += [buf, other]
        out_shape = [jax.ShapeDtypeStruct(buf.shape, buf.dtype)] + tail_shape
        out_specs = [pl.BlockSpec((blk, 4 * D), lambda h, i: (bmap(i), 0))] + tail_specs
    else:
        aliases = {}
        in_specs += [ANY] * ns
        args += list(side)
        out_shape = [jax.ShapeDtypeStruct((n, D), F32), jax.ShapeDtypeStruct((n, D), f_dtype),
                     jax.ShapeDtypeStruct((n, D), F32)] + tail_shape
        out_shape += [jax.ShapeDtypeStruct((NDEV,) + piece_shapes[a], side[a].dtype) for a in range(ns)]
        out_specs = [tok, tok, tok] + tail_specs + [ANY] * ns
    return pl.pallas_call(
        body, name=name, grid=(NH // hp, nb), out_shape=out_shape, in_specs=in_specs, out_specs=out_specs,
        input_output_aliases=aliases,
        scratch_shapes=[pltpu.VMEM((hp, DH, DH), F32)] + (_exchange_sems(ns) if ns else []),
    )(*args)


def _shift(t, s, fill, down):
    n = t.shape[0]
    rows = lax.broadcasted_iota(jnp.int32, t.shape, 0)
    if down:
        return jnp.where(rows >= s, pltpu.roll(t, s, 0), fill)
    return jnp.where(rows < n - s, pltpu.roll(t, n - s, 0), fill)


SUBLANES = 8
LRU_SAVED = 4


def _chain_scan(a, b, h_in, down):
    n = a.shape[0]
    ng = n // SUBLANES
    rows = lax.broadcasted_iota(jnp.int32, (SUBLANES, a.shape[1]), 0)
    local = []
    for g in range(ng):
        aa, bb = a[g * SUBLANES:(g + 1) * SUBLANES], b[g * SUBLANES:(g + 1) * SUBLANES]
        for s in (1, 2, 4):
            if down:
                keep, amt = rows >= s, s
            else:
                keep, amt = rows < SUBLANES - s, SUBLANES - s
            bb = bb + aa * jnp.where(keep, pltpu.roll(bb, amt, 0), 0.0)
            aa = aa * jnp.where(keep, pltpu.roll(aa, amt, 0), 1.0)
        local.append((aa, bb))
    out = [None] * ng
    carry = h_in
    for g in (range(ng) if down else range(ng - 1, -1, -1)):
        aa, bb = local[g]
        hg = bb + aa * carry
        out[g] = hg
        carry = hg[SUBLANES - 1:SUBLANES] if down else hg[0:1]
    return (jnp.concatenate(out, axis=0) if ng > 1 else out[0]), carry


def _conv_taps(xv):
    return (_shift(xv, 1, 0.0, True), xv, _shift(xv, 1, 0.0, False), _shift(xv, 2, 0.0, False))


def _conv(taps, cw, cb):
    return cb + cw[0:1, :] * taps[0] + cw[1:2, :] * taps[1] + cw[2:3, :] * taps[2] + cw[3:4, :] * taps[3]


def _neg_expm1(t):
    series = -t * (1.0 + t * (0.5 + t * (1.0 / 6.0 + t * (1.0 / 24.0 + t * (1.0 / 120.0)))))
    return jnp.where(t > -0.1, series, 1.0 - jnp.exp(t))


def _lru_gates(xc, wr, br, wi, bi, lam):
    xcb = xc.astype(BF16)
    r = _sigmoid(_dot(xcb, wr) + br)
    gi = _sigmoid(_dot(xcb, wi) + bi)
    sp = jnp.maximum(-lam, 0.0) + jnp.log(1.0 + jnp.exp(-jnp.abs(lam)))
    la = -RG_C * r * sp
    a = jnp.exp(la)
    mult = jnp.sqrt(_neg_expm1(2.0 * la))
    return xcb, r, gi, sp, a, mult


def _lru_fwd(xin, blk, cw, cb, wr, br, wi, bi, lam, h0, acc_h, d, name):
    n = xin.shape[0]
    nb = n // blk
    reverse = d == 1
    down = not reverse
    has_acc = acc_h is not None

    def bmap(i):
        return nb - 1 - i if reverse else i

    def body(*refs):
        x_ref, cw_ref, cb_ref, wr_ref, br_ref, wi_ref, bi_ref, lam_ref, h0_ref = refs[:9]
        pos = 9
        acc_ref = refs[pos] if has_acc else None
        pos += int(has_acc)
        h_ref, hin_ref, hfin_ref, sav_a_ref, sav_ref = refs[pos:pos + 5]
        pos += 5
        hsum_ref = refs[pos] if has_acc else None
        carry = refs[-1]
        i = pl.program_id(0)

        @pl.when(i == 0)
        def _():
            carry[...] = h0_ref[...]

        for g in range(NH):
            cols = pl.ds(g * DH, DH)
            xc = _conv(_conv_taps(x_ref[:, cols]), cw_ref[:, cols], cb_ref[:, cols])
            _, r, gi, _, a, mult = _lru_gates(xc, wr_ref[g], br_ref[:, cols], wi_ref[g], bi_ref[:, cols],
                                              lam_ref[:, cols])
            sav_a_ref[:, cols] = a
            for slot, val in enumerate((xc, r, gi, mult)):
                sav_ref[slot, :, cols] = val.astype(BF16)
            hin = carry[:, cols]
            h, h_last = _chain_scan(a, mult * gi * xc, hin, down)
            h_ref[:, cols] = h
            if has_acc:
                hsum_ref[:, cols] = h + acc_ref[:, cols]
            hin_ref[0, :, cols] = hin
            carry[:, cols] = h_last
            hfin_ref[:, cols] = h_last

    vec = pl.BlockSpec((1, D), lambda i: (0, 0))
    wsp = pl.BlockSpec((NH, DH, DH), lambda i: (0, 0, 0))
    tok = pl.BlockSpec((blk, D), lambda i: (bmap(i), 0))
    in_specs = [tok, pl.BlockSpec((4, D), lambda i: (0, 0)), vec, wsp, vec, wsp, vec, vec, vec]
    args = [xin, cw, cb, wr, br, wi, bi, lam, h0]
    out_shape = [jax.ShapeDtypeStruct((n, D), F32), jax.ShapeDtypeStruct((nb, 1, D), F32),
                 jax.ShapeDtypeStruct((1, D), F32), jax.ShapeDtypeStruct((n, D), F32),
                 jax.ShapeDtypeStruct((LRU_SAVED, n, D), BF16)]
    out_specs = [tok, pl.BlockSpec((1, 1, D), lambda i: (bmap(i), 0, 0)), vec, tok,
                 pl.BlockSpec((LRU_SAVED, blk, D), lambda i: (0, bmap(i), 0))]
    if has_acc:
        in_specs.append(tok)
        args.append(acc_h)
        out_shape.append(jax.ShapeDtypeStruct((n, D), F32))
        out_specs.append(tok)
    return pl.pallas_call(
        body, name=name, grid=(nb,), out_shape=out_shape, in_specs=in_specs, out_specs=out_specs,
        scratch_shapes=[pltpu.VMEM((1, D), F32)],
    )(*args)


def _lru_bwd(xin, blk, cw, wr, wi, lam, sav, h, hin, dh, cg_fin, acc_dx, init, d, name):
    n = xin.shape[0]
    nb = n // blk
    reverse = d == 1
    down = not reverse
    first = blk - 1 if reverse else 0
    has_dh = dh is not None
    has_acc = acc_dx is not None
    has_init = init is not None

    def bmap(i):
        return i if reverse else nb - 1 - i

    def body(*refs):
        (x_ref, cw_ref, wr_ref, wi_ref, lam_ref, sav_a_ref, sav_ref, h_ref, hin_ref, cgf_ref) = refs[:10]
        pos = 10
        dh_ref = acc_ref = None
        iwr_ref = iwi_ref = ivec_ref = None
        if has_dh:
            dh_ref = refs[pos]
            pos += 1
        if has_acc:
            acc_ref = refs[pos]
            pos += 1
        if has_init:
            iwr_ref, iwi_ref, ivec_ref = refs[pos:pos + 3]
            pos += 3
        dx_ref, dwr_ref, dwi_ref, vec_ref, cg0_ref, carry = refs[pos:]
        i = pl.program_id(0)

        @pl.when(i == 0)
        def _():
            carry[...] = cgf_ref[...]
            if has_init:
                dwr_ref[...] = iwr_ref[...]
                dwi_ref[...] = iwi_ref[...]
                vec_ref[...] = ivec_ref[...]
            else:
                dwr_ref[...] = jnp.zeros_like(dwr_ref)
                dwi_ref[...] = jnp.zeros_like(dwi_ref)
                vec_ref[...] = jnp.zeros_like(vec_ref)

        for g in range(NH):
            cols = pl.ds(g * DH, DH)
            cwv = cw_ref[:, cols]
            lam_v = lam_ref[:, cols]
            taps = _conv_taps(x_ref[:, cols])
            wr_g, wi_g = wr_ref[g], wi_ref[g]
            a = sav_a_ref[:, cols]
            xcb = sav_ref[0, :, cols]
            xc, r, gi, mult = (sav_ref[slot, :, cols].astype(F32) for slot in range(LRU_SAVED))
            sp = jnp.maximum(-lam_v, 0.0) + jnp.log(1.0 + jnp.exp(-jnp.abs(lam_v)))
            hprev = _shift(h_ref[:, cols], 1, hin_ref[0, :, cols], down)
            a_next = _shift(a, 1, 1.0, not down)
            dhv = dh_ref[:, cols] if has_dh else jnp.zeros_like(a)
            e, _ = _chain_scan(a_next, dhv, carry[:, cols], not down)
            cg = a[first:first + 1, :] * e[first:first + 1, :]
            carry[:, cols] = cg
            cg0_ref[:, cols] = cg
            da = e * hprev
            emult = e * mult
            dgi = emult * xc
            dxc = emult * gi
            dla = da * a - (e * gi * xc) * (a * a) / mult
            dr = dla * (-RG_C * sp)
            sneg = 1.0 - _sigmoid(lam_v)
            dpr = dr * r * (1.0 - r)
            dpi = dgi * gi * (1.0 - gi)
            dprb, dpib = dpr.astype(BF16), dpi.astype(BF16)
            dxc = dxc + _dot_nt(dprb, wr_g) + _dot_nt(dpib, wi_g)
            dwr_ref[g] += _dot_tn(xcb, dprb)
            dwi_ref[g] += _dot_tn(xcb, dpib)
            dx = (cwv[0:1, :] * _shift(dxc, 1, 0.0, False) + cwv[1:2, :] * dxc
                  + cwv[2:3, :] * _shift(dxc, 1, 0.0, True) + cwv[3:4, :] * _shift(dxc, 2, 0.0, True))
            if has_acc:
                dx = dx + acc_ref[:, cols]
            dx_ref[:, cols] = dx
            vec_ref[0:1, cols] += jnp.sum(dpr, axis=0, keepdims=True)
            vec_ref[1:2, cols] += jnp.sum(dpi, axis=0, keepdims=True)
            vec_ref[2:3, cols] += jnp.sum(dla * r, axis=0, keepdims=True) * (RG_C * sneg)
            vec_ref[3:4, cols] += jnp.sum(dxc, axis=0, keepdims=True)
            for kk in range(4):
                vec_ref[4 + kk:5 + kk, cols] += jnp.sum(dxc * taps[kk], axis=0, keepdims=True)

    vec = pl.BlockSpec((1, D), lambda i: (0, 0))
    wsp = pl.BlockSpec((NH, DH, DH), lambda i: (0, 0, 0))
    tok = pl.BlockSpec((blk, D), lambda i: (bmap(i), 0))
    vec16 = pl.BlockSpec((16, D), lambda i: (0, 0))
    in_specs = [tok, pl.BlockSpec((4, D), lambda i: (0, 0)), wsp, wsp, vec, tok,
                pl.BlockSpec((LRU_SAVED, blk, D), lambda i: (0, bmap(i), 0)), tok,
                pl.BlockSpec((1, 1, D), lambda i: (bmap(i), 0, 0)), vec]
    args = [xin, cw, wr, wi, lam, sav[0], sav[1], h, hin, cg_fin]
    if has_dh:
        in_specs.append(tok)
        args.append(dh)
    if has_acc:
        in_specs.append(tok)
        args.append(acc_dx)
    if has_init:
        in_specs += [wsp, wsp, vec16]
        args += list(init)
    return pl.pallas_call(
        body, name=name, grid=(nb,),
        out_shape=[jax.ShapeDtypeStruct((n, D), F32), jax.ShapeDtypeStruct((NH, DH, DH), F32),
                   jax.ShapeDtypeStruct((NH, DH, DH), F32), jax.ShapeDtypeStruct((16, D), F32),
                   jax.ShapeDtypeStruct((1, D), F32)],
        in_specs=in_specs, out_specs=[tok, wsp, wsp, vec16, vec],
        scratch_shapes=[pltpu.VMEM((1, D), F32)],
    )(*args)


def _merge(z, o_f, o_b, hx, xin, tgt, mod, gn, p_a, p_b, w_out, ln_g, ln_b):
    n = xin.shape[0]
    tm = _row_tile(n, 128)

    def body(z4_ref, z6_ref, z7_ref, z8_ref, of_ref, ob_ref, hx_ref, x_ref, t_ref, mod_ref, gn_ref,
             pa_ref, pb_ref, wo_ref, lg_ref, lnb_ref,
             dr_ref, do_ref, dhx_ref, dz_ref, oa_o, obb_o, y_o, dya_o, dyb_o, dout_o, vec_ref):
        @pl.when(pl.program_id(0) == 0)
        def _():
            vec_ref[...] = jnp.zeros_like(vec_ref)

        gt = mod_ref[0:1, 2 * D:3 * D]
        gnv = gn_ref[...]
        o = of_ref[...] + ob_ref[...]
        rs = jnp.concatenate(
            [jnp.broadcast_to(lax.rsqrt(jnp.mean(jnp.square(o[:, h * DH:(h + 1) * DH]), axis=1, keepdims=True)
                                        + RMS_EPS), (tm, DH)) for h in range(NH)], axis=1)
        nrm = o * rs
        rn = nrm * gnv
        z4, z6, z7, z8 = z4_ref[...], z6_ref[...], z7_ref[...], z8_ref[...]
        s4, s6, s7, s8 = _sigmoid(z4), _sigmoid(z6), _sigmoid(z7), _sigmoid(z8)
        sg4, sg6 = z4 * s4, z6 * s6
        hxv = hx_ref[...]
        oa = (rn * sg4).astype(BF16)
        obb = (hxv * sg6).astype(BF16)
        ya = _dot(oa, pa_ref[...])
        yb = _dot(obb, pb_ref[...])
        y = (s7 * ya + s8 * yb).astype(BF16)
        out = _dot(y, wo_ref[...])
        xv = x_ref[...]
        rr = ALPHA * xv + gt * out
        mu = jnp.mean(rr, axis=1, keepdims=True)
        cen = rr - mu
        rstd = lax.rsqrt(jnp.mean(cen * cen, axis=1, keepdims=True) + LN_EPS)
        xhat = cen * rstd
        lg = lg_ref[...]
        err = xhat * lg + lnb_ref[...] - t_ref[...]
        loss_rows = jnp.sum(err * err, axis=1, keepdims=True)
        dxn = err * (1.0 / D)
        dxh = dxn * lg
        dr = rstd * (dxh - jnp.mean(dxh, axis=1, keepdims=True)
                     - xhat * jnp.mean(dxh * xhat, axis=1, keepdims=True))
        dout = (dr * gt).astype(BF16)
        dy = _dot_nt(dout, wo_ref[...])
        dya = (dy * s7).astype(BF16)
        dyb = (dy * s8).astype(BF16)
        doa = _dot_nt(dya, pa_ref[...])
        dob = _dot_nt(dyb, pb_ref[...])
        drn = doa * sg4
        dn = drn * gnv
        dnn = dn * nrm
        corr = jnp.concatenate(
            [jnp.broadcast_to(jnp.mean(dnn[:, h * DH:(h + 1) * DH], axis=1, keepdims=True), (tm, DH))
             for h in range(NH)], axis=1)
        dr_ref[...] = dr
        do_ref[...] = rs * (dn - nrm * corr)
        dhx_ref[...] = dob * sg6
        dz_ref[:, 0:4 * D] = jnp.zeros((tm, 4 * D), BF16)
        dz_ref[:, 4 * D:5 * D] = (doa * rn * _dsilu(z4, s4)).astype(BF16)
        dz_ref[:, 5 * D:6 * D] = jnp.zeros((tm, D), BF16)
        dz_ref[:, 6 * D:7 * D] = (dob * hxv * _dsilu(z6, s6)).astype(BF16)
        dz_ref[:, 7 * D:8 * D] = (dy * ya * s7 * (1.0 - s7)).astype(BF16)
        dz_ref[:, 8 * D:9 * D] = (dy * yb * s8 * (1.0 - s8)).astype(BF16)
        oa_o[...] = oa
        obb_o[...] = obb
        y_o[...] = y
        dya_o[...] = dya
        dyb_o[...] = dyb
        dout_o[...] = dout
        vec_ref[0:1, :] += jnp.sum(dr * out, axis=0, keepdims=True)
        vec_ref[1:2, :] += jnp.sum(dxn * xhat, axis=0, keepdims=True)
        vec_ref[2:3, :] += jnp.sum(dxn, axis=0, keepdims=True)
        vec_ref[3:4, :] += jnp.sum(drn * nrm, axis=0, keepdims=True)
        vec_ref[4:5, :] += jnp.broadcast_to(jnp.sum(loss_rows, axis=0, keepdims=True) * (0.5 / D), (1, D))

    def grp(g):
        return pl.BlockSpec((tm, D), lambda i: (i, g))

    tok = pl.BlockSpec((tm, D), lambda i: (i, 0))
    vec = pl.BlockSpec((1, D), lambda i: (0, 0))
    wsp = pl.BlockSpec((D, D), lambda i: (0, 0))
    return pl.pallas_call(
        body, name="merge", grid=(n // tm,),
        out_shape=[jax.ShapeDtypeStruct((n, D), F32)] * 3
        + [jax.ShapeDtypeStruct((n, NGRP * D), BF16)]
        + [jax.ShapeDtypeStruct((n, D), BF16)] * 6 + [jax.ShapeDtypeStruct((8, D), F32)],
        in_specs=[grp(4), grp(6), grp(7), grp(8), tok, tok, tok, tok, tok,
                  pl.BlockSpec((8, 3 * D), lambda i: (0, 0)), vec, wsp, wsp, wsp, vec, vec],
        out_specs=[tok, tok, tok, pl.BlockSpec((tm, NGRP * D), lambda i: (i, 0))] + [tok] * 6
        + [pl.BlockSpec((8, D), lambda i: (0, 0))],
    )(z, z, z, z, o_f, o_b, hx, xin, tgt, mod, gn, p_a, p_b, w_out, ln_g, ln_b)


def _wmod_grad(c_t, cctx_t, dmx_loc, dmc_loc, name):
    n = dmx_loc.shape[1]

    def body(ct_ref, cc_ref, dmx_ref, dmc_ref, o_ref):
        ct = ct_ref[...]
        sct = ct * _sigmoid(ct)
        cc = cc_ref[...]
        scc = cc * _sigmoid(cc)
        dmc = dmc_ref[0:1, :]
        for b in range(1, NDEV):
            dmc = dmc + dmc_ref[b:b + 1, :]
        acc = scc * dmc
        for b in range(NDEV):
            acc = acc + sct[:, b:b + 1] * dmx_ref[b:b + 1, :]
        o_ref[...] = acc

    return pl.pallas_call(body, name=name, out_shape=jax.ShapeDtypeStruct((D, n), F32))(c_t, cctx_t, dmx_loc, dmc_loc)


PACK_ROWS = 40


def _finalize_small(g_pack, lb, w_mod_full, params):
    npar = len(params)

    def body(*refs):
        gp_ref, lb_ref, wm_ref = refs[:3]
        wmv = refs[3:3 + 3 * npar]
        loss_ref = refs[3 + 3 * npar]
        g_refs = refs[4 + 3 * npar:4 + 4 * npar]
        upd = refs[4 + 4 * npar:4 + 7 * npar]
        tot = refs[-1]
        acc = gp_ref[0]
        for k in range(1, NDEV):
            acc = acc + gp_ref[k]
        tot[...] = acc
        mine = pl.ds(pl.multiple_of(_my_index() * DH, DH), DH)
        (g_cctx, g_bmod, g_bin, g_lbl, g_norm, g_cw, g_cb, g_br, g_bi, g_lam, g_lng, g_lnb) = g_refs

        loss_ref[...] = jnp.broadcast_to(tot[36:37, 0:DH], (8, DH))
        for k in range(3):
            g_bmod[:, k * D:(k + 1) * D] = tot[k:k + 1, :] + tot[3 + k:4 + k, :]
        dmc = jnp.concatenate([tot[3:4, :], tot[4:5, :], tot[5:6, :]], axis=1)
        cv = wmv[0][...]
        proj = _dot_nt(jnp.broadcast_to(dmc, (8, 3 * D)).astype(BF16), wm_ref[...])
        g_cctx[...] = proj[0:1, :] * _dsilu(cv, _sigmoid(cv))
        for k in range(NGRP):
            g_bin[:, k * D:(k + 1) * D] = tot[6 + k:7 + k, :]
        nrm = tot[15:16, 0:DH]
        for h in range(1, NH):
            nrm = nrm + tot[15:16, h * DH:(h + 1) * DH]
        g_norm[...] = nrm
        g_lng[...] = tot[16:17, :]
        g_lnb[...] = tot[17:18, :]
        g_cb[...] = tot[21:22, :] + tot[29:30, :]
        g_cw[0] = tot[22:26, mine] + tot[30:34, mine]
        for ref, row in ((g_br, 18), (g_bi, 19), (g_lam, 20)):
            ref[0, 0:1, :] = tot[row:row + 1, mine]
            ref[0, 1:2, :] = tot[row + 8:row + 9, mine]
        lbl = lb_ref[0:2, mine]
        dl0 = tot[34:36, mine] * lbl * (1.0 - lbl)
        g_lbl[0] = dl0
        g_lbl[1] = -dl0
        for p in range(npar):
            d, mm, vv = _adam_math(g_refs[p][...], wmv[3 * p][...], wmv[3 * p + 1][...], wmv[3 * p + 2][...])
            upd[3 * p][...] = d
            upd[3 * p + 1][...] = mm
            upd[3 * p + 2][...] = vv

    flat = [t for wmv in params for t in wmv]
    shapes = [jax.ShapeDtypeStruct(wmv[0].shape, F32) for wmv in params]
    outs = pl.pallas_call(
        body, name="finalize_small",
        out_shape=[jax.ShapeDtypeStruct((8, DH), F32)] + shapes + [s for s in shapes for _ in range(3)],
        scratch_shapes=[pltpu.VMEM((PACK_ROWS, D), F32)],
    )(g_pack, lb, w_mod_full, *flat)
    grads = list(outs[1:1 + npar])
    upd = [tuple(outs[1 + npar + 3 * p:4 + npar + 3 * p]) for p in range(npar)]
    return outs[0], grads, upd


def _to_colmajor(t, rows):
    return t.reshape(rows, GRID_W, D).transpose(1, 0, 2).reshape(rows * GRID_W, D)


def _to_raster(t, rows):
    return t.reshape(GRID_W, rows, D).transpose(1, 0, 2).reshape(rows * GRID_W, D)


def _local_cols(t, me, width):
    return lax.dynamic_slice_in_dim(t, me * width, width, axis=t.ndim - 1)


def kernel(x, c, ctx, c_ctx, w_mod, b_mod, w_in, b_in, lb_logits, norm_a_g, conv_w, conv_b, w_r, b_r, w_i, b_i, lam, p_a, p_b, w_out, ln_g, ln_b, loss_target, m_c_ctx, m_w_mod, m_b_mod, m_w_in, m_b_in, m_lb_logits, m_norm_a_g, m_conv_w, m_conv_b, m_w_r, m_b_r, m_w_i, m_b_i, m_lam, m_p_a, m_p_b, m_w_out, m_ln_g, m_ln_b, v_c_ctx, v_w_mod, v_b_mod, v_w_in, v_b_in, v_lb_logits, v_norm_a_g, v_conv_w, v_conv_b, v_w_r, v_b_r, v_w_i, v_b_i, v_lam, v_p_a, v_p_b, v_w_out, v_ln_g, v_ln_b):
    me = _my_index()
    xs, cs, tgt = x[0], ctx[0], loss_target[0]
    t_len, c_len = xs.shape[0], cs.shape[0]
    rows = t_len // GRID_W
    wcols = w_in.shape[2]
    mcols = w_mod.shape[2]

    small = jnp.concatenate([lb_logits.reshape(4, DH), conv_w[0], b_r[0], b_i[0], lam[0], jnp.zeros((2, DH), F32),
                             c.reshape(8, DH)], axis=0)
    g_small, g_wmod = _all_gather([small, w_mod[0].astype(BF16)], "gather_params")

    def full_rows(lo, hi):
        return g_small[:, lo:hi, :].transpose(1, 0, 2).reshape(hi - lo, D)

    lbl_f, cw_f, br_f, bi_f, lam_f = full_rows(0, 4), full_rows(4, 8), full_rows(8, 10), full_rows(10, 12), full_rows(12, 14)
    c_all = g_small[:, 16:24, :].reshape(NDEV, D)
    w_mod_f = g_wmod.transpose(1, 0, 2).reshape(D, 3 * D)
    w_r_b, w_i_b = w_r[0].astype(BF16), w_i[0].astype(BF16)

    cc = jnp.concatenate([c.reshape(1, D), c_ctx.reshape(1, D), jnp.zeros((6, D), F32)], axis=0)
    lbl_p = jnp.concatenate([lbl_f.reshape(2, 2, D), jnp.zeros((2, 6, D), F32)], axis=1)
    mod, lb = _prep(cc, w_mod_f, b_mod, lbl_p)
    u_x = _modulate(xs, mod, 0, "modulate_x")
    u_c = _modulate(cs, mod, 1, "modulate_c")
    z_x, w_in_f = _inproj_gather(u_x, w_in[0].astype(BF16), b_in, "inproj_gather")
    z_c = _mm_bias(u_c, w_in_f, b_in, "inproj_c")

    zero_s = jnp.zeros((NH, DH, DH), F32)
    zero_v = jnp.zeros((1, D), F32)
    gla = {}
    out_w = [p_a[0].astype(BF16), p_b[0].astype(BF16), w_out[0].astype(BF16)]
    out_w_f = []
    for d in (0, 1):
        _, ssc, sfc = _gla_fwd(z_c, lb, zero_s, d, f"gla_fwd_c{d}")
        o_d, ssx, _, *gathered = _gla_fwd(z_x, lb, sfc, d, f"gla_fwd_x{d}", side=out_w[1:] if d else out_w[:1])
        out_w_f += [t.reshape(D, D) for t in gathered]
        gla[d] = (ssc, ssx, o_d)
    p_a_f, p_b_f, w_out_f = out_w_f

    x5_c = z_c[:, 5 * D:6 * D]
    x5_x = _to_colmajor(z_x[:, 5 * D:6 * D], rows)
    cb2 = conv_b.reshape(1, D)
    lru = {}
    h_sum = None
    for d in (0, 1):
        prm = (cw_f, cb2, w_r_b[d], br_f[d:d + 1], w_i_b[d], bi_f[d:d + 1], lam_f[d:d + 1])
        h_c, hin_c, hfin_c, *sav_c = _lru_fwd(x5_c, c_len, *prm, zero_v, None, d, f"lru_fwd_c{d}")
        h_x, hin_x, _, sav_a, sav_h, *h_sum = _lru_fwd(x5_x, rows, *prm, hfin_c, lru[0][3] if d else None, d,
                                                       f"lru_fwd_x{d}")
        lru[d] = ((cw_f, w_r_b[d], w_i_b[d], lam_f[d:d + 1]), h_c, hin_c, h_x, hin_x, tuple(sav_c), (sav_a, sav_h))
    hx = _to_raster(h_sum[0], rows)

    gn = jnp.tile(norm_a_g.reshape(1, DH), (1, NH))
    (dr, do, dhx, dz_m, oa, obb, yb16, dya, dyb, dout, mvec) = _merge(
        z_x, gla[0][2], gla[1][2], hx, xs, tgt, mod, gn, p_a_f, p_b_f, w_out_f, ln_g, ln_b)

    dhx_cm = _to_colmajor(dhx, rows)
    lru_dx_x = lru_dx_c = None
    for d in (0, 1):
        prm, h_c, hin_c, h_x, hin_x, sav_c, sav_x = lru[d]
        lru_dx_x, dwr, dwi, lvec, cg0 = _lru_bwd(x5_x, rows, *prm, sav_x, h_x, hin_x, dhx_cm, zero_v, lru_dx_x, None,
                                                 d, f"lru_bwd_x{d}")
        lru_dx_c, dwr, dwi, lvec, _ = _lru_bwd(x5_c, c_len, *prm, sav_c, h_c, hin_c, None, cg0, lru_dx_c,
                                               (dwr, dwi, lvec), d, f"lru_bwd_c{d}")
        lru[d] = (dwr, dwi, lvec)
    dz5_x = _to_raster(lru_dx_x, rows).astype(BF16)
    dz5_c = lru_dx_c.astype(BF16)

    dpa = _mm_tn(oa, dya, None, "dpa", out_dtype=BF16)
    dpb = _mm_tn(obb, dyb, None, "dpb", out_dtype=BF16)
    dwo = _mm_tn(yb16, dout, None, "dwout", out_dtype=BF16)
    wr_pack = jnp.concatenate([lru[0][0], lru[1][0], lru[0][1], lru[1][1]], axis=0).reshape(4 * NH * DH, DH)

    gq_c = gv_c = None
    dzf_c, dlb = {}, {}
    gq_x, dzf_x0, gv_x, dlb_x, ds0, r_pa, r_pb, r_wo, r_wri = _gla_bwd(
        z_x, lb, gla[0][1], do, zero_s, None, None, 0, "gla_bwd_x0", f_dtype=BF16,
        side=[dpa, dpb, dwo, wr_pack], side_splits=[0, 0, 0, 0])
    gq_c, dzf_c[0], gv_c, dlb_c, _ = _gla_bwd(z_c, lb, gla[0][0], None, ds0, None, None, 0, "gla_bwd_c0")
    dlb[0] = dlb_x[0:1] + dlb_c[0:1]
    dz_g, dlb_x, ds0 = _gla_bwd(z_x, lb, gla[1][1], do, zero_s, gq_x, gv_x, 1, "gla_bwd_x1", into=(dz_m, dzf_x0))
    gq_c, dzf_c[1], gv_c, dlb_c, _ = _gla_bwd(z_c, lb, gla[1][0], None, ds0, gq_c, gv_c, 1, "gla_bwd_c1")
    dlb[1] = dlb_x[0:1] + dlb_c[0:1]

    bf = lambda t: t.astype(BF16)
    dz_x = lax.dynamic_update_slice(dz_g, dz5_x, (0, 5 * D))
    zc0 = jnp.zeros((c_len, D), BF16)
    dz_c = jnp.concatenate([bf(gq_c), bf(dzf_c[0]), bf(dzf_c[1]), bf(gv_c), zc0, dz5_c, zc0, zc0, zc0], axis=1)
    dwin_c, dbin_c = _mm_tn(u_c, dz_c, None, "dwin_c", with_colsum=True)

    grad_x, xvec = _input_grad(dz_x, w_in_f, xs, dr, mod, 0, "input_grad_x")
    r_win, dbin = _dwin_exchange(u_x, dz_x, dwin_c, dbin_c, "dwin_exchange")
    _, cvec = _input_grad(dz_c, w_in_f, cs, None, mod, 1, "input_grad_c")
    wri_piece = _sum_rows(r_wri, "sum_w_ri_piece")
    g_w_in, d_w_in, nm_w_in, nv_w_in = _sum_adamw(r_win, w_in, m_w_in, v_w_in, "update_w_in")
    g_p_a, d_p_a, nm_p_a, nv_p_a = _sum_adamw(r_pa, p_a, m_p_a, v_p_a, "update_p_a")
    g_p_b, d_p_b, nm_p_b, nv_p_b = _sum_adamw(r_pb, p_b, m_p_b, v_p_b, "update_p_b")
    g_w_out, d_w_out, nm_w_out, nv_w_out = _sum_adamw(r_wo, w_out, m_w_out, v_w_out, "update_w_out")

    dlb_rows = jnp.concatenate([dlb[0], dlb[1]], axis=0)
    pack = jnp.concatenate([
        xvec[0:1], xvec[1:2], mvec[0:1],
        cvec[0:1], cvec[1:2], jnp.zeros((1, D), F32),
        dbin.reshape(NGRP, D),
        mvec[3:4], mvec[1:2], mvec[2:3],
        lru[0][2][0:8], lru[1][2][0:3],
        lru[1][2][3:8],
        dlb_rows,
        mvec[4:5],
        jnp.zeros((3, D), F32)], axis=0)
    g_pack, g_wri = _all_gather([pack, wri_piece], "gather_small_grads")

    dmx = g_pack[:, 0:3, :].reshape(NDEV, 3 * D)
    dmc = g_pack[:, 3:6, :].reshape(NDEV, 3 * D)
    grad_w_mod = _wmod_grad(c_all.T, c_ctx.reshape(D, 1), _local_cols(dmx, me, mcols), _local_cols(dmc, me, mcols),
                            "grad_w_mod").reshape(1, D, mcols)
    small_params = [(c_ctx.reshape(1, D), m_c_ctx.reshape(1, D), v_c_ctx.reshape(1, D)), (b_mod, m_b_mod, v_b_mod),
                    (b_in, m_b_in, v_b_in), (lb_logits, m_lb_logits, v_lb_logits), (norm_a_g, m_norm_a_g, v_norm_a_g),
                    (conv_w, m_conv_w, v_conv_w), (conv_b, m_conv_b, v_conv_b), (b_r, m_b_r, v_b_r),
                    (b_i, m_b_i, v_b_i), (lam, m_lam, v_lam), (ln_g, m_ln_g, v_ln_g), (ln_b, m_ln_b, v_ln_b)]
    loss_tile, small_g, small_upd = _finalize_small(g_pack, lb, w_mod_f, small_params)
    loss = loss_tile[0, 0]
    (grad_c_ctx, grad_b_mod, grad_b_in, grad_lb_logits, grad_norm_a_g, grad_conv_w, grad_conv_b, grad_b_r, grad_b_i,
     grad_lam, grad_ln_g, grad_ln_b) = small_g
    small_upd[0] = tuple(t.reshape(c_ctx.shape) for t in small_upd[0])
    (o_c_ctx, o_b_mod, o_b_in, o_lb, o_norm, o_conv_w, o_conv_b, o_b_r, o_b_i, o_lam, o_ln_g, o_ln_b) = small_upd

    half = 2 * NH * DH
    g_ri = g_wri.reshape(2 * half, DH)
    grad_w_r, grad_w_i = g_ri[:half].reshape(w_r.shape), g_ri[half:].reshape(w_i.shape)
    d_w_r, nm_w_r, nv_w_r = _adamw(grad_w_r, w_r, m_w_r, v_w_r, "update_w_r")
    d_w_i, nm_w_i, nv_w_i = _adamw(grad_w_i, w_i, m_w_i, v_w_i, "update_w_i")

    d_w_mod, nm_w_mod, nv_w_mod = _adamw(grad_w_mod, w_mod, m_w_mod, v_w_mod, "update_w_mod")

    grads = [grad_c_ctx.reshape(c_ctx.shape), grad_w_mod, grad_b_mod, g_w_in, grad_b_in, grad_lb_logits, grad_norm_a_g,
             grad_conv_w, grad_conv_b, grad_w_r, grad_b_r, grad_w_i, grad_b_i, grad_lam, g_p_a, g_p_b, g_w_out,
             grad_ln_g, grad_ln_b]
    per_kind = []
    for k in range(3):
        per_kind.append([
            o_c_ctx[k], (d_w_mod, nm_w_mod, nv_w_mod)[k], o_b_mod[k], (d_w_in, nm_w_in, nv_w_in)[k], o_b_in[k], o_lb[k],
            o_norm[k], o_conv_w[k], o_conv_b[k], (d_w_r, nm_w_r, nv_w_r)[k], o_b_r[k], (d_w_i, nm_w_i, nv_w_i)[k],
            o_b_i[k], o_lam[k], (d_p_a, nm_p_a, nv_p_a)[k], (d_p_b, nm_p_b, nv_p_b)[k], (d_w_out, nm_w_out, nv_w_out)[k],
            o_ln_g[k], o_ln_b[k]])
    return (loss, grad_x.reshape(x.shape), *grads, *per_kind[0], *per_kind[1], *per_kind[2])
```

```python
import functools

import jax
import jax.numpy as jnp
from jax import lax
from jax.experimental import pallas as pl
from jax.experimental.pallas import tpu as pltpu

F32 = jnp.float32
BF16 = jnp.bfloat16

D = 1024
NH = 8
DH = 128
CHUNK = 64
GLA_HEADS_PER_STEP = 8
GRID_W = 64
NGRP = 9
NDEV = 8
RG_C = 8.0
ALPHA = 2.0 ** 0.25
LN_EPS = 1e-5
RMS_EPS = 1e-6
Q_SCALE = DH ** -0.5
ADAM_LR, ADAM_B1, ADAM_B2, ADAM_EPS, ADAM_WD, ADAM_STEP = 1e-3, 0.9, 0.999, 1e-8, 0.01, 10
ADAM_C1 = 1.0 / (1.0 - ADAM_B1 ** ADAM_STEP)
ADAM_C2 = 1.0 / (1.0 - ADAM_B2 ** ADAM_STEP)

ANY = pl.BlockSpec(memory_space=pl.ANY)


def _sigmoid(t):
    return 1.0 / (1.0 + jnp.exp(-t))


def _dsilu(t, s):
    return s * (1.0 + t * (1.0 - s))


def _dot(a, b):
    return jnp.dot(a, b, preferred_element_type=F32)


def _dot_nt(a, b):
    return lax.dot_general(a, b, (((1,), (1,)), ((), ())), preferred_element_type=F32)


def _dot_tn(a, b):
    return lax.dot_general(a, b, (((0,), (0,)), ((), ())), preferred_element_type=F32)


def _my_index():
    return 4 * lax.axis_index("x") + 2 * lax.axis_index("y") + lax.axis_index("c")


def _dev_tuple(j):
    return (j >> 2, (j >> 1) & 1, j & 1)


def _exchange_sems(n):
    return [pltpu.SemaphoreType.DMA((n * NDEV,)), pltpu.SemaphoreType.DMA((n * NDEV,)), pltpu.SemaphoreType.DMA((n,))]


def _exchange(ins, outs, sems, piece_of=None):
    send_sems, recv_sems, loc_sems = sems
    n = len(ins)

    def src(a, p):
        return ins[a] if piece_of is None else piece_of(ins[a], a, p)

    def push(a, t):
        me, p = _my_index(), _step_peer(t)
        return pltpu.make_async_remote_copy(
            src_ref=src(a, p), dst_ref=outs[a].at[me],
            send_sem=send_sems.at[a * NDEV + t], recv_sem=recv_sems.at[a * NDEV + me],
            device_id=_dev_of(p), device_id_type=pl.DeviceIdType.MESH)

    def local(a):
        me = _my_index()
        return pltpu.make_async_copy(src(a, me), outs[a].at[me], loc_sems.at[a])

    def start():
        for a in range(n):
            local(a).start()
        for t in range(NDEV - 1):
            for a in range(n):
                push(a, t).start()

    def finish():
        me = _my_index()
        for t in range(NDEV - 1):
            for a in range(n):
                push(a, t).wait_send()
        for j in range(NDEV):
            @pl.when(me != j)
            def _():
                for a in range(n):
                    pltpu.make_async_remote_copy(
                        src_ref=src(a, j), dst_ref=outs[a].at[j],
                        send_sem=send_sems.at[a * NDEV], recv_sem=recv_sems.at[a * NDEV + j],
                        device_id=_dev_tuple(j), device_id_type=pl.DeviceIdType.MESH).wait_recv()
        for a in range(n):
            local(a).wait()

    return start, finish


def _all_gather(shards, name):
    n = len(shards)

    def body(*refs):
        start, finish = _exchange(refs[:n], refs[n:2 * n], refs[2 * n:])
        start()
        finish()

    return pl.pallas_call(
        body, name=name,
        out_shape=[jax.ShapeDtypeStruct((NDEV,) + s.shape, s.dtype) for s in shards],
        in_specs=[ANY] * n, out_specs=[ANY] * n, scratch_shapes=_exchange_sems(n),
    )(*shards)


def _pieces(parts, splits):
    shapes = []
    for part, split in zip(parts, splits):
        r, c = part.shape
        shapes.append((r // NDEV, c) if split == 0 else (r, c // NDEV))

    def piece_of(ref, a, j):
        pr, pc = shapes[a]
        if splits[a] == 0:
            start = j * pr if isinstance(j, int) else pl.multiple_of(j * pr, pr)
            return ref.at[pl.ds(start, pr), :]
        start = j * pc if isinstance(j, int) else pl.multiple_of(j * pc, pc)
        return ref.at[:, pl.ds(start, pc)]

    return shapes, piece_of


_STEP_MASKS = ((2, 4, 6, 3, 5, 7, 1, 0), (4, 2, 6, 5, 3, 7, 1, 0))
_GATHER_MASKS = ((0, 1, 2, 4, 3, 5, 6, 7), (0, 1, 4, 2, 5, 3, 6, 7))


def _peer_schedule(table):
    tab = jnp.array(table, jnp.int32)
    return jnp.bitwise_xor(_my_index(), tab[lax.axis_index("c")])


def _step_peer(s, table=_STEP_MASKS):
    def pick(row):
        if isinstance(s, int):
            return jnp.int32(row[s])
        m = jnp.int32(row[NDEV - 1])
        for t in range(NDEV - 2, -1, -1):
            m = jnp.where(s == t, jnp.int32(row[t]), m)
        return m
    mask = jnp.where(lax.axis_index("c") == 0, pick(table[0]), pick(table[1]))
    return jnp.bitwise_xor(_my_index(), mask)


def _dev_of(p):
    return (p // 4, (p // 2) % 2, p % 2)


def _dwin_exchange(u, dz, init, cs_init, name):
    m, ka = u.shape
    n = dz.shape[1]
    pc = n // NDEV
    tk = _row_tile(m, 512)
    nk = m // tk

    def body(pidx_ref, u_ref, dz_ref, init_ref, csi_ref, rwin, cs_ref, acc, sbuf, wsend, wrecv, wloc):
        s, k = pl.program_id(0), pl.program_id(1)
        me = _my_index()

        def slab_copy(slot, p):
            return pltpu.make_async_remote_copy(
                src_ref=sbuf.at[slot], dst_ref=rwin.at[me], send_sem=wsend.at[slot], recv_sem=wrecv.at[me],
                device_id=_dev_of(p), device_id_type=pl.DeviceIdType.MESH)

        @pl.when(k == 0)
        def _():
            acc[...] = init_ref[...]
            cs_ref[...] = csi_ref[...]

        bv = dz_ref[...]
        acc[...] += _dot_tn(u_ref[...], bv)
        cs_ref[...] += jnp.sum(bv.astype(F32), axis=0, keepdims=True)

        @pl.when(k == nk - 1)
        def _():
            slot = s % 2

            @pl.when(s >= 2)
            def _():
                slab_copy(slot, me).wait_send()

            sbuf[slot] = acc[...].astype(BF16)

            @pl.when(s < NDEV - 1)
            def _():
                slab_copy(slot, _step_peer(s)).start()

            @pl.when(s == NDEV - 1)
            def _():
                own = pltpu.make_async_copy(sbuf.at[slot], rwin.at[me], wloc.at[0])
                own.start()
                slab_copy(1 - slot, me).wait_send()
                for j in range(NDEV):
                    @pl.when(me != j)
                    def _():
                        pltpu.make_async_remote_copy(
                            src_ref=sbuf.at[0], dst_ref=rwin.at[j], send_sem=wsend.at[0], recv_sem=wrecv.at[j],
                            device_id=_dev_tuple(j), device_id_type=pl.DeviceIdType.MESH).wait_recv()
                own.wait()

    grid_spec = pltpu.PrefetchScalarGridSpec(
        num_scalar_prefetch=1, grid=(NDEV, nk),
        in_specs=[pl.BlockSpec((tk, ka), lambda s, k, pidx: (k, 0)),
                  pl.BlockSpec((tk, pc), lambda s, k, pidx: (k, pidx[s])),
                  pl.BlockSpec((ka, pc), lambda s, k, pidx: (0, pidx[s])),
                  pl.BlockSpec((1, pc), lambda s, k, pidx: (0, pidx[s]))],
        out_specs=[ANY, pl.BlockSpec((1, pc), lambda s, k, pidx: (0, pidx[s]))],
        scratch_shapes=[pltpu.VMEM((ka, pc), F32), pltpu.VMEM((2, ka, pc), BF16),
                        pltpu.SemaphoreType.DMA((2,)), pltpu.SemaphoreType.DMA((NDEV,)), pltpu.SemaphoreType.DMA((1,))])
    return pl.pallas_call(
        body, name=name, grid_spec=grid_spec,
        out_shape=[jax.ShapeDtypeStruct((NDEV, ka, pc), BF16), jax.ShapeDtypeStruct((1, n), F32)],
    )(_peer_schedule(_STEP_MASKS), u, dz, init, cs_init)


def _inproj_gather(u, w_loc, bias, name):
    m, k = u.shape
    pc = w_loc.shape[1]
    n = pc * NDEV
    tm = _row_tile(m, 512)
    ni = m // tm
    direct = (1, 2, 3, 6)
    relay_sem = {2: 4, 3: 5, 6: 7}

    def body(pidx_ref, u_ref, b_ref, wl_ref, z_ref, wall, wbuf, wsend, wrecv, ldsem, ownsem):
        s, i = pl.program_id(0), pl.program_id(1)
        me = _my_index()

        def shard_push(t):
            return pltpu.make_async_remote_copy(
                src_ref=wl_ref, dst_ref=wall.at[me], send_sem=wsend.at[t], recv_sem=wrecv.at[me],
                device_id=_dev_of(_step_peer(t, _GATHER_MASKS)), device_id_type=pl.DeviceIdType.MESH)

        def relay(t):
            p = _step_peer(t, _GATHER_MASKS)
            return pltpu.make_async_remote_copy(
                src_ref=wall.at[p], dst_ref=wall.at[p], send_sem=wsend.at[relay_sem[t]], recv_sem=wrecv.at[p],
                device_id=_dev_of(_step_peer(1, _GATHER_MASKS)), device_id_type=pl.DeviceIdType.MESH)

        def load(slot, src):
            return pltpu.make_async_copy(src, wbuf.at[slot], ldsem.at[slot])

        own = pltpu.make_async_copy(wl_ref, wall.at[me], ownsem.at[0])

        @pl.when((s == 0) & (i == 0))
        def _():
            own.start()
            load(0, wl_ref).start()
            for t in direct:
                shard_push(t).start()

        @pl.when((i == ni // 2) & (s < NDEV - 1))
        def _():
            nxt = _step_peer(s + 1, _GATHER_MASKS)
            pltpu.make_async_remote_copy(
                src_ref=wl_ref, dst_ref=wall.at[nxt], send_sem=wsend.at[0], recv_sem=wrecv.at[nxt],
                device_id=_dev_of(nxt), device_id_type=pl.DeviceIdType.MESH).wait_recv()
            for t in relay_sem:
                @pl.when(s + 1 == t)
                def _():
                    relay(t).start()
            load((s + 1) % 2, wall.at[nxt]).start()

        @pl.when(i == 0)
        def _():
            load(s % 2, wl_ref).wait()

        z_ref[...] = _dot(u_ref[...], wbuf[s % 2]) + b_ref[...]

        @pl.when((s == NDEV - 1) & (i == ni - 1))
        def _():
            own.wait()
            for t in direct:
                shard_push(t).wait_send()
            for t in relay_sem:
                relay(t).wait_send()

    grid_spec = pltpu.PrefetchScalarGridSpec(
        num_scalar_prefetch=1, grid=(NDEV, ni),
        in_specs=[pl.BlockSpec((tm, k), lambda s, i, pidx: (i, 0)),
                  pl.BlockSpec((1, pc), lambda s, i, pidx: (0, pidx[s])), ANY],
        out_specs=[pl.BlockSpec((tm, pc), lambda s, i, pidx: (i, pidx[s])), ANY],
        scratch_shapes=[pltpu.VMEM((2, k, pc), BF16),
                        pltpu.SemaphoreType.DMA((NDEV,)), pltpu.SemaphoreType.DMA((NDEV,)),
                        pltpu.SemaphoreType.DMA((2,)), pltpu.SemaphoreType.DMA((1,))])
    return pl.pallas_call(
        body, name=name, grid_spec=grid_spec,
        out_shape=[jax.ShapeDtypeStruct((m, n), F32), jax.ShapeDtypeStruct((NDEV, k, pc), w_loc.dtype)],
    )(_peer_schedule(_GATHER_MASKS), u, bias, w_loc)


def _adam_math(g, w, m, v):
    m2 = ADAM_B1 * m + (1.0 - ADAM_B1) * g
    v2 = ADAM_B2 * v + (1.0 - ADAM_B2) * (g * g)
    delta = -ADAM_LR * ((m2 * ADAM_C1) / (jnp.sqrt(v2 * ADAM_C2) + ADAM_EPS) + ADAM_WD * w)
    return delta, m2, v2


def _row_tile(r, cap):
    t = min(r, cap)
    while r % t:
        t //= 2
    return t


def _adamw(g, w, m, v, name):
    shape = w.shape
    cols = shape[-1] if w.ndim >= 2 and shape[-1] % 128 == 0 else 128
    g2, w2, m2, v2 = (t.reshape(-1, cols) for t in (g, w, m, v))
    r = g2.shape[0]
    tr = _row_tile(r, 256)

    def body(g_ref, w_ref, m_ref, v_ref, d_ref, mo_ref, vo_ref):
        d, mm, vv = _adam_math(g_ref[...], w_ref[...], m_ref[...], v_ref[...])
        d_ref[...] = d
        mo_ref[...] = mm
        vo_ref[...] = vv

    spec = pl.BlockSpec((tr, cols), lambda i: (i, 0))
    outs = pl.pallas_call(
        body, name=name, grid=(r // tr,),
        out_shape=[jax.ShapeDtypeStruct((r, cols), F32)] * 3,
        in_specs=[spec] * 4, out_specs=[spec] * 3,
    )(g2, w2, m2, v2)
    return tuple(o.reshape(shape) for o in outs)


def _sum_adamw(parts, w, m, v, name):
    _, r, c = parts.shape
    shape = w.shape
    w2, m2, v2 = (t.reshape(r, c) for t in (w, m, v))
    tr = _row_tile(r, 128)

    def body(p_ref, w_ref, m_ref, v_ref, g_ref, d_ref, mo_ref, vo_ref):
        g = p_ref[0].astype(F32)
        for k in range(1, NDEV):
            g = g + p_ref[k].astype(F32)
        d, mm, vv = _adam_math(g, w_ref[...], m_ref[...], v_ref[...])
        g_ref[...] = g
        d_ref[...] = d
        mo_ref[...] = mm
        vo_ref[...] = vv

    spec = pl.BlockSpec((tr, c), lambda i: (i, 0))
    outs = pl.pallas_call(
        body, name=name, grid=(r // tr,),
        out_shape=[jax.ShapeDtypeStruct((r, c), F32)] * 4,
        in_specs=[pl.BlockSpec((NDEV, tr, c), lambda i: (0, i, 0))] + [spec] * 3, out_specs=[spec] * 4,
    )(parts, w2, m2, v2)
    return tuple(o.reshape(shape) for o in outs)


def _sum_rows(parts, name):
    _, r, c = parts.shape

    def body(p_ref, o_ref):
        g = p_ref[0]
        for k in range(1, NDEV):
            g = g + p_ref[k]
        o_ref[...] = g

    return pl.pallas_call(
        body, name=name, out_shape=jax.ShapeDtypeStruct((r, c), F32),
    )(parts)


def _prep(cc, w_mod_full, b_mod, lbl):
    def body(cc_ref, w_ref, b_ref, l_ref, mod_ref, lb_ref):
        t = cc_ref[...]
        s = (t * _sigmoid(t)).astype(BF16)
        mod_ref[...] = _dot(s, w_ref[...]) + b_ref[...]
        lb_ref[...] = _sigmoid(l_ref[0] - l_ref[1])

    return pl.pallas_call(
        body, name="prep",
        out_shape=[jax.ShapeDtypeStruct((8, 3 * D), F32), jax.ShapeDtypeStruct((8, D), F32)],
    )(cc, w_mod_full, b_mod, lbl)


def _modulate(xin, mod, row, name):
    n = xin.shape[0]
    tm = _row_tile(n, 512)

    def body(x_ref, mod_ref, u_ref):
        sh = mod_ref[row:row + 1, 0:D]
        sc = mod_ref[row:row + 1, D:2 * D]
        u_ref[...] = (x_ref[...] * (1.0 + sc) + sh).astype(BF16)

    return pl.pallas_call(
        body, name=name, grid=(n // tm,),
        out_shape=jax.ShapeDtypeStruct((n, D), BF16),
        in_specs=[pl.BlockSpec((tm, D), lambda i: (i, 0)), pl.BlockSpec((8, 3 * D), lambda i: (0, 0))],
        out_specs=pl.BlockSpec((tm, D), lambda i: (i, 0)),
    )(xin, mod)


def _mm_bias(a, w_all, bias, name):
    m, k = a.shape
    tn = w_all.shape[2]
    n = tn * NDEV
    tm = _row_tile(m, 512)

    def body(a_ref, b_ref, bias_ref, o_ref):
        o_ref[...] = _dot(a_ref[...], b_ref[0]) + bias_ref[...]

    return pl.pallas_call(
        body, name=name, grid=(NDEV, m // tm),
        out_shape=jax.ShapeDtypeStruct((m, n), F32),
        in_specs=[pl.BlockSpec((tm, k), lambda j, i: (i, 0)), pl.BlockSpec((1, k, tn), lambda j, i: (j, 0, 0)),
                  pl.BlockSpec((1, tn), lambda j, i: (0, j))],
        out_specs=pl.BlockSpec((tm, tn), lambda j, i: (i, j)),
    )(a, w_all, bias)


def _mm_tn(a, b, init, name, with_colsum=False, colsum_init=None, out_dtype=F32):
    m, ka = a.shape
    n = b.shape[1]
    tk = _row_tile(m, 512)
    tn = 1024
    nk = m // tk
    has_init = init is not None

    def body(*refs):
        a_ref, b_ref = refs[0], refs[1]
        pos = 2
        init_ref = cs_init_ref = None
        if has_init:
            init_ref = refs[pos]
            pos += 1
            if with_colsum:
                cs_init_ref = refs[pos]
                pos += 1
        o_ref = refs[pos]
        cs_ref = refs[pos + 1] if with_colsum else None
        acc = refs[-1]
        k = pl.program_id(1)

        @pl.when(k == 0)
        def _():
            if has_init:
                acc[...] = init_ref[...]
                if with_colsum:
                    cs_ref[...] = cs_init_ref[...]
            else:
                acc[...] = jnp.zeros_like(acc)
                if with_colsum:
                    cs_ref[...] = jnp.zeros_like(cs_ref)

        bv = b_ref[...]
        acc[...] += _dot_tn(a_ref[...], bv)
        if with_colsum:
            cs_ref[...] += jnp.sum(bv.astype(F32), axis=0, keepdims=True)

        @pl.when(k == nk - 1)
        def _():
            o_ref[...] = acc[...].astype(out_dtype)

    in_specs = [pl.BlockSpec((tk, ka), lambda j, k: (k, 0)), pl.BlockSpec((tk, tn), lambda j, k: (k, j))]
    args = [a, b]
    if has_init:
        in_specs.append(pl.BlockSpec((ka, tn), lambda j, k: (0, j)))
        args.append(init)
        if with_colsum:
            in_specs.append(pl.BlockSpec((1, tn), lambda j, k: (0, j)))
            args.append(colsum_init)
    out_shape = [jax.ShapeDtypeStruct((ka, n), out_dtype)]
    out_specs = [pl.BlockSpec((ka, tn), lambda j, k: (0, j))]
    if with_colsum:
        out_shape.append(jax.ShapeDtypeStruct((1, n), F32))
        out_specs.append(pl.BlockSpec((1, tn), lambda j, k: (0, j)))
    outs = pl.pallas_call(
        body, name=name, grid=(n // tn, nk), out_shape=out_shape, in_specs=in_specs, out_specs=out_specs,
        scratch_shapes=[pltpu.VMEM((ka, tn), F32)],
    )(*args)
    return outs if with_colsum else outs[0]


def _input_grad(dz, w_all, xin, dr, mod, row, name, side=(), side_splits=()):
    m, n = dz.shape
    tm = _row_tile(m, 512)
    tk = w_all.shape[2]
    nk = NDEV
    ni = m // tm
    has_dr = dr is not None
    ns = len(side)
    piece_shapes, piece_of = _pieces(side, side_splits)

    def body(*refs):
        dz_ref, w_ref, x_ref = refs[:3]
        pos = 3
        dr_ref = refs[pos] if has_dr else None
        pos += int(has_dr)
        mod_ref = refs[pos]
        side_in = refs[pos + 1:pos + 1 + ns]
        pos += 1 + ns
        gx_ref = refs[pos] if has_dr else None
        pos += int(has_dr)
        vec_ref = refs[pos]
        side_out = refs[pos + 1:pos + 1 + ns]
        acc = refs[pos + 1 + ns]
        i, k = pl.program_id(0), pl.program_id(1)
        if ns:
            side_start, side_finish = _exchange(side_in, side_out, refs[pos + 2 + ns:], piece_of)

            @pl.when((i == 0) & (k == 0))
            def _():
                side_start()

        @pl.when(k == 0)
        def _():
            acc[...] = jnp.zeros_like(acc)

        @pl.when((i == 0) & (k == 0))
        def _():
            vec_ref[...] = jnp.zeros_like(vec_ref)

        acc[...] += _dot_nt(dz_ref[...], w_ref[0])

        @pl.when(k == nk - 1)
        def _():
            du = acc[...]
            xv = x_ref[...]
            if has_dr:
                sc = mod_ref[row:row + 1, D:2 * D]
                gx_ref[...] = ALPHA * dr_ref[...] + du * (1.0 + sc)
            vec_ref[0:1, :] += jnp.sum(du, axis=0, keepdims=True)
            vec_ref[1:2, :] += jnp.sum(du * xv, axis=0, keepdims=True)

        if ns:
            @pl.when((i == ni - 1) & (k == nk - 1))
            def _():
                side_finish()

    row_spec = pl.BlockSpec((tm, D), lambda i, k: (i, 0))
    in_specs = [pl.BlockSpec((tm, tk), lambda i, k: (i, k)), pl.BlockSpec((1, D, tk), lambda i, k: (k, 0, 0)), row_spec]
    args = [dz, w_all, xin]
    if has_dr:
        in_specs.append(row_spec)
        args.append(dr)
    in_specs.append(pl.BlockSpec((8, 3 * D), lambda i, k: (0, 0)))
    args.append(mod)
    in_specs += [ANY] * ns
    args += list(side)
    out_shape, out_specs = [], []
    if has_dr:
        out_shape.append(jax.ShapeDtypeStruct((m, D), F32))
        out_specs.append(row_spec)
    out_shape.append(jax.ShapeDtypeStruct((8, D), F32))
    out_specs.append(pl.BlockSpec((8, D), lambda i, k: (0, 0)))
    out_shape += [jax.ShapeDtypeStruct((NDEV,) + piece_shapes[a], side[a].dtype) for a in range(ns)]
    out_specs += [ANY] * ns
    outs = pl.pallas_call(
        body, name=name, grid=(ni, nk), out_shape=out_shape, in_specs=in_specs, out_specs=out_specs,
        scratch_shapes=[pltpu.VMEM((tm, D), F32)] + (_exchange_sems(ns) if ns else []),
    )(*args)
    return tuple(outs) if has_dr else (None, *outs)


def _tri(reverse):
    r = lax.broadcasted_iota(jnp.int32, (CHUNK, CHUNK), 0)
    c = lax.broadcasted_iota(jnp.int32, (CHUNK, CHUNK), 1)
    return (c >= r) if reverse else (c <= r)


def _cum_f32(tri_b, t):
    hi = t.astype(BF16)
    r1 = t - hi.astype(F32)
    mid = r1.astype(BF16)
    lo = (r1 - mid.astype(F32)).astype(BF16)
    return _dot(tri_b, hi) + _dot(tri_b, mid) + _dot(tri_b, lo)


def _gla_features(zq, zf, lb):
    sq = _sigmoid(zq)
    q = zq * sq * Q_SCALE
    sf = _sigmoid(zf)
    f = lb + (1.0 - lb) * sf
    return q, sq, f, sf


def _gla_block(n):
    return 256 if n % 256 == 0 else CHUNK


def _gla_fwd(z, lb, s0, d, name, side=()):
    n = z.shape[0]
    blk = _gla_block(n)
    nb, npb = n // blk, blk // CHUNK
    reverse = d == 1
    last = 0 if reverse else CHUNK - 1
    order = list(range(npb))[::-1] if reverse else list(range(npb))
    ns = len(side)

    def bmap(i):
        return nb - 1 - i if reverse else i

    hp = GLA_HEADS_PER_STEP
    hw = hp * DH
    units = [(hh, cidx) for hh in range(hp) for cidx in order]

    def body(zq_ref, zf_ref, zv_ref, lb_ref, s0_ref, *rest):
        side_in = rest[:ns]
        o_ref, ss_ref, sf_ref = rest[ns:ns + 3]
        side_out = rest[ns + 3:2 * ns + 3]
        st = rest[2 * ns + 3]
        i = pl.program_id(1)
        if ns:
            side_start, side_finish = _exchange(side_in, side_out, rest[2 * ns + 4:])

            @pl.when((pl.program_id(0) == 0) & (i == 0))
            def _():
                side_start()

        @pl.when(i == 0)
        def _():
            st[...] = s0_ref[...]

        mask = _tri(reverse)
        tri_b = jnp.where(mask, 1.0, 0.0).astype(BF16)
        feat = {}
        for u in units:
            hh, cidx = u
            rows, cols = pl.ds(cidx * CHUNK, CHUNK), pl.ds(hh * DH, DH)
            q, _, f, _ = _gla_features(zq_ref[rows, cols], zf_ref[rows, cols], lb_ref[d:d + 1, cols])
            feat[u] = (q, 1.0 - f, jnp.log(f), zv_ref[rows, cols].astype(BF16))
        dec = {u: _cum_f32(tri_b, feat[u][2]) for u in units}
        ops = {}
        for u in units:
            q, k, _, vb = feat[u]
            g = dec[u]
            gl = g[last:last + 1, :]
            ops[u] = ((q * jnp.exp(g)).astype(BF16), (k * jnp.exp(-g)).astype(BF16),
                      (k * jnp.exp(gl - g)).astype(BF16), jnp.exp(gl), vb)
        att = {u: jnp.where(mask, _dot_nt(ops[u][0], ops[u][1]), 0.0).astype(BF16) for u in units}
        upd = {u: _dot_tn(ops[u][4], ops[u][2]) for u in units}
        intra = {u: _dot(att[u], ops[u][4]) for u in units}
        s_in = {}
        for hh in range(hp):
            s = st[hh]
            for cidx in order:
                s_in[(hh, cidx)] = s
                s = s * ops[(hh, cidx)][3] + upd[(hh, cidx)]
            st[hh] = s
            sf_ref[hh] = s
        for u in units:
            hh, cidx = u
            rows, cols = pl.ds(cidx * CHUNK, CHUNK), pl.ds(hh * DH, DH)
            o_ref[rows, cols] = intra[u] + _dot_nt(ops[u][0], s_in[u].astype(BF16))
            ss_ref[hh, cidx] = s_in[u]

        if ns:
            @pl.when((pl.program_id(0) == NH // hp - 1) & (i == nb - 1))
            def _():
                side_finish()

    def col(g):
        return lambda h, i: (bmap(i), g * (NH // hp) + h)

    return pl.pallas_call(
        body, name=name, grid=(NH // hp, nb),
        out_shape=[jax.ShapeDtypeStruct((n, D), F32), jax.ShapeDtypeStruct((NH, n // CHUNK, DH, DH), F32),
                   jax.ShapeDtypeStruct((NH, DH, DH), F32)]
        + [jax.ShapeDtypeStruct((NDEV,) + t.shape, t.dtype) for t in side],
        in_specs=[pl.BlockSpec((blk, hw), col(0)), pl.BlockSpec((blk, hw), col(1 + d)),
                  pl.BlockSpec((blk, hw), col(3)), pl.BlockSpec((8, hw), lambda h, i: (0, h)),
                  pl.BlockSpec((hp, DH, DH), lambda h, i: (h, 0, 0))] + [ANY] * ns,
        out_specs=[pl.BlockSpec((blk, hw), lambda h, i: (bmap(i), h)),
                   pl.BlockSpec((hp, npb, DH, DH), lambda h, i: (h, bmap(i), 0, 0)),
                   pl.BlockSpec((hp, DH, DH), lambda h, i: (h, 0, 0))] + [ANY] * ns,
        scratch_shapes=[pltpu.VMEM((hp, DH, DH), F32)] + (_exchange_sems(ns) if ns else []),
    )(z, z, z, lb, s0, *side)


def _gla_bwd(z, lb, s_start, do, ds_fin, acc_q, acc_v, d, name, f_dtype=F32, into=None, side=(), side_splits=()):
    n = z.shape[0]
    blk = _gla_block(n)
    nb, npb = n // blk, blk // CHUNK
    reverse = d == 1
    last = 0 if reverse else CHUNK - 1
    order = list(range(npb)) if reverse else list(range(npb))[::-1]
    has_do = do is not None
    has_acc = acc_q is not None
    fused = into is not None
    assert not fused or d == 1
    ns = len(side)
    assert not (fused and ns)
    piece_shapes, piece_of = _pieces(side, side_splits)
    hp = NH if fused else GLA_HEADS_PER_STEP
    hw = hp * DH
    units = [(hh, cidx) for hh in range(hp) for cidx in order]

    def bmap(i):
        return i if reverse else nb - 1 - i

    def body(*refs):
        zq_ref, zf_ref, zv_ref, lb_ref, ss_ref, dsf_ref = refs[:6]
        pos = 6
        do_ref = aq_ref = av_ref = None
        if has_do:
            do_ref = refs[pos]
            pos += 1
        if has_acc:
            aq_ref, av_ref = refs[pos], refs[pos + 1]
            pos += 2
        if fused:
            other_ref = refs[pos + 1]
            dz_ref, dlb_ref, ds0_ref, dst = refs[pos + 2:]
            dz_ref[:, D:2 * D] = other_ref[...]
        else:
            side_in = refs[pos:pos + ns]
            dzq_ref, dzf_ref, dzv_ref, dlb_ref, ds0_ref = refs[pos + ns:pos + ns + 5]
            side_out = refs[pos + ns + 5:pos + 2 * ns + 5]
            dst = refs[pos + 2 * ns + 5]
        i = pl.program_id(1)
        if ns:
            side_start, side_finish = _exchange(side_in, side_out, refs[pos + 2 * ns + 6:], piece_of)

            @pl.when((pl.program_id(0) == 0) & (i == 0))
            def _():
                side_start()

        @pl.when(i == 0)
        def _():
            dst[...] = dsf_ref[...]
            dlb_ref[...] = jnp.zeros_like(dlb_ref)

        mask = _tri(reverse)
        tri_b = jnp.where(mask, 1.0, 0.0).astype(BF16)
        tri_t = jnp.where(_tri(not reverse), 1.0, 0.0).astype(BF16)

        def where(u):
            return pl.ds(u[1] * CHUNK, CHUNK), pl.ds(u[0] * DH, DH)

        feat = {}
        for u in units:
            rows, cols = where(u)
            zq, zf = zq_ref[rows, cols], zf_ref[rows, cols]
            lbv = lb_ref[d:d + 1, cols]
            q, sq, f, sf = _gla_features(zq, zf, lbv)
            feat[u] = dict(zq=zq, q=q, sq=sq, f=f, sf=sf, lbv=lbv, k=1.0 - f, vb=zv_ref[rows, cols].astype(BF16))
        dec = {u: _cum_f32(tri_b, jnp.log(feat[u]["f"])) for u in units}
        for u in units:
            w = feat[u]
            g = dec[u]
            gl = g[last:last + 1, :]
            w["eg"], w["egi"], w["ege"], w["egl"] = jnp.exp(g), jnp.exp(-g), jnp.exp(gl - g), jnp.exp(gl)
            w["qd"], w["ki"], w["ke"] = w["q"] * w["eg"], w["k"] * w["egi"], w["k"] * w["ege"]
            w["qdb"], w["kib"], w["keb"] = w["qd"].astype(BF16), w["ki"].astype(BF16), w["ke"].astype(BF16)
            w["s_in"] = ss_ref[u[0], u[1]]
        if has_do:
            for u in units:
                w = feat[u]
                rows, cols = where(u)
                w["dob"] = do_ref[rows, cols].astype(BF16)
            for u in units:
                w = feat[u]
                w["a"] = jnp.where(mask, _dot_nt(w["qdb"], w["kib"]), 0.0).astype(BF16)
                w["da"] = jnp.where(mask, _dot_nt(w["dob"], w["vb"]), 0.0).astype(BF16)
                w["m"] = _dot_tn(w["dob"], w["qdb"])
        for hh in range(hp):
            ds = dst[hh]
            for cidx in order:
                w = feat[(hh, cidx)]
                w["ds"] = ds
                ds = ds * w["egl"]
                if has_do:
                    ds = ds + w["m"]
            dst[hh] = ds
            ds0_ref[hh] = ds
        for u in units:
            w = feat[u]
            dsb = w["ds"].astype(BF16)
            w["dke"] = _dot(w["vb"], dsb)
            w["dv"] = _dot_nt(w["keb"], dsb)
            if has_do:
                w["dv"] = w["dv"] + _dot_tn(w["a"], w["dob"])
                w["dqd"] = _dot(w["da"], w["kib"]) + _dot(w["dob"], w["s_in"].astype(BF16))
                w["dki"] = _dot_tn(w["da"], w["qdb"])
        for u in units:
            w = feat[u]
            dkeke = w["dke"] * w["ke"]
            w["dgl"] = (w["egl"] * jnp.sum(w["s_in"] * w["ds"], axis=0, keepdims=True)
                        + jnp.sum(dkeke, axis=0, keepdims=True))
            dg = -dkeke
            dk = w["dke"] * w["ege"]
            if has_do:
                dg = dg + w["dqd"] * w["qd"] - w["dki"] * w["ki"]
                dk = dk + w["dki"] * w["egi"]
            w["dg"], w["dk"] = dg, dk
        dlf = {u: _cum_f32(tri_t, feat[u]["dg"]) for u in units}
        for u in units:
            w = feat[u]
            rows, cols = where(u)
            df = (dlf[u] + w["dgl"]) / w["f"] - w["dk"]
            sf = w["sf"]
            dzf = df * (1.0 - w["lbv"]) * sf * (1.0 - sf)
            dlb_ref[0:1, cols] += jnp.sum(df * (1.0 - sf), axis=0, keepdims=True)
            if has_do:
                dzq = w["dqd"] * w["eg"] * (Q_SCALE * _dsilu(w["zq"], w["sq"]))
            else:
                dzq = jnp.zeros((CHUNK, DH), F32)
            dv = w["dv"]
            if has_acc:
                dzq = dzq + aq_ref[rows, cols]
                dv = dv + av_ref[rows, cols]
            if fused:
                lane = u[0] * DH
                dz_ref[rows, pl.ds(lane, DH)] = dzq.astype(BF16)
                dz_ref[rows, pl.ds(2 * D + lane, DH)] = dzf.astype(BF16)
                dz_ref[rows, pl.ds(3 * D + lane, DH)] = dv.astype(BF16)
            else:
                dzq_ref[rows, cols] = dzq
                dzf_ref[rows, cols] = dzf.astype(f_dtype)
                dzv_ref[rows, cols] = dv

        if ns:
            @pl.when((pl.program_id(0) == NH // hp - 1) & (i == nb - 1))
            def _():
                side_finish()

    def col(g):
        return lambda h, i: (bmap(i), g * (NH // hp) + h)

    tok = pl.BlockSpec((blk, hw), lambda h, i: (bmap(i), h))
    state = pl.BlockSpec((hp, DH, DH), lambda h, i: (h, 0, 0))
    in_specs = [pl.BlockSpec((blk, hw), col(0)), pl.BlockSpec((blk, hw), col(1 + d)), pl.BlockSpec((blk, hw), col(3)),
                pl.BlockSpec((8, hw), lambda h, i: (0, h)),
                pl.BlockSpec((hp, npb, DH, DH), lambda h, i: (h, bmap(i), 0, 0)), state]
    args = [z, z, z, lb, s_start, ds_fin]
    if has_do:
        in_specs.append(tok)
        args.append(do)
    if has_acc:
        in_specs += [tok, tok]
        args += [acc_q, acc_v]
    tail_shape = [jax.ShapeDtypeStruct((8, D), F32), jax.ShapeDtypeStruct((NH, DH, DH), F32)]
    tail_specs = [pl.BlockSpec((8, hw), lambda h, i: (0, h)), state]
    if fused:
        buf, other = into
        aliases = {len(args): 0}
        in_specs += [ANY, tok]
        args += [buf, other]
        out_shape = [jax.ShapeDtypeStruct(buf.shape, buf.dtype)] + tail_shape
        out_specs = [pl.BlockSpec((blk, 4 * D), lambda h, i: (bmap(i), 0))] + tail_specs
    else:
        aliases = {}
        in_specs += [ANY] * ns
        args += list(side)
        out_shape = [jax.ShapeDtypeStruct((n, D), F32), jax.ShapeDtypeStruct((n, D), f_dtype),
                     jax.ShapeDtypeStruct((n, D), F32)] + tail_shape
        out_shape += [jax.ShapeDtypeStruct((NDEV,) + piece_shapes[a], side[a].dtype) for a in range(ns)]
        out_specs = [tok, tok, tok] + tail_specs + [ANY] * ns
    return pl.pallas_call(
        body, name=name, grid=(NH // hp, nb), out_shape=out_shape, in_specs=in_specs, out_specs=out_specs,
        input_output_aliases=aliases,
        scratch_shapes=[pltpu.VMEM((hp, DH, DH), F32)] + (_exchange_sems(ns) if ns else []),
    )(*args)


def _shift(t, s, fill, down):
    n = t.shape[0]
    rows = lax.broadcasted_iota(jnp.int32, t.shape, 0)
    if down:
        return jnp.where(rows >= s, pltpu.roll(t, s, 0), fill)
    return jnp.where(rows < n - s, pltpu.roll(t, n - s, 0), fill)


SUBLANES = 8
LRU_SAVED = 4


def _chain_scan(a, b, h_in, down):
    n = a.shape[0]
    ng = n // SUBLANES
    rows = lax.broadcasted_iota(jnp.int32, (SUBLANES, a.shape[1]), 0)
    local = []
    for g in range(ng):
        aa, bb = a[g * SUBLANES:(g + 1) * SUBLANES], b[g * SUBLANES:(g + 1) * SUBLANES]
        for s in (1, 2, 4):
            if down:
                keep, amt = rows >= s, s
            else:
                keep, amt = rows < SUBLANES - s, SUBLANES - s
            bb = bb + aa * jnp.where(keep, pltpu.roll(bb, amt, 0), 0.0)
            aa = aa * jnp.where(keep, pltpu.roll(aa, amt, 0), 1.0)
        local.append((aa, bb))
    out = [None] * ng
    carry = h_in
    for g in (range(ng) if down else range(ng - 1, -1, -1)):
        aa, bb = local[g]
        hg = bb + aa * carry
        out[g] = hg
        carry = hg[SUBLANES - 1:SUBLANES] if down else hg[0:1]
    return (jnp.concatenate(out, axis=0) if ng > 1 else out[0]), carry


def _conv_taps(xv):
    return (_shift(xv, 1, 0.0, True), xv, _shift(xv, 1, 0.0, False), _shift(xv, 2, 0.0, False))


def _conv(taps, cw, cb):
    return cb + cw[0:1, :] * taps[0] + cw[1:2, :] * taps[1] + cw[2:3, :] * taps[2] + cw[3:4, :] * taps[3]


def _neg_expm1(t):
    series = -t * (1.0 + t * (0.5 + t * (1.0 / 6.0 + t * (1.0 / 24.0 + t * (1.0 / 120.0)))))
    return jnp.where(t > -0.1, series, 1.0 - jnp.exp(t))


def _lru_gates(xc, wr, br, wi, bi, lam):
    xcb = xc.astype(BF16)
    r = _sigmoid(_dot(xcb, wr) + br)
    gi = _sigmoid(_dot(xcb, wi) + bi)
    sp = jnp.maximum(-lam, 0.0) + jnp.log(1.0 + jnp.exp(-jnp.abs(lam)))
    la = -RG_C * r * sp
    a = jnp.exp(la)
    mult = jnp.sqrt(_neg_expm1(2.0 * la))
    return xcb, r, gi, sp, a, mult


def _lru_fwd(xin, blk, cw, cb, wr, br, wi, bi, lam, h0, acc_h, d, name):
    n = xin.shape[0]
    nb = n // blk
    reverse = d == 1
    down = not reverse
    has_acc = acc_h is not None

    def bmap(i):
        return nb - 1 - i if reverse else i

    def body(*refs):
        x_ref, cw_ref, cb_ref, wr_ref, br_ref, wi_ref, bi_ref, lam_ref, h0_ref = refs[:9]
        pos = 9
        acc_ref = refs[pos] if has_acc else None
        pos += int(has_acc)
        h_ref, hin_ref, hfin_ref, sav_a_ref, sav_ref = refs[pos:pos + 5]
        pos += 5
        hsum_ref = refs[pos] if has_acc else None
        carry = refs[-1]
        i = pl.program_id(0)

        @pl.when(i == 0)
        def _():
            carry[...] = h0_ref[...]

        for g in range(NH):
            cols = pl.ds(g * DH, DH)
            xc = _conv(_conv_taps(x_ref[:, cols]), cw_ref[:, cols], cb_ref[:, cols])
            _, r, gi, _, a, mult = _lru_gates(xc, wr_ref[g], br_ref[:, cols], wi_ref[g], bi_ref[:, cols],
                                              lam_ref[:, cols])
            sav_a_ref[:, cols] = a
            for slot, val in enumerate((xc, r, gi, mult)):
                sav_ref[slot, :, cols] = val.astype(BF16)
            hin = carry[:, cols]
            h, h_last = _chain_scan(a, mult * gi * xc, hin, down)
            h_ref[:, cols] = h
            if has_acc:
                hsum_ref[:, cols] = h + acc_ref[:, cols]
            hin_ref[0, :, cols] = hin
            carry[:, cols] = h_last
            hfin_ref[:, cols] = h_last

    vec = pl.BlockSpec((1, D), lambda i: (0, 0))
    wsp = pl.BlockSpec((NH, DH, DH), lambda i: (0, 0, 0))
    tok = pl.BlockSpec((blk, D), lambda i: (bmap(i), 0))
    in_specs = [tok, pl.BlockSpec((4, D), lambda i: (0, 0)), vec, wsp, vec, wsp, vec, vec, vec]
    args = [xin, cw, cb, wr, br, wi, bi, lam, h0]
    out_shape = [jax.ShapeDtypeStruct((n, D), F32), jax.ShapeDtypeStruct((nb, 1, D), F32),
                 jax.ShapeDtypeStruct((1, D), F32), jax.ShapeDtypeStruct((n, D), F32),
                 jax.ShapeDtypeStruct((LRU_SAVED, n, D), BF16)]
    out_specs = [tok, pl.BlockSpec((1, 1, D), lambda i: (bmap(i), 0, 0)), vec, tok,
                 pl.BlockSpec((LRU_SAVED, blk, D), lambda i: (0, bmap(i), 0))]
    if has_acc:
        in_specs.append(tok)
        args.append(acc_h)
        out_shape.append(jax.ShapeDtypeStruct((n, D), F32))
        out_specs.append(tok)
    return pl.pallas_call(
        body, name=name, grid=(nb,), out_shape=out_shape, in_specs=in_specs, out_specs=out_specs,
        scratch_shapes=[pltpu.VMEM((1, D), F32)],
    )(*args)


def _lru_bwd(xin, blk, cw, wr, wi, lam, sav, h, hin, dh, cg_fin, acc_dx, init, d, name):
    n = xin.shape[0]
    nb = n // blk
    reverse = d == 1
    down = not reverse
    first = blk - 1 if reverse else 0
    has_dh = dh is not None
    has_acc = acc_dx is not None
    has_init = init is not None

    def bmap(i):
        return i if reverse else nb - 1 - i

    def body(*refs):
        (x_ref, cw_ref, wr_ref, wi_ref, lam_ref, sav_a_ref, sav_ref, h_ref, hin_ref, cgf_ref) = refs[:10]
        pos = 10
        dh_ref = acc_ref = None
        iwr_ref = iwi_ref = ivec_ref = None
        if has_dh:
            dh_ref = refs[pos]
            pos += 1
        if has_acc:
            acc_ref = refs[pos]
            pos += 1
        if has_init:
            iwr_ref, iwi_ref, ivec_ref = refs[pos:pos + 3]
            pos += 3
        dx_ref, dwr_ref, dwi_ref, vec_ref, cg0_ref, carry = refs[pos:]
        i = pl.program_id(0)

        @pl.when(i == 0)
        def _():
            carry[...] = cgf_ref[...]
            if has_init:
                dwr_ref[...] = iwr_ref[...]
                dwi_ref[...] = iwi_ref[...]
                vec_ref[...] = ivec_ref[...]
            else:
                dwr_ref[...] = jnp.zeros_like(dwr_ref)
                dwi_ref[...] = jnp.zeros_like(dwi_ref)
                vec_ref[...] = jnp.zeros_like(vec_ref)

        for g in range(NH):
            cols = pl.ds(g * DH, DH)
            cwv = cw_ref[:, cols]
            lam_v = lam_ref[:, cols]
            taps = _conv_taps(x_ref[:, cols])
            wr_g, wi_g = wr_ref[g], wi_ref[g]
            a = sav_a_ref[:, cols]
            xcb = sav_ref[0, :, cols]
            xc, r, gi, mult = (sav_ref[slot, :, cols].astype(F32) for slot in range(LRU_SAVED))
            sp = jnp.maximum(-lam_v, 0.0) + jnp.log(1.0 + jnp.exp(-jnp.abs(lam_v)))
            hprev = _shift(h_ref[:, cols], 1, hin_ref[0, :, cols], down)
            a_next = _shift(a, 1, 1.0, not down)
            dhv = dh_ref[:, cols] if has_dh else jnp.zeros_like(a)
            e, _ = _chain_scan(a_next, dhv, carry[:, cols], not down)
            cg = a[first:first + 1, :] * e[first:first + 1, :]
            carry[:, cols] = cg
            cg0_ref[:, cols] = cg
            da = e * hprev
            emult = e * mult
            dgi = emult * xc
            dxc = emult * gi
            dla = da * a - (e * gi * xc) * (a * a) / mult
            dr = dla * (-RG_C * sp)
            sneg = 1.0 - _sigmoid(lam_v)
            dpr = dr * r * (1.0 - r)
            dpi = dgi * gi * (1.0 - gi)
            dprb, dpib = dpr.astype(BF16), dpi.astype(BF16)
            dxc = dxc + _dot_nt(dprb, wr_g) + _dot_nt(dpib, wi_g)
            dwr_ref[g] += _dot_tn(xcb, dprb)
            dwi_ref[g] += _dot_tn(xcb, dpib)
            dx = (cwv[0:1, :] * _shift(dxc, 1, 0.0, False) + cwv[1:2, :] * dxc
                  + cwv[2:3, :] * _shift(dxc, 1, 0.0, True) + cwv[3:4, :] * _shift(dxc, 2, 0.0, True))
            if has_acc:
                dx = dx + acc_ref[:, cols]
            dx_ref[:, cols] = dx
            vec_ref[0:1, cols] += jnp.sum(dpr, axis=0, keepdims=True)
            vec_ref[1:2, cols] += jnp.sum(dpi, axis=0, keepdims=True)
            vec_ref[2:3, cols] += jnp.sum(dla * r, axis=0, keepdims=True) * (RG_C * sneg)
            vec_ref[3:4, cols] += jnp.sum(dxc, axis=0, keepdims=True)
            for kk in range(4):
                vec_ref[4 + kk:5 + kk, cols] += jnp.sum(dxc * taps[kk], axis=0, keepdims=True)

    vec = pl.BlockSpec((1, D), lambda i: (0, 0))
    wsp = pl.BlockSpec((NH, DH, DH), lambda i: (0, 0, 0))
    tok = pl.BlockSpec((blk, D), lambda i: (bmap(i), 0))
    vec16 = pl.BlockSpec((16, D), lambda i: (0, 0))
    in_specs = [tok, pl.BlockSpec((4, D), lambda i: (0, 0)), wsp, wsp, vec, tok,
                pl.BlockSpec((LRU_SAVED, blk, D), lambda i: (0, bmap(i), 0)), tok,
                pl.BlockSpec((1, 1, D), lambda i: (bmap(i), 0, 0)), vec]
    args = [xin, cw, wr, wi, lam, sav[0], sav[1], h, hin, cg_fin]
    if has_dh:
        in_specs.append(tok)
        args.append(dh)
    if has_acc:
        in_specs.append(tok)
        args.append(acc_dx)
    if has_init:
        in_specs += [wsp, wsp, vec16]
        args += list(init)
    return pl.pallas_call(
        body, name=name, grid=(nb,),
        out_shape=[jax.ShapeDtypeStruct((n, D), F32), jax.ShapeDtypeStruct((NH, DH, DH), F32),
                   jax.ShapeDtypeStruct((NH, DH, DH), F32), jax.ShapeDtypeStruct((16, D), F32),
                   jax.ShapeDtypeStruct((1, D), F32)],
        in_specs=in_specs, out_specs=[tok, wsp, wsp, vec16, vec],
        scratch_shapes=[pltpu.VMEM((1, D), F32)],
    )(*args)


def _merge(z, o_f, o_b, hx, xin, tgt, mod, gn, p_a, p_b, w_out, ln_g, ln_b):
    n = xin.shape[0]
    tm = _row_tile(n, 128)

    def body(z4_ref, z6_ref, z7_ref, z8_ref, of_ref, ob_ref, hx_ref, x_ref, t_ref, mod_ref, gn_ref,
             pa_ref, pb_ref, wo_ref, lg_ref, lnb_ref,
             dr_ref, do_ref, dhx_ref, dz_ref, oa_o, obb_o, y_o, dya_o, dyb_o, dout_o, vec_ref):
        @pl.when(pl.program_id(0) == 0)
        def _():
            vec_ref[...] = jnp.zeros_like(vec_ref)

        gt = mod_ref[0:1, 2 * D:3 * D]
        gnv = gn_ref[...]
        o = of_ref[...] + ob_ref[...]
        rs = jnp.concatenate(
            [jnp.broadcast_to(lax.rsqrt(jnp.mean(jnp.square(o[:, h * DH:(h + 1) * DH]), axis=1, keepdims=True)
                                        + RMS_EPS), (tm, DH)) for h in range(NH)], axis=1)
        nrm = o * rs
        rn = nrm * gnv
        z4, z6, z7, z8 = z4_ref[...], z6_ref[...], z7_ref[...], z8_ref[...]
        s4, s6, s7, s8 = _sigmoid(z4), _sigmoid(z6), _sigmoid(z7), _sigmoid(z8)
        sg4, sg6 = z4 * s4, z6 * s6
        hxv = hx_ref[...]
        oa = (rn * sg4).astype(BF16)
        obb = (hxv * sg6).astype(BF16)
        ya = _dot(oa, pa_ref[...])
        yb = _dot(obb, pb_ref[...])
        y = (s7 * ya + s8 * yb).astype(BF16)
        out = _dot(y, wo_ref[...])
        xv = x_ref[...]
        rr = ALPHA * xv + gt * out
        mu = jnp.mean(rr, axis=1, keepdims=True)
        cen = rr - mu
        rstd = lax.rsqrt(jnp.mean(cen * cen, axis=1, keepdims=True) + LN_EPS)
        xhat = cen * rstd
        lg = lg_ref[...]
        err = xhat * lg + lnb_ref[...] - t_ref[...]
        loss_rows = jnp.sum(err * err, axis=1, keepdims=True)
        dxn = err * (1.0 / D)
        dxh = dxn * lg
        dr = rstd * (dxh - jnp.mean(dxh, axis=1, keepdims=True)
                     - xhat * jnp.mean(dxh * xhat, axis=1, keepdims=True))
        dout = (dr * gt).astype(BF16)
        dy = _dot_nt(dout, wo_ref[...])
        dya = (dy * s7).astype(BF16)
        dyb = (dy * s8).astype(BF16)
        doa = _dot_nt(dya, pa_ref[...])
        dob = _dot_nt(dyb, pb_ref[...])
        drn = doa * sg4
        dn = drn * gnv
        dnn = dn * nrm
        corr = jnp.concatenate(
            [jnp.broadcast_to(jnp.mean(dnn[:, h * DH:(h + 1) * DH], axis=1, keepdims=True), (tm, DH))
             for h in range(NH)], axis=1)
        dr_ref[...] = dr
        do_ref[...] = rs * (dn - nrm * corr)
        dhx_ref[...] = dob * sg6
        dz_ref[:, 0:4 * D] = jnp.zeros((tm, 4 * D), BF16)
        dz_ref[:, 4 * D:5 * D] = (doa * rn * _dsilu(z4, s4)).astype(BF16)
        dz_ref[:, 5 * D:6 * D] = jnp.zeros((tm, D), BF16)
        dz_ref[:, 6 * D:7 * D] = (dob * hxv * _dsilu(z6, s6)).astype(BF16)
        dz_ref[:, 7 * D:8 * D] = (dy * ya * s7 * (1.0 - s7)).astype(BF16)
        dz_ref[:, 8 * D:9 * D] = (dy * yb * s8 * (1.0 - s8)).astype(BF16)
        oa_o[...] = oa
        obb_o[...] = obb
        y_o[...] = y
        dya_o[...] = dya
        dyb_o[...] = dyb
        dout_o[...] = dout
        vec_ref[0:1, :] += jnp.sum(dr * out, axis=0, keepdims=True)
        vec_ref[1:2, :] += jnp.sum(dxn * xhat, axis=0, keepdims=True)
        vec_ref[2:3, :] += jnp.sum(dxn, axis=0, keepdims=True)
        vec_ref[3:4, :] += jnp.sum(drn * nrm, axis=0, keepdims=True)
        vec_ref[4:5, :] += jnp.broadcast_to(jnp.sum(loss_rows, axis=0, keepdims=True) * (0.5 / D), (1, D))

    def grp(g):
        return pl.BlockSpec((tm, D), lambda i: (i, g))

    tok = pl.BlockSpec((tm, D), lambda i: (i, 0))
    vec = pl.BlockSpec((1, D), lambda i: (0, 0))
    wsp = pl.BlockSpec((D, D), lambda i: (0, 0))
    return pl.pallas_call(
        body, name="merge", grid=(n // tm,),
        out_shape=[jax.ShapeDtypeStruct((n, D), F32)] * 3
        + [jax.ShapeDtypeStruct((n, NGRP * D), BF16)]
        + [jax.ShapeDtypeStruct((n, D), BF16)] * 6 + [jax.ShapeDtypeStruct((8, D), F32)],
        in_specs=[grp(4), grp(6), grp(7), grp(8), tok, tok, tok, tok, tok,
                  pl.BlockSpec((8, 3 * D), lambda i: (0, 0)), vec, wsp, wsp, wsp, vec, vec],
        out_specs=[tok, tok, tok, pl.BlockSpec((tm, NGRP * D), lambda i: (i, 0))] + [tok] * 6
        + [pl.BlockSpec((8, D), lambda i: (0, 0))],
    )(z, z, z, z, o_f, o_b, hx, xin, tgt, mod, gn, p_a, p_b, w_out, ln_g, ln_b)


def _wmod_grad(c_t, cctx_t, dmx_loc, dmc_loc, name):
    n = dmx_loc.shape[1]

    def body(ct_ref, cc_ref, dmx_ref, dmc_ref, o_ref):
        ct = ct_ref[...]
        sct = ct * _sigmoid(ct)
        cc = cc_ref[...]
        scc = cc * _sigmoid(cc)
        dmc = dmc_ref[0:1, :]
        for b in range(1, NDEV):
            dmc = dmc + dmc_ref[b:b + 1, :]
        acc = scc * dmc
        for b in range(NDEV):
            acc = acc + sct[:, b:b + 1] * dmx_ref[b:b + 1, :]
        o_ref[...] = acc

    return pl.pallas_call(body, name=name, out_shape=jax.ShapeDtypeStruct((D, n), F32))(c_t, cctx_t, dmx_loc, dmc_loc)


PACK_ROWS = 40


def _finalize_small(g_pack, lb, w_mod_full, params):
    npar = len(params)

    def body(*refs):
        gp_ref, lb_ref, wm_ref = refs[:3]
        wmv = refs[3:3 + 3 * npar]
        loss_ref = refs[3 + 3 * npar]
        g_refs = refs[4 + 3 * npar:4 + 4 * npar]
        upd = refs[4 + 4 * npar:4 + 7 * npar]
        tot = refs[-1]
        acc = gp_ref[0]
        for k in range(1, NDEV):
            acc = acc + gp_ref[k]
        tot[...] = acc
        mine = pl.ds(pl.multiple_of(_my_index() * DH, DH), DH)
        (g_cctx, g_bmod, g_bin, g_lbl, g_norm, g_cw, g_cb, g_br, g_bi, g_lam, g_lng, g_lnb) = g_refs

        loss_ref[...] = jnp.broadcast_to(tot[36:37, 0:DH], (8, DH))
        for k in range(3):
            g_bmod[:, k * D:(k + 1) * D] = tot[k:k + 1, :] + tot[3 + k:4 + k, :]
        dmc = jnp.concatenate([tot[3:4, :], tot[4:5, :], tot[5:6, :]], axis=1)
        cv = wmv[0][...]
        proj = _dot_nt(jnp.broadcast_to(dmc, (8, 3 * D)).astype(BF16), wm_ref[...])
        g_cctx[...] = proj[0:1, :] * _dsilu(cv, _sigmoid(cv))
        for k in range(NGRP):
            g_bin[:, k * D:(k + 1) * D] = tot[6 + k:7 + k, :]
        nrm = tot[15:16, 0:DH]
        for h in range(1, NH):
            nrm = nrm + tot[15:16, h * DH:(h + 1) * DH]
        g_norm[...] = nrm
        g_lng[...] = tot[16:17, :]
        g_lnb[...] = tot[17:18, :]
        g_cb[...] = tot[21:22, :] + tot[29:30, :]
        g_cw[0] = tot[22:26, mine] + tot[30:34, mine]
        for ref, row in ((g_br, 18), (g_bi, 19), (g_lam, 20)):
            ref[0, 0:1, :] = tot[row:row + 1, mine]
            ref[0, 1:2, :] = tot[row + 8:row + 9, mine]
        lbl = lb_ref[0:2, mine]
        dl0 = tot[34:36, mine] * lbl * (1.0 - lbl)
        g_lbl[0] = dl0
        g_lbl[1] = -dl0
        for p in range(npar):
            d, mm, vv = _adam_math(g_refs[p][...], wmv[3 * p][...], wmv[3 * p + 1][...], wmv[3 * p + 2][...])
            upd[3 * p][...] = d
            upd[3 * p + 1][...] = mm
            upd[3 * p + 2][...] = vv

    flat = [t for wmv in params for t in wmv]
    shapes = [jax.ShapeDtypeStruct(wmv[0].shape, F32) for wmv in params]
    outs = pl.pallas_call(
        body, name="finalize_small",
        out_shape=[jax.ShapeDtypeStruct((8, DH), F32)] + shapes + [s for s in shapes for _ in range(3)],
        scratch_shapes=[pltpu.VMEM((PACK_ROWS, D), F32)],
    )(g_pack, lb, w_mod_full, *flat)
    grads = list(outs[1:1 + npar])
    upd = [tuple(outs[1 + npar + 3 * p:4 + npar + 3 * p]) for p in range(npar)]
    return outs[0], grads, upd


def _to_colmajor(t, rows):
    return t.reshape(rows, GRID_W, D).transpose(1, 0, 2).reshape(rows * GRID_W, D)


def _to_raster(t, rows):
    return t.reshape(GRID_W, rows, D).transpose(1, 0, 2).reshape(rows * GRID_W, D)


def _local_cols(t, me, width):
    return lax.dynamic_slice_in_dim(t, me * width, width, axis=t.ndim - 1)


def kernel(x, c, ctx, c_ctx, w_mod, b_mod, w_in, b_in, lb_logits, norm_a_g, conv_w, conv_b, w_r, b_r, w_i, b_i, lam, p_a, p_b, w_out, ln_g, ln_b, loss_target, m_c_ctx, m_w_mod, m_b_mod, m_w_in, m_b_in, m_lb_logits, m_norm_a_g, m_conv_w, m_conv_b, m_w_r, m_b_r, m_w_i, m_b_i, m_lam, m_p_a, m_p_b, m_w_out, m_ln_g, m_ln_b, v_c_ctx, v_w_mod, v_b_mod, v_w_in, v_b_in, v_lb_logits, v_norm_a_g, v_conv_w, v_conv_b, v_w_r, v_b_r, v_w_i, v_b_i, v_lam, v_p_a, v_p_b, v_w_out, v_ln_g, v_ln_b):
    me = _my_index()
    xs, cs, tgt = x[0], ctx[0], loss_target[0]
    t_len, c_len = xs.shape[0], cs.shape[0]
    rows = t_len // GRID_W
    wcols = w_in.shape[2]
    mcols = w_mod.shape[2]

    small = jnp.concatenate([lb_logits.reshape(4, DH), conv_w[0], b_r[0], b_i[0], lam[0], jnp.zeros((2, DH), F32),
                             c.reshape(8, DH)], axis=0)
    g_small, g_wmod = _all_gather([small, w_mod[0].astype(BF16)], "gather_params")

    def full_rows(lo, hi):
        return g_small[:, lo:hi, :].transpose(1, 0, 2).reshape(hi - lo, D)

    lbl_f, cw_f, br_f, bi_f, lam_f = full_rows(0, 4), full_rows(4, 8), full_rows(8, 10), full_rows(10, 12), full_rows(12, 14)
    c_all = g_small[:, 16:24, :].reshape(NDEV, D)
    w_mod_f = g_wmod.transpose(1, 0, 2).reshape(D, 3 * D)
    w_r_b, w_i_b = w_r[0].astype(BF16), w_i[0].astype(BF16)

    cc = jnp.concatenate([c.reshape(1, D), c_ctx.reshape(1, D), jnp.zeros((6, D), F32)], axis=0)
    lbl_p = jnp.concatenate([lbl_f.reshape(2, 2, D), jnp.zeros((2, 6, D), F32)], axis=1)
    mod, lb = _prep(cc, w_mod_f, b_mod, lbl_p)
    u_x = _modulate(xs, mod, 0, "modulate_x")
    u_c = _modulate(cs, mod, 1, "modulate_c")
    z_x, w_in_f = _inproj_gather(u_x, w_in[0].astype(BF16), b_in, "inproj_gather")
    z_c = _mm_bias(u_c, w_in_f, b_in, "inproj_c")

    zero_s = jnp.zeros((NH, DH, DH), F32)
    zero_v = jnp.zeros((1, D), F32)
    gla = {}
    out_w = [p_a[0].astype(BF16), p_b[0].astype(BF16), w_out[0].astype(BF16)]
    out_w_f = []
    for d in (0, 1):
        _, ssc, sfc = _gla_fwd(z_c, lb, zero_s, d, f"gla_fwd_c{d}")
        o_d, ssx, _, *gathered = _gla_fwd(z_x, lb, sfc, d, f"gla_fwd_x{d}", side=out_w[1:] if d else out_w[:1])
        out_w_f += [t.reshape(D, D) for t in gathered]
        gla[d] = (ssc, ssx, o_d)
    p_a_f, p_b_f, w_out_f = out_w_f

    x5_c = z_c[:, 5 * D:6 * D]
    x5_x = _to_colmajor(z_x[:, 5 * D:6 * D], rows)
    cb2 = conv_b.reshape(1, D)
    lru = {}
    h_sum = None
    for d in (0, 1):
        prm = (cw_f, cb2, w_r_b[d], br_f[d:d + 1], w_i_b[d], bi_f[d:d + 1], lam_f[d:d + 1])
        h_c, hin_c, hfin_c, *sav_c = _lru_fwd(x5_c, c_len, *prm, zero_v, None, d, f"lru_fwd_c{d}")
        h_x, hin_x, _, sav_a, sav_h, *h_sum = _lru_fwd(x5_x, rows, *prm, hfin_c, lru[0][3] if d else None, d,
                                                       f"lru_fwd_x{d}")
        lru[d] = ((cw_f, w_r_b[d], w_i_b[d], lam_f[d:d + 1]), h_c, hin_c, h_x, hin_x, tuple(sav_c), (sav_a, sav_h))
    hx = _to_raster(h_sum[0], rows)

    gn = jnp.tile(norm_a_g.reshape(1, DH), (1, NH))
    (dr, do, dhx, dz_m, oa, obb, yb16, dya, dyb, dout, mvec) = _merge(
        z_x, gla[0][2], gla[1][2], hx, xs, tgt, mod, gn, p_a_f, p_b_f, w_out_f, ln_g, ln_b)

    dhx_cm = _to_colmajor(dhx, rows)
    lru_dx_x = lru_dx_c = None
    for d in (0, 1):
        prm, h_c, hin_c, h_x, hin_x, sav_c, sav_x = lru[d]
        lru_dx_x, dwr, dwi, lvec, cg0 = _lru_bwd(x5_x, rows, *prm, sav_x, h_x, hin_x, dhx_cm, zero_v, lru_dx_x, None,
                                                 d, f"lru_bwd_x{d}")
        lru_dx_c, dwr, dwi, lvec, _ = _lru_bwd(x5_c, c_len, *prm, sav_c, h_c, hin_c, None, cg0, lru_dx_c,
                                               (dwr, dwi, lvec), d, f"lru_bwd_c{d}")
        lru[d] = (dwr, dwi, lvec)
    dz5_x = _to_raster(lru_dx_x, rows).astype(BF16)
    dz5_c = lru_dx_c.astype(BF16)

    dpa = _mm_tn(oa, dya, None, "dpa", out_dtype=BF16)
    dpb = _mm_tn(obb, dyb, None, "dpb", out_dtype=BF16)
    dwo = _mm_tn(yb16, dout, None, "dwout", out_dtype=BF16)
    wr_pack = jnp.concatenate([lru[0][0], lru[1][0], lru[0][1], lru[1][1]], axis=0).reshape(4 * NH * DH, DH)

    gq_c = gv_c = None
    dzf_c, dlb = {}, {}
    gq_x, dzf_x0, gv_x, dlb_x, ds0, r_pa, r_pb, r_wo, r_wri = _gla_bwd(
        z_x, lb, gla[0][1], do, zero_s, None, None, 0, "gla_bwd_x0", f_dtype=BF16,
        side=[dpa, dpb, dwo, wr_pack], side_splits=[0, 0, 0, 0])
    gq_c, dzf_c[0], gv_c, dlb_c, _ = _gla_bwd(z_c, lb, gla[0][0], None, ds0, None, None, 0, "gla_bwd_c0")
    dlb[0] = dlb_x[0:1] + dlb_c[0:1]
    dz_g, dlb_x, ds0 = _gla_bwd(z_x, lb, gla[1][1], do, zero_s, gq_x, gv_x, 1, "gla_bwd_x1", into=(dz_m, dzf_x0))
    gq_c, dzf_c[1], gv_c, dlb_c, _ = _gla_bwd(z_c, lb, gla[1][0], None, ds0, gq_c, gv_c, 1, "gla_bwd_c1")
    dlb[1] = dlb_x[0:1] + dlb_c[0:1]

    bf = lambda t: t.astype(BF16)
    dz_x = lax.dynamic_update_slice(dz_g, dz5_x, (0, 5 * D))
    zc0 = jnp.zeros((c_len, D), BF16)
    dz_c = jnp.concatenate([bf(gq_c), bf(dzf_c[0]), bf(dzf_c[1]), bf(gv_c), zc0, dz5_c, zc0, zc0, zc0], axis=1)
    dwin_c, dbin_c = _mm_tn(u_c, dz_c, None, "dwin_c", with_colsum=True)

    grad_x, xvec = _input_grad(dz_x, w_in_f, xs, dr, mod, 0, "input_grad_x")
    r_win, dbin = _dwin_exchange(u_x, dz_x, dwin_c, dbin_c, "dwin_exchange")
    _, cvec = _input_grad(dz_c, w_in_f, cs, None, mod, 1, "input_grad_c")
    wri_piece = _sum_rows(r_wri, "sum_w_ri_piece")
    g_w_in, d_w_in, nm_w_in, nv_w_in = _sum_adamw(r_win, w_in, m_w_in, v_w_in, "update_w_in")
    g_p_a, d_p_a, nm_p_a, nv_p_a = _sum_adamw(r_pa, p_a, m_p_a, v_p_a, "update_p_a")
    g_p_b, d_p_b, nm_p_b, nv_p_b = _sum_adamw(r_pb, p_b, m_p_b, v_p_b, "update_p_b")
    g_w_out, d_w_out, nm_w_out, nv_w_out = _sum_adamw(r_wo, w_out, m_w_out, v_w_out, "update_w_out")

    dlb_rows = jnp.concatenate([dlb[0], dlb[1]], axis=0)
    pack = jnp.concatenate([
        xvec[0:1], xvec[1:2], mvec[0:1],
        cvec[0:1], cvec[1:2], jnp.zeros((1, D), F32),
        dbin.reshape(NGRP, D),
        mvec[3:4], mvec[1:2], mvec[2:3],
        lru[0][2][0:8], lru[1][2][0:3],
        lru[1][2][3:8],
        dlb_rows,
        mvec[4:5],
        jnp.zeros((3, D), F32)], axis=0)
    g_pack, g_wri = _all_gather([pack, wri_piece], "gather_small_grads")

    dmx = g_pack[:, 0:3, :].reshape(NDEV, 3 * D)
    dmc = g_pack[:, 3:6, :].reshape(NDEV, 3 * D)
    grad_w_mod = _wmod_grad(c_all.T, c_ctx.reshape(D, 1), _local_cols(dmx, me, mcols), _local_cols(dmc, me, mcols),
                            "grad_w_mod").reshape(1, D, mcols)
    small_params = [(c_ctx.reshape(1, D), m_c_ctx.reshape(1, D), v_c_ctx.reshape(1, D)), (b_mod, m_b_mod, v_b_mod),
                    (b_in, m_b_in, v_b_in), (lb_logits, m_lb_logits, v_lb_logits), (norm_a_g, m_norm_a_g, v_norm_a_g),
                    (conv_w, m_conv_w, v_conv_w), (conv_b, m_conv_b, v_conv_b), (b_r, m_b_r, v_b_r),
                    (b_i, m_b_i, v_b_i), (lam, m_lam, v_lam), (ln_g, m_ln_g, v_ln_g), (ln_b, m_ln_b, v_ln_b)]
    loss_tile, small_g, small_upd = _finalize_small(g_pack, lb, w_mod_f, small_params)
    loss = loss_tile[0, 0]
    (grad_c_ctx, grad_b_mod, grad_b_in, grad_lb_logits, grad_norm_a_g, grad_conv_w, grad_conv_b, grad_b_r, grad_b_i,
     grad_lam, grad_ln_g, grad_ln_b) = small_g
    small_upd[0] = tuple(t.reshape(c_ctx.shape) for t in small_upd[0])
    (o_c_ctx, o_b_mod, o_b_in, o_lb, o_norm, o_conv_w, o_conv_b, o_b_r, o_b_i, o_lam, o_ln_g, o_ln_b) = small_upd

    half = 2 * NH * DH
    g_ri = g_wri.reshape(2 * half, DH)
    grad_w_r, grad_w_i = g_ri[:half].reshape(w_r.shape), g_ri[half:].reshape(w_i.shape)
    d_w_r, nm_w_r, nv_w_r = _adamw(grad_w_r, w_r, m_w_r, v_w_r, "update_w_r")
    d_w_i, nm_w_i, nv_w_i = _adamw(grad_w_i, w_i, m_w_i, v_w_i, "update_w_i")

    d_w_mod, nm_w_mod, nv_w_mod = _adamw(grad_w_mod, w_mod, m_w_mod, v_w_mod, "update_w_mod")

    grads = [grad_c_ctx.reshape(c_ctx.shape), grad_w_mod, grad_b_mod, g_w_in, grad_b_in, grad_lb_logits, grad_norm_a_g,
             grad_conv_w, grad_conv_b, grad_w_r, grad_b_r, grad_w_i, grad_b_i, grad_lam, g_p_a, g_p_b, g_w_out,
             grad_ln_g, grad_ln_b]
    per_kind = []
    for k in range(3):
        per_kind.append([
            o_c_ctx[k], (d_w_mod, nm_w_mod, nv_w_mod)[k], o_b_mod[k], (d_w_in, nm_w_in, nv_w_in)[k], o_b_in[k], o_lb[k],
            o_norm[k], o_conv_w[k], o_conv_b[k], (d_w_r, nm_w_r, nv_w_r)[k], o_b_r[k], (d_w_i, nm_w_i, nv_w_i)[k],
            o_b_i[k], o_lam[k], (d_p_a, nm_p_a, nv_p_a)[k], (d_p_b, nm_p_b, nv_p_b)[k], (d_w_out, nm_w_out, nv_w_out)[k],
            o_ln_g[k], o_ln_b[k]])
    return (loss, grad_x.reshape(x.shape), *grads, *per_kind[0], *per_kind[1], *per_kind[2])
```

```python
import functools

import jax
import jax.numpy as jnp
from jax import lax
from jax.experimental import pallas as pl
from jax.experimental.pallas import tpu as pltpu

F32 = jnp.float32
BF16 = jnp.bfloat16

D = 1024
NH = 8
DH = 128
CHUNK = 64
GLA_HEADS_PER_STEP = 8
GRID_W = 64
NGRP = 9
NDEV = 8
RG_C = 8.0
ALPHA = 2.0 ** 0.25
LN_EPS = 1e-5
RMS_EPS = 1e-6
Q_SCALE = DH ** -0.5
ADAM_LR, ADAM_B1, ADAM_B2, ADAM_EPS, ADAM_WD, ADAM_STEP = 1e-3, 0.9, 0.999, 1e-8, 0.01, 10
ADAM_C1 = 1.0 / (1.0 - ADAM_B1 ** ADAM_STEP)
ADAM_C2 = 1.0 / (1.0 - ADAM_B2 ** ADAM_STEP)

ANY = pl.BlockSpec(memory_space=pl.ANY)


def _sigmoid(t):
    return 1.0 / (1.0 + jnp.exp(-t))


def _dsilu(t, s):
    return s * (1.0 + t * (1.0 - s))


def _dot(a, b):
    return jnp.dot(a, b, preferred_element_type=F32)


def _dot_nt(a, b):
    return lax.dot_general(a, b, (((1,), (1,)), ((), ())), preferred_element_type=F32)


def _dot_tn(a, b):
    return lax.dot_general(a, b, (((0,), (0,)), ((), ())), preferred_element_type=F32)


def _my_index():
    return 4 * lax.axis_index("x") + 2 * lax.axis_index("y") + lax.axis_index("c")


def _dev_tuple(j):
    return (j >> 2, (j >> 1) & 1, j & 1)


def _exchange_sems(n):
    return [pltpu.SemaphoreType.DMA((n * NDEV,)), pltpu.SemaphoreType.DMA((n * NDEV,)), pltpu.SemaphoreType.DMA((n,))]


def _exchange(ins, outs, sems, piece_of=None):
    send_sems, recv_sems, loc_sems = sems
    n = len(ins)

    def src(a, p):
        return ins[a] if piece_of is None else piece_of(ins[a], a, p)

    def push(a, t):
        me, p = _my_index(), _step_peer(t)
        return pltpu.make_async_remote_copy(
            src_ref=src(a, p), dst_ref=outs[a].at[me],
            send_sem=send_sems.at[a * NDEV + t], recv_sem=recv_sems.at[a * NDEV + me],
            device_id=_dev_of(p), device_id_type=pl.DeviceIdType.MESH)

    def local(a):
        me = _my_index()
        return pltpu.make_async_copy(src(a, me), outs[a].at[me], loc_sems.at[a])

    def start():
        for a in range(n):
            local(a).start()
        for t in range(NDEV - 1):
            for a in range(n):
                push(a, t).start()

    def finish():
        me = _my_index()
        for t in range(NDEV - 1):
            for a in range(n):
                push(a, t).wait_send()
        for j in range(NDEV):
            @pl.when(me != j)
            def _():
                for a in range(n):
                    pltpu.make_async_remote_copy(
                        src_ref=src(a, j), dst_ref=outs[a].at[j],
                        send_sem=send_sems.at[a * NDEV], recv_sem=recv_sems.at[a * NDEV + j],
                        device_id=_dev_tuple(j), device_id_type=pl.DeviceIdType.MESH).wait_recv()
        for a in range(n):
            local(a).wait()

    return start, finish


def _all_gather(shards, name):
    n = len(shards)

    def body(*refs):
        start, finish = _exchange(refs[:n], refs[n:2 * n], refs[2 * n:])
        start()
        finish()

    return pl.pallas_call(
        body, name=name,
        out_shape=[jax.ShapeDtypeStruct((NDEV,) + s.shape, s.dtype) for s in shards],
        in_specs=[ANY] * n, out_specs=[ANY] * n, scratch_shapes=_exchange_sems(n),
    )(*shards)


def _pieces(parts, splits):
    shapes = []
    for part, split in zip(parts, splits):
        r, c = part.shape
        shapes.append((r // NDEV, c) if split == 0 else (r, c // NDEV))

    def piece_of(ref, a, j):
        pr, pc = shapes[a]
        if splits[a] == 0:
            start = j * pr if isinstance(j, int) else pl.multiple_of(j * pr, pr)
            return ref.at[pl.ds(start, pr), :]
        start = j * pc if isinstance(j, int) else pl.multiple_of(j * pc, pc)
        return ref.at[:, pl.ds(start, pc)]

    return shapes, piece_of


_STEP_MASKS = ((2, 4, 6, 3, 5, 7, 1, 0), (4, 2, 6, 5, 3, 7, 1, 0))
_GATHER_MASKS = ((0, 1, 2, 4, 3, 5, 6, 7), (0, 1, 4, 2, 5, 3, 6, 7))


def _peer_schedule(table):
    tab = jnp.array(table, jnp.int32)
    return jnp.bitwise_xor(_my_index(), tab[lax.axis_index("c")])


def _step_peer(s, table=_STEP_MASKS):
    def pick(row):
        if isinstance(s, int):
            return jnp.int32(row[s])
        m = jnp.int32(row[NDEV - 1])
        for t in range(NDEV - 2, -1, -1):
            m = jnp.where(s == t, jnp.int32(row[t]), m)
        return m
    mask = jnp.where(lax.axis_index("c") == 0, pick(table[0]), pick(table[1]))
    return jnp.bitwise_xor(_my_index(), mask)


def _dev_of(p):
    return (p // 4, (p // 2) % 2, p % 2)


def _dwin_exchange(u, dz, init, cs_init, name):
    m, ka = u.shape
    n = dz.shape[1]
    pc = n // NDEV
    tk = _row_tile(m, 512)
    nk = m // tk

    def body(pidx_ref, u_ref, dz_ref, init_ref, csi_ref, rwin, cs_ref, acc, sbuf, wsend, wrecv, wloc):
        s, k = pl.program_id(0), pl.program_id(1)
        me = _my_index()

        def slab_copy(slot, p):
            return pltpu.make_async_remote_copy(
                src_ref=sbuf.at[slot], dst_ref=rwin.at[me], send_sem=wsend.at[slot], recv_sem=wrecv.at[me],
                device_id=_dev_of(p), device_id_type=pl.DeviceIdType.MESH)

        @pl.when(k == 0)
        def _():
            acc[...] = init_ref[...]
            cs_ref[...] = csi_ref[...]

        bv = dz_ref[...]
        acc[...] += _dot_tn(u_ref[...], bv)
        cs_ref[...] += jnp.sum(bv.astype(F32), axis=0, keepdims=True)

        @pl.when(k == nk - 1)
        def _():
            slot = s % 2

            @pl.when(s >= 2)
            def _():
                slab_copy(slot, me).wait_send()

            sbuf[slot] = acc[...].astype(BF16)

            @pl.when(s < NDEV - 1)
            def _():
                slab_copy(slot, _step_peer(s)).start()

            @pl.when(s == NDEV - 1)
            def _():
                own = pltpu.make_async_copy(sbuf.at[slot], rwin.at[me], wloc.at[0])
                own.start()
                slab_copy(1 - slot, me).wait_send()
                for j in range(NDEV):
                    @pl.when(me != j)
                    def _():
                        pltpu.make_async_remote_copy(
                            src_ref=sbuf.at[0], dst_ref=rwin.at[j], send_sem=wsend.at[0], recv_sem=wrecv.at[j],
                            device_id=_dev_tuple(j), device_id_type=pl.DeviceIdType.MESH).wait_recv()
                own.wait()

    grid_spec = pltpu.PrefetchScalarGridSpec(
        num_scalar_prefetch=1, grid=(NDEV, nk),
        in_specs=[pl.BlockSpec((tk, ka), lambda s, k, pidx: (k, 0)),
                  pl.BlockSpec((tk, pc), lambda s, k, pidx: (k, pidx[s])),
                  pl.BlockSpec((ka, pc), lambda s, k, pidx: (0, pidx[s])),
                  pl.BlockSpec((1, pc), lambda s, k, pidx: (0, pidx[s]))],
        out_specs=[ANY, pl.BlockSpec((1, pc), lambda s, k, pidx: (0, pidx[s]))],
        scratch_shapes=[pltpu.VMEM((ka, pc), F32), pltpu.VMEM((2, ka, pc), BF16),
                        pltpu.SemaphoreType.DMA((2,)), pltpu.SemaphoreType.DMA((NDEV,)), pltpu.SemaphoreType.DMA((1,))])
    return pl.pallas_call(
        body, name=name, grid_spec=grid_spec,
        out_shape=[jax.ShapeDtypeStruct((NDEV, ka, pc), BF16), jax.ShapeDtypeStruct((1, n), F32)],
    )(_peer_schedule(_STEP_MASKS), u, dz, init, cs_init)


def _inproj_gather(u, w_loc, bias, name):
    m, k = u.shape
    pc = w_loc.shape[1]
    n = pc * NDEV
    tm = _row_tile(m, 512)
    ni = m // tm
    direct = (1, 2, 3, 6)
    relay_sem = {2: 4, 3: 5, 6: 7}

    def body(pidx_ref, u_ref, b_ref, wl_ref, z_ref, wall, wbuf, wsend, wrecv, ldsem, ownsem):
        s, i = pl.program_id(0), pl.program_id(1)
        me = _my_index()

        def shard_push(t):
            return pltpu.make_async_remote_copy(
                src_ref=wl_ref, dst_ref=wall.at[me], send_sem=wsend.at[t], recv_sem=wrecv.at[me],
                device_id=_dev_of(_step_peer(t, _GATHER_MASKS)), device_id_type=pl.DeviceIdType.MESH)

        def relay(t):
            p = _step_peer(t, _GATHER_MASKS)
            return pltpu.make_async_remote_copy(
                src_ref=wall.at[p], dst_ref=wall.at[p], send_sem=wsend.at[relay_sem[t]], recv_sem=wrecv.at[p],
                device_id=_dev_of(_step_peer(1, _GATHER_MASKS)), device_id_type=pl.DeviceIdType.MESH)

        def load(slot, src):
            return pltpu.make_async_copy(src, wbuf.at[slot], ldsem.at[slot])

        own = pltpu.make_async_copy(wl_ref, wall.at[me], ownsem.at[0])

        @pl.when((s == 0) & (i == 0))
        def _():
            own.start()
            load(0, wl_ref).start()
            for t in direct:
                shard_push(t).start()

        @pl.when((i == ni // 2) & (s < NDEV - 1))
        def _():
            nxt = _step_peer(s + 1, _GATHER_MASKS)
            pltpu.make_async_remote_copy(
                src_ref=wl_ref, dst_ref=wall.at[nxt], send_sem=wsend.at[0], recv_sem=wrecv.at[nxt],
                device_id=_dev_of(nxt), device_id_type=pl.DeviceIdType.MESH).wait_recv()
            for t in relay_sem:
                @pl.when(s + 1 == t)
                def _():
                    relay(t).start()
            load((s + 1) % 2, wall.at[nxt]).start()

        @pl.when(i == 0)
        def _():
            load(s % 2, wl_ref).wait()

        z_ref[...] = _dot(u_ref[...], wbuf[s % 2]) + b_ref[...]

        @pl.when((s == NDEV - 1) & (i == ni - 1))
        def _():
            own.wait()
            for t in direct:
                shard_push(t).wait_send()
            for t in relay_sem:
                relay(t).wait_send()

    grid_spec = pltpu.PrefetchScalarGridSpec(
        num_scalar_prefetch=1, grid=(NDEV, ni),
        in_specs=[pl.BlockSpec((tm, k), lambda s, i, pidx: (i, 0)),
                  pl.BlockSpec((1, pc), lambda s, i, pidx: (0, pidx[s])), ANY],
        out_specs=[pl.BlockSpec((tm, pc), lambda s, i, pidx: (i, pidx[s])), ANY],
        scratch_shapes=[pltpu.VMEM((2, k, pc), BF16),
                        pltpu.SemaphoreType.DMA((NDEV,)), pltpu.SemaphoreType.DMA((NDEV,)),
                        pltpu.SemaphoreType.DMA((2,)), pltpu.SemaphoreType.DMA((1,))])
    return pl.pallas_call(
        body, name=name, grid_spec=grid_spec,
        out_shape=[jax.ShapeDtypeStruct((m, n), F32), jax.ShapeDtypeStruct((NDEV, k, pc), w_loc.dtype)],
    )(_peer_schedule(_GATHER_MASKS), u, bias, w_loc)


def _adam_math(g, w, m, v):
    m2 = ADAM_B1 * m + (1.0 - ADAM_B1) * g
    v2 = ADAM_B2 * v + (1.0 - ADAM_B2) * (g * g)
    delta = -ADAM_LR * ((m2 * ADAM_C1) / (jnp.sqrt(v2 * ADAM_C2) + ADAM_EPS) + ADAM_WD * w)
    return delta, m2, v2


def _row_tile(r, cap):
    t = min(r, cap)
    while r % t:
        t //= 2
    return t


def _adamw(g, w, m, v, name):
    shape = w.shape
    cols = shape[-1] if w.ndim >= 2 and shape[-1] % 128 == 0 else 128
    g2, w2, m2, v2 = (t.reshape(-1, cols) for t in (g, w, m, v))
    r = g2.shape[0]
    tr = _row_tile(r, 256)

    def body(g_ref, w_ref, m_ref, v_ref, d_ref, mo_ref, vo_ref):
        d, mm, vv = _adam_math(g_ref[...], w_ref[...], m_ref[...], v_ref[...])
        d_ref[...] = d
        mo_ref[...] = mm
        vo_ref[...] = vv

    spec = pl.BlockSpec((tr, cols), lambda i: (i, 0))
    outs = pl.pallas_call(
        body, name=name, grid=(r // tr,),
        out_shape=[jax.ShapeDtypeStruct((r, cols), F32)] * 3,
        in_specs=[spec] * 4, out_specs=[spec] * 3,
    )(g2, w2, m2, v2)
    return tuple(o.reshape(shape) for o in outs)


def _sum_adamw(parts, w, m, v, name):
    _, r, c = parts.shape
    shape = w.shape
    w2, m2, v2 = (t.reshape(r, c) for t in (w, m, v))
    tr = _row_tile(r, 128)

    def body(p_ref, w_ref, m_ref, v_ref, g_ref, d_ref, mo_ref, vo_ref):
        g = p_ref[0].astype(F32)
        for k in range(1, NDEV):
            g = g + p_ref[k].astype(F32)
        d, mm, vv = _adam_math(g, w_ref[...], m_ref[...], v_ref[...])
        g_ref[...] = g
        d_ref[...] = d
        mo_ref[...] = mm
        vo_ref[...] = vv

    spec = pl.BlockSpec((tr, c), lambda i: (i, 0))
    outs = pl.pallas_call(
        body, name=name, grid=(r // tr,),
        out_shape=[jax.ShapeDtypeStruct((r, c), F32)] * 4,
        in_specs=[pl.BlockSpec((NDEV, tr, c), lambda i: (0, i, 0))] + [spec] * 3, out_specs=[spec] * 4,
    )(parts, w2, m2, v2)
    return tuple(o.reshape(shape) for o in outs)


def _sum_rows(parts, name):
    _, r, c = parts.shape

    def body(p_ref, o_ref):
        g = p_ref[0]
        for k in range(1, NDEV):
            g = g + p_ref[k]
        o_ref[...] = g

    return pl.pallas_call(
        body, name=name, out_shape=jax.ShapeDtypeStruct((r, c), F32),
    )(parts)


def _cast_bf16(arrays):
    n = len(arrays)

    def body(*refs):
        for src, dst in zip(refs[:n], refs[n:]):
            dst[...] = src[...].astype(BF16)

    return pl.pallas_call(
        body, name="cast_weights", out_shape=[jax.ShapeDtypeStruct(a.shape, BF16) for a in arrays],
    )(*arrays)


def _prep(cc, w_mod_full, b_mod, lbl):
    def body(cc_ref, w_ref, b_ref, l_ref, mod_ref, lb_ref):
        t = cc_ref[...]
        s = (t * _sigmoid(t)).astype(BF16)
        mod_ref[...] = _dot(s, w_ref[...]) + b_ref[...]
        lb_ref[...] = _sigmoid(l_ref[0] - l_ref[1])

    return pl.pallas_call(
        body, name="prep",
        out_shape=[jax.ShapeDtypeStruct((8, 3 * D), F32), jax.ShapeDtypeStruct((8, D), F32)],
    )(cc, w_mod_full, b_mod, lbl)


def _modulate(xin, mod, row, name):
    n = xin.shape[0]
    tm = _row_tile(n, 512)

    def body(x_ref, mod_ref, u_ref):
        sh = mod_ref[row:row + 1, 0:D]
        sc = mod_ref[row:row + 1, D:2 * D]
        u_ref[...] = (x_ref[...] * (1.0 + sc) + sh).astype(BF16)

    return pl.pallas_call(
        body, name=name, grid=(n // tm,),
        out_shape=jax.ShapeDtypeStruct((n, D), BF16),
        in_specs=[pl.BlockSpec((tm, D), lambda i: (i, 0)), pl.BlockSpec((8, 3 * D), lambda i: (0, 0))],
        out_specs=pl.BlockSpec((tm, D), lambda i: (i, 0)),
    )(xin, mod)


def _mm_bias(a, w_all, bias, name):
    m, k = a.shape
    tn = w_all.shape[2]
    n = tn * NDEV
    tm = _row_tile(m, 512)

    def body(a_ref, b_ref, bias_ref, o_ref):
        o_ref[...] = _dot(a_ref[...], b_ref[0]) + bias_ref[...]

    return pl.pallas_call(
        body, name=name, grid=(NDEV, m // tm),
        out_shape=jax.ShapeDtypeStruct((m, n), F32),
        in_specs=[pl.BlockSpec((tm, k), lambda j, i: (i, 0)), pl.BlockSpec((1, k, tn), lambda j, i: (j, 0, 0)),
                  pl.BlockSpec((1, tn), lambda j, i: (0, j))],
        out_specs=pl.BlockSpec((tm, tn), lambda j, i: (i, j)),
    )(a, w_all, bias)


def _mm_tn(a, b, init, name, with_colsum=False, colsum_init=None, out_dtype=F32):
    m, ka = a.shape
    n = b.shape[1]
    tk = _row_tile(m, 512)
    tn = 1024
    nk = m // tk
    has_init = init is not None

    def body(*refs):
        a_ref, b_ref = refs[0], refs[1]
        pos = 2
        init_ref = cs_init_ref = None
        if has_init:
            init_ref = refs[pos]
            pos += 1
            if with_colsum:
                cs_init_ref = refs[pos]
                pos += 1
        o_ref = refs[pos]
        cs_ref = refs[pos + 1] if with_colsum else None
        acc = refs[-1]
        k = pl.program_id(1)

        @pl.when(k == 0)
        def _():
            if has_init:
                acc[...] = init_ref[...]
                if with_colsum:
                    cs_ref[...] = cs_init_ref[...]
            else:
                acc[...] = jnp.zeros_like(acc)
                if with_colsum:
                    cs_ref[...] = jnp.zeros_like(cs_ref)

        bv = b_ref[...]
        acc[...] += _dot_tn(a_ref[...], bv)
        if with_colsum:
            cs_ref[...] += jnp.sum(bv.astype(F32), axis=0, keepdims=True)

        @pl.when(k == nk - 1)
        def _():
            o_ref[...] = acc[...].astype(out_dtype)

    in_specs = [pl.BlockSpec((tk, ka), lambda j, k: (k, 0)), pl.BlockSpec((tk, tn), lambda j, k: (k, j))]
    args = [a, b]
    if has_init:
        in_specs.append(pl.BlockSpec((ka, tn), lambda j, k: (0, j)))
        args.append(init)
        if with_colsum:
            in_specs.append(pl.BlockSpec((1, tn), lambda j, k: (0, j)))
            args.append(colsum_init)
    out_shape = [jax.ShapeDtypeStruct((ka, n), out_dtype)]
    out_specs = [pl.BlockSpec((ka, tn), lambda j, k: (0, j))]
    if with_colsum:
        out_shape.append(jax.ShapeDtypeStruct((1, n), F32))
        out_specs.append(pl.BlockSpec((1, tn), lambda j, k: (0, j)))
    outs = pl.pallas_call(
        body, name=name, grid=(n // tn, nk), out_shape=out_shape, in_specs=in_specs, out_specs=out_specs,
        scratch_shapes=[pltpu.VMEM((ka, tn), F32)],
    )(*args)
    return outs if with_colsum else outs[0]


def _input_grad(dz, w_all, xin, dr, mod, row, name, side=(), side_splits=()):
    m, n = dz.shape
    tm = _row_tile(m, 512)
    tk = w_all.shape[2]
    nk = NDEV
    ni = m // tm
    has_dr = dr is not None
    ns = len(side)
    piece_shapes, piece_of = _pieces(side, side_splits)

    def body(*refs):
        dz_ref, w_ref, x_ref = refs[:3]
        pos = 3
        dr_ref = refs[pos] if has_dr else None
        pos += int(has_dr)
        mod_ref = refs[pos]
        side_in = refs[pos + 1:pos + 1 + ns]
        pos += 1 + ns
        gx_ref = refs[pos] if has_dr else None
        pos += int(has_dr)
        vec_ref = refs[pos]
        side_out = refs[pos + 1:pos + 1 + ns]
        acc = refs[pos + 1 + ns]
        i, k = pl.program_id(0), pl.program_id(1)
        if ns:
            side_start, side_finish = _exchange(side_in, side_out, refs[pos + 2 + ns:], piece_of)

            @pl.when((i == 0) & (k == 0))
            def _():
                side_start()

        @pl.when(k == 0)
        def _():
            acc[...] = jnp.zeros_like(acc)

        @pl.when((i == 0) & (k == 0))
        def _():
            vec_ref[...] = jnp.zeros_like(vec_ref)

        acc[...] += _dot_nt(dz_ref[...], w_ref[0])

        @pl.when(k == nk - 1)
        def _():
            du = acc[...]
            xv = x_ref[...]
            if has_dr:
                sc = mod_ref[row:row + 1, D:2 * D]
                gx_ref[...] = ALPHA * dr_ref[...] + du * (1.0 + sc)
            vec_ref[0:1, :] += jnp.sum(du, axis=0, keepdims=True)
            vec_ref[1:2, :] += jnp.sum(du * xv, axis=0, keepdims=True)

        if ns:
            @pl.when((i == ni - 1) & (k == nk - 1))
            def _():
                side_finish()

    row_spec = pl.BlockSpec((tm, D), lambda i, k: (i, 0))
    in_specs = [pl.BlockSpec((tm, tk), lambda i, k: (i, k)), pl.BlockSpec((1, D, tk), lambda i, k: (k, 0, 0)), row_spec]
    args = [dz, w_all, xin]
    if has_dr:
        in_specs.append(row_spec)
        args.append(dr)
    in_specs.append(pl.BlockSpec((8, 3 * D), lambda i, k: (0, 0)))
    args.append(mod)
    in_specs += [ANY] * ns
    args += list(side)
    out_shape, out_specs = [], []
    if has_dr:
        out_shape.append(jax.ShapeDtypeStruct((m, D), F32))
        out_specs.append(row_spec)
    out_shape.append(jax.ShapeDtypeStruct((8, D), F32))
    out_specs.append(pl.BlockSpec((8, D), lambda i, k: (0, 0)))
    out_shape += [jax.ShapeDtypeStruct((NDEV,) + piece_shapes[a], side[a].dtype) for a in range(ns)]
    out_specs += [ANY] * ns
    outs = pl.pallas_call(
        body, name=name, grid=(ni, nk), out_shape=out_shape, in_specs=in_specs, out_specs=out_specs,
        scratch_shapes=[pltpu.VMEM((tm, D), F32)] + (_exchange_sems(ns) if ns else []),
    )(*args)
    return tuple(outs) if has_dr else (None, *outs)


def _tri(reverse):
    r = lax.broadcasted_iota(jnp.int32, (CHUNK, CHUNK), 0)
    c = lax.broadcasted_iota(jnp.int32, (CHUNK, CHUNK), 1)
    return (c >= r) if reverse else (c <= r)


def _cum_f32(tri_b, t):
    hi = t.astype(BF16)
    r1 = t - hi.astype(F32)
    mid = r1.astype(BF16)
    lo = (r1 - mid.astype(F32)).astype(BF16)
    return _dot(tri_b, hi) + _dot(tri_b, mid) + _dot(tri_b, lo)


def _gla_features(zq, zf, lb):
    sq = _sigmoid(zq)
    q = zq * sq * Q_SCALE
    sf = _sigmoid(zf)
    f = lb + (1.0 - lb) * sf
    return q, sq, f, sf


def _gla_block(n):
    return 256 if n % 256 == 0 else CHUNK


def _gla_fwd(z, lb, s0, d, name, side=()):
    n = z.shape[0]
    blk = _gla_block(n)
    nb, npb = n // blk, blk // CHUNK
    reverse = d == 1
    last = 0 if reverse else CHUNK - 1
    order = list(range(npb))[::-1] if reverse else list(range(npb))
    ns = len(side)

    def bmap(i):
        return nb - 1 - i if reverse else i

    hp = GLA_HEADS_PER_STEP
    hw = hp * DH
    units = [(hh, cidx) for hh in range(hp) for cidx in order]

    def body(zq_ref, zf_ref, zv_ref, lb_ref, s0_ref, *rest):
        side_in = rest[:ns]
        o_ref, ss_ref, sf_ref = rest[ns:ns + 3]
        side_out = rest[ns + 3:2 * ns + 3]
        st = rest[2 * ns + 3]
        i = pl.program_id(1)
        if ns:
            side_start, side_finish = _exchange(side_in, side_out, rest[2 * ns + 4:])

            @pl.when((pl.program_id(0) == 0) & (i == 0))
            def _():
                side_start()

        @pl.when(i == 0)
        def _():
            st[...] = s0_ref[...]

        mask = _tri(reverse)
        tri_b = jnp.where(mask, 1.0, 0.0).astype(BF16)
        feat = {}
        for u in units:
            hh, cidx = u
            rows, cols = pl.ds(cidx * CHUNK, CHUNK), pl.ds(hh * DH, DH)
            q, _, f, _ = _gla_features(zq_ref[rows, cols], zf_ref[rows, cols], lb_ref[d:d + 1, cols])
            feat[u] = (q, 1.0 - f, jnp.log(f), zv_ref[rows, cols].astype(BF16))
        dec = {u: _cum_f32(tri_b, feat[u][2]) for u in units}
        ops = {}
        for u in units:
            q, k, _, vb = feat[u]
            g = dec[u]
            gl = g[last:last + 1, :]
            ops[u] = ((q * jnp.exp(g)).astype(BF16), (k * jnp.exp(-g)).astype(BF16),
                      (k * jnp.exp(gl - g)).astype(BF16), jnp.exp(gl), vb)
        att = {u: jnp.where(mask, _dot_nt(ops[u][0], ops[u][1]), 0.0).astype(BF16) for u in units}
        upd = {u: _dot_tn(ops[u][4], ops[u][2]) for u in units}
        intra = {u: _dot(att[u], ops[u][4]) for u in units}
        s_in = {}
        for hh in range(hp):
            s = st[hh]
            for cidx in order:
                s_in[(hh, cidx)] = s
                s = s * ops[(hh, cidx)][3] + upd[(hh, cidx)]
            st[hh] = s
            sf_ref[hh] = s
        for u in units:
            hh, cidx = u
            rows, cols = pl.ds(cidx * CHUNK, CHUNK), pl.ds(hh * DH, DH)
            o_ref[rows, cols] = intra[u] + _dot_nt(ops[u][0], s_in[u].astype(BF16))
            ss_ref[hh, cidx] = s_in[u]

        if ns:
            @pl.when((pl.program_id(0) == NH // hp - 1) & (i == nb - 1))
            def _():
                side_finish()

    def col(g):
        return lambda h, i: (bmap(i), g * (NH // hp) + h)

    return pl.pallas_call(
        body, name=name, grid=(NH // hp, nb),
        out_shape=[jax.ShapeDtypeStruct((n, D), F32), jax.ShapeDtypeStruct((NH, n // CHUNK, DH, DH), F32),
                   jax.ShapeDtypeStruct((NH, DH, DH), F32)]
        + [jax.ShapeDtypeStruct((NDEV,) + t.shape, t.dtype) for t in side],
        in_specs=[pl.BlockSpec((blk, hw), col(0)), pl.BlockSpec((blk, hw), col(1 + d)),
                  pl.BlockSpec((blk, hw), col(3)), pl.BlockSpec((8, hw), lambda h, i: (0, h)),
                  pl.BlockSpec((hp, DH, DH), lambda h, i: (h, 0, 0))] + [ANY] * ns,
        out_specs=[pl.BlockSpec((blk, hw), lambda h, i: (bmap(i), h)),
                   pl.BlockSpec((hp, npb, DH, DH), lambda h, i: (h, bmap(i), 0, 0)),
                   pl.BlockSpec((hp, DH, DH), lambda h, i: (h, 0, 0))] + [ANY] * ns,
        scratch_shapes=[pltpu.VMEM((hp, DH, DH), F32)] + (_exchange_sems(ns) if ns else []),
    )(z, z, z, lb, s0, *side)


def _gla_bwd(z, lb, s_start, do, ds_fin, acc_q, acc_v, d, name, f_dtype=F32, into=None, side=(), side_splits=()):
    n = z.shape[0]
    blk = _gla_block(n)
    nb, npb = n // blk, blk // CHUNK
    reverse = d == 1
    last = 0 if reverse else CHUNK - 1
    order = list(range(npb)) if reverse else list(range(npb))[::-1]
    has_do = do is not None
    has_acc = acc_q is not None
    fused = into is not None
    assert not fused or d == 1
    ns = len(side)
    assert not (fused and ns)
    piece_shapes, piece_of = _pieces(side, side_splits)
    hp = NH if fused else GLA_HEADS_PER_STEP
    hw = hp * DH
    units = [(hh, cidx) for hh in range(hp) for cidx in order]

    def bmap(i):
        return i if reverse else nb - 1 - i

    def body(*refs):
        zq_ref, zf_ref, zv_ref, lb_ref, ss_ref, dsf_ref = refs[:6]
        pos = 6
        do_ref = aq_ref = av_ref = None
        if has_do:
            do_ref = refs[pos]
            pos += 1
        if has_acc:
            aq_ref, av_ref = refs[pos], refs[pos + 1]
            pos += 2
        if fused:
            other_ref = refs[pos + 1]
            dz_ref, dlb_ref, ds0_ref, dst = refs[pos + 2:]
            dz_ref[:, D:2 * D] = other_ref[...]
        else:
            side_in = refs[pos:pos + ns]
            dzq_ref, dzf_ref, dzv_ref, dlb_ref, ds0_ref = refs[pos + ns:pos + ns + 5]
            side_out = refs[pos + ns + 5:pos + 2 * ns + 5]
            dst = refs[pos + 2 * ns + 5]
        i = pl.program_id(1)
        if ns:
            side_start, side_finish = _exchange(side_in, side_out, refs[pos + 2 * ns + 6:], piece_of)

            @pl.when((pl.program_id(0) == 0) & (i == 0))
            def _():
                side_start()

        @pl.when(i == 0)
        def _():
            dst[...] = dsf_ref[...]
            dlb_ref[...] = jnp.zeros_like(dlb_ref)

        mask = _tri(reverse)
        tri_b = jnp.where(mask, 1.0, 0.0).astype(BF16)
        tri_t = jnp.where(_tri(not reverse), 1.0, 0.0).astype(BF16)

        def where(u):
            return pl.ds(u[1] * CHUNK, CHUNK), pl.ds(u[0] * DH, DH)

        feat = {}
        for u in units:
            rows, cols = where(u)
            zq, zf = zq_ref[rows, cols], zf_ref[rows, cols]
            lbv = lb_ref[d:d + 1, cols]
            q, sq, f, sf = _gla_features(zq, zf, lbv)
            feat[u] = dict(zq=zq, q=q, sq=sq, f=f, sf=sf, lbv=lbv, k=1.0 - f, vb=zv_ref[rows, cols].astype(BF16))
        dec = {u: _cum_f32(tri_b, jnp.log(feat[u]["f"])) for u in units}
        for u in units:
            w = feat[u]
            g = dec[u]
            gl = g[last:last + 1, :]
            w["eg"], w["egi"], w["ege"], w["egl"] = jnp.exp(g), jnp.exp(-g), jnp.exp(gl - g), jnp.exp(gl)
            w["qd"], w["ki"], w["ke"] = w["q"] * w["eg"], w["k"] * w["egi"], w["k"] * w["ege"]
            w["qdb"], w["kib"], w["keb"] = w["qd"].astype(BF16), w["ki"].astype(BF16), w["ke"].astype(BF16)
            w["s_in"] = ss_ref[u[0], u[1]]
        if has_do:
            for u in units:
                w = feat[u]
                rows, cols = where(u)
                w["dob"] = do_ref[rows, cols].astype(BF16)
            for u in units:
                w = feat[u]
                w["a"] = jnp.where(mask, _dot_nt(w["qdb"], w["kib"]), 0.0).astype(BF16)
                w["da"] = jnp.where(mask, _dot_nt(w["dob"], w["vb"]), 0.0).astype(BF16)
                w["m"] = _dot_tn(w["dob"], w["qdb"])
        for hh in range(hp):
            ds = dst[hh]
            for cidx in order:
                w = feat[(hh, cidx)]
                w["ds"] = ds
                ds = ds * w["egl"]
                if has_do:
                    ds = ds + w["m"]
            dst[hh] = ds
            ds0_ref[hh] = ds
        for u in units:
            w = feat[u]
            dsb = w["ds"].astype(BF16)
            w["dke"] = _dot(w["vb"], dsb)
            w["dv"] = _dot_nt(w["keb"], dsb)
            if has_do:
                w["dv"] = w["dv"] + _dot_tn(w["a"], w["dob"])
                w["dqd"] = _dot(w["da"], w["kib"]) + _dot(w["dob"], w["s_in"].astype(BF16))
                w["dki"] = _dot_tn(w["da"], w["qdb"])
        for u in units:
            w = feat[u]
            dkeke = w["dke"] * w["ke"]
            w["dgl"] = (w["egl"] * jnp.sum(w["s_in"] * w["ds"], axis=0, keepdims=True)
                        + jnp.sum(dkeke, axis=0, keepdims=True))
            dg = -dkeke
            dk = w["dke"] * w["ege"]
            if has_do:
                dg = dg + w["dqd"] * w["qd"] - w["dki"] * w["ki"]
                dk = dk + w["dki"] * w["egi"]
            w["dg"], w["dk"] = dg, dk
        dlf = {u: _cum_f32(tri_t, feat[u]["dg"]) for u in units}
        for u in units:
            w = feat[u]
            rows, cols = where(u)
            df = (dlf[u] + w["dgl"]) / w["f"] - w["dk"]
            sf = w["sf"]
            dzf = df * (1.0 - w["lbv"]) * sf * (1.0 - sf)
            dlb_ref[0:1, cols] += jnp.sum(df * (1.0 - sf), axis=0, keepdims=True)
            if has_do:
                dzq = w["dqd"] * w["eg"] * (Q_SCALE * _dsilu(w["zq"], w["sq"]))
            else:
                dzq = jnp.zeros((CHUNK, DH), F32)
            dv = w["dv"]
            if has_acc:
                dzq = dzq + aq_ref[rows, cols]
                dv = dv + av_ref[rows, cols]
            if fused:
                lane = u[0] * DH
                dz_ref[rows, pl.ds(lane, DH)] = dzq.astype(BF16)
                dz_ref[rows, pl.ds(2 * D + lane, DH)] = dzf.astype(BF16)
                dz_ref[rows, pl.ds(3 * D + lane, DH)] = dv.astype(BF16)
            else:
                dzq_ref[rows, cols] = dzq
                dzf_ref[rows, cols] = dzf.astype(f_dtype)
                dzv_ref[rows, cols] = dv

        if ns:
            @pl.when((pl.program_id(0) == NH // hp - 1) & (i == nb - 1))
            def _():
                side_finish()

    def col(g):
        return lambda h, i: (bmap(i), g * (NH // hp) + h)

    tok = pl.BlockSpec((blk, hw), lambda h, i: (bmap(i), h))
    state = pl.BlockSpec((hp, DH, DH), lambda h, i: (h, 0, 0))
    in_specs = [pl.BlockSpec((blk, hw), col(0)), pl.BlockSpec((blk, hw), col(1 + d)), pl.BlockSpec((blk, hw), col(3)),
                pl.BlockSpec((8, hw), lambda h, i: (0, h)),
                pl.BlockSpec((hp, npb, DH, DH), lambda h, i: (h, bmap(i), 0, 0)), state]
    args = [z, z, z, lb, s_start, ds_fin]
    if has_do:
        in_specs.append(tok)
        args.append(do)
    if has_acc:
        in_specs += [tok, tok]
        args += [acc_q, acc_v]
    tail_shape = [jax.ShapeDtypeStruct((8, D), F32), jax.ShapeDtypeStruct((NH, DH, DH), F32)]
    tail_specs = [pl.BlockSpec((8, hw), lambda h, i: (0, h)), state]
    if fused:
        buf, other = into
        aliases = {len(args): 0}
        in_specs += [ANY, tok]
        args += [buf, other]
        out_shape = [jax.ShapeDtypeStruct(buf.shape, buf.dtype)] + tail_shape
        out_specs = [pl.BlockSpec((blk, 4 * D), lambda h, i: (bmap(i), 0))] + tail_specs
    else:
        aliases = {}
        in_specs += [ANY] * ns
        args += list(side)
        out_shape = [jax.ShapeDtypeStruct((n, D), F32), jax.ShapeDtypeStruct((n, D), f_dtype),
                     jax.ShapeDtypeStruct((n, D), F32)] + tail_shape
        out_shape += [jax.ShapeDtypeStruct((NDEV,) + piece_shapes[a], side[a].dtype) for a in range(ns)]
        out_specs = [tok, tok, tok] + tail_specs + [ANY] * ns
    return pl.pallas_call(
        body, name=name, grid=(NH // hp, nb), out_shape=out_shape, in_specs=in_specs, out_specs=out_specs,
        input_output_aliases=aliases,
        scratch_shapes=[pltpu.VMEM((hp, DH, DH), F32)] + (_exchange_sems(ns) if ns else []),
    )(*args)


def _shift(t, s, fill, down):
    n = t.shape[0]
    rows = lax.broadcasted_iota(jnp.int32, t.shape, 0)
    if down:
        return jnp.where(rows >= s, pltpu.roll(t, s, 0), fill)
    return jnp.where(rows < n - s, pltpu.roll(t, n - s, 0), fill)


SUBLANES = 8
LRU_SAVED = 4


def _chain_scan(a, b, h_in, down):
    n = a.shape[0]
    ng = n // SUBLANES
    rows = lax.broadcasted_iota(jnp.int32, (SUBLANES, a.shape[1]), 0)
    local = []
    for g in range(ng):
        aa, bb = a[g * SUBLANES:(g + 1) * SUBLANES], b[g * SUBLANES:(g + 1) * SUBLANES]
        for s in (1, 2, 4):
            if down:
                keep, amt = rows >= s, s
            else:
                keep, amt = rows < SUBLANES - s, SUBLANES - s
            bb = bb + aa * jnp.where(keep, pltpu.roll(bb, amt, 0), 0.0)
            aa = aa * jnp.where(keep, pltpu.roll(aa, amt, 0), 1.0)
        local.append((aa, bb))
    out = [None] * ng
    carry = h_in
    for g in (range(ng) if down else range(ng - 1, -1, -1)):
        aa, bb = local[g]
        hg = bb + aa * carry
        out[g] = hg
        carry = hg[SUBLANES - 1:SUBLANES] if down else hg[0:1]
    return (jnp.concatenate(out, axis=0) if ng > 1 else out[0]), carry


def _conv_taps(xv):
    return (_shift(xv, 1, 0.0, True), xv, _shift(xv, 1, 0.0, False), _shift(xv, 2, 0.0, False))


def _conv(taps, cw, cb):
    return cb + cw[0:1, :] * taps[0] + cw[1:2, :] * taps[1] + cw[2:3, :] * taps[2] + cw[3:4, :] * taps[3]


def _neg_expm1(t):
    series = -t * (1.0 + t * (0.5 + t * (1.0 / 6.0 + t * (1.0 / 24.0 + t * (1.0 / 120.0)))))
    return jnp.where(t > -0.1, series, 1.0 - jnp.exp(t))


def _lru_gates(xc, wr, br, wi, bi, lam):
    xcb = xc.astype(BF16)
    r = _sigmoid(_dot(xcb, wr) + br)
    gi = _sigmoid(_dot(xcb, wi) + bi)
    sp = jnp.maximum(-lam, 0.0) + jnp.log(1.0 + jnp.exp(-jnp.abs(lam)))
    la = -RG_C * r * sp
    a = jnp.exp(la)
    mult = jnp.sqrt(_neg_expm1(2.0 * la))
    return xcb, r, gi, sp, a, mult


def _lru_fwd(xin, blk, cw, cb, wr, br, wi, bi, lam, h0, acc_h, d, name):
    n = xin.shape[0]
    nb = n // blk
    reverse = d == 1
    down = not reverse
    has_acc = acc_h is not None

    def bmap(i):
        return nb - 1 - i if reverse else i

    def body(*refs):
        x_ref, cw_ref, cb_ref, wr_ref, br_ref, wi_ref, bi_ref, lam_ref, h0_ref = refs[:9]
        pos = 9
        acc_ref = refs[pos] if has_acc else None
        pos += int(has_acc)
        h_ref, hin_ref, hfin_ref, sav_a_ref, sav_ref = refs[pos:pos + 5]
        pos += 5
        hsum_ref = refs[pos] if has_acc else None
        carry = refs[-1]
        i = pl.program_id(0)

        @pl.when(i == 0)
        def _():
            carry[...] = h0_ref[...]

        for g in range(NH):
            cols = pl.ds(g * DH, DH)
            xc = _conv(_conv_taps(x_ref[:, cols]), cw_ref[:, cols], cb_ref[:, cols])
            _, r, gi, _, a, mult = _lru_gates(xc, wr_ref[g], br_ref[:, cols], wi_ref[g], bi_ref[:, cols],
                                              lam_ref[:, cols])
            sav_a_ref[:, cols] = a
            for slot, val in enumerate((xc, r, gi, mult)):
                sav_ref[slot, :, cols] = val.astype(BF16)
            hin = carry[:, cols]
            h, h_last = _chain_scan(a, mult * gi * xc, hin, down)
            h_ref[:, cols] = h
            if has_acc:
                hsum_ref[:, cols] = h + acc_ref[:, cols]
            hin_ref[0, :, cols] = hin
            carry[:, cols] = h_last
            hfin_ref[:, cols] = h_last

    vec = pl.BlockSpec((1, D), lambda i: (0, 0))
    wsp = pl.BlockSpec((NH, DH, DH), lambda i: (0, 0, 0))
    tok = pl.BlockSpec((blk, D), lambda i: (bmap(i), 0))
    in_specs = [tok, pl.BlockSpec((4, D), lambda i: (0, 0)), vec, wsp, vec, wsp, vec, vec, vec]
    args = [xin, cw, cb, wr, br, wi, bi, lam, h0]
    out_shape = [jax.ShapeDtypeStruct((n, D), F32), jax.ShapeDtypeStruct((nb, 1, D), F32),
                 jax.ShapeDtypeStruct((1, D), F32), jax.ShapeDtypeStruct((n, D), F32),
                 jax.ShapeDtypeStruct((LRU_SAVED, n, D), BF16)]
    out_specs = [tok, pl.BlockSpec((1, 1, D), lambda i: (bmap(i), 0, 0)), vec, tok,
                 pl.BlockSpec((LRU_SAVED, blk, D), lambda i: (0, bmap(i), 0))]
    if has_acc:
        in_specs.append(tok)
        args.append(acc_h)
        out_shape.append(jax.ShapeDtypeStruct((n, D), F32))
        out_specs.append(tok)
    return pl.pallas_call(
        body, name=name, grid=(nb,), out_shape=out_shape, in_specs=in_specs, out_specs=out_specs,
        scratch_shapes=[pltpu.VMEM((1, D), F32)],
    )(*args)


def _lru_bwd(xin, blk, cw, wr, wi, lam, sav, h, hin, dh, cg_fin, acc_dx, init, d, name):
    n = xin.shape[0]
    nb = n // blk
    reverse = d == 1
    down = not reverse
    first = blk - 1 if reverse else 0
    has_dh = dh is not None
    has_acc = acc_dx is not None
    has_init = init is not None

    def bmap(i):
        return i if reverse else nb - 1 - i

    def body(*refs):
        (x_ref, cw_ref, wr_ref, wi_ref, lam_ref, sav_a_ref, sav_ref, h_ref, hin_ref, cgf_ref) = refs[:10]
        pos = 10
        dh_ref = acc_ref = None
        iwr_ref = iwi_ref = ivec_ref = None
        if has_dh:
            dh_ref = refs[pos]
            pos += 1
        if has_acc:
            acc_ref = refs[pos]
            pos += 1
        if has_init:
            iwr_ref, iwi_ref, ivec_ref = refs[pos:pos + 3]
            pos += 3
        dx_ref, dwr_ref, dwi_ref, vec_ref, cg0_ref, carry = refs[pos:]
        i = pl.program_id(0)

        @pl.when(i == 0)
        def _():
            carry[...] = cgf_ref[...]
            if has_init:
                dwr_ref[...] = iwr_ref[...]
                dwi_ref[...] = iwi_ref[...]
                vec_ref[...] = ivec_ref[...]
            else:
                dwr_ref[...] = jnp.zeros_like(dwr_ref)
                dwi_ref[...] = jnp.zeros_like(dwi_ref)
                vec_ref[...] = jnp.zeros_like(vec_ref)

        for g in range(NH):
            cols = pl.ds(g * DH, DH)
            cwv = cw_ref[:, cols]
            lam_v = lam_ref[:, cols]
            taps = _conv_taps(x_ref[:, cols])
            wr_g, wi_g = wr_ref[g], wi_ref[g]
            a = sav_a_ref[:, cols]
            xcb = sav_ref[0, :, cols]
            xc, r, gi, mult = (sav_ref[slot, :, cols].astype(F32) for slot in range(LRU_SAVED))
            sp = jnp.maximum(-lam_v, 0.0) + jnp.log(1.0 + jnp.exp(-jnp.abs(lam_v)))
            hprev = _shift(h_ref[:, cols], 1, hin_ref[0, :, cols], down)
            a_next = _shift(a, 1, 1.0, not down)
            dhv = dh_ref[:, cols] if has_dh else jnp.zeros_like(a)
            e, _ = _chain_scan(a_next, dhv, carry[:, cols], not down)
            cg = a[first:first + 1, :] * e[first:first + 1, :]
            carry[:, cols] = cg
            cg0_ref[:, cols] = cg
            da = e * hprev
            emult = e * mult
            dgi = emult * xc
            dxc = emult * gi
            dla = da * a - (e * gi * xc) * (a * a) / mult
            dr = dla * (-RG_C * sp)
            sneg = 1.0 - _sigmoid(lam_v)
            dpr = dr * r * (1.0 - r)
            dpi = dgi * gi * (1.0 - gi)
            dprb, dpib = dpr.astype(BF16), dpi.astype(BF16)
            dxc = dxc + _dot_nt(dprb, wr_g) + _dot_nt(dpib, wi_g)
            dwr_ref[g] += _dot_tn(xcb, dprb)
            dwi_ref[g] += _dot_tn(xcb, dpib)
            dx = (cwv[0:1, :] * _shift(dxc, 1, 0.0, False) + cwv[1:2, :] * dxc
                  + cwv[2:3, :] * _shift(dxc, 1, 0.0, True) + cwv[3:4, :] * _shift(dxc, 2, 0.0, True))
            if has_acc:
                dx = dx + acc_ref[:, cols]
            dx_ref[:, cols] = dx
            vec_ref[0:1, cols] += jnp.sum(dpr, axis=0, keepdims=True)
            vec_ref[1:2, cols] += jnp.sum(dpi, axis=0, keepdims=True)
            vec_ref[2:3, cols] += jnp.sum(dla * r, axis=0, keepdims=True) * (RG_C * sneg)
            vec_ref[3:4, cols] += jnp.sum(dxc, axis=0, keepdims=True)
            for kk in range(4):
                vec_ref[4 + kk:5 + kk, cols] += jnp.sum(dxc * taps[kk], axis=0, keepdims=True)

    vec = pl.BlockSpec((1, D), lambda i: (0, 0))
    wsp = pl.BlockSpec((NH, DH, DH), lambda i: (0, 0, 0))
    tok = pl.BlockSpec((blk, D), lambda i: (bmap(i), 0))
    vec16 = pl.BlockSpec((16, D), lambda i: (0, 0))
    in_specs = [tok, pl.BlockSpec((4, D), lambda i: (0, 0)), wsp, wsp, vec, tok,
                pl.BlockSpec((LRU_SAVED, blk, D), lambda i: (0, bmap(i), 0)), tok,
                pl.BlockSpec((1, 1, D), lambda i: (bmap(i), 0, 0)), vec]
    args = [xin, cw, wr, wi, lam, sav[0], sav[1], h, hin, cg_fin]
    if has_dh:
        in_specs.append(tok)
        args.append(dh)
    if has_acc:
        in_specs.append(tok)
        args.append(acc_dx)
    if has_init:
        in_specs += [wsp, wsp, vec16]
        args += list(init)
    return pl.pallas_call(
        body, name=name, grid=(nb,),
        out_shape=[jax.ShapeDtypeStruct((n, D), F32), jax.ShapeDtypeStruct((NH, DH, DH), F32),
                   jax.ShapeDtypeStruct((NH, DH, DH), F32), jax.ShapeDtypeStruct((16, D), F32),
                   jax.ShapeDtypeStruct((1, D), F32)],
        in_specs=in_specs, out_specs=[tok, wsp, wsp, vec16, vec],
        scratch_shapes=[pltpu.VMEM((1, D), F32)],
    )(*args)


def _merge(z, o_f, o_b, hx, xin, tgt, mod, gn, p_a, p_b, w_out, ln_g, ln_b):
    n = xin.shape[0]
    tm = _row_tile(n, 128)

    def body(z4_ref, z6_ref, z7_ref, z8_ref, of_ref, ob_ref, hx_ref, x_ref, t_ref, mod_ref, gn_ref,
             pa_ref, pb_ref, wo_ref, lg_ref, lnb_ref,
             dr_ref, do_ref, dhx_ref, dz_ref, oa_o, obb_o, y_o, dya_o, dyb_o, dout_o, vec_ref):
        @pl.when(pl.program_id(0) == 0)
        def _():
            vec_ref[...] = jnp.zeros_like(vec_ref)

        gt = mod_ref[0:1, 2 * D:3 * D]
        gnv = gn_ref[...]
        o = of_ref[...] + ob_ref[...]
        rs = jnp.concatenate(
            [jnp.broadcast_to(lax.rsqrt(jnp.mean(jnp.square(o[:, h * DH:(h + 1) * DH]), axis=1, keepdims=True)
                                        + RMS_EPS), (tm, DH)) for h in range(NH)], axis=1)
        nrm = o * rs
        rn = nrm * gnv
        z4, z6, z7, z8 = z4_ref[...], z6_ref[...], z7_ref[...], z8_ref[...]
        s4, s6, s7, s8 = _sigmoid(z4), _sigmoid(z6), _sigmoid(z7), _sigmoid(z8)
        sg4, sg6 = z4 * s4, z6 * s6
        hxv = hx_ref[...]
        oa = (rn * sg4).astype(BF16)
        obb = (hxv * sg6).astype(BF16)
        ya = _dot(oa, pa_ref[...])
        yb = _dot(obb, pb_ref[...])
        y = (s7 * ya + s8 * yb).astype(BF16)
        out = _dot(y, wo_ref[...])
        xv = x_ref[...]
        rr = ALPHA * xv + gt * out
        mu = jnp.mean(rr, axis=1, keepdims=True)
        cen = rr - mu
        rstd = lax.rsqrt(jnp.mean(cen * cen, axis=1, keepdims=True) + LN_EPS)
        xhat = cen * rstd
        lg = lg_ref[...]
        err = xhat * lg + lnb_ref[...] - t_ref[...]
        loss_rows = jnp.sum(err * err, axis=1, keepdims=True)
        dxn = err * (1.0 / D)
        dxh = dxn * lg
        dr = rstd * (dxh - jnp.mean(dxh, axis=1, keepdims=True)
                     - xhat * jnp.mean(dxh * xhat, axis=1, keepdims=True))
        dout = (dr * gt).astype(BF16)
        dy = _dot_nt(dout, wo_ref[...])
        dya = (dy * s7).astype(BF16)
        dyb = (dy * s8).astype(BF16)
        doa = _dot_nt(dya, pa_ref[...])
        dob = _dot_nt(dyb, pb_ref[...])
        drn = doa * sg4
        dn = drn * gnv
        dnn = dn * nrm
        corr = jnp.concatenate(
            [jnp.broadcast_to(jnp.mean(dnn[:, h * DH:(h + 1) * DH], axis=1, keepdims=True), (tm, DH))
             for h in range(NH)], axis=1)
        dr_ref[...] = dr
        do_ref[...] = rs * (dn - nrm * corr)
        dhx_ref[...] = dob * sg6
        dz_ref[:, 0:4 * D] = jnp.zeros((tm, 4 * D), BF16)
        dz_ref[:, 4 * D:5 * D] = (doa * rn * _dsilu(z4, s4)).astype(BF16)
        dz_ref[:, 5 * D:6 * D] = jnp.zeros((tm, D), BF16)
        dz_ref[:, 6 * D:7 * D] = (dob * hxv * _dsilu(z6, s6)).astype(BF16)
        dz_ref[:, 7 * D:8 * D] = (dy * ya * s7 * (1.0 - s7)).astype(BF16)
        dz_ref[:, 8 * D:9 * D] = (dy * yb * s8 * (1.0 - s8)).astype(BF16)
        oa_o[...] = oa
        obb_o[...] = obb
        y_o[...] = y
        dya_o[...] = dya
        dyb_o[...] = dyb
        dout_o[...] = dout
        vec_ref[0:1, :] += jnp.sum(dr * out, axis=0, keepdims=True)
        vec_ref[1:2, :] += jnp.sum(dxn * xhat, axis=0, keepdims=True)
        vec_ref[2:3, :] += jnp.sum(dxn, axis=0, keepdims=True)
        vec_ref[3:4, :] += jnp.sum(drn * nrm, axis=0, keepdims=True)
        vec_ref[4:5, :] += jnp.broadcast_to(jnp.sum(loss_rows, axis=0, keepdims=True) * (0.5 / D), (1, D))

    def grp(g):
        return pl.BlockSpec((tm, D), lambda i: (i, g))

    tok = pl.BlockSpec((tm, D), lambda i: (i, 0))
    vec = pl.BlockSpec((1, D), lambda i: (0, 0))
    wsp = pl.BlockSpec((D, D), lambda i: (0, 0))
    return pl.pallas_call(
        body, name="merge", grid=(n // tm,),
        out_shape=[jax.ShapeDtypeStruct((n, D), F32)] * 3
        + [jax.ShapeDtypeStruct((n, NGRP * D), BF16)]
        + [jax.ShapeDtypeStruct((n, D), BF16)] * 6 + [jax.ShapeDtypeStruct((8, D), F32)],
        in_specs=[grp(4), grp(6), grp(7), grp(8), tok, tok, tok, tok, tok,
                  pl.BlockSpec((8, 3 * D), lambda i: (0, 0)), vec, wsp, wsp, wsp, vec, vec],
        out_specs=[tok, tok, tok, pl.BlockSpec((tm, NGRP * D), lambda i: (i, 0))] + [tok] * 6
        + [pl.BlockSpec((8, D), lambda i: (0, 0))],
    )(z, z, z, z, o_f, o_b, hx, xin, tgt, mod, gn, p_a, p_b, w_out, ln_g, ln_b)


def _wmod_grad(c_t, cctx_t, dmx_loc, dmc_loc, name):
    n = dmx_loc.shape[1]

    def body(ct_ref, cc_ref, dmx_ref, dmc_ref, o_ref):
        ct = ct_ref[...]
        sct = ct * _sigmoid(ct)
        cc = cc_ref[...]
        scc = cc * _sigmoid(cc)
        dmc = dmc_ref[0:1, :]
        for b in range(1, NDEV):
            dmc = dmc + dmc_ref[b:b + 1, :]
        acc = scc * dmc
        for b in range(NDEV):
            acc = acc + sct[:, b:b + 1] * dmx_ref[b:b + 1, :]
        o_ref[...] = acc

    return pl.pallas_call(body, name=name, out_shape=jax.ShapeDtypeStruct((D, n), F32))(c_t, cctx_t, dmx_loc, dmc_loc)


PACK_ROWS = 40


def _finalize_small(g_pack, lb, w_mod_full, params):
    npar = len(params)

    def body(*refs):
        gp_ref, lb_ref, wm_ref = refs[:3]
        wmv = refs[3:3 + 3 * npar]
        loss_ref = refs[3 + 3 * npar]
        g_refs = refs[4 + 3 * npar:4 + 4 * npar]
        upd = refs[4 + 4 * npar:4 + 7 * npar]
        tot = refs[-1]
        acc = gp_ref[0]
        for k in range(1, NDEV):
            acc = acc + gp_ref[k]
        tot[...] = acc
        mine = pl.ds(pl.multiple_of(_my_index() * DH, DH), DH)
        (g_cctx, g_bmod, g_bin, g_lbl, g_norm, g_cw, g_cb, g_br, g_bi, g_lam, g_lng, g_lnb) = g_refs

        loss_ref[...] = jnp.broadcast_to(tot[36:37, 0:DH], (8, DH))
        for k in range(3):
            g_bmod[:, k * D:(k + 1) * D] = tot[k:k + 1, :] + tot[3 + k:4 + k, :]
        dmc = jnp.concatenate([tot[3:4, :], tot[4:5, :], tot[5:6, :]], axis=1)
        cv = wmv[0][...]
        proj = _dot_nt(jnp.broadcast_to(dmc, (8, 3 * D)).astype(BF16), wm_ref[...])
        g_cctx[...] = proj[0:1, :] * _dsilu(cv, _sigmoid(cv))
        for k in range(NGRP):
            g_bin[:, k * D:(k + 1) * D] = tot[6 + k:7 + k, :]
        nrm = tot[15:16, 0:DH]
        for h in range(1, NH):
            nrm = nrm + tot[15:16, h * DH:(h + 1) * DH]
        g_norm[...] = nrm
        g_lng[...] = tot[16:17, :]
        g_lnb[...] = tot[17:18, :]
        g_cb[...] = tot[21:22, :] + tot[29:30, :]
        g_cw[0] = tot[22:26, mine] + tot[30:34, mine]
        for ref, row in ((g_br, 18), (g_bi, 19), (g_lam, 20)):
            ref[0, 0:1, :] = tot[row:row + 1, mine]
            ref[0, 1:2, :] = tot[row + 8:row + 9, mine]
        lbl = lb_ref[0:2, mine]
        dl0 = tot[34:36, mine] * lbl * (1.0 - lbl)
        g_lbl[0] = dl0
        g_lbl[1] = -dl0
        for p in range(npar):
            d, mm, vv = _adam_math(g_refs[p][...], wmv[3 * p][...], wmv[3 * p + 1][...], wmv[3 * p + 2][...])
            upd[3 * p][...] = d
            upd[3 * p + 1][...] = mm
            upd[3 * p + 2][...] = vv

    flat = [t for wmv in params for t in wmv]
    shapes = [jax.ShapeDtypeStruct(wmv[0].shape, F32) for wmv in params]
    outs = pl.pallas_call(
        body, name="finalize_small",
        out_shape=[jax.ShapeDtypeStruct((8, DH), F32)] + shapes + [s for s in shapes for _ in range(3)],
        scratch_shapes=[pltpu.VMEM((PACK_ROWS, D), F32)],
    )(g_pack, lb, w_mod_full, *flat)
    grads = list(outs[1:1 + npar])
    upd = [tuple(outs[1 + npar + 3 * p:4 + npar + 3 * p]) for p in range(npar)]
    return outs[0], grads, upd


def _to_colmajor(t, rows):
    return t.reshape(rows, GRID_W, D).transpose(1, 0, 2).reshape(rows * GRID_W, D)


def _to_raster(t, rows):
    return t.reshape(GRID_W, rows, D).transpose(1, 0, 2).reshape(rows * GRID_W, D)


def _local_cols(t, me, width):
    return lax.dynamic_slice_in_dim(t, me * width, width, axis=t.ndim - 1)


def kernel(x, c, ctx, c_ctx, w_mod, b_mod, w_in, b_in, lb_logits, norm_a_g, conv_w, conv_b, w_r, b_r, w_i, b_i, lam, p_a, p_b, w_out, ln_g, ln_b, loss_target, m_c_ctx, m_w_mod, m_b_mod, m_w_in, m_b_in, m_lb_logits, m_norm_a_g, m_conv_w, m_conv_b, m_w_r, m_b_r, m_w_i, m_b_i, m_lam, m_p_a, m_p_b, m_w_out, m_ln_g, m_ln_b, v_c_ctx, v_w_mod, v_b_mod, v_w_in, v_b_in, v_lb_logits, v_norm_a_g, v_conv_w, v_conv_b, v_w_r, v_b_r, v_w_i, v_b_i, v_lam, v_p_a, v_p_b, v_w_out, v_ln_g, v_ln_b):
    me = _my_index()
    xs, cs, tgt = x[0], ctx[0], loss_target[0]
    t_len, c_len = xs.shape[0], cs.shape[0]
    rows = t_len // GRID_W
    wcols = w_in.shape[2]
    mcols = w_mod.shape[2]

    w_mod_b, w_in_b, p_a_b, p_b_b, w_out_b, w_r_b, w_i_b = _cast_bf16(
        [w_mod[0], w_in[0], p_a[0], p_b[0], w_out[0], w_r[0], w_i[0]])
    small = jnp.concatenate([lb_logits.reshape(4, DH), conv_w[0], b_r[0], b_i[0], lam[0], jnp.zeros((2, DH), F32),
                             c.reshape(8, DH)], axis=0)
    g_small, g_wmod = _all_gather([small, w_mod_b], "gather_params")

    def full_rows(lo, hi):
        return g_small[:, lo:hi, :].transpose(1, 0, 2).reshape(hi - lo, D)

    lbl_f, cw_f, br_f, bi_f, lam_f = full_rows(0, 4), full_rows(4, 8), full_rows(8, 10), full_rows(10, 12), full_rows(12, 14)
    c_all = g_small[:, 16:24, :].reshape(NDEV, D)
    w_mod_f = g_wmod.transpose(1, 0, 2).reshape(D, 3 * D)

    cc = jnp.concatenate([c.reshape(1, D), c_ctx.reshape(1, D), jnp.zeros((6, D), F32)], axis=0)
    lbl_p = jnp.concatenate([lbl_f.reshape(2, 2, D), jnp.zeros((2, 6, D), F32)], axis=1)
    mod, lb = _prep(cc, w_mod_f, b_mod, lbl_p)
    u_x = _modulate(xs, mod, 0, "modulate_x")
    u_c = _modulate(cs, mod, 1, "modulate_c")
    z_x, w_in_f = _inproj_gather(u_x, w_in_b, b_in, "inproj_gather")
    z_c = _mm_bias(u_c, w_in_f, b_in, "inproj_c")

    zero_s = jnp.zeros((NH, DH, DH), F32)
    zero_v = jnp.zeros((1, D), F32)
    gla = {}
    out_w = [p_a_b, p_b_b, w_out_b]
    out_w_f = []
    for d in (0, 1):
        _, ssc, sfc = _gla_fwd(z_c, lb, zero_s, d, f"gla_fwd_c{d}")
        o_d, ssx, _, *gathered = _gla_fwd(z_x, lb, sfc, d, f"gla_fwd_x{d}", side=out_w[1:] if d else out_w[:1])
        out_w_f += [t.reshape(D, D) for t in gathered]
        gla[d] = (ssc, ssx, o_d)
    p_a_f, p_b_f, w_out_f = out_w_f

    x5_c = z_c[:, 5 * D:6 * D]
    x5_x = _to_colmajor(z_x[:, 5 * D:6 * D], rows)
    cb2 = conv_b.reshape(1, D)
    lru = {}
    h_sum = None
    for d in (0, 1):
        prm = (cw_f, cb2, w_r_b[d], br_f[d:d + 1], w_i_b[d], bi_f[d:d + 1], lam_f[d:d + 1])
        h_c, hin_c, hfin_c, *sav_c = _lru_fwd(x5_c, c_len, *prm, zero_v, None, d, f"lru_fwd_c{d}")
        h_x, hin_x, _, sav_a, sav_h, *h_sum = _lru_fwd(x5_x, rows, *prm, hfin_c, lru[0][3] if d else None, d,
                                                       f"lru_fwd_x{d}")
        lru[d] = ((cw_f, w_r_b[d], w_i_b[d], lam_f[d:d + 1]), h_c, hin_c, h_x, hin_x, tuple(sav_c), (sav_a, sav_h))
    hx = _to_raster(h_sum[0], rows)

    gn = jnp.tile(norm_a_g.reshape(1, DH), (1, NH))
    (dr, do, dhx, dz_m, oa, obb, yb16, dya, dyb, dout, mvec) = _merge(
        z_x, gla[0][2], gla[1][2], hx, xs, tgt, mod, gn, p_a_f, p_b_f, w_out_f, ln_g, ln_b)

    dhx_cm = _to_colmajor(dhx, rows)
    lru_dx_x = lru_dx_c = None
    for d in (0, 1):
        prm, h_c, hin_c, h_x, hin_x, sav_c, sav_x = lru[d]
        lru_dx_x, dwr, dwi, lvec, cg0 = _lru_bwd(x5_x, rows, *prm, sav_x, h_x, hin_x, dhx_cm, zero_v, lru_dx_x, None,
                                                 d, f"lru_bwd_x{d}")
        lru_dx_c, dwr, dwi, lvec, _ = _lru_bwd(x5_c, c_len, *prm, sav_c, h_c, hin_c, None, cg0, lru_dx_c,
                                               (dwr, dwi, lvec), d, f"lru_bwd_c{d}")
        lru[d] = (dwr, dwi, lvec)
    dz5_x = _to_raster(lru_dx_x, rows).astype(BF16)
    dz5_c = lru_dx_c.astype(BF16)

    dpa = _mm_tn(oa, dya, None, "dpa", out_dtype=BF16)
    dpb = _mm_tn(obb, dyb, None, "dpb", out_dtype=BF16)
    dwo = _mm_tn(yb16, dout, None, "dwout", out_dtype=BF16)
    wr_pack = jnp.concatenate([lru[0][0], lru[1][0], lru[0][1], lru[1][1]], axis=0).reshape(4 * NH * DH, DH)

    gq_c = gv_c = None
    dzf_c, dlb = {}, {}
    gq_x, dzf_x0, gv_x, dlb_x, ds0, r_pa, r_pb, r_wo, r_wri = _gla_bwd(
        z_x, lb, gla[0][1], do, zero_s, None, None, 0, "gla_bwd_x0", f_dtype=BF16,
        side=[dpa, dpb, dwo, wr_pack], side_splits=[0, 0, 0, 0])
    gq_c, dzf_c[0], gv_c, dlb_c, _ = _gla_bwd(z_c, lb, gla[0][0], None, ds0, None, None, 0, "gla_bwd_c0")
    dlb[0] = dlb_x[0:1] + dlb_c[0:1]
    dz_g, dlb_x, ds0 = _gla_bwd(z_x, lb, gla[1][1], do, zero_s, gq_x, gv_x, 1, "gla_bwd_x1", into=(dz_m, dzf_x0))
    gq_c, dzf_c[1], gv_c, dlb_c, _ = _gla_bwd(z_c, lb, gla[1][0], None, ds0, gq_c, gv_c, 1, "gla_bwd_c1")
    dlb[1] = dlb_x[0:1] + dlb_c[0:1]

    bf = lambda t: t.astype(BF16)
    dz_x = lax.dynamic_update_slice(dz_g, dz5_x, (0, 5 * D))
    zc0 = jnp.zeros((c_len, D), BF16)
    dz_c = jnp.concatenate([bf(gq_c), bf(dzf_c[0]), bf(dzf_c[1]), bf(gv_c), zc0, dz5_c, zc0, zc0, zc0], axis=1)
    dwin_c, dbin_c = _mm_tn(u_c, dz_c, None, "dwin_c", with_colsum=True)

    grad_x, xvec = _input_grad(dz_x, w_in_f, xs, dr, mod, 0, "input_grad_x")
    r_win, dbin = _dwin_exchange(u_x, dz_x, dwin_c, dbin_c, "dwin_exchange")
    _, cvec = _input_grad(dz_c, w_in_f, cs, None, mod, 1, "input_grad_c")
    wri_piece = _sum_rows(r_wri, "sum_w_ri_piece")
    g_w_in, d_w_in, nm_w_in, nv_w_in = _sum_adamw(r_win, w_in, m_w_in, v_w_in, "update_w_in")
    g_p_a, d_p_a, nm_p_a, nv_p_a = _sum_adamw(r_pa, p_a, m_p_a, v_p_a, "update_p_a")
    g_p_b, d_p_b, nm_p_b, nv_p_b = _sum_adamw(r_pb, p_b, m_p_b, v_p_b, "update_p_b")
    g_w_out, d_w_out, nm_w_out, nv_w_out = _sum_adamw(r_wo, w_out, m_w_out, v_w_out, "update_w_out")

    dlb_rows = jnp.concatenate([dlb[0], dlb[1]], axis=0)
    pack = jnp.concatenate([
        xvec[0:1], xvec[1:2], mvec[0:1],
        cvec[0:1], cvec[1:2], jnp.zeros((1, D), F32),
        dbin.reshape(NGRP, D),
        mvec[3:4], mvec[1:2], mvec[2:3],
        lru[0][2][0:8], lru[1][2][0:3],
        lru[1][2][3:8],
        dlb_rows,
        mvec[4:5],
        jnp.zeros((3, D), F32)], axis=0)
    g_pack, g_wri = _all_gather([pack, wri_piece], "gather_small_grads")

    dmx = g_pack[:, 0:3, :].reshape(NDEV, 3 * D)
    dmc = g_pack[:, 3:6, :].reshape(NDEV, 3 * D)
    grad_w_mod = _wmod_grad(c_all.T, c_ctx.reshape(D, 1), _local_cols(dmx, me, mcols), _local_cols(dmc, me, mcols),
                            "grad_w_mod").reshape(1, D, mcols)
    small_params = [(c_ctx.reshape(1, D), m_c_ctx.reshape(1, D), v_c_ctx.reshape(1, D)), (b_mod, m_b_mod, v_b_mod),
                    (b_in, m_b_in, v_b_in), (lb_logits, m_lb_logits, v_lb_logits), (norm_a_g, m_norm_a_g, v_norm_a_g),
                    (conv_w, m_conv_w, v_conv_w), (conv_b, m_conv_b, v_conv_b), (b_r, m_b_r, v_b_r),
                    (b_i, m_b_i, v_b_i), (lam, m_lam, v_lam), (ln_g, m_ln_g, v_ln_g), (ln_b, m_ln_b, v_ln_b)]
    loss_tile, small_g, small_upd = _finalize_small(g_pack, lb, w_mod_f, small_params)
    loss = loss_tile[0, 0]
    (grad_c_ctx, grad_b_mod, grad_b_in, grad_lb_logits, grad_norm_a_g, grad_conv_w, grad_conv_b, grad_b_r, grad_b_i,
     grad_lam, grad_ln_g, grad_ln_b) = small_g
    small_upd[0] = tuple(t.reshape(c_ctx.shape) for t in small_upd[0])
    (o_c_ctx, o_b_mod, o_b_in, o_lb, o_norm, o_conv_w, o_conv_b, o_b_r, o_b_i, o_lam, o_ln_g, o_ln_b) = small_upd

    half = 2 * NH * DH
    g_ri = g_wri.reshape(2 * half, DH)
    grad_w_r, grad_w_i = g_ri[:half].reshape(w_r.shape), g_ri[half:].reshape(w_i.shape)
    d_w_r, nm_w_r, nv_w_r = _adamw(grad_w_r, w_r, m_w_r, v_w_r, "update_w_r")
    d_w_i, nm_w_i, nv_w_i = _adamw(grad_w_i, w_i, m_w_i, v_w_i, "update_w_i")

    d_w_mod, nm_w_mod, nv_w_mod = _adamw(grad_w_mod, w_mod, m_w_mod, v_w_mod, "update_w_mod")

    grads = [grad_c_ctx.reshape(c_ctx.shape), grad_w_mod, grad_b_mod, g_w_in, grad_b_in, grad_lb_logits, grad_norm_a_g,
             grad_conv_w, grad_conv_b, grad_w_r, grad_b_r, grad_w_i, grad_b_i, grad_lam, g_p_a, g_p_b, g_w_out,
             grad_ln_g, grad_ln_b]
    per_kind = []
    for k in range(3):
        per_kind.append([
            o_c_ctx[k], (d_w_mod, nm_w_mod, nv_w_mod)[k], o_b_mod[k], (d_w_in, nm_w_in, nv_w_in)[k], o_b_in[k], o_lb[k],
            o_norm[k], o_conv_w[k], o_conv_b[k], (d_w_r, nm_w_r, nv_w_r)[k], o_b_r[k], (d_w_i, nm_w_i, nv_w_i)[k],
            o_b_i[k], o_lam[k], (d_p_a, nm_p_a, nv_p_a)[k], (d_p_b, nm_p_b, nv_p_b)[k], (d_w_out, nm_w_out, nv_w_out)[k],
            o_ln_g[k], o_ln_b[k]])
    return (loss, grad_x.reshape(x.shape), *grads, *per_kind[0], *per_kind[1], *per_kind[2])
```

```python
import functools

import jax
import jax.numpy as jnp
from jax import lax
from jax.experimental import pallas as pl
from jax.experimental.pallas import tpu as pltpu

F32 = jnp.float32
BF16 = jnp.bfloat16

D = 1024
NH = 8
DH = 128
CHUNK = 64
GLA_HEADS_PER_STEP = 8
GRID_W = 64
NGRP = 9
NDEV = 8
RG_C = 8.0
ALPHA = 2.0 ** 0.25
LN_EPS = 1e-5
RMS_EPS = 1e-6
Q_SCALE = DH ** -0.5
ADAM_LR, ADAM_B1, ADAM_B2, ADAM_EPS, ADAM_WD, ADAM_STEP = 1e-3, 0.9, 0.999, 1e-8, 0.01, 10
ADAM_C1 = 1.0 / (1.0 - ADAM_B1 ** ADAM_STEP)
ADAM_C2 = 1.0 / (1.0 - ADAM_B2 ** ADAM_STEP)

ANY = pl.BlockSpec(memory_space=pl.ANY)


def _sigmoid(t):
    return 1.0 / (1.0 + jnp.exp(-t))


def _dsilu(t, s):
    return s * (1.0 + t * (1.0 - s))


def _dot(a, b):
    return jnp.dot(a, b, preferred_element_type=F32)


def _dot_nt(a, b):
    return lax.dot_general(a, b, (((1,), (1,)), ((), ())), preferred_element_type=F32)


def _dot_tn(a, b):
    return lax.dot_general(a, b, (((0,), (0,)), ((), ())), preferred_element_type=F32)


def _my_index():
    return 4 * lax.axis_index("x") + 2 * lax.axis_index("y") + lax.axis_index("c")


def _dev_tuple(j):
    return (j >> 2, (j >> 1) & 1, j & 1)


def _exchange_sems(n):
    return [pltpu.SemaphoreType.DMA((n * NDEV,)), pltpu.SemaphoreType.DMA((n * NDEV,)), pltpu.SemaphoreType.DMA((n,))]


def _exchange(ins, outs, sems, piece_of=None):
    send_sems, recv_sems, loc_sems = sems
    n = len(ins)

    def src(a, p):
        return ins[a] if piece_of is None else piece_of(ins[a], a, p)

    def push(a, t):
        me, p = _my_index(), _step_peer(t)
        return pltpu.make_async_remote_copy(
            src_ref=src(a, p), dst_ref=outs[a].at[me],
            send_sem=send_sems.at[a * NDEV + t], recv_sem=recv_sems.at[a * NDEV + me],
            device_id=_dev_of(p), device_id_type=pl.DeviceIdType.MESH)

    def local(a):
        me = _my_index()
        return pltpu.make_async_copy(src(a, me), outs[a].at[me], loc_sems.at[a])

    def start():
        for a in range(n):
            local(a).start()
        for t in range(NDEV - 1):
            for a in range(n):
                push(a, t).start()

    def finish():
        me = _my_index()
        for t in range(NDEV - 1):
            for a in range(n):
                push(a, t).wait_send()
        for j in range(NDEV):
            @pl.when(me != j)
            def _():
                for a in range(n):
                    pltpu.make_async_remote_copy(
                        src_ref=src(a, j), dst_ref=outs[a].at[j],
                        send_sem=send_sems.at[a * NDEV], recv_sem=recv_sems.at[a * NDEV + j],
                        device_id=_dev_tuple(j), device_id_type=pl.DeviceIdType.MESH).wait_recv()
        for a in range(n):
            local(a).wait()

    return start, finish


def _all_gather(shards, name):
    n = len(shards)

    def body(*refs):
        start, finish = _exchange(refs[:n], refs[n:2 * n], refs[2 * n:])
        start()
        finish()

    return pl.pallas_call(
        body, name=name,
        out_shape=[jax.ShapeDtypeStruct((NDEV,) + s.shape, s.dtype) for s in shards],
        in_specs=[ANY] * n, out_specs=[ANY] * n, scratch_shapes=_exchange_sems(n),
    )(*shards)


def _pieces(parts, splits):
    shapes = []
    for part, split in zip(parts, splits):
        r, c = part.shape
        shapes.append((r // NDEV, c) if split == 0 else (r, c // NDEV))

    def piece_of(ref, a, j):
        pr, pc = shapes[a]
        if splits[a] == 0:
            start = j * pr if isinstance(j, int) else pl.multiple_of(j * pr, pr)
            return ref.at[pl.ds(start, pr), :]
        start = j * pc if isinstance(j, int) else pl.multiple_of(j * pc, pc)
        return ref.at[:, pl.ds(start, pc)]

    return shapes, piece_of


_STEP_MASKS = ((2, 4, 6, 3, 5, 7, 1, 0), (4, 2, 6, 5, 3, 7, 1, 0))
_GATHER_MASKS = ((0, 1, 2, 4, 3, 5, 6, 7), (0, 1, 4, 2, 5, 3, 6, 7))


def _peer_schedule(table):
    tab = jnp.array(table, jnp.int32)
    return jnp.bitwise_xor(_my_index(), tab[lax.axis_index("c")])


def _step_peer(s, table=_STEP_MASKS):
    def pick(row):
        if isinstance(s, int):
            return jnp.int32(row[s])
        m = jnp.int32(row[NDEV - 1])
        for t in range(NDEV - 2, -1, -1):
            m = jnp.where(s == t, jnp.int32(row[t]), m)
        return m
    mask = jnp.where(lax.axis_index("c") == 0, pick(table[0]), pick(table[1]))
    return jnp.bitwise_xor(_my_index(), mask)


def _dev_of(p):
    return (p // 4, (p // 2) % 2, p % 2)


def _dwin_exchange(u, dz, init, cs_init, name):
    m, ka = u.shape
    n = dz.shape[1]
    pc = n // NDEV
    tk = _row_tile(m, 512)
    nk = m // tk

    def body(pidx_ref, u_ref, dz_ref, init_ref, csi_ref, rwin, cs_ref, acc, sbuf, wsend, wrecv, wloc):
        s, k = pl.program_id(0), pl.program_id(1)
        me = _my_index()

        def slab_copy(slot, p):
            return pltpu.make_async_remote_copy(
                src_ref=sbuf.at[slot], dst_ref=rwin.at[me], send_sem=wsend.at[slot], recv_sem=wrecv.at[me],
                device_id=_dev_of(p), device_id_type=pl.DeviceIdType.MESH)

        @pl.when(k == 0)
        def _():
            acc[...] = init_ref[...]
            cs_ref[...] = csi_ref[...]

        bv = dz_ref[...]
        acc[...] += _dot_tn(u_ref[...], bv)
        cs_ref[...] += jnp.sum(bv.astype(F32), axis=0, keepdims=True)

        @pl.when(k == nk - 1)
        def _():
            slot = s % 2

            @pl.when(s >= 2)
            def _():
                slab_copy(slot, me).wait_send()

            sbuf[slot] = acc[...].astype(BF16)

            @pl.when(s < NDEV - 1)
            def _():
                slab_copy(slot, _step_peer(s)).start()

            @pl.when(s == NDEV - 1)
            def _():
                own = pltpu.make_async_copy(sbuf.at[slot], rwin.at[me], wloc.at[0])
                own.start()
                slab_copy(1 - slot, me).wait_send()
                for j in range(NDEV):
                    @pl.when(me != j)
                    def _():
                        pltpu.make_async_remote_copy(
                            src_ref=sbuf.at[0], dst_ref=rwin.at[j], send_sem=wsend.at[0], recv_sem=wrecv.at[j],
                            device_id=_dev_tuple(j), device_id_type=pl.DeviceIdType.MESH).wait_recv()
                own.wait()

    grid_spec = pltpu.PrefetchScalarGridSpec(
        num_scalar_prefetch=1, grid=(NDEV, nk),
        in_specs=[pl.BlockSpec((tk, ka), lambda s, k, pidx: (k, 0)),
                  pl.BlockSpec((tk, pc), lambda s, k, pidx: (k, pidx[s])),
                  pl.BlockSpec((ka, pc), lambda s, k, pidx: (0, pidx[s])),
                  pl.BlockSpec((1, pc), lambda s, k, pidx: (0, pidx[s]))],
        out_specs=[ANY, pl.BlockSpec((1, pc), lambda s, k, pidx: (0, pidx[s]))],
        scratch_shapes=[pltpu.VMEM((ka, pc), F32), pltpu.VMEM((2, ka, pc), BF16),
                        pltpu.SemaphoreType.DMA((2,)), pltpu.SemaphoreType.DMA((NDEV,)), pltpu.SemaphoreType.DMA((1,))])
    return pl.pallas_call(
        body, name=name, grid_spec=grid_spec,
        out_shape=[jax.ShapeDtypeStruct((NDEV, ka, pc), BF16), jax.ShapeDtypeStruct((1, n), F32)],
    )(_peer_schedule(_STEP_MASKS), u, dz, init, cs_init)


def _inproj_gather(u, w_loc, bias, name):
    m, k = u.shape
    pc = w_loc.shape[1]
    n = pc * NDEV
    tm = _row_tile(m, 512)
    ni = m // tm
    direct = (1, 2, 3, 6)
    relay_sem = {2: 4, 3: 5, 6: 7}

    def body(pidx_ref, u_ref, b_ref, wl_ref, z_ref, wall, wbuf, wsend, wrecv, ldsem, ownsem):
        s, i = pl.program_id(0), pl.program_id(1)
        me = _my_index()

        def shard_push(t):
            return pltpu.make_async_remote_copy(
                src_ref=wl_ref, dst_ref=wall.at[me], send_sem=wsend.at[t], recv_sem=wrecv.at[me],
                device_id=_dev_of(_step_peer(t, _GATHER_MASKS)), device_id_type=pl.DeviceIdType.MESH)

        def relay(t):
            p = _step_peer(t, _GATHER_MASKS)
            return pltpu.make_async_remote_copy(
                src_ref=wall.at[p], dst_ref=wall.at[p], send_sem=wsend.at[relay_sem[t]], recv_sem=wrecv.at[p],
                device_id=_dev_of(_step_peer(1, _GATHER_MASKS)), device_id_type=pl.DeviceIdType.MESH)

        def load(slot, src):
            return pltpu.make_async_copy(src, wbuf.at[slot], ldsem.at[slot])

        own = pltpu.make_async_copy(wl_ref, wall.at[me], ownsem.at[0])

        @pl.when((s == 0) & (i == 0))
        def _():
            own.start()
            load(0, wl_ref).start()
            for t in direct:
                shard_push(t).start()

        @pl.when((i == ni // 2) & (s < NDEV - 1))
        def _():
            nxt = _step_peer(s + 1, _GATHER_MASKS)
            pltpu.make_async_remote_copy(
                src_ref=wl_ref, dst_ref=wall.at[nxt], send_sem=wsend.at[0], recv_sem=wrecv.at[nxt],
                device_id=_dev_of(nxt), device_id_type=pl.DeviceIdType.MESH).wait_recv()
            for t in relay_sem:
                @pl.when(s + 1 == t)
                def _():
                    relay(t).start()
            load((s + 1) % 2, wall.at[nxt]).start()

        @pl.when(i == 0)
        def _():
            load(s % 2, wl_ref).wait()

        z_ref[...] = _dot(u_ref[...], wbuf[s % 2]) + b_ref[...]

        @pl.when((s == NDEV - 1) & (i == ni - 1))
        def _():
            own.wait()
            for t in direct:
                shard_push(t).wait_send()
            for t in relay_sem:
                relay(t).wait_send()

    grid_spec = pltpu.PrefetchScalarGridSpec(
        num_scalar_prefetch=1, grid=(NDEV, ni),
        in_specs=[pl.BlockSpec((tm, k), lambda s, i, pidx: (i, 0)),
                  pl.BlockSpec((1, pc), lambda s, i, pidx: (0, pidx[s])), ANY],
        out_specs=[pl.BlockSpec((tm, pc), lambda s, i, pidx: (i, pidx[s])), ANY],
        scratch_shapes=[pltpu.VMEM((2, k, pc), BF16),
                        pltpu.SemaphoreType.DMA((NDEV,)), pltpu.SemaphoreType.DMA((NDEV,)),
                        pltpu.SemaphoreType.DMA((2,)), pltpu.SemaphoreType.DMA((1,))])
    return pl.pallas_call(
        body, name=name, grid_spec=grid_spec,
        out_shape=[jax.ShapeDtypeStruct((m, n), F32), jax.ShapeDtypeStruct((NDEV, k, pc), w_loc.dtype)],
    )(_peer_schedule(_GATHER_MASKS), u, bias, w_loc)


def _adam_math(g, w, m, v):
    m2 = ADAM_B1 * m + (1.0 - ADAM_B1) * g
    v2 = ADAM_B2 * v + (1.0 - ADAM_B2) * (g * g)
    delta = -ADAM_LR * ((m2 * ADAM_C1) / (jnp.sqrt(v2 * ADAM_C2) + ADAM_EPS) + ADAM_WD * w)
    return delta, m2, v2


def _row_tile(r, cap):
    t = min(r, cap)
    while r % t:
        t //= 2
    return t


def _adamw(g, w, m, v, name):
    shape = w.shape
    cols = shape[-1] if w.ndim >= 2 and shape[-1] % 128 == 0 else 128
    g2, w2, m2, v2 = (t.reshape(-1, cols) for t in (g, w, m, v))
    r = g2.shape[0]
    tr = _row_tile(r, 256)

    def body(g_ref, w_ref, m_ref, v_ref, d_ref, mo_ref, vo_ref):
        d, mm, vv = _adam_math(g_ref[...], w_ref[...], m_ref[...], v_ref[...])
        d_ref[...] = d
        mo_ref[...] = mm
        vo_ref[...] = vv

    spec = pl.BlockSpec((tr, cols), lambda i: (i, 0))
    outs = pl.pallas_call(
        body, name=name, grid=(r // tr,),
        out_shape=[jax.ShapeDtypeStruct((r, cols), F32)] * 3,
        in_specs=[spec] * 4, out_specs=[spec] * 3,
    )(g2, w2, m2, v2)
    return tuple(o.reshape(shape) for o in outs)


def _sum_adamw(parts, w, m, v, name):
    _, r, c = parts.shape
    shape = w.shape
    w2, m2, v2 = (t.reshape(r, c) for t in (w, m, v))
    tr = _row_tile(r, 128)

    def body(p_ref, w_ref, m_ref, v_ref, g_ref, d_ref, mo_ref, vo_ref):
        g = p_ref[0].astype(F32)
        for k in range(1, NDEV):
            g = g + p_ref[k].astype(F32)
        d, mm, vv = _adam_math(g, w_ref[...], m_ref[...], v_ref[...])
        g_ref[...] = g
        d_ref[...] = d
        mo_ref[...] = mm
        vo_ref[...] = vv

    spec = pl.BlockSpec((tr, c), lambda i: (i, 0))
    outs = pl.pallas_call(
        body, name=name, grid=(r // tr,),
        out_shape=[jax.ShapeDtypeStruct((r, c), F32)] * 4,
        in_specs=[pl.BlockSpec((NDEV, tr, c), lambda i: (0, i, 0))] + [spec] * 3, out_specs=[spec] * 4,
    )(parts, w2, m2, v2)
    return tuple(o.reshape(shape) for o in outs)


def _sum_rows(parts, name):
    _, r, c = parts.shape

    def body(p_ref, o_ref):
        g = p_ref[0]
        for k in range(1, NDEV):
            g = g + p_ref[k]
        o_ref[...] = g

    return pl.pallas_call(
        body, name=name, out_shape=jax.ShapeDtypeStruct((r, c), F32),
    )(parts)


def _cast_bf16(arrays):
    n = len(arrays)

    def body(*refs):
        for src, dst in zip(refs[:n], refs[n:]):
            dst[...] = src[...].astype(BF16)

    return pl.pallas_call(
        body, name="cast_weights", out_shape=[jax.ShapeDtypeStruct(a.shape, BF16) for a in arrays],
    )(*arrays)


def _prep(cc, w_mod_full, b_mod, lbl):
    def body(cc_ref, w_ref, b_ref, l_ref, mod_ref, lb_ref):
        t = cc_ref[...]
        s = (t * _sigmoid(t)).astype(BF16)
        mod_ref[...] = _dot(s, w_ref[...]) + b_ref[...]
        lb_ref[...] = _sigmoid(l_ref[0] - l_ref[1])

    return pl.pallas_call(
        body, name="prep",
        out_shape=[jax.ShapeDtypeStruct((8, 3 * D), F32), jax.ShapeDtypeStruct((8, D), F32)],
    )(cc, w_mod_full, b_mod, lbl)


def _modulate(xin, mod, row, name):
    n = xin.shape[0]
    tm = _row_tile(n, 512)

    def body(x_ref, mod_ref, u_ref):
        sh = mod_ref[row:row + 1, 0:D]
        sc = mod_ref[row:row + 1, D:2 * D]
        u_ref[...] = (x_ref[...] * (1.0 + sc) + sh).astype(BF16)

    return pl.pallas_call(
        body, name=name, grid=(n // tm,),
        out_shape=jax.ShapeDtypeStruct((n, D), BF16),
        in_specs=[pl.BlockSpec((tm, D), lambda i: (i, 0)), pl.BlockSpec((8, 3 * D), lambda i: (0, 0))],
        out_specs=pl.BlockSpec((tm, D), lambda i: (i, 0)),
    )(xin, mod)


def _mm_bias(a, w_all, bias, name):
    m, k = a.shape
    tn = w_all.shape[2]
    n = tn * NDEV
    tm = _row_tile(m, 512)

    def body(a_ref, b_ref, bias_ref, o_ref):
        o_ref[...] = _dot(a_ref[...], b_ref[0]) + bias_ref[...]

    return pl.pallas_call(
        body, name=name, grid=(NDEV, m // tm),
        out_shape=jax.ShapeDtypeStruct((m, n), F32),
        in_specs=[pl.BlockSpec((tm, k), lambda j, i: (i, 0)), pl.BlockSpec((1, k, tn), lambda j, i: (j, 0, 0)),
                  pl.BlockSpec((1, tn), lambda j, i: (0, j))],
        out_specs=pl.BlockSpec((tm, tn), lambda j, i: (i, j)),
    )(a, w_all, bias)


def _mm_tn(a, b, init, name, with_colsum=False, colsum_init=None, out_dtype=F32):
    m, ka = a.shape
    n = b.shape[1]
    tk = _row_tile(m, 512)
    tn = 1024
    nk = m // tk
    has_init = init is not None

    def body(*refs):
        a_ref, b_ref = refs[0], refs[1]
        pos = 2
        init_ref = cs_init_ref = None
        if has_init:
            init_ref = refs[pos]
            pos += 1
            if with_colsum:
                cs_init_ref = refs[pos]
                pos += 1
        o_ref = refs[pos]
        cs_ref = refs[pos + 1] if with_colsum else None
        acc = refs[-1]
        k = pl.program_id(1)

        @pl.when(k == 0)
        def _():
            if has_init:
                acc[...] = init_ref[...]
                if with_colsum:
                    cs_ref[...] = cs_init_ref[...]
            else:
                acc[...] = jnp.zeros_like(acc)
                if with_colsum:
                    cs_ref[...] = jnp.zeros_like(cs_ref)

        bv = b_ref[...]
        acc[...] += _dot_tn(a_ref[...], bv)
        if with_colsum:
            cs_ref[...] += jnp.sum(bv.astype(F32), axis=0, keepdims=True)

        @pl.when(k == nk - 1)
        def _():
            o_ref[...] = acc[...].astype(out_dtype)

    in_specs = [pl.BlockSpec((tk, ka), lambda j, k: (k, 0)), pl.BlockSpec((tk, tn), lambda j, k: (k, j))]
    args = [a, b]
    if has_init:
        in_specs.append(pl.BlockSpec((ka, tn), lambda j, k: (0, j)))
        args.append(init)
        if with_colsum:
            in_specs.append(pl.BlockSpec((1, tn), lambda j, k: (0, j)))
            args.append(colsum_init)
    out_shape = [jax.ShapeDtypeStruct((ka, n), out_dtype)]
    out_specs = [pl.BlockSpec((ka, tn), lambda j, k: (0, j))]
    if with_colsum:
        out_shape.append(jax.ShapeDtypeStruct((1, n), F32))
        out_specs.append(pl.BlockSpec((1, tn), lambda j, k: (0, j)))
    outs = pl.pallas_call(
        body, name=name, grid=(n // tn, nk), out_shape=out_shape, in_specs=in_specs, out_specs=out_specs,
        scratch_shapes=[pltpu.VMEM((ka, tn), F32)],
    )(*args)
    return outs if with_colsum else outs[0]


def _input_grad(dz, w_all, xin, dr, mod, row, name, side=(), side_splits=()):
    m, n = dz.shape
    tm = _row_tile(m, 512)
    tk = w_all.shape[2]
    nk = NDEV
    ni = m // tm
    has_dr = dr is not None
    ns = len(side)
    piece_shapes, piece_of = _pieces(side, side_splits)

    def body(*refs):
        dz_ref, w_ref, x_ref = refs[:3]
        pos = 3
        dr_ref = refs[pos] if has_dr else None
        pos += int(has_dr)
        mod_ref = refs[pos]
        side_in = refs[pos + 1:pos + 1 + ns]
        pos += 1 + ns
        gx_ref = refs[pos] if has_dr else None
        pos += int(has_dr)
        vec_ref = refs[pos]
        side_out = refs[pos + 1:pos + 1 + ns]
        acc = refs[pos + 1 + ns]
        i, k = pl.program_id(0), pl.program_id(1)
        if ns:
            side_start, side_finish = _exchange(side_in, side_out, refs[pos + 2 + ns:], piece_of)

            @pl.when((i == 0) & (k == 0))
            def _():
                side_start()

        @pl.when(k == 0)
        def _():
            acc[...] = jnp.zeros_like(acc)

        @pl.when((i == 0) & (k == 0))
        def _():
            vec_ref[...] = jnp.zeros_like(vec_ref)

        acc[...] += _dot_nt(dz_ref[...], w_ref[0])

        @pl.when(k == nk - 1)
        def _():
            du = acc[...]
            xv = x_ref[...]
            if has_dr:
                sc = mod_ref[row:row + 1, D:2 * D]
                gx_ref[...] = ALPHA * dr_ref[...] + du * (1.0 + sc)
            vec_ref[0:1, :] += jnp.sum(du, axis=0, keepdims=True)
            vec_ref[1:2, :] += jnp.sum(du * xv, axis=0, keepdims=True)

        if ns:
            @pl.when((i == ni - 1) & (k == nk - 1))
            def _():
                side_finish()

    row_spec = pl.BlockSpec((tm, D), lambda i, k: (i, 0))
    in_specs = [pl.BlockSpec((tm, tk), lambda i, k: (i, k)), pl.BlockSpec((1, D, tk), lambda i, k: (k, 0, 0)), row_spec]
    args = [dz, w_all, xin]
    if has_dr:
        in_specs.append(row_spec)
        args.append(dr)
    in_specs.append(pl.BlockSpec((8, 3 * D), lambda i, k: (0, 0)))
    args.append(mod)
    in_specs += [ANY] * ns
    args += list(side)
    out_shape, out_specs = [], []
    if has_dr:
        out_shape.append(jax.ShapeDtypeStruct((m, D), F32))
        out_specs.append(row_spec)
    out_shape.append(jax.ShapeDtypeStruct((8, D), F32))
    out_specs.append(pl.BlockSpec((8, D), lambda i, k: (0, 0)))
    out_shape += [jax.ShapeDtypeStruct((NDEV,) + piece_shapes[a], side[a].dtype) for a in range(ns)]
    out_specs += [ANY] * ns
    outs = pl.pallas_call(
        body, name=name, grid=(ni, nk), out_shape=out_shape, in_specs=in_specs, out_specs=out_specs,
        scratch_shapes=[pltpu.VMEM((tm, D), F32)] + (_exchange_sems(ns) if ns else []),
    )(*args)
    return tuple(outs) if has_dr else (None, *outs)


def _tri(reverse):
    r = lax.broadcasted_iota(jnp.int32, (CHUNK, CHUNK), 0)
    c = lax.broadcasted_iota(jnp.int32, (CHUNK, CHUNK), 1)
    return (c >= r) if reverse else (c <= r)


def _cum_f32(tri_b, t):
    hi = t.astype(BF16)
    r1 = t - hi.astype(F32)
    mid = r1.astype(BF16)
    lo = (r1 - mid.astype(F32)).astype(BF16)
    return _dot(tri_b, hi) + _dot(tri_b, mid) + _dot(tri_b, lo)


def _gla_features(zq, zf, lb):
    sq = _sigmoid(zq)
    q = zq * sq * Q_SCALE
    sf = _sigmoid(zf)
    f = lb + (1.0 - lb) * sf
    return q, sq, f, sf


def _gla_block(n):
    return 256 if n % 256 == 0 else CHUNK


def _gla_fwd(z, lb, s0, d, name, side=()):
    n = z.shape[0]
    blk = _gla_block(n)
    nb, npb = n // blk, blk // CHUNK
    reverse = d == 1
    last = 0 if reverse else CHUNK - 1
    order = list(range(npb))[::-1] if reverse else list(range(npb))
    ns = len(side)

    def bmap(i):
        return nb - 1 - i if reverse else i

    hp = GLA_HEADS_PER_STEP
    hw = hp * DH
    units = [(hh, cidx) for hh in range(hp) for cidx in order]

    def body(zq_ref, zf_ref, zv_ref, lb_ref, s0_ref, *rest):
        side_in = rest[:ns]
        o_ref, ss_ref, sf_ref = rest[ns:ns + 3]
        side_out = rest[ns + 3:2 * ns + 3]
        st = rest[2 * ns + 3]
        i = pl.program_id(1)
        if ns:
            side_start, side_finish = _exchange(side_in, side_out, rest[2 * ns + 4:])

            @pl.when((pl.program_id(0) == 0) & (i == 0))
            def _():
                side_start()

        @pl.when(i == 0)
        def _():
            st[...] = s0_ref[...]

        mask = _tri(reverse)
        tri_b = jnp.where(mask, 1.0, 0.0).astype(BF16)
        feat = {}
        for u in units:
            hh, cidx = u
            rows, cols = pl.ds(cidx * CHUNK, CHUNK), pl.ds(hh * DH, DH)
            q, _, f, _ = _gla_features(zq_ref[rows, cols], zf_ref[rows, cols], lb_ref[d:d + 1, cols])
            feat[u] = (q, 1.0 - f, jnp.log(f), zv_ref[rows, cols].astype(BF16))
        dec = {u: _cum_f32(tri_b, feat[u][2]) for u in units}
        ops = {}
        for u in units:
            q, k, _, vb = feat[u]
            g = dec[u]
            gl = g[last:last + 1, :]
            ops[u] = ((q * jnp.exp(g)).astype(BF16), (k * jnp.exp(-g)).astype(BF16),
                      (k * jnp.exp(gl - g)).astype(BF16), jnp.exp(gl), vb)
        att = {u: jnp.where(mask, _dot_nt(ops[u][0], ops[u][1]), 0.0).astype(BF16) for u in units}
        upd = {u: _dot_tn(ops[u][4], ops[u][2]) for u in units}
        intra = {u: _dot(att[u], ops[u][4]) for u in units}
        s_in = {}
        for hh in range(hp):
            s = st[hh]
            for cidx in order:
                s_in[(hh, cidx)] = s
                s = s * ops[(hh, cidx)][3] + upd[(hh, cidx)]
            st[hh] = s
            sf_ref[hh] = s
        for u in units:
            hh, cidx = u
            rows, cols = pl.ds(cidx * CHUNK, CHUNK), pl.ds(hh * DH, DH)
            o_ref[rows, cols] = intra[u] + _dot_nt(ops[u][0], s_in[u].astype(BF16))
            ss_ref[hh, cidx] = s_in[u]

        if ns:
            @pl.when((pl.program_id(0) == NH // hp - 1) & (i == nb - 1))
            def _():
                side_finish()

    def col(g):
        return lambda h, i: (bmap(i), g * (NH // hp) + h)

    return pl.pallas_call(
        body, name=name, grid=(NH // hp, nb),
        out_shape=[jax.ShapeDtypeStruct((n, D), F32), jax.ShapeDtypeStruct((NH, n // CHUNK, DH, DH), F32),
                   jax.ShapeDtypeStruct((NH, DH, DH), F32)]
        + [jax.ShapeDtypeStruct((NDEV,) + t.shape, t.dtype) for t in side],
        in_specs=[pl.BlockSpec((blk, hw), col(0)), pl.BlockSpec((blk, hw), col(1 + d)),
                  pl.BlockSpec((blk, hw), col(3)), pl.BlockSpec((8, hw), lambda h, i: (0, h)),
                  pl.BlockSpec((hp, DH, DH), lambda h, i: (h, 0, 0))] + [ANY] * ns,
        out_specs=[pl.BlockSpec((blk, hw), lambda h, i: (bmap(i), h)),
                   pl.BlockSpec((hp, npb, DH, DH), lambda h, i: (h, bmap(i), 0, 0)),
                   pl.BlockSpec((hp, DH, DH), lambda h, i: (h, 0, 0))] + [ANY] * ns,
        scratch_shapes=[pltpu.VMEM((hp, DH, DH), F32)] + (_exchange_sems(ns) if ns else []),
    )(z, z, z, lb, s0, *side)


def _gla_bwd(z, lb, s_start, do, ds_fin, acc_q, acc_v, d, name, f_dtype=F32, into=None, side=(), side_splits=()):
    n = z.shape[0]
    blk = _gla_block(n)
    nb, npb = n // blk, blk // CHUNK
    reverse = d == 1
    last = 0 if reverse else CHUNK - 1
    order = list(range(npb)) if reverse else list(range(npb))[::-1]
    has_do = do is not None
    has_acc = acc_q is not None
    fused = into is not None
    assert not fused or d == 1
    ns = len(side)
    assert not (fused and ns)
    piece_shapes, piece_of = _pieces(side, side_splits)
    hp = NH if fused else GLA_HEADS_PER_STEP
    hw = hp * DH
    units = [(hh, cidx) for hh in range(hp) for cidx in order]

    def bmap(i):
        return i if reverse else nb - 1 - i

    def body(*refs):
        zq_ref, zf_ref, zv_ref, lb_ref, ss_ref, dsf_ref = refs[:6]
        pos = 6
        do_ref = aq_ref = av_ref = None
        if has_do:
            do_ref = refs[pos]
            pos += 1
        if has_acc:
            aq_ref, av_ref = refs[pos], refs[pos + 1]
            pos += 2
        if fused:
            other_ref = refs[pos + 1]
            dz_ref, dlb_ref, ds0_ref, dst = refs[pos + 2:]
            dz_ref[:, D:2 * D] = other_ref[...]
        else:
            side_in = refs[pos:pos + ns]
            dzq_ref, dzf_ref, dzv_ref, dlb_ref, ds0_ref = refs[pos + ns:pos + ns + 5]
            side_out = refs[pos + ns + 5:pos + 2 * ns + 5]
            dst = refs[pos + 2 * ns + 5]
        i = pl.program_id(1)
        if ns:
            side_start, side_finish = _exchange(side_in, side_out, refs[pos + 2 * ns + 6:], piece_of)

            @pl.when((pl.program_id(0) == 0) & (i == 0))
            def _():
                side_start()

        @pl.when(i == 0)
        def _():
            dst[...] = dsf_ref[...]
            dlb_ref[...] = jnp.zeros_like(dlb_ref)

        mask = _tri(reverse)
        tri_b = jnp.where(mask, 1.0, 0.0).astype(BF16)
        tri_t = jnp.where(_tri(not reverse), 1.0, 0.0).astype(BF16)

        def where(u):
            return pl.ds(u[1] * CHUNK, CHUNK), pl.ds(u[0] * DH, DH)

        feat = {}
        for u in units:
            rows, cols = where(u)
            zq, zf = zq_ref[rows, cols], zf_ref[rows, cols]
            lbv = lb_ref[d:d + 1, cols]
            q, sq, f, sf = _gla_features(zq, zf, lbv)
            feat[u] = dict(zq=zq, q=q, sq=sq, f=f, sf=sf, lbv=lbv, k=1.0 - f, vb=zv_ref[rows, cols].astype(BF16))
        dec = {u: _cum_f32(tri_b, jnp.log(feat[u]["f"])) for u in units}
        for u in units:
            w = feat[u]
            g = dec[u]
            gl = g[last:last + 1, :]
            w["eg"], w["egi"], w["ege"], w["egl"] = jnp.exp(g), jnp.exp(-g), jnp.exp(gl - g), jnp.exp(gl)
            w["qd"], w["ki"], w["ke"] = w["q"] * w["eg"], w["k"] * w["egi"], w["k"] * w["ege"]
            w["qdb"], w["kib"], w["keb"] = w["qd"].astype(BF16), w["ki"].astype(BF16), w["ke"].astype(BF16)
            w["s_in"] = ss_ref[u[0], u[1]]
        if has_do:
            for u in units:
                w = feat[u]
                rows, cols = where(u)
                w["dob"] = do_ref[rows, cols].astype(BF16)
            for u in units:
                w = feat[u]
                w["a"] = jnp.where(mask, _dot_nt(w["qdb"], w["kib"]), 0.0).astype(BF16)
                w["da"] = jnp.where(mask, _dot_nt(w["dob"], w["vb"]), 0.0).astype(BF16)
                w["m"] = _dot_tn(w["dob"], w["qdb"])
        for hh in range(hp):
            ds = dst[hh]
            for cidx in order:
                w = feat[(hh, cidx)]
                w["ds"] = ds
                ds = ds * w["egl"]
                if has_do:
                    ds = ds + w["m"]
            dst[hh] = ds
            ds0_ref[hh] = ds
        for u in units:
            w = feat[u]
            dsb = w["ds"].astype(BF16)
            w["dke"] = _dot(w["vb"], dsb)
            w["dv"] = _dot_nt(w["keb"], dsb)
            if has_do:
                w["dv"] = w["dv"] + _dot_tn(w["a"], w["dob"])
                w["dqd"] = _dot(w["da"], w["kib"]) + _dot(w["dob"], w["s_in"].astype(BF16))
                w["dki"] = _dot_tn(w["da"], w["qdb"])
        for u in units:
            w = feat[u]
            dkeke = w["dke"] * w["ke"]
            w["dgl"] = (w["egl"] * jnp.sum(w["s_in"] * w["ds"], axis=0, keepdims=True)
                        + jnp.sum(dkeke, axis=0, keepdims=True))
            dg = -dkeke
            dk = w["dke"] * w["ege"]
            if has_do:
                dg = dg + w["dqd"] * w["qd"] - w["dki"] * w["ki"]
                dk = dk + w["dki"] * w["egi"]
            w["dg"], w["dk"] = dg, dk
        dlf = {u: _cum_f32(tri_t, feat[u]["dg"]) for u in units}
        for u in units:
            w = feat[u]
            rows, cols = where(u)
            df = (dlf[u] + w["dgl"]) / w["f"] - w["dk"]
            sf = w["sf"]
            dzf = df * (1.0 - w["lbv"]) * sf * (1.0 - sf)
            dlb_ref[0:1, cols] += jnp.sum(df * (1.0 - sf), axis=0, keepdims=True)
            if has_do:
                dzq = w["dqd"] * w["eg"] * (Q_SCALE * _dsilu(w["zq"], w["sq"]))
            else:
                dzq = jnp.zeros((CHUNK, DH), F32)
            dv = w["dv"]
            if has_acc:
                dzq = dzq + aq_ref[rows, cols]
                dv = dv + av_ref[rows, cols]
            if fused:
                lane = u[0] * DH
                dz_ref[rows, pl.ds(lane, DH)] = dzq.astype(BF16)
                dz_ref[rows, pl.ds(2 * D + lane, DH)] = dzf.astype(BF16)
                dz_ref[rows, pl.ds(3 * D + lane, DH)] = dv.astype(BF16)
            else:
                dzq_ref[rows, cols] = dzq
                dzf_ref[rows, cols] = dzf.astype(f_dtype)
                dzv_ref[rows, cols] = dv

        if ns:
            @pl.when((pl.program_id(0) == NH // hp - 1) & (i == nb - 1))
            def _():
                side_finish()

    def col(g):
        return lambda h, i: (bmap(i), g * (NH // hp) + h)

    tok = pl.BlockSpec((blk, hw), lambda h, i: (bmap(i), h))
    state = pl.BlockSpec((hp, DH, DH), lambda h, i: (h, 0, 0))
    in_specs = [pl.BlockSpec((blk, hw), col(0)), pl.BlockSpec((blk, hw), col(1 + d)), pl.BlockSpec((blk, hw), col(3)),
                pl.BlockSpec((8, hw), lambda h, i: (0, h)),
                pl.BlockSpec((hp, npb, DH, DH), lambda h, i: (h, bmap(i), 0, 0)), state]
    args = [z, z, z, lb, s_start, ds_fin]
    if has_do:
        in_specs.append(tok)
        args.append(do)
    if has_acc:
        in_specs += [tok, tok]
        args += [acc_q, acc_v]
    tail_shape = [jax.ShapeDtypeStruct((8, D), F32), jax.ShapeDtypeStruct((NH, DH, DH), F32)]
    tail_specs = [pl.BlockSpec((8, hw), lambda h, i: (0, h)), state]
    if fused:
        buf, other = into
        aliases = {len(args): 0}
        in_specs += [ANY, tok]
        args += [buf, other]
        out_shape = [jax.ShapeDtypeStruct(buf.shape, buf.dtype)] + tail_shape
        out_specs = [pl.BlockSpec((blk, 4 * D), lambda h, i: (bmap(i), 0))] + tail_specs
    else:
        aliases = {}
        in_specs += [ANY] * ns
        args += list(side)
        out_shape = [jax.ShapeDtypeStruct((n, D), F32), jax.ShapeDtypeStruct((n, D), f_dtype),
                     jax.ShapeDtypeStruct((n, D), F32)] + tail_shape
        out_shape += [jax.ShapeDtypeStruct((NDEV,) + piece_shapes[a], side[a].dtype) for a in range(ns)]
        out_specs = [tok, tok, tok] + tail_specs + [ANY] * ns
    return pl.pallas_call(
        body, name=name, grid=(NH // hp, nb), out_shape=out_shape, in_specs=in_specs, out_specs=out_specs,
        input_output_aliases=aliases,
        scratch_shapes=[pltpu.VMEM((hp, DH, DH), F32)] + (_exchange_sems(ns) if ns else []),
    )(*args)


def _shift(t, s, fill, down):
    n = t.shape[0]
    rows = lax.broadcasted_iota(jnp.int32, t.shape, 0)
    if down:
        return jnp.where(rows >= s, pltpu.roll(t, s, 0), fill)
    return jnp.where(rows < n - s, pltpu.roll(t, n - s, 0), fill)


SUBLANES = 8
LRU_SAVED = 4


def _chain_scan(a, b, h_in, down):
    n = a.shape[0]
    ng = n // SUBLANES
    rows = lax.broadcasted_iota(jnp.int32, (SUBLANES, a.shape[1]), 0)
    local = []
    for g in range(ng):
        aa, bb = a[g * SUBLANES:(g + 1) * SUBLANES], b[g * SUBLANES:(g + 1) * SUBLANES]
        for s in (1, 2, 4):
            if down:
                keep, amt = rows >= s, s
            else:
                keep, amt = rows < SUBLANES - s, SUBLANES - s
            bb = bb + aa * jnp.where(keep, pltpu.roll(bb, amt, 0), 0.0)
            aa = aa * jnp.where(keep, pltpu.roll(aa, amt, 0), 1.0)
        local.append((aa, bb))
    out = [None] * ng
    carry = h_in
    for g in (range(ng) if down else range(ng - 1, -1, -1)):
        aa, bb = local[g]
        hg = bb + aa * carry
        out[g] = hg
        carry = hg[SUBLANES - 1:SUBLANES] if down else hg[0:1]
    return (jnp.concatenate(out, axis=0) if ng > 1 else out[0]), carry


def _conv_taps(xv):
    return (_shift(xv, 1, 0.0, True), xv, _shift(xv, 1, 0.0, False), _shift(xv, 2, 0.0, False))


def _conv(taps, cw, cb):
    return cb + cw[0:1, :] * taps[0] + cw[1:2, :] * taps[1] + cw[2:3, :] * taps[2] + cw[3:4, :] * taps[3]


def _neg_expm1(t):
    series = -t * (1.0 + t * (0.5 + t * (1.0 / 6.0 + t * (1.0 / 24.0 + t * (1.0 / 120.0)))))
    return jnp.where(t > -0.1, series, 1.0 - jnp.exp(t))


def _lru_gates(xc, wr, br, wi, bi, lam):
    xcb = xc.astype(BF16)
    r = _sigmoid(_dot(xcb, wr) + br)
    gi = _sigmoid(_dot(xcb, wi) + bi)
    sp = jnp.maximum(-lam, 0.0) + jnp.log(1.0 + jnp.exp(-jnp.abs(lam)))
    la = -RG_C * r * sp
    a = jnp.exp(la)
    mult = jnp.sqrt(_neg_expm1(2.0 * la))
    return xcb, r, gi, sp, a, mult


def _lru_fwd(xin, blk, cw, cb, wr, br, wi, bi, lam, h0, acc_h, d, name):
    n = xin.shape[0]
    nb = n // blk
    reverse = d == 1
    down = not reverse
    has_acc = acc_h is not None

    def bmap(i):
        return nb - 1 - i if reverse else i

    def body(*refs):
        x_ref, cw_ref, cb_ref, wr_ref, br_ref, wi_ref, bi_ref, lam_ref, h0_ref = refs[:9]
        pos = 9
        acc_ref = refs[pos] if has_acc else None
        pos += int(has_acc)
        h_ref, hin_ref, hfin_ref, sav_a_ref, sav_ref = refs[pos:pos + 5]
        pos += 5
        hsum_ref = refs[pos] if has_acc else None
        carry = refs[-1]
        i = pl.program_id(0)

        @pl.when(i == 0)
        def _():
            carry[...] = h0_ref[...]

        for g in range(NH):
            cols = pl.ds(g * DH, DH)
            xc = _conv(_conv_taps(x_ref[:, cols]), cw_ref[:, cols], cb_ref[:, cols])
            _, r, gi, _, a, mult = _lru_gates(xc, wr_ref[g], br_ref[:, cols], wi_ref[g], bi_ref[:, cols],
                                              lam_ref[:, cols])
            sav_a_ref[:, cols] = a
            for slot, val in enumerate((xc, r, gi, mult)):
                sav_ref[slot, :, cols] = val.astype(BF16)
            hin = carry[:, cols]
            h, h_last = _chain_scan(a, mult * gi * xc, hin, down)
            h_ref[:, cols] = h
            if has_acc:
                hsum_ref[:, cols] = h + acc_ref[:, cols]
            hin_ref[0, :, cols] = hin
            carry[:, cols] = h_last
            hfin_ref[:, cols] = h_last

    vec = pl.BlockSpec((1, D), lambda i: (0, 0))
    wsp = pl.BlockSpec((NH, DH, DH), lambda i: (0, 0, 0))
    tok = pl.BlockSpec((blk, D), lambda i: (bmap(i), 0))
    in_specs = [tok, pl.BlockSpec((4, D), lambda i: (0, 0)), vec, wsp, vec, wsp, vec, vec, vec]
    args = [xin, cw, cb, wr, br, wi, bi, lam, h0]
    out_shape = [jax.ShapeDtypeStruct((n, D), F32), jax.ShapeDtypeStruct((nb, 1, D), F32),
                 jax.ShapeDtypeStruct((1, D), F32), jax.ShapeDtypeStruct((n, D), F32),
                 jax.ShapeDtypeStruct((LRU_SAVED, n, D), BF16)]
    out_specs = [tok, pl.BlockSpec((1, 1, D), lambda i: (bmap(i), 0, 0)), vec, tok,
                 pl.BlockSpec((LRU_SAVED, blk, D), lambda i: (0, bmap(i), 0))]
    if has_acc:
        in_specs.append(tok)
        args.append(acc_h)
        out_shape.append(jax.ShapeDtypeStruct((n, D), F32))
        out_specs.append(tok)
    return pl.pallas_call(
        body, name=name, grid=(nb,), out_shape=out_shape, in_specs=in_specs, out_specs=out_specs,
        scratch_shapes=[pltpu.VMEM((1, D), F32)],
    )(*args)


def _lru_bwd(xin, blk, cw, wr, wi, lam, sav, h, hin, dh, cg_fin, acc_dx, init, d, name, dx_dtype=F32):
    n = xin.shape[0]
    nb = n // blk
    reverse = d == 1
    down = not reverse
    first = blk - 1 if reverse else 0
    has_dh = dh is not None
    has_acc = acc_dx is not None
    has_init = init is not None

    def bmap(i):
        return i if reverse else nb - 1 - i

    def body(*refs):
        (x_ref, cw_ref, wr_ref, wi_ref, lam_ref, sav_a_ref, sav_ref, h_ref, hin_ref, cgf_ref) = refs[:10]
        pos = 10
        dh_ref = acc_ref = None
        iwr_ref = iwi_ref = ivec_ref = None
        if has_dh:
            dh_ref = refs[pos]
            pos += 1
        if has_acc:
            acc_ref = refs[pos]
            pos += 1
        if has_init:
            iwr_ref, iwi_ref, ivec_ref = refs[pos:pos + 3]
            pos += 3
        dx_ref, dwr_ref, dwi_ref, vec_ref, cg0_ref, carry = refs[pos:]
        i = pl.program_id(0)

        @pl.when(i == 0)
        def _():
            carry[...] = cgf_ref[...]
            if has_init:
                dwr_ref[...] = iwr_ref[...]
                dwi_ref[...] = iwi_ref[...]
                vec_ref[...] = ivec_ref[...]
            else:
                dwr_ref[...] = jnp.zeros_like(dwr_ref)
                dwi_ref[...] = jnp.zeros_like(dwi_ref)
                vec_ref[...] = jnp.zeros_like(vec_ref)

        for g in range(NH):
            cols = pl.ds(g * DH, DH)
            cwv = cw_ref[:, cols]
            lam_v = lam_ref[:, cols]
            taps = _conv_taps(x_ref[:, cols])
            wr_g, wi_g = wr_ref[g], wi_ref[g]
            a = sav_a_ref[:, cols]
            xcb = sav_ref[0, :, cols]
            xc, r, gi, mult = (sav_ref[slot, :, cols].astype(F32) for slot in range(LRU_SAVED))
            sp = jnp.maximum(-lam_v, 0.0) + jnp.log(1.0 + jnp.exp(-jnp.abs(lam_v)))
            hprev = _shift(h_ref[:, cols], 1, hin_ref[0, :, cols], down)
            a_next = _shift(a, 1, 1.0, not down)
            dhv = dh_ref[:, cols] if has_dh else jnp.zeros_like(a)
            e, _ = _chain_scan(a_next, dhv, carry[:, cols], not down)
            cg = a[first:first + 1, :] * e[first:first + 1, :]
            carry[:, cols] = cg
            cg0_ref[:, cols] = cg
            da = e * hprev
            emult = e * mult
            dgi = emult * xc
            dxc = emult * gi
            dla = da * a - (e * gi * xc) * (a * a) / mult
            dr = dla * (-RG_C * sp)
            sneg = 1.0 - _sigmoid(lam_v)
            dpr = dr * r * (1.0 - r)
            dpi = dgi * gi * (1.0 - gi)
            dprb, dpib = dpr.astype(BF16), dpi.astype(BF16)
            dxc = dxc + _dot_nt(dprb, wr_g) + _dot_nt(dpib, wi_g)
            dwr_ref[g] += _dot_tn(xcb, dprb)
            dwi_ref[g] += _dot_tn(xcb, dpib)
            dx = (cwv[0:1, :] * _shift(dxc, 1, 0.0, False) + cwv[1:2, :] * dxc
                  + cwv[2:3, :] * _shift(dxc, 1, 0.0, True) + cwv[3:4, :] * _shift(dxc, 2, 0.0, True))
            if has_acc:
                dx = dx + acc_ref[:, cols]
            dx_ref[:, cols] = dx.astype(dx_dtype)
            vec_ref[0:1, cols] += jnp.sum(dpr, axis=0, keepdims=True)
            vec_ref[1:2, cols] += jnp.sum(dpi, axis=0, keepdims=True)
            vec_ref[2:3, cols] += jnp.sum(dla * r, axis=0, keepdims=True) * (RG_C * sneg)
            vec_ref[3:4, cols] += jnp.sum(dxc, axis=0, keepdims=True)
            for kk in range(4):
                vec_ref[4 + kk:5 + kk, cols] += jnp.sum(dxc * taps[kk], axis=0, keepdims=True)

    vec = pl.BlockSpec((1, D), lambda i: (0, 0))
    wsp = pl.BlockSpec((NH, DH, DH), lambda i: (0, 0, 0))
    tok = pl.BlockSpec((blk, D), lambda i: (bmap(i), 0))
    vec16 = pl.BlockSpec((16, D), lambda i: (0, 0))
    in_specs = [tok, pl.BlockSpec((4, D), lambda i: (0, 0)), wsp, wsp, vec, tok,
                pl.BlockSpec((LRU_SAVED, blk, D), lambda i: (0, bmap(i), 0)), tok,
                pl.BlockSpec((1, 1, D), lambda i: (bmap(i), 0, 0)), vec]
    args = [xin, cw, wr, wi, lam, sav[0], sav[1], h, hin, cg_fin]
    if has_dh:
        in_specs.append(tok)
        args.append(dh)
    if has_acc:
        in_specs.append(tok)
        args.append(acc_dx)
    if has_init:
        in_specs += [wsp, wsp, vec16]
        args += list(init)
    return pl.pallas_call(
        body, name=name, grid=(nb,),
        out_shape=[jax.ShapeDtypeStruct((n, D), dx_dtype), jax.ShapeDtypeStruct((NH, DH, DH), F32),
                   jax.ShapeDtypeStruct((NH, DH, DH), F32), jax.ShapeDtypeStruct((16, D), F32),
                   jax.ShapeDtypeStruct((1, D), F32)],
        in_specs=in_specs, out_specs=[tok, wsp, wsp, vec16, vec],
        scratch_shapes=[pltpu.VMEM((1, D), F32)],
    )(*args)


def _merge(z, o_f, o_b, hx, xin, tgt, mod, gn, p_a, p_b, w_out, ln_g, ln_b):
    n = xin.shape[0]
    tm = _row_tile(n, 128)

    def body(z4_ref, z6_ref, z7_ref, z8_ref, of_ref, ob_ref, hx_ref, x_ref, t_ref, mod_ref, gn_ref,
             pa_ref, pb_ref, wo_ref, lg_ref, lnb_ref,
             dr_ref, do_ref, dhx_ref, dz_ref, oa_o, obb_o, y_o, dya_o, dyb_o, dout_o, vec_ref):
        @pl.when(pl.program_id(0) == 0)
        def _():
            vec_ref[...] = jnp.zeros_like(vec_ref)

        gt = mod_ref[0:1, 2 * D:3 * D]
        gnv = gn_ref[...]
        o = of_ref[...] + ob_ref[...]
        rs = jnp.concatenate(
            [jnp.broadcast_to(lax.rsqrt(jnp.mean(jnp.square(o[:, h * DH:(h + 1) * DH]), axis=1, keepdims=True)
                                        + RMS_EPS), (tm, DH)) for h in range(NH)], axis=1)
        nrm = o * rs
        rn = nrm * gnv
        z4, z6, z7, z8 = z4_ref[...], z6_ref[...], z7_ref[...], z8_ref[...]
        s4, s6, s7, s8 = _sigmoid(z4), _sigmoid(z6), _sigmoid(z7), _sigmoid(z8)
        sg4, sg6 = z4 * s4, z6 * s6
        hxv = hx_ref[...]
        oa = (rn * sg4).astype(BF16)
        obb = (hxv * sg6).astype(BF16)
        ya = _dot(oa, pa_ref[...])
        yb = _dot(obb, pb_ref[...])
        y = (s7 * ya + s8 * yb).astype(BF16)
        out = _dot(y, wo_ref[...])
        xv = x_ref[...]
        rr = ALPHA * xv + gt * out
        mu = jnp.mean(rr, axis=1, keepdims=True)
        cen = rr - mu
        rstd = lax.rsqrt(jnp.mean(cen * cen, axis=1, keepdims=True) + LN_EPS)
        xhat = cen * rstd
        lg = lg_ref[...]
        err = xhat * lg + lnb_ref[...] - t_ref[...]
        loss_rows = jnp.sum(err * err, axis=1, keepdims=True)
        dxn = err * (1.0 / D)
        dxh = dxn * lg
        dr = rstd * (dxh - jnp.mean(dxh, axis=1, keepdims=True)
                     - xhat * jnp.mean(dxh * xhat, axis=1, keepdims=True))
        dout = (dr * gt).astype(BF16)
        dy = _dot_nt(dout, wo_ref[...])
        dya = (dy * s7).astype(BF16)
        dyb = (dy * s8).astype(BF16)
        doa = _dot_nt(dya, pa_ref[...])
        dob = _dot_nt(dyb, pb_ref[...])
        drn = doa * sg4
        dn = drn * gnv
        dnn = dn * nrm
        corr = jnp.concatenate(
            [jnp.broadcast_to(jnp.mean(dnn[:, h * DH:(h + 1) * DH], axis=1, keepdims=True), (tm, DH))
             for h in range(NH)], axis=1)
        dr_ref[...] = dr
        do_ref[...] = rs * (dn - nrm * corr)
        dhx_ref[...] = dob * sg6
        dz_ref[:, 0:4 * D] = jnp.zeros((tm, 4 * D), BF16)
        dz_ref[:, 4 * D:5 * D] = (doa * rn * _dsilu(z4, s4)).astype(BF16)
        dz_ref[:, 5 * D:6 * D] = jnp.zeros((tm, D), BF16)
        dz_ref[:, 6 * D:7 * D] = (dob * hxv * _dsilu(z6, s6)).astype(BF16)
        dz_ref[:, 7 * D:8 * D] = (dy * ya * s7 * (1.0 - s7)).astype(BF16)
        dz_ref[:, 8 * D:9 * D] = (dy * yb * s8 * (1.0 - s8)).astype(BF16)
        oa_o[...] = oa
        obb_o[...] = obb
        y_o[...] = y
        dya_o[...] = dya
        dyb_o[...] = dyb
        dout_o[...] = dout
        vec_ref[0:1, :] += jnp.sum(dr * out, axis=0, keepdims=True)
        vec_ref[1:2, :] += jnp.sum(dxn * xhat, axis=0, keepdims=True)
        vec_ref[2:3, :] += jnp.sum(dxn, axis=0, keepdims=True)
        vec_ref[3:4, :] += jnp.sum(drn * nrm, axis=0, keepdims=True)
        vec_ref[4:5, :] += jnp.broadcast_to(jnp.sum(loss_rows, axis=0, keepdims=True) * (0.5 / D), (1, D))

    def grp(g):
        return pl.BlockSpec((tm, D), lambda i: (i, g))

    tok = pl.BlockSpec((tm, D), lambda i: (i, 0))
    vec = pl.BlockSpec((1, D), lambda i: (0, 0))
    wsp = pl.BlockSpec((D, D), lambda i: (0, 0))
    return pl.pallas_call(
        body, name="merge", grid=(n // tm,),
        out_shape=[jax.ShapeDtypeStruct((n, D), F32)] * 3
        + [jax.ShapeDtypeStruct((n, NGRP * D), BF16)]
        + [jax.ShapeDtypeStruct((n, D), BF16)] * 6 + [jax.ShapeDtypeStruct((8, D), F32)],
        in_specs=[grp(4), grp(6), grp(7), grp(8), tok, tok, tok, tok, tok,
                  pl.BlockSpec((8, 3 * D), lambda i: (0, 0)), vec, wsp, wsp, wsp, vec, vec],
        out_specs=[tok, tok, tok, pl.BlockSpec((tm, NGRP * D), lambda i: (i, 0))] + [tok] * 6
        + [pl.BlockSpec((8, D), lambda i: (0, 0))],
    )(z, z, z, z, o_f, o_b, hx, xin, tgt, mod, gn, p_a, p_b, w_out, ln_g, ln_b)


def _wmod_grad(c_t, cctx_t, dmx_loc, dmc_loc, name):
    n = dmx_loc.shape[1]

    def body(ct_ref, cc_ref, dmx_ref, dmc_ref, o_ref):
        ct = ct_ref[...]
        sct = ct * _sigmoid(ct)
        cc = cc_ref[...]
        scc = cc * _sigmoid(cc)
        dmc = dmc_ref[0:1, :]
        for b in range(1, NDEV):
            dmc = dmc + dmc_ref[b:b + 1, :]
        acc = scc * dmc
        for b in range(NDEV):
            acc = acc + sct[:, b:b + 1] * dmx_ref[b:b + 1, :]
        o_ref[...] = acc

    return pl.pallas_call(body, name=name, out_shape=jax.ShapeDtypeStruct((D, n), F32))(c_t, cctx_t, dmx_loc, dmc_loc)


PACK_ROWS = 40


def _finalize_small(g_pack, lb, w_mod_full, params):
    npar = len(params)

    def body(*refs):
        gp_ref, lb_ref, wm_ref = refs[:3]
        wmv = refs[3:3 + 3 * npar]
        loss_ref = refs[3 + 3 * npar]
        g_refs = refs[4 + 3 * npar:4 + 4 * npar]
        upd = refs[4 + 4 * npar:4 + 7 * npar]
        tot = refs[-1]
        acc = gp_ref[0]
        for k in range(1, NDEV):
            acc = acc + gp_ref[k]
        tot[...] = acc
        mine = pl.ds(pl.multiple_of(_my_index() * DH, DH), DH)
        (g_cctx, g_bmod, g_bin, g_lbl, g_norm, g_cw, g_cb, g_br, g_bi, g_lam, g_lng, g_lnb) = g_refs

        loss_ref[...] = jnp.broadcast_to(tot[36:37, 0:DH], (8, DH))
        for k in range(3):
            g_bmod[:, k * D:(k + 1) * D] = tot[k:k + 1, :] + tot[3 + k:4 + k, :]
        dmc = jnp.concatenate([tot[3:4, :], tot[4:5, :], tot[5:6, :]], axis=1)
        cv = wmv[0][...]
        proj = _dot_nt(jnp.broadcast_to(dmc, (8, 3 * D)).astype(BF16), wm_ref[...])
        g_cctx[...] = proj[0:1, :] * _dsilu(cv, _sigmoid(cv))
        for k in range(NGRP):
            g_bin[:, k * D:(k + 1) * D] = tot[6 + k:7 + k, :]
        nrm = tot[15:16, 0:DH]
        for h in range(1, NH):
            nrm = nrm + tot[15:16, h * DH:(h + 1) * DH]
        g_norm[...] = nrm
        g_lng[...] = tot[16:17, :]
        g_lnb[...] = tot[17:18, :]
        g_cb[...] = tot[21:22, :] + tot[29:30, :]
        g_cw[0] = tot[22:26, mine] + tot[30:34, mine]
        for ref, row in ((g_br, 18), (g_bi, 19), (g_lam, 20)):
            ref[0, 0:1, :] = tot[row:row + 1, mine]
            ref[0, 1:2, :] = tot[row + 8:row + 9, mine]
        lbl = lb_ref[0:2, mine]
        dl0 = tot[34:36, mine] * lbl * (1.0 - lbl)
        g_lbl[0] = dl0
        g_lbl[1] = -dl0
        for p in range(npar):
            d, mm, vv = _adam_math(g_refs[p][...], wmv[3 * p][...], wmv[3 * p + 1][...], wmv[3 * p + 2][...])
            upd[3 * p][...] = d
            upd[3 * p + 1][...] = mm
            upd[3 * p + 2][...] = vv

    flat = [t for wmv in params for t in wmv]
    shapes = [jax.ShapeDtypeStruct(wmv[0].shape, F32) for wmv in params]
    outs = pl.pallas_call(
        body, name="finalize_small",
        out_shape=[jax.ShapeDtypeStruct((8, DH), F32)] + shapes + [s for s in shapes for _ in range(3)],
        scratch_shapes=[pltpu.VMEM((PACK_ROWS, D), F32)],
    )(g_pack, lb, w_mod_full, *flat)
    grads = list(outs[1:1 + npar])
    upd = [tuple(outs[1 + npar + 3 * p:4 + npar + 3 * p]) for p in range(npar)]
    return outs[0], grads, upd


def _to_colmajor(t, rows):
    return t.reshape(rows, GRID_W, D).transpose(1, 0, 2).reshape(rows * GRID_W, D)


def _to_raster(t, rows):
    return t.reshape(GRID_W, rows, D).transpose(1, 0, 2).reshape(rows * GRID_W, D)


def _local_cols(t, me, width):
    return lax.dynamic_slice_in_dim(t, me * width, width, axis=t.ndim - 1)


def kernel(x, c, ctx, c_ctx, w_mod, b_mod, w_in, b_in, lb_logits, norm_a_g, conv_w, conv_b, w_r, b_r, w_i, b_i, lam, p_a, p_b, w_out, ln_g, ln_b, loss_target, m_c_ctx, m_w_mod, m_b_mod, m_w_in, m_b_in, m_lb_logits, m_norm_a_g, m_conv_w, m_conv_b, m_w_r, m_b_r, m_w_i, m_b_i, m_lam, m_p_a, m_p_b, m_w_out, m_ln_g, m_ln_b, v_c_ctx, v_w_mod, v_b_mod, v_w_in, v_b_in, v_lb_logits, v_norm_a_g, v_conv_w, v_conv_b, v_w_r, v_b_r, v_w_i, v_b_i, v_lam, v_p_a, v_p_b, v_w_out, v_ln_g, v_ln_b):
    me = _my_index()
    xs, cs, tgt = x[0], ctx[0], loss_target[0]
    t_len, c_len = xs.shape[0], cs.shape[0]
    rows = t_len // GRID_W
    wcols = w_in.shape[2]
    mcols = w_mod.shape[2]

    w_mod_b, w_in_b, p_a_b, p_b_b, w_out_b, w_r_b, w_i_b = _cast_bf16(
        [w_mod[0], w_in[0], p_a[0], p_b[0], w_out[0], w_r[0], w_i[0]])
    small = jnp.concatenate([lb_logits.reshape(4, DH), conv_w[0], b_r[0], b_i[0], lam[0], jnp.zeros((2, DH), F32),
                             c.reshape(8, DH)], axis=0)
    g_small, g_wmod = _all_gather([small, w_mod_b], "gather_params")

    def full_rows(lo, hi):
        return g_small[:, lo:hi, :].transpose(1, 0, 2).reshape(hi - lo, D)

    lbl_f, cw_f, br_f, bi_f, lam_f = full_rows(0, 4), full_rows(4, 8), full_rows(8, 10), full_rows(10, 12), full_rows(12, 14)
    c_all = g_small[:, 16:24, :].reshape(NDEV, D)
    w_mod_f = g_wmod.transpose(1, 0, 2).reshape(D, 3 * D)

    cc = jnp.concatenate([c.reshape(1, D), c_ctx.reshape(1, D), jnp.zeros((6, D), F32)], axis=0)
    lbl_p = jnp.concatenate([lbl_f.reshape(2, 2, D), jnp.zeros((2, 6, D), F32)], axis=1)
    mod, lb = _prep(cc, w_mod_f, b_mod, lbl_p)
    u_x = _modulate(xs, mod, 0, "modulate_x")
    u_c = _modulate(cs, mod, 1, "modulate_c")
    z_x, w_in_f = _inproj_gather(u_x, w_in_b, b_in, "inproj_gather")
    z_c = _mm_bias(u_c, w_in_f, b_in, "inproj_c")

    zero_s = jnp.zeros((NH, DH, DH), F32)
    zero_v = jnp.zeros((1, D), F32)
    gla = {}
    out_w = [p_a_b, p_b_b, w_out_b]
    out_w_f = []
    for d in (0, 1):
        _, ssc, sfc = _gla_fwd(z_c, lb, zero_s, d, f"gla_fwd_c{d}")
        o_d, ssx, _, *gathered = _gla_fwd(z_x, lb, sfc, d, f"gla_fwd_x{d}", side=out_w[1:] if d else out_w[:1])
        out_w_f += [t.reshape(D, D) for t in gathered]
        gla[d] = (ssc, ssx, o_d)
    p_a_f, p_b_f, w_out_f = out_w_f

    x5_c = z_c[:, 5 * D:6 * D]
    x5_x = _to_colmajor(z_x[:, 5 * D:6 * D], rows)
    cb2 = conv_b.reshape(1, D)
    lru = {}
    h_sum = None
    for d in (0, 1):
        prm = (cw_f, cb2, w_r_b[d], br_f[d:d + 1], w_i_b[d], bi_f[d:d + 1], lam_f[d:d + 1])
        h_c, hin_c, hfin_c, *sav_c = _lru_fwd(x5_c, c_len, *prm, zero_v, None, d, f"lru_fwd_c{d}")
        h_x, hin_x, _, sav_a, sav_h, *h_sum = _lru_fwd(x5_x, rows, *prm, hfin_c, lru[0][3] if d else None, d,
                                                       f"lru_fwd_x{d}")
        lru[d] = ((cw_f, w_r_b[d], w_i_b[d], lam_f[d:d + 1]), h_c, hin_c, h_x, hin_x, tuple(sav_c), (sav_a, sav_h))
    hx = _to_raster(h_sum[0], rows)

    gn = jnp.tile(norm_a_g.reshape(1, DH), (1, NH))
    (dr, do, dhx, dz_m, oa, obb, yb16, dya, dyb, dout, mvec) = _merge(
        z_x, gla[0][2], gla[1][2], hx, xs, tgt, mod, gn, p_a_f, p_b_f, w_out_f, ln_g, ln_b)

    dhx_cm = _to_colmajor(dhx, rows)
    lru_dx_x = lru_dx_c = None
    for d in (0, 1):
        prm, h_c, hin_c, h_x, hin_x, sav_c, sav_x = lru[d]
        dx_dtype = BF16 if d else F32
        lru_dx_x, dwr, dwi, lvec, cg0 = _lru_bwd(x5_x, rows, *prm, sav_x, h_x, hin_x, dhx_cm, zero_v, lru_dx_x, None,
                                                 d, f"lru_bwd_x{d}", dx_dtype)
        lru_dx_c, dwr, dwi, lvec, _ = _lru_bwd(x5_c, c_len, *prm, sav_c, h_c, hin_c, None, cg0, lru_dx_c,
                                               (dwr, dwi, lvec), d, f"lru_bwd_c{d}", dx_dtype)
        lru[d] = (dwr, dwi, lvec)
    dz5_x = _to_raster(lru_dx_x, rows)
    dz5_c = lru_dx_c

    dpa = _mm_tn(oa, dya, None, "dpa", out_dtype=BF16)
    dpb = _mm_tn(obb, dyb, None, "dpb", out_dtype=BF16)
    dwo = _mm_tn(yb16, dout, None, "dwout", out_dtype=BF16)
    wr_pack = jnp.concatenate([lru[0][0], lru[1][0], lru[0][1], lru[1][1]], axis=0).reshape(4 * NH * DH, DH)

    gq_c = gv_c = None
    dzf_c, dlb = {}, {}
    gq_x, dzf_x0, gv_x, dlb_x, ds0, r_pa, r_pb, r_wo, r_wri = _gla_bwd(
        z_x, lb, gla[0][1], do, zero_s, None, None, 0, "gla_bwd_x0", f_dtype=BF16,
        side=[dpa, dpb, dwo, wr_pack], side_splits=[0, 0, 0, 0])
    gq_c, dzf_c[0], gv_c, dlb_c, _ = _gla_bwd(z_c, lb, gla[0][0], None, ds0, None, None, 0, "gla_bwd_c0")
    dlb[0] = dlb_x[0:1] + dlb_c[0:1]
    dz_g, dlb_x, ds0 = _gla_bwd(z_x, lb, gla[1][1], do, zero_s, gq_x, gv_x, 1, "gla_bwd_x1", into=(dz_m, dzf_x0))
    gq_c, dzf_c[1], gv_c, dlb_c, _ = _gla_bwd(z_c, lb, gla[1][0], None, ds0, gq_c, gv_c, 1, "gla_bwd_c1")
    dlb[1] = dlb_x[0:1] + dlb_c[0:1]

    bf = lambda t: t.astype(BF16)
    dz_x = lax.dynamic_update_slice(dz_g, dz5_x, (0, 5 * D))
    zc0 = jnp.zeros((c_len, D), BF16)
    dz_c = jnp.concatenate([bf(gq_c), bf(dzf_c[0]), bf(dzf_c[1]), bf(gv_c), zc0, dz5_c, zc0, zc0, zc0], axis=1)
    dwin_c, dbin_c = _mm_tn(u_c, dz_c, None, "dwin_c", with_colsum=True)

    grad_x, xvec = _input_grad(dz_x, w_in_f, xs, dr, mod, 0, "input_grad_x")
    r_win, dbin = _dwin_exchange(u_x, dz_x, dwin_c, dbin_c, "dwin_exchange")
    _, cvec = _input_grad(dz_c, w_in_f, cs, None, mod, 1, "input_grad_c")
    wri_piece = _sum_rows(r_wri, "sum_w_ri_piece")
    g_w_in, d_w_in, nm_w_in, nv_w_in = _sum_adamw(r_win, w_in, m_w_in, v_w_in, "update_w_in")
    g_p_a, d_p_a, nm_p_a, nv_p_a = _sum_adamw(r_pa, p_a, m_p_a, v_p_a, "update_p_a")
    g_p_b, d_p_b, nm_p_b, nv_p_b = _sum_adamw(r_pb, p_b, m_p_b, v_p_b, "update_p_b")
    g_w_out, d_w_out, nm_w_out, nv_w_out = _sum_adamw(r_wo, w_out, m_w_out, v_w_out, "update_w_out")

    dlb_rows = jnp.concatenate([dlb[0], dlb[1]], axis=0)
    pack = jnp.concatenate([
        xvec[0:1], xvec[1:2], mvec[0:1],
        cvec[0:1], cvec[1:2], jnp.zeros((1, D), F32),
        dbin.reshape(NGRP, D),
        mvec[3:4], mvec[1:2], mvec[2:3],
        lru[0][2][0:8], lru[1][2][0:3],
        lru[1][2][3:8],
        dlb_rows,
        mvec[4:5],
        jnp.zeros((3, D), F32)], axis=0)
    g_pack, g_wri = _all_gather([pack, wri_piece], "gather_small_grads")

    dmx = g_pack[:, 0:3, :].reshape(NDEV, 3 * D)
    dmc = g_pack[:, 3:6, :].reshape(NDEV, 3 * D)
    grad_w_mod = _wmod_grad(c_all.T, c_ctx.reshape(D, 1), _local_cols(dmx, me, mcols), _local_cols(dmc, me, mcols),
                            "grad_w_mod").reshape(1, D, mcols)
    small_params = [(c_ctx.reshape(1, D), m_c_ctx.reshape(1, D), v_c_ctx.reshape(1, D)), (b_mod, m_b_mod, v_b_mod),
                    (b_in, m_b_in, v_b_in), (lb_logits, m_lb_logits, v_lb_logits), (norm_a_g, m_norm_a_g, v_norm_a_g),
                    (conv_w, m_conv_w, v_conv_w), (conv_b, m_conv_b, v_conv_b), (b_r, m_b_r, v_b_r),
                    (b_i, m_b_i, v_b_i), (lam, m_lam, v_lam), (ln_g, m_ln_g, v_ln_g), (ln_b, m_ln_b, v_ln_b)]
    loss_tile, small_g, small_upd = _finalize_small(g_pack, lb, w_mod_f, small_params)
    loss = loss_tile[0, 0]
    (grad_c_ctx, grad_b_mod, grad_b_in, grad_lb_logits, grad_norm_a_g, grad_conv_w, grad_conv_b, grad_b_r, grad_b_i,
     grad_lam, grad_ln_g, grad_ln_b) = small_g
    small_upd[0] = tuple(t.reshape(c_ctx.shape) for t in small_upd[0])
    (o_c_ctx, o_b_mod, o_b_in, o_lb, o_norm, o_conv_w, o_conv_b, o_b_r, o_b_i, o_lam, o_ln_g, o_ln_b) = small_upd

    half = 2 * NH * DH
    g_ri = g_wri.reshape(2 * half, DH)
    grad_w_r, grad_w_i = g_ri[:half].reshape(w_r.shape), g_ri[half:].reshape(w_i.shape)
    d_w_r, nm_w_r, nv_w_r = _adamw(grad_w_r, w_r, m_w_r, v_w_r, "update_w_r")
    d_w_i, nm_w_i, nv_w_i = _adamw(grad_w_i, w_i, m_w_i, v_w_i, "update_w_i")

    d_w_mod, nm_w_mod, nv_w_mod = _adamw(grad_w_mod, w_mod, m_w_mod, v_w_mod, "update_w_mod")

    grads = [grad_c_ctx.reshape(c_ctx.shape), grad_w_mod, grad_b_mod, g_w_in, grad_b_in, grad_lb_logits, grad_norm_a_g,
             grad_conv_w, grad_conv_b, grad_w_r, grad_b_r, grad_w_i, grad_b_i, grad_lam, g_p_a, g_p_b, g_w_out,
             grad_ln_g, grad_ln_b]
    per_kind = []
    for k in range(3):
        per_kind.append([
            o_c_ctx[k], (d_w_mod, nm_w_mod, nv_w_mod)[k], o_b_mod[k], (d_w_in, nm_w_in, nv_w_in)[k], o_b_in[k], o_lb[k],
            o_norm[k], o_conv_w[k], o_conv_b[k], (d_w_r, nm_w_r, nv_w_r)[k], o_b_r[k], (d_w_i, nm_w_i, nv_w_i)[k],
            o_b_i[k], o_lam[k], (d_p_a, nm_p_a, nv_p_a)[k], (d_p_b, nm_p_b, nv_p_b)[k], (d_w_out, nm_w_out, nv_w_out)[k],
            o_ln_g[k], o_ln_b[k]])
    return (loss, grad_x.reshape(x.shape), *grads, *per_kind[0], *per_kind[1], *per_kind[2])
```

```python
import functools

import jax
import jax.numpy as jnp
from jax import lax
from jax.experimental import pallas as pl
from jax.experimental.pallas import tpu as pltpu

F32 = jnp.float32
BF16 = jnp.bfloat16

D = 1024
NH = 8
DH = 128
CHUNK = 64
GLA_HEADS_PER_STEP = 8
GRID_W = 64
NGRP = 9
NDEV = 8
RG_C = 8.0
ALPHA = 2.0 ** 0.25
LN_EPS = 1e-5
RMS_EPS = 1e-6
Q_SCALE = DH ** -0.5
ADAM_LR, ADAM_B1, ADAM_B2, ADAM_EPS, ADAM_WD, ADAM_STEP = 1e-3, 0.9, 0.999, 1e-8, 0.01, 10
ADAM_C1 = 1.0 / (1.0 - ADAM_B1 ** ADAM_STEP)
ADAM_C2 = 1.0 / (1.0 - ADAM_B2 ** ADAM_STEP)

ANY = pl.BlockSpec(memory_space=pl.ANY)


def _sigmoid(t):
    return 1.0 / (1.0 + jnp.exp(-t))


def _dsilu(t, s):
    return s * (1.0 + t * (1.0 - s))


def _dot(a, b):
    return jnp.dot(a, b, preferred_element_type=F32)


def _dot_nt(a, b):
    return lax.dot_general(a, b, (((1,), (1,)), ((), ())), preferred_element_type=F32)


def _dot_tn(a, b):
    return lax.dot_general(a, b, (((0,), (0,)), ((), ())), preferred_element_type=F32)


def _my_index():
    return 4 * lax.axis_index("x") + 2 * lax.axis_index("y") + lax.axis_index("c")


def _dev_tuple(j):
    return (j >> 2, (j >> 1) & 1, j & 1)


def _exchange_sems(n):
    return [pltpu.SemaphoreType.DMA((n * NDEV,)), pltpu.SemaphoreType.DMA((n * NDEV,)), pltpu.SemaphoreType.DMA((n,))]


def _exchange(ins, outs, sems, piece_of=None):
    send_sems, recv_sems, loc_sems = sems
    n = len(ins)

    def src(a, p):
        return ins[a] if piece_of is None else piece_of(ins[a], a, p)

    def push(a, t):
        me, p = _my_index(), _step_peer(t)
        return pltpu.make_async_remote_copy(
            src_ref=src(a, p), dst_ref=outs[a].at[me],
            send_sem=send_sems.at[a * NDEV + t], recv_sem=recv_sems.at[a * NDEV + me],
            device_id=_dev_of(p), device_id_type=pl.DeviceIdType.MESH)

    def local(a):
        me = _my_index()
        return pltpu.make_async_copy(src(a, me), outs[a].at[me], loc_sems.at[a])

    def start():
        for a in range(n):
            local(a).start()
        for t in range(NDEV - 1):
            for a in range(n):
                push(a, t).start()

    def finish():
        me = _my_index()
        for t in range(NDEV - 1):
            for a in range(n):
                push(a, t).wait_send()
        for j in range(NDEV):
            @pl.when(me != j)
            def _():
                for a in range(n):
                    pltpu.make_async_remote_copy(
                        src_ref=src(a, j), dst_ref=outs[a].at[j],
                        send_sem=send_sems.at[a * NDEV], recv_sem=recv_sems.at[a * NDEV + j],
                        device_id=_dev_tuple(j), device_id_type=pl.DeviceIdType.MESH).wait_recv()
        for a in range(n):
            local(a).wait()

    return start, finish


def _all_gather(shards, name):
    n = len(shards)

    def body(*refs):
        start, finish = _exchange(refs[:n], refs[n:2 * n], refs[2 * n:])
        start()
        finish()

    return pl.pallas_call(
        body, name=name,
        out_shape=[jax.ShapeDtypeStruct((NDEV,) + s.shape, s.dtype) for s in shards],
        in_specs=[ANY] * n, out_specs=[ANY] * n, scratch_shapes=_exchange_sems(n),
    )(*shards)


def _pieces(parts, splits):
    shapes = []
    for part, split in zip(parts, splits):
        r, c = part.shape
        shapes.append((r // NDEV, c) if split == 0 else (r, c // NDEV))

    def piece_of(ref, a, j):
        pr, pc = shapes[a]
        if splits[a] == 0:
            start = j * pr if isinstance(j, int) else pl.multiple_of(j * pr, pr)
            return ref.at[pl.ds(start, pr), :]
        start = j * pc if isinstance(j, int) else pl.multiple_of(j * pc, pc)
        return ref.at[:, pl.ds(start, pc)]

    return shapes, piece_of


_STEP_MASKS = ((2, 4, 6, 3, 5, 7, 1, 0), (4, 2, 6, 5, 3, 7, 1, 0))
_GATHER_MASKS = ((0, 1, 2, 4, 3, 5, 6, 7), (0, 1, 4, 2, 5, 3, 6, 7))


def _peer_schedule(table):
    tab = jnp.array(table, jnp.int32)
    return jnp.bitwise_xor(_my_index(), tab[lax.axis_index("c")])


def _step_peer(s, table=_STEP_MASKS):
    def pick(row):
        if isinstance(s, int):
            return jnp.int32(row[s])
        m = jnp.int32(row[NDEV - 1])
        for t in range(NDEV - 2, -1, -1):
            m = jnp.where(s == t, jnp.int32(row[t]), m)
        return m
    mask = jnp.where(lax.axis_index("c") == 0, pick(table[0]), pick(table[1]))
    return jnp.bitwise_xor(_my_index(), mask)


def _dev_of(p):
    return (p // 4, (p // 2) % 2, p % 2)


def _dwin_exchange(u, dz, init, cs_init, name):
    m, ka = u.shape
    n = dz.shape[1]
    pc = n // NDEV
    tk = _row_tile(m, 512)
    nk = m // tk

    def body(pidx_ref, u_ref, dz_ref, init_ref, csi_ref, rwin, cs_ref, acc, sbuf, wsend, wrecv, wloc):
        s, k = pl.program_id(0), pl.program_id(1)
        me = _my_index()

        def slab_copy(slot, p):
            return pltpu.make_async_remote_copy(
                src_ref=sbuf.at[slot], dst_ref=rwin.at[me], send_sem=wsend.at[slot], recv_sem=wrecv.at[me],
                device_id=_dev_of(p), device_id_type=pl.DeviceIdType.MESH)

        @pl.when(k == 0)
        def _():
            acc[...] = init_ref[...]
            cs_ref[...] = csi_ref[...]

        bv = dz_ref[...]
        acc[...] += _dot_tn(u_ref[...], bv)
        cs_ref[...] += jnp.sum(bv.astype(F32), axis=0, keepdims=True)

        @pl.when(k == nk - 1)
        def _():
            slot = s % 2

            @pl.when(s >= 2)
            def _():
                slab_copy(slot, me).wait_send()

            sbuf[slot] = acc[...].astype(BF16)

            @pl.when(s < NDEV - 1)
            def _():
                slab_copy(slot, _step_peer(s)).start()

            @pl.when(s == NDEV - 1)
            def _():
                own = pltpu.make_async_copy(sbuf.at[slot], rwin.at[me], wloc.at[0])
                own.start()
                slab_copy(1 - slot, me).wait_send()
                for j in range(NDEV):
                    @pl.when(me != j)
                    def _():
                        pltpu.make_async_remote_copy(
                            src_ref=sbuf.at[0], dst_ref=rwin.at[j], send_sem=wsend.at[0], recv_sem=wrecv.at[j],
                            device_id=_dev_tuple(j), device_id_type=pl.DeviceIdType.MESH).wait_recv()
                own.wait()

    grid_spec = pltpu.PrefetchScalarGridSpec(
        num_scalar_prefetch=1, grid=(NDEV, nk),
        in_specs=[pl.BlockSpec((tk, ka), lambda s, k, pidx: (k, 0)),
                  pl.BlockSpec((tk, pc), lambda s, k, pidx: (k, pidx[s])),
                  pl.BlockSpec((ka, pc), lambda s, k, pidx: (0, pidx[s])),
                  pl.BlockSpec((1, pc), lambda s, k, pidx: (0, pidx[s]))],
        out_specs=[ANY, pl.BlockSpec((1, pc), lambda s, k, pidx: (0, pidx[s]))],
        scratch_shapes=[pltpu.VMEM((ka, pc), F32), pltpu.VMEM((2, ka, pc), BF16),
                        pltpu.SemaphoreType.DMA((2,)), pltpu.SemaphoreType.DMA((NDEV,)), pltpu.SemaphoreType.DMA((1,))])
    return pl.pallas_call(
        body, name=name, grid_spec=grid_spec,
        out_shape=[jax.ShapeDtypeStruct((NDEV, ka, pc), BF16), jax.ShapeDtypeStruct((1, n), F32)],
    )(_peer_schedule(_STEP_MASKS), u, dz, init, cs_init)


def _inproj_gather(xin, mod, w_loc, bias, name):
    m, k = xin.shape
    pc = w_loc.shape[1]
    n = pc * NDEV
    tm = _row_tile(m, 512)
    ni = m // tm
    direct = (1, 2, 3, 6)
    relay_sem = {2: 4, 3: 5, 6: 7}

    def body(pidx_ref, x_ref, mod_ref, b_ref, wl_ref, z_ref, u_ref, wall, wbuf, wsend, wrecv, ldsem, ownsem):
        s, i = pl.program_id(0), pl.program_id(1)
        me = _my_index()
        ub = (x_ref[...] * (1.0 + mod_ref[0:1, D:2 * D]) + mod_ref[0:1, 0:D]).astype(BF16)

        @pl.when(s == 0)
        def _():
            u_ref[...] = ub

        def shard_push(t):
            return pltpu.make_async_remote_copy(
                src_ref=wl_ref, dst_ref=wall.at[me], send_sem=wsend.at[t], recv_sem=wrecv.at[me],
                device_id=_dev_of(_step_peer(t, _GATHER_MASKS)), device_id_type=pl.DeviceIdType.MESH)

        def relay(t):
            p = _step_peer(t, _GATHER_MASKS)
            return pltpu.make_async_remote_copy(
                src_ref=wall.at[p], dst_ref=wall.at[p], send_sem=wsend.at[relay_sem[t]], recv_sem=wrecv.at[p],
                device_id=_dev_of(_step_peer(1, _GATHER_MASKS)), device_id_type=pl.DeviceIdType.MESH)

        def load(slot, src):
            return pltpu.make_async_copy(src, wbuf.at[slot], ldsem.at[slot])

        own = pltpu.make_async_copy(wl_ref, wall.at[me], ownsem.at[0])

        @pl.when((s == 0) & (i == 0))
        def _():
            own.start()
            load(0, wl_ref).start()
            for t in direct:
                shard_push(t).start()

        @pl.when((i == ni // 2) & (s < NDEV - 1))
        def _():
            nxt = _step_peer(s + 1, _GATHER_MASKS)
            pltpu.make_async_remote_copy(
                src_ref=wl_ref, dst_ref=wall.at[nxt], send_sem=wsend.at[0], recv_sem=wrecv.at[nxt],
                device_id=_dev_of(nxt), device_id_type=pl.DeviceIdType.MESH).wait_recv()
            for t in relay_sem:
                @pl.when(s + 1 == t)
                def _():
                    relay(t).start()
            load((s + 1) % 2, wall.at[nxt]).start()

        @pl.when(i == 0)
        def _():
            load(s % 2, wl_ref).wait()

        z_ref[...] = _dot(ub, wbuf[s % 2]) + b_ref[...]

        @pl.when((s == NDEV - 1) & (i == ni - 1))
        def _():
            own.wait()
            for t in direct:
                shard_push(t).wait_send()
            for t in relay_sem:
                relay(t).wait_send()

    grid_spec = pltpu.PrefetchScalarGridSpec(
        num_scalar_prefetch=1, grid=(NDEV, ni),
        in_specs=[pl.BlockSpec((tm, k), lambda s, i, pidx: (i, 0)),
                  pl.BlockSpec((8, 3 * D), lambda s, i, pidx: (0, 0)),
                  pl.BlockSpec((1, pc), lambda s, i, pidx: (0, pidx[s])), ANY],
        out_specs=[pl.BlockSpec((tm, pc), lambda s, i, pidx: (i, pidx[s])),
                   pl.BlockSpec((tm, k), lambda s, i, pidx: (jnp.where(s == 0, i, ni - 1), 0)), ANY],
        scratch_shapes=[pltpu.VMEM((2, k, pc), BF16),
                        pltpu.SemaphoreType.DMA((NDEV,)), pltpu.SemaphoreType.DMA((NDEV,)),
                        pltpu.SemaphoreType.DMA((2,)), pltpu.SemaphoreType.DMA((1,))])
    return pl.pallas_call(
        body, name=name, grid_spec=grid_spec,
        out_shape=[jax.ShapeDtypeStruct((m, n), F32), jax.ShapeDtypeStruct((m, k), BF16),
                   jax.ShapeDtypeStruct((NDEV, k, pc), w_loc.dtype)],
    )(_peer_schedule(_GATHER_MASKS), xin, mod, bias, w_loc)


def _adam_math(g, w, m, v):
    m2 = ADAM_B1 * m + (1.0 - ADAM_B1) * g
    v2 = ADAM_B2 * v + (1.0 - ADAM_B2) * (g * g)
    delta = -ADAM_LR * ((m2 * ADAM_C1) / (jnp.sqrt(v2 * ADAM_C2) + ADAM_EPS) + ADAM_WD * w)
    return delta, m2, v2


def _row_tile(r, cap):
    t = min(r, cap)
    while r % t:
        t //= 2
    return t


def _adamw(g, w, m, v, name):
    shape = w.shape
    cols = shape[-1] if w.ndim >= 2 and shape[-1] % 128 == 0 else 128
    g2, w2, m2, v2 = (t.reshape(-1, cols) for t in (g, w, m, v))
    r = g2.shape[0]
    tr = _row_tile(r, 256)

    def body(g_ref, w_ref, m_ref, v_ref, d_ref, mo_ref, vo_ref):
        d, mm, vv = _adam_math(g_ref[...], w_ref[...], m_ref[...], v_ref[...])
        d_ref[...] = d
        mo_ref[...] = mm
        vo_ref[...] = vv

    spec = pl.BlockSpec((tr, cols), lambda i: (i, 0))
    outs = pl.pallas_call(
        body, name=name, grid=(r // tr,),
        out_shape=[jax.ShapeDtypeStruct((r, cols), F32)] * 3,
        in_specs=[spec] * 4, out_specs=[spec] * 3,
    )(g2, w2, m2, v2)
    return tuple(o.reshape(shape) for o in outs)


def _sum_adamw(parts, w, m, v, name):
    _, r, c = parts.shape
    shape = w.shape
    w2, m2, v2 = (t.reshape(r, c) for t in (w, m, v))
    tr = _row_tile(r, 128)

    def body(p_ref, w_ref, m_ref, v_ref, g_ref, d_ref, mo_ref, vo_ref):
        g = p_ref[0].astype(F32)
        for k in range(1, NDEV):
            g = g + p_ref[k].astype(F32)
        d, mm, vv = _adam_math(g, w_ref[...], m_ref[...], v_ref[...])
        g_ref[...] = g
        d_ref[...] = d
        mo_ref[...] = mm
        vo_ref[...] = vv

    spec = pl.BlockSpec((tr, c), lambda i: (i, 0))
    outs = pl.pallas_call(
        body, name=name, grid=(r // tr,),
        out_shape=[jax.ShapeDtypeStruct((r, c), F32)] * 4,
        in_specs=[pl.BlockSpec((NDEV, tr, c), lambda i: (0, i, 0))] + [spec] * 3, out_specs=[spec] * 4,
    )(parts, w2, m2, v2)
    return tuple(o.reshape(shape) for o in outs)


def _sum_rows(parts, name):
    _, r, c = parts.shape

    def body(p_ref, o_ref):
        g = p_ref[0]
        for k in range(1, NDEV):
            g = g + p_ref[k]
        o_ref[...] = g

    return pl.pallas_call(
        body, name=name, out_shape=jax.ShapeDtypeStruct((r, c), F32),
    )(parts)


def _cast_bf16(arrays):
    n = len(arrays)

    def body(*refs):
        for src, dst in zip(refs[:n], refs[n:]):
            dst[...] = src[...].astype(BF16)

    return pl.pallas_call(
        body, name="cast_weights", out_shape=[jax.ShapeDtypeStruct(a.shape, BF16) for a in arrays],
    )(*arrays)


def _prep(cc, w_mod_full, b_mod, lbl):
    def body(cc_ref, w_ref, b_ref, l_ref, mod_ref, lb_ref):
        t = cc_ref[...]
        s = (t * _sigmoid(t)).astype(BF16)
        mod_ref[...] = _dot(s, w_ref[...]) + b_ref[...]
        lb_ref[...] = _sigmoid(l_ref[0] - l_ref[1])

    return pl.pallas_call(
        body, name="prep",
        out_shape=[jax.ShapeDtypeStruct((8, 3 * D), F32), jax.ShapeDtypeStruct((8, D), F32)],
    )(cc, w_mod_full, b_mod, lbl)


def _modulate(xin, mod, row, name):
    n = xin.shape[0]
    tm = _row_tile(n, 512)

    def body(x_ref, mod_ref, u_ref):
        sh = mod_ref[row:row + 1, 0:D]
        sc = mod_ref[row:row + 1, D:2 * D]
        u_ref[...] = (x_ref[...] * (1.0 + sc) + sh).astype(BF16)

    return pl.pallas_call(
        body, name=name, grid=(n // tm,),
        out_shape=jax.ShapeDtypeStruct((n, D), BF16),
        in_specs=[pl.BlockSpec((tm, D), lambda i: (i, 0)), pl.BlockSpec((8, 3 * D), lambda i: (0, 0))],
        out_specs=pl.BlockSpec((tm, D), lambda i: (i, 0)),
    )(xin, mod)


def _mm_bias(a, w_all, bias, name):
    m, k = a.shape
    tn = w_all.shape[2]
    n = tn * NDEV
    tm = _row_tile(m, 512)

    def body(a_ref, b_ref, bias_ref, o_ref):
        o_ref[...] = _dot(a_ref[...], b_ref[0]) + bias_ref[...]

    return pl.pallas_call(
        body, name=name, grid=(NDEV, m // tm),
        out_shape=jax.ShapeDtypeStruct((m, n), F32),
        in_specs=[pl.BlockSpec((tm, k), lambda j, i: (i, 0)), pl.BlockSpec((1, k, tn), lambda j, i: (j, 0, 0)),
                  pl.BlockSpec((1, tn), lambda j, i: (0, j))],
        out_specs=pl.BlockSpec((tm, tn), lambda j, i: (i, j)),
    )(a, w_all, bias)


def _mm_tn(a, b, init, name, with_colsum=False, colsum_init=None, out_dtype=F32):
    m, ka = a.shape
    n = b.shape[1]
    tk = _row_tile(m, 512)
    tn = 1024
    nk = m // tk
    has_init = init is not None

    def body(*refs):
        a_ref, b_ref = refs[0], refs[1]
        pos = 2
        init_ref = cs_init_ref = None
        if has_init:
            init_ref = refs[pos]
            pos += 1
            if with_colsum:
                cs_init_ref = refs[pos]
                pos += 1
        o_ref = refs[pos]
        cs_ref = refs[pos + 1] if with_colsum else None
        acc = refs[-1]
        k = pl.program_id(1)

        @pl.when(k == 0)
        def _():
            if has_init:
                acc[...] = init_ref[...]
                if with_colsum:
                    cs_ref[...] = cs_init_ref[...]
            else:
                acc[...] = jnp.zeros_like(acc)
                if with_colsum:
                    cs_ref[...] = jnp.zeros_like(cs_ref)

        bv = b_ref[...]
        acc[...] += _dot_tn(a_ref[...], bv)
        if with_colsum:
            cs_ref[...] += jnp.sum(bv.astype(F32), axis=0, keepdims=True)

        @pl.when(k == nk - 1)
        def _():
            o_ref[...] = acc[...].astype(out_dtype)

    in_specs = [pl.BlockSpec((tk, ka), lambda j, k: (k, 0)), pl.BlockSpec((tk, tn), lambda j, k: (k, j))]
    args = [a, b]
    if has_init:
        in_specs.append(pl.BlockSpec((ka, tn), lambda j, k: (0, j)))
        args.append(init)
        if with_colsum:
            in_specs.append(pl.BlockSpec((1, tn), lambda j, k: (0, j)))
            args.append(colsum_init)
    out_shape = [jax.ShapeDtypeStruct((ka, n), out_dtype)]
    out_specs = [pl.BlockSpec((ka, tn), lambda j, k: (0, j))]
    if with_colsum:
        out_shape.append(jax.ShapeDtypeStruct((1, n), F32))
        out_specs.append(pl.BlockSpec((1, tn), lambda j, k: (0, j)))
    outs = pl.pallas_call(
        body, name=name, grid=(n // tn, nk), out_shape=out_shape, in_specs=in_specs, out_specs=out_specs,
        scratch_shapes=[pltpu.VMEM((ka, tn), F32)],
    )(*args)
    return outs if with_colsum else outs[0]


def _input_grad(dz, w_all, xin, dr, mod, row, name, side=(), side_splits=()):
    m, n = dz.shape
    tm = _row_tile(m, 512)
    tk = w_all.shape[2]
    nk = NDEV
    ni = m // tm
    has_dr = dr is not None
    ns = len(side)
    piece_shapes, piece_of = _pieces(side, side_splits)

    def body(*refs):
        dz_ref, w_ref, x_ref = refs[:3]
        pos = 3
        dr_ref = refs[pos] if has_dr else None
        pos += int(has_dr)
        mod_ref = refs[pos]
        side_in = refs[pos + 1:pos + 1 + ns]
        pos += 1 + ns
        gx_ref = refs[pos] if has_dr else None
        pos += int(has_dr)
        vec_ref = refs[pos]
        side_out = refs[pos + 1:pos + 1 + ns]
        acc = refs[pos + 1 + ns]
        i, k = pl.program_id(0), pl.program_id(1)
        if ns:
            side_start, side_finish = _exchange(side_in, side_out, refs[pos + 2 + ns:], piece_of)

            @pl.when((i == 0) & (k == 0))
            def _():
                side_start()

        @pl.when(k == 0)
        def _():
            acc[...] = jnp.zeros_like(acc)

        @pl.when((i == 0) & (k == 0))
        def _():
            vec_ref[...] = jnp.zeros_like(vec_ref)

        acc[...] += _dot_nt(dz_ref[...], w_ref[0])

        @pl.when(k == nk - 1)
        def _():
            du = acc[...]
            xv = x_ref[...]
            if has_dr:
                sc = mod_ref[row:row + 1, D:2 * D]
                gx_ref[...] = ALPHA * dr_ref[...] + du * (1.0 + sc)
            vec_ref[0:1, :] += jnp.sum(du, axis=0, keepdims=True)
            vec_ref[1:2, :] += jnp.sum(du * xv, axis=0, keepdims=True)

        if ns:
            @pl.when((i == ni - 1) & (k == nk - 1))
            def _():
                side_finish()

    row_spec = pl.BlockSpec((tm, D), lambda i, k: (i, 0))
    in_specs = [pl.BlockSpec((tm, tk), lambda i, k: (i, k)), pl.BlockSpec((1, D, tk), lambda i, k: (k, 0, 0)), row_spec]
    args = [dz, w_all, xin]
    if has_dr:
        in_specs.append(row_spec)
        args.append(dr)
    in_specs.append(pl.BlockSpec((8, 3 * D), lambda i, k: (0, 0)))
    args.append(mod)
    in_specs += [ANY] * ns
    args += list(side)
    out_shape, out_specs = [], []
    if has_dr:
        out_shape.append(jax.ShapeDtypeStruct((m, D), F32))
        out_specs.append(row_spec)
    out_shape.append(jax.ShapeDtypeStruct((8, D), F32))
    out_specs.append(pl.BlockSpec((8, D), lambda i, k: (0, 0)))
    out_shape += [jax.ShapeDtypeStruct((NDEV,) + piece_shapes[a], side[a].dtype) for a in range(ns)]
    out_specs += [ANY] * ns
    outs = pl.pallas_call(
        body, name=name, grid=(ni, nk), out_shape=out_shape, in_specs=in_specs, out_specs=out_specs,
        scratch_shapes=[pltpu.VMEM((tm, D), F32)] + (_exchange_sems(ns) if ns else []),
    )(*args)
    return tuple(outs) if has_dr else (None, *outs)


def _tri(reverse):
    r = lax.broadcasted_iota(jnp.int32, (CHUNK, CHUNK), 0)
    c = lax.broadcasted_iota(jnp.int32, (CHUNK, CHUNK), 1)
    return (c >= r) if reverse else (c <= r)


def _cum_f32(tri_b, t):
    hi = t.astype(BF16)
    r1 = t - hi.astype(F32)
    mid = r1.astype(BF16)
    lo = (r1 - mid.astype(F32)).astype(BF16)
    return _dot(tri_b, hi) + _dot(tri_b, mid) + _dot(tri_b, lo)


def _gla_features(zq, zf, lb):
    sq = _sigmoid(zq)
    q = zq * sq * Q_SCALE
    sf = _sigmoid(zf)
    f = lb + (1.0 - lb) * sf
    return q, sq, f, sf


def _gla_block(n):
    return 256 if n % 256 == 0 else CHUNK


def _gla_fwd(z, lb, s0, d, name, side=()):
    n = z.shape[0]
    blk = _gla_block(n)
    nb, npb = n // blk, blk // CHUNK
    reverse = d == 1
    last = 0 if reverse else CHUNK - 1
    order = list(range(npb))[::-1] if reverse else list(range(npb))
    ns = len(side)

    def bmap(i):
        return nb - 1 - i if reverse else i

    hp = GLA_HEADS_PER_STEP
    hw = hp * DH
    units = [(hh, cidx) for hh in range(hp) for cidx in order]

    def body(zq_ref, zf_ref, zv_ref, lb_ref, s0_ref, *rest):
        side_in = rest[:ns]
        o_ref, ss_ref, sf_ref = rest[ns:ns + 3]
        side_out = rest[ns + 3:2 * ns + 3]
        st = rest[2 * ns + 3]
        i = pl.program_id(1)
        if ns:
            side_start, side_finish = _exchange(side_in, side_out, rest[2 * ns + 4:])

            @pl.when((pl.program_id(0) == 0) & (i == 0))
            def _():
                side_start()

        @pl.when(i == 0)
        def _():
            st[...] = s0_ref[...]

        mask = _tri(reverse)
        tri_b = jnp.where(mask, 1.0, 0.0).astype(BF16)
        feat = {}
        for u in units:
            hh, cidx = u
            rows, cols = pl.ds(cidx * CHUNK, CHUNK), pl.ds(hh * DH, DH)
            q, _, f, _ = _gla_features(zq_ref[rows, cols], zf_ref[rows, cols], lb_ref[d:d + 1, cols])
            feat[u] = (q, 1.0 - f, jnp.log(f), zv_ref[rows, cols].astype(BF16))
        dec = {u: _cum_f32(tri_b, feat[u][2]) for u in units}
        ops = {}
        for u in units:
            q, k, _, vb = feat[u]
            g = dec[u]
            gl = g[last:last + 1, :]
            ops[u] = ((q * jnp.exp(g)).astype(BF16), (k * jnp.exp(-g)).astype(BF16),
                      (k * jnp.exp(gl - g)).astype(BF16), jnp.exp(gl), vb)
        att = {u: jnp.where(mask, _dot_nt(ops[u][0], ops[u][1]), 0.0).astype(BF16) for u in units}
        upd = {u: _dot_tn(ops[u][4], ops[u][2]) for u in units}
        intra = {u: _dot(att[u], ops[u][4]) for u in units}
        s_in = {}
        for hh in range(hp):
            s = st[hh]
            for cidx in order:
                s_in[(hh, cidx)] = s
                s = s * ops[(hh, cidx)][3] + upd[(hh, cidx)]
            st[hh] = s
            sf_ref[hh] = s
        for u in units:
            hh, cidx = u
            rows, cols = pl.ds(cidx * CHUNK, CHUNK), pl.ds(hh * DH, DH)
            o_ref[rows, cols] = intra[u] + _dot_nt(ops[u][0], s_in[u].astype(BF16))
            ss_ref[hh, cidx] = s_in[u]

        if ns:
            @pl.when((pl.program_id(0) == NH // hp - 1) & (i == nb - 1))
            def _():
                side_finish()

    def col(g):
        return lambda h, i: (bmap(i), g * (NH // hp) + h)

    return pl.pallas_call(
        body, name=name, grid=(NH // hp, nb),
        out_shape=[jax.ShapeDtypeStruct((n, D), F32), jax.ShapeDtypeStruct((NH, n // CHUNK, DH, DH), F32),
                   jax.ShapeDtypeStruct((NH, DH, DH), F32)]
        + [jax.ShapeDtypeStruct((NDEV,) + t.shape, t.dtype) for t in side],
        in_specs=[pl.BlockSpec((blk, hw), col(0)), pl.BlockSpec((blk, hw), col(1 + d)),
                  pl.BlockSpec((blk, hw), col(3)), pl.BlockSpec((8, hw), lambda h, i: (0, h)),
                  pl.BlockSpec((hp, DH, DH), lambda h, i: (h, 0, 0))] + [ANY] * ns,
        out_specs=[pl.BlockSpec((blk, hw), lambda h, i: (bmap(i), h)),
                   pl.BlockSpec((hp, npb, DH, DH), lambda h, i: (h, bmap(i), 0, 0)),
                   pl.BlockSpec((hp, DH, DH), lambda h, i: (h, 0, 0))] + [ANY] * ns,
        scratch_shapes=[pltpu.VMEM((hp, DH, DH), F32)] + (_exchange_sems(ns) if ns else []),
    )(z, z, z, lb, s0, *side)


def _gla_bwd(z, lb, s_start, do, ds_fin, acc_q, acc_v, d, name, f_dtype=F32, into=None, side=(), side_splits=()):
    n = z.shape[0]
    blk = _gla_block(n)
    nb, npb = n // blk, blk // CHUNK
    reverse = d == 1
    last = 0 if reverse else CHUNK - 1
    order = list(range(npb)) if reverse else list(range(npb))[::-1]
    has_do = do is not None
    has_acc = acc_q is not None
    fused = into is not None
    assert not fused or d == 1
    ns = len(side)
    assert not (fused and ns)
    piece_shapes, piece_of = _pieces(side, side_splits)
    hp = NH if fused else GLA_HEADS_PER_STEP
    hw = hp * DH
    units = [(hh, cidx) for hh in range(hp) for cidx in order]

    def bmap(i):
        return i if reverse else nb - 1 - i

    def body(*refs):
        zq_ref, zf_ref, zv_ref, lb_ref, ss_ref, dsf_ref = refs[:6]
        pos = 6
        do_ref = aq_ref = av_ref = None
        if has_do:
            do_ref = refs[pos]
            pos += 1
        if has_acc:
            aq_ref, av_ref = refs[pos], refs[pos + 1]
            pos += 2
        if fused:
            other_ref = refs[pos + 1]
            dz_ref, dlb_ref, ds0_ref, dst = refs[pos + 2:]
            dz_ref[:, D:2 * D] = other_ref[...]
        else:
            side_in = refs[pos:pos + ns]
            dzq_ref, dzf_ref, dzv_ref, dlb_ref, ds0_ref = refs[pos + ns:pos + ns + 5]
            side_out = refs[pos + ns + 5:pos + 2 * ns + 5]
            dst = refs[pos + 2 * ns + 5]
        i = pl.program_id(1)
        if ns:
            side_start, side_finish = _exchange(side_in, side_out, refs[pos + 2 * ns + 6:], piece_of)

            @pl.when((pl.program_id(0) == 0) & (i == 0))
            def _():
                side_start()

        @pl.when(i == 0)
        def _():
            dst[...] = dsf_ref[...]
            dlb_ref[...] = jnp.zeros_like(dlb_ref)

        mask = _tri(reverse)
        tri_b = jnp.where(mask, 1.0, 0.0).astype(BF16)
        tri_t = jnp.where(_tri(not reverse), 1.0, 0.0).astype(BF16)

        def where(u):
            return pl.ds(u[1] * CHUNK, CHUNK), pl.ds(u[0] * DH, DH)

        feat = {}
        for u in units:
            rows, cols = where(u)
            zq, zf = zq_ref[rows, cols], zf_ref[rows, cols]
            lbv = lb_ref[d:d + 1, cols]
            q, sq, f, sf = _gla_features(zq, zf, lbv)
            feat[u] = dict(zq=zq, q=q, sq=sq, f=f, sf=sf, lbv=lbv, k=1.0 - f, vb=zv_ref[rows, cols].astype(BF16))
        dec = {u: _cum_f32(tri_b, jnp.log(feat[u]["f"])) for u in units}
        for u in units:
            w = feat[u]
            g = dec[u]
            gl = g[last:last + 1, :]
            w["eg"], w["egi"], w["ege"], w["egl"] = jnp.exp(g), jnp.exp(-g), jnp.exp(gl - g), jnp.exp(gl)
            w["qd"], w["ki"], w["ke"] = w["q"] * w["eg"], w["k"] * w["egi"], w["k"] * w["ege"]
            w["qdb"], w["kib"], w["keb"] = w["qd"].astype(BF16), w["ki"].astype(BF16), w["ke"].astype(BF16)
            w["s_in"] = ss_ref[u[0], u[1]]
        if has_do:
            for u in units:
                w = feat[u]
                rows, cols = where(u)
                w["dob"] = do_ref[rows, cols].astype(BF16)
            for u in units:
                w = feat[u]
                w["a"] = jnp.where(mask, _dot_nt(w["qdb"], w["kib"]), 0.0).astype(BF16)
                w["da"] = jnp.where(mask, _dot_nt(w["dob"], w["vb"]), 0.0).astype(BF16)
                w["m"] = _dot_tn(w["dob"], w["qdb"])
        for hh in range(hp):
            ds = dst[hh]
            for cidx in order:
                w = feat[(hh, cidx)]
                w["ds"] = ds
                ds = ds * w["egl"]
                if has_do:
                    ds = ds + w["m"]
            dst[hh] = ds
            ds0_ref[hh] = ds
        for u in units:
            w = feat[u]
            dsb = w["ds"].astype(BF16)
            w["dke"] = _dot(w["vb"], dsb)
            w["dv"] = _dot_nt(w["keb"], dsb)
            if has_do:
                w["dv"] = w["dv"] + _dot_tn(w["a"], w["dob"])
                w["dqd"] = _dot(w["da"], w["kib"]) + _dot(w["dob"], w["s_in"].astype(BF16))
                w["dki"] = _dot_tn(w["da"], w["qdb"])
        for u in units:
            w = feat[u]
            dkeke = w["dke"] * w["ke"]
            w["dgl"] = (w["egl"] * jnp.sum(w["s_in"] * w["ds"], axis=0, keepdims=True)
                        + jnp.sum(dkeke, axis=0, keepdims=True))
            dg = -dkeke
            dk = w["dke"] * w["ege"]
            if has_do:
                dg = dg + w["dqd"] * w["qd"] - w["dki"] * w["ki"]
                dk = dk + w["dki"] * w["egi"]
            w["dg"], w["dk"] = dg, dk
        dlf = {u: _cum_f32(tri_t, feat[u]["dg"]) for u in units}
        for u in units:
            w = feat[u]
            rows, cols = where(u)
            df = (dlf[u] + w["dgl"]) / w["f"] - w["dk"]
            sf = w["sf"]
            dzf = df * (1.0 - w["lbv"]) * sf * (1.0 - sf)
            dlb_ref[0:1, cols] += jnp.sum(df * (1.0 - sf), axis=0, keepdims=True)
            if has_do:
                dzq = w["dqd"] * w["eg"] * (Q_SCALE * _dsilu(w["zq"], w["sq"]))
            else:
                dzq = jnp.zeros((CHUNK, DH), F32)
            dv = w["dv"]
            if has_acc:
                dzq = dzq + aq_ref[rows, cols]
                dv = dv + av_ref[rows, cols]
            if fused:
                lane = u[0] * DH
                dz_ref[rows, pl.ds(lane, DH)] = dzq.astype(BF16)
                dz_ref[rows, pl.ds(2 * D + lane, DH)] = dzf.astype(BF16)
                dz_ref[rows, pl.ds(3 * D + lane, DH)] = dv.astype(BF16)
            else:
                dzq_ref[rows, cols] = dzq
                dzf_ref[rows, cols] = dzf.astype(f_dtype)
                dzv_ref[rows, cols] = dv

        if ns:
            @pl.when((pl.program_id(0) == NH // hp - 1) & (i == nb - 1))
            def _():
                side_finish()

    def col(g):
        return lambda h, i: (bmap(i), g * (NH // hp) + h)

    tok = pl.BlockSpec((blk, hw), lambda h, i: (bmap(i), h))
    state = pl.BlockSpec((hp, DH, DH), lambda h, i: (h, 0, 0))
    in_specs = [pl.BlockSpec((blk, hw), col(0)), pl.BlockSpec((blk, hw), col(1 + d)), pl.BlockSpec((blk, hw), col(3)),
                pl.BlockSpec((8, hw), lambda h, i: (0, h)),
                pl.BlockSpec((hp, npb, DH, DH), lambda h, i: (h, bmap(i), 0, 0)), state]
    args = [z, z, z, lb, s_start, ds_fin]
    if has_do:
        in_specs.append(tok)
        args.append(do)
    if has_acc:
        in_specs += [tok, tok]
        args += [acc_q, acc_v]
    tail_shape = [jax.ShapeDtypeStruct((8, D), F32), jax.ShapeDtypeStruct((NH, DH, DH), F32)]
    tail_specs = [pl.BlockSpec((8, hw), lambda h, i: (0, h)), state]
    if fused:
        buf, other = into
        aliases = {len(args): 0}
        in_specs += [ANY, tok]
        args += [buf, other]
        out_shape = [jax.ShapeDtypeStruct(buf.shape, buf.dtype)] + tail_shape
        out_specs = [pl.BlockSpec((blk, 4 * D), lambda h, i: (bmap(i), 0))] + tail_specs
    else:
        aliases = {}
        in_specs += [ANY] * ns
        args += list(side)
        out_shape = [jax.ShapeDtypeStruct((n, D), F32), jax.ShapeDtypeStruct((n, D), f_dtype),
                     jax.ShapeDtypeStruct((n, D), F32)] + tail_shape
        out_shape += [jax.ShapeDtypeStruct((NDEV,) + piece_shapes[a], side[a].dtype) for a in range(ns)]
        out_specs = [tok, tok, tok] + tail_specs + [ANY] * ns
    return pl.pallas_call(
        body, name=name, grid=(NH // hp, nb), out_shape=out_shape, in_specs=in_specs, out_specs=out_specs,
        input_output_aliases=aliases,
        scratch_shapes=[pltpu.VMEM((hp, DH, DH), F32)] + (_exchange_sems(ns) if ns else []),
    )(*args)


def _shift(t, s, fill, down):
    n = t.shape[0]
    rows = lax.broadcasted_iota(jnp.int32, t.shape, 0)
    if down:
        return jnp.where(rows >= s, pltpu.roll(t, s, 0), fill)
    return jnp.where(rows < n - s, pltpu.roll(t, n - s, 0), fill)


SUBLANES = 8
LRU_SAVED = 4


def _chain_scan(a, b, h_in, down):
    n = a.shape[0]
    ng = n // SUBLANES
    rows = lax.broadcasted_iota(jnp.int32, (SUBLANES, a.shape[1]), 0)
    local = []
    for g in range(ng):
        aa, bb = a[g * SUBLANES:(g + 1) * SUBLANES], b[g * SUBLANES:(g + 1) * SUBLANES]
        for s in (1, 2, 4):
            if down:
                keep, amt = rows >= s, s
            else:
                keep, amt = rows < SUBLANES - s, SUBLANES - s
            bb = bb + aa * jnp.where(keep, pltpu.roll(bb, amt, 0), 0.0)
            aa = aa * jnp.where(keep, pltpu.roll(aa, amt, 0), 1.0)
        local.append((aa, bb))
    out = [None] * ng
    carry = h_in
    for g in (range(ng) if down else range(ng - 1, -1, -1)):
        aa, bb = local[g]
        hg = bb + aa * carry
        out[g] = hg
        carry = hg[SUBLANES - 1:SUBLANES] if down else hg[0:1]
    return (jnp.concatenate(out, axis=0) if ng > 1 else out[0]), carry


def _conv_taps(xv):
    return (_shift(xv, 1, 0.0, True), xv, _shift(xv, 1, 0.0, False), _shift(xv, 2, 0.0, False))


def _conv(taps, cw, cb):
    return cb + cw[0:1, :] * taps[0] + cw[1:2, :] * taps[1] + cw[2:3, :] * taps[2] + cw[3:4, :] * taps[3]


def _neg_expm1(t):
    series = -t * (1.0 + t * (0.5 + t * (1.0 / 6.0 + t * (1.0 / 24.0 + t * (1.0 / 120.0)))))
    return jnp.where(t > -0.1, series, 1.0 - jnp.exp(t))


def _lru_gates(xc, wr, br, wi, bi, lam):
    xcb = xc.astype(BF16)
    r = _sigmoid(_dot(xcb, wr) + br)
    gi = _sigmoid(_dot(xcb, wi) + bi)
    sp = jnp.maximum(-lam, 0.0) + jnp.log(1.0 + jnp.exp(-jnp.abs(lam)))
    la = -RG_C * r * sp
    a = jnp.exp(la)
    mult = jnp.sqrt(_neg_expm1(2.0 * la))
    return xcb, r, gi, sp, a, mult


def _lru_fwd(xin, blk, cw, cb, wr, br, wi, bi, lam, h0, acc_h, d, name):
    n = xin.shape[0]
    nb = n // blk
    reverse = d == 1
    down = not reverse
    has_acc = acc_h is not None

    def bmap(i):
        return nb - 1 - i if reverse else i

    def body(*refs):
        x_ref, cw_ref, cb_ref, wr_ref, br_ref, wi_ref, bi_ref, lam_ref, h0_ref = refs[:9]
        pos = 9
        acc_ref = refs[pos] if has_acc else None
        pos += int(has_acc)
        h_ref, hin_ref, hfin_ref, sav_a_ref, sav_ref = refs[pos:pos + 5]
        pos += 5
        hsum_ref = refs[pos] if has_acc else None
        carry = refs[-1]
        i = pl.program_id(0)

        @pl.when(i == 0)
        def _():
            carry[...] = h0_ref[...]

        for g in range(NH):
            cols = pl.ds(g * DH, DH)
            xc = _conv(_conv_taps(x_ref[:, cols]), cw_ref[:, cols], cb_ref[:, cols])
            _, r, gi, _, a, mult = _lru_gates(xc, wr_ref[g], br_ref[:, cols], wi_ref[g], bi_ref[:, cols],
                                              lam_ref[:, cols])
            sav_a_ref[:, cols] = a
            for slot, val in enumerate((xc, r, gi, mult)):
                sav_ref[slot, :, cols] = val.astype(BF16)
            hin = carry[:, cols]
            h, h_last = _chain_scan(a, mult * gi * xc, hin, down)
            h_ref[:, cols] = h
            if has_acc:
                hsum_ref[:, cols] = h + acc_ref[:, cols]
            hin_ref[0, :, cols] = hin
            carry[:, cols] = h_last
            hfin_ref[:, cols] = h_last

    vec = pl.BlockSpec((1, D), lambda i: (0, 0))
    wsp = pl.BlockSpec((NH, DH, DH), lambda i: (0, 0, 0))
    tok = pl.BlockSpec((blk, D), lambda i: (bmap(i), 0))
    in_specs = [tok, pl.BlockSpec((4, D), lambda i: (0, 0)), vec, wsp, vec, wsp, vec, vec, vec]
    args = [xin, cw, cb, wr, br, wi, bi, lam, h0]
    out_shape = [jax.ShapeDtypeStruct((n, D), F32), jax.ShapeDtypeStruct((nb, 1, D), F32),
                 jax.ShapeDtypeStruct((1, D), F32), jax.ShapeDtypeStruct((n, D), F32),
                 jax.ShapeDtypeStruct((LRU_SAVED, n, D), BF16)]
    out_specs = [tok, pl.BlockSpec((1, 1, D), lambda i: (bmap(i), 0, 0)), vec, tok,
                 pl.BlockSpec((LRU_SAVED, blk, D), lambda i: (0, bmap(i), 0))]
    if has_acc:
        in_specs.append(tok)
        args.append(acc_h)
        out_shape.append(jax.ShapeDtypeStruct((n, D), F32))
        out_specs.append(tok)
    return pl.pallas_call(
        body, name=name, grid=(nb,), out_shape=out_shape, in_specs=in_specs, out_specs=out_specs,
        scratch_shapes=[pltpu.VMEM((1, D), F32)],
    )(*args)


def _lru_bwd(xin, blk, cw, wr, wi, lam, sav, h, hin, dh, cg_fin, acc_dx, init, d, name, dx_dtype=F32):
    n = xin.shape[0]
    nb = n // blk
    reverse = d == 1
    down = not reverse
    first = blk - 1 if reverse else 0
    has_dh = dh is not None
    has_acc = acc_dx is not None
    has_init = init is not None

    def bmap(i):
        return i if reverse else nb - 1 - i

    def body(*refs):
        (x_ref, cw_ref, wr_ref, wi_ref, lam_ref, sav_a_ref, sav_ref, h_ref, hin_ref, cgf_ref) = refs[:10]
        pos = 10
        dh_ref = acc_ref = None
        iwr_ref = iwi_ref = ivec_ref = None
        if has_dh:
            dh_ref = refs[pos]
            pos += 1
        if has_acc:
            acc_ref = refs[pos]
            pos += 1
        if has_init:
            iwr_ref, iwi_ref, ivec_ref = refs[pos:pos + 3]
            pos += 3
        dx_ref, dwr_ref, dwi_ref, vec_ref, cg0_ref, carry = refs[pos:]
        i = pl.program_id(0)

        @pl.when(i == 0)
        def _():
            carry[...] = cgf_ref[...]
            if has_init:
                dwr_ref[...] = iwr_ref[...]
                dwi_ref[...] = iwi_ref[...]
                vec_ref[...] = ivec_ref[...]
            else:
                dwr_ref[...] = jnp.zeros_like(dwr_ref)
                dwi_ref[...] = jnp.zeros_like(dwi_ref)
                vec_ref[...] = jnp.zeros_like(vec_ref)

        for g in range(NH):
            cols = pl.ds(g * DH, DH)
            cwv = cw_ref[:, cols]
            lam_v = lam_ref[:, cols]
            taps = _conv_taps(x_ref[:, cols])
            wr_g, wi_g = wr_ref[g], wi_ref[g]
            a = sav_a_ref[:, cols]
            xcb = sav_ref[0, :, cols]
            xc, r, gi, mult = (sav_ref[slot, :, cols].astype(F32) for slot in range(LRU_SAVED))
            sp = jnp.maximum(-lam_v, 0.0) + jnp.log(1.0 + jnp.exp(-jnp.abs(lam_v)))
            hprev = _shift(h_ref[:, cols], 1, hin_ref[0, :, cols], down)
            a_next = _shift(a, 1, 1.0, not down)
            dhv = dh_ref[:, cols] if has_dh else jnp.zeros_like(a)
            e, _ = _chain_scan(a_next, dhv, carry[:, cols], not down)
            cg = a[first:first + 1, :] * e[first:first + 1, :]
            carry[:, cols] = cg
            cg0_ref[:, cols] = cg
            da = e * hprev
            emult = e * mult
            dgi = emult * xc
            dxc = emult * gi
            dla = da * a - (e * gi * xc) * (a * a) / mult
            dr = dla * (-RG_C * sp)
            sneg = 1.0 - _sigmoid(lam_v)
            dpr = dr * r * (1.0 - r)
            dpi = dgi * gi * (1.0 - gi)
            dprb, dpib = dpr.astype(BF16), dpi.astype(BF16)
            dxc = dxc + _dot_nt(dprb, wr_g) + _dot_nt(dpib, wi_g)
            dwr_ref[g] += _dot_tn(xcb, dprb)
            dwi_ref[g] += _dot_tn(xcb, dpib)
            dx = (cwv[0:1, :] * _shift(dxc, 1, 0.0, False) + cwv[1:2, :] * dxc
                  + cwv[2:3, :] * _shift(dxc, 1, 0.0, True) + cwv[3:4, :] * _shift(dxc, 2, 0.0, True))
            if has_acc:
                dx = dx + acc_ref[:, cols]
            dx_ref[:, cols] = dx.astype(dx_dtype)
            vec_ref[0:1, cols] += jnp.sum(dpr, axis=0, keepdims=True)
            vec_ref[1:2, cols] += jnp.sum(dpi, axis=0, keepdims=True)
            vec_ref[2:3, cols] += jnp.sum(dla * r, axis=0, keepdims=True) * (RG_C * sneg)
            vec_ref[3:4, cols] += jnp.sum(dxc, axis=0, keepdims=True)
            for kk in range(4):
                vec_ref[4 + kk:5 + kk, cols] += jnp.sum(dxc * taps[kk], axis=0, keepdims=True)

    vec = pl.BlockSpec((1, D), lambda i: (0, 0))
    wsp = pl.BlockSpec((NH, DH, DH), lambda i: (0, 0, 0))
    tok = pl.BlockSpec((blk, D), lambda i: (bmap(i), 0))
    vec16 = pl.BlockSpec((16, D), lambda i: (0, 0))
    in_specs = [tok, pl.BlockSpec((4, D), lambda i: (0, 0)), wsp, wsp, vec, tok,
                pl.BlockSpec((LRU_SAVED, blk, D), lambda i: (0, bmap(i), 0)), tok,
                pl.BlockSpec((1, 1, D), lambda i: (bmap(i), 0, 0)), vec]
    args = [xin, cw, wr, wi, lam, sav[0], sav[1], h, hin, cg_fin]
    if has_dh:
        in_specs.append(tok)
        args.append(dh)
    if has_acc:
        in_specs.append(tok)
        args.append(acc_dx)
    if has_init:
        in_specs += [wsp, wsp, vec16]
        args += list(init)
    return pl.pallas_call(
        body, name=name, grid=(nb,),
        out_shape=[jax.ShapeDtypeStruct((n, D), dx_dtype), jax.ShapeDtypeStruct((NH, DH, DH), F32),
                   jax.ShapeDtypeStruct((NH, DH, DH), F32), jax.ShapeDtypeStruct((16, D), F32),
                   jax.ShapeDtypeStruct((1, D), F32)],
        in_specs=in_specs, out_specs=[tok, wsp, wsp, vec16, vec],
        scratch_shapes=[pltpu.VMEM((1, D), F32)],
    )(*args)


def _merge(z, o_f, o_b, hx, xin, tgt, mod, gn, p_a, p_b, w_out, ln_g, ln_b):
    n = xin.shape[0]
    tm = _row_tile(n, 128)

    def body(z4_ref, z6_ref, z7_ref, z8_ref, of_ref, ob_ref, hx_ref, x_ref, t_ref, mod_ref, gn_ref,
             pa_ref, pb_ref, wo_ref, lg_ref, lnb_ref,
             dr_ref, do_ref, dhx_ref, dz_ref, oa_o, obb_o, y_o, dya_o, dyb_o, dout_o, vec_ref):
        @pl.when(pl.program_id(0) == 0)
        def _():
            vec_ref[...] = jnp.zeros_like(vec_ref)

        gt = mod_ref[0:1, 2 * D:3 * D]
        gnv = gn_ref[...]
        o = of_ref[...] + ob_ref[...]
        rs = jnp.concatenate(
            [jnp.broadcast_to(lax.rsqrt(jnp.mean(jnp.square(o[:, h * DH:(h + 1) * DH]), axis=1, keepdims=True)
                                        + RMS_EPS), (tm, DH)) for h in range(NH)], axis=1)
        nrm = o * rs
        rn = nrm * gnv
        z4, z6, z7, z8 = z4_ref[...], z6_ref[...], z7_ref[...], z8_ref[...]
        s4, s6, s7, s8 = _sigmoid(z4), _sigmoid(z6), _sigmoid(z7), _sigmoid(z8)
        sg4, sg6 = z4 * s4, z6 * s6
        hxv = hx_ref[...]
        oa = (rn * sg4).astype(BF16)
        obb = (hxv * sg6).astype(BF16)
        ya = _dot(oa, pa_ref[...])
        yb = _dot(obb, pb_ref[...])
        y = (s7 * ya + s8 * yb).astype(BF16)
        out = _dot(y, wo_ref[...])
        xv = x_ref[...]
        rr = ALPHA * xv + gt * out
        mu = jnp.mean(rr, axis=1, keepdims=True)
        cen = rr - mu
        rstd = lax.rsqrt(jnp.mean(cen * cen, axis=1, keepdims=True) + LN_EPS)
        xhat = cen * rstd
        lg = lg_ref[...]
        err = xhat * lg + lnb_ref[...] - t_ref[...]
        loss_rows = jnp.sum(err * err, axis=1, keepdims=True)
        dxn = err * (1.0 / D)
        dxh = dxn * lg
        dr = rstd * (dxh - jnp.mean(dxh, axis=1, keepdims=True)
                     - xhat * jnp.mean(dxh * xhat, axis=1, keepdims=True))
        dout = (dr * gt).astype(BF16)
        dy = _dot_nt(dout, wo_ref[...])
        dya = (dy * s7).astype(BF16)
        dyb = (dy * s8).astype(BF16)
        doa = _dot_nt(dya, pa_ref[...])
        dob = _dot_nt(dyb, pb_ref[...])
        drn = doa * sg4
        dn = drn * gnv
        dnn = dn * nrm
        corr = jnp.concatenate(
            [jnp.broadcast_to(jnp.mean(dnn[:, h * DH:(h + 1) * DH], axis=1, keepdims=True), (tm, DH))
             for h in range(NH)], axis=1)
        dr_ref[...] = dr
        do_ref[...] = rs * (dn - nrm * corr)
        dhx_ref[...] = dob * sg6
        dz_ref[:, 0:4 * D] = jnp.zeros((tm, 4 * D), BF16)
        dz_ref[:, 4 * D:5 * D] = (doa * rn * _dsilu(z4, s4)).astype(BF16)
        dz_ref[:, 5 * D:6 * D] = jnp.zeros((tm, D), BF16)
        dz_ref[:, 6 * D:7 * D] = (dob * hxv * _dsilu(z6, s6)).astype(BF16)
        dz_ref[:, 7 * D:8 * D] = (dy * ya * s7 * (1.0 - s7)).astype(BF16)
        dz_ref[:, 8 * D:9 * D] = (dy * yb * s8 * (1.0 - s8)).astype(BF16)
        oa_o[...] = oa
        obb_o[...] = obb
        y_o[...] = y
        dya_o[...] = dya
        dyb_o[...] = dyb
        dout_o[...] = dout
        vec_ref[0:1, :] += jnp.sum(dr * out, axis=0, keepdims=True)
        vec_ref[1:2, :] += jnp.sum(dxn * xhat, axis=0, keepdims=True)
        vec_ref[2:3, :] += jnp.sum(dxn, axis=0, keepdims=True)
        vec_ref[3:4, :] += jnp.sum(drn * nrm, axis=0, keepdims=True)
        vec_ref[4:5, :] += jnp.broadcast_to(jnp.sum(loss_rows, axis=0, keepdims=True) * (0.5 / D), (1, D))

    def grp(g):
        return pl.BlockSpec((tm, D), lambda i: (i, g))

    tok = pl.BlockSpec((tm, D), lambda i: (i, 0))
    vec = pl.BlockSpec((1, D), lambda i: (0, 0))
    wsp = pl.BlockSpec((D, D), lambda i: (0, 0))
    return pl.pallas_call(
        body, name="merge", grid=(n // tm,),
        out_shape=[jax.ShapeDtypeStruct((n, D), F32)] * 3
        + [jax.ShapeDtypeStruct((n, NGRP * D), BF16)]
        + [jax.ShapeDtypeStruct((n, D), BF16)] * 6 + [jax.ShapeDtypeStruct((8, D), F32)],
        in_specs=[grp(4), grp(6), grp(7), grp(8), tok, tok, tok, tok, tok,
                  pl.BlockSpec((8, 3 * D), lambda i: (0, 0)), vec, wsp, wsp, wsp, vec, vec],
        out_specs=[tok, tok, tok, pl.BlockSpec((tm, NGRP * D), lambda i: (i, 0))] + [tok] * 6
        + [pl.BlockSpec((8, D), lambda i: (0, 0))],
    )(z, z, z, z, o_f, o_b, hx, xin, tgt, mod, gn, p_a, p_b, w_out, ln_g, ln_b)


def _wmod_grad(c_t, cctx_t, dmx_loc, dmc_loc, name):
    n = dmx_loc.shape[1]

    def body(ct_ref, cc_ref, dmx_ref, dmc_ref, o_ref):
        ct = ct_ref[...]
        sct = ct * _sigmoid(ct)
        cc = cc_ref[...]
        scc = cc * _sigmoid(cc)
        dmc = dmc_ref[0:1, :]
        for b in range(1, NDEV):
            dmc = dmc + dmc_ref[b:b + 1, :]
        acc = scc * dmc
        for b in range(NDEV):
            acc = acc + sct[:, b:b + 1] * dmx_ref[b:b + 1, :]
        o_ref[...] = acc

    return pl.pallas_call(body, name=name, out_shape=jax.ShapeDtypeStruct((D, n), F32))(c_t, cctx_t, dmx_loc, dmc_loc)


PACK_ROWS = 40


def _finalize_small(g_pack, lb, w_mod_full, params):
    npar = len(params)

    def body(*refs):
        gp_ref, lb_ref, wm_ref = refs[:3]
        wmv = refs[3:3 + 3 * npar]
        loss_ref = refs[3 + 3 * npar]
        g_refs = refs[4 + 3 * npar:4 + 4 * npar]
        upd = refs[4 + 4 * npar:4 + 7 * npar]
        tot = refs[-1]
        acc = gp_ref[0]
        for k in range(1, NDEV):
            acc = acc + gp_ref[k]
        tot[...] = acc
        mine = pl.ds(pl.multiple_of(_my_index() * DH, DH), DH)
        (g_cctx, g_bmod, g_bin, g_lbl, g_norm, g_cw, g_cb, g_br, g_bi, g_lam, g_lng, g_lnb) = g_refs

        loss_ref[...] = jnp.broadcast_to(tot[36:37, 0:DH], (8, DH))
        for k in range(3):
            g_bmod[:, k * D:(k + 1) * D] = tot[k:k + 1, :] + tot[3 + k:4 + k, :]
        dmc = jnp.concatenate([tot[3:4, :], tot[4:5, :], tot[5:6, :]], axis=1)
        cv = wmv[0][...]
        proj = _dot_nt(jnp.broadcast_to(dmc, (8, 3 * D)).astype(BF16), wm_ref[...])
        g_cctx[...] = proj[0:1, :] * _dsilu(cv, _sigmoid(cv))
        for k in range(NGRP):
            g_bin[:, k * D:(k + 1) * D] = tot[6 + k:7 + k, :]
        nrm = tot[15:16, 0:DH]
        for h in range(1, NH):
            nrm = nrm + tot[15:16, h * DH:(h + 1) * DH]
        g_norm[...] = nrm
        g_lng[...] = tot[16:17, :]
        g_lnb[...] = tot[17:18, :]
        g_cb[...] = tot[21:22, :] + tot[29:30, :]
        g_cw[0] = tot[22:26, mine] + tot[30:34, mine]
        for ref, row in ((g_br, 18), (g_bi, 19), (g_lam, 20)):
            ref[0, 0:1, :] = tot[row:row + 1, mine]
            ref[0, 1:2, :] = tot[row + 8:row + 9, mine]
        lbl = lb_ref[0:2, mine]
        dl0 = tot[34:36, mine] * lbl * (1.0 - lbl)
        g_lbl[0] = dl0
        g_lbl[1] = -dl0
        for p in range(npar):
            d, mm, vv = _adam_math(g_refs[p][...], wmv[3 * p][...], wmv[3 * p + 1][...], wmv[3 * p + 2][...])
            upd[3 * p][...] = d
            upd[3 * p + 1][...] = mm
            upd[3 * p + 2][...] = vv

    flat = [t for wmv in params for t in wmv]
    shapes = [jax.ShapeDtypeStruct(wmv[0].shape, F32) for wmv in params]
    outs = pl.pallas_call(
        body, name="finalize_small",
        out_shape=[jax.ShapeDtypeStruct((8, DH), F32)] + shapes + [s for s in shapes for _ in range(3)],
        scratch_shapes=[pltpu.VMEM((PACK_ROWS, D), F32)],
    )(g_pack, lb, w_mod_full, *flat)
    grads = list(outs[1:1 + npar])
    upd = [tuple(outs[1 + npar + 3 * p:4 + npar + 3 * p]) for p in range(npar)]
    return outs[0], grads, upd


def _to_colmajor(t, rows):
    return t.reshape(rows, GRID_W, D).transpose(1, 0, 2).reshape(rows * GRID_W, D)


def _to_raster(t, rows):
    return t.reshape(GRID_W, rows, D).transpose(1, 0, 2).reshape(rows * GRID_W, D)


def _local_cols(t, me, width):
    return lax.dynamic_slice_in_dim(t, me * width, width, axis=t.ndim - 1)


def kernel(x, c, ctx, c_ctx, w_mod, b_mod, w_in, b_in, lb_logits, norm_a_g, conv_w, conv_b, w_r, b_r, w_i, b_i, lam, p_a, p_b, w_out, ln_g, ln_b, loss_target, m_c_ctx, m_w_mod, m_b_mod, m_w_in, m_b_in, m_lb_logits, m_norm_a_g, m_conv_w, m_conv_b, m_w_r, m_b_r, m_w_i, m_b_i, m_lam, m_p_a, m_p_b, m_w_out, m_ln_g, m_ln_b, v_c_ctx, v_w_mod, v_b_mod, v_w_in, v_b_in, v_lb_logits, v_norm_a_g, v_conv_w, v_conv_b, v_w_r, v_b_r, v_w_i, v_b_i, v_lam, v_p_a, v_p_b, v_w_out, v_ln_g, v_ln_b):
    me = _my_index()
    xs, cs, tgt = x[0], ctx[0], loss_target[0]
    t_len, c_len = xs.shape[0], cs.shape[0]
    rows = t_len // GRID_W
    wcols = w_in.shape[2]
    mcols = w_mod.shape[2]

    w_mod_b, w_in_b, p_a_b, p_b_b, w_out_b, w_r_b, w_i_b = _cast_bf16(
        [w_mod[0], w_in[0], p_a[0], p_b[0], w_out[0], w_r[0], w_i[0]])
    small = jnp.concatenate([lb_logits.reshape(4, DH), conv_w[0], b_r[0], b_i[0], lam[0], jnp.zeros((2, DH), F32),
                             c.reshape(8, DH)], axis=0)
    g_small, g_wmod = _all_gather([small, w_mod_b], "gather_params")

    def full_rows(lo, hi):
        return g_small[:, lo:hi, :].transpose(1, 0, 2).reshape(hi - lo, D)

    lbl_f, cw_f, br_f, bi_f, lam_f = full_rows(0, 4), full_rows(4, 8), full_rows(8, 10), full_rows(10, 12), full_rows(12, 14)
    c_all = g_small[:, 16:24, :].reshape(NDEV, D)
    w_mod_f = g_wmod.transpose(1, 0, 2).reshape(D, 3 * D)

    cc = jnp.concatenate([c.reshape(1, D), c_ctx.reshape(1, D), jnp.zeros((6, D), F32)], axis=0)
    lbl_p = jnp.concatenate([lbl_f.reshape(2, 2, D), jnp.zeros((2, 6, D), F32)], axis=1)
    mod, lb = _prep(cc, w_mod_f, b_mod, lbl_p)
    u_c = _modulate(cs, mod, 1, "modulate_c")
    z_x, u_x, w_in_f = _inproj_gather(xs, mod, w_in_b, b_in, "inproj_gather")
    z_c = _mm_bias(u_c, w_in_f, b_in, "inproj_c")

    zero_s = jnp.zeros((NH, DH, DH), F32)
    zero_v = jnp.zeros((1, D), F32)
    gla = {}
    out_w = [p_a_b, p_b_b, w_out_b]
    out_w_f = []
    for d in (0, 1):
        _, ssc, sfc = _gla_fwd(z_c, lb, zero_s, d, f"gla_fwd_c{d}")
        o_d, ssx, _, *gathered = _gla_fwd(z_x, lb, sfc, d, f"gla_fwd_x{d}", side=out_w[1:] if d else out_w[:1])
        out_w_f += [t.reshape(D, D) for t in gathered]
        gla[d] = (ssc, ssx, o_d)
    p_a_f, p_b_f, w_out_f = out_w_f

    x5_c = z_c[:, 5 * D:6 * D]
    x5_x = _to_colmajor(z_x[:, 5 * D:6 * D], rows)
    cb2 = conv_b.reshape(1, D)
    lru = {}
    h_sum = None
    for d in (0, 1):
        prm = (cw_f, cb2, w_r_b[d], br_f[d:d + 1], w_i_b[d], bi_f[d:d + 1], lam_f[d:d + 1])
        h_c, hin_c, hfin_c, *sav_c = _lru_fwd(x5_c, c_len, *prm, zero_v, None, d, f"lru_fwd_c{d}")
        h_x, hin_x, _, sav_a, sav_h, *h_sum = _lru_fwd(x5_x, rows, *prm, hfin_c, lru[0][3] if d else None, d,
                                                       f"lru_fwd_x{d}")
        lru[d] = ((cw_f, w_r_b[d], w_i_b[d], lam_f[d:d + 1]), h_c, hin_c, h_x, hin_x, tuple(sav_c), (sav_a, sav_h))
    hx = _to_raster(h_sum[0], rows)

    gn = jnp.tile(norm_a_g.reshape(1, DH), (1, NH))
    (dr, do, dhx, dz_m, oa, obb, yb16, dya, dyb, dout, mvec) = _merge(
        z_x, gla[0][2], gla[1][2], hx, xs, tgt, mod, gn, p_a_f, p_b_f, w_out_f, ln_g, ln_b)

    dhx_cm = _to_colmajor(dhx, rows)
    lru_dx_x = lru_dx_c = None
    for d in (0, 1):
        prm, h_c, hin_c, h_x, hin_x, sav_c, sav_x = lru[d]
        dx_dtype = BF16 if d else F32
        lru_dx_x, dwr, dwi, lvec, cg0 = _lru_bwd(x5_x, rows, *prm, sav_x, h_x, hin_x, dhx_cm, zero_v, lru_dx_x, None,
                                                 d, f"lru_bwd_x{d}", dx_dtype)
        lru_dx_c, dwr, dwi, lvec, _ = _lru_bwd(x5_c, c_len, *prm, sav_c, h_c, hin_c, None, cg0, lru_dx_c,
                                               (dwr, dwi, lvec), d, f"lru_bwd_c{d}", dx_dtype)
        lru[d] = (dwr, dwi, lvec)
    dz5_x = _to_raster(lru_dx_x, rows)
    dz5_c = lru_dx_c

    dpa = _mm_tn(oa, dya, None, "dpa", out_dtype=BF16)
    dpb = _mm_tn(obb, dyb, None, "dpb", out_dtype=BF16)
    dwo = _mm_tn(yb16, dout, None, "dwout", out_dtype=BF16)
    wr_pack = jnp.concatenate([lru[0][0], lru[1][0], lru[0][1], lru[1][1]], axis=0).reshape(4 * NH * DH, DH)

    gq_c = gv_c = None
    dzf_c, dlb = {}, {}
    gq_x, dzf_x0, gv_x, dlb_x, ds0, r_pa, r_pb, r_wo, r_wri = _gla_bwd(
        z_x, lb, gla[0][1], do, zero_s, None, None, 0, "gla_bwd_x0", f_dtype=BF16,
        side=[dpa, dpb, dwo, wr_pack], side_splits=[0, 0, 0, 0])
    gq_c, dzf_c[0], gv_c, dlb_c, _ = _gla_bwd(z_c, lb, gla[0][0], None, ds0, None, None, 0, "gla_bwd_c0")
    dlb[0] = dlb_x[0:1] + dlb_c[0:1]
    dz_g, dlb_x, ds0 = _gla_bwd(z_x, lb, gla[1][1], do, zero_s, gq_x, gv_x, 1, "gla_bwd_x1", into=(dz_m, dzf_x0))
    gq_c, dzf_c[1], gv_c, dlb_c, _ = _gla_bwd(z_c, lb, gla[1][0], None, ds0, gq_c, gv_c, 1, "gla_bwd_c1")
    dlb[1] = dlb_x[0:1] + dlb_c[0:1]

    bf = lambda t: t.astype(BF16)
    dz_x = lax.dynamic_update_slice(dz_g, dz5_x, (0, 5 * D))
    zc0 = jnp.zeros((c_len, D), BF16)
    dz_c = jnp.concatenate([bf(gq_c), bf(dzf_c[0]), bf(dzf_c[1]), bf(gv_c), zc0, dz5_c, zc0, zc0, zc0], axis=1)
    dwin_c, dbin_c = _mm_tn(u_c, dz_c, None, "dwin_c", with_colsum=True)

    grad_x, xvec = _input_grad(dz_x, w_in_f, xs, dr, mod, 0, "input_grad_x")
    r_win, dbin = _dwin_exchange(u_x, dz_x, dwin_c, dbin_c, "dwin_exchange")
    _, cvec = _input_grad(dz_c, w_in_f, cs, None, mod, 1, "input_grad_c")
    wri_piece = _sum_rows(r_wri, "sum_w_ri_piece")
    g_w_in, d_w_in, nm_w_in, nv_w_in = _sum_adamw(r_win, w_in, m_w_in, v_w_in, "update_w_in")
    g_p_a, d_p_a, nm_p_a, nv_p_a = _sum_adamw(r_pa, p_a, m_p_a, v_p_a, "update_p_a")
    g_p_b, d_p_b, nm_p_b, nv_p_b = _sum_adamw(r_pb, p_b, m_p_b, v_p_b, "update_p_b")
    g_w_out, d_w_out, nm_w_out, nv_w_out = _sum_adamw(r_wo, w_out, m_w_out, v_w_out, "update_w_out")

    dlb_rows = jnp.concatenate([dlb[0], dlb[1]], axis=0)
    pack = jnp.concatenate([
        xvec[0:1], xvec[1:2], mvec[0:1],
        cvec[0:1], cvec[1:2], jnp.zeros((1, D), F32),
        dbin.reshape(NGRP, D),
        mvec[3:4], mvec[1:2], mvec[2:3],
        lru[0][2][0:8], lru[1][2][0:3],
        lru[1][2][3:8],
        dlb_rows,
        mvec[4:5],
        jnp.zeros((3, D), F32)], axis=0)
    g_pack, g_wri = _all_gather([pack, wri_piece], "gather_small_grads")

    dmx = g_pack[:, 0:3, :].reshape(NDEV, 3 * D)
    dmc = g_pack[:, 3:6, :].reshape(NDEV, 3 * D)
    grad_w_mod = _wmod_grad(c_all.T, c_ctx.reshape(D, 1), _local_cols(dmx, me, mcols), _local_cols(dmc, me, mcols),
                            "grad_w_mod").reshape(1, D, mcols)
    small_params = [(c_ctx.reshape(1, D), m_c_ctx.reshape(1, D), v_c_ctx.reshape(1, D)), (b_mod, m_b_mod, v_b_mod),
                    (b_in, m_b_in, v_b_in), (lb_logits, m_lb_logits, v_lb_logits), (norm_a_g, m_norm_a_g, v_norm_a_g),
                    (conv_w, m_conv_w, v_conv_w), (conv_b, m_conv_b, v_conv_b), (b_r, m_b_r, v_b_r),
                    (b_i, m_b_i, v_b_i), (lam, m_lam, v_lam), (ln_g, m_ln_g, v_ln_g), (ln_b, m_ln_b, v_ln_b)]
    loss_tile, small_g, small_upd = _finalize_small(g_pack, lb, w_mod_f, small_params)
    loss = loss_tile[0, 0]
    (grad_c_ctx, grad_b_mod, grad_b_in, grad_lb_logits, grad_norm_a_g, grad_conv_w, grad_conv_b, grad_b_r, grad_b_i,
     grad_lam, grad_ln_g, grad_ln_b) = small_g
    small_upd[0] = tuple(t.reshape(c_ctx.shape) for t in small_upd[0])
    (o_c_ctx, o_b_mod, o_b_in, o_lb, o_norm, o_conv_w, o_conv_b, o_b_r, o_b_i, o_lam, o_ln_g, o_ln_b) = small_upd

    half = 2 * NH * DH
    g_ri = g_wri.reshape(2 * half, DH)
    grad_w_r, grad_w_i = g_ri[:half].reshape(w_r.shape), g_ri[half:].reshape(w_i.shape)
    d_w_r, nm_w_r, nv_w_r = _adamw(grad_w_r, w_r, m_w_r, v_w_r, "update_w_r")
    d_w_i, nm_w_i, nv_w_i = _adamw(grad_w_i, w_i, m_w_i, v_w_i, "update_w_i")

    d_w_mod, nm_w_mod, nv_w_mod = _adamw(grad_w_mod, w_mod, m_w_mod, v_w_mod, "update_w_mod")

    grads = [grad_c_ctx.reshape(c_ctx.shape), grad_w_mod, grad_b_mod, g_w_in, grad_b_in, grad_lb_logits, grad_norm_a_g,
             grad_conv_w, grad_conv_b, grad_w_r, grad_b_r, grad_w_i, grad_b_i, grad_lam, g_p_a, g_p_b, g_w_out,
             grad_ln_g, grad_ln_b]
    per_kind = []
    for k in range(3):
        per_kind.append([
            o_c_ctx[k], (d_w_mod, nm_w_mod, nv_w_mod)[k], o_b_mod[k], (d_w_in, nm_w_in, nv_w_in)[k], o_b_in[k], o_lb[k],
            o_norm[k], o_conv_w[k], o_conv_b[k], (d_w_r, nm_w_r, nv_w_r)[k], o_b_r[k], (d_w_i, nm_w_i, nv_w_i)[k],
            o_b_i[k], o_lam[k], (d_p_a, nm_p_a, nv_p_a)[k], (d_p_b, nm_p_b, nv_p_b)[k], (d_w_out, nm_w_out, nv_w_out)[k],
            o_ln_g[k], o_ln_b[k]])
    return (loss, grad_x.reshape(x.shape), *grads, *per_kind[0], *per_kind[1], *per_kind[2])
```

```python
import functools

import jax
import jax.numpy as jnp
from jax import lax
from jax.experimental import pallas as pl
from jax.experimental.pallas import tpu as pltpu

F32 = jnp.float32
BF16 = jnp.bfloat16

D = 1024
NH = 8
DH = 128
CHUNK = 64
GLA_HEADS_PER_STEP = 8
GRID_W = 64
NGRP = 9
NDEV = 8
RG_C = 8.0
ALPHA = 2.0 ** 0.25
LN_EPS = 1e-5
RMS_EPS = 1e-6
Q_SCALE = DH ** -0.5
ADAM_LR, ADAM_B1, ADAM_B2, ADAM_EPS, ADAM_WD, ADAM_STEP = 1e-3, 0.9, 0.999, 1e-8, 0.01, 10
ADAM_C1 = 1.0 / (1.0 - ADAM_B1 ** ADAM_STEP)
ADAM_C2 = 1.0 / (1.0 - ADAM_B2 ** ADAM_STEP)

ANY = pl.BlockSpec(memory_space=pl.ANY)


def _sigmoid(t):
    return 1.0 / (1.0 + jnp.exp(-t))


def _dsilu(t, s):
    return s * (1.0 + t * (1.0 - s))


def _dot(a, b):
    return jnp.dot(a, b, preferred_element_type=F32)


def _dot_nt(a, b):
    return lax.dot_general(a, b, (((1,), (1,)), ((), ())), preferred_element_type=F32)


def _dot_tn(a, b):
    return lax.dot_general(a, b, (((0,), (0,)), ((), ())), preferred_element_type=F32)


def _my_index():
    return 4 * lax.axis_index("x") + 2 * lax.axis_index("y") + lax.axis_index("c")


def _dev_tuple(j):
    return (j >> 2, (j >> 1) & 1, j & 1)


def _exchange_sems(n):
    return [pltpu.SemaphoreType.DMA((n * NDEV,)), pltpu.SemaphoreType.DMA((n * NDEV,)), pltpu.SemaphoreType.DMA((n,))]


def _exchange(ins, outs, sems, piece_of=None):
    send_sems, recv_sems, loc_sems = sems
    n = len(ins)

    def src(a, p):
        return ins[a] if piece_of is None else piece_of(ins[a], a, p)

    def push(a, t):
        me, p = _my_index(), _step_peer(t)
        return pltpu.make_async_remote_copy(
            src_ref=src(a, p), dst_ref=outs[a].at[me],
            send_sem=send_sems.at[a * NDEV + t], recv_sem=recv_sems.at[a * NDEV + me],
            device_id=_dev_of(p), device_id_type=pl.DeviceIdType.MESH)

    def local(a):
        me = _my_index()
        return pltpu.make_async_copy(src(a, me), outs[a].at[me], loc_sems.at[a])

    def start():
        for a in range(n):
            local(a).start()
        for t in range(NDEV - 1):
            for a in range(n):
                push(a, t).start()

    def finish():
        me = _my_index()
        for t in range(NDEV - 1):
            for a in range(n):
                push(a, t).wait_send()
        for j in range(NDEV):
            @pl.when(me != j)
            def _():
                for a in range(n):
                    pltpu.make_async_remote_copy(
                        src_ref=src(a, j), dst_ref=outs[a].at[j],
                        send_sem=send_sems.at[a * NDEV], recv_sem=recv_sems.at[a * NDEV + j],
                        device_id=_dev_tuple(j), device_id_type=pl.DeviceIdType.MESH).wait_recv()
        for a in range(n):
            local(a).wait()

    return start, finish


def _all_gather(shards, name):
    n = len(shards)

    def body(*refs):
        start, finish = _exchange(refs[:n], refs[n:2 * n], refs[2 * n:])
        start()
        finish()

    return pl.pallas_call(
        body, name=name,
        out_shape=[jax.ShapeDtypeStruct((NDEV,) + s.shape, s.dtype) for s in shards],
        in_specs=[ANY] * n, out_specs=[ANY] * n, scratch_shapes=_exchange_sems(n),
    )(*shards)


def _pieces(parts, splits):
    shapes = []
    for part, split in zip(parts, splits):
        r, c = part.shape
        shapes.append((r // NDEV, c) if split == 0 else (r, c // NDEV))

    def piece_of(ref, a, j):
        pr, pc = shapes[a]
        if splits[a] == 0:
            start = j * pr if isinstance(j, int) else pl.multiple_of(j * pr, pr)
            return ref.at[pl.ds(start, pr), :]
        start = j * pc if isinstance(j, int) else pl.multiple_of(j * pc, pc)
        return ref.at[:, pl.ds(start, pc)]

    return shapes, piece_of


_STEP_MASKS = ((2, 4, 6, 3, 5, 7, 1, 0), (4, 2, 6, 5, 3, 7, 1, 0))
_GATHER_MASKS = ((0, 1, 2, 4, 3, 5, 6, 7), (0, 1, 4, 2, 5, 3, 6, 7))


def _peer_schedule(table):
    tab = jnp.array(table, jnp.int32)
    return jnp.bitwise_xor(_my_index(), tab[lax.axis_index("c")])


def _step_peer(s, table=_STEP_MASKS):
    def pick(row):
        if isinstance(s, int):
            return jnp.int32(row[s])
        m = jnp.int32(row[NDEV - 1])
        for t in range(NDEV - 2, -1, -1):
            m = jnp.where(s == t, jnp.int32(row[t]), m)
        return m
    mask = jnp.where(lax.axis_index("c") == 0, pick(table[0]), pick(table[1]))
    return jnp.bitwise_xor(_my_index(), mask)


def _dev_of(p):
    return (p // 4, (p // 2) % 2, p % 2)


def _all_gather_two_level(shards, name):
    n = len(shards)
    direct = (1, 2, 3, 6)
    relay_sem = {2: 4, 3: 5, 6: 7}

    def body(*refs):
        ins, outs = refs[:n], refs[n:2 * n]
        send_sems, recv_sems, loc_sems = refs[2 * n:]
        me = _my_index()
        sibling = _step_peer(1, _GATHER_MASKS)

        def push(a, t):
            return pltpu.make_async_remote_copy(
                src_ref=ins[a], dst_ref=outs[a].at[me], send_sem=send_sems.at[a * NDEV + t],
                recv_sem=recv_sems.at[a * NDEV + me],
                device_id=_dev_of(_step_peer(t, _GATHER_MASKS)), device_id_type=pl.DeviceIdType.MESH)

        def arrival(a, p):
            return pltpu.make_async_remote_copy(
                src_ref=ins[a], dst_ref=outs[a].at[p], send_sem=send_sems.at[a * NDEV],
                recv_sem=recv_sems.at[a * NDEV + p], device_id=_dev_of(p), device_id_type=pl.DeviceIdType.MESH)

        def relay(a, t):
            p = _step_peer(t, _GATHER_MASKS)
            return pltpu.make_async_remote_copy(
                src_ref=outs[a].at[p], dst_ref=outs[a].at[p], send_sem=send_sems.at[a * NDEV + relay_sem[t]],
                recv_sem=recv_sems.at[a * NDEV + p], device_id=_dev_of(sibling), device_id_type=pl.DeviceIdType.MESH)

        for a in range(n):
            pltpu.make_async_copy(ins[a], outs[a].at[me], loc_sems.at[a]).start()
        for t in direct:
            for a in range(n):
                push(a, t).start()
        for t in relay_sem:
            for a in range(n):
                arrival(a, _step_peer(t, _GATHER_MASKS)).wait_recv()
                relay(a, t).start()
        for t in (1, 4, 5, 7):
            for a in range(n):
                arrival(a, _step_peer(t, _GATHER_MASKS)).wait_recv()
        for a in range(n):
            for t in direct:
                push(a, t).wait_send()
            for t in relay_sem:
                relay(a, t).wait_send()
            pltpu.make_async_copy(ins[a], outs[a].at[me], loc_sems.at[a]).wait()

    return pl.pallas_call(
        body, name=name,
        out_shape=[jax.ShapeDtypeStruct((NDEV,) + s.shape, s.dtype) for s in shards],
        in_specs=[ANY] * n, out_specs=[ANY] * n, scratch_shapes=_exchange_sems(n),
    )(*shards)


def _dwin_exchange(u, dz, init, cs_init, name):
    m, ka = u.shape
    n = dz.shape[1]
    pc = n // NDEV
    tk = _row_tile(m, 512)
    nk = m // tk

    def body(pidx_ref, u_ref, dz_ref, init_ref, csi_ref, rwin, cs_ref, acc, sbuf, wsend, wrecv, wloc):
        s, k = pl.program_id(0), pl.program_id(1)
        me = _my_index()

        def slab_copy(slot, p):
            return pltpu.make_async_remote_copy(
                src_ref=sbuf.at[slot], dst_ref=rwin.at[me], send_sem=wsend.at[slot], recv_sem=wrecv.at[me],
                device_id=_dev_of(p), device_id_type=pl.DeviceIdType.MESH)

        @pl.when(k == 0)
        def _():
            acc[...] = init_ref[...]
            cs_ref[...] = csi_ref[...]

        bv = dz_ref[...]
        acc[...] += _dot_tn(u_ref[...], bv)
        cs_ref[...] += jnp.sum(bv.astype(F32), axis=0, keepdims=True)

        @pl.when(k == nk - 1)
        def _():
            slot = s % 2

            @pl.when(s >= 2)
            def _():
                slab_copy(slot, me).wait_send()

            sbuf[slot] = acc[...].astype(BF16)

            @pl.when(s < NDEV - 1)
            def _():
                slab_copy(slot, _step_peer(s)).start()

            @pl.when(s == NDEV - 1)
            def _():
                own = pltpu.make_async_copy(sbuf.at[slot], rwin.at[me], wloc.at[0])
                own.start()
                slab_copy(1 - slot, me).wait_send()
                for j in range(NDEV):
                    @pl.when(me != j)
                    def _():
                        pltpu.make_async_remote_copy(
                            src_ref=sbuf.at[0], dst_ref=rwin.at[j], send_sem=wsend.at[0], recv_sem=wrecv.at[j],
                            device_id=_dev_tuple(j), device_id_type=pl.DeviceIdType.MESH).wait_recv()
                own.wait()

    grid_spec = pltpu.PrefetchScalarGridSpec(
        num_scalar_prefetch=1, grid=(NDEV, nk),
        in_specs=[pl.BlockSpec((tk, ka), lambda s, k, pidx: (k, 0)),
                  pl.BlockSpec((tk, pc), lambda s, k, pidx: (k, pidx[s])),
                  pl.BlockSpec((ka, pc), lambda s, k, pidx: (0, pidx[s])),
                  pl.BlockSpec((1, pc), lambda s, k, pidx: (0, pidx[s]))],
        out_specs=[ANY, pl.BlockSpec((1, pc), lambda s, k, pidx: (0, pidx[s]))],
        scratch_shapes=[pltpu.VMEM((ka, pc), F32), pltpu.VMEM((2, ka, pc), BF16),
                        pltpu.SemaphoreType.DMA((2,)), pltpu.SemaphoreType.DMA((NDEV,)), pltpu.SemaphoreType.DMA((1,))])
    return pl.pallas_call(
        body, name=name, grid_spec=grid_spec,
        out_shape=[jax.ShapeDtypeStruct((NDEV, ka, pc), BF16), jax.ShapeDtypeStruct((1, n), F32)],
    )(_peer_schedule(_STEP_MASKS), u, dz, init, cs_init)


def _inproj_gather(u, w_loc, bias, name):
    m, k = u.shape
    pc = w_loc.shape[1]
    n = pc * NDEV
    tm = _row_tile(m, 512)
    ni = m // tm
    direct = (1, 2, 3, 6)
    relay_sem = {2: 4, 3: 5, 6: 7}

    def body(pidx_ref, u_ref, b_ref, wl_ref, z_ref, wall, wbuf, wsend, wrecv, ldsem, ownsem):
        s, i = pl.program_id(0), pl.program_id(1)
        me = _my_index()

        def shard_push(t):
            return pltpu.make_async_remote_copy(
                src_ref=wl_ref, dst_ref=wall.at[me], send_sem=wsend.at[t], recv_sem=wrecv.at[me],
                device_id=_dev_of(_step_peer(t, _GATHER_MASKS)), device_id_type=pl.DeviceIdType.MESH)

        def relay(t):
            p = _step_peer(t, _GATHER_MASKS)
            return pltpu.make_async_remote_copy(
                src_ref=wall.at[p], dst_ref=wall.at[p], send_sem=wsend.at[relay_sem[t]], recv_sem=wrecv.at[p],
                device_id=_dev_of(_step_peer(1, _GATHER_MASKS)), device_id_type=pl.DeviceIdType.MESH)

        def load(slot, src):
            return pltpu.make_async_copy(src, wbuf.at[slot], ldsem.at[slot])

        own = pltpu.make_async_copy(wl_ref, wall.at[me], ownsem.at[0])

        @pl.when((s == 0) & (i == 0))
        def _():
            own.start()
            load(0, wl_ref).start()
            for t in direct:
                shard_push(t).start()

        @pl.when((i == ni // 2) & (s < NDEV - 1))
        def _():
            nxt = _step_peer(s + 1, _GATHER_MASKS)
            pltpu.make_async_remote_copy(
                src_ref=wl_ref, dst_ref=wall.at[nxt], send_sem=wsend.at[0], recv_sem=wrecv.at[nxt],
                device_id=_dev_of(nxt), device_id_type=pl.DeviceIdType.MESH).wait_recv()
            for t in relay_sem:
                @pl.when(s + 1 == t)
                def _():
                    relay(t).start()
            load((s + 1) % 2, wall.at[nxt]).start()

        @pl.when(i == 0)
        def _():
            load(s % 2, wl_ref).wait()

        z_ref[...] = _dot(u_ref[...], wbuf[s % 2]) + b_ref[...]

        @pl.when((s == NDEV - 1) & (i == ni - 1))
        def _():
            own.wait()
            for t in direct:
                shard_push(t).wait_send()
            for t in relay_sem:
                relay(t).wait_send()

    grid_spec = pltpu.PrefetchScalarGridSpec(
        num_scalar_prefetch=1, grid=(NDEV, ni),
        in_specs=[pl.BlockSpec((tm, k), lambda s, i, pidx: (i, 0)),
                  pl.BlockSpec((1, pc), lambda s, i, pidx: (0, pidx[s])), ANY],
        out_specs=[pl.BlockSpec((tm, pc), lambda s, i, pidx: (i, pidx[s])), ANY],
        scratch_shapes=[pltpu.VMEM((2, k, pc), BF16),
                        pltpu.SemaphoreType.DMA((NDEV,)), pltpu.SemaphoreType.DMA((NDEV,)),
                        pltpu.SemaphoreType.DMA((2,)), pltpu.SemaphoreType.DMA((1,))])
    return pl.pallas_call(
        body, name=name, grid_spec=grid_spec,
        out_shape=[jax.ShapeDtypeStruct((m, n), F32), jax.ShapeDtypeStruct((NDEV, k, pc), w_loc.dtype)],
    )(_peer_schedule(_GATHER_MASKS), u, bias, w_loc)


def _adam_math(g, w, m, v):
    m2 = ADAM_B1 * m + (1.0 - ADAM_B1) * g
    v2 = ADAM_B2 * v + (1.0 - ADAM_B2) * (g * g)
    delta = -ADAM_LR * ((m2 * ADAM_C1) / (jnp.sqrt(v2 * ADAM_C2) + ADAM_EPS) + ADAM_WD * w)
    return delta, m2, v2


def _row_tile(r, cap):
    t = min(r, cap)
    while r % t:
        t //= 2
    return t


def _adamw(g, w, m, v, name):
    shape = w.shape
    cols = shape[-1] if w.ndim >= 2 and shape[-1] % 128 == 0 else 128
    g2, w2, m2, v2 = (t.reshape(-1, cols) for t in (g, w, m, v))
    r = g2.shape[0]
    tr = _row_tile(r, 256)

    def body(g_ref, w_ref, m_ref, v_ref, d_ref, mo_ref, vo_ref):
        d, mm, vv = _adam_math(g_ref[...], w_ref[...], m_ref[...], v_ref[...])
        d_ref[...] = d
        mo_ref[...] = mm
        vo_ref[...] = vv

    spec = pl.BlockSpec((tr, cols), lambda i: (i, 0))
    outs = pl.pallas_call(
        body, name=name, grid=(r // tr,),
        out_shape=[jax.ShapeDtypeStruct((r, cols), F32)] * 3,
        in_specs=[spec] * 4, out_specs=[spec] * 3,
    )(g2, w2, m2, v2)
    return tuple(o.reshape(shape) for o in outs)


def _sum_adamw(parts, w, m, v, name):
    _, r, c = parts.shape
    shape = w.shape
    w2, m2, v2 = (t.reshape(r, c) for t in (w, m, v))
    tr = _row_tile(r, 128)

    def body(p_ref, w_ref, m_ref, v_ref, g_ref, d_ref, mo_ref, vo_ref):
        g = p_ref[0].astype(F32)
        for k in range(1, NDEV):
            g = g + p_ref[k].astype(F32)
        d, mm, vv = _adam_math(g, w_ref[...], m_ref[...], v_ref[...])
        g_ref[...] = g
        d_ref[...] = d
        mo_ref[...] = mm
        vo_ref[...] = vv

    spec = pl.BlockSpec((tr, c), lambda i: (i, 0))
    outs = pl.pallas_call(
        body, name=name, grid=(r // tr,),
        out_shape=[jax.ShapeDtypeStruct((r, c), F32)] * 4,
        in_specs=[pl.BlockSpec((NDEV, tr, c), lambda i: (0, i, 0))] + [spec] * 3, out_specs=[spec] * 4,
    )(parts, w2, m2, v2)
    return tuple(o.reshape(shape) for o in outs)


def _sum_rows(parts, name):
    _, r, c = parts.shape

    def body(p_ref, o_ref):
        g = p_ref[0]
        for k in range(1, NDEV):
            g = g + p_ref[k]
        o_ref[...] = g

    return pl.pallas_call(
        body, name=name, out_shape=jax.ShapeDtypeStruct((r, c), F32),
    )(parts)


def _cast_bf16(arrays):
    n = len(arrays)

    def body(*refs):
        for src, dst in zip(refs[:n], refs[n:]):
            dst[...] = src[...].astype(BF16)

    return pl.pallas_call(
        body, name="cast_weights", out_shape=[jax.ShapeDtypeStruct(a.shape, BF16) for a in arrays],
    )(*arrays)


def _prep(cc, w_mod_full, b_mod, lbl):
    def body(cc_ref, w_ref, b_ref, l_ref, mod_ref, lb_ref):
        t = cc_ref[...]
        s = (t * _sigmoid(t)).astype(BF16)
        mod_ref[...] = _dot(s, w_ref[...]) + b_ref[...]
        lb_ref[...] = _sigmoid(l_ref[0] - l_ref[1])

    return pl.pallas_call(
        body, name="prep",
        out_shape=[jax.ShapeDtypeStruct((8, 3 * D), F32), jax.ShapeDtypeStruct((8, D), F32)],
    )(cc, w_mod_full, b_mod, lbl)


def _modulate(xin, mod, row, name):
    n = xin.shape[0]
    tm = _row_tile(n, 512)

    def body(x_ref, mod_ref, u_ref):
        sh = mod_ref[row:row + 1, 0:D]
        sc = mod_ref[row:row + 1, D:2 * D]
        u_ref[...] = (x_ref[...] * (1.0 + sc) + sh).astype(BF16)

    return pl.pallas_call(
        body, name=name, grid=(n // tm,),
        out_shape=jax.ShapeDtypeStruct((n, D), BF16),
        in_specs=[pl.BlockSpec((tm, D), lambda i: (i, 0)), pl.BlockSpec((8, 3 * D), lambda i: (0, 0))],
        out_specs=pl.BlockSpec((tm, D), lambda i: (i, 0)),
    )(xin, mod)


def _mm_bias(a, w_all, bias, name):
    m, k = a.shape
    tn = w_all.shape[2]
    n = tn * NDEV
    tm = _row_tile(m, 512)

    def body(a_ref, b_ref, bias_ref, o_ref):
        o_ref[...] = _dot(a_ref[...], b_ref[0]) + bias_ref[...]

    return pl.pallas_call(
        body, name=name, grid=(NDEV, m // tm),
        out_shape=jax.ShapeDtypeStruct((m, n), F32),
        in_specs=[pl.BlockSpec((tm, k), lambda j, i: (i, 0)), pl.BlockSpec((1, k, tn), lambda j, i: (j, 0, 0)),
                  pl.BlockSpec((1, tn), lambda j, i: (0, j))],
        out_specs=pl.BlockSpec((tm, tn), lambda j, i: (i, j)),
    )(a, w_all, bias)


def _mm_tn(a, b, init, name, with_colsum=False, colsum_init=None, out_dtype=F32):
    m, ka = a.shape
    n = b.shape[1]
    tk = _row_tile(m, 512)
    tn = 1024
    nk = m // tk
    has_init = init is not None

    def body(*refs):
        a_ref, b_ref = refs[0], refs[1]
        pos = 2
        init_ref = cs_init_ref = None
        if has_init:
            init_ref = refs[pos]
            pos += 1
            if with_colsum:
                cs_init_ref = refs[pos]
                pos += 1
        o_ref = refs[pos]
        cs_ref = refs[pos + 1] if with_colsum else None
        acc = refs[-1]
        k = pl.program_id(1)

        @pl.when(k == 0)
        def _():
            if has_init:
                acc[...] = init_ref[...]
                if with_colsum:
                    cs_ref[...] = cs_init_ref[...]
            else:
                acc[...] = jnp.zeros_like(acc)
                if with_colsum:
                    cs_ref[...] = jnp.zeros_like(cs_ref)

        bv = b_ref[...]
        acc[...] += _dot_tn(a_ref[...], bv)
        if with_colsum:
            cs_ref[...] += jnp.sum(bv.astype(F32), axis=0, keepdims=True)

        @pl.when(k == nk - 1)
        def _():
            o_ref[...] = acc[...].astype(out_dtype)

    in_specs = [pl.BlockSpec((tk, ka), lambda j, k: (k, 0)), pl.BlockSpec((tk, tn), lambda j, k: (k, j))]
    args = [a, b]
    if has_init:
        in_specs.append(pl.BlockSpec((ka, tn), lambda j, k: (0, j)))
        args.append(init)
        if with_colsum:
            in_specs.append(pl.BlockSpec((1, tn), lambda j, k: (0, j)))
            args.append(colsum_init)
    out_shape = [jax.ShapeDtypeStruct((ka, n), out_dtype)]
    out_specs = [pl.BlockSpec((ka, tn), lambda j, k: (0, j))]
    if with_colsum:
        out_shape.append(jax.ShapeDtypeStruct((1, n), F32))
        out_specs.append(pl.BlockSpec((1, tn), lambda j, k: (0, j)))
    outs = pl.pallas_call(
        body, name=name, grid=(n // tn, nk), out_shape=out_shape, in_specs=in_specs, out_specs=out_specs,
        scratch_shapes=[pltpu.VMEM((ka, tn), F32)],
    )(*args)
    return outs if with_colsum else outs[0]


def _input_grad(dz, w_all, xin, dr, mod, row, name, side=(), side_splits=()):
    m, n = dz.shape
    tm = _row_tile(m, 512)
    tk = w_all.shape[2]
    nk = NDEV
    ni = m // tm
    has_dr = dr is not None
    ns = len(side)
    piece_shapes, piece_of = _pieces(side, side_splits)

    def body(*refs):
        dz_ref, w_ref, x_ref = refs[:3]
        pos = 3
        dr_ref = refs[pos] if has_dr else None
        pos += int(has_dr)
        mod_ref = refs[pos]
        side_in = refs[pos + 1:pos + 1 + ns]
        pos += 1 + ns
        gx_ref = refs[pos] if has_dr else None
        pos += int(has_dr)
        vec_ref = refs[pos]
        side_out = refs[pos + 1:pos + 1 + ns]
        acc = refs[pos + 1 + ns]
        i, k = pl.program_id(0), pl.program_id(1)
        if ns:
            side_start, side_finish = _exchange(side_in, side_out, refs[pos + 2 + ns:], piece_of)

            @pl.when((i == 0) & (k == 0))
            def _():
                side_start()

        @pl.when(k == 0)
        def _():
            acc[...] = jnp.zeros_like(acc)

        @pl.when((i == 0) & (k == 0))
        def _():
            vec_ref[...] = jnp.zeros_like(vec_ref)

        acc[...] += _dot_nt(dz_ref[...], w_ref[0])

        @pl.when(k == nk - 1)
        def _():
            du = acc[...]
            xv = x_ref[...]
            if has_dr:
                sc = mod_ref[row:row + 1, D:2 * D]
                gx_ref[...] = ALPHA * dr_ref[...] + du * (1.0 + sc)
            vec_ref[0:1, :] += jnp.sum(du, axis=0, keepdims=True)
            vec_ref[1:2, :] += jnp.sum(du * xv, axis=0, keepdims=True)

        if ns:
            @pl.when((i == ni - 1) & (k == nk - 1))
            def _():
                side_finish()

    row_spec = pl.BlockSpec((tm, D), lambda i, k: (i, 0))
    in_specs = [pl.BlockSpec((tm, tk), lambda i, k: (i, k)), pl.BlockSpec((1, D, tk), lambda i, k: (k, 0, 0)), row_spec]
    args = [dz, w_all, xin]
    if has_dr:
        in_specs.append(row_spec)
        args.append(dr)
    in_specs.append(pl.BlockSpec((8, 3 * D), lambda i, k: (0, 0)))
    args.append(mod)
    in_specs += [ANY] * ns
    args += list(side)
    out_shape, out_specs = [], []
    if has_dr:
        out_shape.append(jax.ShapeDtypeStruct((m, D), F32))
        out_specs.append(row_spec)
    out_shape.append(jax.ShapeDtypeStruct((8, D), F32))
    out_specs.append(pl.BlockSpec((8, D), lambda i, k: (0, 0)))
    out_shape += [jax.ShapeDtypeStruct((NDEV,) + piece_shapes[a], side[a].dtype) for a in range(ns)]
    out_specs += [ANY] * ns
    outs = pl.pallas_call(
        body, name=name, grid=(ni, nk), out_shape=out_shape, in_specs=in_specs, out_specs=out_specs,
        scratch_shapes=[pltpu.VMEM((tm, D), F32)] + (_exchange_sems(ns) if ns else []),
    )(*args)
    return tuple(outs) if has_dr else (None, *outs)


def _tri(reverse):
    r = lax.broadcasted_iota(jnp.int32, (CHUNK, CHUNK), 0)
    c = lax.broadcasted_iota(jnp.int32, (CHUNK, CHUNK), 1)
    return (c >= r) if reverse else (c <= r)


def _cum_f32(tri_b, t):
    hi = t.astype(BF16)
    r1 = t - hi.astype(F32)
    mid = r1.astype(BF16)
    lo = (r1 - mid.astype(F32)).astype(BF16)
    return _dot(tri_b, hi) + _dot(tri_b, mid) + _dot(tri_b, lo)


def _gla_features(zq, zf, lb):
    sq = _sigmoid(zq)
    q = zq * sq * Q_SCALE
    sf = _sigmoid(zf)
    f = lb + (1.0 - lb) * sf
    return q, sq, f, sf


def _gla_block(n):
    return 256 if n % 256 == 0 else CHUNK


def _gla_fwd(z, lb, s0, d, name, side=()):
    n = z.shape[0]
    blk = _gla_block(n)
    nb, npb = n // blk, blk // CHUNK
    reverse = d == 1
    last = 0 if reverse else CHUNK - 1
    order = list(range(npb))[::-1] if reverse else list(range(npb))
    ns = len(side)

    def bmap(i):
        return nb - 1 - i if reverse else i

    hp = GLA_HEADS_PER_STEP
    hw = hp * DH
    units = [(hh, cidx) for hh in range(hp) for cidx in order]

    def body(zq_ref, zf_ref, zv_ref, lb_ref, s0_ref, *rest):
        side_in = rest[:ns]
        o_ref, ss_ref, sf_ref = rest[ns:ns + 3]
        side_out = rest[ns + 3:2 * ns + 3]
        st = rest[2 * ns + 3]
        i = pl.program_id(1)
        if ns:
            side_start, side_finish = _exchange(side_in, side_out, rest[2 * ns + 4:])

            @pl.when((pl.program_id(0) == 0) & (i == 0))
            def _():
                side_start()

        @pl.when(i == 0)
        def _():
            st[...] = s0_ref[...]

        mask = _tri(reverse)
        tri_b = jnp.where(mask, 1.0, 0.0).astype(BF16)
        feat = {}
        for u in units:
            hh, cidx = u
            rows, cols = pl.ds(cidx * CHUNK, CHUNK), pl.ds(hh * DH, DH)
            q, _, f, _ = _gla_features(zq_ref[rows, cols], zf_ref[rows, cols], lb_ref[d:d + 1, cols])
            feat[u] = (q, 1.0 - f, jnp.log(f), zv_ref[rows, cols].astype(BF16))
        dec = {u: _cum_f32(tri_b, feat[u][2]) for u in units}
        ops = {}
        for u in units:
            q, k, _, vb = feat[u]
            g = dec[u]
            gl = g[last:last + 1, :]
            ops[u] = ((q * jnp.exp(g)).astype(BF16), (k * jnp.exp(-g)).astype(BF16),
                      (k * jnp.exp(gl - g)).astype(BF16), jnp.exp(gl), vb)
        att = {u: jnp.where(mask, _dot_nt(ops[u][0], ops[u][1]), 0.0).astype(BF16) for u in units}
        upd = {u: _dot_tn(ops[u][4], ops[u][2]) for u in units}
        intra = {u: _dot(att[u], ops[u][4]) for u in units}
        s_in = {}
        for hh in range(hp):
            s = st[hh]
            for cidx in order:
                s_in[(hh, cidx)] = s
                s = s * ops[(hh, cidx)][3] + upd[(hh, cidx)]
            st[hh] = s
            sf_ref[hh] = s
        for u in units:
            hh, cidx = u
            rows, cols = pl.ds(cidx * CHUNK, CHUNK), pl.ds(hh * DH, DH)
            o_ref[rows, cols] = intra[u] + _dot_nt(ops[u][0], s_in[u].astype(BF16))
            ss_ref[hh, cidx] = s_in[u]

        if ns:
            @pl.when((pl.program_id(0) == NH // hp - 1) & (i == nb - 1))
            def _():
                side_finish()

    def col(g):
        return lambda h, i: (bmap(i), g * (NH // hp) + h)

    return pl.pallas_call(
        body, name=name, grid=(NH // hp, nb),
        out_shape=[jax.ShapeDtypeStruct((n, D), F32), jax.ShapeDtypeStruct((NH, n // CHUNK, DH, DH), F32),
                   jax.ShapeDtypeStruct((NH, DH, DH), F32)]
        + [jax.ShapeDtypeStruct((NDEV,) + t.shape, t.dtype) for t in side],
        in_specs=[pl.BlockSpec((blk, hw), col(0)), pl.BlockSpec((blk, hw), col(1 + d)),
                  pl.BlockSpec((blk, hw), col(3)), pl.BlockSpec((8, hw), lambda h, i: (0, h)),
                  pl.BlockSpec((hp, DH, DH), lambda h, i: (h, 0, 0))] + [ANY] * ns,
        out_specs=[pl.BlockSpec((blk, hw), lambda h, i: (bmap(i), h)),
                   pl.BlockSpec((hp, npb, DH, DH), lambda h, i: (h, bmap(i), 0, 0)),
                   pl.BlockSpec((hp, DH, DH), lambda h, i: (h, 0, 0))] + [ANY] * ns,
        scratch_shapes=[pltpu.VMEM((hp, DH, DH), F32)] + (_exchange_sems(ns) if ns else []),
    )(z, z, z, lb, s0, *side)


def _gla_bwd(z, lb, s_start, do, ds_fin, acc_q, acc_v, d, name, f_dtype=F32, into=None, side=(), side_splits=()):
    n = z.shape[0]
    blk = _gla_block(n)
    nb, npb = n // blk, blk // CHUNK
    reverse = d == 1
    last = 0 if reverse else CHUNK - 1
    order = list(range(npb)) if reverse else list(range(npb))[::-1]
    has_do = do is not None
    has_acc = acc_q is not None
    fused = into is not None
    assert not fused or d == 1
    ns = len(side)
    assert not (fused and ns)
    piece_shapes, piece_of = _pieces(side, side_splits)
    hp = NH if fused else GLA_HEADS_PER_STEP
    hw = hp * DH
    units = [(hh, cidx) for hh in range(hp) for cidx in order]

    def bmap(i):
        return i if reverse else nb - 1 - i

    def body(*refs):
        zq_ref, zf_ref, zv_ref, lb_ref, ss_ref, dsf_ref = refs[:6]
        pos = 6
        do_ref = aq_ref = av_ref = None
        if has_do:
            do_ref = refs[pos]
            pos += 1
        if has_acc:
            aq_ref, av_ref = refs[pos], refs[pos + 1]
            pos += 2
        if fused:
            other_ref = refs[pos + 1]
            dz_ref, dlb_ref, ds0_ref, dst = refs[pos + 2:]
            dz_ref[:, D:2 * D] = other_ref[...]
        else:
            side_in = refs[pos:pos + ns]
            dzq_ref, dzf_ref, dzv_ref, dlb_ref, ds0_ref = refs[pos + ns:pos + ns + 5]
            side_out = refs[pos + ns + 5:pos + 2 * ns + 5]
            dst = refs[pos + 2 * ns + 5]
        i = pl.program_id(1)
        if ns:
            side_start, side_finish = _exchange(side_in, side_out, refs[pos + 2 * ns + 6:], piece_of)

            @pl.when((pl.program_id(0) == 0) & (i == 0))
            def _():
                side_start()

        @pl.when(i == 0)
        def _():
            dst[...] = dsf_ref[...]
            dlb_ref[...] = jnp.zeros_like(dlb_ref)

        mask = _tri(reverse)
        tri_b = jnp.where(mask, 1.0, 0.0).astype(BF16)
        tri_t = jnp.where(_tri(not reverse), 1.0, 0.0).astype(BF16)

        def where(u):
            return pl.ds(u[1] * CHUNK, CHUNK), pl.ds(u[0] * DH, DH)

        feat = {}
        for u in units:
            rows, cols = where(u)
            zq, zf = zq_ref[rows, cols], zf_ref[rows, cols]
            lbv = lb_ref[d:d + 1, cols]
            q, sq, f, sf = _gla_features(zq, zf, lbv)
            feat[u] = dict(zq=zq, q=q, sq=sq, f=f, sf=sf, lbv=lbv, k=1.0 - f, vb=zv_ref[rows, cols].astype(BF16))
        dec = {u: _cum_f32(tri_b, jnp.log(feat[u]["f"])) for u in units}
        for u in units:
            w = feat[u]
            g = dec[u]
            gl = g[last:last + 1, :]
            w["eg"], w["egi"], w["ege"], w["egl"] = jnp.exp(g), jnp.exp(-g), jnp.exp(gl - g), jnp.exp(gl)
            w["qd"], w["ki"], w["ke"] = w["q"] * w["eg"], w["k"] * w["egi"], w["k"] * w["ege"]
            w["qdb"], w["kib"], w["keb"] = w["qd"].astype(BF16), w["ki"].astype(BF16), w["ke"].astype(BF16)
            w["s_in"] = ss_ref[u[0], u[1]]
        if has_do:
            for u in units:
                w = feat[u]
                rows, cols = where(u)
                w["dob"] = do_ref[rows, cols].astype(BF16)
            for u in units:
                w = feat[u]
                w["a"] = jnp.where(mask, _dot_nt(w["qdb"], w["kib"]), 0.0).astype(BF16)
                w["da"] = jnp.where(mask, _dot_nt(w["dob"], w["vb"]), 0.0).astype(BF16)
                w["m"] = _dot_tn(w["dob"], w["qdb"])
        for hh in range(hp):
            ds = dst[hh]
            for cidx in order:
                w = feat[(hh, cidx)]
                w["ds"] = ds
                ds = ds * w["egl"]
                if has_do:
                    ds = ds + w["m"]
            dst[hh] = ds
            ds0_ref[hh] = ds
        for u in units:
            w = feat[u]
            dsb = w["ds"].astype(BF16)
            w["dke"] = _dot(w["vb"], dsb)
            w["dv"] = _dot_nt(w["keb"], dsb)
            if has_do:
                w["dv"] = w["dv"] + _dot_tn(w["a"], w["dob"])
                w["dqd"] = _dot(w["da"], w["kib"]) + _dot(w["dob"], w["s_in"].astype(BF16))
                w["dki"] = _dot_tn(w["da"], w["qdb"])
        for u in units:
            w = feat[u]
            dkeke = w["dke"] * w["ke"]
            w["dgl"] = (w["egl"] * jnp.sum(w["s_in"] * w["ds"], axis=0, keepdims=True)
                        + jnp.sum(dkeke, axis=0, keepdims=True))
            dg = -dkeke
            dk = w["dke"] * w["ege"]
            if has_do:
                dg = dg + w["dqd"] * w["qd"] - w["dki"] * w["ki"]
                dk = dk + w["dki"] * w["egi"]
            w["dg"], w["dk"] = dg, dk
        dlf = {u: _cum_f32(tri_t, feat[u]["dg"]) for u in units}
        for u in units:
            w = feat[u]
            rows, cols = where(u)
            df = (dlf[u] + w["dgl"]) / w["f"] - w["dk"]
            sf = w["sf"]
            dzf = df * (1.0 - w["lbv"]) * sf * (1.0 - sf)
            dlb_ref[0:1, cols] += jnp.sum(df * (1.0 - sf), axis=0, keepdims=True)
            if has_do:
                dzq = w["dqd"] * w["eg"] * (Q_SCALE * _dsilu(w["zq"], w["sq"]))
            else:
                dzq = jnp.zeros((CHUNK, DH), F32)
            dv = w["dv"]
            if has_acc:
                dzq = dzq + aq_ref[rows, cols]
                dv = dv + av_ref[rows, cols]
            if fused:
                lane = u[0] * DH
                dz_ref[rows, pl.ds(lane, DH)] = dzq.astype(BF16)
                dz_ref[rows, pl.ds(2 * D + lane, DH)] = dzf.astype(BF16)
                dz_ref[rows, pl.ds(3 * D + lane, DH)] = dv.astype(BF16)
            else:
                dzq_ref[rows, cols] = dzq
                dzf_ref[rows, cols] = dzf.astype(f_dtype)
                dzv_ref[rows, cols] = dv

        if ns:
            @pl.when((pl.program_id(0) == NH // hp - 1) & (i == nb - 1))
            def _():
                side_finish()

    def col(g):
        return lambda h, i: (bmap(i), g * (NH // hp) + h)

    tok = pl.BlockSpec((blk, hw), lambda h, i: (bmap(i), h))
    state = pl.BlockSpec((hp, DH, DH), lambda h, i: (h, 0, 0))
    in_specs = [pl.BlockSpec((blk, hw), col(0)), pl.BlockSpec((blk, hw), col(1 + d)), pl.BlockSpec((blk, hw), col(3)),
                pl.BlockSpec((8, hw), lambda h, i: (0, h)),
                pl.BlockSpec((hp, npb, DH, DH), lambda h, i: (h, bmap(i), 0, 0)), state]
    args = [z, z, z, lb, s_start, ds_fin]
    if has_do:
        in_specs.append(tok)
        args.append(do)
    if has_acc:
        in_specs += [tok, tok]
        args += [acc_q, acc_v]
    tail_shape = [jax.ShapeDtypeStruct((8, D), F32), jax.ShapeDtypeStruct((NH, DH, DH), F32)]
    tail_specs = [pl.BlockSpec((8, hw), lambda h, i: (0, h)), state]
    if fused:
        buf, other = into
        aliases = {len(args): 0}
        in_specs += [ANY, tok]
        args += [buf, other]
        out_shape = [jax.ShapeDtypeStruct(buf.shape, buf.dtype)] + tail_shape
        out_specs = [pl.BlockSpec((blk, 4 * D), lambda h, i: (bmap(i), 0))] + tail_specs
    else:
        aliases = {}
        in_specs += [ANY] * ns
        args += list(side)
        out_shape = [jax.ShapeDtypeStruct((n, D), F32), jax.ShapeDtypeStruct((n, D), f_dtype),
                     jax.ShapeDtypeStruct((n, D), F32)] + tail_shape
        out_shape += [jax.ShapeDtypeStruct((NDEV,) + piece_shapes[a], side[a].dtype) for a in range(ns)]
        out_specs = [tok, tok, tok] + tail_specs + [ANY] * ns
    return pl.pallas_call(
        body, name=name, grid=(NH // hp, nb), out_shape=out_shape, in_specs=in_specs, out_specs=out_specs,
        input_output_aliases=aliases,
        scratch_shapes=[pltpu.VMEM((hp, DH, DH), F32)] + (_exchange_sems(ns) if ns else []),
    )(*args)


def _shift(t, s, fill, down):
    n = t.shape[0]
    rows = lax.broadcasted_iota(jnp.int32, t.shape, 0)
    if down:
        return jnp.where(rows >= s, pltpu.roll(t, s, 0), fill)
    return jnp.where(rows < n - s, pltpu.roll(t, n - s, 0), fill)


SUBLANES = 8
LRU_SAVED = 4


def _chain_scan(a, b, h_in, down):
    n = a.shape[0]
    ng = n // SUBLANES
    rows = lax.broadcasted_iota(jnp.int32, (SUBLANES, a.shape[1]), 0)
    local = []
    for g in range(ng):
        aa, bb = a[g * SUBLANES:(g + 1) * SUBLANES], b[g * SUBLANES:(g + 1) * SUBLANES]
        for s in (1, 2, 4):
            if down:
                keep, amt = rows >= s, s
            else:
                keep, amt = rows < SUBLANES - s, SUBLANES - s
            bb = bb + aa * jnp.where(keep, pltpu.roll(bb, amt, 0), 0.0)
            aa = aa * jnp.where(keep, pltpu.roll(aa, amt, 0), 1.0)
        local.append((aa, bb))
    out = [None] * ng
    carry = h_in
    for g in (range(ng) if down else range(ng - 1, -1, -1)):
        aa, bb = local[g]
        hg = bb + aa * carry
        out[g] = hg
        carry = hg[SUBLANES - 1:SUBLANES] if down else hg[0:1]
    return (jnp.concatenate(out, axis=0) if ng > 1 else out[0]), carry


def _conv_taps(xv):
    return (_shift(xv, 1, 0.0, True), xv, _shift(xv, 1, 0.0, False), _shift(xv, 2, 0.0, False))


def _conv(taps, cw, cb):
    return cb + cw[0:1, :] * taps[0] + cw[1:2, :] * taps[1] + cw[2:3, :] * taps[2] + cw[3:4, :] * taps[3]


def _neg_expm1(t):
    series = -t * (1.0 + t * (0.5 + t * (1.0 / 6.0 + t * (1.0 / 24.0 + t * (1.0 / 120.0)))))
    return jnp.where(t > -0.1, series, 1.0 - jnp.exp(t))


def _lru_gates(xc, wr, br, wi, bi, lam):
    xcb = xc.astype(BF16)
    r = _sigmoid(_dot(xcb, wr) + br)
    gi = _sigmoid(_dot(xcb, wi) + bi)
    sp = jnp.maximum(-lam, 0.0) + jnp.log(1.0 + jnp.exp(-jnp.abs(lam)))
    la = -RG_C * r * sp
    a = jnp.exp(la)
    mult = jnp.sqrt(_neg_expm1(2.0 * la))
    return xcb, r, gi, sp, a, mult


def _lru_fwd(xin, blk, cw, cb, wr, br, wi, bi, lam, h0, acc_h, d, name):
    n = xin.shape[0]
    nb = n // blk
    reverse = d == 1
    down = not reverse
    has_acc = acc_h is not None

    def bmap(i):
        return nb - 1 - i if reverse else i

    def body(*refs):
        x_ref, cw_ref, cb_ref, wr_ref, br_ref, wi_ref, bi_ref, lam_ref, h0_ref = refs[:9]
        pos = 9
        acc_ref = refs[pos] if has_acc else None
        pos += int(has_acc)
        h_ref, hin_ref, hfin_ref, sav_a_ref, sav_ref = refs[pos:pos + 5]
        pos += 5
        hsum_ref = refs[pos] if has_acc else None
        carry = refs[-1]
        i = pl.program_id(0)

        @pl.when(i == 0)
        def _():
            carry[...] = h0_ref[...]

        for g in range(NH):
            cols = pl.ds(g * DH, DH)
            xc = _conv(_conv_taps(x_ref[:, cols]), cw_ref[:, cols], cb_ref[:, cols])
            _, r, gi, _, a, mult = _lru_gates(xc, wr_ref[g], br_ref[:, cols], wi_ref[g], bi_ref[:, cols],
                                              lam_ref[:, cols])
            sav_a_ref[:, cols] = a
            for slot, val in enumerate((xc, r, gi, mult)):
                sav_ref[slot, :, cols] = val.astype(BF16)
            hin = carry[:, cols]
            h, h_last = _chain_scan(a, mult * gi * xc, hin, down)
            h_ref[:, cols] = h
            if has_acc:
                hsum_ref[:, cols] = h + acc_ref[:, cols]
            hin_ref[0, :, cols] = hin
            carry[:, cols] = h_last
            hfin_ref[:, cols] = h_last

    vec = pl.BlockSpec((1, D), lambda i: (0, 0))
    wsp = pl.BlockSpec((NH, DH, DH), lambda i: (0, 0, 0))
    tok = pl.BlockSpec((blk, D), lambda i: (bmap(i), 0))
    in_specs = [tok, pl.BlockSpec((4, D), lambda i: (0, 0)), vec, wsp, vec, wsp, vec, vec, vec]
    args = [xin, cw, cb, wr, br, wi, bi, lam, h0]
    out_shape = [jax.ShapeDtypeStruct((n, D), F32), jax.ShapeDtypeStruct((nb, 1, D), F32),
                 jax.ShapeDtypeStruct((1, D), F32), jax.ShapeDtypeStruct((n, D), F32),
                 jax.ShapeDtypeStruct((LRU_SAVED, n, D), BF16)]
    out_specs = [tok, pl.BlockSpec((1, 1, D), lambda i: (bmap(i), 0, 0)), vec, tok,
                 pl.BlockSpec((LRU_SAVED, blk, D), lambda i: (0, bmap(i), 0))]
    if has_acc:
        in_specs.append(tok)
        args.append(acc_h)
        out_shape.append(jax.ShapeDtypeStruct((n, D), F32))
        out_specs.append(tok)
    return pl.pallas_call(
        body, name=name, grid=(nb,), out_shape=out_shape, in_specs=in_specs, out_specs=out_specs,
        scratch_shapes=[pltpu.VMEM((1, D), F32)],
    )(*args)


def _lru_bwd(xin, blk, cw, wr, wi, lam, sav, h, hin, dh, cg_fin, acc_dx, init, d, name, dx_dtype=F32):
    n = xin.shape[0]
    nb = n // blk
    reverse = d == 1
    down = not reverse
    first = blk - 1 if reverse else 0
    has_dh = dh is not None
    has_acc = acc_dx is not None
    has_init = init is not None

    def bmap(i):
        return i if reverse else nb - 1 - i

    def body(*refs):
        (x_ref, cw_ref, wr_ref, wi_ref, lam_ref, sav_a_ref, sav_ref, h_ref, hin_ref, cgf_ref) = refs[:10]
        pos = 10
        dh_ref = acc_ref = None
        iwr_ref = iwi_ref = ivec_ref = None
        if has_dh:
            dh_ref = refs[pos]
            pos += 1
        if has_acc:
            acc_ref = refs[pos]
            pos += 1
        if has_init:
            iwr_ref, iwi_ref, ivec_ref = refs[pos:pos + 3]
            pos += 3
        dx_ref, dwr_ref, dwi_ref, vec_ref, cg0_ref, carry = refs[pos:]
        i = pl.program_id(0)

        @pl.when(i == 0)
        def _():
            carry[...] = cgf_ref[...]
            if has_init:
                dwr_ref[...] = iwr_ref[...]
                dwi_ref[...] = iwi_ref[...]
                vec_ref[...] = ivec_ref[...]
            else:
                dwr_ref[...] = jnp.zeros_like(dwr_ref)
                dwi_ref[...] = jnp.zeros_like(dwi_ref)
                vec_ref[...] = jnp.zeros_like(vec_ref)

        for g in range(NH):
            cols = pl.ds(g * DH, DH)
            cwv = cw_ref[:, cols]
            lam_v = lam_ref[:, cols]
            taps = _conv_taps(x_ref[:, cols])
            wr_g, wi_g = wr_ref[g], wi_ref[g]
            a = sav_a_ref[:, cols]
            xcb = sav_ref[0, :, cols]
            xc, r, gi, mult = (sav_ref[slot, :, cols].astype(F32) for slot in range(LRU_SAVED))
            sp = jnp.maximum(-lam_v, 0.0) + jnp.log(1.0 + jnp.exp(-jnp.abs(lam_v)))
            hprev = _shift(h_ref[:, cols], 1, hin_ref[0, :, cols], down)
            a_next = _shift(a, 1, 1.0, not down)
            dhv = dh_ref[:, cols] if has_dh else jnp.zeros_like(a)
            e, _ = _chain_scan(a_next, dhv, carry[:, cols], not down)
            cg = a[first:first + 1, :] * e[first:first + 1, :]
            carry[:, cols] = cg
            cg0_ref[:, cols] = cg
            da = e * hprev
            emult = e * mult
            dgi = emult * xc
            dxc = emult * gi
            dla = da * a - (e * gi * xc) * (a * a) / mult
            dr = dla * (-RG_C * sp)
            sneg = 1.0 - _sigmoid(lam_v)
            dpr = dr * r * (1.0 - r)
            dpi = dgi * gi * (1.0 - gi)
            dprb, dpib = dpr.astype(BF16), dpi.astype(BF16)
            dxc = dxc + _dot_nt(dprb, wr_g) + _dot_nt(dpib, wi_g)
            dwr_ref[g] += _dot_tn(xcb, dprb)
            dwi_ref[g] += _dot_tn(xcb, dpib)
            dx = (cwv[0:1, :] * _shift(dxc, 1, 0.0, False) + cwv[1:2, :] * dxc
                  + cwv[2:3, :] * _shift(dxc, 1, 0.0, True) + cwv[3:4, :] * _shift(dxc, 2, 0.0, True))
            if has_acc:
                dx = dx + acc_ref[:, cols]
            dx_ref[:, cols] = dx.astype(dx_dtype)
            vec_ref[0:1, cols] += jnp.sum(dpr, axis=0, keepdims=True)
            vec_ref[1:2, cols] += jnp.sum(dpi, axis=0, keepdims=True)
            vec_ref[2:3, cols] += jnp.sum(dla * r, axis=0, keepdims=True) * (RG_C * sneg)
            vec_ref[3:4, cols] += jnp.sum(dxc, axis=0, keepdims=True)
            for kk in range(4):
                vec_ref[4 + kk:5 + kk, cols] += jnp.sum(dxc * taps[kk], axis=0, keepdims=True)

    vec = pl.BlockSpec((1, D), lambda i: (0, 0))
    wsp = pl.BlockSpec((NH, DH, DH), lambda i: (0, 0, 0))
    tok = pl.BlockSpec((blk, D), lambda i: (bmap(i), 0))
    vec16 = pl.BlockSpec((16, D), lambda i: (0, 0))
    in_specs = [tok, pl.BlockSpec((4, D), lambda i: (0, 0)), wsp, wsp, vec, tok,
                pl.BlockSpec((LRU_SAVED, blk, D), lambda i: (0, bmap(i), 0)), tok,
                pl.BlockSpec((1, 1, D), lambda i: (bmap(i), 0, 0)), vec]
    args = [xin, cw, wr, wi, lam, sav[0], sav[1], h, hin, cg_fin]
    if has_dh:
        in_specs.append(tok)
        args.append(dh)
    if has_acc:
        in_specs.append(tok)
        args.append(acc_dx)
    if has_init:
        in_specs += [wsp, wsp, vec16]
        args += list(init)
    return pl.pallas_call(
        body, name=name, grid=(nb,),
        out_shape=[jax.ShapeDtypeStruct((n, D), dx_dtype), jax.ShapeDtypeStruct((NH, DH, DH), F32),
                   jax.ShapeDtypeStruct((NH, DH, DH), F32), jax.ShapeDtypeStruct((16, D), F32),
                   jax.ShapeDtypeStruct((1, D), F32)],
        in_specs=in_specs, out_specs=[tok, wsp, wsp, vec16, vec],
        scratch_shapes=[pltpu.VMEM((1, D), F32)],
    )(*args)


def _merge(z, o_f, o_b, hx, xin, tgt, mod, gn, p_a, p_b, w_out, ln_g, ln_b):
    n = xin.shape[0]
    tm = _row_tile(n, 128)

    def body(z4_ref, z6_ref, z7_ref, z8_ref, of_ref, ob_ref, hx_ref, x_ref, t_ref, mod_ref, gn_ref,
             pa_ref, pb_ref, wo_ref, lg_ref, lnb_ref,
             dr_ref, do_ref, dhx_ref, dz_ref, oa_o, obb_o, y_o, dya_o, dyb_o, dout_o, vec_ref):
        @pl.when(pl.program_id(0) == 0)
        def _():
            vec_ref[...] = jnp.zeros_like(vec_ref)

        gt = mod_ref[0:1, 2 * D:3 * D]
        gnv = gn_ref[...]
        o = of_ref[...] + ob_ref[...]
        rs = jnp.concatenate(
            [jnp.broadcast_to(lax.rsqrt(jnp.mean(jnp.square(o[:, h * DH:(h + 1) * DH]), axis=1, keepdims=True)
                                        + RMS_EPS), (tm, DH)) for h in range(NH)], axis=1)
        nrm = o * rs
        rn = nrm * gnv
        z4, z6, z7, z8 = z4_ref[...], z6_ref[...], z7_ref[...], z8_ref[...]
        s4, s6, s7, s8 = _sigmoid(z4), _sigmoid(z6), _sigmoid(z7), _sigmoid(z8)
        sg4, sg6 = z4 * s4, z6 * s6
        hxv = hx_ref[...]
        oa = (rn * sg4).astype(BF16)
        obb = (hxv * sg6).astype(BF16)
        ya = _dot(oa, pa_ref[...])
        yb = _dot(obb, pb_ref[...])
        y = (s7 * ya + s8 * yb).astype(BF16)
        out = _dot(y, wo_ref[...])
        xv = x_ref[...]
        rr = ALPHA * xv + gt * out
        mu = jnp.mean(rr, axis=1, keepdims=True)
        cen = rr - mu
        rstd = lax.rsqrt(jnp.mean(cen * cen, axis=1, keepdims=True) + LN_EPS)
        xhat = cen * rstd
        lg = lg_ref[...]
        err = xhat * lg + lnb_ref[...] - t_ref[...]
        loss_rows = jnp.sum(err * err, axis=1, keepdims=True)
        dxn = err * (1.0 / D)
        dxh = dxn * lg
        dr = rstd * (dxh - jnp.mean(dxh, axis=1, keepdims=True)
                     - xhat * jnp.mean(dxh * xhat, axis=1, keepdims=True))
        dout = (dr * gt).astype(BF16)
        dy = _dot_nt(dout, wo_ref[...])
        dya = (dy * s7).astype(BF16)
        dyb = (dy * s8).astype(BF16)
        doa = _dot_nt(dya, pa_ref[...])
        dob = _dot_nt(dyb, pb_ref[...])
        drn = doa * sg4
        dn = drn * gnv
        dnn = dn * nrm
        corr = jnp.concatenate(
            [jnp.broadcast_to(jnp.mean(dnn[:, h * DH:(h + 1) * DH], axis=1, keepdims=True), (tm, DH))
             for h in range(NH)], axis=1)
        dr_ref[...] = dr
        do_ref[...] = rs * (dn - nrm * corr)
        dhx_ref[...] = dob * sg6
        dz_ref[:, 0:4 * D] = jnp.zeros((tm, 4 * D), BF16)
        dz_ref[:, 4 * D:5 * D] = (doa * rn * _dsilu(z4, s4)).astype(BF16)
        dz_ref[:, 5 * D:6 * D] = jnp.zeros((tm, D), BF16)
        dz_ref[:, 6 * D:7 * D] = (dob * hxv * _dsilu(z6, s6)).astype(BF16)
        dz_ref[:, 7 * D:8 * D] = (dy * ya * s7 * (1.0 - s7)).astype(BF16)
        dz_ref[:, 8 * D:9 * D] = (dy * yb * s8 * (1.0 - s8)).astype(BF16)
        oa_o[...] = oa
        obb_o[...] = obb
        y_o[...] = y
        dya_o[...] = dya
        dyb_o[...] = dyb
        dout_o[...] = dout
        vec_ref[0:1, :] += jnp.sum(dr * out, axis=0, keepdims=True)
        vec_ref[1:2, :] += jnp.sum(dxn * xhat, axis=0, keepdims=True)
        vec_ref[2:3, :] += jnp.sum(dxn, axis=0, keepdims=True)
        vec_ref[3:4, :] += jnp.sum(drn * nrm, axis=0, keepdims=True)
        vec_ref[4:5, :] += jnp.broadcast_to(jnp.sum(loss_rows, axis=0, keepdims=True) * (0.5 / D), (1, D))

    def grp(g):
        return pl.BlockSpec((tm, D), lambda i: (i, g))

    tok = pl.BlockSpec((tm, D), lambda i: (i, 0))
    vec = pl.BlockSpec((1, D), lambda i: (0, 0))
    wsp = pl.BlockSpec((D, D), lambda i: (0, 0))
    return pl.pallas_call(
        body, name="merge", grid=(n // tm,),
        out_shape=[jax.ShapeDtypeStruct((n, D), F32)] * 3
        + [jax.ShapeDtypeStruct((n, NGRP * D), BF16)]
        + [jax.ShapeDtypeStruct((n, D), BF16)] * 6 + [jax.ShapeDtypeStruct((8, D), F32)],
        in_specs=[grp(4), grp(6), grp(7), grp(8), tok, tok, tok, tok, tok,
                  pl.BlockSpec((8, 3 * D), lambda i: (0, 0)), vec, wsp, wsp, wsp, vec, vec],
        out_specs=[tok, tok, tok, pl.BlockSpec((tm, NGRP * D), lambda i: (i, 0))] + [tok] * 6
        + [pl.BlockSpec((8, D), lambda i: (0, 0))],
    )(z, z, z, z, o_f, o_b, hx, xin, tgt, mod, gn, p_a, p_b, w_out, ln_g, ln_b)


def _wmod_grad(c_t, cctx_t, dmx_loc, dmc_loc, name):
    n = dmx_loc.shape[1]

    def body(ct_ref, cc_ref, dmx_ref, dmc_ref, o_ref):
        ct = ct_ref[...]
        sct = ct * _sigmoid(ct)
        cc = cc_ref[...]
        scc = cc * _sigmoid(cc)
        dmc = dmc_ref[0:1, :]
        for b in range(1, NDEV):
            dmc = dmc + dmc_ref[b:b + 1, :]
        acc = scc * dmc
        for b in range(NDEV):
            acc = acc + sct[:, b:b + 1] * dmx_ref[b:b + 1, :]
        o_ref[...] = acc

    return pl.pallas_call(body, name=name, out_shape=jax.ShapeDtypeStruct((D, n), F32))(c_t, cctx_t, dmx_loc, dmc_loc)


PACK_ROWS = 40


def _finalize_small(g_pack, lb, w_mod_full, params):
    npar = len(params)

    def body(*refs):
        gp_ref, lb_ref, wm_ref = refs[:3]
        wmv = refs[3:3 + 3 * npar]
        loss_ref = refs[3 + 3 * npar]
        g_refs = refs[4 + 3 * npar:4 + 4 * npar]
        upd = refs[4 + 4 * npar:4 + 7 * npar]
        tot = refs[-1]
        acc = gp_ref[0]
        for k in range(1, NDEV):
            acc = acc + gp_ref[k]
        tot[...] = acc
        mine = pl.ds(pl.multiple_of(_my_index() * DH, DH), DH)
        (g_cctx, g_bmod, g_bin, g_lbl, g_norm, g_cw, g_cb, g_br, g_bi, g_lam, g_lng, g_lnb) = g_refs

        loss_ref[...] = jnp.broadcast_to(tot[36:37, 0:DH], (8, DH))
        for k in range(3):
            g_bmod[:, k * D:(k + 1) * D] = tot[k:k + 1, :] + tot[3 + k:4 + k, :]
        dmc = jnp.concatenate([tot[3:4, :], tot[4:5, :], tot[5:6, :]], axis=1)
        cv = wmv[0][...]
        proj = _dot_nt(jnp.broadcast_to(dmc, (8, 3 * D)).astype(BF16), wm_ref[...])
        g_cctx[...] = proj[0:1, :] * _dsilu(cv, _sigmoid(cv))
        for k in range(NGRP):
            g_bin[:, k * D:(k + 1) * D] = tot[6 + k:7 + k, :]
        nrm = tot[15:16, 0:DH]
        for h in range(1, NH):
            nrm = nrm + tot[15:16, h * DH:(h + 1) * DH]
        g_norm[...] = nrm
        g_lng[...] = tot[16:17, :]
        g_lnb[...] = tot[17:18, :]
        g_cb[...] = tot[21:22, :] + tot[29:30, :]
        g_cw[0] = tot[22:26, mine] + tot[30:34, mine]
        for ref, row in ((g_br, 18), (g_bi, 19), (g_lam, 20)):
            ref[0, 0:1, :] = tot[row:row + 1, mine]
            ref[0, 1:2, :] = tot[row + 8:row + 9, mine]
        lbl = lb_ref[0:2, mine]
        dl0 = tot[34:36, mine] * lbl * (1.0 - lbl)
        g_lbl[0] = dl0
        g_lbl[1] = -dl0
        for p in range(npar):
            d, mm, vv = _adam_math(g_refs[p][...], wmv[3 * p][...], wmv[3 * p + 1][...], wmv[3 * p + 2][...])
            upd[3 * p][...] = d
            upd[3 * p + 1][...] = mm
            upd[3 * p + 2][...] = vv

    flat = [t for wmv in params for t in wmv]
    shapes = [jax.ShapeDtypeStruct(wmv[0].shape, F32) for wmv in params]
    outs = pl.pallas_call(
        body, name="finalize_small",
        out_shape=[jax.ShapeDtypeStruct((8, DH), F32)] + shapes + [s for s in shapes for _ in range(3)],
        scratch_shapes=[pltpu.VMEM((PACK_ROWS, D), F32)],
    )(g_pack, lb, w_mod_full, *flat)
    grads = list(outs[1:1 + npar])
    upd = [tuple(outs[1 + npar + 3 * p:4 + npar + 3 * p]) for p in range(npar)]
    return outs[0], grads, upd


def _to_colmajor(t, rows):
    return t.reshape(rows, GRID_W, D).transpose(1, 0, 2).reshape(rows * GRID_W, D)


def _to_raster(t, rows):
    return t.reshape(GRID_W, rows, D).transpose(1, 0, 2).reshape(rows * GRID_W, D)


def _local_cols(t, me, width):
    return lax.dynamic_slice_in_dim(t, me * width, width, axis=t.ndim - 1)


def kernel(x, c, ctx, c_ctx, w_mod, b_mod, w_in, b_in, lb_logits, norm_a_g, conv_w, conv_b, w_r, b_r, w_i, b_i, lam, p_a, p_b, w_out, ln_g, ln_b, loss_target, m_c_ctx, m_w_mod, m_b_mod, m_w_in, m_b_in, m_lb_logits, m_norm_a_g, m_conv_w, m_conv_b, m_w_r, m_b_r, m_w_i, m_b_i, m_lam, m_p_a, m_p_b, m_w_out, m_ln_g, m_ln_b, v_c_ctx, v_w_mod, v_b_mod, v_w_in, v_b_in, v_lb_logits, v_norm_a_g, v_conv_w, v_conv_b, v_w_r, v_b_r, v_w_i, v_b_i, v_lam, v_p_a, v_p_b, v_w_out, v_ln_g, v_ln_b):
    me = _my_index()
    xs, cs, tgt = x[0], ctx[0], loss_target[0]
    t_len, c_len = xs.shape[0], cs.shape[0]
    rows = t_len // GRID_W
    wcols = w_in.shape[2]
    mcols = w_mod.shape[2]

    w_mod_b, w_in_b, p_a_b, p_b_b, w_out_b, w_r_b, w_i_b = _cast_bf16(
        [w_mod[0], w_in[0], p_a[0], p_b[0], w_out[0], w_r[0], w_i[0]])
    small = jnp.concatenate([lb_logits.reshape(4, DH), conv_w[0], b_r[0], b_i[0], lam[0], jnp.zeros((2, DH), F32),
                             c.reshape(8, DH)], axis=0)
    g_small, g_wmod = _all_gather_two_level([small, w_mod_b], "gather_params")

    def full_rows(lo, hi):
        return g_small[:, lo:hi, :].transpose(1, 0, 2).reshape(hi - lo, D)

    lbl_f, cw_f, br_f, bi_f, lam_f = full_rows(0, 4), full_rows(4, 8), full_rows(8, 10), full_rows(10, 12), full_rows(12, 14)
    c_all = g_small[:, 16:24, :].reshape(NDEV, D)
    w_mod_f = g_wmod.transpose(1, 0, 2).reshape(D, 3 * D)

    cc = jnp.concatenate([c.reshape(1, D), c_ctx.reshape(1, D), jnp.zeros((6, D), F32)], axis=0)
    lbl_p = jnp.concatenate([lbl_f.reshape(2, 2, D), jnp.zeros((2, 6, D), F32)], axis=1)
    mod, lb = _prep(cc, w_mod_f, b_mod, lbl_p)
    u_x = _modulate(xs, mod, 0, "modulate_x")
    u_c = _modulate(cs, mod, 1, "modulate_c")
    z_x, w_in_f = _inproj_gather(u_x, w_in_b, b_in, "inproj_gather")
    z_c = _mm_bias(u_c, w_in_f, b_in, "inproj_c")

    zero_s = jnp.zeros((NH, DH, DH), F32)
    zero_v = jnp.zeros((1, D), F32)
    gla = {}
    out_w = [p_a_b, p_b_b, w_out_b]
    out_w_f = []
    for d in (0, 1):
        _, ssc, sfc = _gla_fwd(z_c, lb, zero_s, d, f"gla_fwd_c{d}")
        o_d, ssx, _, *gathered = _gla_fwd(z_x, lb, sfc, d, f"gla_fwd_x{d}", side=out_w[1:] if d else out_w[:1])
        out_w_f += [t.reshape(D, D) for t in gathered]
        gla[d] = (ssc, ssx, o_d)
    p_a_f, p_b_f, w_out_f = out_w_f

    x5_c = z_c[:, 5 * D:6 * D]
    x5_x = _to_colmajor(z_x[:, 5 * D:6 * D], rows)
    cb2 = conv_b.reshape(1, D)
    lru = {}
    h_sum = None
    for d in (0, 1):
        prm = (cw_f, cb2, w_r_b[d], br_f[d:d + 1], w_i_b[d], bi_f[d:d + 1], lam_f[d:d + 1])
        h_c, hin_c, hfin_c, *sav_c = _lru_fwd(x5_c, c_len, *prm, zero_v, None, d, f"lru_fwd_c{d}")
        h_x, hin_x, _, sav_a, sav_h, *h_sum = _lru_fwd(x5_x, rows, *prm, hfin_c, lru[0][3] if d else None, d,
                                                       f"lru_fwd_x{d}")
        lru[d] = ((cw_f, w_r_b[d], w_i_b[d], lam_f[d:d + 1]), h_c, hin_c, h_x, hin_x, tuple(sav_c), (sav_a, sav_h))
    hx = _to_raster(h_sum[0], rows)

    gn = jnp.tile(norm_a_g.reshape(1, DH), (1, NH))
    (dr, do, dhx, dz_m, oa, obb, yb16, dya, dyb, dout, mvec) = _merge(
        z_x, gla[0][2], gla[1][2], hx, xs, tgt, mod, gn, p_a_f, p_b_f, w_out_f, ln_g, ln_b)

    dhx_cm = _to_colmajor(dhx, rows)
    lru_dx_x = lru_dx_c = None
    for d in (0, 1):
        prm, h_c, hin_c, h_x, hin_x, sav_c, sav_x = lru[d]
        dx_dtype = BF16 if d else F32
        lru_dx_x, dwr, dwi, lvec, cg0 = _lru_bwd(x5_x, rows, *prm, sav_x, h_x, hin_x, dhx_cm, zero_v, lru_dx_x, None,
                                                 d, f"lru_bwd_x{d}", dx_dtype)
        lru_dx_c, dwr, dwi, lvec, _ = _lru_bwd(x5_c, c_len, *prm, sav_c, h_c, hin_c, None, cg0, lru_dx_c,
                                               (dwr, dwi, lvec), d, f"lru_bwd_c{d}", dx_dtype)
        lru[d] = (dwr, dwi, lvec)
    dz5_x = _to_raster(lru_dx_x, rows)
    dz5_c = lru_dx_c

    dpa = _mm_tn(oa, dya, None, "dpa", out_dtype=BF16)
    dpb = _mm_tn(obb, dyb, None, "dpb", out_dtype=BF16)
    dwo = _mm_tn(yb16, dout, None, "dwout", out_dtype=BF16)
    wr_pack = jnp.concatenate([lru[0][0], lru[1][0], lru[0][1], lru[1][1]], axis=0).reshape(4 * NH * DH, DH)

    gq_c = gv_c = None
    dzf_c, dlb = {}, {}
    gq_x, dzf_x0, gv_x, dlb_x, ds0, r_pa, r_pb, r_wo, r_wri = _gla_bwd(
        z_x, lb, gla[0][1], do, zero_s, None, None, 0, "gla_bwd_x0", f_dtype=BF16,
        side=[dpa, dpb, dwo, wr_pack], side_splits=[0, 0, 0, 0])
    gq_c, dzf_c[0], gv_c, dlb_c, _ = _gla_bwd(z_c, lb, gla[0][0], None, ds0, None, None, 0, "gla_bwd_c0")
    dlb[0] = dlb_x[0:1] + dlb_c[0:1]
    dz_g, dlb_x, ds0 = _gla_bwd(z_x, lb, gla[1][1], do, zero_s, gq_x, gv_x, 1, "gla_bwd_x1", into=(dz_m, dzf_x0))
    gq_c, dzf_c[1], gv_c, dlb_c, _ = _gla_bwd(z_c, lb, gla[1][0], None, ds0, gq_c, gv_c, 1, "gla_bwd_c1")
    dlb[1] = dlb_x[0:1] + dlb_c[0:1]

    bf = lambda t: t.astype(BF16)
    dz_x = lax.dynamic_update_slice(dz_g, dz5_x, (0, 5 * D))
    zc0 = jnp.zeros((c_len, D), BF16)
    dz_c = jnp.concatenate([bf(gq_c), bf(dzf_c[0]), bf(dzf_c[1]), bf(gv_c), zc0, dz5_c, zc0, zc0, zc0], axis=1)
    dwin_c, dbin_c = _mm_tn(u_c, dz_c, None, "dwin_c", with_colsum=True)

    grad_x, xvec = _input_grad(dz_x, w_in_f, xs, dr, mod, 0, "input_grad_x")
    r_win, dbin = _dwin_exchange(u_x, dz_x, dwin_c, dbin_c, "dwin_exchange")
    _, cvec = _input_grad(dz_c, w_in_f, cs, None, mod, 1, "input_grad_c")
    wri_piece = _sum_rows(r_wri, "sum_w_ri_piece")
    g_w_in, d_w_in, nm_w_in, nv_w_in = _sum_adamw(r_win, w_in, m_w_in, v_w_in, "update_w_in")
    g_p_a, d_p_a, nm_p_a, nv_p_a = _sum_adamw(r_pa, p_a, m_p_a, v_p_a, "update_p_a")
    g_p_b, d_p_b, nm_p_b, nv_p_b = _sum_adamw(r_pb, p_b, m_p_b, v_p_b, "update_p_b")
    g_w_out, d_w_out, nm_w_out, nv_w_out = _sum_adamw(r_wo, w_out, m_w_out, v_w_out, "update_w_out")

    dlb_rows = jnp.concatenate([dlb[0], dlb[1]], axis=0)
    pack = jnp.concatenate([
        xvec[0:1], xvec[1:2], mvec[0:1],
        cvec[0:1], cvec[1:2], jnp.zeros((1, D), F32),
        dbin.reshape(NGRP, D),
        mvec[3:4], mvec[1:2], mvec[2:3],
        lru[0][2][0:8], lru[1][2][0:3],
        lru[1][2][3:8],
        dlb_rows,
        mvec[4:5],
        jnp.zeros((3, D), F32)], axis=0)
    g_pack, g_wri = _all_gather([pack, wri_piece], "gather_small_grads")

    dmx = g_pack[:, 0:3, :].reshape(NDEV, 3 * D)
    dmc = g_pack[:, 3:6, :].reshape(NDEV, 3 * D)
    grad_w_mod = _wmod_grad(c_all.T, c_ctx.reshape(D, 1), _local_cols(dmx, me, mcols), _local_cols(dmc, me, mcols),
                            "grad_w_mod").reshape(1, D, mcols)
    small_params = [(c_ctx.reshape(1, D), m_c_ctx.reshape(1, D), v_c_ctx.reshape(1, D)), (b_mod, m_b_mod, v_b_mod),
                    (b_in, m_b_in, v_b_in), (lb_logits, m_lb_logits, v_lb_logits), (norm_a_g, m_norm_a_g, v_norm_a_g),
                    (conv_w, m_conv_w, v_conv_w), (conv_b, m_conv_b, v_conv_b), (b_r, m_b_r, v_b_r),
                    (b_i, m_b_i, v_b_i), (lam, m_lam, v_lam), (ln_g, m_ln_g, v_ln_g), (ln_b, m_ln_b, v_ln_b)]
    loss_tile, small_g, small_upd = _finalize_small(g_pack, lb, w_mod_f, small_params)
    loss = loss_tile[0, 0]
    (grad_c_ctx, grad_b_mod, grad_b_in, grad_lb_logits, grad_norm_a_g, grad_conv_w, grad_conv_b, grad_b_r, grad_b_i,
     grad_lam, grad_ln_g, grad_ln_b) = small_g
    small_upd[0] = tuple(t.reshape(c_ctx.shape) for t in small_upd[0])
    (o_c_ctx, o_b_mod, o_b_in, o_lb, o_norm, o_conv_w, o_conv_b, o_b_r, o_b_i, o_lam, o_ln_g, o_ln_b) = small_upd

    half = 2 * NH * DH
    g_ri = g_wri.reshape(2 * half, DH)
    grad_w_r, grad_w_i = g_ri[:half].reshape(w_r.shape), g_ri[half:].reshape(w_i.shape)
    d_w_r, nm_w_r, nv_w_r = _adamw(grad_w_r, w_r, m_w_r, v_w_r, "update_w_r")
    d_w_i, nm_w_i, nv_w_i = _adamw(grad_w_i, w_i, m_w_i, v_w_i, "update_w_i")

    d_w_mod, nm_w_mod, nv_w_mod = _adamw(grad_w_mod, w_mod, m_w_mod, v_w_mod, "update_w_mod")

    grads = [grad_c_ctx.reshape(c_ctx.shape), grad_w_mod, grad_b_mod, g_w_in, grad_b_in, grad_lb_logits, grad_norm_a_g,
             grad_conv_w, grad_conv_b, grad_w_r, grad_b_r, grad_w_i, grad_b_i, grad_lam, g_p_a, g_p_b, g_w_out,
             grad_ln_g, grad_ln_b]
    per_kind = []
    for k in range(3):
        per_kind.append([
            o_c_ctx[k], (d_w_mod, nm_w_mod, nv_w_mod)[k], o_b_mod[k], (d_w_in, nm_w_in, nv_w_in)[k], o_b_in[k], o_lb[k],
            o_norm[k], o_conv_w[k], o_conv_b[k], (d_w_r, nm_w_r, nv_w_r)[k], o_b_r[k], (d_w_i, nm_w_i, nv_w_i)[k],
            o_b_i[k], o_lam[k], (d_p_a, nm_p_a, nv_p_a)[k], (d_p_b, nm_p_b, nv_p_b)[k], (d_w_out, nm_w_out, nv_w_out)[k],
            o_ln_g[k], o_ln_b[k]])
    return (loss, grad_x.reshape(x.shape), *grads, *per_kind[0], *per_kind[1], *per_kind[2])
```

```python
import functools

import jax
import jax.numpy as jnp
from jax import lax
from jax.experimental import pallas as pl
from jax.experimental.pallas import tpu as pltpu

F32 = jnp.float32
BF16 = jnp.bfloat16

D = 1024
NH = 8
DH = 128
CHUNK = 64
GLA_HEADS_PER_STEP = 8
GRID_W = 64
NGRP = 9
NDEV = 8
RG_C = 8.0
ALPHA = 2.0 ** 0.25
LN_EPS = 1e-5
RMS_EPS = 1e-6
Q_SCALE = DH ** -0.5
ADAM_LR, ADAM_B1, ADAM_B2, ADAM_EPS, ADAM_WD, ADAM_STEP = 1e-3, 0.9, 0.999, 1e-8, 0.01, 10
ADAM_C1 = 1.0 / (1.0 - ADAM_B1 ** ADAM_STEP)
ADAM_C2 = 1.0 / (1.0 - ADAM_B2 ** ADAM_STEP)

ANY = pl.BlockSpec(memory_space=pl.ANY)


def _sigmoid(t):
    return 1.0 / (1.0 + jnp.exp(-t))


def _dsilu(t, s):
    return s * (1.0 + t * (1.0 - s))


def _dot(a, b):
    return jnp.dot(a, b, preferred_element_type=F32)


def _dot_nt(a, b):
    return lax.dot_general(a, b, (((1,), (1,)), ((), ())), preferred_element_type=F32)


def _dot_tn(a, b):
    return lax.dot_general(a, b, (((0,), (0,)), ((), ())), preferred_element_type=F32)


def _my_index():
    return 4 * lax.axis_index("x") + 2 * lax.axis_index("y") + lax.axis_index("c")


def _dev_tuple(j):
    return (j >> 2, (j >> 1) & 1, j & 1)


def _exchange_sems(n):
    return [pltpu.SemaphoreType.DMA((n * NDEV,)), pltpu.SemaphoreType.DMA((n * NDEV,)), pltpu.SemaphoreType.DMA((n,))]


def _exchange(ins, outs, sems, piece_of=None):
    send_sems, recv_sems, loc_sems = sems
    n = len(ins)

    def src(a, p):
        return ins[a] if piece_of is None else piece_of(ins[a], a, p)

    def push(a, t):
        me, p = _my_index(), _step_peer(t)
        return pltpu.make_async_remote_copy(
            src_ref=src(a, p), dst_ref=outs[a].at[me],
            send_sem=send_sems.at[a * NDEV + t], recv_sem=recv_sems.at[a * NDEV + me],
            device_id=_dev_of(p), device_id_type=pl.DeviceIdType.MESH)

    def local(a):
        me = _my_index()
        return pltpu.make_async_copy(src(a, me), outs[a].at[me], loc_sems.at[a])

    def start():
        for a in range(n):
            local(a).start()
        for t in range(NDEV - 1):
            for a in range(n):
                push(a, t).start()

    def finish():
        me = _my_index()
        for t in range(NDEV - 1):
            for a in range(n):
                push(a, t).wait_send()
        for j in range(NDEV):
            @pl.when(me != j)
            def _():
                for a in range(n):
                    pltpu.make_async_remote_copy(
                        src_ref=src(a, j), dst_ref=outs[a].at[j],
                        send_sem=send_sems.at[a * NDEV], recv_sem=recv_sems.at[a * NDEV + j],
                        device_id=_dev_tuple(j), device_id_type=pl.DeviceIdType.MESH).wait_recv()
        for a in range(n):
            local(a).wait()

    return start, finish


def _all_gather(shards, name):
    n = len(shards)

    def body(*refs):
        start, finish = _exchange(refs[:n], refs[n:2 * n], refs[2 * n:])
        start()
        finish()

    return pl.pallas_call(
        body, name=name,
        out_shape=[jax.ShapeDtypeStruct((NDEV,) + s.shape, s.dtype) for s in shards],
        in_specs=[ANY] * n, out_specs=[ANY] * n, scratch_shapes=_exchange_sems(n),
    )(*shards)


def _pieces(parts, splits):
    shapes = []
    for part, split in zip(parts, splits):
        r, c = part.shape
        shapes.append((r // NDEV, c) if split == 0 else (r, c // NDEV))

    def piece_of(ref, a, j):
        pr, pc = shapes[a]
        if splits[a] == 0:
            start = j * pr if isinstance(j, int) else pl.multiple_of(j * pr, pr)
            return ref.at[pl.ds(start, pr), :]
        start = j * pc if isinstance(j, int) else pl.multiple_of(j * pc, pc)
        return ref.at[:, pl.ds(start, pc)]

    return shapes, piece_of


_STEP_MASKS = ((2, 4, 6, 3, 5, 7, 1, 0), (4, 2, 6, 5, 3, 7, 1, 0))
_GATHER_MASKS = ((0, 1, 2, 4, 3, 5, 6, 7), (0, 1, 4, 2, 5, 3, 6, 7))


def _peer_schedule(table):
    tab = jnp.array(table, jnp.int32)
    return jnp.bitwise_xor(_my_index(), tab[lax.axis_index("c")])


def _step_peer(s, table=_STEP_MASKS):
    def pick(row):
        if isinstance(s, int):
            return jnp.int32(row[s])
        m = jnp.int32(row[NDEV - 1])
        for t in range(NDEV - 2, -1, -1):
            m = jnp.where(s == t, jnp.int32(row[t]), m)
        return m
    mask = jnp.where(lax.axis_index("c") == 0, pick(table[0]), pick(table[1]))
    return jnp.bitwise_xor(_my_index(), mask)


def _dev_of(p):
    return (p // 4, (p // 2) % 2, p % 2)


def _all_gather_two_level(shards, name):
    n = len(shards)
    direct = (1, 2, 3, 6)
    relay_sem = {2: 4, 3: 5, 6: 7}

    def body(*refs):
        ins, outs = refs[:n], refs[n:2 * n]
        send_sems, recv_sems, loc_sems = refs[2 * n:]
        me = _my_index()
        sibling = _step_peer(1, _GATHER_MASKS)

        def push(a, t):
            return pltpu.make_async_remote_copy(
                src_ref=ins[a], dst_ref=outs[a].at[me], send_sem=send_sems.at[a * NDEV + t],
                recv_sem=recv_sems.at[a * NDEV + me],
                device_id=_dev_of(_step_peer(t, _GATHER_MASKS)), device_id_type=pl.DeviceIdType.MESH)

        def arrival(a, p):
            return pltpu.make_async_remote_copy(
                src_ref=ins[a], dst_ref=outs[a].at[p], send_sem=send_sems.at[a * NDEV],
                recv_sem=recv_sems.at[a * NDEV + p], device_id=_dev_of(p), device_id_type=pl.DeviceIdType.MESH)

        def relay(a, t):
            p = _step_peer(t, _GATHER_MASKS)
            return pltpu.make_async_remote_copy(
                src_ref=outs[a].at[p], dst_ref=outs[a].at[p], send_sem=send_sems.at[a * NDEV + relay_sem[t]],
                recv_sem=recv_sems.at[a * NDEV + p], device_id=_dev_of(sibling), device_id_type=pl.DeviceIdType.MESH)

        for a in range(n):
            pltpu.make_async_copy(ins[a], outs[a].at[me], loc_sems.at[a]).start()
        for t in direct:
            for a in range(n):
                push(a, t).start()
        for t in relay_sem:
            for a in range(n):
                arrival(a, _step_peer(t, _GATHER_MASKS)).wait_recv()
                relay(a, t).start()
        for t in (1, 4, 5, 7):
            for a in range(n):
                arrival(a, _step_peer(t, _GATHER_MASKS)).wait_recv()
        for a in range(n):
            for t in direct:
                push(a, t).wait_send()
            for t in relay_sem:
                relay(a, t).wait_send()
            pltpu.make_async_copy(ins[a], outs[a].at[me], loc_sems.at[a]).wait()

    return pl.pallas_call(
        body, name=name,
        out_shape=[jax.ShapeDtypeStruct((NDEV,) + s.shape, s.dtype) for s in shards],
        in_specs=[ANY] * n, out_specs=[ANY] * n, scratch_shapes=_exchange_sems(n),
    )(*shards)


def _dwin_exchange(u, dz, init, cs_init, name):
    m, ka = u.shape
    n = dz.shape[1]
    pc = n // NDEV
    tk = _row_tile(m, 512)
    nk = m // tk

    def body(pidx_ref, u_ref, dz_ref, init_ref, csi_ref, rwin, cs_ref, acc, sbuf, wsend, wrecv, wloc):
        s, k = pl.program_id(0), pl.program_id(1)
        me = _my_index()

        def slab_copy(slot, p):
            return pltpu.make_async_remote_copy(
                src_ref=sbuf.at[slot], dst_ref=rwin.at[me], send_sem=wsend.at[slot], recv_sem=wrecv.at[me],
                device_id=_dev_of(p), device_id_type=pl.DeviceIdType.MESH)

        @pl.when(k == 0)
        def _():
            acc[...] = init_ref[...]
            cs_ref[...] = csi_ref[...]

        bv = dz_ref[...]
        acc[...] += _dot_tn(u_ref[...], bv)
        cs_ref[...] += jnp.sum(bv.astype(F32), axis=0, keepdims=True)

        @pl.when(k == nk - 1)
        def _():
            slot = s % 2

            @pl.when(s >= 2)
            def _():
                slab_copy(slot, me).wait_send()

            sbuf[slot] = acc[...].astype(BF16)

            @pl.when(s < NDEV - 1)
            def _():
                slab_copy(slot, _step_peer(s)).start()

            @pl.when(s == NDEV - 1)
            def _():
                own = pltpu.make_async_copy(sbuf.at[slot], rwin.at[me], wloc.at[0])
                own.start()
                slab_copy(1 - slot, me).wait_send()
                for j in range(NDEV):
                    @pl.when(me != j)
                    def _():
                        pltpu.make_async_remote_copy(
                            src_ref=sbuf.at[0], dst_ref=rwin.at[j], send_sem=wsend.at[0], recv_sem=wrecv.at[j],
                            device_id=_dev_tuple(j), device_id_type=pl.DeviceIdType.MESH).wait_recv()
                own.wait()

    grid_spec = pltpu.PrefetchScalarGridSpec(
        num_scalar_prefetch=1, grid=(NDEV, nk),
        in_specs=[pl.BlockSpec((tk, ka), lambda s, k, pidx: (k, 0)),
                  pl.BlockSpec((tk, pc), lambda s, k, pidx: (k, pidx[s])),
                  pl.BlockSpec((ka, pc), lambda s, k, pidx: (0, pidx[s])),
                  pl.BlockSpec((1, pc), lambda s, k, pidx: (0, pidx[s]))],
        out_specs=[ANY, pl.BlockSpec((1, pc), lambda s, k, pidx: (0, pidx[s]))],
        scratch_shapes=[pltpu.VMEM((ka, pc), F32), pltpu.VMEM((2, ka, pc), BF16),
                        pltpu.SemaphoreType.DMA((2,)), pltpu.SemaphoreType.DMA((NDEV,)), pltpu.SemaphoreType.DMA((1,))])
    return pl.pallas_call(
        body, name=name, grid_spec=grid_spec,
        out_shape=[jax.ShapeDtypeStruct((NDEV, ka, pc), BF16), jax.ShapeDtypeStruct((1, n), F32)],
    )(_peer_schedule(_STEP_MASKS), u, dz, init, cs_init)


def _inproj_gather(u, w_loc, bias, name):
    m, k = u.shape
    pc = w_loc.shape[1]
    n = pc * NDEV
    tm = _row_tile(m, 512)
    ni = m // tm
    direct = (1, 2, 3, 6)
    relay_sem = {2: 4, 3: 5, 6: 7}

    def body(pidx_ref, u_ref, b_ref, wl_ref, z_ref, wall, wbuf, wsend, wrecv, ldsem, ownsem):
        s, i = pl.program_id(0), pl.program_id(1)
        me = _my_index()

        def shard_push(t):
            return pltpu.make_async_remote_copy(
                src_ref=wl_ref, dst_ref=wall.at[me], send_sem=wsend.at[t], recv_sem=wrecv.at[me],
                device_id=_dev_of(_step_peer(t, _GATHER_MASKS)), device_id_type=pl.DeviceIdType.MESH)

        def relay(t):
            p = _step_peer(t, _GATHER_MASKS)
            return pltpu.make_async_remote_copy(
                src_ref=wall.at[p], dst_ref=wall.at[p], send_sem=wsend.at[relay_sem[t]], recv_sem=wrecv.at[p],
                device_id=_dev_of(_step_peer(1, _GATHER_MASKS)), device_id_type=pl.DeviceIdType.MESH)

        def load(slot, src):
            return pltpu.make_async_copy(src, wbuf.at[slot], ldsem.at[slot])

        own = pltpu.make_async_copy(wl_ref, wall.at[me], ownsem.at[0])

        @pl.when((s == 0) & (i == 0))
        def _():
            own.start()
            load(0, wl_ref).start()
            for t in direct:
                shard_push(t).start()

        @pl.when((i == ni // 2) & (s < NDEV - 1))
        def _():
            nxt = _step_peer(s + 1, _GATHER_MASKS)
            pltpu.make_async_remote_copy(
                src_ref=wl_ref, dst_ref=wall.at[nxt], send_sem=wsend.at[0], recv_sem=wrecv.at[nxt],
                device_id=_dev_of(nxt), device_id_type=pl.DeviceIdType.MESH).wait_recv()
            for t in relay_sem:
                @pl.when(s + 1 == t)
                def _():
                    relay(t).start()
            load((s + 1) % 2, wall.at[nxt]).start()

        @pl.when(i == 0)
        def _():
            load(s % 2, wl_ref).wait()

        z_ref[...] = _dot(u_ref[...], wbuf[s % 2]) + b_ref[...]

        @pl.when((s == NDEV - 1) & (i == ni - 1))
        def _():
            own.wait()
            for t in direct:
                shard_push(t).wait_send()
            for t in relay_sem:
                relay(t).wait_send()

    grid_spec = pltpu.PrefetchScalarGridSpec(
        num_scalar_prefetch=1, grid=(NDEV, ni),
        in_specs=[pl.BlockSpec((tm, k), lambda s, i, pidx: (i, 0)),
                  pl.BlockSpec((1, pc), lambda s, i, pidx: (0, pidx[s])), ANY],
        out_specs=[pl.BlockSpec((tm, pc), lambda s, i, pidx: (i, pidx[s])), ANY],
        scratch_shapes=[pltpu.VMEM((2, k, pc), BF16),
                        pltpu.SemaphoreType.DMA((NDEV,)), pltpu.SemaphoreType.DMA((NDEV,)),
                        pltpu.SemaphoreType.DMA((2,)), pltpu.SemaphoreType.DMA((1,))])
    return pl.pallas_call(
        body, name=name, grid_spec=grid_spec,
        out_shape=[jax.ShapeDtypeStruct((m, n), F32), jax.ShapeDtypeStruct((NDEV, k, pc), w_loc.dtype)],
    )(_peer_schedule(_GATHER_MASKS), u, bias, w_loc)


def _adam_math(g, w, m, v):
    m2 = ADAM_B1 * m + (1.0 - ADAM_B1) * g
    v2 = ADAM_B2 * v + (1.0 - ADAM_B2) * (g * g)
    delta = -ADAM_LR * ((m2 * ADAM_C1) / (jnp.sqrt(v2 * ADAM_C2) + ADAM_EPS) + ADAM_WD * w)
    return delta, m2, v2


def _row_tile(r, cap):
    t = min(r, cap)
    while r % t:
        t //= 2
    return t


def _adamw(g, w, m, v, name):
    shape = w.shape
    cols = shape[-1] if w.ndim >= 2 and shape[-1] % 128 == 0 else 128
    g2, w2, m2, v2 = (t.reshape(-1, cols) for t in (g, w, m, v))
    r = g2.shape[0]
    tr = _row_tile(r, 256)

    def body(g_ref, w_ref, m_ref, v_ref, d_ref, mo_ref, vo_ref):
        d, mm, vv = _adam_math(g_ref[...], w_ref[...], m_ref[...], v_ref[...])
        d_ref[...] = d
        mo_ref[...] = mm
        vo_ref[...] = vv

    spec = pl.BlockSpec((tr, cols), lambda i: (i, 0))
    outs = pl.pallas_call(
        body, name=name, grid=(r // tr,),
        out_shape=[jax.ShapeDtypeStruct((r, cols), F32)] * 3,
        in_specs=[spec] * 4, out_specs=[spec] * 3,
    )(g2, w2, m2, v2)
    return tuple(o.reshape(shape) for o in outs)


def _sum_adamw(parts, w, m, v, name):
    _, r, c = parts.shape
    shape = w.shape
    w2, m2, v2 = (t.reshape(r, c) for t in (w, m, v))
    tr = _row_tile(r, 128)

    def body(p_ref, w_ref, m_ref, v_ref, g_ref, d_ref, mo_ref, vo_ref):
        g = p_ref[0].astype(F32)
        for k in range(1, NDEV):
            g = g + p_ref[k].astype(F32)
        d, mm, vv = _adam_math(g, w_ref[...], m_ref[...], v_ref[...])
        g_ref[...] = g
        d_ref[...] = d
        mo_ref[...] = mm
        vo_ref[...] = vv

    spec = pl.BlockSpec((tr, c), lambda i: (i, 0))
    outs = pl.pallas_call(
        body, name=name, grid=(r // tr,),
        out_shape=[jax.ShapeDtypeStruct((r, c), F32)] * 4,
        in_specs=[pl.BlockSpec((NDEV, tr, c), lambda i: (0, i, 0))] + [spec] * 3, out_specs=[spec] * 4,
    )(parts, w2, m2, v2)
    return tuple(o.reshape(shape) for o in outs)


def _sum_rows(parts, name):
    _, r, c = parts.shape

    def body(p_ref, o_ref):
        g = p_ref[0]
        for k in range(1, NDEV):
            g = g + p_ref[k]
        o_ref[...] = g

    return pl.pallas_call(
        body, name=name, out_shape=jax.ShapeDtypeStruct((r, c), F32),
    )(parts)


def _cast_bf16(arrays):
    n = len(arrays)

    def body(*refs):
        for src, dst in zip(refs[:n], refs[n:]):
            dst[...] = src[...].astype(BF16)

    return pl.pallas_call(
        body, name="cast_weights", out_shape=[jax.ShapeDtypeStruct(a.shape, BF16) for a in arrays],
    )(*arrays)


def _prep(cc, w_mod_full, b_mod, lbl):
    def body(cc_ref, w_ref, b_ref, l_ref, mod_ref, lb_ref):
        t = cc_ref[...]
        s = (t * _sigmoid(t)).astype(BF16)
        mod_ref[...] = _dot(s, w_ref[...]) + b_ref[...]
        lb_ref[...] = _sigmoid(l_ref[0] - l_ref[1])

    return pl.pallas_call(
        body, name="prep",
        out_shape=[jax.ShapeDtypeStruct((8, 3 * D), F32), jax.ShapeDtypeStruct((8, D), F32)],
    )(cc, w_mod_full, b_mod, lbl)


def _modulate(xin, mod, row, name):
    n = xin.shape[0]
    tm = _row_tile(n, 512)

    def body(x_ref, mod_ref, u_ref):
        sh = mod_ref[row:row + 1, 0:D]
        sc = mod_ref[row:row + 1, D:2 * D]
        u_ref[...] = (x_ref[...] * (1.0 + sc) + sh).astype(BF16)

    return pl.pallas_call(
        body, name=name, grid=(n // tm,),
        out_shape=jax.ShapeDtypeStruct((n, D), BF16),
        in_specs=[pl.BlockSpec((tm, D), lambda i: (i, 0)), pl.BlockSpec((8, 3 * D), lambda i: (0, 0))],
        out_specs=pl.BlockSpec((tm, D), lambda i: (i, 0)),
    )(xin, mod)


def _mm_bias(a, w_all, bias, name):
    m, k = a.shape
    tn = w_all.shape[2]
    n = tn * NDEV
    tm = _row_tile(m, 512)

    def body(a_ref, b_ref, bias_ref, o_ref):
        o_ref[...] = _dot(a_ref[...], b_ref[0]) + bias_ref[...]

    return pl.pallas_call(
        body, name=name, grid=(NDEV, m // tm),
        out_shape=jax.ShapeDtypeStruct((m, n), F32),
        in_specs=[pl.BlockSpec((tm, k), lambda j, i: (i, 0)), pl.BlockSpec((1, k, tn), lambda j, i: (j, 0, 0)),
                  pl.BlockSpec((1, tn), lambda j, i: (0, j))],
        out_specs=pl.BlockSpec((tm, tn), lambda j, i: (i, j)),
    )(a, w_all, bias)


def _mm_tn(a, b, init, name, with_colsum=False, colsum_init=None, out_dtype=F32):
    m, ka = a.shape
    n = b.shape[1]
    tk = _row_tile(m, 512)
    tn = 1024
    nk = m // tk
    has_init = init is not None

    def body(*refs):
        a_ref, b_ref = refs[0], refs[1]
        pos = 2
        init_ref = cs_init_ref = None
        if has_init:
            init_ref = refs[pos]
            pos += 1
            if with_colsum:
                cs_init_ref = refs[pos]
                pos += 1
        o_ref = refs[pos]
        cs_ref = refs[pos + 1] if with_colsum else None
        acc = refs[-1]
        k = pl.program_id(1)

        @pl.when(k == 0)
        def _():
            if has_init:
                acc[...] = init_ref[...]
                if with_colsum:
                    cs_ref[...] = cs_init_ref[...]
            else:
                acc[...] = jnp.zeros_like(acc)
                if with_colsum:
                    cs_ref[...] = jnp.zeros_like(cs_ref)

        bv = b_ref[...]
        acc[...] += _dot_tn(a_ref[...], bv)
        if with_colsum:
            cs_ref[...] += jnp.sum(bv.astype(F32), axis=0, keepdims=True)

        @pl.when(k == nk - 1)
        def _():
            o_ref[...] = acc[...].astype(out_dtype)

    in_specs = [pl.BlockSpec((tk, ka), lambda j, k: (k, 0)), pl.BlockSpec((tk, tn), lambda j, k: (k, j))]
    args = [a, b]
    if has_init:
        in_specs.append(pl.BlockSpec((ka, tn), lambda j, k: (0, j)))
        args.append(init)
        if with_colsum:
            in_specs.append(pl.BlockSpec((1, tn), lambda j, k: (0, j)))
            args.append(colsum_init)
    out_shape = [jax.ShapeDtypeStruct((ka, n), out_dtype)]
    out_specs = [pl.BlockSpec((ka, tn), lambda j, k: (0, j))]
    if with_colsum:
        out_shape.append(jax.ShapeDtypeStruct((1, n), F32))
        out_specs.append(pl.BlockSpec((1, tn), lambda j, k: (0, j)))
    outs = pl.pallas_call(
        body, name=name, grid=(n // tn, nk), out_shape=out_shape, in_specs=in_specs, out_specs=out_specs,
        scratch_shapes=[pltpu.VMEM((ka, tn), F32)],
    )(*args)
    return outs if with_colsum else outs[0]


def _input_grad(dz, w_all, xin, dr, mod, row, name, side=(), side_splits=()):
    m, n = dz.shape
    tm = _row_tile(m, 512)
    tk = w_all.shape[2]
    nk = NDEV
    ni = m // tm
    has_dr = dr is not None
    ns = len(side)
    piece_shapes, piece_of = _pieces(side, side_splits)

    def body(*refs):
        dz_ref, w_ref, x_ref = refs[:3]
        pos = 3
        dr_ref = refs[pos] if has_dr else None
        pos += int(has_dr)
        mod_ref = refs[pos]
        side_in = refs[pos + 1:pos + 1 + ns]
        pos += 1 + ns
        gx_ref = refs[pos] if has_dr else None
        pos += int(has_dr)
        vec_ref = refs[pos]
        side_out = refs[pos + 1:pos + 1 + ns]
        acc = refs[pos + 1 + ns]
        i, k = pl.program_id(0), pl.program_id(1)
        if ns:
            side_start, side_finish = _exchange(side_in, side_out, refs[pos + 2 + ns:], piece_of)

            @pl.when((i == 0) & (k == 0))
            def _():
                side_start()

        @pl.when(k == 0)
        def _():
            acc[...] = jnp.zeros_like(acc)

        @pl.when((i == 0) & (k == 0))
        def _():
            vec_ref[...] = jnp.zeros_like(vec_ref)

        acc[...] += _dot_nt(dz_ref[...], w_ref[0])

        @pl.when(k == nk - 1)
        def _():
            du = acc[...]
            xv = x_ref[...]
            if has_dr:
                sc = mod_ref[row:row + 1, D:2 * D]
                gx_ref[...] = ALPHA * dr_ref[...] + du * (1.0 + sc)
            vec_ref[0:1, :] += jnp.sum(du, axis=0, keepdims=True)
            vec_ref[1:2, :] += jnp.sum(du * xv, axis=0, keepdims=True)

        if ns:
            @pl.when((i == ni - 1) & (k == nk - 1))
            def _():
                side_finish()

    row_spec = pl.BlockSpec((tm, D), lambda i, k: (i, 0))
    in_specs = [pl.BlockSpec((tm, tk), lambda i, k: (i, k)), pl.BlockSpec((1, D, tk), lambda i, k: (k, 0, 0)), row_spec]
    args = [dz, w_all, xin]
    if has_dr:
        in_specs.append(row_spec)
        args.append(dr)
    in_specs.append(pl.BlockSpec((8, 3 * D), lambda i, k: (0, 0)))
    args.append(mod)
    in_specs += [ANY] * ns
    args += list(side)
    out_shape, out_specs = [], []
    if has_dr:
        out_shape.append(jax.ShapeDtypeStruct((m, D), F32))
        out_specs.append(row_spec)
    out_shape.append(jax.ShapeDtypeStruct((8, D), F32))
    out_specs.append(pl.BlockSpec((8, D), lambda i, k: (0, 0)))
    out_shape += [jax.ShapeDtypeStruct((NDEV,) + piece_shapes[a], side[a].dtype) for a in range(ns)]
    out_specs += [ANY] * ns
    outs = pl.pallas_call(
        body, name=name, grid=(ni, nk), out_shape=out_shape, in_specs=in_specs, out_specs=out_specs,
        scratch_shapes=[pltpu.VMEM((tm, D), F32)] + (_exchange_sems(ns) if ns else []),
    )(*args)
    return tuple(outs) if has_dr else (None, *outs)


def _tri(reverse):
    r = lax.broadcasted_iota(jnp.int32, (CHUNK, CHUNK), 0)
    c = lax.broadcasted_iota(jnp.int32, (CHUNK, CHUNK), 1)
    return (c >= r) if reverse else (c <= r)


def _cum_f32(tri_b, t):
    hi = t.astype(BF16)
    r1 = t - hi.astype(F32)
    mid = r1.astype(BF16)
    lo = (r1 - mid.astype(F32)).astype(BF16)
    return _dot(tri_b, hi) + _dot(tri_b, mid) + _dot(tri_b, lo)


def _gla_features(zq, zf, lb):
    sq = _sigmoid(zq)
    q = zq * sq * Q_SCALE
    sf = _sigmoid(zf)
    f = lb + (1.0 - lb) * sf
    return q, sq, f, sf


def _gla_block(n):
    return 256 if n % 256 == 0 else CHUNK


def _gla_fwd(z, lb, s0, d, name, side=()):
    n = z.shape[0]
    blk = _gla_block(n)
    nb, npb = n // blk, blk // CHUNK
    reverse = d == 1
    last = 0 if reverse else CHUNK - 1
    order = list(range(npb))[::-1] if reverse else list(range(npb))
    ns = len(side)

    def bmap(i):
        return nb - 1 - i if reverse else i

    hp = GLA_HEADS_PER_STEP
    hw = hp * DH
    units = [(hh, cidx) for hh in range(hp) for cidx in order]

    def body(zq_ref, zf_ref, zv_ref, lb_ref, s0_ref, *rest):
        side_in = rest[:ns]
        o_ref, ss_ref, sf_ref = rest[ns:ns + 3]
        side_out = rest[ns + 3:2 * ns + 3]
        st = rest[2 * ns + 3]
        i = pl.program_id(1)
        if ns:
            side_start, side_finish = _exchange(side_in, side_out, rest[2 * ns + 4:])

            @pl.when((pl.program_id(0) == 0) & (i == 0))
            def _():
                side_start()

        @pl.when(i == 0)
        def _():
            st[...] = s0_ref[...]

        mask = _tri(reverse)
        tri_b = jnp.where(mask, 1.0, 0.0).astype(BF16)
        feat = {}
        for u in units:
            hh, cidx = u
            rows, cols = pl.ds(cidx * CHUNK, CHUNK), pl.ds(hh * DH, DH)
            q, _, f, _ = _gla_features(zq_ref[rows, cols], zf_ref[rows, cols], lb_ref[d:d + 1, cols])
            feat[u] = (q, 1.0 - f, jnp.log(f), zv_ref[rows, cols].astype(BF16))
        dec = {u: _cum_f32(tri_b, feat[u][2]) for u in units}
        ops = {}
        for u in units:
            q, k, _, vb = feat[u]
            g = dec[u]
            gl = g[last:last + 1, :]
            ops[u] = ((q * jnp.exp(g)).astype(BF16), (k * jnp.exp(-g)).astype(BF16),
                      (k * jnp.exp(gl - g)).astype(BF16), jnp.exp(gl), vb)
        att = {u: jnp.where(mask, _dot_nt(ops[u][0], ops[u][1]), 0.0).astype(BF16) for u in units}
        upd = {u: _dot_tn(ops[u][4], ops[u][2]) for u in units}
        intra = {u: _dot(att[u], ops[u][4]) for u in units}
        s_in = {}
        for hh in range(hp):
            s = st[hh]
            for cidx in order:
                s_in[(hh, cidx)] = s
                s = s * ops[(hh, cidx)][3] + upd[(hh, cidx)]
            st[hh] = s
            sf_ref[hh] = s
        for u in units:
            hh, cidx = u
            rows, cols = pl.ds(cidx * CHUNK, CHUNK), pl.ds(hh * DH, DH)
            o_ref[rows, cols] = intra[u] + _dot_nt(ops[u][0], s_in[u].astype(BF16))
            ss_ref[hh, cidx] = s_in[u]

        if ns:
            @pl.when((pl.program_id(0) == NH // hp - 1) & (i == nb - 1))
            def _():
                side_finish()

    def col(g):
        return lambda h, i: (bmap(i), g * (NH // hp) + h)

    return pl.pallas_call(
        body, name=name, grid=(NH // hp, nb),
        out_shape=[jax.ShapeDtypeStruct((n, D), F32), jax.ShapeDtypeStruct((NH, n // CHUNK, DH, DH), F32),
                   jax.ShapeDtypeStruct((NH, DH, DH), F32)]
        + [jax.ShapeDtypeStruct((NDEV,) + t.shape, t.dtype) for t in side],
        in_specs=[pl.BlockSpec((blk, hw), col(0)), pl.BlockSpec((blk, hw), col(1 + d)),
                  pl.BlockSpec((blk, hw), col(3)), pl.BlockSpec((8, hw), lambda h, i: (0, h)),
                  pl.BlockSpec((hp, DH, DH), lambda h, i: (h, 0, 0))] + [ANY] * ns,
        out_specs=[pl.BlockSpec((blk, hw), lambda h, i: (bmap(i), h)),
                   pl.BlockSpec((hp, npb, DH, DH), lambda h, i: (h, bmap(i), 0, 0)),
                   pl.BlockSpec((hp, DH, DH), lambda h, i: (h, 0, 0))] + [ANY] * ns,
        scratch_shapes=[pltpu.VMEM((hp, DH, DH), F32)] + (_exchange_sems(ns) if ns else []),
    )(z, z, z, lb, s0, *side)


def _gla_bwd(z, lb, s_start, do, ds_fin, acc_q, acc_v, d, name, f_dtype=F32, into=None, side=(), side_splits=()):
    n = z.shape[0]
    blk = _gla_block(n)
    nb, npb = n // blk, blk // CHUNK
    reverse = d == 1
    last = 0 if reverse else CHUNK - 1
    order = list(range(npb)) if reverse else list(range(npb))[::-1]
    has_do = do is not None
    has_acc = acc_q is not None
    fused = into is not None
    assert not fused or d == 1
    ns = len(side)
    assert not (fused and ns)
    piece_shapes, piece_of = _pieces(side, side_splits)
    hp = NH if fused else GLA_HEADS_PER_STEP
    hw = hp * DH
    units = [(hh, cidx) for hh in range(hp) for cidx in order]

    def bmap(i):
        return i if reverse else nb - 1 - i

    def body(*refs):
        zq_ref, zf_ref, zv_ref, lb_ref, ss_ref, dsf_ref = refs[:6]
        pos = 6
        do_ref = aq_ref = av_ref = None
        if has_do:
            do_ref = refs[pos]
            pos += 1
        if has_acc:
            aq_ref, av_ref = refs[pos], refs[pos + 1]
            pos += 2
        if fused:
            other_ref = refs[pos + 1]
            dz_ref, dlb_ref, ds0_ref, dst = refs[pos + 2:]
            dz_ref[:, D:2 * D] = other_ref[...]
        else:
            side_in = refs[pos:pos + ns]
            dzq_ref, dzf_ref, dzv_ref, dlb_ref, ds0_ref = refs[pos + ns:pos + ns + 5]
            side_out = refs[pos + ns + 5:pos + 2 * ns + 5]
            dst = refs[pos + 2 * ns + 5]
        i = pl.program_id(1)
        if ns:
            side_start, side_finish = _exchange(side_in, side_out, refs[pos + 2 * ns + 6:], piece_of)

            @pl.when((pl.program_id(0) == 0) & (i == 0))
            def _():
                side_start()

        @pl.when(i == 0)
        def _():
            dst[...] = dsf_ref[...]
            dlb_ref[...] = jnp.zeros_like(dlb_ref)

        mask = _tri(reverse)
        tri_b = jnp.where(mask, 1.0, 0.0).astype(BF16)
        tri_t = jnp.where(_tri(not reverse), 1.0, 0.0).astype(BF16)

        def where(u):
            return pl.ds(u[1] * CHUNK, CHUNK), pl.ds(u[0] * DH, DH)

        feat = {}
        for u in units:
            rows, cols = where(u)
            zq, zf = zq_ref[rows, cols], zf_ref[rows, cols]
            lbv = lb_ref[d:d + 1, cols]
            q, sq, f, sf = _gla_features(zq, zf, lbv)
            feat[u] = dict(zq=zq, q=q, sq=sq, f=f, sf=sf, lbv=lbv, k=1.0 - f, vb=zv_ref[rows, cols].astype(BF16))
        dec = {u: _cum_f32(tri_b, jnp.log(feat[u]["f"])) for u in units}
        for u in units:
            w = feat[u]
            g = dec[u]
            gl = g[last:last + 1, :]
            w["eg"], w["egi"], w["ege"], w["egl"] = jnp.exp(g), jnp.exp(-g), jnp.exp(gl - g), jnp.exp(gl)
            w["qd"], w["ki"], w["ke"] = w["q"] * w["eg"], w["k"] * w["egi"], w["k"] * w["ege"]
            w["qdb"], w["kib"], w["keb"] = w["qd"].astype(BF16), w["ki"].astype(BF16), w["ke"].astype(BF16)
            w["s_in"] = ss_ref[u[0], u[1]]
        if has_do:
            for u in units:
                w = feat[u]
                rows, cols = where(u)
                w["dob"] = do_ref[rows, cols].astype(BF16)
            for u in units:
                w = feat[u]
                w["a"] = jnp.where(mask, _dot_nt(w["qdb"], w["kib"]), 0.0).astype(BF16)
                w["da"] = jnp.where(mask, _dot_nt(w["dob"], w["vb"]), 0.0).astype(BF16)
                w["m"] = _dot_tn(w["dob"], w["qdb"])
        for hh in range(hp):
            ds = dst[hh]
            for cidx in order:
                w = feat[(hh, cidx)]
                w["ds"] = ds
                ds = ds * w["egl"]
                if has_do:
                    ds = ds + w["m"]
            dst[hh] = ds
            ds0_ref[hh] = ds
        for u in units:
            w = feat[u]
            dsb = w["ds"].astype(BF16)
            w["dke"] = _dot(w["vb"], dsb)
            w["dv"] = _dot_nt(w["keb"], dsb)
            if has_do:
                w["dv"] = w["dv"] + _dot_tn(w["a"], w["dob"])
                w["dqd"] = _dot(w["da"], w["kib"]) + _dot(w["dob"], w["s_in"].astype(BF16))
                w["dki"] = _dot_tn(w["da"], w["qdb"])
        for u in units:
            w = feat[u]
            dkeke = w["dke"] * w["ke"]
            w["dgl"] = (w["egl"] * jnp.sum(w["s_in"] * w["ds"], axis=0, keepdims=True)
                        + jnp.sum(dkeke, axis=0, keepdims=True))
            dg = -dkeke
            dk = w["dke"] * w["ege"]
            if has_do:
                dg = dg + w["dqd"] * w["qd"] - w["dki"] * w["ki"]
                dk = dk + w["dki"] * w["egi"]
            w["dg"], w["dk"] = dg, dk
        dlf = {u: _cum_f32(tri_t, feat[u]["dg"]) for u in units}
        for u in units:
            w = feat[u]
            rows, cols = where(u)
            df = (dlf[u] + w["dgl"]) / w["f"] - w["dk"]
            sf = w["sf"]
            dzf = df * (1.0 - w["lbv"]) * sf * (1.0 - sf)
            dlb_ref[0:1, cols] += jnp.sum(df * (1.0 - sf), axis=0, keepdims=True)
            if has_do:
                dzq = w["dqd"] * w["eg"] * (Q_SCALE * _dsilu(w["zq"], w["sq"]))
            else:
                dzq = jnp.zeros((CHUNK, DH), F32)
            dv = w["dv"]
            if has_acc:
                dzq = dzq + aq_ref[rows, cols]
                dv = dv + av_ref[rows, cols]
            if fused:
                lane = u[0] * DH
                dz_ref[rows, pl.ds(lane, DH)] = dzq.astype(BF16)
                dz_ref[rows, pl.ds(2 * D + lane, DH)] = dzf.astype(BF16)
                dz_ref[rows, pl.ds(3 * D + lane, DH)] = dv.astype(BF16)
            else:
                dzq_ref[rows, cols] = dzq
                dzf_ref[rows, cols] = dzf.astype(f_dtype)
                dzv_ref[rows, cols] = dv

        if ns:
            @pl.when((pl.program_id(0) == NH // hp - 1) & (i == nb - 1))
            def _():
                side_finish()

    def col(g):
        return lambda h, i: (bmap(i), g * (NH // hp) + h)

    tok = pl.BlockSpec((blk, hw), lambda h, i: (bmap(i), h))
    state = pl.BlockSpec((hp, DH, DH), lambda h, i: (h, 0, 0))
    in_specs = [pl.BlockSpec((blk, hw), col(0)), pl.BlockSpec((blk, hw), col(1 + d)), pl.BlockSpec((blk, hw), col(3)),
                pl.BlockSpec((8, hw), lambda h, i: (0, h)),
                pl.BlockSpec((hp, npb, DH, DH), lambda h, i: (h, bmap(i), 0, 0)), state]
    args = [z, z, z, lb, s_start, ds_fin]
    if has_do:
        in_specs.append(tok)
        args.append(do)
    if has_acc:
        in_specs += [tok, tok]
        args += [acc_q, acc_v]
    tail_shape = [jax.ShapeDtypeStruct((8, D), F32), jax.ShapeDtypeStruct((NH, DH, DH), F32)]
    tail_specs = [pl.BlockSpec((8, hw), lambda h, i: (0, h)), state]
    if fused:
        buf, other = into
        aliases = {len(args): 0}
        in_specs += [ANY, tok]
        args += [buf, other]
        out_shape = [jax.ShapeDtypeStruct(buf.shape, buf.dtype)] + tail_shape
        out_specs = [pl.BlockSpec((blk, 4 * D), lambda h, i: (bmap(i), 0))] + tail_specs
    else:
        aliases = {}
        in_specs += [ANY] * ns
        args += list(side)
        out_shape = [jax.ShapeDtypeStruct((n, D), F32), jax.ShapeDtypeStruct((n, D), f_dtype),
                     jax.ShapeDtypeStruct((n, D), F32)] + tail_shape
        out_shape += [jax.ShapeDtypeStruct((NDEV,) + piece_shapes[a], side[a].dtype) for a in range(ns)]
        out_specs = [tok, tok, tok] + tail_specs + [ANY] * ns
    return pl.pallas_call(
        body, name=name, grid=(NH // hp, nb), out_shape=out_shape, in_specs=in_specs, out_specs=out_specs,
        input_output_aliases=aliases,
        scratch_shapes=[pltpu.VMEM((hp, DH, DH), F32)] + (_exchange_sems(ns) if ns else []),
    )(*args)


def _shift(t, s, fill, down):
    n = t.shape[0]
    rows = lax.broadcasted_iota(jnp.int32, t.shape, 0)
    if down:
        return jnp.where(rows >= s, pltpu.roll(t, s, 0), fill)
    return jnp.where(rows < n - s, pltpu.roll(t, n - s, 0), fill)


SUBLANES = 8
LRU_SAVED = 4


def _chain_scan(a, b, h_in, down):
    n = a.shape[0]
    ng = n // SUBLANES
    rows = lax.broadcasted_iota(jnp.int32, (SUBLANES, a.shape[1]), 0)
    local = []
    for g in range(ng):
        aa, bb = a[g * SUBLANES:(g + 1) * SUBLANES], b[g * SUBLANES:(g + 1) * SUBLANES]
        for s in (1, 2, 4):
            if down:
                keep, amt = rows >= s, s
            else:
                keep, amt = rows < SUBLANES - s, SUBLANES - s
            bb = bb + aa * jnp.where(keep, pltpu.roll(bb, amt, 0), 0.0)
            aa = aa * jnp.where(keep, pltpu.roll(aa, amt, 0), 1.0)
        local.append((aa, bb))
    out = [None] * ng
    carry = h_in
    for g in (range(ng) if down else range(ng - 1, -1, -1)):
        aa, bb = local[g]
        hg = bb + aa * carry
        out[g] = hg
        carry = hg[SUBLANES - 1:SUBLANES] if down else hg[0:1]
    return (jnp.concatenate(out, axis=0) if ng > 1 else out[0]), carry


def _conv_taps(xv):
    return (_shift(xv, 1, 0.0, True), xv, _shift(xv, 1, 0.0, False), _shift(xv, 2, 0.0, False))


def _conv(taps, cw, cb):
    return cb + cw[0:1, :] * taps[0] + cw[1:2, :] * taps[1] + cw[2:3, :] * taps[2] + cw[3:4, :] * taps[3]


def _neg_expm1(t):
    series = -t * (1.0 + t * (0.5 + t * (1.0 / 6.0 + t * (1.0 / 24.0 + t * (1.0 / 120.0)))))
    return jnp.where(t > -0.1, series, 1.0 - jnp.exp(t))


def _lru_gates(xc, wr, br, wi, bi, lam):
    xcb = xc.astype(BF16)
    r = _sigmoid(_dot(xcb, wr) + br)
    gi = _sigmoid(_dot(xcb, wi) + bi)
    sp = jnp.maximum(-lam, 0.0) + jnp.log(1.0 + jnp.exp(-jnp.abs(lam)))
    la = -RG_C * r * sp
    a = jnp.exp(la)
    mult = jnp.sqrt(_neg_expm1(2.0 * la))
    return xcb, r, gi, sp, a, mult


def _lru_fwd(xin, blk, cw, cb, wr, br, wi, bi, lam, h0, acc_h, d, name):
    n = xin.shape[0]
    nb = n // blk
    reverse = d == 1
    down = not reverse
    has_acc = acc_h is not None

    def bmap(i):
        return nb - 1 - i if reverse else i

    def body(*refs):
        x_ref, cw_ref, cb_ref, wr_ref, br_ref, wi_ref, bi_ref, lam_ref, h0_ref = refs[:9]
        pos = 9
        acc_ref = refs[pos] if has_acc else None
        pos += int(has_acc)
        h_ref, hin_ref, hfin_ref, sav_a_ref, sav_ref = refs[pos:pos + 5]
        pos += 5
        hsum_ref = refs[pos] if has_acc else None
        carry = refs[-1]
        i = pl.program_id(0)

        @pl.when(i == 0)
        def _():
            carry[...] = h0_ref[...]

        for g in range(NH):
            cols = pl.ds(g * DH, DH)
            xc = _conv(_conv_taps(x_ref[:, cols]), cw_ref[:, cols], cb_ref[:, cols])
            _, r, gi, _, a, mult = _lru_gates(xc, wr_ref[g], br_ref[:, cols], wi_ref[g], bi_ref[:, cols],
                                              lam_ref[:, cols])
            sav_a_ref[:, cols] = a
            for slot, val in enumerate((xc, r, gi, mult)):
                sav_ref[slot, :, cols] = val.astype(BF16)
            hin = carry[:, cols]
            h, h_last = _chain_scan(a, mult * gi * xc, hin, down)
            h_ref[:, cols] = h
            if has_acc:
                hsum_ref[:, cols] = h + acc_ref[:, cols]
            hin_ref[0, :, cols] = hin
            carry[:, cols] = h_last
            hfin_ref[:, cols] = h_last

    vec = pl.BlockSpec((1, D), lambda i: (0, 0))
    wsp = pl.BlockSpec((NH, DH, DH), lambda i: (0, 0, 0))
    tok = pl.BlockSpec((blk, D), lambda i: (bmap(i), 0))
    in_specs = [tok, pl.BlockSpec((4, D), lambda i: (0, 0)), vec, wsp, vec, wsp, vec, vec, vec]
    args = [xin, cw, cb, wr, br, wi, bi, lam, h0]
    out_shape = [jax.ShapeDtypeStruct((n, D), F32), jax.ShapeDtypeStruct((nb, 1, D), F32),
                 jax.ShapeDtypeStruct((1, D), F32), jax.ShapeDtypeStruct((n, D), F32),
                 jax.ShapeDtypeStruct((LRU_SAVED, n, D), BF16)]
    out_specs = [tok, pl.BlockSpec((1, 1, D), lambda i: (bmap(i), 0, 0)), vec, tok,
                 pl.BlockSpec((LRU_SAVED, blk, D), lambda i: (0, bmap(i), 0))]
    if has_acc:
        in_specs.append(tok)
        args.append(acc_h)
        out_shape.append(jax.ShapeDtypeStruct((n, D), F32))
        out_specs.append(tok)
    return pl.pallas_call(
        body, name=name, grid=(nb,), out_shape=out_shape, in_specs=in_specs, out_specs=out_specs,
        scratch_shapes=[pltpu.VMEM((1, D), F32)],
    )(*args)


def _lru_bwd(xin, blk, cw, wr, wi, lam, sav, h, hin, dh, cg_fin, acc_dx, init, d, name, dx_dtype=F32):
    n = xin.shape[0]
    nb = n // blk
    reverse = d == 1
    down = not reverse
    first = blk - 1 if reverse else 0
    has_dh = dh is not None
    has_acc = acc_dx is not None
    has_init = init is not None

    def bmap(i):
        return i if reverse else nb - 1 - i

    def body(*refs):
        (x_ref, cw_ref, wr_ref, wi_ref, lam_ref, sav_a_ref, sav_ref, h_ref, hin_ref, cgf_ref) = refs[:10]
        pos = 10
        dh_ref = acc_ref = None
        iwr_ref = iwi_ref = ivec_ref = None
        if has_dh:
            dh_ref = refs[pos]
            pos += 1
        if has_acc:
            acc_ref = refs[pos]
            pos += 1
        if has_init:
            iwr_ref, iwi_ref, ivec_ref = refs[pos:pos + 3]
            pos += 3
        dx_ref, dwr_ref, dwi_ref, vec_ref, cg0_ref, carry = refs[pos:]
        i = pl.program_id(0)

        @pl.when(i == 0)
        def _():
            carry[...] = cgf_ref[...]
            if has_init:
                dwr_ref[...] = iwr_ref[...]
                dwi_ref[...] = iwi_ref[...]
                vec_ref[...] = ivec_ref[...]
            else:
                dwr_ref[...] = jnp.zeros_like(dwr_ref)
                dwi_ref[...] = jnp.zeros_like(dwi_ref)
                vec_ref[...] = jnp.zeros_like(vec_ref)

        for g in range(NH):
            cols = pl.ds(g * DH, DH)
            cwv = cw_ref[:, cols]
            lam_v = lam_ref[:, cols]
            taps = _conv_taps(x_ref[:, cols])
            wr_g, wi_g = wr_ref[g], wi_ref[g]
            a = sav_a_ref[:, cols]
            xcb = sav_ref[0, :, cols]
            xc, r, gi, mult = (sav_ref[slot, :, cols].astype(F32) for slot in range(LRU_SAVED))
            sp = jnp.maximum(-lam_v, 0.0) + jnp.log(1.0 + jnp.exp(-jnp.abs(lam_v)))
            hprev = _shift(h_ref[:, cols], 1, hin_ref[0, :, cols], down)
            a_next = _shift(a, 1, 1.0, not down)
            dhv = dh_ref[:, cols] if has_dh else jnp.zeros_like(a)
            e, _ = _chain_scan(a_next, dhv, carry[:, cols], not down)
            cg = a[first:first + 1, :] * e[first:first + 1, :]
            carry[:, cols] = cg
            cg0_ref[:, cols] = cg
            da = e * hprev
            emult = e * mult
            dgi = emult * xc
            dxc = emult * gi
            dla = da * a - (e * gi * xc) * (a * a) / mult
            dr = dla * (-RG_C * sp)
            sneg = 1.0 - _sigmoid(lam_v)
            dpr = dr * r * (1.0 - r)
            dpi = dgi * gi * (1.0 - gi)
            dprb, dpib = dpr.astype(BF16), dpi.astype(BF16)
            dxc = dxc + _dot_nt(dprb, wr_g) + _dot_nt(dpib, wi_g)
            dwr_ref[g] += _dot_tn(xcb, dprb)
            dwi_ref[g] += _dot_tn(xcb, dpib)
            dx = (cwv[0:1, :] * _shift(dxc, 1, 0.0, False) + cwv[1:2, :] * dxc
                  + cwv[2:3, :] * _shift(dxc, 1, 0.0, True) + cwv[3:4, :] * _shift(dxc, 2, 0.0, True))
            if has_acc:
                dx = dx + acc_ref[:, cols]
            dx_ref[:, cols] = dx.astype(dx_dtype)
            vec_ref[0:1, cols] += jnp.sum(dpr, axis=0, keepdims=True)
            vec_ref[1:2, cols] += jnp.sum(dpi, axis=0, keepdims=True)
            vec_ref[2:3, cols] += jnp.sum(dla * r, axis=0, keepdims=True) * (RG_C * sneg)
            vec_ref[3:4, cols] += jnp.sum(dxc, axis=0, keepdims=True)
            for kk in range(4):
                vec_ref[4 + kk:5 + kk, cols] += jnp.sum(dxc * taps[kk], axis=0, keepdims=True)

    vec = pl.BlockSpec((1, D), lambda i: (0, 0))
    wsp = pl.BlockSpec((NH, DH, DH), lambda i: (0, 0, 0))
    tok = pl.BlockSpec((blk, D), lambda i: (bmap(i), 0))
    vec16 = pl.BlockSpec((16, D), lambda i: (0, 0))
    in_specs = [tok, pl.BlockSpec((4, D), lambda i: (0, 0)), wsp, wsp, vec, tok,
                pl.BlockSpec((LRU_SAVED, blk, D), lambda i: (0, bmap(i), 0)), tok,
                pl.BlockSpec((1, 1, D), lambda i: (bmap(i), 0, 0)), vec]
    args = [xin, cw, wr, wi, lam, sav[0], sav[1], h, hin, cg_fin]
    if has_dh:
        in_specs.append(tok)
        args.append(dh)
    if has_acc:
        in_specs.append(tok)
        args.append(acc_dx)
    if has_init:
        in_specs += [wsp, wsp, vec16]
        args += list(init)
    return pl.pallas_call(
        body, name=name, grid=(nb,),
        out_shape=[jax.ShapeDtypeStruct((n, D), dx_dtype), jax.ShapeDtypeStruct((NH, DH, DH), F32),
                   jax.ShapeDtypeStruct((NH, DH, DH), F32), jax.ShapeDtypeStruct((16, D), F32),
                   jax.ShapeDtypeStruct((1, D), F32)],
        in_specs=in_specs, out_specs=[tok, wsp, wsp, vec16, vec],
        scratch_shapes=[pltpu.VMEM((1, D), F32)],
    )(*args)


def _merge(z, o_f, o_b, hx, xin, tgt, mod, gn, p_a, p_b, w_out, ln_g, ln_b):
    n = xin.shape[0]
    tm = _row_tile(n, 128)

    def body(z4_ref, z6_ref, z7_ref, z8_ref, of_ref, ob_ref, hx_ref, x_ref, t_ref, mod_ref, gn_ref,
             pa_ref, pb_ref, wo_ref, lg_ref, lnb_ref,
             dr_ref, do_ref, dhx_ref, dz_ref, oa_o, obb_o, y_o, dya_o, dyb_o, dout_o, vec_ref):
        @pl.when(pl.program_id(0) == 0)
        def _():
            vec_ref[...] = jnp.zeros_like(vec_ref)

        gt = mod_ref[0:1, 2 * D:3 * D]
        gnv = gn_ref[...]
        o = of_ref[...] + ob_ref[...]
        rs = jnp.concatenate(
            [jnp.broadcast_to(lax.rsqrt(jnp.mean(jnp.square(o[:, h * DH:(h + 1) * DH]), axis=1, keepdims=True)
                                        + RMS_EPS), (tm, DH)) for h in range(NH)], axis=1)
        nrm = o * rs
        rn = nrm * gnv
        z4, z6, z7, z8 = z4_ref[...], z6_ref[...], z7_ref[...], z8_ref[...]
        s4, s6, s7, s8 = _sigmoid(z4), _sigmoid(z6), _sigmoid(z7), _sigmoid(z8)
        sg4, sg6 = z4 * s4, z6 * s6
        hxv = hx_ref[...]
        oa = (rn * sg4).astype(BF16)
        obb = (hxv * sg6).astype(BF16)
        ya = _dot(oa, pa_ref[...])
        yb = _dot(obb, pb_ref[...])
        y = (s7 * ya + s8 * yb).astype(BF16)
        out = _dot(y, wo_ref[...])
        xv = x_ref[...]
        rr = ALPHA * xv + gt * out
        mu = jnp.mean(rr, axis=1, keepdims=True)
        cen = rr - mu
        rstd = lax.rsqrt(jnp.mean(cen * cen, axis=1, keepdims=True) + LN_EPS)
        xhat = cen * rstd
        lg = lg_ref[...]
        err = xhat * lg + lnb_ref[...] - t_ref[...]
        loss_rows = jnp.sum(err * err, axis=1, keepdims=True)
        dxn = err * (1.0 / D)
        dxh = dxn * lg
        dr = rstd * (dxh - jnp.mean(dxh, axis=1, keepdims=True)
                     - xhat * jnp.mean(dxh * xhat, axis=1, keepdims=True))
        dout = (dr * gt).astype(BF16)
        dy = _dot_nt(dout, wo_ref[...])
        dya = (dy * s7).astype(BF16)
        dyb = (dy * s8).astype(BF16)
        doa = _dot_nt(dya, pa_ref[...])
        dob = _dot_nt(dyb, pb_ref[...])
        drn = doa * sg4
        dn = drn * gnv
        dnn = dn * nrm
        corr = jnp.concatenate(
            [jnp.broadcast_to(jnp.mean(dnn[:, h * DH:(h + 1) * DH], axis=1, keepdims=True), (tm, DH))
             for h in range(NH)], axis=1)
        dr_ref[...] = dr
        do_ref[...] = rs * (dn - nrm * corr)
        dhx_ref[...] = dob * sg6
        dz_ref[:, 0:4 * D] = jnp.zeros((tm, 4 * D), BF16)
        dz_ref[:, 4 * D:5 * D] = (doa * rn * _dsilu(z4, s4)).astype(BF16)
        dz_ref[:, 5 * D:6 * D] = jnp.zeros((tm, D), BF16)
        dz_ref[:, 6 * D:7 * D] = (dob * hxv * _dsilu(z6, s6)).astype(BF16)
        dz_ref[:, 7 * D:8 * D] = (dy * ya * s7 * (1.0 - s7)).astype(BF16)
        dz_ref[:, 8 * D:9 * D] = (dy * yb * s8 * (1.0 - s8)).astype(BF16)
        oa_o[...] = oa
        obb_o[...] = obb
        y_o[...] = y
        dya_o[...] = dya
        dyb_o[...] = dyb
        dout_o[...] = dout
        vec_ref[0:1, :] += jnp.sum(dr * out, axis=0, keepdims=True)
        vec_ref[1:2, :] += jnp.sum(dxn * xhat, axis=0, keepdims=True)
        vec_ref[2:3, :] += jnp.sum(dxn, axis=0, keepdims=True)
        vec_ref[3:4, :] += jnp.sum(drn * nrm, axis=0, keepdims=True)
        vec_ref[4:5, :] += jnp.broadcast_to(jnp.sum(loss_rows, axis=0, keepdims=True) * (0.5 / D), (1, D))

    def grp(g):
        return pl.BlockSpec((tm, D), lambda i: (i, g))

    tok = pl.BlockSpec((tm, D), lambda i: (i, 0))
    vec = pl.BlockSpec((1, D), lambda i: (0, 0))
    wsp = pl.BlockSpec((D, D), lambda i: (0, 0))
    return pl.pallas_call(
        body, name="merge", grid=(n // tm,),
        out_shape=[jax.ShapeDtypeStruct((n, D), F32)] * 3
        + [jax.ShapeDtypeStruct((n, NGRP * D), BF16)]
        + [jax.ShapeDtypeStruct((n, D), BF16)] * 6 + [jax.ShapeDtypeStruct((8, D), F32)],
        in_specs=[grp(4), grp(6), grp(7), grp(8), tok, tok, tok, tok, tok,
                  pl.BlockSpec((8, 3 * D), lambda i: (0, 0)), vec, wsp, wsp, wsp, vec, vec],
        out_specs=[tok, tok, tok, pl.BlockSpec((tm, NGRP * D), lambda i: (i, 0))] + [tok] * 6
        + [pl.BlockSpec((8, D), lambda i: (0, 0))],
    )(z, z, z, z, o_f, o_b, hx, xin, tgt, mod, gn, p_a, p_b, w_out, ln_g, ln_b)


def _wmod_grad(c_t, cctx_t, dmx_loc, dmc_loc, name):
    n = dmx_loc.shape[1]

    def body(ct_ref, cc_ref, dmx_ref, dmc_ref, o_ref):
        ct = ct_ref[...]
        sct = ct * _sigmoid(ct)
        cc = cc_ref[...]
        scc = cc * _sigmoid(cc)
        dmc = dmc_ref[0:1, :]
        for b in range(1, NDEV):
            dmc = dmc + dmc_ref[b:b + 1, :]
        acc = scc * dmc
        for b in range(NDEV):
            acc = acc + sct[:, b:b + 1] * dmx_ref[b:b + 1, :]
        o_ref[...] = acc

    return pl.pallas_call(body, name=name, out_shape=jax.ShapeDtypeStruct((D, n), F32))(c_t, cctx_t, dmx_loc, dmc_loc)


PACK_ROWS = 40


def _finalize_small(g_pack, lb, w_mod_full, params):
    npar = len(params)

    def body(*refs):
        gp_ref, lb_ref, wm_ref = refs[:3]
        wmv = refs[3:3 + 3 * npar]
        loss_ref = refs[3 + 3 * npar]
        g_refs = refs[4 + 3 * npar:4 + 4 * npar]
        upd = refs[4 + 4 * npar:4 + 7 * npar]
        tot = refs[-1]
        acc = gp_ref[0]
        for k in range(1, NDEV):
            acc = acc + gp_ref[k]
        tot[...] = acc
        mine = pl.ds(pl.multiple_of(_my_index() * DH, DH), DH)
        (g_cctx, g_bmod, g_bin, g_lbl, g_norm, g_cw, g_cb, g_br, g_bi, g_lam, g_lng, g_lnb) = g_refs

        loss_ref[...] = jnp.broadcast_to(tot[36:37, 0:DH], (8, DH))
        for k in range(3):
            g_bmod[:, k * D:(k + 1) * D] = tot[k:k + 1, :] + tot[3 + k:4 + k, :]
        dmc = jnp.concatenate([tot[3:4, :], tot[4:5, :], tot[5:6, :]], axis=1)
        cv = wmv[0][...]
        proj = _dot_nt(jnp.broadcast_to(dmc, (8, 3 * D)).astype(BF16), wm_ref[...])
        g_cctx[...] = proj[0:1, :] * _dsilu(cv, _sigmoid(cv))
        for k in range(NGRP):
            g_bin[:, k * D:(k + 1) * D] = tot[6 + k:7 + k, :]
        nrm = tot[15:16, 0:DH]
        for h in range(1, NH):
            nrm = nrm + tot[15:16, h * DH:(h + 1) * DH]
        g_norm[...] = nrm
        g_lng[...] = tot[16:17, :]
        g_lnb[...] = tot[17:18, :]
        g_cb[...] = tot[21:22, :] + tot[29:30, :]
        g_cw[0] = tot[22:26, mine] + tot[30:34, mine]
        for ref, row in ((g_br, 18), (g_bi, 19), (g_lam, 20)):
            ref[0, 0:1, :] = tot[row:row + 1, mine]
            ref[0, 1:2, :] = tot[row + 8:row + 9, mine]
        lbl = lb_ref[0:2, mine]
        dl0 = tot[34:36, mine] * lbl * (1.0 - lbl)
        g_lbl[0] = dl0
        g_lbl[1] = -dl0
        for p in range(npar):
            d, mm, vv = _adam_math(g_refs[p][...], wmv[3 * p][...], wmv[3 * p + 1][...], wmv[3 * p + 2][...])
            upd[3 * p][...] = d
            upd[3 * p + 1][...] = mm
            upd[3 * p + 2][...] = vv

    flat = [t for wmv in params for t in wmv]
    shapes = [jax.ShapeDtypeStruct(wmv[0].shape, F32) for wmv in params]
    outs = pl.pallas_call(
        body, name="finalize_small",
        out_shape=[jax.ShapeDtypeStruct((8, DH), F32)] + shapes + [s for s in shapes for _ in range(3)],
        scratch_shapes=[pltpu.VMEM((PACK_ROWS, D), F32)],
    )(g_pack, lb, w_mod_full, *flat)
    grads = list(outs[1:1 + npar])
    upd = [tuple(outs[1 + npar + 3 * p:4 + npar + 3 * p]) for p in range(npar)]
    return outs[0], grads, upd


def _to_colmajor(t, rows):
    return t.reshape(rows, GRID_W, D).transpose(1, 0, 2).reshape(rows * GRID_W, D)


def _to_raster(t, rows):
    return t.reshape(GRID_W, rows, D).transpose(1, 0, 2).reshape(rows * GRID_W, D)


def _local_cols(t, me, width):
    return lax.dynamic_slice_in_dim(t, me * width, width, axis=t.ndim - 1)


def kernel(x, c, ctx, c_ctx, w_mod, b_mod, w_in, b_in, lb_logits, norm_a_g, conv_w, conv_b, w_r, b_r, w_i, b_i, lam, p_a, p_b, w_out, ln_g, ln_b, loss_target, m_c_ctx, m_w_mod, m_b_mod, m_w_in, m_b_in, m_lb_logits, m_norm_a_g, m_conv_w, m_conv_b, m_w_r, m_b_r, m_w_i, m_b_i, m_lam, m_p_a, m_p_b, m_w_out, m_ln_g, m_ln_b, v_c_ctx, v_w_mod, v_b_mod, v_w_in, v_b_in, v_lb_logits, v_norm_a_g, v_conv_w, v_conv_b, v_w_r, v_b_r, v_w_i, v_b_i, v_lam, v_p_a, v_p_b, v_w_out, v_ln_g, v_ln_b):
    me = _my_index()
    xs, cs, tgt = x[0], ctx[0], loss_target[0]
    t_len, c_len = xs.shape[0], cs.shape[0]
    rows = t_len // GRID_W
    wcols = w_in.shape[2]
    mcols = w_mod.shape[2]

    w_mod_b, w_in_b, p_a_b, p_b_b, w_out_b, w_r_b, w_i_b = _cast_bf16(
        [w_mod[0], w_in[0], p_a[0], p_b[0], w_out[0], w_r[0], w_i[0]])
    small = jnp.concatenate([lb_logits.reshape(4, DH), conv_w[0], b_r[0], b_i[0], lam[0], jnp.zeros((2, DH), F32),
                             c.reshape(8, DH)], axis=0)
    g_small, g_wmod = _all_gather_two_level([small, w_mod_b], "gather_params")

    def full_rows(lo, hi):
        return g_small[:, lo:hi, :].transpose(1, 0, 2).reshape(hi - lo, D)

    lbl_f, cw_f, br_f, bi_f, lam_f = full_rows(0, 4), full_rows(4, 8), full_rows(8, 10), full_rows(10, 12), full_rows(12, 14)
    c_all = g_small[:, 16:24, :].reshape(NDEV, D)
    w_mod_f = g_wmod.transpose(1, 0, 2).reshape(D, 3 * D)

    cc = jnp.concatenate([c.reshape(1, D), c_ctx.reshape(1, D), jnp.zeros((6, D), F32)], axis=0)
    lbl_p = jnp.concatenate([lbl_f.reshape(2, 2, D), jnp.zeros((2, 6, D), F32)], axis=1)
    mod, lb = _prep(cc, w_mod_f, b_mod, lbl_p)
    u_x = _modulate(xs, mod, 0, "modulate_x")
    u_c = _modulate(cs, mod, 1, "modulate_c")
    z_x, w_in_f = _inproj_gather(u_x, w_in_b, b_in, "inproj_gather")
    z_c = _mm_bias(u_c, w_in_f, b_in, "inproj_c")

    zero_s = jnp.zeros((NH, DH, DH), F32)
    zero_v = jnp.zeros((1, D), F32)
    gla = {}
    out_w = [p_a_b, p_b_b, w_out_b]
    out_w_f = []
    for d in (0, 1):
        _, ssc, sfc = _gla_fwd(z_c, lb, zero_s, d, f"gla_fwd_c{d}")
        o_d, ssx, _, *gathered = _gla_fwd(z_x, lb, sfc, d, f"gla_fwd_x{d}", side=out_w[1:] if d else out_w[:1])
        out_w_f += [t.reshape(D, D) for t in gathered]
        gla[d] = (ssc, ssx, o_d)
    p_a_f, p_b_f, w_out_f = out_w_f

    x5_c = z_c[:, 5 * D:6 * D]
    x5_x = _to_colmajor(z_x[:, 5 * D:6 * D], rows)
    cb2 = conv_b.reshape(1, D)
    lru = {}
    h_sum = None
    for d in (0, 1):
        prm = (cw_f, cb2, w_r_b[d], br_f[d:d + 1], w_i_b[d], bi_f[d:d + 1], lam_f[d:d + 1])
        h_c, hin_c, hfin_c, *sav_c = _lru_fwd(x5_c, c_len, *prm, zero_v, None, d, f"lru_fwd_c{d}")
        h_x, hin_x, _, sav_a, sav_h, *h_sum = _lru_fwd(x5_x, rows, *prm, hfin_c, lru[0][3] if d else None, d,
                                                       f"lru_fwd_x{d}")
        lru[d] = ((cw_f, w_r_b[d], w_i_b[d], lam_f[d:d + 1]), h_c, hin_c, h_x, hin_x, tuple(sav_c), (sav_a, sav_h))
    hx = _to_raster(h_sum[0], rows)

    gn = jnp.tile(norm_a_g.reshape(1, DH), (1, NH))
    (dr, do, dhx, dz_m, oa, obb, yb16, dya, dyb, dout, mvec) = _merge(
        z_x, gla[0][2], gla[1][2], hx, xs, tgt, mod, gn, p_a_f, p_b_f, w_out_f, ln_g, ln_b)

    dhx_cm = _to_colmajor(dhx, rows)
    lru_dx_x = lru_dx_c = None
    for d in (0, 1):
        prm, h_c, hin_c, h_x, hin_x, sav_c, sav_x = lru[d]
        dx_dtype = BF16 if d else F32
        lru_dx_x, dwr, dwi, lvec, cg0 = _lru_bwd(x5_x, rows, *prm, sav_x, h_x, hin_x, dhx_cm, zero_v, lru_dx_x, None,
                                                 d, f"lru_bwd_x{d}", dx_dtype)
        lru_dx_c, dwr, dwi, lvec, _ = _lru_bwd(x5_c, c_len, *prm, sav_c, h_c, hin_c, None, cg0, lru_dx_c,
                                               (dwr, dwi, lvec), d, f"lru_bwd_c{d}", dx_dtype)
        lru[d] = (dwr, dwi, lvec)
    dz5_x = _to_raster(lru_dx_x, rows)
    dz5_c = lru_dx_c

    dpa = _mm_tn(oa, dya, None, "dpa", out_dtype=BF16)
    dpb = _mm_tn(obb, dyb, None, "dpb", out_dtype=BF16)
    dwo = _mm_tn(yb16, dout, None, "dwout", out_dtype=BF16)
    wr_pack = jnp.concatenate([lru[0][0], lru[1][0], lru[0][1], lru[1][1]], axis=0).reshape(4 * NH * DH, DH)

    gq_c = gv_c = None
    dzf_c, dlb = {}, {}
    gq_x, dzf_x0, gv_x, dlb_x, ds0, r_pa, r_pb, r_wo, r_wri = _gla_bwd(
        z_x, lb, gla[0][1], do, zero_s, None, None, 0, "gla_bwd_x0", f_dtype=BF16,
        side=[dpa, dpb, dwo, wr_pack], side_splits=[0, 0, 0, 0])
    gq_c, dzf_c[0], gv_c, dlb_c, _ = _gla_bwd(z_c, lb, gla[0][0], None, ds0, None, None, 0, "gla_bwd_c0")
    dlb[0] = dlb_x[0:1] + dlb_c[0:1]
    dz_g, dlb_x, ds0 = _gla_bwd(z_x, lb, gla[1][1], do, zero_s, gq_x, gv_x, 1, "gla_bwd_x1", into=(dz_m, dzf_x0))
    gq_c, dzf_c[1], gv_c, dlb_c, _ = _gla_bwd(z_c, lb, gla[1][0], None, ds0, gq_c, gv_c, 1, "gla_bwd_c1")
    dlb[1] = dlb_x[0:1] + dlb_c[0:1]

    bf = lambda t: t.astype(BF16)
    dz_x = lax.dynamic_update_slice(dz_g, dz5_x, (0, 5 * D))
    zc0 = jnp.zeros((c_len, D), BF16)
    dz_c = jnp.concatenate([bf(gq_c), bf(dzf_c[0]), bf(dzf_c[1]), bf(gv_c), zc0, dz5_c, zc0, zc0, zc0], axis=1)
    dwin_c, dbin_c = _mm_tn(u_c, dz_c, None, "dwin_c", with_colsum=True)

    grad_x, xvec = _input_grad(dz_x, w_in_f, xs, dr, mod, 0, "input_grad_x")
    r_win, dbin = _dwin_exchange(u_x, dz_x, dwin_c, dbin_c, "dwin_exchange")
    _, cvec = _input_grad(dz_c, w_in_f, cs, None, mod, 1, "input_grad_c")
    wri_piece = _sum_rows(r_wri, "sum_w_ri_piece")
    g_w_in, d_w_in, nm_w_in, nv_w_in = _sum_adamw(r_win, w_in, m_w_in, v_w_in, "update_w_in")
    g_p_a, d_p_a, nm_p_a, nv_p_a = _sum_adamw(r_pa, p_a, m_p_a, v_p_a, "update_p_a")
    g_p_b, d_p_b, nm_p_b, nv_p_b = _sum_adamw(r_pb, p_b, m_p_b, v_p_b, "update_p_b")
    g_w_out, d_w_out, nm_w_out, nv_w_out = _sum_adamw(r_wo, w_out, m_w_out, v_w_out, "update_w_out")

    dlb_rows = jnp.concatenate([dlb[0], dlb[1]], axis=0)
    pack = jnp.concatenate([
        xvec[0:1], xvec[1:2], mvec[0:1],
        cvec[0:1], cvec[1:2], jnp.zeros((1, D), F32),
        dbin.reshape(NGRP, D),
        mvec[3:4], mvec[1:2], mvec[2:3],
        lru[0][2][0:8], lru[1][2][0:3],
        lru[1][2][3:8],
        dlb_rows,
        mvec[4:5],
        jnp.zeros((3, D), F32)], axis=0)
    g_pack, g_wri = _all_gather_two_level([pack, wri_piece], "gather_small_grads")

    dmx = g_pack[:, 0:3, :].reshape(NDEV, 3 * D)
    dmc = g_pack[:, 3:6, :].reshape(NDEV, 3 * D)
    grad_w_mod = _wmod_grad(c_all.T, c_ctx.reshape(D, 1), _local_cols(dmx, me, mcols), _local_cols(dmc, me, mcols),
                            "grad_w_mod").reshape(1, D, mcols)
    small_params = [(c_ctx.reshape(1, D), m_c_ctx.reshape(1, D), v_c_ctx.reshape(1, D)), (b_mod, m_b_mod, v_b_mod),
                    (b_in, m_b_in, v_b_in), (lb_logits, m_lb_logits, v_lb_logits), (norm_a_g, m_norm_a_g, v_norm_a_g),
                    (conv_w, m_conv_w, v_conv_w), (conv_b, m_conv_b, v_conv_b), (b_r, m_b_r, v_b_r),
                    (b_i, m_b_i, v_b_i), (lam, m_lam, v_lam), (ln_g, m_ln_g, v_ln_g), (ln_b, m_ln_b, v_ln_b)]
    loss_tile, small_g, small_upd = _finalize_small(g_pack, lb, w_mod_f, small_params)
    loss = loss_tile[0, 0]
    (grad_c_ctx, grad_b_mod, grad_b_in, grad_lb_logits, grad_norm_a_g, grad_conv_w, grad_conv_b, grad_b_r, grad_b_i,
     grad_lam, grad_ln_g, grad_ln_b) = small_g
    small_upd[0] = tuple(t.reshape(c_ctx.shape) for t in small_upd[0])
    (o_c_ctx, o_b_mod, o_b_in, o_lb, o_norm, o_conv_w, o_conv_b, o_b_r, o_b_i, o_lam, o_ln_g, o_ln_b) = small_upd

    half = 2 * NH * DH
    g_ri = g_wri.reshape(2 * half, DH)
    grad_w_r, grad_w_i = g_ri[:half].reshape(w_r.shape), g_ri[half:].reshape(w_i.shape)
    d_w_r, nm_w_r, nv_w_r = _adamw(grad_w_r, w_r, m_w_r, v_w_r, "update_w_r")
    d_w_i, nm_w_i, nv_w_i = _adamw(grad_w_i, w_i, m_w_i, v_w_i, "update_w_i")

    d_w_mod, nm_w_mod, nv_w_mod = _adamw(grad_w_mod, w_mod, m_w_mod, v_w_mod, "update_w_mod")

    grads = [grad_c_ctx.reshape(c_ctx.shape), grad_w_mod, grad_b_mod, g_w_in, grad_b_in, grad_lb_logits, grad_norm_a_g,
             grad_conv_w, grad_conv_b, grad_w_r, grad_b_r, grad_w_i, grad_b_i, grad_lam, g_p_a, g_p_b, g_w_out,
             grad_ln_g, grad_ln_b]
    per_kind = []
    for k in range(3):
        per_kind.append([
            o_c_ctx[k], (d_w_mod, nm_w_mod, nv_w_mod)[k], o_b_mod[k], (d_w_in, nm_w_in, nv_w_in)[k], o_b_in[k], o_lb[k],
            o_norm[k], o_conv_w[k], o_conv_b[k], (d_w_r, nm_w_r, nv_w_r)[k], o_b_r[k], (d_w_i, nm_w_i, nv_w_i)[k],
            o_b_i[k], o_lam[k], (d_p_a, nm_p_a, nv_p_a)[k], (d_p_b, nm_p_b, nv_p_b)[k], (d_w_out, nm_w_out, nv_w_out)[k],
            o_ln_g[k], o_ln_b[k]])
    return (loss, grad_x.reshape(x.shape), *grads, *per_kind[0], *per_kind[1], *per_kind[2])
```

```python
import functools

import jax
import jax.numpy as jnp
from jax import lax
from jax.experimental import pallas as pl
from jax.experimental.pallas import tpu as pltpu

F32 = jnp.float32
BF16 = jnp.bfloat16

D = 1024
NH = 8
DH = 128
CHUNK = 64
GLA_HEADS_PER_STEP = 8
GRID_W = 64
NGRP = 9
NDEV = 8
RG_C = 8.0
ALPHA = 2.0 ** 0.25
LN_EPS = 1e-5
RMS_EPS = 1e-6
Q_SCALE = DH ** -0.5
ADAM_LR, ADAM_B1, ADAM_B2, ADAM_EPS, ADAM_WD, ADAM_STEP = 1e-3, 0.9, 0.999, 1e-8, 0.01, 10
ADAM_C1 = 1.0 / (1.0 - ADAM_B1 ** ADAM_STEP)
ADAM_C2 = 1.0 / (1.0 - ADAM_B2 ** ADAM_STEP)

ANY = pl.BlockSpec(memory_space=pl.ANY)


def _sigmoid(t):
    return 1.0 / (1.0 + jnp.exp(-t))


def _dsilu(t, s):
    return s * (1.0 + t * (1.0 - s))


def _dot(a, b):
    return jnp.dot(a, b, preferred_element_type=F32)


def _dot_nt(a, b):
    return lax.dot_general(a, b, (((1,), (1,)), ((), ())), preferred_element_type=F32)


def _dot_tn(a, b):
    return lax.dot_general(a, b, (((0,), (0,)), ((), ())), preferred_element_type=F32)


def _my_index():
    return 4 * lax.axis_index("x") + 2 * lax.axis_index("y") + lax.axis_index("c")


def _dev_tuple(j):
    return (j >> 2, (j >> 1) & 1, j & 1)


def _exchange_sems(n):
    return [pltpu.SemaphoreType.DMA((n * NDEV,)), pltpu.SemaphoreType.DMA((n * NDEV,)), pltpu.SemaphoreType.DMA((n,))]


def _exchange(ins, outs, sems, piece_of=None):
    send_sems, recv_sems, loc_sems = sems
    n = len(ins)

    def src(a, p):
        return ins[a] if piece_of is None else piece_of(ins[a], a, p)

    def push(a, t):
        me, p = _my_index(), _step_peer(t)
        return pltpu.make_async_remote_copy(
            src_ref=src(a, p), dst_ref=outs[a].at[me],
            send_sem=send_sems.at[a * NDEV + t], recv_sem=recv_sems.at[a * NDEV + me],
            device_id=_dev_of(p), device_id_type=pl.DeviceIdType.MESH)

    def local(a):
        me = _my_index()
        return pltpu.make_async_copy(src(a, me), outs[a].at[me], loc_sems.at[a])

    def start():
        for a in range(n):
            local(a).start()
        for t in range(NDEV - 1):
            for a in range(n):
                push(a, t).start()

    def finish():
        me = _my_index()
        for t in range(NDEV - 1):
            for a in range(n):
                push(a, t).wait_send()
        for j in range(NDEV):
            @pl.when(me != j)
            def _():
                for a in range(n):
                    pltpu.make_async_remote_copy(
                        src_ref=src(a, j), dst_ref=outs[a].at[j],
                        send_sem=send_sems.at[a * NDEV], recv_sem=recv_sems.at[a * NDEV + j],
                        device_id=_dev_tuple(j), device_id_type=pl.DeviceIdType.MESH).wait_recv()
        for a in range(n):
            local(a).wait()

    return start, finish


def _all_gather(shards, name):
    n = len(shards)

    def body(*refs):
        start, finish = _exchange(refs[:n], refs[n:2 * n], refs[2 * n:])
        start()
        finish()

    return pl.pallas_call(
        body, name=name,
        out_shape=[jax.ShapeDtypeStruct((NDEV,) + s.shape, s.dtype) for s in shards],
        in_specs=[ANY] * n, out_specs=[ANY] * n, scratch_shapes=_exchange_sems(n),
    )(*shards)


def _pieces(parts, splits):
    shapes = []
    for part, split in zip(parts, splits):
        r, c = part.shape
        shapes.append((r // NDEV, c) if split == 0 else (r, c // NDEV))

    def piece_of(ref, a, j):
        pr, pc = shapes[a]
        if splits[a] == 0:
            start = j * pr if isinstance(j, int) else pl.multiple_of(j * pr, pr)
            return ref.at[pl.ds(start, pr), :]
        start = j * pc if isinstance(j, int) else pl.multiple_of(j * pc, pc)
        return ref.at[:, pl.ds(start, pc)]

    return shapes, piece_of


_STEP_MASKS = ((2, 4, 6, 3, 5, 7, 1, 0), (4, 2, 6, 5, 3, 7, 1, 0))
_GATHER_MASKS = ((0, 1, 2, 4, 3, 5, 6, 7), (0, 1, 4, 2, 5, 3, 6, 7))


def _peer_schedule(table):
    tab = jnp.array(table, jnp.int32)
    return jnp.bitwise_xor(_my_index(), tab[lax.axis_index("c")])


def _step_peer(s, table=_STEP_MASKS):
    def pick(row):
        if isinstance(s, int):
            return jnp.int32(row[s])
        m = jnp.int32(row[NDEV - 1])
        for t in range(NDEV - 2, -1, -1):
            m = jnp.where(s == t, jnp.int32(row[t]), m)
        return m
    mask = jnp.where(lax.axis_index("c") == 0, pick(table[0]), pick(table[1]))
    return jnp.bitwise_xor(_my_index(), mask)


def _dev_of(p):
    return (p // 4, (p // 2) % 2, p % 2)


def _all_gather_two_level(shards, name):
    n = len(shards)
    direct = (1, 2, 3, 6)
    relay_sem = {2: 4, 3: 5, 6: 7}

    def body(*refs):
        ins, outs = refs[:n], refs[n:2 * n]
        send_sems, recv_sems, loc_sems = refs[2 * n:]
        me = _my_index()
        sibling = _step_peer(1, _GATHER_MASKS)

        def push(a, t):
            return pltpu.make_async_remote_copy(
                src_ref=ins[a], dst_ref=outs[a].at[me], send_sem=send_sems.at[a * NDEV + t],
                recv_sem=recv_sems.at[a * NDEV + me],
                device_id=_dev_of(_step_peer(t, _GATHER_MASKS)), device_id_type=pl.DeviceIdType.MESH)

        def arrival(a, p):
            return pltpu.make_async_remote_copy(
                src_ref=ins[a], dst_ref=outs[a].at[p], send_sem=send_sems.at[a * NDEV],
                recv_sem=recv_sems.at[a * NDEV + p], device_id=_dev_of(p), device_id_type=pl.DeviceIdType.MESH)

        def relay(a, t):
            p = _step_peer(t, _GATHER_MASKS)
            return pltpu.make_async_remote_copy(
                src_ref=outs[a].at[p], dst_ref=outs[a].at[p], send_sem=send_sems.at[a * NDEV + relay_sem[t]],
                recv_sem=recv_sems.at[a * NDEV + p], device_id=_dev_of(sibling), device_id_type=pl.DeviceIdType.MESH)

        for a in range(n):
            pltpu.make_async_copy(ins[a], outs[a].at[me], loc_sems.at[a]).start()
        for t in direct:
            for a in range(n):
                push(a, t).start()
        for t in relay_sem:
            for a in range(n):
                arrival(a, _step_peer(t, _GATHER_MASKS)).wait_recv()
                relay(a, t).start()
        for t in (1, 4, 5, 7):
            for a in range(n):
                arrival(a, _step_peer(t, _GATHER_MASKS)).wait_recv()
        for a in range(n):
            for t in direct:
                push(a, t).wait_send()
            for t in relay_sem:
                relay(a, t).wait_send()
            pltpu.make_async_copy(ins[a], outs[a].at[me], loc_sems.at[a]).wait()

    return pl.pallas_call(
        body, name=name,
        out_shape=[jax.ShapeDtypeStruct((NDEV,) + s.shape, s.dtype) for s in shards],
        in_specs=[ANY] * n, out_specs=[ANY] * n, scratch_shapes=_exchange_sems(n),
    )(*shards)


def _dwin_exchange(u, dz, init, cs_init, name):
    m, ka = u.shape
    n = dz.shape[1]
    pc = n // NDEV
    tk = _row_tile(m, 512)
    nk = m // tk

    def body(pidx_ref, u_ref, dz_ref, init_ref, csi_ref, rwin, cs_ref, acc, sbuf, wsend, wrecv, wloc):
        s, k = pl.program_id(0), pl.program_id(1)
        me = _my_index()

        def slab_copy(slot, p):
            return pltpu.make_async_remote_copy(
                src_ref=sbuf.at[slot], dst_ref=rwin.at[me], send_sem=wsend.at[slot], recv_sem=wrecv.at[me],
                device_id=_dev_of(p), device_id_type=pl.DeviceIdType.MESH)

        @pl.when(k == 0)
        def _():
            acc[...] = init_ref[...]
            cs_ref[...] = csi_ref[...]

        bv = dz_ref[...]
        acc[...] += _dot_tn(u_ref[...], bv)
        cs_ref[...] += jnp.sum(bv.astype(F32), axis=0, keepdims=True)

        @pl.when(k == nk - 1)
        def _():
            slot = s % 2

            @pl.when(s >= 2)
            def _():
                slab_copy(slot, me).wait_send()

            sbuf[slot] = acc[...].astype(BF16)

            @pl.when(s < NDEV - 1)
            def _():
                slab_copy(slot, _step_peer(s)).start()

            @pl.when(s == NDEV - 1)
            def _():
                own = pltpu.make_async_copy(sbuf.at[slot], rwin.at[me], wloc.at[0])
                own.start()
                slab_copy(1 - slot, me).wait_send()
                for j in range(NDEV):
                    @pl.when(me != j)
                    def _():
                        pltpu.make_async_remote_copy(
                            src_ref=sbuf.at[0], dst_ref=rwin.at[j], send_sem=wsend.at[0], recv_sem=wrecv.at[j],
                            device_id=_dev_tuple(j), device_id_type=pl.DeviceIdType.MESH).wait_recv()
                own.wait()

    grid_spec = pltpu.PrefetchScalarGridSpec(
        num_scalar_prefetch=1, grid=(NDEV, nk),
        in_specs=[pl.BlockSpec((tk, ka), lambda s, k, pidx: (k, 0)),
                  pl.BlockSpec((tk, pc), lambda s, k, pidx: (k, pidx[s])),
                  pl.BlockSpec((ka, pc), lambda s, k, pidx: (0, pidx[s])),
                  pl.BlockSpec((1, pc), lambda s, k, pidx: (0, pidx[s]))],
        out_specs=[ANY, pl.BlockSpec((1, pc), lambda s, k, pidx: (0, pidx[s]))],
        scratch_shapes=[pltpu.VMEM((ka, pc), F32), pltpu.VMEM((2, ka, pc), BF16),
                        pltpu.SemaphoreType.DMA((2,)), pltpu.SemaphoreType.DMA((NDEV,)), pltpu.SemaphoreType.DMA((1,))])
    return pl.pallas_call(
        body, name=name, grid_spec=grid_spec,
        out_shape=[jax.ShapeDtypeStruct((NDEV, ka, pc), BF16), jax.ShapeDtypeStruct((1, n), F32)],
    )(_peer_schedule(_STEP_MASKS), u, dz, init, cs_init)


def _inproj_gather(u, w_loc, bias, name):
    m, k = u.shape
    pc = w_loc.shape[1]
    n = pc * NDEV
    tm = _row_tile(m, 512)
    ni = m // tm
    direct = (1, 2, 3, 6)
    relay_sem = {2: 4, 3: 5, 6: 7}

    def body(pidx_ref, u_ref, b_ref, wl_ref, z_ref, wall, wbuf, wsend, wrecv, ldsem, ownsem):
        s, i = pl.program_id(0), pl.program_id(1)
        me = _my_index()

        def shard_push(t):
            return pltpu.make_async_remote_copy(
                src_ref=wl_ref, dst_ref=wall.at[me], send_sem=wsend.at[t], recv_sem=wrecv.at[me],
                device_id=_dev_of(_step_peer(t, _GATHER_MASKS)), device_id_type=pl.DeviceIdType.MESH)

        def relay(t):
            p = _step_peer(t, _GATHER_MASKS)
            return pltpu.make_async_remote_copy(
                src_ref=wall.at[p], dst_ref=wall.at[p], send_sem=wsend.at[relay_sem[t]], recv_sem=wrecv.at[p],
                device_id=_dev_of(_step_peer(1, _GATHER_MASKS)), device_id_type=pl.DeviceIdType.MESH)

        def load(slot, src):
            return pltpu.make_async_copy(src, wbuf.at[slot], ldsem.at[slot])

        own = pltpu.make_async_copy(wl_ref, wall.at[me], ownsem.at[0])

        @pl.when((s == 0) & (i == 0))
        def _():
            own.start()
            load(0, wl_ref).start()
            for t in direct:
                shard_push(t).start()

        @pl.when((i == ni // 2) & (s < NDEV - 1))
        def _():
            nxt = _step_peer(s + 1, _GATHER_MASKS)
            pltpu.make_async_remote_copy(
                src_ref=wl_ref, dst_ref=wall.at[nxt], send_sem=wsend.at[0], recv_sem=wrecv.at[nxt],
                device_id=_dev_of(nxt), device_id_type=pl.DeviceIdType.MESH).wait_recv()
            for t in relay_sem:
                @pl.when(s + 1 == t)
                def _():
                    relay(t).start()
            load((s + 1) % 2, wall.at[nxt]).start()

        @pl.when(i == 0)
        def _():
            load(s % 2, wl_ref).wait()

        z_ref[...] = _dot(u_ref[...], wbuf[s % 2]) + b_ref[...]

        @pl.when((s == NDEV - 1) & (i == ni - 1))
        def _():
            own.wait()
            for t in direct:
                shard_push(t).wait_send()
            for t in relay_sem:
                relay(t).wait_send()

    grid_spec = pltpu.PrefetchScalarGridSpec(
        num_scalar_prefetch=1, grid=(NDEV, ni),
        in_specs=[pl.BlockSpec((tm, k), lambda s, i, pidx: (i, 0)),
                  pl.BlockSpec((1, pc), lambda s, i, pidx: (0, pidx[s])), ANY],
        out_specs=[pl.BlockSpec((tm, pc), lambda s, i, pidx: (i, pidx[s])), ANY],
        scratch_shapes=[pltpu.VMEM((2, k, pc), BF16),
                        pltpu.SemaphoreType.DMA((NDEV,)), pltpu.SemaphoreType.DMA((NDEV,)),
                        pltpu.SemaphoreType.DMA((2,)), pltpu.SemaphoreType.DMA((1,))])
    return pl.pallas_call(
        body, name=name, grid_spec=grid_spec,
        out_shape=[jax.ShapeDtypeStruct((m, n), F32), jax.ShapeDtypeStruct((NDEV, k, pc), w_loc.dtype)],
    )(_peer_schedule(_GATHER_MASKS), u, bias, w_loc)


def _adam_math(g, w, m, v):
    m2 = ADAM_B1 * m + (1.0 - ADAM_B1) * g
    v2 = ADAM_B2 * v + (1.0 - ADAM_B2) * (g * g)
    delta = -ADAM_LR * ((m2 * ADAM_C1) / (jnp.sqrt(v2 * ADAM_C2) + ADAM_EPS) + ADAM_WD * w)
    return delta, m2, v2


def _row_tile(r, cap):
    t = min(r, cap)
    while r % t:
        t //= 2
    return t


def _adamw(g, w, m, v, name):
    shape = w.shape
    cols = shape[-1] if w.ndim >= 2 and shape[-1] % 128 == 0 else 128
    g2, w2, m2, v2 = (t.reshape(-1, cols) for t in (g, w, m, v))
    r = g2.shape[0]
    tr = _row_tile(r, 256)

    def body(g_ref, w_ref, m_ref, v_ref, d_ref, mo_ref, vo_ref):
        d, mm, vv = _adam_math(g_ref[...], w_ref[...], m_ref[...], v_ref[...])
        d_ref[...] = d
        mo_ref[...] = mm
        vo_ref[...] = vv

    spec = pl.BlockSpec((tr, cols), lambda i: (i, 0))
    outs = pl.pallas_call(
        body, name=name, grid=(r // tr,),
        out_shape=[jax.ShapeDtypeStruct((r, cols), F32)] * 3,
        in_specs=[spec] * 4, out_specs=[spec] * 3,
    )(g2, w2, m2, v2)
    return tuple(o.reshape(shape) for o in outs)


def _sum_adamw(parts, w, m, v, name):
    _, r, c = parts.shape
    shape = w.shape
    w2, m2, v2 = (t.reshape(r, c) for t in (w, m, v))
    tr = _row_tile(r, 128)

    def body(p_ref, w_ref, m_ref, v_ref, g_ref, d_ref, mo_ref, vo_ref):
        g = p_ref[0].astype(F32)
        for k in range(1, NDEV):
            g = g + p_ref[k].astype(F32)
        d, mm, vv = _adam_math(g, w_ref[...], m_ref[...], v_ref[...])
        g_ref[...] = g
        d_ref[...] = d
        mo_ref[...] = mm
        vo_ref[...] = vv

    spec = pl.BlockSpec((tr, c), lambda i: (i, 0))
    outs = pl.pallas_call(
        body, name=name, grid=(r // tr,),
        out_shape=[jax.ShapeDtypeStruct((r, c), F32)] * 4,
        in_specs=[pl.BlockSpec((NDEV, tr, c), lambda i: (0, i, 0))] + [spec] * 3, out_specs=[spec] * 4,
    )(parts, w2, m2, v2)
    return tuple(o.reshape(shape) for o in outs)


def _sum_rows(parts, name):
    _, r, c = parts.shape

    def body(p_ref, o_ref):
        g = p_ref[0]
        for k in range(1, NDEV):
            g = g + p_ref[k]
        o_ref[...] = g

    return pl.pallas_call(
        body, name=name, out_shape=jax.ShapeDtypeStruct((r, c), F32),
    )(parts)


def _cast_bf16(arrays):
    n = len(arrays)

    def body(*refs):
        for src, dst in zip(refs[:n], refs[n:]):
            dst[...] = src[...].astype(BF16)

    return pl.pallas_call(
        body, name="cast_weights", out_shape=[jax.ShapeDtypeStruct(a.shape, BF16) for a in arrays],
    )(*arrays)


def _prep(cc, w_mod_full, b_mod, lbl):
    def body(cc_ref, w_ref, b_ref, l_ref, mod_ref, lb_ref):
        t = cc_ref[...]
        s = (t * _sigmoid(t)).astype(BF16)
        mod_ref[...] = _dot(s, w_ref[...]) + b_ref[...]
        lb_ref[...] = _sigmoid(l_ref[0] - l_ref[1])

    return pl.pallas_call(
        body, name="prep",
        out_shape=[jax.ShapeDtypeStruct((8, 3 * D), F32), jax.ShapeDtypeStruct((8, D), F32)],
    )(cc, w_mod_full, b_mod, lbl)


def _modulate(xin, mod, row, name):
    n = xin.shape[0]
    tm = _row_tile(n, 512)

    def body(x_ref, mod_ref, u_ref):
        sh = mod_ref[row:row + 1, 0:D]
        sc = mod_ref[row:row + 1, D:2 * D]
        u_ref[...] = (x_ref[...] * (1.0 + sc) + sh).astype(BF16)

    return pl.pallas_call(
        body, name=name, grid=(n // tm,),
        out_shape=jax.ShapeDtypeStruct((n, D), BF16),
        in_specs=[pl.BlockSpec((tm, D), lambda i: (i, 0)), pl.BlockSpec((8, 3 * D), lambda i: (0, 0))],
        out_specs=pl.BlockSpec((tm, D), lambda i: (i, 0)),
    )(xin, mod)


def _mm_bias(a, w_all, bias, name):
    m, k = a.shape
    tn = w_all.shape[2]
    n = tn * NDEV
    tm = _row_tile(m, 512)

    def body(a_ref, b_ref, bias_ref, o_ref):
        o_ref[...] = _dot(a_ref[...], b_ref[0]) + bias_ref[...]

    return pl.pallas_call(
        body, name=name, grid=(NDEV, m // tm),
        out_shape=jax.ShapeDtypeStruct((m, n), F32),
        in_specs=[pl.BlockSpec((tm, k), lambda j, i: (i, 0)), pl.BlockSpec((1, k, tn), lambda j, i: (j, 0, 0)),
                  pl.BlockSpec((1, tn), lambda j, i: (0, j))],
        out_specs=pl.BlockSpec((tm, tn), lambda j, i: (i, j)),
    )(a, w_all, bias)


def _mm_tn(a, b, init, name, with_colsum=False, colsum_init=None, out_dtype=F32):
    m, ka = a.shape
    n = b.shape[1]
    tk = _row_tile(m, 512)
    tn = 1024
    nk = m // tk
    has_init = init is not None

    def body(*refs):
        a_ref, b_ref = refs[0], refs[1]
        pos = 2
        init_ref = cs_init_ref = None
        if has_init:
            init_ref = refs[pos]
            pos += 1
            if with_colsum:
                cs_init_ref = refs[pos]
                pos += 1
        o_ref = refs[pos]
        cs_ref = refs[pos + 1] if with_colsum else None
        acc = refs[-1]
        k = pl.program_id(1)

        @pl.when(k == 0)
        def _():
            if has_init:
                acc[...] = init_ref[...]
                if with_colsum:
                    cs_ref[...] = cs_init_ref[...]
            else:
                acc[...] = jnp.zeros_like(acc)
                if with_colsum:
                    cs_ref[...] = jnp.zeros_like(cs_ref)

        bv = b_ref[...]
        acc[...] += _dot_tn(a_ref[...], bv)
        if with_colsum:
            cs_ref[...] += jnp.sum(bv.astype(F32), axis=0, keepdims=True)

        @pl.when(k == nk - 1)
        def _():
            o_ref[...] = acc[...].astype(out_dtype)

    in_specs = [pl.BlockSpec((tk, ka), lambda j, k: (k, 0)), pl.BlockSpec((tk, tn), lambda j, k: (k, j))]
    args = [a, b]
    if has_init:
        in_specs.append(pl.BlockSpec((ka, tn), lambda j, k: (0, j)))
        args.append(init)
        if with_colsum:
            in_specs.append(pl.BlockSpec((1, tn), lambda j, k: (0, j)))
            args.append(colsum_init)
    out_shape = [jax.ShapeDtypeStruct((ka, n), out_dtype)]
    out_specs = [pl.BlockSpec((ka, tn), lambda j, k: (0, j))]
    if with_colsum:
        out_shape.append(jax.ShapeDtypeStruct((1, n), F32))
        out_specs.append(pl.BlockSpec((1, tn), lambda j, k: (0, j)))
    outs = pl.pallas_call(
        body, name=name, grid=(n // tn, nk), out_shape=out_shape, in_specs=in_specs, out_specs=out_specs,
        scratch_shapes=[pltpu.VMEM((ka, tn), F32)],
    )(*args)
    return outs if with_colsum else outs[0]


def _input_grad(dz, w_all, xin, dr, mod, row, name, side=(), side_splits=()):
    m, n = dz.shape
    tm = _row_tile(m, 1024)
    tk = w_all.shape[2]
    nk = NDEV
    ni = m // tm
    has_dr = dr is not None
    ns = len(side)
    piece_shapes, piece_of = _pieces(side, side_splits)

    def body(*refs):
        dz_ref, w_ref, x_ref = refs[:3]
        pos = 3
        dr_ref = refs[pos] if has_dr else None
        pos += int(has_dr)
        mod_ref = refs[pos]
        side_in = refs[pos + 1:pos + 1 + ns]
        pos += 1 + ns
        gx_ref = refs[pos] if has_dr else None
        pos += int(has_dr)
        vec_ref = refs[pos]
        side_out = refs[pos + 1:pos + 1 + ns]
        acc = refs[pos + 1 + ns]
        i, k = pl.program_id(0), pl.program_id(1)
        if ns:
            side_start, side_finish = _exchange(side_in, side_out, refs[pos + 2 + ns:], piece_of)

            @pl.when((i == 0) & (k == 0))
            def _():
                side_start()

        @pl.when(k == 0)
        def _():
            acc[...] = jnp.zeros_like(acc)

        @pl.when((i == 0) & (k == 0))
        def _():
            vec_ref[...] = jnp.zeros_like(vec_ref)

        acc[...] += _dot_nt(dz_ref[...], w_ref[0])

        @pl.when(k == nk - 1)
        def _():
            du = acc[...]
            xv = x_ref[...]
            if has_dr:
                sc = mod_ref[row:row + 1, D:2 * D]
                gx_ref[...] = ALPHA * dr_ref[...] + du * (1.0 + sc)
            vec_ref[0:1, :] += jnp.sum(du, axis=0, keepdims=True)
            vec_ref[1:2, :] += jnp.sum(du * xv, axis=0, keepdims=True)

        if ns:
            @pl.when((i == ni - 1) & (k == nk - 1))
            def _():
                side_finish()

    row_spec = pl.BlockSpec((tm, D), lambda i, k: (i, 0))
    in_specs = [pl.BlockSpec((tm, tk), lambda i, k: (i, k)), pl.BlockSpec((1, D, tk), lambda i, k: (k, 0, 0)), row_spec]
    args = [dz, w_all, xin]
    if has_dr:
        in_specs.append(row_spec)
        args.append(dr)
    in_specs.append(pl.BlockSpec((8, 3 * D), lambda i, k: (0, 0)))
    args.append(mod)
    in_specs += [ANY] * ns
    args += list(side)
    out_shape, out_specs = [], []
    if has_dr:
        out_shape.append(jax.ShapeDtypeStruct((m, D), F32))
        out_specs.append(row_spec)
    out_shape.append(jax.ShapeDtypeStruct((8, D), F32))
    out_specs.append(pl.BlockSpec((8, D), lambda i, k: (0, 0)))
    out_shape += [jax.ShapeDtypeStruct((NDEV,) + piece_shapes[a], side[a].dtype) for a in range(ns)]
    out_specs += [ANY] * ns
    outs = pl.pallas_call(
        body, name=name, grid=(ni, nk), out_shape=out_shape, in_specs=in_specs, out_specs=out_specs,
        scratch_shapes=[pltpu.VMEM((tm, D), F32)] + (_exchange_sems(ns) if ns else []),
    )(*args)
    return tuple(outs) if has_dr else (None, *outs)


def _tri(reverse):
    r = lax.broadcasted_iota(jnp.int32, (CHUNK, CHUNK), 0)
    c = lax.broadcasted_iota(jnp.int32, (CHUNK, CHUNK), 1)
    return (c >= r) if reverse else (c <= r)


def _cum_f32(tri_b, t):
    hi = t.astype(BF16)
    r1 = t - hi.astype(F32)
    mid = r1.astype(BF16)
    lo = (r1 - mid.astype(F32)).astype(BF16)
    return _dot(tri_b, hi) + _dot(tri_b, mid) + _dot(tri_b, lo)


def _gla_features(zq, zf, lb):
    sq = _sigmoid(zq)
    q = zq * sq * Q_SCALE
    sf = _sigmoid(zf)
    f = lb + (1.0 - lb) * sf
    return q, sq, f, sf


def _gla_block(n):
    return 256 if n % 256 == 0 else CHUNK


def _gla_fwd(z, lb, s0, d, name, side=()):
    n = z.shape[0]
    blk = _gla_block(n)
    nb, npb = n // blk, blk // CHUNK
    reverse = d == 1
    last = 0 if reverse else CHUNK - 1
    order = list(range(npb))[::-1] if reverse else list(range(npb))
    ns = len(side)

    def bmap(i):
        return nb - 1 - i if reverse else i

    hp = GLA_HEADS_PER_STEP
    hw = hp * DH
    units = [(hh, cidx) for hh in range(hp) for cidx in order]

    def body(zq_ref, zf_ref, zv_ref, lb_ref, s0_ref, *rest):
        side_in = rest[:ns]
        o_ref, ss_ref, sf_ref = rest[ns:ns + 3]
        side_out = rest[ns + 3:2 * ns + 3]
        st = rest[2 * ns + 3]
        i = pl.program_id(1)
        if ns:
            side_start, side_finish = _exchange(side_in, side_out, rest[2 * ns + 4:])

            @pl.when((pl.program_id(0) == 0) & (i == 0))
            def _():
                side_start()

        @pl.when(i == 0)
        def _():
            st[...] = s0_ref[...]

        mask = _tri(reverse)
        tri_b = jnp.where(mask, 1.0, 0.0).astype(BF16)
        feat = {}
        for u in units:
            hh, cidx = u
            rows, cols = pl.ds(cidx * CHUNK, CHUNK), pl.ds(hh * DH, DH)
            q, _, f, _ = _gla_features(zq_ref[rows, cols], zf_ref[rows, cols], lb_ref[d:d + 1, cols])
            feat[u] = (q, 1.0 - f, jnp.log(f), zv_ref[rows, cols].astype(BF16))
        dec = {u: _cum_f32(tri_b, feat[u][2]) for u in units}
        ops = {}
        for u in units:
            q, k, _, vb = feat[u]
            g = dec[u]
            gl = g[last:last + 1, :]
            ops[u] = ((q * jnp.exp(g)).astype(BF16), (k * jnp.exp(-g)).astype(BF16),
                      (k * jnp.exp(gl - g)).astype(BF16), jnp.exp(gl), vb)
        att = {u: jnp.where(mask, _dot_nt(ops[u][0], ops[u][1]), 0.0).astype(BF16) for u in units}
        upd = {u: _dot_tn(ops[u][4], ops[u][2]) for u in units}
        intra = {u: _dot(att[u], ops[u][4]) for u in units}
        s_in = {}
        for hh in range(hp):
            s = st[hh]
            for cidx in order:
                s_in[(hh, cidx)] = s
                s = s * ops[(hh, cidx)][3] + upd[(hh, cidx)]
            st[hh] = s
            sf_ref[hh] = s
        for u in units:
            hh, cidx = u
            rows, cols = pl.ds(cidx * CHUNK, CHUNK), pl.ds(hh * DH, DH)
            o_ref[rows, cols] = intra[u] + _dot_nt(ops[u][0], s_in[u].astype(BF16))
            ss_ref[hh, cidx] = s_in[u]

        if ns:
            @pl.when((pl.program_id(0) == NH // hp - 1) & (i == nb - 1))
            def _():
                side_finish()

    def col(g):
        return lambda h, i: (bmap(i), g * (NH // hp) + h)

    return pl.pallas_call(
        body, name=name, grid=(NH // hp, nb),
        out_shape=[jax.ShapeDtypeStruct((n, D), F32), jax.ShapeDtypeStruct((NH, n // CHUNK, DH, DH), F32),
                   jax.ShapeDtypeStruct((NH, DH, DH), F32)]
        + [jax.ShapeDtypeStruct((NDEV,) + t.shape, t.dtype) for t in side],
        in_specs=[pl.BlockSpec((blk, hw), col(0)), pl.BlockSpec((blk, hw), col(1 + d)),
                  pl.BlockSpec((blk, hw), col(3)), pl.BlockSpec((8, hw), lambda h, i: (0, h)),
                  pl.BlockSpec((hp, DH, DH), lambda h, i: (h, 0, 0))] + [ANY] * ns,
        out_specs=[pl.BlockSpec((blk, hw), lambda h, i: (bmap(i), h)),
                   pl.BlockSpec((hp, npb, DH, DH), lambda h, i: (h, bmap(i), 0, 0)),
                   pl.BlockSpec((hp, DH, DH), lambda h, i: (h, 0, 0))] + [ANY] * ns,
        scratch_shapes=[pltpu.VMEM((hp, DH, DH), F32)] + (_exchange_sems(ns) if ns else []),
    )(z, z, z, lb, s0, *side)


def _gla_bwd(z, lb, s_start, do, ds_fin, acc_q, acc_v, d, name, f_dtype=F32, into=None, side=(), side_splits=()):
    n = z.shape[0]
    blk = _gla_block(n)
    nb, npb = n // blk, blk // CHUNK
    reverse = d == 1
    last = 0 if reverse else CHUNK - 1
    order = list(range(npb)) if reverse else list(range(npb))[::-1]
    has_do = do is not None
    has_acc = acc_q is not None
    fused = into is not None
    assert not fused or d == 1
    ns = len(side)
    assert not (fused and ns)
    piece_shapes, piece_of = _pieces(side, side_splits)
    hp = NH if fused else GLA_HEADS_PER_STEP
    hw = hp * DH
    units = [(hh, cidx) for hh in range(hp) for cidx in order]

    def bmap(i):
        return i if reverse else nb - 1 - i

    def body(*refs):
        zq_ref, zf_ref, zv_ref, lb_ref, ss_ref, dsf_ref = refs[:6]
        pos = 6
        do_ref = aq_ref = av_ref = None
        if has_do:
            do_ref = refs[pos]
            pos += 1
        if has_acc:
            aq_ref, av_ref = refs[pos], refs[pos + 1]
            pos += 2
        if fused:
            other_ref = refs[pos + 1]
            dz_ref, dlb_ref, ds0_ref, dst = refs[pos + 2:]
            dz_ref[:, D:2 * D] = other_ref[...]
        else:
            side_in = refs[pos:pos + ns]
            dzq_ref, dzf_ref, dzv_ref, dlb_ref, ds0_ref = refs[pos + ns:pos + ns + 5]
            side_out = refs[pos + ns + 5:pos + 2 * ns + 5]
            dst = refs[pos + 2 * ns + 5]
        i = pl.program_id(1)
        if ns:
            side_start, side_finish = _exchange(side_in, side_out, refs[pos + 2 * ns + 6:], piece_of)

            @pl.when((pl.program_id(0) == 0) & (i == 0))
            def _():
                side_start()

        @pl.when(i == 0)
        def _():
            dst[...] = dsf_ref[...]
            dlb_ref[...] = jnp.zeros_like(dlb_ref)

        mask = _tri(reverse)
        tri_b = jnp.where(mask, 1.0, 0.0).astype(BF16)
        tri_t = jnp.where(_tri(not reverse), 1.0, 0.0).astype(BF16)

        def where(u):
            return pl.ds(u[1] * CHUNK, CHUNK), pl.ds(u[0] * DH, DH)

        feat = {}
        for u in units:
            rows, cols = where(u)
            zq, zf = zq_ref[rows, cols], zf_ref[rows, cols]
            lbv = lb_ref[d:d + 1, cols]
            q, sq, f, sf = _gla_features(zq, zf, lbv)
            feat[u] = dict(zq=zq, q=q, sq=sq, f=f, sf=sf, lbv=lbv, k=1.0 - f, vb=zv_ref[rows, cols].astype(BF16))
        dec = {u: _cum_f32(tri_b, jnp.log(feat[u]["f"])) for u in units}
        for u in units:
            w = feat[u]
            g = dec[u]
            gl = g[last:last + 1, :]
            w["eg"], w["egi"], w["ege"], w["egl"] = jnp.exp(g), jnp.exp(-g), jnp.exp(gl - g), jnp.exp(gl)
            w["qd"], w["ki"], w["ke"] = w["q"] * w["eg"], w["k"] * w["egi"], w["k"] * w["ege"]
            w["qdb"], w["kib"], w["keb"] = w["qd"].astype(BF16), w["ki"].astype(BF16), w["ke"].astype(BF16)
            w["s_in"] = ss_ref[u[0], u[1]]
        if has_do:
            for u in units:
                w = feat[u]
                rows, cols = where(u)
                w["dob"] = do_ref[rows, cols].astype(BF16)
            for u in units:
                w = feat[u]
                w["a"] = jnp.where(mask, _dot_nt(w["qdb"], w["kib"]), 0.0).astype(BF16)
                w["da"] = jnp.where(mask, _dot_nt(w["dob"], w["vb"]), 0.0).astype(BF16)
                w["m"] = _dot_tn(w["dob"], w["qdb"])
        for hh in range(hp):
            ds = dst[hh]
            for cidx in order:
                w = feat[(hh, cidx)]
                w["ds"] = ds
                ds = ds * w["egl"]
                if has_do:
                    ds = ds + w["m"]
            dst[hh] = ds
            ds0_ref[hh] = ds
        for u in units:
            w = feat[u]
            dsb = w["ds"].astype(BF16)
            w["dke"] = _dot(w["vb"], dsb)
            w["dv"] = _dot_nt(w["keb"], dsb)
            if has_do:
                w["dv"] = w["dv"] + _dot_tn(w["a"], w["dob"])
                w["dqd"] = _dot(w["da"], w["kib"]) + _dot(w["dob"], w["s_in"].astype(BF16))
                w["dki"] = _dot_tn(w["da"], w["qdb"])
        for u in units:
            w = feat[u]
            dkeke = w["dke"] * w["ke"]
            w["dgl"] = (w["egl"] * jnp.sum(w["s_in"] * w["ds"], axis=0, keepdims=True)
                        + jnp.sum(dkeke, axis=0, keepdims=True))
            dg = -dkeke
            dk = w["dke"] * w["ege"]
            if has_do:
                dg = dg + w["dqd"] * w["qd"] - w["dki"] * w["ki"]
                dk = dk + w["dki"] * w["egi"]
            w["dg"], w["dk"] = dg, dk
        dlf = {u: _cum_f32(tri_t, feat[u]["dg"]) for u in units}
        for u in units:
            w = feat[u]
            rows, cols = where(u)
            df = (dlf[u] + w["dgl"]) / w["f"] - w["dk"]
            sf = w["sf"]
            dzf = df * (1.0 - w["lbv"]) * sf * (1.0 - sf)
            dlb_ref[0:1, cols] += jnp.sum(df * (1.0 - sf), axis=0, keepdims=True)
            if has_do:
                dzq = w["dqd"] * w["eg"] * (Q_SCALE * _dsilu(w["zq"], w["sq"]))
            else:
                dzq = jnp.zeros((CHUNK, DH), F32)
            dv = w["dv"]
            if has_acc:
                dzq = dzq + aq_ref[rows, cols]
                dv = dv + av_ref[rows, cols]
            if fused:
                lane = u[0] * DH
                dz_ref[rows, pl.ds(lane, DH)] = dzq.astype(BF16)
                dz_ref[rows, pl.ds(2 * D + lane, DH)] = dzf.astype(BF16)
                dz_ref[rows, pl.ds(3 * D + lane, DH)] = dv.astype(BF16)
            else:
                dzq_ref[rows, cols] = dzq
                dzf_ref[rows, cols] = dzf.astype(f_dtype)
                dzv_ref[rows, cols] = dv

        if ns:
            @pl.when((pl.program_id(0) == NH // hp - 1) & (i == nb - 1))
            def _():
                side_finish()

    def col(g):
        return lambda h, i: (bmap(i), g * (NH // hp) + h)

    tok = pl.BlockSpec((blk, hw), lambda h, i: (bmap(i), h))
    state = pl.BlockSpec((hp, DH, DH), lambda h, i: (h, 0, 0))
    in_specs = [pl.BlockSpec((blk, hw), col(0)), pl.BlockSpec((blk, hw), col(1 + d)), pl.BlockSpec((blk, hw), col(3)),
                pl.BlockSpec((8, hw), lambda h, i: (0, h)),
                pl.BlockSpec((hp, npb, DH, DH), lambda h, i: (h, bmap(i), 0, 0)), state]
    args = [z, z, z, lb, s_start, ds_fin]
    if has_do:
        in_specs.append(tok)
        args.append(do)
    if has_acc:
        in_specs += [tok, tok]
        args += [acc_q, acc_v]
    tail_shape = [jax.ShapeDtypeStruct((8, D), F32), jax.ShapeDtypeStruct((NH, DH, DH), F32)]
    tail_specs = [pl.BlockSpec((8, hw), lambda h, i: (0, h)), state]
    if fused:
        buf, other = into
        aliases = {len(args): 0}
        in_specs += [ANY, tok]
        args += [buf, other]
        out_shape = [jax.ShapeDtypeStruct(buf.shape, buf.dtype)] + tail_shape
        out_specs = [pl.BlockSpec((blk, 4 * D), lambda h, i: (bmap(i), 0))] + tail_specs
    else:
        aliases = {}
        in_specs += [ANY] * ns
        args += list(side)
        out_shape = [jax.ShapeDtypeStruct((n, D), F32), jax.ShapeDtypeStruct((n, D), f_dtype),
                     jax.ShapeDtypeStruct((n, D), F32)] + tail_shape
        out_shape += [jax.ShapeDtypeStruct((NDEV,) + piece_shapes[a], side[a].dtype) for a in range(ns)]
        out_specs = [tok, tok, tok] + tail_specs + [ANY] * ns
    return pl.pallas_call(
        body, name=name, grid=(NH // hp, nb), out_shape=out_shape, in_specs=in_specs, out_specs=out_specs,
        input_output_aliases=aliases,
        scratch_shapes=[pltpu.VMEM((hp, DH, DH), F32)] + (_exchange_sems(ns) if ns else []),
    )(*args)


def _shift(t, s, fill, down):
    n = t.shape[0]
    rows = lax.broadcasted_iota(jnp.int32, t.shape, 0)
    if down:
        return jnp.where(rows >= s, pltpu.roll(t, s, 0), fill)
    return jnp.where(rows < n - s, pltpu.roll(t, n - s, 0), fill)


SUBLANES = 8
LRU_SAVED = 4


def _chain_scan(a, b, h_in, down):
    n = a.shape[0]
    ng = n // SUBLANES
    rows = lax.broadcasted_iota(jnp.int32, (SUBLANES, a.shape[1]), 0)
    local = []
    for g in range(ng):
        aa, bb = a[g * SUBLANES:(g + 1) * SUBLANES], b[g * SUBLANES:(g + 1) * SUBLANES]
        for s in (1, 2, 4):
            if down:
                keep, amt = rows >= s, s
            else:
                keep, amt = rows < SUBLANES - s, SUBLANES - s
            bb = bb + aa * jnp.where(keep, pltpu.roll(bb, amt, 0), 0.0)
            aa = aa * jnp.where(keep, pltpu.roll(aa, amt, 0), 1.0)
        local.append((aa, bb))
    out = [None] * ng
    carry = h_in
    for g in (range(ng) if down else range(ng - 1, -1, -1)):
        aa, bb = local[g]
        hg = bb + aa * carry
        out[g] = hg
        carry = hg[SUBLANES - 1:SUBLANES] if down else hg[0:1]
    return (jnp.concatenate(out, axis=0) if ng > 1 else out[0]), carry


def _conv_taps(xv):
    return (_shift(xv, 1, 0.0, True), xv, _shift(xv, 1, 0.0, False), _shift(xv, 2, 0.0, False))


def _conv(taps, cw, cb):
    return cb + cw[0:1, :] * taps[0] + cw[1:2, :] * taps[1] + cw[2:3, :] * taps[2] + cw[3:4, :] * taps[3]


def _neg_expm1(t):
    series = -t * (1.0 + t * (0.5 + t * (1.0 / 6.0 + t * (1.0 / 24.0 + t * (1.0 / 120.0)))))
    return jnp.where(t > -0.1, series, 1.0 - jnp.exp(t))


def _lru_gates(xc, wr, br, wi, bi, lam):
    xcb = xc.astype(BF16)
    r = _sigmoid(_dot(xcb, wr) + br)
    gi = _sigmoid(_dot(xcb, wi) + bi)
    sp = jnp.maximum(-lam, 0.0) + jnp.log(1.0 + jnp.exp(-jnp.abs(lam)))
    la = -RG_C * r * sp
    a = jnp.exp(la)
    mult = jnp.sqrt(_neg_expm1(2.0 * la))
    return xcb, r, gi, sp, a, mult


def _lru_fwd(xin, blk, cw, cb, wr, br, wi, bi, lam, h0, acc_h, d, name):
    n = xin.shape[0]
    nb = n // blk
    reverse = d == 1
    down = not reverse
    has_acc = acc_h is not None

    def bmap(i):
        return nb - 1 - i if reverse else i

    def body(*refs):
        x_ref, cw_ref, cb_ref, wr_ref, br_ref, wi_ref, bi_ref, lam_ref, h0_ref = refs[:9]
        pos = 9
        acc_ref = refs[pos] if has_acc else None
        pos += int(has_acc)
        h_ref, hin_ref, hfin_ref, sav_a_ref, sav_ref = refs[pos:pos + 5]
        pos += 5
        hsum_ref = refs[pos] if has_acc else None
        carry = refs[-1]
        i = pl.program_id(0)

        @pl.when(i == 0)
        def _():
            carry[...] = h0_ref[...]

        for g in range(NH):
            cols = pl.ds(g * DH, DH)
            xc = _conv(_conv_taps(x_ref[:, cols]), cw_ref[:, cols], cb_ref[:, cols])
            _, r, gi, _, a, mult = _lru_gates(xc, wr_ref[g], br_ref[:, cols], wi_ref[g], bi_ref[:, cols],
                                              lam_ref[:, cols])
            sav_a_ref[:, cols] = a
            for slot, val in enumerate((xc, r, gi, mult)):
                sav_ref[slot, :, cols] = val.astype(BF16)
            hin = carry[:, cols]
            h, h_last = _chain_scan(a, mult * gi * xc, hin, down)
            h_ref[:, cols] = h
            if has_acc:
                hsum_ref[:, cols] = h + acc_ref[:, cols]
            hin_ref[0, :, cols] = hin
            carry[:, cols] = h_last
            hfin_ref[:, cols] = h_last

    vec = pl.BlockSpec((1, D), lambda i: (0, 0))
    wsp = pl.BlockSpec((NH, DH, DH), lambda i: (0, 0, 0))
    tok = pl.BlockSpec((blk, D), lambda i: (bmap(i), 0))
    in_specs = [tok, pl.BlockSpec((4, D), lambda i: (0, 0)), vec, wsp, vec, wsp, vec, vec, vec]
    args = [xin, cw, cb, wr, br, wi, bi, lam, h0]
    out_shape = [jax.ShapeDtypeStruct((n, D), F32), jax.ShapeDtypeStruct((nb, 1, D), F32),
                 jax.ShapeDtypeStruct((1, D), F32), jax.ShapeDtypeStruct((n, D), F32),
                 jax.ShapeDtypeStruct((LRU_SAVED, n, D), BF16)]
    out_specs = [tok, pl.BlockSpec((1, 1, D), lambda i: (bmap(i), 0, 0)), vec, tok,
                 pl.BlockSpec((LRU_SAVED, blk, D), lambda i: (0, bmap(i), 0))]
    if has_acc:
        in_specs.append(tok)
        args.append(acc_h)
        out_shape.append(jax.ShapeDtypeStruct((n, D), F32))
        out_specs.append(tok)
    return pl.pallas_call(
        body, name=name, grid=(nb,), out_shape=out_shape, in_specs=in_specs, out_specs=out_specs,
        scratch_shapes=[pltpu.VMEM((1, D), F32)],
    )(*args)


def _lru_bwd(xin, blk, cw, wr, wi, lam, sav, h, hin, dh, cg_fin, acc_dx, init, d, name, dx_dtype=F32):
    n = xin.shape[0]
    nb = n // blk
    reverse = d == 1
    down = not reverse
    first = blk - 1 if reverse else 0
    has_dh = dh is not None
    has_acc = acc_dx is not None
    has_init = init is not None

    def bmap(i):
        return i if reverse else nb - 1 - i

    def body(*refs):
        (x_ref, cw_ref, wr_ref, wi_ref, lam_ref, sav_a_ref, sav_ref, h_ref, hin_ref, cgf_ref) = refs[:10]
        pos = 10
        dh_ref = acc_ref = None
        iwr_ref = iwi_ref = ivec_ref = None
        if has_dh:
            dh_ref = refs[pos]
            pos += 1
        if has_acc:
            acc_ref = refs[pos]
            pos += 1
        if has_init:
            iwr_ref, iwi_ref, ivec_ref = refs[pos:pos + 3]
            pos += 3
        dx_ref, dwr_ref, dwi_ref, vec_ref, cg0_ref, carry = refs[pos:]
        i = pl.program_id(0)

        @pl.when(i == 0)
        def _():
            carry[...] = cgf_ref[...]
            if has_init:
                dwr_ref[...] = iwr_ref[...]
                dwi_ref[...] = iwi_ref[...]
                vec_ref[...] = ivec_ref[...]
            else:
                dwr_ref[...] = jnp.zeros_like(dwr_ref)
                dwi_ref[...] = jnp.zeros_like(dwi_ref)
                vec_ref[...] = jnp.zeros_like(vec_ref)

        for g in range(NH):
            cols = pl.ds(g * DH, DH)
            cwv = cw_ref[:, cols]
            lam_v = lam_ref[:, cols]
            taps = _conv_taps(x_ref[:, cols])
            wr_g, wi_g = wr_ref[g], wi_ref[g]
            a = sav_a_ref[:, cols]
            xcb = sav_ref[0, :, cols]
            xc, r, gi, mult = (sav_ref[slot, :, cols].astype(F32) for slot in range(LRU_SAVED))
            sp = jnp.maximum(-lam_v, 0.0) + jnp.log(1.0 + jnp.exp(-jnp.abs(lam_v)))
            hprev = _shift(h_ref[:, cols], 1, hin_ref[0, :, cols], down)
            a_next = _shift(a, 1, 1.0, not down)
            dhv = dh_ref[:, cols] if has_dh else jnp.zeros_like(a)
            e, _ = _chain_scan(a_next, dhv, carry[:, cols], not down)
            cg = a[first:first + 1, :] * e[first:first + 1, :]
            carry[:, cols] = cg
            cg0_ref[:, cols] = cg
            da = e * hprev
            emult = e * mult
            dgi = emult * xc
            dxc = emult * gi
            dla = da * a - (e * gi * xc) * (a * a) / mult
            dr = dla * (-RG_C * sp)
            sneg = 1.0 - _sigmoid(lam_v)
            dpr = dr * r * (1.0 - r)
            dpi = dgi * gi * (1.0 - gi)
            dprb, dpib = dpr.astype(BF16), dpi.astype(BF16)
            dxc = dxc + _dot_nt(dprb, wr_g) + _dot_nt(dpib, wi_g)
            dwr_ref[g] += _dot_tn(xcb, dprb)
            dwi_ref[g] += _dot_tn(xcb, dpib)
            dx = (cwv[0:1, :] * _shift(dxc, 1, 0.0, False) + cwv[1:2, :] * dxc
                  + cwv[2:3, :] * _shift(dxc, 1, 0.0, True) + cwv[3:4, :] * _shift(dxc, 2, 0.0, True))
            if has_acc:
                dx = dx + acc_ref[:, cols]
            dx_ref[:, cols] = dx.astype(dx_dtype)
            vec_ref[0:1, cols] += jnp.sum(dpr, axis=0, keepdims=True)
            vec_ref[1:2, cols] += jnp.sum(dpi, axis=0, keepdims=True)
            vec_ref[2:3, cols] += jnp.sum(dla * r, axis=0, keepdims=True) * (RG_C * sneg)
            vec_ref[3:4, cols] += jnp.sum(dxc, axis=0, keepdims=True)
            for kk in range(4):
                vec_ref[4 + kk:5 + kk, cols] += jnp.sum(dxc * taps[kk], axis=0, keepdims=True)

    vec = pl.BlockSpec((1, D), lambda i: (0, 0))
    wsp = pl.BlockSpec((NH, DH, DH), lambda i: (0, 0, 0))
    tok = pl.BlockSpec((blk, D), lambda i: (bmap(i), 0))
    vec16 = pl.BlockSpec((16, D), lambda i: (0, 0))
    in_specs = [tok, pl.BlockSpec((4, D), lambda i: (0, 0)), wsp, wsp, vec, tok,
                pl.BlockSpec((LRU_SAVED, blk, D), lambda i: (0, bmap(i), 0)), tok,
                pl.BlockSpec((1, 1, D), lambda i: (bmap(i), 0, 0)), vec]
    args = [xin, cw, wr, wi, lam, sav[0], sav[1], h, hin, cg_fin]
    if has_dh:
        in_specs.append(tok)
        args.append(dh)
    if has_acc:
        in_specs.append(tok)
        args.append(acc_dx)
    if has_init:
        in_specs += [wsp, wsp, vec16]
        args += list(init)
    return pl.pallas_call(
        body, name=name, grid=(nb,),
        out_shape=[jax.ShapeDtypeStruct((n, D), dx_dtype), jax.ShapeDtypeStruct((NH, DH, DH), F32),
                   jax.ShapeDtypeStruct((NH, DH, DH), F32), jax.ShapeDtypeStruct((16, D), F32),
                   jax.ShapeDtypeStruct((1, D), F32)],
        in_specs=in_specs, out_specs=[tok, wsp, wsp, vec16, vec],
        scratch_shapes=[pltpu.VMEM((1, D), F32)],
    )(*args)


def _merge(z, o_f, o_b, hx, xin, tgt, mod, gn, p_a, p_b, w_out, ln_g, ln_b):
    n = xin.shape[0]
    tm = _row_tile(n, 128)

    def body(z4_ref, z6_ref, z7_ref, z8_ref, of_ref, ob_ref, hx_ref, x_ref, t_ref, mod_ref, gn_ref,
             pa_ref, pb_ref, wo_ref, lg_ref, lnb_ref,
             dr_ref, do_ref, dhx_ref, dz_ref, oa_o, obb_o, y_o, dya_o, dyb_o, dout_o, vec_ref):
        @pl.when(pl.program_id(0) == 0)
        def _():
            vec_ref[...] = jnp.zeros_like(vec_ref)

        gt = mod_ref[0:1, 2 * D:3 * D]
        gnv = gn_ref[...]
        o = of_ref[...] + ob_ref[...]
        rs = jnp.concatenate(
            [jnp.broadcast_to(lax.rsqrt(jnp.mean(jnp.square(o[:, h * DH:(h + 1) * DH]), axis=1, keepdims=True)
                                        + RMS_EPS), (tm, DH)) for h in range(NH)], axis=1)
        nrm = o * rs
        rn = nrm * gnv
        z4, z6, z7, z8 = z4_ref[...], z6_ref[...], z7_ref[...], z8_ref[...]
        s4, s6, s7, s8 = _sigmoid(z4), _sigmoid(z6), _sigmoid(z7), _sigmoid(z8)
        sg4, sg6 = z4 * s4, z6 * s6
        hxv = hx_ref[...]
        oa = (rn * sg4).astype(BF16)
        obb = (hxv * sg6).astype(BF16)
        ya = _dot(oa, pa_ref[...])
        yb = _dot(obb, pb_ref[...])
        y = (s7 * ya + s8 * yb).astype(BF16)
        out = _dot(y, wo_ref[...])
        xv = x_ref[...]
        rr = ALPHA * xv + gt * out
        mu = jnp.mean(rr, axis=1, keepdims=True)
        cen = rr - mu
        rstd = lax.rsqrt(jnp.mean(cen * cen, axis=1, keepdims=True) + LN_EPS)
        xhat = cen * rstd
        lg = lg_ref[...]
        err = xhat * lg + lnb_ref[...] - t_ref[...]
        loss_rows = jnp.sum(err * err, axis=1, keepdims=True)
        dxn = err * (1.0 / D)
        dxh = dxn * lg
        dr = rstd * (dxh - jnp.mean(dxh, axis=1, keepdims=True)
                     - xhat * jnp.mean(dxh * xhat, axis=1, keepdims=True))
        dout = (dr * gt).astype(BF16)
        dy = _dot_nt(dout, wo_ref[...])
        dya = (dy * s7).astype(BF16)
        dyb = (dy * s8).astype(BF16)
        doa = _dot_nt(dya, pa_ref[...])
        dob = _dot_nt(dyb, pb_ref[...])
        drn = doa * sg4
        dn = drn * gnv
        dnn = dn * nrm
        corr = jnp.concatenate(
            [jnp.broadcast_to(jnp.mean(dnn[:, h * DH:(h + 1) * DH], axis=1, keepdims=True), (tm, DH))
             for h in range(NH)], axis=1)
        dr_ref[...] = dr
        do_ref[...] = rs * (dn - nrm * corr)
        dhx_ref[...] = dob * sg6
        dz_ref[:, 0:4 * D] = jnp.zeros((tm, 4 * D), BF16)
        dz_ref[:, 4 * D:5 * D] = (doa * rn * _dsilu(z4, s4)).astype(BF16)
        dz_ref[:, 5 * D:6 * D] = jnp.zeros((tm, D), BF16)
        dz_ref[:, 6 * D:7 * D] = (dob * hxv * _dsilu(z6, s6)).astype(BF16)
        dz_ref[:, 7 * D:8 * D] = (dy * ya * s7 * (1.0 - s7)).astype(BF16)
        dz_ref[:, 8 * D:9 * D] = (dy * yb * s8 * (1.0 - s8)).astype(BF16)
        oa_o[...] = oa
        obb_o[...] = obb
        y_o[...] = y
        dya_o[...] = dya
        dyb_o[...] = dyb
        dout_o[...] = dout
        vec_ref[0:1, :] += jnp.sum(dr * out, axis=0, keepdims=True)
        vec_ref[1:2, :] += jnp.sum(dxn * xhat, axis=0, keepdims=True)
        vec_ref[2:3, :] += jnp.sum(dxn, axis=0, keepdims=True)
        vec_ref[3:4, :] += jnp.sum(drn * nrm, axis=0, keepdims=True)
        vec_ref[4:5, :] += jnp.broadcast_to(jnp.sum(loss_rows, axis=0, keepdims=True) * (0.5 / D), (1, D))

    def grp(g):
        return pl.BlockSpec((tm, D), lambda i: (i, g))

    tok = pl.BlockSpec((tm, D), lambda i: (i, 0))
    vec = pl.BlockSpec((1, D), lambda i: (0, 0))
    wsp = pl.BlockSpec((D, D), lambda i: (0, 0))
    return pl.pallas_call(
        body, name="merge", grid=(n // tm,),
        out_shape=[jax.ShapeDtypeStruct((n, D), F32)] * 3
        + [jax.ShapeDtypeStruct((n, NGRP * D), BF16)]
        + [jax.ShapeDtypeStruct((n, D), BF16)] * 6 + [jax.ShapeDtypeStruct((8, D), F32)],
        in_specs=[grp(4), grp(6), grp(7), grp(8), tok, tok, tok, tok, tok,
                  pl.BlockSpec((8, 3 * D), lambda i: (0, 0)), vec, wsp, wsp, wsp, vec, vec],
        out_specs=[tok, tok, tok, pl.BlockSpec((tm, NGRP * D), lambda i: (i, 0))] + [tok] * 6
        + [pl.BlockSpec((8, D), lambda i: (0, 0))],
    )(z, z, z, z, o_f, o_b, hx, xin, tgt, mod, gn, p_a, p_b, w_out, ln_g, ln_b)


def _wmod_grad(c_t, cctx_t, dmx_loc, dmc_loc, name):
    n = dmx_loc.shape[1]

    def body(ct_ref, cc_ref, dmx_ref, dmc_ref, o_ref):
        ct = ct_ref[...]
        sct = ct * _sigmoid(ct)
        cc = cc_ref[...]
        scc = cc * _sigmoid(cc)
        dmc = dmc_ref[0:1, :]
        for b in range(1, NDEV):
            dmc = dmc + dmc_ref[b:b + 1, :]
        acc = scc * dmc
        for b in range(NDEV):
            acc = acc + sct[:, b:b + 1] * dmx_ref[b:b + 1, :]
        o_ref[...] = acc

    return pl.pallas_call(body, name=name, out_shape=jax.ShapeDtypeStruct((D, n), F32))(c_t, cctx_t, dmx_loc, dmc_loc)


PACK_ROWS = 40


def _finalize_small(g_pack, lb, w_mod_full, params):
    npar = len(params)

    def body(*refs):
        gp_ref, lb_ref, wm_ref = refs[:3]
        wmv = refs[3:3 + 3 * npar]
        loss_ref = refs[3 + 3 * npar]
        g_refs = refs[4 + 3 * npar:4 + 4 * npar]
        upd = refs[4 + 4 * npar:4 + 7 * npar]
        tot = refs[-1]
        acc = gp_ref[0]
        for k in range(1, NDEV):
            acc = acc + gp_ref[k]
        tot[...] = acc
        mine = pl.ds(pl.multiple_of(_my_index() * DH, DH), DH)
        (g_cctx, g_bmod, g_bin, g_lbl, g_norm, g_cw, g_cb, g_br, g_bi, g_lam, g_lng, g_lnb) = g_refs

        loss_ref[...] = jnp.broadcast_to(tot[36:37, 0:DH], (8, DH))
        for k in range(3):
            g_bmod[:, k * D:(k + 1) * D] = tot[k:k + 1, :] + tot[3 + k:4 + k, :]
        dmc = jnp.concatenate([tot[3:4, :], tot[4:5, :], tot[5:6, :]], axis=1)
        cv = wmv[0][...]
        proj = _dot_nt(jnp.broadcast_to(dmc, (8, 3 * D)).astype(BF16), wm_ref[...])
        g_cctx[...] = proj[0:1, :] * _dsilu(cv, _sigmoid(cv))
        for k in range(NGRP):
            g_bin[:, k * D:(k + 1) * D] = tot[6 + k:7 + k, :]
        nrm = tot[15:16, 0:DH]
        for h in range(1, NH):
            nrm = nrm + tot[15:16, h * DH:(h + 1) * DH]
        g_norm[...] = nrm
        g_lng[...] = tot[16:17, :]
        g_lnb[...] = tot[17:18, :]
        g_cb[...] = tot[21:22, :] + tot[29:30, :]
        g_cw[0] = tot[22:26, mine] + tot[30:34, mine]
        for ref, row in ((g_br, 18), (g_bi, 19), (g_lam, 20)):
            ref[0, 0:1, :] = tot[row:row + 1, mine]
            ref[0, 1:2, :] = tot[row + 8:row + 9, mine]
        lbl = lb_ref[0:2, mine]
        dl0 = tot[34:36, mine] * lbl * (1.0 - lbl)
        g_lbl[0] = dl0
        g_lbl[1] = -dl0
        for p in range(npar):
            d, mm, vv = _adam_math(g_refs[p][...], wmv[3 * p][...], wmv[3 * p + 1][...], wmv[3 * p + 2][...])
            upd[3 * p][...] = d
            upd[3 * p + 1][...] = mm
            upd[3 * p + 2][...] = vv

    flat = [t for wmv in params for t in wmv]
    shapes = [jax.ShapeDtypeStruct(wmv[0].shape, F32) for wmv in params]
    outs = pl.pallas_call(
        body, name="finalize_small",
        out_shape=[jax.ShapeDtypeStruct((8, DH), F32)] + shapes + [s for s in shapes for _ in range(3)],
        scratch_shapes=[pltpu.VMEM((PACK_ROWS, D), F32)],
    )(g_pack, lb, w_mod_full, *flat)
    grads = list(outs[1:1 + npar])
    upd = [tuple(outs[1 + npar + 3 * p:4 + npar + 3 * p]) for p in range(npar)]
    return outs[0], grads, upd


def _to_colmajor(t, rows):
    return t.reshape(rows, GRID_W, D).transpose(1, 0, 2).reshape(rows * GRID_W, D)


def _to_raster(t, rows):
    return t.reshape(GRID_W, rows, D).transpose(1, 0, 2).reshape(rows * GRID_W, D)


def _local_cols(t, me, width):
    return lax.dynamic_slice_in_dim(t, me * width, width, axis=t.ndim - 1)


def kernel(x, c, ctx, c_ctx, w_mod, b_mod, w_in, b_in, lb_logits, norm_a_g, conv_w, conv_b, w_r, b_r, w_i, b_i, lam, p_a, p_b, w_out, ln_g, ln_b, loss_target, m_c_ctx, m_w_mod, m_b_mod, m_w_in, m_b_in, m_lb_logits, m_norm_a_g, m_conv_w, m_conv_b, m_w_r, m_b_r, m_w_i, m_b_i, m_lam, m_p_a, m_p_b, m_w_out, m_ln_g, m_ln_b, v_c_ctx, v_w_mod, v_b_mod, v_w_in, v_b_in, v_lb_logits, v_norm_a_g, v_conv_w, v_conv_b, v_w_r, v_b_r, v_w_i, v_b_i, v_lam, v_p_a, v_p_b, v_w_out, v_ln_g, v_ln_b):
    me = _my_index()
    xs, cs, tgt = x[0], ctx[0], loss_target[0]
    t_len, c_len = xs.shape[0], cs.shape[0]
    rows = t_len // GRID_W
    wcols = w_in.shape[2]
    mcols = w_mod.shape[2]

    w_mod_b, w_in_b, p_a_b, p_b_b, w_out_b, w_r_b, w_i_b = _cast_bf16(
        [w_mod[0], w_in[0], p_a[0], p_b[0], w_out[0], w_r[0], w_i[0]])
    small = jnp.concatenate([lb_logits.reshape(4, DH), conv_w[0], b_r[0], b_i[0], lam[0], jnp.zeros((2, DH), F32),
                             c.reshape(8, DH)], axis=0)
    g_small, g_wmod = _all_gather_two_level([small, w_mod_b], "gather_params")

    def full_rows(lo, hi):
        return g_small[:, lo:hi, :].transpose(1, 0, 2).reshape(hi - lo, D)

    lbl_f, cw_f, br_f, bi_f, lam_f = full_rows(0, 4), full_rows(4, 8), full_rows(8, 10), full_rows(10, 12), full_rows(12, 14)
    c_all = g_small[:, 16:24, :].reshape(NDEV, D)
    w_mod_f = g_wmod.transpose(1, 0, 2).reshape(D, 3 * D)

    cc = jnp.concatenate([c.reshape(1, D), c_ctx.reshape(1, D), jnp.zeros((6, D), F32)], axis=0)
    lbl_p = jnp.concatenate([lbl_f.reshape(2, 2, D), jnp.zeros((2, 6, D), F32)], axis=1)
    mod, lb = _prep(cc, w_mod_f, b_mod, lbl_p)
    u_x = _modulate(xs, mod, 0, "modulate_x")
    u_c = _modulate(cs, mod, 1, "modulate_c")
    z_x, w_in_f = _inproj_gather(u_x, w_in_b, b_in, "inproj_gather")
    z_c = _mm_bias(u_c, w_in_f, b_in, "inproj_c")

    zero_s = jnp.zeros((NH, DH, DH), F32)
    zero_v = jnp.zeros((1, D), F32)
    gla = {}
    out_w = [p_a_b, p_b_b, w_out_b]
    out_w_f = []
    for d in (0, 1):
        _, ssc, sfc = _gla_fwd(z_c, lb, zero_s, d, f"gla_fwd_c{d}")
        o_d, ssx, _, *gathered = _gla_fwd(z_x, lb, sfc, d, f"gla_fwd_x{d}", side=out_w[1:] if d else out_w[:1])
        out_w_f += [t.reshape(D, D) for t in gathered]
        gla[d] = (ssc, ssx, o_d)
    p_a_f, p_b_f, w_out_f = out_w_f

    x5_c = z_c[:, 5 * D:6 * D]
    x5_x = _to_colmajor(z_x[:, 5 * D:6 * D], rows)
    cb2 = conv_b.reshape(1, D)
    lru = {}
    h_sum = None
    for d in (0, 1):
        prm = (cw_f, cb2, w_r_b[d], br_f[d:d + 1], w_i_b[d], bi_f[d:d + 1], lam_f[d:d + 1])
        h_c, hin_c, hfin_c, *sav_c = _lru_fwd(x5_c, c_len, *prm, zero_v, None, d, f"lru_fwd_c{d}")
        h_x, hin_x, _, sav_a, sav_h, *h_sum = _lru_fwd(x5_x, rows, *prm, hfin_c, lru[0][3] if d else None, d,
                                                       f"lru_fwd_x{d}")
        lru[d] = ((cw_f, w_r_b[d], w_i_b[d], lam_f[d:d + 1]), h_c, hin_c, h_x, hin_x, tuple(sav_c), (sav_a, sav_h))
    hx = _to_raster(h_sum[0], rows)

    gn = jnp.tile(norm_a_g.reshape(1, DH), (1, NH))
    (dr, do, dhx, dz_m, oa, obb, yb16, dya, dyb, dout, mvec) = _merge(
        z_x, gla[0][2], gla[1][2], hx, xs, tgt, mod, gn, p_a_f, p_b_f, w_out_f, ln_g, ln_b)

    dhx_cm = _to_colmajor(dhx, rows)
    lru_dx_x = lru_dx_c = None
    for d in (0, 1):
        prm, h_c, hin_c, h_x, hin_x, sav_c, sav_x = lru[d]
        dx_dtype = BF16 if d else F32
        lru_dx_x, dwr, dwi, lvec, cg0 = _lru_bwd(x5_x, rows, *prm, sav_x, h_x, hin_x, dhx_cm, zero_v, lru_dx_x, None,
                                                 d, f"lru_bwd_x{d}", dx_dtype)
        lru_dx_c, dwr, dwi, lvec, _ = _lru_bwd(x5_c, c_len, *prm, sav_c, h_c, hin_c, None, cg0, lru_dx_c,
                                               (dwr, dwi, lvec), d, f"lru_bwd_c{d}", dx_dtype)
        lru[d] = (dwr, dwi, lvec)
    dz5_x = _to_raster(lru_dx_x, rows)
    dz5_c = lru_dx_c

    dpa = _mm_tn(oa, dya, None, "dpa", out_dtype=BF16)
    dpb = _mm_tn(obb, dyb, None, "dpb", out_dtype=BF16)
    dwo = _mm_tn(yb16, dout, None, "dwout", out_dtype=BF16)
    wr_pack = jnp.concatenate([lru[0][0], lru[1][0], lru[0][1], lru[1][1]], axis=0).reshape(4 * NH * DH, DH)

    gq_c = gv_c = None
    dzf_c, dlb = {}, {}
    gq_x, dzf_x0, gv_x, dlb_x, ds0, r_pa, r_pb, r_wo, r_wri = _gla_bwd(
        z_x, lb, gla[0][1], do, zero_s, None, None, 0, "gla_bwd_x0", f_dtype=BF16,
        side=[dpa, dpb, dwo, wr_pack], side_splits=[0, 0, 0, 0])
    gq_c, dzf_c[0], gv_c, dlb_c, _ = _gla_bwd(z_c, lb, gla[0][0], None, ds0, None, None, 0, "gla_bwd_c0")
    dlb[0] = dlb_x[0:1] + dlb_c[0:1]
    dz_g, dlb_x, ds0 = _gla_bwd(z_x, lb, gla[1][1], do, zero_s, gq_x, gv_x, 1, "gla_bwd_x1", into=(dz_m, dzf_x0))
    gq_c, dzf_c[1], gv_c, dlb_c, _ = _gla_bwd(z_c, lb, gla[1][0], None, ds0, gq_c, gv_c, 1, "gla_bwd_c1")
    dlb[1] = dlb_x[0:1] + dlb_c[0:1]

    bf = lambda t: t.astype(BF16)
    dz_x = lax.dynamic_update_slice(dz_g, dz5_x, (0, 5 * D))
    zc0 = jnp.zeros((c_len, D), BF16)
    dz_c = jnp.concatenate([bf(gq_c), bf(dzf_c[0]), bf(dzf_c[1]), bf(gv_c), zc0, dz5_c, zc0, zc0, zc0], axis=1)
    dwin_c, dbin_c = _mm_tn(u_c, dz_c, None, "dwin_c", with_colsum=True)

    grad_x, xvec = _input_grad(dz_x, w_in_f, xs, dr, mod, 0, "input_grad_x")
    r_win, dbin = _dwin_exchange(u_x, dz_x, dwin_c, dbin_c, "dwin_exchange")
    _, cvec = _input_grad(dz_c, w_in_f, cs, None, mod, 1, "input_grad_c")
    wri_piece = _sum_rows(r_wri, "sum_w_ri_piece")
    g_w_in, d_w_in, nm_w_in, nv_w_in = _sum_adamw(r_win, w_in, m_w_in, v_w_in, "update_w_in")
    g_p_a, d_p_a, nm_p_a, nv_p_a = _sum_adamw(r_pa, p_a, m_p_a, v_p_a, "update_p_a")
    g_p_b, d_p_b, nm_p_b, nv_p_b = _sum_adamw(r_pb, p_b, m_p_b, v_p_b, "update_p_b")
    g_w_out, d_w_out, nm_w_out, nv_w_out = _sum_adamw(r_wo, w_out, m_w_out, v_w_out, "update_w_out")

    dlb_rows = jnp.concatenate([dlb[0], dlb[1]], axis=0)
    pack = jnp.concatenate([
        xvec[0:1], xvec[1:2], mvec[0:1],
        cvec[0:1], cvec[1:2], jnp.zeros((1, D), F32),
        dbin.reshape(NGRP, D),
        mvec[3:4], mvec[1:2], mvec[2:3],
        lru[0][2][0:8], lru[1][2][0:3],
        lru[1][2][3:8],
        dlb_rows,
        mvec[4:5],
        jnp.zeros((3, D), F32)], axis=0)
    g_pack, g_wri = _all_gather_two_level([pack, wri_piece], "gather_small_grads")

    dmx = g_pack[:, 0:3, :].reshape(NDEV, 3 * D)
    dmc = g_pack[:, 3:6, :].reshape(NDEV, 3 * D)
    grad_w_mod = _wmod_grad(c_all.T, c_ctx.reshape(D, 1), _local_cols(dmx, me, mcols), _local_cols(dmc, me, mcols),
                            "grad_w_mod").reshape(1, D, mcols)
    small_params = [(c_ctx.reshape(1, D), m_c_ctx.reshape(1, D), v_c_ctx.reshape(1, D)), (b_mod, m_b_mod, v_b_mod),
                    (b_in, m_b_in, v_b_in), (lb_logits, m_lb_logits, v_lb_logits), (norm_a_g, m_norm_a_g, v_norm_a_g),
                    (conv_w, m_conv_w, v_conv_w), (conv_b, m_conv_b, v_conv_b), (b_r, m_b_r, v_b_r),
                    (b_i, m_b_i, v_b_i), (lam, m_lam, v_lam), (ln_g, m_ln_g, v_ln_g), (ln_b, m_ln_b, v_ln_b)]
    loss_tile, small_g, small_upd = _finalize_small(g_pack, lb, w_mod_f, small_params)
    loss = loss_tile[0, 0]
    (grad_c_ctx, grad_b_mod, grad_b_in, grad_lb_logits, grad_norm_a_g, grad_conv_w, grad_conv_b, grad_b_r, grad_b_i,
     grad_lam, grad_ln_g, grad_ln_b) = small_g
    small_upd[0] = tuple(t.reshape(c_ctx.shape) for t in small_upd[0])
    (o_c_ctx, o_b_mod, o_b_in, o_lb, o_norm, o_conv_w, o_conv_b, o_b_r, o_b_i, o_lam, o_ln_g, o_ln_b) = small_upd

    half = 2 * NH * DH
    g_ri = g_wri.reshape(2 * half, DH)
    grad_w_r, grad_w_i = g_ri[:half].reshape(w_r.shape), g_ri[half:].reshape(w_i.shape)
    d_w_r, nm_w_r, nv_w_r = _adamw(grad_w_r, w_r, m_w_r, v_w_r, "update_w_r")
    d_w_i, nm_w_i, nv_w_i = _adamw(grad_w_i, w_i, m_w_i, v_w_i, "update_w_i")

    d_w_mod, nm_w_mod, nv_w_mod = _adamw(grad_w_mod, w_mod, m_w_mod, v_w_mod, "update_w_mod")

    grads = [grad_c_ctx.reshape(c_ctx.shape), grad_w_mod, grad_b_mod, g_w_in, grad_b_in, grad_lb_logits, grad_norm_a_g,
             grad_conv_w, grad_conv_b, grad_w_r, grad_b_r, grad_w_i, grad_b_i, grad_lam, g_p_a, g_p_b, g_w_out,
             grad_ln_g, grad_ln_b]
    per_kind = []
    for k in range(3):
        per_kind.append([
            o_c_ctx[k], (d_w_mod, nm_w_mod, nv_w_mod)[k], o_b_mod[k], (d_w_in, nm_w_in, nv_w_in)[k], o_b_in[k], o_lb[k],
            o_norm[k], o_conv_w[k], o_conv_b[k], (d_w_r, nm_w_r, nv_w_r)[k], o_b_r[k], (d_w_i, nm_w_i, nv_w_i)[k],
            o_b_i[k], o_lam[k], (d_p_a, nm_p_a, nv_p_a)[k], (d_p_b, nm_p_b, nv_p_b)[k], (d_w_out, nm_w_out, nv_w_out)[k],
            o_ln_g[k], o_ln_b[k]])
    return (loss, grad_x.reshape(x.shape), *grads, *per_kind[0], *per_kind[1], *per_kind[2])
```
